```python
import jax
import jax.numpy as jnp
from jax import lax
import numpy as np

D_MODEL = 1024
BATCH = 16
SEQ = 2048
DEPTH = 2

GRID_W = 64
CTX_LEN = 256
BLOCK = 128
WINDOW = 128
ROPE_THETA = 10000.0
EPS = 1e-6
NEG_INF = -1e30
N_MOD = 6

MLA_HEADS = 8
MLA_NOPE = 64
MLA_ROPE = 32
MLA_QK = MLA_NOPE + MLA_ROPE
MLA_V = 64
MLA_Q_RANK = 256
MLA_KV_RANK = 128

LRU_WIDTH = 512
LRU_BLOCKS = 8
LRU_BLOCK_DIM = LRU_WIDTH // LRU_BLOCKS
LRU_C = 8.0
CONV_W = 4

SWA_HEADS = 8
SWA_KV_HEADS = 2
SWA_GROUP = SWA_HEADS // SWA_KV_HEADS
SWA_HEAD_DIM = 64
SWA_KV_DIM = SWA_KV_HEADS * SWA_HEAD_DIM

GROUP_WIDTH = 512
MIX_WIDTH = 3 * GROUP_WIDTH
D_FF = 4 * D_MODEL
IN_SIZES = (MLA_Q_RANK, MLA_KV_RANK, MLA_ROPE, LRU_WIDTH, LRU_WIDTH, SWA_HEADS * SWA_HEAD_DIM, SWA_KV_DIM, SWA_KV_DIM)
IN_WIDTH = sum(IN_SIZES)

kernel_name = "hybrid_mla_rglru_swa_dit_block"


def rms_norm(x, g):
    xf = x.astype(jnp.float32)
    y = xf * lax.rsqrt(jnp.mean(xf * xf, axis=-1, keepdims=True) + EPS)
    return (y * g.astype(jnp.float32)).astype(x.dtype)


def modulate(x, g, shift, scale):
    return rms_norm(x, g) * (1 + scale) + shift


def axial_rope_tables(rows, dim):
    quarter = dim // 4
    n = rows * GRID_W
    row = jnp.repeat(jnp.arange(rows), GRID_W)
    col = jnp.tile(jnp.arange(GRID_W), rows)
    inv_freq = ROPE_THETA ** (-jnp.arange(quarter, dtype=jnp.float32) / quarter)
    ang = jnp.stack([row, col], axis=-1).astype(jnp.float32)[:, :, None] * inv_freq
    ang = jnp.broadcast_to(ang[:, :, None, :], (n, 2, 2, quarter)).reshape(n, dim)
    return jnp.cos(ang), jnp.sin(ang)


def apply_axial_rope(x, rope):
    cos, sin = rope
    d = x.shape[-1]
    xa = x.reshape(x.shape[:-1] + (2, 2, d // 4))
    rot = jnp.concatenate([-xa[..., 1:, :], xa[..., :1, :]], axis=-2).reshape(x.shape)
    return (x * cos[:, None, :] + rot * sin[:, None, :]).astype(x.dtype)


def split_columns(p):
    parts, start = [], 0
    for size in IN_SIZES:
        parts.append(p[..., start:start + size])
        start += size
    return parts


def mla_query(cq, q_a_g, w_uq, q_g, rope):
    B, L, _ = cq.shape
    q = rms_norm((rms_norm(cq, q_a_g) @ w_uq).reshape(B, L, MLA_HEADS, MLA_QK), q_g)
    if rope is not None:
        q = jnp.concatenate([q[..., :MLA_NOPE], apply_axial_rope(q[..., MLA_NOPE:], rope)], axis=-1)
    return q


def mla_key_value(ckv, k_rope, kv_a_g, w_ukv, k_g, rope):
    B, L, _ = ckv.shape
    kv = (rms_norm(ckv, kv_a_g) @ w_ukv).reshape(B, L, MLA_HEADS, MLA_NOPE + MLA_V)
    k_shared = jnp.broadcast_to(k_rope[:, :, None, :], (B, L, MLA_HEADS, MLA_ROPE))
    k = rms_norm(jnp.concatenate([kv[..., :MLA_NOPE], k_shared], axis=-1), k_g)
    if rope is not None:
        k = jnp.concatenate([k[..., :MLA_NOPE], apply_axial_rope(k[..., MLA_NOPE:], rope)], axis=-1)
    return k, kv[..., MLA_NOPE:]


def dense_block_attention(q, k, v):
    B, L, H, dq = q.shape
    nb = L // BLOCK
    scale = dq ** -0.5
    qb = q.reshape(B, nb, BLOCK, H, dq).transpose(1, 0, 2, 3, 4)

    def one_block(q_blk):
        s = jnp.einsum('bqhd,bkhd->bhqk', q_blk, k).astype(jnp.float32) * scale
        p = jax.nn.softmax(s, axis=-1).astype(v.dtype)
        return jnp.einsum('bhqk,bkhd->bqhd', p, v)

    o = lax.map(one_block, qb)
    return o.transpose(1, 0, 2, 3, 4).reshape(B, L, H * v.shape[-1])


def centred_depthwise_conv(x, w, b):
    left = CONV_W // 2
    y = lax.conv_general_dilated(x, w[:, None, :], window_strides=(1,), padding=[(left, CONV_W - 1 - left)],
                                 dimension_numbers=('NWC', 'WIO', 'NWC'), feature_group_count=x.shape[-1])
    return y + b


def rglru_coefficients(x, gate_w, gate_b, lam):
    B, L, W = x.shape
    xb = x.reshape(B, L, LRU_BLOCKS, LRU_BLOCK_DIM)
    g = jnp.einsum('blnc,zgncm->zgblnm', xb, gate_w).reshape(2, 2, B, L, W)
    g = jax.nn.sigmoid((g + gate_b[:, :, None, None, :]).astype(jnp.float32))
    r, i = g[:, 0], g[:, 1]
    log_a = -LRU_C * r * jax.nn.softplus(-lam.astype(jnp.float32))[:, None, None, :]
    a = jnp.exp(log_a)
    u = jnp.sqrt(-jnp.expm1(2.0 * log_a)) * (i * x.astype(jnp.float32)[None])
    return a, u


def linear_scan(a, u, h0, reverse):
    def combine(e1, e2):
        a1, u1 = e1
        a2, u2 = e2
        return a1 * a2, a2 * u1 + u2

    a_cum, h = lax.associative_scan(combine, (a, u), reverse=reverse, axis=1)
    return a_cum * h0[:, None, :] + h


def swa_query(sq, q_g, rope):
    B, L, _ = sq.shape
    q = rms_norm(sq.reshape(B, L, SWA_HEADS, SWA_HEAD_DIM), q_g)
    if rope is not None:
        q = apply_axial_rope(q, rope)
    return q.reshape(B, L, SWA_KV_HEADS, SWA_GROUP, SWA_HEAD_DIM)


def swa_key_value(sk, sv, k_g, rope):
    B, L, _ = sk.shape
    k = rms_norm(sk.reshape(B, L, SWA_KV_HEADS, SWA_HEAD_DIM), k_g)
    if rope is not None:
        k = apply_axial_rope(k, rope)
    return k, sv.reshape(B, L, SWA_KV_HEADS, SWA_HEAD_DIM)


def sink_column(sink, s):
    return jnp.broadcast_to(sink.astype(jnp.float32)[None, :, :, None, None], s.shape[:-1] + (1,))


def window_attention(q, k, v, k_ctx, v_ctx, sink):
    B, L, KV, G, d = q.shape
    nb = L // BLOCK
    span = BLOCK + 2 * WINDOW
    n_ctx = k_ctx.shape[1]
    scale = d ** -0.5
    pad = ((0, 0), (WINDOW, WINDOW), (0, 0), (0, 0))
    kp, vp = jnp.pad(k, pad), jnp.pad(v, pad)
    qb = q.reshape(B, nb, BLOCK, KV, G, d).transpose(1, 0, 2, 3, 4, 5)
    qi = jnp.arange(BLOCK)[:, None]
    kj = jnp.arange(span)[None, :]
    band = (kj >= qi) & (kj <= qi + 2 * WINDOW)

    def one_block(args):
        b, q_blk = args
        start = b * BLOCK
        k_win = lax.dynamic_slice_in_dim(kp, start, span, axis=1)
        v_win = lax.dynamic_slice_in_dim(vp, start, span, axis=1)
        j = start - WINDOW + kj
        mask = band & (j >= 0) & (j < L)
        s_win = jnp.einsum('bqkgd,bskd->bkgqs', q_blk, k_win).astype(jnp.float32) * scale
        s_win = jnp.where(mask, s_win, NEG_INF)
        s_ctx = jnp.einsum('bqkgd,bckd->bkgqc', q_blk, k_ctx).astype(jnp.float32) * scale
        s = jnp.concatenate([s_win, s_ctx, sink_column(sink, s_win)], axis=-1)
        p = jax.nn.softmax(s, axis=-1).astype(v.dtype)
        return (jnp.einsum('bkgqs,bskd->bqkgd', p[..., :span], v_win)
                + jnp.einsum('bkgqc,bckd->bqkgd', p[..., span:span + n_ctx], v_ctx))

    o = lax.map(one_block, (jnp.arange(nb), qb))
    return o.transpose(1, 0, 2, 3, 4, 5).reshape(B, L, KV * G * d)


def context_attention(q, k, v, sink):
    B, L, KV, G, d = q.shape
    s = jnp.einsum('bqkgd,bckd->bkgqc', q, k).astype(jnp.float32) * d ** -0.5
    p = jax.nn.softmax(jnp.concatenate([s, sink_column(sink, s)], axis=-1), axis=-1)[..., :-1].astype(v.dtype)
    return jnp.einsum('bkgqc,bckd->bqkgd', p, v).reshape(B, L, KV * G * d)


def merge_groups(o_a, o_b, o_c, g, w_out):
    B, L, _ = o_a.shape
    o = jnp.concatenate([o_a, o_b, o_c], axis=-1).reshape(B, L, 3, GROUP_WIDTH)
    return rms_norm(o, g.reshape(3, GROUP_WIDTH)).reshape(B, L, MIX_WIDTH) @ w_out


def sq_relu_mlp(h, w1, w2):
    return jnp.square(jax.nn.relu(h @ w1)) @ w2


def token_mixers(h, hc, w_in, q_a_g, w_uq, kv_a_g, w_ukv, mla_q_g, mla_k_g, conv_w, conv_b,
                 lru_gate_w, lru_gate_b, lru_lambda, swa_q_g, swa_k_g, swa_sink, group_g, w_out,
                 rope_mla, rope_swa, with_ctx_out):
    B = h.shape[0]
    cq, ckv, kr, lx, lg, sq, sk, sv = split_columns(h @ w_in)
    c_cq, c_ckv, c_kr, c_lx, c_lg, c_sq, c_sk, c_sv = split_columns(hc @ w_in)

    k_a, v_a = mla_key_value(ckv, kr, kv_a_g, w_ukv, mla_k_g, rope_mla)
    kc_a, vc_a = mla_key_value(c_ckv, c_kr, kv_a_g, w_ukv, mla_k_g, None)
    o_a = dense_block_attention(mla_query(cq, q_a_g, w_uq, mla_q_g, rope_mla),
                                jnp.concatenate([kc_a, k_a], axis=1), jnp.concatenate([vc_a, v_a], axis=1))

    a, u = rglru_coefficients(centred_depthwise_conv(lx, conv_w, conv_b), lru_gate_w, lru_gate_b, lru_lambda)
    ac, uc = rglru_coefficients(centred_depthwise_conv(c_lx, conv_w, conv_b), lru_gate_w, lru_gate_b, lru_lambda)
    zero = jnp.zeros((B, LRU_WIDTH), jnp.float32)
    hc_f = linear_scan(ac[0], uc[0], zero, False)
    hc_b = linear_scan(ac[1], uc[1], zero, True)
    h_f = linear_scan(a[0], u[0], hc_f[:, -1], False)
    h_b = linear_scan(a[1], u[1], hc_b[:, 0], True)
    o_b = (h_f + h_b).astype(h.dtype) * jax.nn.gelu(lg)

    sink = swa_sink.reshape(SWA_KV_HEADS, SWA_GROUP)
    k_c, v_c = swa_key_value(sk, sv, swa_k_g, rope_swa)
    kc_c, vc_c = swa_key_value(c_sk, c_sv, swa_k_g, None)
    o_c = window_attention(swa_query(sq, swa_q_g, rope_swa), k_c, v_c, kc_c, vc_c, sink)

    y = merge_groups(o_a, o_b, o_c, group_g, w_out)
    if not with_ctx_out:
        return y, None
    oc_a = dense_block_attention(mla_query(c_cq, q_a_g, w_uq, mla_q_g, None), kc_a, vc_a)
    oc_b = (hc_f + hc_b).astype(hc.dtype) * jax.nn.gelu(c_lg)
    oc_c = context_attention(swa_query(c_sq, swa_q_g, None), kc_c, vc_c, sink)
    return y, merge_groups(oc_a, oc_b, oc_c, group_g, w_out)


def _fwd_setup_inputs(seed: int = 0) -> dict:
    key = jax.random.key(seed)
    ks = jax.random.split(key, 32)

    def nrm(k, shape, fan_in, gain=1.0):
        return gain * fan_in ** -0.5 * jax.random.normal(k, shape, jnp.float32)

    def gains(k, shape):
        return 1.0 + 0.05 * jax.random.normal(k, shape, jnp.float32)

    u = jax.random.uniform(ks[18], (DEPTH, 2, LRU_WIDTH), jnp.float32, minval=0.9, maxval=0.999)
    a_base = u ** (1.0 / LRU_C)
    return {
        'x': jax.random.normal(ks[0], (BATCH, SEQ, D_MODEL), jnp.float32),
        'c': jax.random.normal(ks[1], (BATCH, D_MODEL), jnp.float32),
        'ctx': jax.random.normal(ks[2], (BATCH, CTX_LEN, D_MODEL), jnp.float32),
        'c_ctx': jax.random.normal(ks[3], (D_MODEL,), jnp.float32),
        'w_mod': nrm(ks[4], (DEPTH, D_MODEL, N_MOD * D_MODEL), D_MODEL, 0.5),
        'b_mod': 0.02 * jax.random.normal(ks[5], (DEPTH, N_MOD * D_MODEL), jnp.float32),
        'norm1_g': gains(ks[6], (DEPTH, D_MODEL)),
        'w_in': nrm(ks[7], (DEPTH, D_MODEL, IN_WIDTH), D_MODEL),
        'q_a_g': gains(ks[8], (DEPTH, MLA_Q_RANK)),
        'w_uq': nrm(ks[9], (DEPTH, MLA_Q_RANK, MLA_HEADS * MLA_QK), MLA_Q_RANK),
        'kv_a_g': gains(ks[10], (DEPTH, MLA_KV_RANK)),
        'w_ukv': nrm(ks[11], (DEPTH, MLA_KV_RANK, MLA_HEADS * (MLA_NOPE + MLA_V)), MLA_KV_RANK),
        'mla_q_g': gains(ks[12], (DEPTH, MLA_QK)),
        'mla_k_g': gains(ks[13], (DEPTH, MLA_QK)),
        'conv_w': nrm(ks[14], (DEPTH, CONV_W, LRU_WIDTH), CONV_W),
        'conv_b': 0.02 * jax.random.normal(ks[15], (DEPTH, LRU_WIDTH), jnp.float32),
        'lru_gate_w': nrm(ks[16], (DEPTH, 2, 2, LRU_BLOCKS, LRU_BLOCK_DIM, LRU_BLOCK_DIM), LRU_BLOCK_DIM),
        'lru_gate_b': 0.02 * jax.random.normal(ks[17], (DEPTH, 2, 2, LRU_WIDTH), jnp.float32),
        'lru_lambda': jnp.log(a_base) - jnp.log1p(-a_base),
        'swa_q_g': gains(ks[19], (DEPTH, SWA_HEAD_DIM)),
        'swa_k_g': gains(ks[20], (DEPTH, SWA_HEAD_DIM)),
        'swa_sink': jax.random.normal(ks[21], (DEPTH, SWA_HEADS), jnp.float32),
        'group_g': gains(ks[22], (DEPTH, MIX_WIDTH)),
        'w_out': nrm(ks[23], (DEPTH, MIX_WIDTH, D_MODEL), MIX_WIDTH),
        'norm2_g': gains(ks[24], (DEPTH, D_MODEL)),
        'w_ff1': nrm(ks[25], (DEPTH, D_MODEL, D_FF), D_MODEL),
        'w_ff2': nrm(ks[26], (DEPTH, D_FF, D_MODEL), D_FF),
    }


def _fwd_reference(x, c, ctx, c_ctx, w_mod, b_mod, norm1_g, w_in, q_a_g, w_uq, kv_a_g, w_ukv, mla_q_g, mla_k_g,
              conv_w, conv_b, lru_gate_w, lru_gate_b, lru_lambda, swa_q_g, swa_k_g, swa_sink, group_g, w_out,
              norm2_g, w_ff1, w_ff2):
    rows = x.shape[1] // GRID_W
    rope_mla = axial_rope_tables(rows, MLA_ROPE)
    rope_swa = axial_rope_tables(rows, SWA_HEAD_DIM)
    act_c = jax.nn.silu(c)
    act_cc = jax.nn.silu(c_ctx)
    xc = ctx
    for layer in range(DEPTH):
        last = layer == DEPTH - 1
        sh1, sc1, g1, sh2, sc2, g2 = jnp.split((act_c @ w_mod[layer] + b_mod[layer])[:, None, :], N_MOD, axis=-1)
        csh1, csc1, cg1, csh2, csc2, cg2 = jnp.split(act_cc @ w_mod[layer] + b_mod[layer], N_MOD, axis=-1)
        y, yc = token_mixers(modulate(x, norm1_g[layer], sh1, sc1), modulate(xc, norm1_g[layer], csh1, csc1),
                             w_in[layer], q_a_g[layer], w_uq[layer], kv_a_g[layer], w_ukv[layer],
                             mla_q_g[layer], mla_k_g[layer], conv_w[layer], conv_b[layer],
                             lru_gate_w[layer], lru_gate_b[layer], lru_lambda[layer],
                             swa_q_g[layer], swa_k_g[layer], swa_sink[layer], group_g[layer], w_out[layer],
                             rope_mla, rope_swa, not last)
        x = x + g1 * y
        x = x + g2 * sq_relu_mlp(modulate(x, norm2_g[layer], sh2, sc2), w_ff1[layer], w_ff2[layer])
        if not last:
            xc = xc + cg1 * yc
            xc = xc + cg2 * sq_relu_mlp(modulate(xc, norm2_g[layer], csh2, csc2), w_ff1[layer], w_ff2[layer])
    return x


import jax as _jax
import jax.numpy as _jnp

TWIN_FORMAT = 'train_step'
FWD_PARAMS = ['x', 'c', 'ctx', 'c_ctx', 'w_mod', 'b_mod', 'norm1_g', 'w_in', 'q_a_g', 'w_uq', 'kv_a_g', 'w_ukv', 'mla_q_g', 'mla_k_g', 'conv_w', 'conv_b', 'lru_gate_w', 'lru_gate_b', 'lru_lambda', 'swa_q_g', 'swa_k_g', 'swa_sink', 'group_g', 'w_out', 'norm2_g', 'w_ff1', 'w_ff2']
TWIN_WEIGHTS = ['c_ctx', 'w_mod', 'b_mod', 'norm1_g', 'w_in', 'q_a_g', 'w_uq', 'kv_a_g', 'w_ukv', 'mla_q_g', 'mla_k_g', 'conv_w', 'conv_b', 'lru_gate_w', 'lru_gate_b', 'lru_lambda', 'swa_q_g', 'swa_k_g', 'swa_sink', 'group_g', 'w_out', 'norm2_g', 'w_ff1', 'w_ff2']
TWIN_DIFF_INPUT = 'x'
TWIN_INPUTS = ['x', 'c', 'ctx', 'c_ctx', 'w_mod', 'b_mod', 'norm1_g', 'w_in', 'q_a_g', 'w_uq', 'kv_a_g', 'w_ukv', 'mla_q_g', 'mla_k_g', 'conv_w', 'conv_b', 'lru_gate_w', 'lru_gate_b', 'lru_lambda', 'swa_q_g', 'swa_k_g', 'swa_sink', 'group_g', 'w_out', 'norm2_g', 'w_ff1', 'w_ff2', 'loss_target', 'm_c_ctx', 'm_w_mod', 'm_b_mod', 'm_norm1_g', 'm_w_in', 'm_q_a_g', 'm_w_uq', 'm_kv_a_g', 'm_w_ukv', 'm_mla_q_g', 'm_mla_k_g', 'm_conv_w', 'm_conv_b', 'm_lru_gate_w', 'm_lru_gate_b', 'm_lru_lambda', 'm_swa_q_g', 'm_swa_k_g', 'm_swa_sink', 'm_group_g', 'm_w_out', 'm_norm2_g', 'm_w_ff1', 'm_w_ff2', 'v_c_ctx', 'v_w_mod', 'v_b_mod', 'v_norm1_g', 'v_w_in', 'v_q_a_g', 'v_w_uq', 'v_kv_a_g', 'v_w_ukv', 'v_mla_q_g', 'v_mla_k_g', 'v_conv_w', 'v_conv_b', 'v_lru_gate_w', 'v_lru_gate_b', 'v_lru_lambda', 'v_swa_q_g', 'v_swa_k_g', 'v_swa_sink', 'v_group_g', 'v_w_out', 'v_norm2_g', 'v_w_ff1', 'v_w_ff2']
TWIN_OUTPUTS = ['loss', 'grad_x', 'grad_c_ctx', 'grad_w_mod', 'grad_b_mod', 'grad_norm1_g', 'grad_w_in', 'grad_q_a_g', 'grad_w_uq', 'grad_kv_a_g', 'grad_w_ukv', 'grad_mla_q_g', 'grad_mla_k_g', 'grad_conv_w', 'grad_conv_b', 'grad_lru_gate_w', 'grad_lru_gate_b', 'grad_lru_lambda', 'grad_swa_q_g', 'grad_swa_k_g', 'grad_swa_sink', 'grad_group_g', 'grad_w_out', 'grad_norm2_g', 'grad_w_ff1', 'grad_w_ff2', 'delta_c_ctx', 'delta_w_mod', 'delta_b_mod', 'delta_norm1_g', 'delta_w_in', 'delta_q_a_g', 'delta_w_uq', 'delta_kv_a_g', 'delta_w_ukv', 'delta_mla_q_g', 'delta_mla_k_g', 'delta_conv_w', 'delta_conv_b', 'delta_lru_gate_w', 'delta_lru_gate_b', 'delta_lru_lambda', 'delta_swa_q_g', 'delta_swa_k_g', 'delta_swa_sink', 'delta_group_g', 'delta_w_out', 'delta_norm2_g', 'delta_w_ff1', 'delta_w_ff2', 'new_m_c_ctx', 'new_m_w_mod', 'new_m_b_mod', 'new_m_norm1_g', 'new_m_w_in', 'new_m_q_a_g', 'new_m_w_uq', 'new_m_kv_a_g', 'new_m_w_ukv', 'new_m_mla_q_g', 'new_m_mla_k_g', 'new_m_conv_w', 'new_m_conv_b', 'new_m_lru_gate_w', 'new_m_lru_gate_b', 'new_m_lru_lambda', 'new_m_swa_q_g', 'new_m_swa_k_g', 'new_m_swa_sink', 'new_m_group_g', 'new_m_w_out', 'new_m_norm2_g', 'new_m_w_ff1', 'new_m_w_ff2', 'new_v_c_ctx', 'new_v_w_mod', 'new_v_b_mod', 'new_v_norm1_g', 'new_v_w_in', 'new_v_q_a_g', 'new_v_w_uq', 'new_v_kv_a_g', 'new_v_w_ukv', 'new_v_mla_q_g', 'new_v_mla_k_g', 'new_v_conv_w', 'new_v_conv_b', 'new_v_lru_gate_w', 'new_v_lru_gate_b', 'new_v_lru_lambda', 'new_v_swa_q_g', 'new_v_swa_k_g', 'new_v_swa_sink', 'new_v_group_g', 'new_v_w_out', 'new_v_norm2_g', 'new_v_w_ff1', 'new_v_w_ff2']
TWIN_LEAF_KINDS = {'loss': 'loss', 'grad_x': 'grad_x', 'grad_c_ctx': 'grad_w', 'grad_w_mod': 'grad_w', 'grad_b_mod': 'grad_w', 'grad_norm1_g': 'grad_w', 'grad_w_in': 'grad_w', 'grad_q_a_g': 'grad_w', 'grad_w_uq': 'grad_w', 'grad_kv_a_g': 'grad_w', 'grad_w_ukv': 'grad_w', 'grad_mla_q_g': 'grad_w', 'grad_mla_k_g': 'grad_w', 'grad_conv_w': 'grad_w', 'grad_conv_b': 'grad_w', 'grad_lru_gate_w': 'grad_w', 'grad_lru_gate_b': 'grad_w', 'grad_lru_lambda': 'grad_w', 'grad_swa_q_g': 'grad_w', 'grad_swa_k_g': 'grad_w', 'grad_swa_sink': 'grad_w', 'grad_group_g': 'grad_w', 'grad_w_out': 'grad_w', 'grad_norm2_g': 'grad_w', 'grad_w_ff1': 'grad_w', 'grad_w_ff2': 'grad_w', 'delta_c_ctx': 'delta_w', 'delta_w_mod': 'delta_w', 'delta_b_mod': 'delta_w', 'delta_norm1_g': 'delta_w', 'delta_w_in': 'delta_w', 'delta_q_a_g': 'delta_w', 'delta_w_uq': 'delta_w', 'delta_kv_a_g': 'delta_w', 'delta_w_ukv': 'delta_w', 'delta_mla_q_g': 'delta_w', 'delta_mla_k_g': 'delta_w', 'delta_conv_w': 'delta_w', 'delta_conv_b': 'delta_w', 'delta_lru_gate_w': 'delta_w', 'delta_lru_gate_b': 'delta_w', 'delta_lru_lambda': 'delta_w', 'delta_swa_q_g': 'delta_w', 'delta_swa_k_g': 'delta_w', 'delta_swa_sink': 'delta_w', 'delta_group_g': 'delta_w', 'delta_w_out': 'delta_w', 'delta_norm2_g': 'delta_w', 'delta_w_ff1': 'delta_w', 'delta_w_ff2': 'delta_w', 'new_m_c_ctx': 'new_m', 'new_m_w_mod': 'new_m', 'new_m_b_mod': 'new_m', 'new_m_norm1_g': 'new_m', 'new_m_w_in': 'new_m', 'new_m_q_a_g': 'new_m', 'new_m_w_uq': 'new_m', 'new_m_kv_a_g': 'new_m', 'new_m_w_ukv': 'new_m', 'new_m_mla_q_g': 'new_m', 'new_m_mla_k_g': 'new_m', 'new_m_conv_w': 'new_m', 'new_m_conv_b': 'new_m', 'new_m_lru_gate_w': 'new_m', 'new_m_lru_gate_b': 'new_m', 'new_m_lru_lambda': 'new_m', 'new_m_swa_q_g': 'new_m', 'new_m_swa_k_g': 'new_m', 'new_m_swa_sink': 'new_m', 'new_m_group_g': 'new_m', 'new_m_w_out': 'new_m', 'new_m_norm2_g': 'new_m', 'new_m_w_ff1': 'new_m', 'new_m_w_ff2': 'new_m', 'new_v_c_ctx': 'new_v', 'new_v_w_mod': 'new_v', 'new_v_b_mod': 'new_v', 'new_v_norm1_g': 'new_v', 'new_v_w_in': 'new_v', 'new_v_q_a_g': 'new_v', 'new_v_w_uq': 'new_v', 'new_v_kv_a_g': 'new_v', 'new_v_w_ukv': 'new_v', 'new_v_mla_q_g': 'new_v', 'new_v_mla_k_g': 'new_v', 'new_v_conv_w': 'new_v', 'new_v_conv_b': 'new_v', 'new_v_lru_gate_w': 'new_v', 'new_v_lru_gate_b': 'new_v', 'new_v_lru_lambda': 'new_v', 'new_v_swa_q_g': 'new_v', 'new_v_swa_k_g': 'new_v', 'new_v_swa_sink': 'new_v', 'new_v_group_g': 'new_v', 'new_v_w_out': 'new_v', 'new_v_norm2_g': 'new_v', 'new_v_w_ff1': 'new_v', 'new_v_w_ff2': 'new_v'}


def _forward(args):
    return _fwd_reference(*[args[k] for k in FWD_PARAMS])


def _output_shape():
    out = _jax.eval_shape(lambda: _forward(_fwd_setup_inputs(0)))
    return out.shape, out.dtype

N_MICROBATCH = 1
ADAM_LR = 0.001
ADAM_B1 = 0.9
ADAM_B2 = 0.999
ADAM_EPS = 1e-08
ADAM_WD = 0.01
ADAM_STEP = 10
PER_EXAMPLE_BATCH_AXIS = {'x': 0, 'c': 0, 'ctx': 0, 'loss_target': 0}
SHARED_INPUTS = []
_WEIGHT_DTYPES = {'c_ctx': _jnp.float32, 'w_mod': _jnp.float32, 'b_mod': _jnp.float32, 'norm1_g': _jnp.float32, 'w_in': _jnp.float32, 'q_a_g': _jnp.float32, 'w_uq': _jnp.float32, 'kv_a_g': _jnp.float32, 'w_ukv': _jnp.float32, 'mla_q_g': _jnp.float32, 'mla_k_g': _jnp.float32, 'conv_w': _jnp.float32, 'conv_b': _jnp.float32, 'lru_gate_w': _jnp.float32, 'lru_gate_b': _jnp.float32, 'lru_lambda': _jnp.float32, 'swa_q_g': _jnp.float32, 'swa_k_g': _jnp.float32, 'swa_sink': _jnp.float32, 'group_g': _jnp.float32, 'w_out': _jnp.float32, 'norm2_g': _jnp.float32, 'w_ff1': _jnp.float32, 'w_ff2': _jnp.float32}
MOMENT_SCALE = {'c_ctx': 1.793196e+00, 'w_mod': 5.308509e+00, 'b_mod': 1.044629e+01, 'norm1_g': 1.428890e+00, 'w_in': 2.191099e+00, 'q_a_g': 2.298486e-01, 'w_uq': 1.255652e-01, 'kv_a_g': 5.672442e+00, 'w_ukv': 1.907714e+00, 'mla_q_g': 3.521433e-01, 'mla_k_g': 3.609249e-01, 'conv_w': 1.282333e+00, 'conv_b': 4.288810e+00, 'lru_gate_w': 1.460511e-01, 'lru_gate_b': 1.976744e-01, 'lru_lambda': 2.733837e-01, 'swa_q_g': 4.418062e-01, 'swa_k_g': 4.334338e-01, 'swa_sink': 5.980428e-03, 'group_g': 4.017375e+00, 'w_out': 2.978846e+00, 'norm2_g': 1.321451e+01, 'w_ff1': 8.265966e-01, 'w_ff2': 2.645819e+00}


def _to_microbatches(a, axis):
    t = _jnp.moveaxis(a, axis, 0)
    t = t.reshape((N_MICROBATCH, t.shape[0] // N_MICROBATCH) + t.shape[1:])
    return _jnp.moveaxis(t, 1, axis + 1)


def setup_inputs(seed: int = 0) -> dict:
    inp = _fwd_setup_inputs(seed)
    key = _jax.random.fold_in(_jax.random.key(seed), 7919)
    shape, _ = _output_shape()
    out = dict(inp)
    out["loss_target"] = _jax.random.normal(_jax.random.fold_in(key, 0), shape, _jnp.float32)
    for i, name in enumerate(TWIN_WEIGHTS):
        w = inp[name].astype(_jnp.float32)
        if MOMENT_SCALE is None:
            s = _jnp.sqrt(_jnp.mean(_jnp.square(w)) + 1e-30)
        else:
            s = MOMENT_SCALE[name]
        km, kv = _jax.random.split(_jax.random.fold_in(key, i + 1))
        out[name] = w
        out["m_" + name] = s * _jax.random.normal(km, w.shape, _jnp.float32)
        out["v_" + name] = (s * s) * _jax.random.uniform(kv, w.shape, _jnp.float32, 0.5, 1.5)
    if N_MICROBATCH > 1:
        for name, axis in PER_EXAMPLE_BATCH_AXIS.items():
            out[name] = _to_microbatches(out[name], axis)
    return {'x': out['x'], 'c': out['c'], 'ctx': out['ctx'], 'c_ctx': out['c_ctx'], 'w_mod': out['w_mod'], 'b_mod': out['b_mod'], 'norm1_g': out['norm1_g'], 'w_in': out['w_in'], 'q_a_g': out['q_a_g'], 'w_uq': out['w_uq'], 'kv_a_g': out['kv_a_g'], 'w_ukv': out['w_ukv'], 'mla_q_g': out['mla_q_g'], 'mla_k_g': out['mla_k_g'], 'conv_w': out['conv_w'], 'conv_b': out['conv_b'], 'lru_gate_w': out['lru_gate_w'], 'lru_gate_b': out['lru_gate_b'], 'lru_lambda': out['lru_lambda'], 'swa_q_g': out['swa_q_g'], 'swa_k_g': out['swa_k_g'], 'swa_sink': out['swa_sink'], 'group_g': out['group_g'], 'w_out': out['w_out'], 'norm2_g': out['norm2_g'], 'w_ff1': out['w_ff1'], 'w_ff2': out['w_ff2'], 'loss_target': out['loss_target'], 'm_c_ctx': out['m_c_ctx'], 'm_w_mod': out['m_w_mod'], 'm_b_mod': out['m_b_mod'], 'm_norm1_g': out['m_norm1_g'], 'm_w_in': out['m_w_in'], 'm_q_a_g': out['m_q_a_g'], 'm_w_uq': out['m_w_uq'], 'm_kv_a_g': out['m_kv_a_g'], 'm_w_ukv': out['m_w_ukv'], 'm_mla_q_g': out['m_mla_q_g'], 'm_mla_k_g': out['m_mla_k_g'], 'm_conv_w': out['m_conv_w'], 'm_conv_b': out['m_conv_b'], 'm_lru_gate_w': out['m_lru_gate_w'], 'm_lru_gate_b': out['m_lru_gate_b'], 'm_lru_lambda': out['m_lru_lambda'], 'm_swa_q_g': out['m_swa_q_g'], 'm_swa_k_g': out['m_swa_k_g'], 'm_swa_sink': out['m_swa_sink'], 'm_group_g': out['m_group_g'], 'm_w_out': out['m_w_out'], 'm_norm2_g': out['m_norm2_g'], 'm_w_ff1': out['m_w_ff1'], 'm_w_ff2': out['m_w_ff2'], 'v_c_ctx': out['v_c_ctx'], 'v_w_mod': out['v_w_mod'], 'v_b_mod': out['v_b_mod'], 'v_norm1_g': out['v_norm1_g'], 'v_w_in': out['v_w_in'], 'v_q_a_g': out['v_q_a_g'], 'v_w_uq': out['v_w_uq'], 'v_kv_a_g': out['v_kv_a_g'], 'v_w_ukv': out['v_w_ukv'], 'v_mla_q_g': out['v_mla_q_g'], 'v_mla_k_g': out['v_mla_k_g'], 'v_conv_w': out['v_conv_w'], 'v_conv_b': out['v_conv_b'], 'v_lru_gate_w': out['v_lru_gate_w'], 'v_lru_gate_b': out['v_lru_gate_b'], 'v_lru_lambda': out['v_lru_lambda'], 'v_swa_q_g': out['v_swa_q_g'], 'v_swa_k_g': out['v_swa_k_g'], 'v_swa_sink': out['v_swa_sink'], 'v_group_g': out['v_group_g'], 'v_w_out': out['v_w_out'], 'v_norm2_g': out['v_norm2_g'], 'v_w_ff1': out['v_w_ff1'], 'v_w_ff2': out['v_w_ff2']}


def _loss(weights, diff, rest, loss_target):
    with _jax.named_scope("forward"):
        args = {**rest, TWIN_DIFF_INPUT: diff, **{k: w.astype(_WEIGHT_DTYPES[k]) for k, w in weights.items()}}
        y = _forward(args)
    with _jax.named_scope("loss_head"):
        err = _jnp.square(y.astype(_jnp.float32) - loss_target)
        return 0.5 * _jnp.sum(_jnp.mean(err, axis=-1)) if err.ndim else 0.5 * err


def _adamw(w, g, m, v):
    m = ADAM_B1 * m + (1.0 - ADAM_B1) * g
    v = ADAM_B2 * v + (1.0 - ADAM_B2) * _jnp.square(g)
    m_hat = m / (1.0 - ADAM_B1 ** ADAM_STEP)
    v_hat = v / (1.0 - ADAM_B2 ** ADAM_STEP)
    delta = -ADAM_LR * (m_hat / (_jnp.sqrt(v_hat) + ADAM_EPS) + ADAM_WD * w)
    return delta, m, v


def reference(x, c, ctx, c_ctx, w_mod, b_mod, norm1_g, w_in, q_a_g, w_uq, kv_a_g, w_ukv, mla_q_g, mla_k_g, conv_w, conv_b, lru_gate_w, lru_gate_b, lru_lambda, swa_q_g, swa_k_g, swa_sink, group_g, w_out, norm2_g, w_ff1, w_ff2, loss_target, m_c_ctx, m_w_mod, m_b_mod, m_norm1_g, m_w_in, m_q_a_g, m_w_uq, m_kv_a_g, m_w_ukv, m_mla_q_g, m_mla_k_g, m_conv_w, m_conv_b, m_lru_gate_w, m_lru_gate_b, m_lru_lambda, m_swa_q_g, m_swa_k_g, m_swa_sink, m_group_g, m_w_out, m_norm2_g, m_w_ff1, m_w_ff2, v_c_ctx, v_w_mod, v_b_mod, v_norm1_g, v_w_in, v_q_a_g, v_w_uq, v_kv_a_g, v_w_ukv, v_mla_q_g, v_mla_k_g, v_conv_w, v_conv_b, v_lru_gate_w, v_lru_gate_b, v_lru_lambda, v_swa_q_g, v_swa_k_g, v_swa_sink, v_group_g, v_w_out, v_norm2_g, v_w_ff1, v_w_ff2):
    given = dict(x=x, c=c, ctx=ctx, c_ctx=c_ctx, w_mod=w_mod, b_mod=b_mod, norm1_g=norm1_g, w_in=w_in, q_a_g=q_a_g, w_uq=w_uq, kv_a_g=kv_a_g, w_ukv=w_ukv, mla_q_g=mla_q_g, mla_k_g=mla_k_g, conv_w=conv_w, conv_b=conv_b, lru_gate_w=lru_gate_w, lru_gate_b=lru_gate_b, lru_lambda=lru_lambda, swa_q_g=swa_q_g, swa_k_g=swa_k_g, swa_sink=swa_sink, group_g=group_g, w_out=w_out, norm2_g=norm2_g, w_ff1=w_ff1, w_ff2=w_ff2, loss_target=loss_target, m_c_ctx=m_c_ctx, m_w_mod=m_w_mod, m_b_mod=m_b_mod, m_norm1_g=m_norm1_g, m_w_in=m_w_in, m_q_a_g=m_q_a_g, m_w_uq=m_w_uq, m_kv_a_g=m_kv_a_g, m_w_ukv=m_w_ukv, m_mla_q_g=m_mla_q_g, m_mla_k_g=m_mla_k_g, m_conv_w=m_conv_w, m_conv_b=m_conv_b, m_lru_gate_w=m_lru_gate_w, m_lru_gate_b=m_lru_gate_b, m_lru_lambda=m_lru_lambda, m_swa_q_g=m_swa_q_g, m_swa_k_g=m_swa_k_g, m_swa_sink=m_swa_sink, m_group_g=m_group_g, m_w_out=m_w_out, m_norm2_g=m_norm2_g, m_w_ff1=m_w_ff1, m_w_ff2=m_w_ff2, v_c_ctx=v_c_ctx, v_w_mod=v_w_mod, v_b_mod=v_b_mod, v_norm1_g=v_norm1_g, v_w_in=v_w_in, v_q_a_g=v_q_a_g, v_w_uq=v_w_uq, v_kv_a_g=v_kv_a_g, v_w_ukv=v_w_ukv, v_mla_q_g=v_mla_q_g, v_mla_k_g=v_mla_k_g, v_conv_w=v_conv_w, v_conv_b=v_conv_b, v_lru_gate_w=v_lru_gate_w, v_lru_gate_b=v_lru_gate_b, v_lru_lambda=v_lru_lambda, v_swa_q_g=v_swa_q_g, v_swa_k_g=v_swa_k_g, v_swa_sink=v_swa_sink, v_group_g=v_group_g, v_w_out=v_w_out, v_norm2_g=v_norm2_g, v_w_ff1=v_w_ff1, v_w_ff2=v_w_ff2)
    weights = {n: given[n] for n in TWIN_WEIGHTS}
    shared = {n: given[n] for n in SHARED_INPUTS}
    per_example = {n: given[n] for n in ['x', 'c', 'ctx']}
    grad_fn = _jax.value_and_grad(_loss, argnums=(0, 1))

    def one_microbatch(ex, loss_target):
        ex = dict(ex)
        diff = ex.pop(TWIN_DIFF_INPUT)
        return grad_fn(weights, diff, {**shared, **ex}, loss_target)

    if N_MICROBATCH == 1:
        loss, (grad_w, grad_x) = one_microbatch(per_example, given["loss_target"])
    else:
        def body(carry, xs):
            loss_sum, grad_sum = carry
            l_k, (gw_k, gx_k) = one_microbatch(xs[0], xs[1])
            with _jax.named_scope("update"):
                return (loss_sum + l_k, _jax.tree.map(_jnp.add, grad_sum, gw_k)), gx_k

        init = (_jnp.zeros((), _jnp.float32), _jax.tree.map(_jnp.zeros_like, weights))
        (loss, grad_w), grad_x = _jax.lax.scan(body, init, (per_example, given["loss_target"]))
    with _jax.named_scope("update"):
        delta_w, new_m, new_v = {}, {}, {}
        for n in TWIN_WEIGHTS:
            delta_w[n], new_m[n], new_v[n] = _adamw(weights[n], grad_w[n], given["m_" + n], given["v_" + n])
    return (loss, grad_x, *[grad_w[n] for n in TWIN_WEIGHTS], *[delta_w[n] for n in TWIN_WEIGHTS],
            *[new_m[n] for n in TWIN_WEIGHTS], *[new_v[n] for n in TWIN_WEIGHTS])
```

```python
import functools
import math

import jax
import jax.numpy as jnp
from jax import lax
from jax.experimental import pallas as pl
from jax.experimental.pallas import tpu as pltpu

F32, BF16 = jnp.float32, jnp.bfloat16

N_DEV = 8
DEPTH = 2
D_MODEL = 1024
D_FF = 4096
N_MOD = 6
GRID_W = 64
WINDOW = 128
ROPE_THETA = 10000.0
EPS = 1e-6
NEG_INF = -1e30
LRU_C = 8.0
LRU_WIDTH = 512
MLA_HEADS, MLA_NOPE, MLA_ROPE, MLA_V = 8, 64, 32, 64
MLA_QK = MLA_NOPE + MLA_ROPE
MLA_Q_RANK, MLA_KV_RANK = 256, 128
SWA_HEADS, SWA_KV_HEADS, SWA_GROUP, SWA_HEAD_DIM = 8, 2, 4, 64
GROUP_WIDTH = 512
IN_SIZES = (256, 128, 32, 512, 512, 512, 128, 128)
IN_WIDTH = sum(IN_SIZES)
ADAM_LR, ADAM_B1, ADAM_B2, ADAM_EPS, ADAM_WD, ADAM_STEP = 0.001, 0.9, 0.999, 1e-08, 0.01, 10

LANE = 128
SUBLANE = 8
TB = 256
QB_SWA = 128
PACK_W = 1024
VMEM_LIMIT = 56 * 1024 * 1024
P_WIDTH = 3072
PC_SQ, PC_LX, PC_LG, PC_CQ, PC_SK, PC_SV, PC_CKV, PC_KR = 0, 1024, 1536, 2048, 2304, 2560, 2816, 2944
MIX_P = 2560


def _pcall(body, **kw):
    return pl.pallas_call(body, **kw)


def _cparams(n_grid):
    return pltpu.CompilerParams(dimension_semantics=("arbitrary",) * n_grid, vmem_limit_bytes=VMEM_LIMIT)


def _dg(a, b, ca, cb):
    return lax.dot_general(a.astype(BF16), b.astype(BF16), (((ca,), (cb,)), ((), ())),
                           preferred_element_type=F32)


@jax.custom_vjp
def _nn(a, b):
    return _dg(a, b, 1, 0)


@jax.custom_vjp
def _nt(a, b):
    return _dg(a, b, 1, 1)


@jax.custom_vjp
def _tn(a, b):
    return _dg(a, b, 0, 0)


_nn.defvjp(lambda a, b: (_nn(a, b), (a, b)), lambda r, ct: (_nt(ct, r[1]), _tn(r[0], ct)))
_nt.defvjp(lambda a, b: (_nt(a, b), (a, b)), lambda r, ct: (_nn(ct, r[1]), _tn(ct, r[0])))
_tn.defvjp(lambda a, b: (_tn(a, b), (a, b)), lambda r, ct: (_nt(r[1], ct), _nn(r[0], ct)))


@functools.partial(jax.custom_vjp, nondiff_argnums=(1, 2))
def _roll(x, shift, axis):
    return pltpu.roll(x, shift % x.shape[axis], axis)


_roll.defvjp(lambda x, shift, axis: (_roll(x, shift, axis), None),
             lambda shift, axis, _, ct: (_roll(ct, -shift, axis),))


@functools.partial(jax.custom_vjp, nondiff_argnums=(1, 2))
def _split(x, n, axis):
    w = x.shape[axis] // n
    return tuple(lax.slice_in_dim(x, i * w, (i + 1) * w, axis=axis) for i in range(n))


_split.defvjp(lambda x, n, axis: (_split(x, n, axis), None),
              lambda n, axis, _, cts: (jnp.concatenate(cts, axis=axis),))


@jax.custom_vjp
def _unstack(x):
    return tuple(x[i] for i in range(x.shape[0]))


_unstack.defvjp(lambda x: (_unstack(x), None), lambda _, cts: (jnp.stack(cts, axis=0),))


def _sig(x):
    return 0.5 * (jnp.tanh(0.5 * x) + 1.0)


def _gelu(x):
    return 0.5 * x * (1.0 + jnp.tanh(math.sqrt(2.0 / math.pi) * (x + 0.044715 * (x * x * x))))


def _rms(x, g, n):
    ms = jnp.sum(x * x, axis=-1, keepdims=True) * (1.0 / n)
    return x * lax.rsqrt(ms + EPS) * g


def _rope(y, cos, sa, sb, quarter):
    return y * cos + _roll(y, -quarter, 1) * sa + _roll(y, quarter, 1) * sb


def _softmax_rows(s, extra=None):
    m = jnp.max(s, axis=-1, keepdims=True)
    if extra is not None:
        m = jnp.maximum(m, extra)
    m = lax.stop_gradient(m)
    e = jnp.exp(s - m)
    den = jnp.sum(e, axis=-1, keepdims=True)
    if extra is not None:
        den = den + jnp.exp(extra - m)
    return e / den


class _A:
    def __init__(self, arr, block, imap, kind="row", first=None, gdtype=F32, gshape=None, gimap=None):
        self.arr, self.block, self.imap, self.kind, self.first = arr, block, imap, kind, first
        self.gdtype, self.gshape, self.gimap = gdtype, gshape, gimap


def _all_zero(*ids):
    return functools.reduce(jnp.logical_and, [i == 0 for i in ids])


def _par(arr):
    nd = arr.ndim
    return _A(arr, arr.shape, lambda *ids: (0,) * nd, "acc", first=_all_zero)


def _op_fwd(name, fn, grid, args, outs):
    n_in = len(args)

    def body(*refs):
        vals = [r[...].astype(F32) for r in refs[:n_in]]
        for r, v in zip(refs[n_in:], fn(*vals)):
            r[...] = v.astype(r.dtype)

    return _pcall(
        body, name=name, grid=grid,
        in_specs=[pl.BlockSpec(a.block, a.imap) for a in args],
        out_specs=[pl.BlockSpec(o[2], o[3]) for o in outs],
        out_shape=[jax.ShapeDtypeStruct(o[0], o[1]) for o in outs],
        compiler_params=_cparams(len(grid)),
    )(*[a.arr for a in args])


def _op_bwd(name, fn, grid, args, outs, ct_arrays):
    n_in, n_ct = len(args), len(outs)
    didx = [i for i, a in enumerate(args) if a.kind != "const"]

    def body(*refs):
        ids = [pl.program_id(i) for i in range(len(grid))]
        vals = [r[...].astype(F32) for r in refs[:n_in]]

        def g(*dv):
            full = list(vals)
            for i, v in zip(didx, dv):
                full[i] = v
            return tuple(fn(*full))

        _, vjp = jax.vjp(g, *[vals[i] for i in didx])
        grads = vjp(tuple(r[...].astype(F32) for r in refs[n_in:n_in + n_ct]))
        for gr, i, r in zip(grads, didx, refs[n_in + n_ct:]):
            a = args[i]
            if a.kind == "row":
                r[...] = gr.astype(r.dtype)
            else:
                first = a.first(*ids)

                @pl.when(first)
                def _():
                    r[...] = gr

                @pl.when(jnp.logical_not(first))
                def _():
                    r[...] += gr

    g_specs, g_shapes = [], []
    for i in didx:
        a = args[i]
        if a.kind == "row":
            g_specs.append(pl.BlockSpec(a.block, a.gimap or a.imap))
            g_shapes.append(jax.ShapeDtypeStruct(a.gshape or a.arr.shape, a.gdtype))
        else:
            g_specs.append(pl.BlockSpec(a.block, a.imap))
            g_shapes.append(jax.ShapeDtypeStruct(a.arr.shape, F32))
    return _pcall(
        body, name=name, grid=grid,
        in_specs=[pl.BlockSpec(a.block, a.imap) for a in args] + [pl.BlockSpec(o[2], o[3]) for o in outs],
        out_specs=g_specs, out_shape=g_shapes,
        compiler_params=_cparams(len(grid)),
    )(*[a.arr for a in args], *ct_arrays)


def _rowop(name, fn, grid, args, outs):
    res = _op_fwd(name, fn, grid, args, outs)
    return res, lambda *cts: _op_bwd(name + "_bwd", fn, grid, args, outs, cts)


def _pick(n, cap):
    best = None
    for t in range(LANE, cap + 1, LANE):
        if n % t == 0:
            best = t
    return best or n


def _mm(a, b, mode, out_dtype, name, epi=None, aux=None):
    if mode == "nn":
        (m, k), n = a.shape, b.shape[1]
    elif mode == "nt":
        (m, k), n = a.shape, b.shape[0]
    else:
        (k, m), n = a.shape, b.shape[1]
    tm = 512 if m % 512 == 0 else m
    tn, tk = _pick(n, 1024), _pick(k, 512)
    nk = k // tk
    if mode == "tn":
        a_spec = pl.BlockSpec((tk, tm), lambda i, j, kk: (kk, i))
    else:
        a_spec = pl.BlockSpec((tm, tk), lambda i, j, kk: (i, kk))
    if mode == "nt":
        b_spec = pl.BlockSpec((tn, tk), lambda i, j, kk: (j, kk))
    else:
        b_spec = pl.BlockSpec((tk, tn), lambda i, j, kk: (kk, j))
    dims = {"nn": (1, 0), "nt": (1, 1), "tn": (0, 0)}[mode]
    o_spec = pl.BlockSpec((tm, tn), lambda i, j, kk: (i, j))
    n_aux = 0 if aux is None else 1
    n_out = 2 if epi == "sqrelu" else 1

    def body(*refs):
        a_ref, b_ref = refs[0], refs[1]
        o_refs = refs[2 + n_aux:2 + n_aux + n_out]
        acc = refs[-1]
        kk = pl.program_id(2)

        @pl.when(kk == 0)
        def _():
            acc[...] = jnp.zeros_like(acc)

        acc[...] += _dg(a_ref[...], b_ref[...], *dims)

        @pl.when(kk == nk - 1)
        def _():
            r = acc[...]
            if epi == "sqrelu":
                o_refs[0][...] = r.astype(o_refs[0].dtype)
                rl = jnp.maximum(r, 0.0)
                o_refs[1][...] = (rl * rl).astype(o_refs[1].dtype)
            elif epi == "dsqrelu":
                pre = refs[2][...].astype(F32)
                o_refs[0][...] = (r * (2.0 * jnp.maximum(pre, 0.0))).astype(o_refs[0].dtype)
            else:
                o_refs[0][...] = r.astype(o_refs[0].dtype)

    res = _pcall(
        body, name=name, grid=(m // tm, n // tn, nk),
        in_specs=[a_spec, b_spec] + [o_spec] * n_aux,
        out_specs=[o_spec] * n_out,
        out_shape=[jax.ShapeDtypeStruct((m, n), out_dtype)] * n_out,
        scratch_shapes=[pltpu.VMEM((tm, tn), F32)],
        compiler_params=_cparams(3),
    )(a, b, *([aux] if aux is not None else []))
    return res if n_out == 2 else res[0]


def _mla_block(q, k, v):
    p = _softmax_rows(_nt(q, k) * (MLA_QK ** -0.5))
    return _nn(p, v)


def _mla_attn(q, k, v, tc, ctx_q, name):
    bsz, t_all, _ = q.shape
    n_t = t_all // TB
    grid = (bsz, MLA_HEADS, n_t)
    q_spec = pl.BlockSpec((None, TB, LANE), lambda b, h, t: (b, t, h))
    kv_spec = pl.BlockSpec((None, t_all, LANE), lambda b, h, t: (b, 0, h))

    def fwd_body(q_ref, k_ref, v_ref, o_ref):
        t = pl.program_id(2)

        @pl.when(t == 0)
        def _():
            if ctx_q:
                o_ref[...] = _mla_block(q_ref[...], k_ref[0:tc, :], v_ref[0:tc, :])
            else:
                o_ref[...] = jnp.zeros_like(o_ref)

        @pl.when(t > 0)
        def _():
            o_ref[...] = _mla_block(q_ref[...], k_ref[...], v_ref[...])

    o = _pcall(fwd_body, name=name, grid=grid, in_specs=[q_spec, kv_spec, kv_spec], out_specs=q_spec,
               out_shape=jax.ShapeDtypeStruct(q.shape, F32), compiler_params=_cparams(3))(q, k, v)

    def bwd(do):
        def bwd_body(q_ref, k_ref, v_ref, do_ref, dq_ref, dk_ref, dv_ref):
            t = pl.program_id(2)

            @pl.when(t == 0)
            def _():
                dk_ref[...] = jnp.zeros_like(dk_ref)
                dv_ref[...] = jnp.zeros_like(dv_ref)
                if ctx_q:
                    _, vjp = jax.vjp(_mla_block, q_ref[...].astype(F32), k_ref[0:tc, :].astype(F32),
                                     v_ref[0:tc, :].astype(F32))
                    dq, dk, dv = vjp(do_ref[...])
                    dq_ref[...] = dq
                    dk_ref[0:tc, :] = dk
                    dv_ref[0:tc, :] = dv
                else:
                    dq_ref[...] = jnp.zeros_like(dq_ref)

            @pl.when(t > 0)
            def _():
                _, vjp = jax.vjp(_mla_block, q_ref[...].astype(F32), k_ref[...].astype(F32), v_ref[...].astype(F32))
                dq, dk, dv = vjp(do_ref[...])
                dq_ref[...] = dq
                dk_ref[...] += dk
                dv_ref[...] += dv

        return _pcall(bwd_body, name=name + "_bwd", grid=grid, in_specs=[q_spec, kv_spec, kv_spec, q_spec],
                      out_specs=[q_spec, kv_spec, kv_spec],
                      out_shape=[jax.ShapeDtypeStruct(q.shape, F32)] * 3, compiler_params=_cparams(3))(q, k, v, do)

    return o, bwd


def _swa_ctx_block(q, kc, vc, sink):
    sk = jnp.sum(sink, axis=-1, keepdims=True) * (1.0 / LANE)
    sks = _split(sk, SWA_GROUP, 0)
    outs = []
    for qh, s_h in zip(_split(q, SWA_GROUP, 1), sks):
        p = _softmax_rows(_nt(qh, kc) * (SWA_HEAD_DIM ** -0.5), s_h)
        outs.append(_nn(p, vc))
    return jnp.concatenate(outs, axis=1)


def _swa_win_block(q, kc, kw, vc, vw, sink, mask):
    keys = jnp.concatenate([kc, kw], axis=0)
    vals = jnp.concatenate([vc, vw], axis=0)
    sk = jnp.sum(sink, axis=-1, keepdims=True) * (1.0 / LANE)
    sks = _split(sk, SWA_GROUP, 0)
    outs = []
    for qh, s_h in zip(_split(q, SWA_GROUP, 1), sks):
        s = jnp.where(mask, _nt(qh, keys) * (SWA_HEAD_DIM ** -0.5), NEG_INF)
        outs.append(_nn(_softmax_rows(s, s_h), vals))
    return jnp.concatenate(outs, axis=1)


def _swa_attn(q, k, p_all, sink_b, tc, ctx_q, name):
    bsz, t_all, _ = q.shape
    n_q = t_all // QB_SWA
    n_cq = tc // QB_SWA
    lat = t_all - tc
    span = QB_SWA + 2 * WINDOW
    gw = SWA_GROUP * LANE
    grid = (bsz, SWA_KV_HEADS, n_q)
    q_spec = pl.BlockSpec((None, QB_SWA, gw), lambda b, g, i: (b, i, g))
    k_spec = pl.BlockSpec((None, t_all, LANE), lambda b, g, i: (b, 0, g))
    v_spec = pl.BlockSpec((None, t_all, LANE), lambda b, g, i: (b, 0, PC_SV // LANE + g))
    s_spec = pl.BlockSpec((None, SWA_GROUP * QB_SWA, LANE), lambda b, g, i: (g, 0, 0))

    def window(i):
        q0 = (i - n_cq) * QB_SWA
        w0 = jnp.clip(q0 - WINDOW, 0, lat - span)
        w0 = pl.multiple_of(w0, QB_SWA)
        qi = q0 + lax.broadcasted_iota(jnp.int32, (QB_SWA, tc + span), 0)
        col = lax.broadcasted_iota(jnp.int32, (QB_SWA, tc + span), 1)
        kj = w0 + col - tc
        mask = (col < tc) | ((kj >= qi - WINDOW) & (kj <= qi + WINDOW))
        return w0, mask

    def fwd_body(q_ref, k_ref, v_ref, s_ref, o_ref):
        i = pl.program_id(2)

        @pl.when(i < n_cq)
        def _():
            if ctx_q:
                o_ref[...] = _swa_ctx_block(q_ref[...].astype(F32), k_ref[0:tc, :], v_ref[0:tc, :], s_ref[...])
            else:
                o_ref[...] = jnp.zeros_like(o_ref)

        @pl.when(i >= n_cq)
        def _():
            w0, mask = window(i)
            o_ref[...] = _swa_win_block(q_ref[...].astype(F32), k_ref[0:tc, :], k_ref[pl.ds(tc + w0, span), :],
                                        v_ref[0:tc, :], v_ref[pl.ds(tc + w0, span), :], s_ref[...], mask)

    o = _pcall(fwd_body, name=name, grid=grid, in_specs=[q_spec, k_spec, v_spec, s_spec], out_specs=q_spec,
               out_shape=jax.ShapeDtypeStruct(q.shape, F32), compiler_params=_cparams(3))(q, k, p_all, sink_b)

    def bwd(do):
        def bwd_body(q_ref, k_ref, v_ref, s_ref, do_ref, dq_ref, dk_ref, dv_ref, ds_ref):
            i = pl.program_id(2)

            @pl.when(i == 0)
            def _():
                dk_ref[...] = jnp.zeros_like(dk_ref)
                dv_ref[...] = jnp.zeros_like(dv_ref)
                ds_ref[...] = jnp.zeros_like(ds_ref)

            @pl.when(i < n_cq)
            def _():
                if ctx_q:
                    _, vjp = jax.vjp(_swa_ctx_block, q_ref[...].astype(F32), k_ref[0:tc, :].astype(F32),
                                     v_ref[0:tc, :], s_ref[...])
                    dq, dk, dv, ds = vjp(do_ref[...])
                    dq_ref[...] = dq
                    dk_ref[0:tc, :] += dk
                    dv_ref[0:tc, :] += dv
                    ds_ref[...] += ds
                else:
                    dq_ref[...] = jnp.zeros_like(dq_ref)

            @pl.when(i >= n_cq)
            def _():
                w0, mask = window(i)
                win = pl.ds(tc + w0, span)
                _, vjp = jax.vjp(functools.partial(_swa_win_block, mask=mask), q_ref[...].astype(F32),
                                 k_ref[0:tc, :].astype(F32), k_ref[win, :].astype(F32),
                                 v_ref[0:tc, :], v_ref[win, :], s_ref[...])
                dq, dkc, dkw, dvc, dvw, ds = vjp(do_ref[...])
                dq_ref[...] = dq
                dk_ref[0:tc, :] += dkc
                dk_ref[win, :] += dkw
                dv_ref[0:tc, :] += dvc
                dv_ref[win, :] += dvw
                ds_ref[...] += ds

        kv_out = pl.BlockSpec((None, t_all, LANE), lambda b, g, i: (b, 0, g))
        ds_spec = pl.BlockSpec((None, None, SWA_GROUP * QB_SWA, LANE), lambda b, g, i: (b, g, 0, 0))
        kv_shape = jax.ShapeDtypeStruct((bsz, t_all, SWA_KV_HEADS * LANE), F32)
        return _pcall(
            bwd_body, name=name + "_bwd", grid=grid, in_specs=[q_spec, k_spec, v_spec, s_spec, q_spec],
            out_specs=[q_spec, kv_out, kv_out, ds_spec],
            out_shape=[jax.ShapeDtypeStruct(q.shape, F32), kv_shape, kv_shape,
                       jax.ShapeDtypeStruct((bsz,) + sink_b.shape, F32)],
            compiler_params=_cparams(3))(q, k, p_all, sink_b, do)

    return o, bwd


def _scan_rows(a, u, reverse, a_s, u_s, c_s):
    t_all, c = a.shape
    row8 = lax.broadcasted_iota(jnp.int32, a.shape, 0) % SUBLANE
    for d in (1, 2, 4):
        sh = d if not reverse else t_all - d
        ar, ur = pltpu.roll(a, sh, 0), pltpu.roll(u, sh, 0)
        m = (row8 >= d) if not reverse else (row8 < SUBLANE - d)
        u = jnp.where(m, a * ur + u, u)
        a = jnp.where(m, a * ar, a)
    a_s[...] = a
    u_s[...] = u
    n_tiles = t_all // SUBLANE

    def step(j, carry):
        tile = j if not reverse else n_tiles - 1 - j
        base = pl.multiple_of(tile * SUBLANE, SUBLANE)
        c_s[pl.ds(base, SUBLANE), :] = jnp.broadcast_to(carry, (SUBLANE, c))
        last = base + (0 if reverse else SUBLANE - 1)
        return a_s[pl.ds(last, 1), :] * carry + u_s[pl.ds(last, 1), :]

    lax.fori_loop(0, n_tiles, step, jnp.zeros((1, c), F32))
    return a_s[...] * c_s[...] + u_s[...]


def _shift_rows(x, reverse_src):
    t_all = x.shape[0]
    row = lax.broadcasted_iota(jnp.int32, x.shape, 0)
    if reverse_src:
        return jnp.where(row == t_all - 1, 0.0, pltpu.roll(x, t_all - 1, 0))
    return jnp.where(row == 0, 0.0, pltpu.roll(x, 1, 0))


def _lru_scan(a0, u0, a1, u1, name):
    bsz, t_all, w = a0.shape
    grid = (bsz, w // LANE)
    spec = pl.BlockSpec((None, t_all, LANE), lambda b, c: (b, 0, c))
    scratch = [pltpu.VMEM((t_all, LANE), F32)] * 3
    shape = jax.ShapeDtypeStruct(a0.shape, F32)

    def fwd_body(a0_ref, u0_ref, a1_ref, u1_ref, h0_ref, h1_ref, a_s, u_s, c_s):
        h0_ref[...] = _scan_rows(a0_ref[...], u0_ref[...], False, a_s, u_s, c_s)
        h1_ref[...] = _scan_rows(a1_ref[...], u1_ref[...], True, a_s, u_s, c_s)

    h0, h1 = _pcall(fwd_body, name=name, grid=grid, in_specs=[spec] * 4, out_specs=[spec] * 2,
                    out_shape=[shape] * 2, scratch_shapes=scratch, compiler_params=_cparams(2))(a0, u0, a1, u1)

    def bwd(dh0, dh1):
        def bwd_body(a0_ref, h0_ref, g0_ref, a1_ref, h1_ref, g1_ref, da0_ref, du0_ref, da1_ref, du1_ref,
                     a_s, u_s, c_s):
            g0 = _scan_rows(_shift_rows(a0_ref[...], True), g0_ref[...], True, a_s, u_s, c_s)
            du0_ref[...] = g0
            da0_ref[...] = g0 * _shift_rows(h0_ref[...], False)
            g1 = _scan_rows(_shift_rows(a1_ref[...], False), g1_ref[...], False, a_s, u_s, c_s)
            du1_ref[...] = g1
            da1_ref[...] = g1 * _shift_rows(h1_ref[...], True)

        return _pcall(bwd_body, name=name + "_bwd", grid=grid, in_specs=[spec] * 6, out_specs=[spec] * 4,
                      out_shape=[shape] * 4, scratch_shapes=scratch,
                      compiler_params=_cparams(2))(a0, h0, dh0, a1, h1, dh1)

    return h0, h1, bwd


def _f_mod(x, g, shift, scale):
    return (_rms(x, g, D_MODEL) * (1.0 + scale) + shift,)


def _f_mla_q(cq, ga, w, gh, cos, sa, sb):
    n = _rms(cq, ga, MLA_Q_RANK)
    outs = []
    for wh in _split(w, MLA_HEADS, 1):
        outs.append(_rope(_rms(_nn(n, wh), gh, MLA_QK), cos, sa, sb, MLA_ROPE // 4))
    return (jnp.concatenate(outs, axis=1),)


def _f_mla_kv(ckv, krp, ga, wk, wv, gh, cos, sa, sb):
    n = _rms(ckv, ga, MLA_KV_RANK)
    outs = []
    for wh in _split(wk, MLA_HEADS, 1):
        outs.append(_rope(_rms(_nn(n, wh) + krp, gh, MLA_QK), cos, sa, sb, MLA_ROPE // 4))
    return jnp.concatenate(outs, axis=1), _nn(n, wv)


def _f_conv(x, w0, w1, w2, w3, bias, tc):
    t_all = x.shape[0]
    row = lax.broadcasted_iota(jnp.int32, x.shape, 0)
    lo = jnp.where(row < tc, 0, tc)
    hi = jnp.where(row < tc, tc, t_all)
    y = bias + jnp.zeros_like(x)
    for kk, wk in enumerate((w0, w1, w2, w3)):
        src = row + (kk - 2)
        xs = x if kk == 2 else _roll(x, 2 - kk, 0)
        y = y + wk * jnp.where((src >= lo) & (src < hi), xs, 0.0)
    return (y,)


def _f_gates(xc, w16, b00, b01, b10, b11, sp0, sp1):
    ws = _unstack(w16)
    n_cb = LRU_WIDTH // LANE
    xcs = _split(xc, n_cb, 1)
    bias = [_split(b, n_cb, 1) for b in (b00, b01, b10, b11)]
    sps = [_split(s, n_cb, 1) for s in (sp0, sp1)]
    res = [[], [], [], []]
    for c in range(n_cb):
        for z in range(2):
            r = _sig(_nn(xcs[c], ws[c * 4 + 2 * z]) + bias[2 * z][c])
            i = _sig(_nn(xcs[c], ws[c * 4 + 2 * z + 1]) + bias[2 * z + 1][c])
            la = -LRU_C * r * sps[z][c]
            res[2 * z].append(jnp.exp(la))
            res[2 * z + 1].append(jnp.sqrt(-jnp.tanh(la) * (jnp.exp(2.0 * la) + 1.0)) * (i * xcs[c]))
    return tuple(jnp.concatenate(r, axis=1) for r in res)


def _f_lru_out(h0, h1, lg):
    return ((h0 + h1) * _gelu(lg),)


def _f_swa_qk(sq, sk, gq, gk, cos, sa, sb):
    qs = [_rope(_rms(x, gq, SWA_HEAD_DIM), cos, sa, sb, SWA_HEAD_DIM // 4) for x in _split(sq, SWA_HEADS, 1)]
    ks = [_rope(_rms(x, gk, SWA_HEAD_DIM), cos, sa, sb, SWA_HEAD_DIM // 4) for x in _split(sk, SWA_KV_HEADS, 1)]
    return jnp.concatenate(qs, axis=1), jnp.concatenate(ks, axis=1)


def _f_merge(oa, ob, oc, ga, gb, gc):
    return (jnp.concatenate([_rms(oa, ga, GROUP_WIDTH), _rms(ob, gb, GROUP_WIDTH), _rms(oc, gc, GROUP_WIDTH)],
                            axis=1),)


def _f_resid_mod(x, y, gate, g, shift, scale):
    x1 = x + gate * y
    return x1, _rms(x1, g, D_MODEL) * (1.0 + scale) + shift


def _f_resid(x, y, gate):
    return (x + gate * y,)


def _layer(li, x, mods, w, s, tabs, tc, ctx_q):
    bsz, t_all, _ = x.shape
    n_t = t_all // TB
    grid = (bsz, n_t)
    rows = lambda b, t: (b, t, 0)

    def row(arr, width=None, idx=0, gdtype=F32, gshape=None):
        width = width or arr.shape[-1]
        return _A(arr, (None, TB, width), lambda b, t: (b, t, idx), "row", gdtype=gdtype, gshape=gshape,
                  gimap=rows if gshape is not None else None)

    def out(width, dtype, imap=rows):
        return ((bsz, t_all, width), dtype, (None, TB, width), imap)

    def modarg(arr):
        return _A(arr, (None, None, 1, D_MODEL), lambda b, t: (b, jnp.minimum(t, 1), 0, 0), "acc",
                  first=lambda b, t: t <= 1)

    def tab(arr):
        return _A(arr, (TB, LANE), lambda b, t: (t, 0), "const")

    def pcol(p_all, col, width):
        return row(p_all, width, col // width, gdtype=BF16, gshape=(bsz, t_all, width))

    nm = lambda base: "%s_l%d" % (base, li)
    sh1, sc1, g1, sh2, sc2, g2 = mods
    m_all = bsz * t_all

    (h,), b_mod1 = _rowop(nm("mod1"), _f_mod, grid, [row(x), _par(s["norm1_g"]), modarg(sh1), modarg(sc1)],
                          [out(D_MODEL, BF16)])
    p_all = _mm(h.reshape(m_all, D_MODEL), w["win"], "nn", F32, nm("mm_in")).reshape(bsz, t_all, P_WIDTH)

    tq, tk_ = tabs["mla"], tabs["mla"]
    (q_a,), b_mq = _rowop(nm("mla_q"), _f_mla_q, grid,
                          [pcol(p_all, PC_CQ, 256), _par(s["q_a_g"]), _par(w["wuq"]), _par(s["mla_q_g"])]
                          + [tab(a) for a in tq], [out(MLA_HEADS * LANE, BF16)])
    (k_a, v_a), b_mkv = _rowop(nm("mla_kv"), _f_mla_kv, grid,
                               [pcol(p_all, PC_CKV, 128), pcol(p_all, PC_KR, 128), _par(s["kv_a_g"]), _par(w["wk"]),
                                _par(w["wv"]), _par(s["mla_k_g"])] + [tab(a) for a in tk_],
                               [out(MLA_HEADS * LANE, BF16), out(MLA_HEADS * LANE, BF16)])
    o_a, b_attn_a = _mla_attn(q_a, k_a, v_a, tc, ctx_q, nm("mla_attn"))

    n_cb = LRU_WIDTH // LANE
    conv_grid = (n_cb, bsz)
    cpar = lambda arr: _A(arr, (1, LANE), lambda c, b: (0, c), "acc", first=lambda c, b: b == 0)
    conv_args = [_A(p_all, (None, t_all, LANE), lambda c, b: (b, 0, PC_LX // LANE + c), "row", gdtype=BF16,
                    gshape=(bsz, t_all, LRU_WIDTH), gimap=lambda c, b: (b, 0, c))]
    conv_args += [cpar(a) for a in s["conv_w"]] + [cpar(s["conv_b"])]
    conv_out = [((bsz, t_all, LRU_WIDTH), F32, (None, t_all, LANE), lambda c, b: (b, 0, c))]
    (xc,), b_conv = _rowop(nm("lru_conv"), functools.partial(_f_conv, tc=tc), conv_grid, conv_args, conv_out)
    rot = lambda b, t: (b, (t + n_t - 1) % n_t, 0)
    (a0, u0, a1, u1), b_gates = _rowop(
        nm("lru_gates"), _f_gates, grid,
        [row(xc), _par(s["wbd"])] + [_par(a) for a in s["gate_b"]] + [_par(a) for a in s["sp"]],
        [out(LRU_WIDTH, F32), out(LRU_WIDTH, F32), out(LRU_WIDTH, F32, rot), out(LRU_WIDTH, F32, rot)])
    h0, h1, b_scan = _lru_scan(a0, u0, a1, u1, nm("lru_scan"))
    h1_arg = _A(h1, (None, TB, LRU_WIDTH), rot, "row")
    (o_b,), b_lout = _rowop(nm("lru_out"), _f_lru_out, grid, [row(h0), h1_arg, pcol(p_all, PC_LG, 512)],
                            [out(LRU_WIDTH, F32)])

    ts = tabs["swa"]
    (q_c, k_c), b_sqk = _rowop(nm("swa_qk"), _f_swa_qk, grid,
                               [pcol(p_all, PC_SQ, 1024), pcol(p_all, PC_SK, 256), _par(s["swa_q_g"]),
                                _par(s["swa_k_g"])] + [tab(a) for a in ts],
                               [out(SWA_HEADS * LANE, BF16), out(SWA_KV_HEADS * LANE, BF16)])
    o_c, b_attn_c = _swa_attn(q_c, k_c, p_all, s["sink_b"], tc, ctx_q, nm("swa_attn"))

    (y_in,), b_merge = _rowop(nm("merge"), _f_merge, grid,
                              [row(o_a), row(o_b), row(o_c), _par(s["g_a"]), _par(s["g_b"]), _par(s["g_c"])],
                              [out(MIX_P, BF16)])
    y = _mm(y_in.reshape(m_all, MIX_P), w["wout"], "nn", F32, nm("mm_out")).reshape(bsz, t_all, D_MODEL)
    (x1, hm), b_rm = _rowop(nm("resid_mod"), _f_resid_mod, grid,
                            [row(x), row(y, gdtype=BF16), modarg(g1), _par(s["norm2_g"]), modarg(sh2), modarg(sc2)],
                            [out(D_MODEL, F32), out(D_MODEL, BF16)])
    pre, act = _mm(hm.reshape(m_all, D_MODEL), w["ff1"], "nn", BF16, nm("mm_ff1"), epi="sqrelu")
    y2 = _mm(act, w["ff2"], "nn", F32, nm("mm_ff2")).reshape(bsz, t_all, D_MODEL)
    (x2,), b_res = _rowop(nm("resid"), _f_resid, grid, [row(x1), row(y2, gdtype=BF16), modarg(g2)],
                          [out(D_MODEL, F32)])

    def bwd(dx2):
        dw, ds = {}, {}
        dx1a, dy2, dg2 = b_res(dx2)
        dy2 = dy2.reshape(m_all, D_MODEL)
        dpre = _mm(dy2, w["ff2"], "nt", BF16, nm("mm_ff2_dx"), epi="dsqrelu", aux=pre)
        dw["ff2"] = _mm(act, dy2, "tn", F32, nm("mm_ff2_dw"))
        dhm = _mm(dpre, w["ff1"], "nt", F32, nm("mm_ff1_dx")).reshape(bsz, t_all, D_MODEL)
        dw["ff1"] = _mm(hm.reshape(m_all, D_MODEL), dpre, "tn", F32, nm("mm_ff1_dw"))
        dxa, dy, dg1, ds["norm2_g"], dsh2, dsc2 = b_rm(dx1a, dhm)
        dy = dy.reshape(m_all, D_MODEL)
        dy_in = _mm(dy, w["wout"], "nt", F32, nm("mm_out_dx")).reshape(bsz, t_all, MIX_P)
        dw["wout"] = _mm(y_in.reshape(m_all, MIX_P), dy, "tn", F32, nm("mm_out_dw"))
        do_a, do_b, do_c, ds["g_a"], ds["g_b"], ds["g_c"] = b_merge(dy_in)

        dq_c, dk_c, dsv, dsink = b_attn_c(do_c)
        ds["sink_b"] = jnp.sum(dsink, axis=0)
        dsq, dsk, ds["swa_q_g"], ds["swa_k_g"] = b_sqk(dq_c, dk_c)

        dh0, dh1, dlg = b_lout(do_b)
        da0, du0, da1, du1 = b_scan(dh0, dh1)
        gates_g = b_gates(da0, du0, da1, du1)
        dxc, ds["wbd"] = gates_g[0], gates_g[1]
        ds["gate_b"], ds["sp"] = list(gates_g[2:6]), list(gates_g[6:8])
        conv_g = b_conv(dxc)
        dlx, ds["conv_w"], ds["conv_b"] = conv_g[0], list(conv_g[1:5]), conv_g[5]

        dq_a, dk_a, dv_a = b_attn_a(do_a)
        dcq, ds["q_a_g"], dw["wuq"], ds["mla_q_g"] = b_mq(dq_a)
        dckv, dkr, ds["kv_a_g"], dw["wk"], dw["wv"], ds["mla_k_g"] = b_mkv(dk_a, dv_a)

        dp = jnp.concatenate([dsq, dlx, dlg, dcq, dsk, dsv.astype(BF16), dckv, dkr], axis=-1)
        dp = dp.reshape(m_all, P_WIDTH)
        dh = _mm(dp, w["win"], "nt", F32, nm("mm_in_dx")).reshape(bsz, t_all, D_MODEL)
        dw["win"] = _mm(h.reshape(m_all, D_MODEL), dp, "tn", F32, nm("mm_in_dw"))
        dxb, ds["norm1_g"], dsh1, dsc1 = b_mod1(dh)
        return dxa + dxb, [dsh1, dsc1, dg1, dsh2, dsc2, dg2], dw, ds

    return x2, bwd


def _loss_and_grad(x2, target, tc):
    bsz, t_all, d = x2.shape
    n_t = t_all // TB
    n_c = tc // TB

    def body(x_ref, t_ref, l_ref, dx_ref):
        b, t = pl.program_id(0), pl.program_id(1)

        @pl.when((b == 0) & (t == 0))
        def _():
            l_ref[...] = jnp.zeros_like(l_ref)

        @pl.when(t < n_c)
        def _():
            dx_ref[...] = jnp.zeros_like(dx_ref)

        @pl.when(t >= n_c)
        def _():
            e = x_ref[...] - t_ref[...]
            dx_ref[...] = e * (1.0 / d)
            l_ref[...] += jnp.sum(e * e) * (0.5 / d)

    loss, dx = _pcall(
        body, name="loss", grid=(bsz, n_t),
        in_specs=[pl.BlockSpec((None, TB, d), lambda b, t: (b, t, 0)),
                  pl.BlockSpec((None, TB, d), lambda b, t: (b, jnp.maximum(t - n_c, 0), 0))],
        out_specs=[pl.BlockSpec((SUBLANE, LANE), lambda b, t: (0, 0)),
                   pl.BlockSpec((None, TB, d), lambda b, t: (b, t, 0))],
        out_shape=[jax.ShapeDtypeStruct((SUBLANE, LANE), F32), jax.ShapeDtypeStruct(x2.shape, F32)],
        compiler_params=_cparams(2))(x2, target)
    return loss[0, 0], dx


def _rope_tables(lat, tc, dim, lane0):
    quarter = dim // 4
    pos = jnp.arange(lat)
    grid_pos = jnp.stack([pos // GRID_W, pos % GRID_W], axis=-1).astype(F32)
    lane = jnp.arange(LANE)
    p = jnp.clip(lane - lane0, 0, dim - 1)
    active = (lane >= lane0) & (lane < lane0 + dim)
    axis, half, qi = p // (dim // 2), (p % (dim // 2)) // quarter, p % quarter
    inv = ROPE_THETA ** (-qi.astype(F32) / quarter)
    ang = jnp.where(axis[None, :] == 0, grid_pos[:, 0:1], grid_pos[:, 1:2]) * inv[None, :]
    cos = jnp.where(active, jnp.cos(ang), 1.0)
    sin = jnp.where(active, jnp.sin(ang), 0.0)
    sa = jnp.where(half == 0, -sin, 0.0)
    sb = jnp.where(half == 1, sin, 0.0)
    ctx1, ctx0 = jnp.ones((tc, LANE), F32), jnp.zeros((tc, LANE), F32)
    return (jnp.concatenate([ctx1, cos], 0), jnp.concatenate([ctx0, sa], 0), jnp.concatenate([ctx0, sb], 0))


def _local_step(x, ctx, target, mods, wp, sp):
    tc, lat = ctx.shape[1], x.shape[1]
    tabs = {"mla": _rope_tables(lat, tc, MLA_ROPE, MLA_NOPE), "swa": _rope_tables(lat, tc, SWA_HEAD_DIM, 0)}
    stream = jnp.concatenate([ctx, x], axis=1)
    bwds = []
    for li in range(DEPTH):
        stream, bwd = _layer(li, stream, mods[li], wp[li], sp[li], tabs, tc, ctx_q=li < DEPTH - 1)
        bwds.append(bwd)
    loss, dstream = _loss_and_grad(stream, target, tc)
    dmods, dw, ds = [None] * DEPTH, [None] * DEPTH, [None] * DEPTH
    for li in reversed(range(DEPTH)):
        dstream, dmods[li], dw[li], ds[li] = bwds[li](dstream)
    return loss, dstream[:, tc:], dmods, dw, ds


_BIG = (("w_in", (DEPTH, D_MODEL, IN_WIDTH // N_DEV), 2), ("w_uq", (DEPTH, MLA_Q_RANK, MLA_HEADS * MLA_QK // N_DEV), 2),
        ("w_ukv", (DEPTH, MLA_KV_RANK, 128), 2), ("w_out", (DEPTH, 1536 // N_DEV, D_MODEL), 1),
        ("w_ff1", (DEPTH, D_MODEL, D_FF // N_DEV), 2), ("w_ff2", (DEPTH, D_FF // N_DEV, D_MODEL), 1))


def _pad_heads(wm, n_heads, dim, axis=-1):
    axis = axis % wm.ndim
    shp = wm.shape[:axis] + (n_heads, dim) + wm.shape[axis + 1:]
    pad = [(0, 0)] * len(shp)
    pad[axis + 1] = (0, LANE - dim)
    out = jnp.pad(wm.reshape(shp), pad)
    return out.reshape(wm.shape[:axis] + (n_heads * LANE,) + wm.shape[axis + 1:])


def _prep_big(gathered):
    flat = gathered.reshape(N_DEV, -1)
    full, off = {}, 0
    for name, shp, ax in _BIG:
        n = math.prod(shp)
        piece = flat[:, off:off + n].reshape((N_DEV,) + shp)
        off += n
        piece = jnp.moveaxis(piece, 0, ax)
        full[name] = piece.reshape(shp[:ax] + (N_DEV * shp[ax],) + shp[ax + 1:])
    layers = []
    for li in range(DEPTH):
        cq, ckv, kr, lx, lg, sq, sk, sv = _split_cols(full["w_in"][li])
        win = jnp.concatenate([_pad_heads(sq, SWA_HEADS, SWA_HEAD_DIM), lx, lg, cq,
                               _pad_heads(sk, SWA_KV_HEADS, SWA_HEAD_DIM), _pad_heads(sv, SWA_KV_HEADS, SWA_HEAD_DIM),
                               ckv, jnp.pad(kr, ((0, 0), (MLA_NOPE, LANE - MLA_QK)))], axis=1)
        ukv = full["w_ukv"][li].reshape(MLA_KV_RANK, MLA_HEADS, MLA_NOPE + MLA_V)
        wo = full["w_out"][li]
        wout = jnp.concatenate([_pad_heads(wo[:GROUP_WIDTH], MLA_HEADS, MLA_V, axis=0), wo[GROUP_WIDTH:2 * GROUP_WIDTH],
                                _pad_heads(wo[2 * GROUP_WIDTH:], SWA_HEADS, SWA_HEAD_DIM, axis=0)], axis=0)
        layers.append({
            "win": win,
            "wuq": _pad_heads(full["w_uq"][li], MLA_HEADS, MLA_QK),
            "wk": _pad_heads(ukv[:, :, :MLA_NOPE].reshape(MLA_KV_RANK, -1), MLA_HEADS, MLA_NOPE),
            "wv": _pad_heads(ukv[:, :, MLA_NOPE:].reshape(MLA_KV_RANK, -1), MLA_HEADS, MLA_V),
            "wout": wout, "ff1": full["w_ff1"][li], "ff2": full["w_ff2"][li]})
    return layers


def _split_cols(wm):
    parts, start = [], 0
    for size in IN_SIZES:
        parts.append(wm[:, start:start + size])
        start += size
    return parts


def _prep_small(raw):
    layers = []
    eye2 = jnp.eye(2, dtype=F32)
    for li in range(DEPTH):
        r1 = lambda a: a.reshape(1, -1)
        gw = raw["lru_gate_w"][li].reshape(2, 2, 4, 2, 64, 64)
        wbd = jnp.einsum("zgknCm,nN->knCzgNm", gw, eye2).reshape(4, LANE, 4, LANE)
        wbd = wbd.transpose(0, 2, 1, 3).reshape(16, LANE, LANE)
        gg = raw["group_g"][li]
        sink = raw["swa_sink"][li].reshape(SWA_KV_HEADS, SWA_GROUP, 1, 1)
        layers.append({
            "norm1_g": r1(raw["norm1_g"][li]), "norm2_g": r1(raw["norm2_g"][li]),
            "q_a_g": r1(raw["q_a_g"][li]), "kv_a_g": r1(raw["kv_a_g"][li]),
            "mla_q_g": jnp.pad(r1(raw["mla_q_g"][li]), ((0, 0), (0, LANE - MLA_QK))),
            "mla_k_g": jnp.pad(r1(raw["mla_k_g"][li]), ((0, 0), (0, LANE - MLA_QK))),
            "swa_q_g": jnp.pad(r1(raw["swa_q_g"][li]), ((0, 0), (0, LANE - SWA_HEAD_DIM))),
            "swa_k_g": jnp.pad(r1(raw["swa_k_g"][li]), ((0, 0), (0, LANE - SWA_HEAD_DIM))),
            "conv_w": [r1(raw["conv_w"][li][kk]) for kk in range(4)], "conv_b": r1(raw["conv_b"][li]),
            "wbd": wbd,
            "gate_b": [r1(raw["lru_gate_b"][li][z, g]) for z in range(2) for g in range(2)],
            "sp": [r1(jax.nn.softplus(-raw["lru_lambda"][li][z])) for z in range(2)],
            "sink_b": jnp.broadcast_to(sink, (SWA_KV_HEADS, SWA_GROUP, QB_SWA, LANE)).reshape(
                SWA_KV_HEADS, SWA_GROUP * QB_SWA, LANE),
            "g_a": _pad_heads(r1(gg[:GROUP_WIDTH]), MLA_HEADS, MLA_V), "g_b": r1(gg[GROUP_WIDTH:2 * GROUP_WIDTH]),
            "g_c": _pad_heads(r1(gg[2 * GROUP_WIDTH:]), SWA_HEADS, SWA_HEAD_DIM)})
    return layers


def _mesh_pos():
    return lax.axis_index("x"), lax.axis_index("y"), lax.axis_index("c")


def _peer(pos, k):
    return tuple(1 - p if (k >> s) & 1 else p for p, s in zip(pos, (2, 1, 0)))


def _dev_index(pos):
    return 4 * pos[0] + 2 * pos[1] + pos[2]


def _exchange(buf, gather, name):
    def body(x_ref, o_ref, send_sems, recv_sems, local_sem):
        pos = _mesh_pos()
        me = _dev_index(pos)
        src_mine = x_ref if gather else x_ref.at[me]
        local = pltpu.make_async_copy(src_mine, o_ref.at[me], local_sem)
        local.start()
        sends, recvs = [], []
        for k in range(1, N_DEV):
            peer = _peer(pos, k)
            pidx = _dev_index(peer)
            src = x_ref if gather else x_ref.at[pidx]
            sends.append(pltpu.make_async_remote_copy(
                src_ref=src, dst_ref=o_ref.at[me], send_sem=send_sems.at[k - 1], recv_sem=recv_sems.at[k - 1],
                device_id=peer, device_id_type=pl.DeviceIdType.MESH))
            recvs.append(pltpu.make_async_remote_copy(
                src_ref=src, dst_ref=o_ref.at[pidx], send_sem=send_sems.at[k - 1], recv_sem=recv_sems.at[k - 1],
                device_id=peer, device_id_type=pl.DeviceIdType.MESH))
        for cp in sends:
            cp.start()
        for cp in recvs:
            cp.wait_recv()
        for cp in sends:
            cp.wait_send()
        local.wait()

    shape = (N_DEV,) + tuple(buf.shape[-2:])
    return _pcall(
        body, name=name, out_shape=jax.ShapeDtypeStruct(shape, buf.dtype),
        in_specs=[pl.BlockSpec(memory_space=pl.ANY)], out_specs=pl.BlockSpec(memory_space=pl.ANY),
        scratch_shapes=[pltpu.SemaphoreType.DMA((N_DEV - 1,)), pltpu.SemaphoreType.DMA((N_DEV - 1,)),
                        pltpu.SemaphoreType.DMA],
    )(buf)


def _pack(arrs, dtype):
    flat = jnp.concatenate([a.reshape(-1).astype(dtype) for a in arrs])
    rows = -(-flat.size // PACK_W)
    rows = -(-rows // 16) * 16
    return jnp.pad(flat, (0, rows * PACK_W - flat.size)).reshape(rows, PACK_W)


def _unpack(buf, shapes, lead=()):
    flat = buf.reshape(lead + (-1,))
    out, off = [], 0
    for shp in shapes:
        n = math.prod(shp)
        out.append(flat[..., off:off + n].reshape(lead + tuple(shp)))
        off += n
    return out


def _sum_sources(buf, name):
    _, r, c = buf.shape
    tr = _rows_tile(r)

    def body(x_ref, o_ref):
        acc = x_ref[0]
        for d in range(1, N_DEV):
            acc = acc + x_ref[d]
        o_ref[...] = acc

    return _pcall(body, name=name, grid=(r // tr,),
                  in_specs=[pl.BlockSpec((N_DEV, tr, c), lambda i: (0, i, 0))],
                  out_specs=pl.BlockSpec((tr, c), lambda i: (i, 0)),
                  out_shape=jax.ShapeDtypeStruct((r, c), F32), compiler_params=_cparams(1))(buf)


def _rows_tile(r):
    best = r
    for t in range(SUBLANE, 257, SUBLANE):
        if r % t == 0:
            best = t
    return best


def _adamw(grads, wgt, m, v, name):
    n_src, r, c = grads.shape
    tr = _rows_tile(r)
    bc1 = 1.0 - ADAM_B1 ** ADAM_STEP
    bc2 = 1.0 - ADAM_B2 ** ADAM_STEP

    def body(g_ref, w_ref, m_ref, v_ref, go_ref, d_ref, mo_ref, vo_ref):
        g = g_ref[0].astype(F32)
        for d in range(1, n_src):
            g = g + g_ref[d].astype(F32)
        m_new = ADAM_B1 * m_ref[...] + (1.0 - ADAM_B1) * g
        v_new = ADAM_B2 * v_ref[...] + (1.0 - ADAM_B2) * (g * g)
        go_ref[...] = g
        mo_ref[...] = m_new
        vo_ref[...] = v_new
        d_ref[...] = -ADAM_LR * ((m_new / bc1) / (jnp.sqrt(v_new / bc2) + ADAM_EPS) + ADAM_WD * w_ref[...])

    spec = pl.BlockSpec((tr, c), lambda i: (i, 0))
    return _pcall(body, name=name, grid=(r // tr,),
                  in_specs=[pl.BlockSpec((n_src, tr, c), lambda i: (0, i, 0)), spec, spec, spec],
                  out_specs=[spec] * 4, out_shape=[jax.ShapeDtypeStruct((r, c), F32)] * 4,
                  compiler_params=_cparams(1))(grads, wgt, m, v)


def _silu(z):
    return z * jax.nn.sigmoid(z)


_WEIGHTS = ("c_ctx", "w_mod", "b_mod", "norm1_g", "w_in", "q_a_g", "w_uq", "kv_a_g", "w_ukv", "mla_q_g", "mla_k_g",
            "conv_w", "conv_b", "lru_gate_w", "lru_gate_b", "lru_lambda", "swa_q_g", "swa_k_g", "swa_sink", "group_g",
            "w_out", "norm2_g", "w_ff1", "w_ff2")
_SHARDED_SMALL = ("conv_w", "lru_gate_b", "lru_lambda")
_REPL_RAW = ("norm1_g", "q_a_g", "kv_a_g", "mla_q_g", "mla_k_g", "conv_b", "lru_gate_w", "swa_q_g", "swa_k_g",
             "swa_sink", "group_g", "norm2_g")
MOD_ROWS = 32


def kernel(x, c, ctx, c_ctx, w_mod, b_mod, norm1_g, w_in, q_a_g, w_uq, kv_a_g, w_ukv, mla_q_g, mla_k_g, conv_w, conv_b, lru_gate_w, lru_gate_b, lru_lambda, swa_q_g, swa_k_g, swa_sink, group_g, w_out, norm2_g, w_ff1, w_ff2, loss_target, m_c_ctx, m_w_mod, m_b_mod, m_norm1_g, m_w_in, m_q_a_g, m_w_uq, m_kv_a_g, m_w_ukv, m_mla_q_g, m_mla_k_g, m_conv_w, m_conv_b, m_lru_gate_w, m_lru_gate_b, m_lru_lambda, m_swa_q_g, m_swa_k_g, m_swa_sink, m_group_g, m_w_out, m_norm2_g, m_w_ff1, m_w_ff2, v_c_ctx, v_w_mod, v_b_mod, v_norm1_g, v_w_in, v_q_a_g, v_w_uq, v_kv_a_g, v_w_ukv, v_mla_q_g, v_mla_k_g, v_conv_w, v_conv_b, v_lru_gate_w, v_lru_gate_b, v_lru_lambda, v_swa_q_g, v_swa_k_g, v_swa_sink, v_group_g, v_w_out, v_norm2_g, v_w_ff1, v_w_ff2):
    wts = dict(c_ctx=c_ctx, w_mod=w_mod, b_mod=b_mod, norm1_g=norm1_g, w_in=w_in, q_a_g=q_a_g, w_uq=w_uq,
               kv_a_g=kv_a_g, w_ukv=w_ukv, mla_q_g=mla_q_g, mla_k_g=mla_k_g, conv_w=conv_w, conv_b=conv_b,
               lru_gate_w=lru_gate_w, lru_gate_b=lru_gate_b, lru_lambda=lru_lambda, swa_q_g=swa_q_g, swa_k_g=swa_k_g,
               swa_sink=swa_sink, group_g=group_g, w_out=w_out, norm2_g=norm2_g, w_ff1=w_ff1, w_ff2=w_ff2)
    mom1 = dict(zip(_WEIGHTS, (m_c_ctx, m_w_mod, m_b_mod, m_norm1_g, m_w_in, m_q_a_g, m_w_uq, m_kv_a_g, m_w_ukv,
                               m_mla_q_g, m_mla_k_g, m_conv_w, m_conv_b, m_lru_gate_w, m_lru_gate_b, m_lru_lambda,
                               m_swa_q_g, m_swa_k_g, m_swa_sink, m_group_g, m_w_out, m_norm2_g, m_w_ff1, m_w_ff2)))
    mom2 = dict(zip(_WEIGHTS, (v_c_ctx, v_w_mod, v_b_mod, v_norm1_g, v_w_in, v_q_a_g, v_w_uq, v_kv_a_g, v_w_ukv,
                               v_mla_q_g, v_mla_k_g, v_conv_w, v_conv_b, v_lru_gate_w, v_lru_gate_b, v_lru_lambda,
                               v_swa_q_g, v_swa_k_g, v_swa_sink, v_group_g, v_w_out, v_norm2_g, v_w_ff1, v_w_ff2)))
    bsz = x.shape[0]
    n_ex = bsz * N_DEV
    me = _dev_index(_mesh_pos())
    mod_cols = w_mod.shape[-1]

    small_shapes = [c.shape, conv_w.shape, lru_gate_b.shape, lru_lambda.shape]
    g_small = _exchange(_pack([c, conv_w, lru_gate_b, lru_lambda], F32), True, "ag_small")
    c_all, conv_w_all, gate_b_all, lam_all = _unpack(g_small, small_shapes, lead=(N_DEV,))
    c_all = c_all.reshape(n_ex, D_MODEL)
    cat_last = lambda a: jnp.moveaxis(a, 0, -2).reshape(a.shape[1:-1] + (N_DEV * a.shape[-1],))
    conv_w_full, gate_b_full, lam_full = cat_last(conv_w_all), cat_last(gate_b_all), cat_last(lam_all)

    act = jnp.zeros((MOD_ROWS, D_MODEL), F32).at[:n_ex].set(_silu(c_all)).at[n_ex].set(_silu(c_ctx))
    mod_part = jnp.concatenate([_mm(act, w_mod[li], "nn", F32, "mm_mod_l%d" % li) for li in range(DEPTH)], axis=1)
    mod_all = _exchange(mod_part, True, "ag_mod")
    mods = []
    for li in range(DEPTH):
        full = jnp.moveaxis(mod_all[:, :, li * mod_cols:(li + 1) * mod_cols], 0, 1).reshape(MOD_ROWS, -1) + b_mod[li]
        mine = lax.dynamic_slice_in_dim(full, me * bsz, bsz, axis=0)
        ctx_row = jnp.broadcast_to(full[n_ex], mine.shape)
        both = jnp.stack([ctx_row, mine], axis=1).reshape(bsz, 2, N_MOD, 1, D_MODEL)
        mods.append([both[:, :, j] for j in range(N_MOD)])

    big_local = _pack([wts[n] for n, _, _ in _BIG], BF16)
    big_all = _exchange(big_local, True, "ag_big")
    wp, big_vjp = jax.vjp(_prep_big, big_all)
    raw = {n: wts[n] for n in _REPL_RAW}
    raw.update(conv_w=conv_w_full, lru_gate_b=gate_b_full, lru_lambda=lam_full)
    sp, small_vjp = jax.vjp(_prep_small, raw)

    loss_part, grad_x, dmods, dw, ds = _local_step(x, ctx, loss_target, mods, wp, sp)

    (g_big,) = big_vjp([{k: v.astype(BF16) for k, v in d.items()} for d in dw])
    g_recv = _exchange(g_big, False, "rs_big")
    (d_raw,) = small_vjp(ds)

    dm_rows = []
    for li in range(DEPTH):
        dm = jnp.concatenate(dmods[li], axis=-1)
        dm_rows.append(jnp.concatenate([dm[:, 1, 0], jnp.sum(dm[:, 0, 0], axis=0, keepdims=True)], axis=0))
    dm_mine = jnp.concatenate(dm_rows, axis=1)
    dm_mine = jnp.pad(dm_mine, ((0, SUBLANE - bsz - 1), (0, 0)))
    dm_all = _exchange(dm_mine, True, "ag_dmod")
    g_wmod, g_bmod, dact_ctx = [], [], jnp.zeros((D_MODEL,), F32)
    for li in range(DEPTH):
        part = dm_all[:, :, li * N_MOD * D_MODEL:(li + 1) * N_MOD * D_MODEL]
        dm32 = jnp.zeros((MOD_ROWS, N_MOD * D_MODEL), F32).at[:n_ex].set(part[:, :bsz].reshape(n_ex, -1))
        dm32 = dm32.at[n_ex].set(jnp.sum(part[:, bsz], axis=0))
        g_bmod.append(jnp.sum(dm32, axis=0))
        cols = lax.dynamic_slice_in_dim(dm32, me * mod_cols, mod_cols, axis=1)
        g_wmod.append(_mm(act, cols, "tn", F32, "mm_mod_dw_l%d" % li))
        dact_ctx = dact_ctx + _mm(cols, w_mod[li], "nt", F32, "mm_mod_dx_l%d" % li)[n_ex]
    sg = jax.nn.sigmoid(c_ctx)
    g_cctx_part = dact_ctx * (sg * (1.0 + c_ctx * (1.0 - sg)))

    small_names = list(_REPL_RAW) + list(_SHARDED_SMALL)
    small_parts = [d_raw[n] for n in small_names] + [g_cctx_part, loss_part.reshape(1)]
    small_shapes = [a.shape for a in small_parts]
    small_sum = _sum_sources(_exchange(_pack(small_parts, F32), True, "ag_grads"), "sum_grads")
    small_tot = _unpack(small_sum, small_shapes)
    grads = dict(zip(small_names, small_tot[:len(small_names)]))
    grads["c_ctx"], loss = small_tot[-2], small_tot[-1][0]
    for n in _SHARDED_SMALL:
        width = wts[n].shape[-1]
        grads[n] = lax.dynamic_slice_in_dim(grads[n], me * width, width, axis=grads[n].ndim - 1)
    grads["b_mod"] = jnp.stack(g_bmod, axis=0)

    delta, new_m, new_v = {}, {}, {}
    recv_flat = g_recv.reshape(N_DEV, -1)
    off = 0
    for n, shp, _ in _BIG:
        cnt = math.prod(shp)
        two_d = (shp[0] * shp[1], shp[2])
        src = recv_flat[:, off:off + cnt].reshape((N_DEV,) + two_d)
        off += cnt
        res = _adamw(src, wts[n].reshape(two_d), mom1[n].reshape(two_d), mom2[n].reshape(two_d), "adamw_" + n)
        grads[n], delta[n], new_m[n], new_v[n] = [r.reshape(shp) for r in res]
    two_d = (DEPTH * D_MODEL, mod_cols)
    res = _adamw(jnp.stack(g_wmod, axis=0).reshape((1,) + two_d), w_mod.reshape(two_d), mom1["w_mod"].reshape(two_d),
                 mom2["w_mod"].reshape(two_d), "adamw_w_mod")
    grads["w_mod"], delta["w_mod"], new_m["w_mod"], new_v["w_mod"] = [r.reshape(w_mod.shape) for r in res]
    rest = [n for n in _WEIGHTS if n not in delta]
    shapes = [wts[n].shape for n in rest]
    res = _adamw(_pack([grads[n] for n in rest], F32)[None], _pack([wts[n] for n in rest], F32),
                 _pack([mom1[n] for n in rest], F32), _pack([mom2[n] for n in rest], F32), "adamw_small")
    for tgt, buf in zip((delta, new_m, new_v), res[1:]):
        tgt.update(zip(rest, _unpack(buf, shapes)))

    return (loss, grad_x, *[grads[n] for n in _WEIGHTS], *[delta[n] for n in _WEIGHTS],
            *[new_m[n] for n in _WEIGHTS], *[new_v[n] for n in _WEIGHTS])
```

```python
import functools
import math

import jax
import jax.numpy as jnp
from jax import lax
from jax.experimental import pallas as pl
from jax.experimental.pallas import tpu as pltpu

F32, BF16 = jnp.float32, jnp.bfloat16

N_DEV = 8
DEPTH = 2
D_MODEL = 1024
D_FF = 4096
N_MOD = 6
GRID_W = 64
WINDOW = 128
ROPE_THETA = 10000.0
EPS = 1e-6
NEG_INF = -1e30
LRU_C = 8.0
LRU_WIDTH = 512
MLA_HEADS, MLA_NOPE, MLA_ROPE, MLA_V = 8, 64, 32, 64
MLA_QK = MLA_NOPE + MLA_ROPE
MLA_Q_RANK, MLA_KV_RANK = 256, 128
SWA_HEADS, SWA_KV_HEADS, SWA_GROUP, SWA_HEAD_DIM = 8, 2, 4, 64
GROUP_WIDTH = 512
IN_SIZES = (256, 128, 32, 512, 512, 512, 128, 128)
IN_WIDTH = sum(IN_SIZES)
ADAM_LR, ADAM_B1, ADAM_B2, ADAM_EPS, ADAM_WD, ADAM_STEP = 0.001, 0.9, 0.999, 1e-08, 0.01, 10

LANE = 128
SUBLANE = 8
TB = 256
QB_SWA = 128
PACK_W = 1024
VMEM_LIMIT = 56 * 1024 * 1024
P_WIDTH = 3072
PC_SQ, PC_LX, PC_LG, PC_CQ, PC_SK, PC_SV, PC_CKV, PC_KR = 0, 1024, 1536, 2048, 2304, 2560, 2816, 2944
MIX_P = 2560


def _pcall(body, **kw):
    return pl.pallas_call(body, **kw)


def _cparams(n_grid):
    return pltpu.CompilerParams(dimension_semantics=("arbitrary",) * n_grid, vmem_limit_bytes=VMEM_LIMIT)


def _dg(a, b, ca, cb):
    return lax.dot_general(a.astype(BF16), b.astype(BF16), (((ca,), (cb,)), ((), ())),
                           preferred_element_type=F32)


@jax.custom_vjp
def _nn(a, b):
    return _dg(a, b, 1, 0)


@jax.custom_vjp
def _nt(a, b):
    return _dg(a, b, 1, 1)


@jax.custom_vjp
def _tn(a, b):
    return _dg(a, b, 0, 0)


_nn.defvjp(lambda a, b: (_nn(a, b), (a, b)), lambda r, ct: (_nt(ct, r[1]), _tn(r[0], ct)))
_nt.defvjp(lambda a, b: (_nt(a, b), (a, b)), lambda r, ct: (_nn(ct, r[1]), _tn(ct, r[0])))
_tn.defvjp(lambda a, b: (_tn(a, b), (a, b)), lambda r, ct: (_nt(r[1], ct), _nn(r[0], ct)))


@functools.partial(jax.custom_vjp, nondiff_argnums=(1, 2))
def _roll(x, shift, axis):
    return pltpu.roll(x, shift % x.shape[axis], axis)


_roll.defvjp(lambda x, shift, axis: (_roll(x, shift, axis), None),
             lambda shift, axis, _, ct: (_roll(ct, -shift, axis),))


@functools.partial(jax.custom_vjp, nondiff_argnums=(1, 2))
def _split(x, n, axis):
    w = x.shape[axis] // n
    return tuple(lax.slice_in_dim(x, i * w, (i + 1) * w, axis=axis) for i in range(n))


_split.defvjp(lambda x, n, axis: (_split(x, n, axis), None),
              lambda n, axis, _, cts: (jnp.concatenate(cts, axis=axis),))


@jax.custom_vjp
def _unstack(x):
    return tuple(x[i] for i in range(x.shape[0]))


_unstack.defvjp(lambda x: (_unstack(x), None), lambda _, cts: (jnp.stack(cts, axis=0),))


def _sig(x):
    return 0.5 * (jnp.tanh(0.5 * x) + 1.0)


def _gelu(x):
    return 0.5 * x * (1.0 + jnp.tanh(math.sqrt(2.0 / math.pi) * (x + 0.044715 * (x * x * x))))


def _rms(x, g, n):
    ms = jnp.sum(x * x, axis=-1, keepdims=True) * (1.0 / n)
    return x * lax.rsqrt(ms + EPS) * g


def _rope(y, cos, sa, sb, quarter):
    return y * cos + _roll(y, -quarter, 1) * sa + _roll(y, quarter, 1) * sb


def _softmax_rows(s, extra=None):
    m = jnp.max(s, axis=-1, keepdims=True)
    if extra is not None:
        m = jnp.maximum(m, extra)
    m = lax.stop_gradient(m)
    e = jnp.exp(s - m)
    den = jnp.sum(e, axis=-1, keepdims=True)
    if extra is not None:
        den = den + jnp.exp(extra - m)
    return e / den


class _A:
    def __init__(self, arr, block, imap, kind="row", first=None, gdtype=F32, gshape=None, gimap=None):
        self.arr, self.block, self.imap, self.kind, self.first = arr, block, imap, kind, first
        self.gdtype, self.gshape, self.gimap = gdtype, gshape, gimap


def _all_zero(*ids):
    return functools.reduce(jnp.logical_and, [i == 0 for i in ids])


def _par(arr):
    nd = arr.ndim
    return _A(arr, arr.shape, lambda *ids: (0,) * nd, "acc", first=_all_zero)


def _op_fwd(name, fn, grid, args, outs):
    n_in = len(args)

    def body(*refs):
        vals = [r[...].astype(F32) for r in refs[:n_in]]
        for r, v in zip(refs[n_in:], fn(*vals)):
            r[...] = v.astype(r.dtype)

    return _pcall(
        body, name=name, grid=grid,
        in_specs=[pl.BlockSpec(a.block, a.imap) for a in args],
        out_specs=[pl.BlockSpec(o[2], o[3]) for o in outs],
        out_shape=[jax.ShapeDtypeStruct(o[0], o[1]) for o in outs],
        compiler_params=_cparams(len(grid)),
    )(*[a.arr for a in args])


def _op_bwd(name, fn, grid, args, outs, ct_arrays):
    n_in, n_ct = len(args), len(outs)
    didx = [i for i, a in enumerate(args) if a.kind != "const"]

    def body(*refs):
        ids = [pl.program_id(i) for i in range(len(grid))]
        vals = [r[...].astype(F32) for r in refs[:n_in]]

        def g(*dv):
            full = list(vals)
            for i, v in zip(didx, dv):
                full[i] = v
            return tuple(fn(*full))

        _, vjp = jax.vjp(g, *[vals[i] for i in didx])
        grads = vjp(tuple(r[...].astype(F32) for r in refs[n_in:n_in + n_ct]))
        for gr, i, r in zip(grads, didx, refs[n_in + n_ct:]):
            a = args[i]
            if a.kind == "row":
                r[...] = gr.astype(r.dtype)
            else:
                first = a.first(*ids)

                @pl.when(first)
                def _():
                    r[...] = gr

                @pl.when(jnp.logical_not(first))
                def _():
                    r[...] += gr

    g_specs, g_shapes = [], []
    for i in didx:
        a = args[i]
        if a.kind == "row":
            g_specs.append(pl.BlockSpec(a.block, a.gimap or a.imap))
            g_shapes.append(jax.ShapeDtypeStruct(a.gshape or a.arr.shape, a.gdtype))
        else:
            g_specs.append(pl.BlockSpec(a.block, a.imap))
            g_shapes.append(jax.ShapeDtypeStruct(a.arr.shape, F32))
    return _pcall(
        body, name=name, grid=grid,
        in_specs=[pl.BlockSpec(a.block, a.imap) for a in args] + [pl.BlockSpec(o[2], o[3]) for o in outs],
        out_specs=g_specs, out_shape=g_shapes,
        compiler_params=_cparams(len(grid)),
    )(*[a.arr for a in args], *ct_arrays)


def _rowop(name, fn, grid, args, outs):
    res = _op_fwd(name, fn, grid, args, outs)
    return res, lambda *cts: _op_bwd(name + "_bwd", fn, grid, args, outs, cts)


def _pick(n, cap):
    best = None
    for t in range(LANE, cap + 1, LANE):
        if n % t == 0:
            best = t
    return best or n


def _mm(a, b, mode, out_dtype, name, epi=None, aux=None):
    if mode == "nn":
        (m, k), n = a.shape, b.shape[1]
    elif mode == "nt":
        (m, k), n = a.shape, b.shape[0]
    else:
        (k, m), n = a.shape, b.shape[1]
    tm = 512 if m % 512 == 0 else m
    tn, tk = _pick(n, 1024), _pick(k, 1280)
    nk = k // tk
    if mode == "tn":
        a_spec = pl.BlockSpec((tk, tm), lambda i, j, kk: (kk, i))
    else:
        a_spec = pl.BlockSpec((tm, tk), lambda i, j, kk: (i, kk))
    if mode == "nt":
        b_spec = pl.BlockSpec((tn, tk), lambda i, j, kk: (j, kk))
    else:
        b_spec = pl.BlockSpec((tk, tn), lambda i, j, kk: (kk, j))
    dims = {"nn": (1, 0), "nt": (1, 1), "tn": (0, 0)}[mode]
    o_spec = pl.BlockSpec((tm, tn), lambda i, j, kk: (i, j))
    n_aux = 0 if aux is None else 1
    n_out = 2 if epi == "sqrelu" else 1

    def body(*refs):
        a_ref, b_ref = refs[0], refs[1]
        o_refs = refs[2 + n_aux:2 + n_aux + n_out]
        acc = refs[-1]
        kk = pl.program_id(2)
        part = _dg(a_ref[...], b_ref[...], *dims)

        if nk > 1:
            @pl.when(kk == 0)
            def _():
                acc[...] = part

            @pl.when((kk > 0) & (kk < nk - 1))
            def _():
                acc[...] += part

        @pl.when(kk == nk - 1)
        def _():
            r = part if nk == 1 else acc[...] + part
            if epi == "sqrelu":
                o_refs[0][...] = r.astype(o_refs[0].dtype)
                rl = jnp.maximum(r, 0.0)
                o_refs[1][...] = (rl * rl).astype(o_refs[1].dtype)
            elif epi == "dsqrelu":
                pre = refs[2][...].astype(F32)
                o_refs[0][...] = (r * (2.0 * jnp.maximum(pre, 0.0))).astype(o_refs[0].dtype)
            else:
                o_refs[0][...] = r.astype(o_refs[0].dtype)

    res = _pcall(
        body, name=name, grid=(m // tm, n // tn, nk),
        in_specs=[a_spec, b_spec] + [o_spec] * n_aux,
        out_specs=[o_spec] * n_out,
        out_shape=[jax.ShapeDtypeStruct((m, n), out_dtype)] * n_out,
        scratch_shapes=[pltpu.VMEM((tm, tn), F32)],
        compiler_params=_cparams(3),
    )(a, b, *([aux] if aux is not None else []))
    return res if n_out == 2 else res[0]


def _mla_block(q, k, v):
    p = _softmax_rows(_nt(q, k) * (MLA_QK ** -0.5))
    return _nn(p, v)


def _mla_attn(q, k, v, tc, ctx_q, name):
    bsz, t_all, _ = q.shape
    n_t = t_all // TB
    grid = (bsz, MLA_HEADS, n_t)
    q_spec = pl.BlockSpec((None, TB, LANE), lambda b, h, t: (b, t, h))
    kv_spec = pl.BlockSpec((None, t_all, LANE), lambda b, h, t: (b, 0, h))

    def fwd_body(q_ref, k_ref, v_ref, o_ref):
        t = pl.program_id(2)

        @pl.when(t == 0)
        def _():
            if ctx_q:
                o_ref[...] = _mla_block(q_ref[...], k_ref[0:tc, :], v_ref[0:tc, :])
            else:
                o_ref[...] = jnp.zeros_like(o_ref)

        @pl.when(t > 0)
        def _():
            o_ref[...] = _mla_block(q_ref[...], k_ref[...], v_ref[...])

    o = _pcall(fwd_body, name=name, grid=grid, in_specs=[q_spec, kv_spec, kv_spec], out_specs=q_spec,
               out_shape=jax.ShapeDtypeStruct(q.shape, F32), compiler_params=_cparams(3))(q, k, v)

    def bwd(do):
        def bwd_body(q_ref, k_ref, v_ref, do_ref, dq_ref, dk_ref, dv_ref):
            t = pl.program_id(2)

            @pl.when(t == 0)
            def _():
                dk_ref[...] = jnp.zeros_like(dk_ref)
                dv_ref[...] = jnp.zeros_like(dv_ref)
                if ctx_q:
                    _, vjp = jax.vjp(_mla_block, q_ref[...].astype(F32), k_ref[0:tc, :].astype(F32),
                                     v_ref[0:tc, :].astype(F32))
                    dq, dk, dv = vjp(do_ref[...])
                    dq_ref[...] = dq
                    dk_ref[0:tc, :] = dk
                    dv_ref[0:tc, :] = dv
                else:
                    dq_ref[...] = jnp.zeros_like(dq_ref)

            @pl.when(t > 0)
            def _():
                _, vjp = jax.vjp(_mla_block, q_ref[...].astype(F32), k_ref[...].astype(F32), v_ref[...].astype(F32))
                dq, dk, dv = vjp(do_ref[...])
                dq_ref[...] = dq
                dk_ref[...] += dk
                dv_ref[...] += dv

        return _pcall(bwd_body, name=name + "_bwd", grid=grid, in_specs=[q_spec, kv_spec, kv_spec, q_spec],
                      out_specs=[q_spec, kv_spec, kv_spec],
                      out_shape=[jax.ShapeDtypeStruct(q.shape, F32)] * 3, compiler_params=_cparams(3))(q, k, v, do)

    return o, bwd


def _swa_ctx_block(q, kc, vc, sink):
    sk = jnp.sum(sink, axis=-1, keepdims=True) * (1.0 / LANE)
    sks = _split(sk, SWA_GROUP, 0)
    outs = []
    for qh, s_h in zip(_split(q, SWA_GROUP, 1), sks):
        p = _softmax_rows(_nt(qh, kc) * (SWA_HEAD_DIM ** -0.5), s_h)
        outs.append(_nn(p, vc))
    return jnp.concatenate(outs, axis=1)


def _swa_win_block(q, kc, kw, vc, vw, sink, mask):
    keys = jnp.concatenate([kc, kw], axis=0)
    vals = jnp.concatenate([vc, vw], axis=0)
    sk = jnp.sum(sink, axis=-1, keepdims=True) * (1.0 / LANE)
    sks = _split(sk, SWA_GROUP, 0)
    outs = []
    for qh, s_h in zip(_split(q, SWA_GROUP, 1), sks):
        s = jnp.where(mask, _nt(qh, keys) * (SWA_HEAD_DIM ** -0.5), NEG_INF)
        outs.append(_nn(_softmax_rows(s, s_h), vals))
    return jnp.concatenate(outs, axis=1)


def _swa_attn(q, k, p_all, sink_b, tc, ctx_q, name):
    bsz, t_all, _ = q.shape
    n_q = t_all // QB_SWA
    n_cq = tc // QB_SWA
    lat = t_all - tc
    span = QB_SWA + 2 * WINDOW
    gw = SWA_GROUP * LANE
    grid = (bsz, SWA_KV_HEADS, n_q)
    q_spec = pl.BlockSpec((None, QB_SWA, gw), lambda b, g, i: (b, i, g))
    k_spec = pl.BlockSpec((None, t_all, LANE), lambda b, g, i: (b, 0, g))
    v_spec = pl.BlockSpec((None, t_all, LANE), lambda b, g, i: (b, 0, PC_SV // LANE + g))
    s_spec = pl.BlockSpec((None, SWA_GROUP * QB_SWA, LANE), lambda b, g, i: (g, 0, 0))

    def window(i):
        q0 = (i - n_cq) * QB_SWA
        w0 = jnp.clip(q0 - WINDOW, 0, lat - span)
        w0 = pl.multiple_of(w0, QB_SWA)
        qi = q0 + lax.broadcasted_iota(jnp.int32, (QB_SWA, tc + span), 0)
        col = lax.broadcasted_iota(jnp.int32, (QB_SWA, tc + span), 1)
        kj = w0 + col - tc
        mask = (col < tc) | ((kj >= qi - WINDOW) & (kj <= qi + WINDOW))
        return w0, mask

    def fwd_body(q_ref, k_ref, v_ref, s_ref, o_ref):
        i = pl.program_id(2)

        @pl.when(i < n_cq)
        def _():
            if ctx_q:
                o_ref[...] = _swa_ctx_block(q_ref[...].astype(F32), k_ref[0:tc, :], v_ref[0:tc, :], s_ref[...])
            else:
                o_ref[...] = jnp.zeros_like(o_ref)

        @pl.when(i >= n_cq)
        def _():
            w0, mask = window(i)
            o_ref[...] = _swa_win_block(q_ref[...].astype(F32), k_ref[0:tc, :], k_ref[pl.ds(tc + w0, span), :],
                                        v_ref[0:tc, :], v_ref[pl.ds(tc + w0, span), :], s_ref[...], mask)

    o = _pcall(fwd_body, name=name, grid=grid, in_specs=[q_spec, k_spec, v_spec, s_spec], out_specs=q_spec,
               out_shape=jax.ShapeDtypeStruct(q.shape, F32), compiler_params=_cparams(3))(q, k, p_all, sink_b)

    def bwd(do):
        def bwd_body(q_ref, k_ref, v_ref, s_ref, do_ref, dq_ref, dk_ref, dv_ref, ds_ref):
            i = pl.program_id(2)

            @pl.when(i == 0)
            def _():
                dk_ref[...] = jnp.zeros_like(dk_ref)
                dv_ref[...] = jnp.zeros_like(dv_ref)
                ds_ref[...] = jnp.zeros_like(ds_ref)

            @pl.when(i < n_cq)
            def _():
                if ctx_q:
                    _, vjp = jax.vjp(_swa_ctx_block, q_ref[...].astype(F32), k_ref[0:tc, :].astype(F32),
                                     v_ref[0:tc, :], s_ref[...])
                    dq, dk, dv, ds = vjp(do_ref[...])
                    dq_ref[...] = dq
                    dk_ref[0:tc, :] += dk
                    dv_ref[0:tc, :] += dv
                    ds_ref[...] += ds
                else:
                    dq_ref[...] = jnp.zeros_like(dq_ref)

            @pl.when(i >= n_cq)
            def _():
                w0, mask = window(i)
                win = pl.ds(tc + w0, span)
                _, vjp = jax.vjp(functools.partial(_swa_win_block, mask=mask), q_ref[...].astype(F32),
                                 k_ref[0:tc, :].astype(F32), k_ref[win, :].astype(F32),
                                 v_ref[0:tc, :], v_ref[win, :], s_ref[...])
                dq, dkc, dkw, dvc, dvw, ds = vjp(do_ref[...])
                dq_ref[...] = dq
                dk_ref[0:tc, :] += dkc
                dk_ref[win, :] += dkw
                dv_ref[0:tc, :] += dvc
                dv_ref[win, :] += dvw
                ds_ref[...] += ds

        kv_out = pl.BlockSpec((None, t_all, LANE), lambda b, g, i: (b, 0, g))
        ds_spec = pl.BlockSpec((None, None, SWA_GROUP * QB_SWA, LANE), lambda b, g, i: (b, g, 0, 0))
        kv_shape = jax.ShapeDtypeStruct((bsz, t_all, SWA_KV_HEADS * LANE), F32)
        return _pcall(
            bwd_body, name=name + "_bwd", grid=grid, in_specs=[q_spec, k_spec, v_spec, s_spec, q_spec],
            out_specs=[q_spec, kv_out, kv_out, ds_spec],
            out_shape=[jax.ShapeDtypeStruct(q.shape, F32), kv_shape, kv_shape,
                       jax.ShapeDtypeStruct((bsz,) + sink_b.shape, F32)],
            compiler_params=_cparams(3))(q, k, p_all, sink_b, do)

    return o, bwd


def _scan_rows(a, u, reverse, a_s, u_s, c_s):
    t_all, c = a.shape
    row8 = lax.broadcasted_iota(jnp.int32, a.shape, 0) % SUBLANE
    for d in (1, 2, 4):
        sh = d if not reverse else t_all - d
        ar, ur = pltpu.roll(a, sh, 0), pltpu.roll(u, sh, 0)
        m = (row8 >= d) if not reverse else (row8 < SUBLANE - d)
        u = jnp.where(m, a * ur + u, u)
        a = jnp.where(m, a * ar, a)
    a_s[...] = a
    u_s[...] = u
    n_tiles = t_all // SUBLANE

    def step(j, carry):
        tile = j if not reverse else n_tiles - 1 - j
        base = pl.multiple_of(tile * SUBLANE, SUBLANE)
        c_s[pl.ds(base, SUBLANE), :] = jnp.broadcast_to(carry, (SUBLANE, c))
        last = base + (0 if reverse else SUBLANE - 1)
        return a_s[pl.ds(last, 1), :] * carry + u_s[pl.ds(last, 1), :]

    lax.fori_loop(0, n_tiles, step, jnp.zeros((1, c), F32))
    return a_s[...] * c_s[...] + u_s[...]


def _shift_rows(x, reverse_src):
    t_all = x.shape[0]
    row = lax.broadcasted_iota(jnp.int32, x.shape, 0)
    if reverse_src:
        return jnp.where(row == t_all - 1, 0.0, pltpu.roll(x, t_all - 1, 0))
    return jnp.where(row == 0, 0.0, pltpu.roll(x, 1, 0))


def _lru_scan(a0, u0, a1, u1, name):
    bsz, t_all, w = a0.shape
    grid = (bsz, w // LANE)
    spec = pl.BlockSpec((None, t_all, LANE), lambda b, c: (b, 0, c))
    scratch = [pltpu.VMEM((t_all, LANE), F32)] * 3
    shape = jax.ShapeDtypeStruct(a0.shape, F32)

    def fwd_body(a0_ref, u0_ref, a1_ref, u1_ref, h0_ref, h1_ref, a_s, u_s, c_s):
        h0_ref[...] = _scan_rows(a0_ref[...], u0_ref[...], False, a_s, u_s, c_s)
        h1_ref[...] = _scan_rows(a1_ref[...], u1_ref[...], True, a_s, u_s, c_s)

    h0, h1 = _pcall(fwd_body, name=name, grid=grid, in_specs=[spec] * 4, out_specs=[spec] * 2,
                    out_shape=[shape] * 2, scratch_shapes=scratch, compiler_params=_cparams(2))(a0, u0, a1, u1)

    def bwd(dh0, dh1):
        def bwd_body(a0_ref, h0_ref, g0_ref, a1_ref, h1_ref, g1_ref, da0_ref, du0_ref, da1_ref, du1_ref,
                     a_s, u_s, c_s):
            g0 = _scan_rows(_shift_rows(a0_ref[...], True), g0_ref[...], True, a_s, u_s, c_s)
            du0_ref[...] = g0
            da0_ref[...] = g0 * _shift_rows(h0_ref[...], False)
            g1 = _scan_rows(_shift_rows(a1_ref[...], False), g1_ref[...], False, a_s, u_s, c_s)
            du1_ref[...] = g1
            da1_ref[...] = g1 * _shift_rows(h1_ref[...], True)

        return _pcall(bwd_body, name=name + "_bwd", grid=grid, in_specs=[spec] * 6, out_specs=[spec] * 4,
                      out_shape=[shape] * 4, scratch_shapes=scratch,
                      compiler_params=_cparams(2))(a0, h0, dh0, a1, h1, dh1)

    return h0, h1, bwd


def _f_mod(x, g, shift, scale):
    return (_rms(x, g, D_MODEL) * (1.0 + scale) + shift,)


def _f_mla_q(cq, ga, w, gh, cos, sa, sb):
    n = _rms(cq, ga, MLA_Q_RANK)
    outs = []
    for wh in _split(w, MLA_HEADS, 1):
        outs.append(_rope(_rms(_nn(n, wh), gh, MLA_QK), cos, sa, sb, MLA_ROPE // 4))
    return (jnp.concatenate(outs, axis=1),)


def _f_mla_kv(ckv, krp, ga, wk, wv, gh, cos, sa, sb):
    n = _rms(ckv, ga, MLA_KV_RANK)
    outs = []
    for wh in _split(wk, MLA_HEADS, 1):
        outs.append(_rope(_rms(_nn(n, wh) + krp, gh, MLA_QK), cos, sa, sb, MLA_ROPE // 4))
    return jnp.concatenate(outs, axis=1), _nn(n, wv)


def _f_conv(x, w0, w1, w2, w3, bias, tc):
    t_all = x.shape[0]
    row = lax.broadcasted_iota(jnp.int32, x.shape, 0)
    lo = jnp.where(row < tc, 0, tc)
    hi = jnp.where(row < tc, tc, t_all)
    y = bias + jnp.zeros_like(x)
    for kk, wk in enumerate((w0, w1, w2, w3)):
        src = row + (kk - 2)
        xs = x if kk == 2 else _roll(x, 2 - kk, 0)
        y = y + wk * jnp.where((src >= lo) & (src < hi), xs, 0.0)
    return (y,)


def _f_gates(xc, w16, b00, b01, b10, b11, sp0, sp1):
    ws = _unstack(w16)
    n_cb = LRU_WIDTH // LANE
    xcs = _split(xc, n_cb, 1)
    bias = [_split(b, n_cb, 1) for b in (b00, b01, b10, b11)]
    sps = [_split(s, n_cb, 1) for s in (sp0, sp1)]
    res = [[], [], [], []]
    for c in range(n_cb):
        for z in range(2):
            r = _sig(_nn(xcs[c], ws[c * 4 + 2 * z]) + bias[2 * z][c])
            i = _sig(_nn(xcs[c], ws[c * 4 + 2 * z + 1]) + bias[2 * z + 1][c])
            la = -LRU_C * r * sps[z][c]
            res[2 * z].append(jnp.exp(la))
            res[2 * z + 1].append(jnp.sqrt(-jnp.tanh(la) * (jnp.exp(2.0 * la) + 1.0)) * (i * xcs[c]))
    return tuple(jnp.concatenate(r, axis=1) for r in res)


def _f_lru_out(h0, h1, lg):
    return ((h0 + h1) * _gelu(lg),)


def _f_swa_qk(sq, sk, gq, gk, cos, sa, sb):
    qs = [_rope(_rms(x, gq, SWA_HEAD_DIM), cos, sa, sb, SWA_HEAD_DIM // 4) for x in _split(sq, SWA_HEADS, 1)]
    ks = [_rope(_rms(x, gk, SWA_HEAD_DIM), cos, sa, sb, SWA_HEAD_DIM // 4) for x in _split(sk, SWA_KV_HEADS, 1)]
    return jnp.concatenate(qs, axis=1), jnp.concatenate(ks, axis=1)


def _f_merge(oa, ob, oc, ga, gb, gc):
    return (jnp.concatenate([_rms(oa, ga, GROUP_WIDTH), _rms(ob, gb, GROUP_WIDTH), _rms(oc, gc, GROUP_WIDTH)],
                            axis=1),)


def _f_resid_mod(x, y, gate, g, shift, scale):
    x1 = x + gate * y
    return x1, _rms(x1, g, D_MODEL) * (1.0 + scale) + shift


def _f_resid(x, y, gate):
    return (x + gate * y,)


def _layer(li, x, mods, w, s, tabs, tc, ctx_q):
    bsz, t_all, _ = x.shape
    n_t = t_all // TB
    grid = (bsz, n_t)
    rows = lambda b, t: (b, t, 0)

    def row(arr, width=None, idx=0, gdtype=F32, gshape=None):
        width = width or arr.shape[-1]
        return _A(arr, (None, TB, width), lambda b, t: (b, t, idx), "row", gdtype=gdtype, gshape=gshape,
                  gimap=rows if gshape is not None else None)

    def out(width, dtype, imap=rows):
        return ((bsz, t_all, width), dtype, (None, TB, width), imap)

    def modarg(arr):
        return _A(arr, (None, None, 1, D_MODEL), lambda b, t: (b, jnp.minimum(t, 1), 0, 0), "acc",
                  first=lambda b, t: t <= 1)

    def tab(arr):
        return _A(arr, (TB, LANE), lambda b, t: (t, 0), "const")

    def pcol(p_all, col, width):
        return row(p_all, width, col // width, gdtype=BF16, gshape=(bsz, t_all, width))

    nm = lambda base: "%s_l%d" % (base, li)
    sh1, sc1, g1, sh2, sc2, g2 = mods
    m_all = bsz * t_all

    (h,), b_mod1 = _rowop(nm("mod1"), _f_mod, grid, [row(x), _par(s["norm1_g"]), modarg(sh1), modarg(sc1)],
                          [out(D_MODEL, BF16)])
    p_all = _mm(h.reshape(m_all, D_MODEL), w["win"], "nn", F32, nm("mm_in")).reshape(bsz, t_all, P_WIDTH)

    tq, tk_ = tabs["mla"], tabs["mla"]
    (q_a,), b_mq = _rowop(nm("mla_q"), _f_mla_q, grid,
                          [pcol(p_all, PC_CQ, 256), _par(s["q_a_g"]), _par(w["wuq"]), _par(s["mla_q_g"])]
                          + [tab(a) for a in tq], [out(MLA_HEADS * LANE, BF16)])
    (k_a, v_a), b_mkv = _rowop(nm("mla_kv"), _f_mla_kv, grid,
                               [pcol(p_all, PC_CKV, 128), pcol(p_all, PC_KR, 128), _par(s["kv_a_g"]), _par(w["wk"]),
                                _par(w["wv"]), _par(s["mla_k_g"])] + [tab(a) for a in tk_],
                               [out(MLA_HEADS * LANE, BF16), out(MLA_HEADS * LANE, BF16)])
    o_a, b_attn_a = _mla_attn(q_a, k_a, v_a, tc, ctx_q, nm("mla_attn"))

    n_cb = LRU_WIDTH // LANE
    conv_grid = (n_cb, bsz)
    cpar = lambda arr: _A(arr, (1, LANE), lambda c, b: (0, c), "acc", first=lambda c, b: b == 0)
    conv_args = [_A(p_all, (None, t_all, LANE), lambda c, b: (b, 0, PC_LX // LANE + c), "row", gdtype=BF16,
                    gshape=(bsz, t_all, LRU_WIDTH), gimap=lambda c, b: (b, 0, c))]
    conv_args += [cpar(a) for a in s["conv_w"]] + [cpar(s["conv_b"])]
    conv_out = [((bsz, t_all, LRU_WIDTH), F32, (None, t_all, LANE), lambda c, b: (b, 0, c))]
    (xc,), b_conv = _rowop(nm("lru_conv"), functools.partial(_f_conv, tc=tc), conv_grid, conv_args, conv_out)
    rot = lambda b, t: (b, (t + n_t - 1) % n_t, 0)
    (a0, u0, a1, u1), b_gates = _rowop(
        nm("lru_gates"), _f_gates, grid,
        [row(xc), _par(s["wbd"])] + [_par(a) for a in s["gate_b"]] + [_par(a) for a in s["sp"]],
        [out(LRU_WIDTH, F32), out(LRU_WIDTH, F32), out(LRU_WIDTH, F32, rot), out(LRU_WIDTH, F32, rot)])
    h0, h1, b_scan = _lru_scan(a0, u0, a1, u1, nm("lru_scan"))
    h1_arg = _A(h1, (None, TB, LRU_WIDTH), rot, "row")
    (o_b,), b_lout = _rowop(nm("lru_out"), _f_lru_out, grid, [row(h0), h1_arg, pcol(p_all, PC_LG, 512)],
                            [out(LRU_WIDTH, F32)])

    ts = tabs["swa"]
    (q_c, k_c), b_sqk = _rowop(nm("swa_qk"), _f_swa_qk, grid,
                               [pcol(p_all, PC_SQ, 1024), pcol(p_all, PC_SK, 256), _par(s["swa_q_g"]),
                                _par(s["swa_k_g"])] + [tab(a) for a in ts],
                               [out(SWA_HEADS * LANE, BF16), out(SWA_KV_HEADS * LANE, BF16)])
    o_c, b_attn_c = _swa_attn(q_c, k_c, p_all, s["sink_b"], tc, ctx_q, nm("swa_attn"))

    (y_in,), b_merge = _rowop(nm("merge"), _f_merge, grid,
                              [row(o_a), row(o_b), row(o_c), _par(s["g_a"]), _par(s["g_b"]), _par(s["g_c"])],
                              [out(MIX_P, BF16)])
    y = _mm(y_in.reshape(m_all, MIX_P), w["wout"], "nn", F32, nm("mm_out")).reshape(bsz, t_all, D_MODEL)
    (x1, hm), b_rm = _rowop(nm("resid_mod"), _f_resid_mod, grid,
                            [row(x), row(y, gdtype=BF16), modarg(g1), _par(s["norm2_g"]), modarg(sh2), modarg(sc2)],
                            [out(D_MODEL, F32), out(D_MODEL, BF16)])
    pre, act = _mm(hm.reshape(m_all, D_MODEL), w["ff1"], "nn", BF16, nm("mm_ff1"), epi="sqrelu")
    y2 = _mm(act, w["ff2"], "nn", F32, nm("mm_ff2")).reshape(bsz, t_all, D_MODEL)
    (x2,), b_res = _rowop(nm("resid"), _f_resid, grid, [row(x1), row(y2, gdtype=BF16), modarg(g2)],
                          [out(D_MODEL, F32)])

    def bwd(dx2):
        dw, ds = {}, {}
        dx1a, dy2, dg2 = b_res(dx2)
        dy2 = dy2.reshape(m_all, D_MODEL)
        dpre = _mm(dy2, w["ff2"], "nt", BF16, nm("mm_ff2_dx"), epi="dsqrelu", aux=pre)
        dw["ff2"] = _mm(act, dy2, "tn", F32, nm("mm_ff2_dw"))
        dhm = _mm(dpre, w["ff1"], "nt", F32, nm("mm_ff1_dx")).reshape(bsz, t_all, D_MODEL)
        dw["ff1"] = _mm(hm.reshape(m_all, D_MODEL), dpre, "tn", F32, nm("mm_ff1_dw"))
        dxa, dy, dg1, ds["norm2_g"], dsh2, dsc2 = b_rm(dx1a, dhm)
        dy = dy.reshape(m_all, D_MODEL)
        dy_in = _mm(dy, w["wout"], "nt", F32, nm("mm_out_dx")).reshape(bsz, t_all, MIX_P)
        dw["wout"] = _mm(y_in.reshape(m_all, MIX_P), dy, "tn", F32, nm("mm_out_dw"))
        do_a, do_b, do_c, ds["g_a"], ds["g_b"], ds["g_c"] = b_merge(dy_in)

        dq_c, dk_c, dsv, dsink = b_attn_c(do_c)
        ds["sink_b"] = jnp.sum(dsink, axis=0)
        dsq, dsk, ds["swa_q_g"], ds["swa_k_g"] = b_sqk(dq_c, dk_c)

        dh0, dh1, dlg = b_lout(do_b)
        da0, du0, da1, du1 = b_scan(dh0, dh1)
        gates_g = b_gates(da0, du0, da1, du1)
        dxc, ds["wbd"] = gates_g[0], gates_g[1]
        ds["gate_b"], ds["sp"] = list(gates_g[2:6]), list(gates_g[6:8])
        conv_g = b_conv(dxc)
        dlx, ds["conv_w"], ds["conv_b"] = conv_g[0], list(conv_g[1:5]), conv_g[5]

        dq_a, dk_a, dv_a = b_attn_a(do_a)
        dcq, ds["q_a_g"], dw["wuq"], ds["mla_q_g"] = b_mq(dq_a)
        dckv, dkr, ds["kv_a_g"], dw["wk"], dw["wv"], ds["mla_k_g"] = b_mkv(dk_a, dv_a)

        dp = jnp.concatenate([dsq, dlx, dlg, dcq, dsk, dsv.astype(BF16), dckv, dkr], axis=-1)
        dp = dp.reshape(m_all, P_WIDTH)
        dh = _mm(dp, w["win"], "nt", F32, nm("mm_in_dx")).reshape(bsz, t_all, D_MODEL)
        dw["win"] = _mm(h.reshape(m_all, D_MODEL), dp, "tn", F32, nm("mm_in_dw"))
        dxb, ds["norm1_g"], dsh1, dsc1 = b_mod1(dh)
        return dxa + dxb, [dsh1, dsc1, dg1, dsh2, dsc2, dg2], dw, ds

    return x2, bwd


def _loss_and_grad(x2, target, tc):
    bsz, t_all, d = x2.shape
    n_t = t_all // TB
    n_c = tc // TB

    def body(x_ref, t_ref, l_ref, dx_ref):
        b, t = pl.program_id(0), pl.program_id(1)

        @pl.when((b == 0) & (t == 0))
        def _():
            l_ref[...] = jnp.zeros_like(l_ref)

        @pl.when(t < n_c)
        def _():
            dx_ref[...] = jnp.zeros_like(dx_ref)

        @pl.when(t >= n_c)
        def _():
            e = x_ref[...] - t_ref[...]
            dx_ref[...] = e * (1.0 / d)
            l_ref[...] += jnp.sum(e * e) * (0.5 / d)

    loss, dx = _pcall(
        body, name="loss", grid=(bsz, n_t),
        in_specs=[pl.BlockSpec((None, TB, d), lambda b, t: (b, t, 0)),
                  pl.BlockSpec((None, TB, d), lambda b, t: (b, jnp.maximum(t - n_c, 0), 0))],
        out_specs=[pl.BlockSpec((SUBLANE, LANE), lambda b, t: (0, 0)),
                   pl.BlockSpec((None, TB, d), lambda b, t: (b, t, 0))],
        out_shape=[jax.ShapeDtypeStruct((SUBLANE, LANE), F32), jax.ShapeDtypeStruct(x2.shape, F32)],
        compiler_params=_cparams(2))(x2, target)
    return loss[0, 0], dx


def _rope_tables(lat, tc, dim, lane0):
    quarter = dim // 4
    pos = jnp.arange(lat)
    grid_pos = jnp.stack([pos // GRID_W, pos % GRID_W], axis=-1).astype(F32)
    lane = jnp.arange(LANE)
    p = jnp.clip(lane - lane0, 0, dim - 1)
    active = (lane >= lane0) & (lane < lane0 + dim)
    axis, half, qi = p // (dim // 2), (p % (dim // 2)) // quarter, p % quarter
    inv = ROPE_THETA ** (-qi.astype(F32) / quarter)
    ang = jnp.where(axis[None, :] == 0, grid_pos[:, 0:1], grid_pos[:, 1:2]) * inv[None, :]
    cos = jnp.where(active, jnp.cos(ang), 1.0)
    sin = jnp.where(active, jnp.sin(ang), 0.0)
    sa = jnp.where(half == 0, -sin, 0.0)
    sb = jnp.where(half == 1, sin, 0.0)
    ctx1, ctx0 = jnp.ones((tc, LANE), F32), jnp.zeros((tc, LANE), F32)
    return (jnp.concatenate([ctx1, cos], 0), jnp.concatenate([ctx0, sa], 0), jnp.concatenate([ctx0, sb], 0))


def _local_step(x, ctx, target, mods, wp, sp):
    tc, lat = ctx.shape[1], x.shape[1]
    tabs = {"mla": _rope_tables(lat, tc, MLA_ROPE, MLA_NOPE), "swa": _rope_tables(lat, tc, SWA_HEAD_DIM, 0)}
    stream = jnp.concatenate([ctx, x], axis=1)
    bwds = []
    for li in range(DEPTH):
        stream, bwd = _layer(li, stream, mods[li], wp[li], sp[li], tabs, tc, ctx_q=li < DEPTH - 1)
        bwds.append(bwd)
    loss, dstream = _loss_and_grad(stream, target, tc)
    dmods, dw, ds = [None] * DEPTH, [None] * DEPTH, [None] * DEPTH
    for li in reversed(range(DEPTH)):
        dstream, dmods[li], dw[li], ds[li] = bwds[li](dstream)
    return loss, dstream[:, tc:], dmods, dw, ds


_BIG = (("w_in", (DEPTH, D_MODEL, IN_WIDTH // N_DEV), 2), ("w_uq", (DEPTH, MLA_Q_RANK, MLA_HEADS * MLA_QK // N_DEV), 2),
        ("w_ukv", (DEPTH, MLA_KV_RANK, 128), 2), ("w_out", (DEPTH, 1536 // N_DEV, D_MODEL), 1),
        ("w_ff1", (DEPTH, D_MODEL, D_FF // N_DEV), 2), ("w_ff2", (DEPTH, D_FF // N_DEV, D_MODEL), 1))


def _pad_heads(wm, n_heads, dim, axis=-1):
    axis = axis % wm.ndim
    shp = wm.shape[:axis] + (n_heads, dim) + wm.shape[axis + 1:]
    pad = [(0, 0)] * len(shp)
    pad[axis + 1] = (0, LANE - dim)
    out = jnp.pad(wm.reshape(shp), pad)
    return out.reshape(wm.shape[:axis] + (n_heads * LANE,) + wm.shape[axis + 1:])


def _prep_big(gathered):
    full = {}
    for (name, shp, ax), piece in zip(_BIG, gathered):
        piece = jnp.moveaxis(piece, 0, ax)
        full[name] = piece.reshape(shp[:ax] + (N_DEV * shp[ax],) + shp[ax + 1:])
    layers = []
    for li in range(DEPTH):
        cq, ckv, kr, lx, lg, sq, sk, sv = _split_cols(full["w_in"][li])
        win = jnp.concatenate([_pad_heads(sq, SWA_HEADS, SWA_HEAD_DIM), lx, lg, cq,
                               _pad_heads(sk, SWA_KV_HEADS, SWA_HEAD_DIM), _pad_heads(sv, SWA_KV_HEADS, SWA_HEAD_DIM),
                               ckv, jnp.pad(kr, ((0, 0), (MLA_NOPE, LANE - MLA_QK)))], axis=1)
        ukv = full["w_ukv"][li].reshape(MLA_KV_RANK, MLA_HEADS, MLA_NOPE + MLA_V)
        wo = full["w_out"][li]
        wout = jnp.concatenate([_pad_heads(wo[:GROUP_WIDTH], MLA_HEADS, MLA_V, axis=0), wo[GROUP_WIDTH:2 * GROUP_WIDTH],
                                _pad_heads(wo[2 * GROUP_WIDTH:], SWA_HEADS, SWA_HEAD_DIM, axis=0)], axis=0)
        layers.append({
            "win": win,
            "wuq": _pad_heads(full["w_uq"][li], MLA_HEADS, MLA_QK),
            "wk": _pad_heads(ukv[:, :, :MLA_NOPE].reshape(MLA_KV_RANK, -1), MLA_HEADS, MLA_NOPE),
            "wv": _pad_heads(ukv[:, :, MLA_NOPE:].reshape(MLA_KV_RANK, -1), MLA_HEADS, MLA_V),
            "wout": wout, "ff1": full["w_ff1"][li], "ff2": full["w_ff2"][li]})
    return layers


def _split_cols(wm):
    parts, start = [], 0
    for size in IN_SIZES:
        parts.append(wm[:, start:start + size])
        start += size
    return parts


def _prep_small(raw):
    layers = []
    eye2 = jnp.eye(2, dtype=F32)
    for li in range(DEPTH):
        r1 = lambda a: a.reshape(1, -1)
        gw = raw["lru_gate_w"][li].reshape(2, 2, 4, 2, 64, 64)
        wbd = jnp.einsum("zgknCm,nN->knCzgNm", gw, eye2).reshape(4, LANE, 4, LANE)
        wbd = wbd.transpose(0, 2, 1, 3).reshape(16, LANE, LANE)
        gg = raw["group_g"][li]
        sink = raw["swa_sink"][li].reshape(SWA_KV_HEADS, SWA_GROUP, 1, 1)
        layers.append({
            "norm1_g": r1(raw["norm1_g"][li]), "norm2_g": r1(raw["norm2_g"][li]),
            "q_a_g": r1(raw["q_a_g"][li]), "kv_a_g": r1(raw["kv_a_g"][li]),
            "mla_q_g": jnp.pad(r1(raw["mla_q_g"][li]), ((0, 0), (0, LANE - MLA_QK))),
            "mla_k_g": jnp.pad(r1(raw["mla_k_g"][li]), ((0, 0), (0, LANE - MLA_QK))),
            "swa_q_g": jnp.pad(r1(raw["swa_q_g"][li]), ((0, 0), (0, LANE - SWA_HEAD_DIM))),
            "swa_k_g": jnp.pad(r1(raw["swa_k_g"][li]), ((0, 0), (0, LANE - SWA_HEAD_DIM))),
            "conv_w": [r1(raw["conv_w"][li][kk]) for kk in range(4)], "conv_b": r1(raw["conv_b"][li]),
            "wbd": wbd,
            "gate_b": [r1(raw["lru_gate_b"][li][z, g]) for z in range(2) for g in range(2)],
            "sp": [r1(jax.nn.softplus(-raw["lru_lambda"][li][z])) for z in range(2)],
            "sink_b": jnp.broadcast_to(sink, (SWA_KV_HEADS, SWA_GROUP, QB_SWA, LANE)).reshape(
                SWA_KV_HEADS, SWA_GROUP * QB_SWA, LANE),
            "g_a": _pad_heads(r1(gg[:GROUP_WIDTH]), MLA_HEADS, MLA_V), "g_b": r1(gg[GROUP_WIDTH:2 * GROUP_WIDTH]),
            "g_c": _pad_heads(r1(gg[2 * GROUP_WIDTH:]), SWA_HEADS, SWA_HEAD_DIM)})
    return layers


def _mesh_pos():
    return lax.axis_index("x"), lax.axis_index("y"), lax.axis_index("c")


def _peer(pos, k):
    return tuple(1 - p if (k >> s) & 1 else p for p, s in zip(pos, (2, 1, 0)))


def _dev_index(pos):
    return 4 * pos[0] + 2 * pos[1] + pos[2]


def _exchange(bufs, gather, name):
    n = len(bufs)

    def body(*refs):
        x_refs, o_refs = refs[:n], refs[n:2 * n]
        send_sems, recv_sems, local_sems = refs[2 * n:]
        pos = _mesh_pos()
        me = _dev_index(pos)
        locals_, sends, recvs = [], [], []
        for j in range(n):
            src_mine = x_refs[j] if gather else x_refs[j].at[me]
            locals_.append(pltpu.make_async_copy(src_mine, o_refs[j].at[me], local_sems.at[j]))
        for k in range(1, N_DEV):
            peer = _peer(pos, k)
            pidx = _dev_index(peer)
            for j in range(n):
                src = x_refs[j] if gather else x_refs[j].at[pidx]
                sem = (k - 1) * n + j
                sends.append(pltpu.make_async_remote_copy(
                    src_ref=src, dst_ref=o_refs[j].at[me], send_sem=send_sems.at[sem], recv_sem=recv_sems.at[sem],
                    device_id=peer, device_id_type=pl.DeviceIdType.MESH))
                recvs.append(pltpu.make_async_remote_copy(
                    src_ref=src, dst_ref=o_refs[j].at[pidx], send_sem=send_sems.at[sem], recv_sem=recv_sems.at[sem],
                    device_id=peer, device_id_type=pl.DeviceIdType.MESH))
        for cp in locals_ + sends:
            cp.start()
        for cp in recvs:
            cp.wait_recv()
        for cp in sends:
            cp.wait_send()
        for cp in locals_:
            cp.wait()

    shapes = [jax.ShapeDtypeStruct((N_DEV,) + tuple(b.shape if gather else b.shape[1:]), b.dtype) for b in bufs]
    return _pcall(
        body, name=name, out_shape=shapes,
        in_specs=[pl.BlockSpec(memory_space=pl.ANY)] * n, out_specs=[pl.BlockSpec(memory_space=pl.ANY)] * n,
        scratch_shapes=[pltpu.SemaphoreType.DMA(((N_DEV - 1) * n,)), pltpu.SemaphoreType.DMA(((N_DEV - 1) * n,)),
                        pltpu.SemaphoreType.DMA((n,))],
    )(*bufs)


def _pack(arrs, dtype):
    flat = jnp.concatenate([a.reshape(-1).astype(dtype) for a in arrs])
    rows = -(-flat.size // PACK_W)
    rows = -(-rows // 16) * 16
    return jnp.pad(flat, (0, rows * PACK_W - flat.size)).reshape(rows, PACK_W)


def _unpack(buf, shapes, lead=()):
    flat = buf.reshape(lead + (-1,))
    out, off = [], 0
    for shp in shapes:
        n = math.prod(shp)
        out.append(flat[..., off:off + n].reshape(lead + tuple(shp)))
        off += n
    return out


def _sum_sources(buf, name):
    _, r, c = buf.shape
    tr = _rows_tile(r)

    def body(x_ref, o_ref):
        acc = x_ref[0]
        for d in range(1, N_DEV):
            acc = acc + x_ref[d]
        o_ref[...] = acc

    return _pcall(body, name=name, grid=(r // tr,),
                  in_specs=[pl.BlockSpec((N_DEV, tr, c), lambda i: (0, i, 0))],
                  out_specs=pl.BlockSpec((tr, c), lambda i: (i, 0)),
                  out_shape=jax.ShapeDtypeStruct((r, c), F32), compiler_params=_cparams(1))(buf)


def _rows_tile(r):
    best = r
    for t in range(SUBLANE, 257, SUBLANE):
        if r % t == 0:
            best = t
    return best


def _adamw(grads, wgt, m, v, name):
    n_src, r, c = grads.shape
    tr = _rows_tile(r)
    bc1 = 1.0 - ADAM_B1 ** ADAM_STEP
    bc2 = 1.0 - ADAM_B2 ** ADAM_STEP

    def body(g_ref, w_ref, m_ref, v_ref, go_ref, d_ref, mo_ref, vo_ref):
        g = g_ref[0].astype(F32)
        for d in range(1, n_src):
            g = g + g_ref[d].astype(F32)
        m_new = ADAM_B1 * m_ref[...] + (1.0 - ADAM_B1) * g
        v_new = ADAM_B2 * v_ref[...] + (1.0 - ADAM_B2) * (g * g)
        go_ref[...] = g
        mo_ref[...] = m_new
        vo_ref[...] = v_new
        d_ref[...] = -ADAM_LR * ((m_new / bc1) / (jnp.sqrt(v_new / bc2) + ADAM_EPS) + ADAM_WD * w_ref[...])

    spec = pl.BlockSpec((tr, c), lambda i: (i, 0))
    return _pcall(body, name=name, grid=(r // tr,),
                  in_specs=[pl.BlockSpec((n_src, tr, c), lambda i: (0, i, 0)), spec, spec, spec],
                  out_specs=[spec] * 4, out_shape=[jax.ShapeDtypeStruct((r, c), F32)] * 4,
                  compiler_params=_cparams(1))(grads, wgt, m, v)


def _silu(z):
    return z * jax.nn.sigmoid(z)


_WEIGHTS = ("c_ctx", "w_mod", "b_mod", "norm1_g", "w_in", "q_a_g", "w_uq", "kv_a_g", "w_ukv", "mla_q_g", "mla_k_g",
            "conv_w", "conv_b", "lru_gate_w", "lru_gate_b", "lru_lambda", "swa_q_g", "swa_k_g", "swa_sink", "group_g",
            "w_out", "norm2_g", "w_ff1", "w_ff2")
_SHARDED_SMALL = ("conv_w", "lru_gate_b", "lru_lambda")
_REPL_RAW = ("norm1_g", "q_a_g", "kv_a_g", "mla_q_g", "mla_k_g", "conv_b", "lru_gate_w", "swa_q_g", "swa_k_g",
             "swa_sink", "group_g", "norm2_g")
MOD_ROWS = 32


def kernel(x, c, ctx, c_ctx, w_mod, b_mod, norm1_g, w_in, q_a_g, w_uq, kv_a_g, w_ukv, mla_q_g, mla_k_g, conv_w, conv_b, lru_gate_w, lru_gate_b, lru_lambda, swa_q_g, swa_k_g, swa_sink, group_g, w_out, norm2_g, w_ff1, w_ff2, loss_target, m_c_ctx, m_w_mod, m_b_mod, m_norm1_g, m_w_in, m_q_a_g, m_w_uq, m_kv_a_g, m_w_ukv, m_mla_q_g, m_mla_k_g, m_conv_w, m_conv_b, m_lru_gate_w, m_lru_gate_b, m_lru_lambda, m_swa_q_g, m_swa_k_g, m_swa_sink, m_group_g, m_w_out, m_norm2_g, m_w_ff1, m_w_ff2, v_c_ctx, v_w_mod, v_b_mod, v_norm1_g, v_w_in, v_q_a_g, v_w_uq, v_kv_a_g, v_w_ukv, v_mla_q_g, v_mla_k_g, v_conv_w, v_conv_b, v_lru_gate_w, v_lru_gate_b, v_lru_lambda, v_swa_q_g, v_swa_k_g, v_swa_sink, v_group_g, v_w_out, v_norm2_g, v_w_ff1, v_w_ff2):
    wts = dict(c_ctx=c_ctx, w_mod=w_mod, b_mod=b_mod, norm1_g=norm1_g, w_in=w_in, q_a_g=q_a_g, w_uq=w_uq,
               kv_a_g=kv_a_g, w_ukv=w_ukv, mla_q_g=mla_q_g, mla_k_g=mla_k_g, conv_w=conv_w, conv_b=conv_b,
               lru_gate_w=lru_gate_w, lru_gate_b=lru_gate_b, lru_lambda=lru_lambda, swa_q_g=swa_q_g, swa_k_g=swa_k_g,
               swa_sink=swa_sink, group_g=group_g, w_out=w_out, norm2_g=norm2_g, w_ff1=w_ff1, w_ff2=w_ff2)
    mom1 = dict(zip(_WEIGHTS, (m_c_ctx, m_w_mod, m_b_mod, m_norm1_g, m_w_in, m_q_a_g, m_w_uq, m_kv_a_g, m_w_ukv,
                               m_mla_q_g, m_mla_k_g, m_conv_w, m_conv_b, m_lru_gate_w, m_lru_gate_b, m_lru_lambda,
                               m_swa_q_g, m_swa_k_g, m_swa_sink, m_group_g, m_w_out, m_norm2_g, m_w_ff1, m_w_ff2)))
    mom2 = dict(zip(_WEIGHTS, (v_c_ctx, v_w_mod, v_b_mod, v_norm1_g, v_w_in, v_q_a_g, v_w_uq, v_kv_a_g, v_w_ukv,
                               v_mla_q_g, v_mla_k_g, v_conv_w, v_conv_b, v_lru_gate_w, v_lru_gate_b, v_lru_lambda,
                               v_swa_q_g, v_swa_k_g, v_swa_sink, v_group_g, v_w_out, v_norm2_g, v_w_ff1, v_w_ff2)))
    bsz = x.shape[0]
    n_ex = bsz * N_DEV
    me = _dev_index(_mesh_pos())
    mod_cols = w_mod.shape[-1]

    small_shapes = [c.shape, conv_w.shape, lru_gate_b.shape, lru_lambda.shape]
    (g_small,) = _exchange([_pack([c, conv_w, lru_gate_b, lru_lambda], F32)], True, "ag_small")
    c_all, conv_w_all, gate_b_all, lam_all = _unpack(g_small, small_shapes, lead=(N_DEV,))
    c_all = c_all.reshape(n_ex, D_MODEL)
    cat_last = lambda a: jnp.moveaxis(a, 0, -2).reshape(a.shape[1:-1] + (N_DEV * a.shape[-1],))
    conv_w_full, gate_b_full, lam_full = cat_last(conv_w_all), cat_last(gate_b_all), cat_last(lam_all)

    act = jnp.zeros((MOD_ROWS, D_MODEL), F32).at[:n_ex].set(_silu(c_all)).at[n_ex].set(_silu(c_ctx))
    mod_part = jnp.concatenate([_mm(act, w_mod[li], "nn", F32, "mm_mod_l%d" % li) for li in range(DEPTH)], axis=1)
    (mod_all,) = _exchange([mod_part], True, "ag_mod")
    mods = []
    for li in range(DEPTH):
        full = jnp.moveaxis(mod_all[:, :, li * mod_cols:(li + 1) * mod_cols], 0, 1).reshape(MOD_ROWS, -1) + b_mod[li]
        mine = lax.dynamic_slice_in_dim(full, me * bsz, bsz, axis=0)
        ctx_row = jnp.broadcast_to(full[n_ex], mine.shape)
        both = jnp.stack([ctx_row, mine], axis=1).reshape(bsz, 2, N_MOD, 1, D_MODEL)
        mods.append([both[:, :, j] for j in range(N_MOD)])

    big_all = _exchange([wts[n].astype(BF16) for n, _, _ in _BIG], True, "ag_big")
    wp, big_vjp = jax.vjp(_prep_big, big_all)
    raw = {n: wts[n] for n in _REPL_RAW}
    raw.update(conv_w=conv_w_full, lru_gate_b=gate_b_full, lru_lambda=lam_full)
    sp, small_vjp = jax.vjp(_prep_small, raw)

    loss_part, grad_x, dmods, dw, ds = _local_step(x, ctx, loss_target, mods, wp, sp)

    (g_big,) = big_vjp([{k: v.astype(BF16) for k, v in d.items()} for d in dw])
    g_recv = _exchange(g_big, False, "rs_big")
    (d_raw,) = small_vjp(ds)

    dm_rows = []
    for li in range(DEPTH):
        dm = jnp.concatenate(dmods[li], axis=-1)
        dm_rows.append(jnp.concatenate([dm[:, 1, 0], jnp.sum(dm[:, 0, 0], axis=0, keepdims=True)], axis=0))
    dm_mine = jnp.concatenate(dm_rows, axis=1)
    dm_mine = jnp.pad(dm_mine, ((0, SUBLANE - bsz - 1), (0, 0)))
    (dm_all,) = _exchange([dm_mine], True, "ag_dmod")
    g_wmod, g_bmod, dact_ctx = [], [], jnp.zeros((D_MODEL,), F32)
    for li in range(DEPTH):
        part = dm_all[:, :, li * N_MOD * D_MODEL:(li + 1) * N_MOD * D_MODEL]
        dm32 = jnp.zeros((MOD_ROWS, N_MOD * D_MODEL), F32).at[:n_ex].set(part[:, :bsz].reshape(n_ex, -1))
        dm32 = dm32.at[n_ex].set(jnp.sum(part[:, bsz], axis=0))
        g_bmod.append(jnp.sum(dm32, axis=0))
        cols = lax.dynamic_slice_in_dim(dm32, me * mod_cols, mod_cols, axis=1)
        g_wmod.append(_mm(act, cols, "tn", F32, "mm_mod_dw_l%d" % li))
        dact_ctx = dact_ctx + _mm(cols, w_mod[li], "nt", F32, "mm_mod_dx_l%d" % li)[n_ex]
    sg = jax.nn.sigmoid(c_ctx)
    g_cctx_part = dact_ctx * (sg * (1.0 + c_ctx * (1.0 - sg)))

    small_names = list(_REPL_RAW) + list(_SHARDED_SMALL)
    small_parts = [d_raw[n] for n in small_names] + [g_cctx_part, loss_part.reshape(1)]
    small_shapes = [a.shape for a in small_parts]
    small_sum = _sum_sources(_exchange([_pack(small_parts, F32)], True, "ag_grads")[0], "sum_grads")
    small_tot = _unpack(small_sum, small_shapes)
    grads = dict(zip(small_names, small_tot[:len(small_names)]))
    grads["c_ctx"], loss = small_tot[-2], small_tot[-1][0]
    for n in _SHARDED_SMALL:
        width = wts[n].shape[-1]
        grads[n] = lax.dynamic_slice_in_dim(grads[n], me * width, width, axis=grads[n].ndim - 1)
    grads["b_mod"] = jnp.stack(g_bmod, axis=0)

    delta, new_m, new_v = {}, {}, {}
    for (n, shp, _), recv in zip(_BIG, g_recv):
        two_d = (shp[0] * shp[1], shp[2])
        src = recv.reshape((N_DEV,) + two_d)
        res = _adamw(src, wts[n].reshape(two_d), mom1[n].reshape(two_d), mom2[n].reshape(two_d), "adamw_" + n)
        grads[n], delta[n], new_m[n], new_v[n] = [r.reshape(shp) for r in res]
    two_d = (DEPTH * D_MODEL, mod_cols)
    res = _adamw(jnp.stack(g_wmod, axis=0).reshape((1,) + two_d), w_mod.reshape(two_d), mom1["w_mod"].reshape(two_d),
                 mom2["w_mod"].reshape(two_d), "adamw_w_mod")
    grads["w_mod"], delta["w_mod"], new_m["w_mod"], new_v["w_mod"] = [r.reshape(w_mod.shape) for r in res]
    rest = [n for n in _WEIGHTS if n not in delta]
    shapes = [wts[n].shape for n in rest]
    res = _adamw(_pack([grads[n] for n in rest], F32)[None], _pack([wts[n] for n in rest], F32),
                 _pack([mom1[n] for n in rest], F32), _pack([mom2[n] for n in rest], F32), "adamw_small")
    for tgt, buf in zip((delta, new_m, new_v), res[1:]):
        tgt.update(zip(rest, _unpack(buf, shapes)))

    return (loss, grad_x, *[grads[n] for n in _WEIGHTS], *[delta[n] for n in _WEIGHTS],
            *[new_m[n] for n in _WEIGHTS], *[new_v[n] for n in _WEIGHTS])
```

```python
import functools
import math

import jax
import jax.numpy as jnp
from jax import lax
from jax.experimental import pallas as pl
from jax.experimental.pallas import tpu as pltpu

F32, BF16 = jnp.float32, jnp.bfloat16

N_DEV = 8
DEPTH = 2
D_MODEL = 1024
D_FF = 4096
N_MOD = 6
GRID_W = 64
WINDOW = 128
ROPE_THETA = 10000.0
EPS = 1e-6
NEG_INF = -1e30
LRU_C = 8.0
LRU_WIDTH = 512
MLA_HEADS, MLA_NOPE, MLA_ROPE, MLA_V = 8, 64, 32, 64
MLA_QK = MLA_NOPE + MLA_ROPE
MLA_Q_RANK, MLA_KV_RANK = 256, 128
SWA_HEADS, SWA_KV_HEADS, SWA_GROUP, SWA_HEAD_DIM = 8, 2, 4, 64
GROUP_WIDTH = 512
IN_SIZES = (256, 128, 32, 512, 512, 512, 128, 128)
IN_WIDTH = sum(IN_SIZES)
ADAM_LR, ADAM_B1, ADAM_B2, ADAM_EPS, ADAM_WD, ADAM_STEP = 0.001, 0.9, 0.999, 1e-08, 0.01, 10

LANE = 128
SUBLANE = 8
TB = 256
QB_SWA = 128
PACK_W = 1024
VMEM_LIMIT = 56 * 1024 * 1024
P_WIDTH = 3072
PC_SQ, PC_LX, PC_LG, PC_CQ, PC_SK, PC_SV, PC_CKV, PC_KR = 0, 1024, 1536, 2048, 2304, 2560, 2816, 2944
MIX_P = 2560


def _pcall(body, **kw):
    return pl.pallas_call(body, **kw)


def _cparams(n_grid):
    return pltpu.CompilerParams(dimension_semantics=("arbitrary",) * n_grid, vmem_limit_bytes=VMEM_LIMIT)


def _dg(a, b, ca, cb):
    return lax.dot_general(a.astype(BF16), b.astype(BF16), (((ca,), (cb,)), ((), ())),
                           preferred_element_type=F32)


@jax.custom_vjp
def _nn(a, b):
    return _dg(a, b, 1, 0)


@jax.custom_vjp
def _nt(a, b):
    return _dg(a, b, 1, 1)


@jax.custom_vjp
def _tn(a, b):
    return _dg(a, b, 0, 0)


_nn.defvjp(lambda a, b: (_nn(a, b), (a, b)), lambda r, ct: (_nt(ct, r[1]), _tn(r[0], ct)))
_nt.defvjp(lambda a, b: (_nt(a, b), (a, b)), lambda r, ct: (_nn(ct, r[1]), _tn(ct, r[0])))
_tn.defvjp(lambda a, b: (_tn(a, b), (a, b)), lambda r, ct: (_nt(r[1], ct), _nn(r[0], ct)))


@functools.partial(jax.custom_vjp, nondiff_argnums=(1, 2))
def _roll(x, shift, axis):
    return pltpu.roll(x, shift % x.shape[axis], axis)


_roll.defvjp(lambda x, shift, axis: (_roll(x, shift, axis), None),
             lambda shift, axis, _, ct: (_roll(ct, -shift, axis),))


@functools.partial(jax.custom_vjp, nondiff_argnums=(1, 2))
def _split(x, n, axis):
    w = x.shape[axis] // n
    return tuple(lax.slice_in_dim(x, i * w, (i + 1) * w, axis=axis) for i in range(n))


_split.defvjp(lambda x, n, axis: (_split(x, n, axis), None),
              lambda n, axis, _, cts: (jnp.concatenate(cts, axis=axis),))


@jax.custom_vjp
def _unstack(x):
    return tuple(x[i] for i in range(x.shape[0]))


_unstack.defvjp(lambda x: (_unstack(x), None), lambda _, cts: (jnp.stack(cts, axis=0),))


def _sig(x):
    return 0.5 * (jnp.tanh(0.5 * x) + 1.0)


def _gelu(x):
    return 0.5 * x * (1.0 + jnp.tanh(math.sqrt(2.0 / math.pi) * (x + 0.044715 * (x * x * x))))


def _rms(x, g, n):
    ms = jnp.sum(x * x, axis=-1, keepdims=True) * (1.0 / n)
    return x * lax.rsqrt(ms + EPS) * g


def _rope(y, cos, sa, sb, quarter):
    return y * cos + _roll(y, -quarter, 1) * sa + _roll(y, quarter, 1) * sb


def _softmax_rows(s, extra=None):
    m = jnp.max(s, axis=-1, keepdims=True)
    if extra is not None:
        m = jnp.maximum(m, extra)
    m = lax.stop_gradient(m)
    e = jnp.exp(s - m)
    den = jnp.sum(e, axis=-1, keepdims=True)
    if extra is not None:
        den = den + jnp.exp(extra - m)
    return e / den


class _A:
    def __init__(self, arr, block, imap, kind="row", first=None, gdtype=F32, gshape=None, gimap=None):
        self.arr, self.block, self.imap, self.kind, self.first = arr, block, imap, kind, first
        self.gdtype, self.gshape, self.gimap = gdtype, gshape, gimap


def _all_zero(*ids):
    return functools.reduce(jnp.logical_and, [i == 0 for i in ids])


def _par(arr):
    nd = arr.ndim
    return _A(arr, arr.shape, lambda *ids: (0,) * nd, "acc", first=_all_zero)


def _op_fwd(name, fn, grid, args, outs):
    n_in = len(args)

    def body(*refs):
        vals = [r[...].astype(F32) for r in refs[:n_in]]
        for r, v in zip(refs[n_in:], fn(*vals)):
            r[...] = v.astype(r.dtype)

    return _pcall(
        body, name=name, grid=grid,
        in_specs=[pl.BlockSpec(a.block, a.imap) for a in args],
        out_specs=[pl.BlockSpec(o[2], o[3]) for o in outs],
        out_shape=[jax.ShapeDtypeStruct(o[0], o[1]) for o in outs],
        compiler_params=_cparams(len(grid)),
    )(*[a.arr for a in args])


def _op_bwd(name, fn, grid, args, outs, ct_arrays):
    n_in, n_ct = len(args), len(outs)
    didx = [i for i, a in enumerate(args) if a.kind != "const"]

    def body(*refs):
        ids = [pl.program_id(i) for i in range(len(grid))]
        vals = [r[...].astype(F32) for r in refs[:n_in]]

        def g(*dv):
            full = list(vals)
            for i, v in zip(didx, dv):
                full[i] = v
            return tuple(fn(*full))

        _, vjp = jax.vjp(g, *[vals[i] for i in didx])
        grads = vjp(tuple(r[...].astype(F32) for r in refs[n_in:n_in + n_ct]))
        for gr, i, r in zip(grads, didx, refs[n_in + n_ct:]):
            a = args[i]
            if a.kind == "row":
                r[...] = gr.astype(r.dtype)
            else:
                first = a.first(*ids)

                @pl.when(first)
                def _():
                    r[...] = gr

                @pl.when(jnp.logical_not(first))
                def _():
                    r[...] += gr

    g_specs, g_shapes = [], []
    for i in didx:
        a = args[i]
        if a.kind == "row":
            g_specs.append(pl.BlockSpec(a.block, a.gimap or a.imap))
            g_shapes.append(jax.ShapeDtypeStruct(a.gshape or a.arr.shape, a.gdtype))
        else:
            g_specs.append(pl.BlockSpec(a.block, a.imap))
            g_shapes.append(jax.ShapeDtypeStruct(a.arr.shape, F32))
    return _pcall(
        body, name=name, grid=grid,
        in_specs=[pl.BlockSpec(a.block, a.imap) for a in args] + [pl.BlockSpec(o[2], o[3]) for o in outs],
        out_specs=g_specs, out_shape=g_shapes,
        compiler_params=_cparams(len(grid)),
    )(*[a.arr for a in args], *ct_arrays)


def _rowop(name, fn, grid, args, outs):
    res = _op_fwd(name, fn, grid, args, outs)
    return res, lambda *cts: _op_bwd(name + "_bwd", fn, grid, args, outs, cts)


def _pick(n, cap):
    best = None
    for t in range(LANE, cap + 1, LANE):
        if n % t == 0:
            best = t
    return best or n


def _mm(a, b, mode, out_dtype, name, epi=None, aux=None):
    if mode == "nn":
        (m, k), n = a.shape, b.shape[1]
    elif mode == "nt":
        (m, k), n = a.shape, b.shape[0]
    else:
        (k, m), n = a.shape, b.shape[1]
    tm = 512 if m % 512 == 0 else m
    tn, tk = _pick(n, 1024), _pick(k, 1280)
    nk = k // tk
    if mode == "tn":
        a_spec = pl.BlockSpec((tk, tm), lambda i, j, kk: (kk, i))
    else:
        a_spec = pl.BlockSpec((tm, tk), lambda i, j, kk: (i, kk))
    if mode == "nt":
        b_spec = pl.BlockSpec((tn, tk), lambda i, j, kk: (j, kk))
    else:
        b_spec = pl.BlockSpec((tk, tn), lambda i, j, kk: (kk, j))
    dims = {"nn": (1, 0), "nt": (1, 1), "tn": (0, 0)}[mode]
    o_spec = pl.BlockSpec((tm, tn), lambda i, j, kk: (i, j))
    n_aux = 0 if aux is None else 1
    n_out = 2 if epi == "sqrelu" else 1

    def body(*refs):
        a_ref, b_ref = refs[0], refs[1]
        o_refs = refs[2 + n_aux:2 + n_aux + n_out]
        acc = refs[-1]
        kk = pl.program_id(2)
        part = _dg(a_ref[...], b_ref[...], *dims)

        if nk > 1:
            @pl.when(kk == 0)
            def _():
                acc[...] = part

            @pl.when((kk > 0) & (kk < nk - 1))
            def _():
                acc[...] += part

        @pl.when(kk == nk - 1)
        def _():
            r = part if nk == 1 else acc[...] + part
            if epi == "sqrelu":
                o_refs[0][...] = r.astype(o_refs[0].dtype)
                rl = jnp.maximum(r, 0.0)
                o_refs[1][...] = (rl * rl).astype(o_refs[1].dtype)
            elif epi == "dsqrelu":
                pre = refs[2][...].astype(F32)
                o_refs[0][...] = (r * (2.0 * jnp.maximum(pre, 0.0))).astype(o_refs[0].dtype)
            else:
                o_refs[0][...] = r.astype(o_refs[0].dtype)

    res = _pcall(
        body, name=name, grid=(m // tm, n // tn, nk),
        in_specs=[a_spec, b_spec] + [o_spec] * n_aux,
        out_specs=[o_spec] * n_out,
        out_shape=[jax.ShapeDtypeStruct((m, n), out_dtype)] * n_out,
        scratch_shapes=[pltpu.VMEM((tm, tn), F32)],
        compiler_params=_cparams(3),
    )(a, b, *([aux] if aux is not None else []))
    return res if n_out == 2 else res[0]


def _mla_block(q, k, v):
    p = _softmax_rows(_nt(q, k) * (MLA_QK ** -0.5))
    return _nn(p, v)


def _call_with_exchange(body, xchg, *, name, grid, in_specs, out_specs, out_shape, operands, scratch_shapes=()):
    if xchg is None:
        res = _pcall(body, name=name, grid=grid, in_specs=in_specs, out_specs=out_specs, out_shape=out_shape,
                     scratch_shapes=list(scratch_shapes), compiler_params=_cparams(len(grid)))(*operands)
        return list(res), []
    n_in, n_out, n_sc, n = len(in_specs), len(out_specs), len(scratch_shapes), xchg.n

    def wrapped(*refs):
        ins, x_refs = refs[:n_in], refs[n_in:n_in + n]
        outs, xo_refs = refs[n_in + n:n_in + n + n_out], refs[n_in + n + n_out:n_in + 2 * n + n_out]
        scratch, sems = refs[n_in + 2 * n + n_out:n_in + 2 * n + n_out + n_sc], refs[n_in + 2 * n + n_out + n_sc:]
        ids = [pl.program_id(i) for i in range(len(grid))]

        @pl.when(functools.reduce(jnp.logical_and, [i == 0 for i in ids]))
        def _():
            xchg.start(x_refs, xo_refs, sems)

        body(*ins, *outs, *scratch)

        @pl.when(functools.reduce(jnp.logical_and, [i == g - 1 for i, g in zip(ids, grid)]))
        def _():
            xchg.wait(x_refs, xo_refs, sems)

    res = _pcall(wrapped, name=name, grid=grid, in_specs=list(in_specs) + xchg.specs,
                 out_specs=list(out_specs) + xchg.specs, out_shape=list(out_shape) + xchg.out_shape,
                 scratch_shapes=list(scratch_shapes) + xchg.scratch, compiler_params=_cparams(len(grid)),
                 )(*operands, *xchg.bufs)
    return list(res[:n_out]), list(res[n_out:])


def _mla_attn(q, k, v, tc, ctx_q, name, xchg=None):
    bsz, t_all, _ = q.shape
    n_t = t_all // TB
    grid = (bsz, MLA_HEADS, n_t)
    q_spec = pl.BlockSpec((None, TB, LANE), lambda b, h, t: (b, t, h))
    kv_spec = pl.BlockSpec((None, t_all, LANE), lambda b, h, t: (b, 0, h))

    def fwd_body(q_ref, k_ref, v_ref, o_ref):
        t = pl.program_id(2)

        @pl.when(t == 0)
        def _():
            if ctx_q:
                o_ref[...] = _mla_block(q_ref[...], k_ref[0:tc, :], v_ref[0:tc, :])
            else:
                o_ref[...] = jnp.zeros_like(o_ref)

        @pl.when(t > 0)
        def _():
            o_ref[...] = _mla_block(q_ref[...], k_ref[...], v_ref[...])

    (o,), gathered = _call_with_exchange(
        fwd_body, xchg, name=name, grid=grid, in_specs=[q_spec, kv_spec, kv_spec], out_specs=[q_spec],
        out_shape=[jax.ShapeDtypeStruct(q.shape, F32)], operands=(q, k, v))

    def bwd(do, xchg=None):
        def bwd_body(q_ref, k_ref, v_ref, do_ref, dq_ref, dk_ref, dv_ref):
            t = pl.program_id(2)

            @pl.when(t == 0)
            def _():
                dk_ref[...] = jnp.zeros_like(dk_ref)
                dv_ref[...] = jnp.zeros_like(dv_ref)
                if ctx_q:
                    _, vjp = jax.vjp(_mla_block, q_ref[...].astype(F32), k_ref[0:tc, :].astype(F32),
                                     v_ref[0:tc, :].astype(F32))
                    dq, dk, dv = vjp(do_ref[...])
                    dq_ref[...] = dq
                    dk_ref[0:tc, :] = dk
                    dv_ref[0:tc, :] = dv
                else:
                    dq_ref[...] = jnp.zeros_like(dq_ref)

            @pl.when(t > 0)
            def _():
                _, vjp = jax.vjp(_mla_block, q_ref[...].astype(F32), k_ref[...].astype(F32), v_ref[...].astype(F32))
                dq, dk, dv = vjp(do_ref[...])
                dq_ref[...] = dq
                dk_ref[...] += dk
                dv_ref[...] += dv

        return _call_with_exchange(
            bwd_body, xchg, name=name + "_bwd", grid=grid, in_specs=[q_spec, kv_spec, kv_spec, q_spec],
            out_specs=[q_spec, kv_spec, kv_spec], out_shape=[jax.ShapeDtypeStruct(q.shape, F32)] * 3,
            operands=(q, k, v, do))

    return o, gathered, bwd


def _swa_ctx_block(q, kc, vc, sink):
    sk = jnp.sum(sink, axis=-1, keepdims=True) * (1.0 / LANE)
    sks = _split(sk, SWA_GROUP, 0)
    outs = []
    for qh, s_h in zip(_split(q, SWA_GROUP, 1), sks):
        p = _softmax_rows(_nt(qh, kc) * (SWA_HEAD_DIM ** -0.5), s_h)
        outs.append(_nn(p, vc))
    return jnp.concatenate(outs, axis=1)


def _swa_win_block(q, kc, kw, vc, vw, sink, mask):
    keys = jnp.concatenate([kc, kw], axis=0)
    vals = jnp.concatenate([vc, vw], axis=0)
    sk = jnp.sum(sink, axis=-1, keepdims=True) * (1.0 / LANE)
    sks = _split(sk, SWA_GROUP, 0)
    outs = []
    for qh, s_h in zip(_split(q, SWA_GROUP, 1), sks):
        s = jnp.where(mask, _nt(qh, keys) * (SWA_HEAD_DIM ** -0.5), NEG_INF)
        outs.append(_nn(_softmax_rows(s, s_h), vals))
    return jnp.concatenate(outs, axis=1)


def _swa_attn(q, k, p_all, sink_b, tc, ctx_q, name, xchg=None):
    bsz, t_all, _ = q.shape
    n_q = t_all // QB_SWA
    n_cq = tc // QB_SWA
    lat = t_all - tc
    span = QB_SWA + 2 * WINDOW
    gw = SWA_GROUP * LANE
    grid = (bsz, SWA_KV_HEADS, n_q)
    q_spec = pl.BlockSpec((None, QB_SWA, gw), lambda b, g, i: (b, i, g))
    k_spec = pl.BlockSpec((None, t_all, LANE), lambda b, g, i: (b, 0, g))
    v_spec = pl.BlockSpec((None, t_all, LANE), lambda b, g, i: (b, 0, PC_SV // LANE + g))
    s_spec = pl.BlockSpec((None, SWA_GROUP * QB_SWA, LANE), lambda b, g, i: (g, 0, 0))

    def window(i):
        q0 = (i - n_cq) * QB_SWA
        w0 = jnp.clip(q0 - WINDOW, 0, lat - span)
        w0 = pl.multiple_of(w0, QB_SWA)
        qi = q0 + lax.broadcasted_iota(jnp.int32, (QB_SWA, tc + span), 0)
        col = lax.broadcasted_iota(jnp.int32, (QB_SWA, tc + span), 1)
        kj = w0 + col - tc
        mask = (col < tc) | ((kj >= qi - WINDOW) & (kj <= qi + WINDOW))
        return w0, mask

    def fwd_body(q_ref, k_ref, v_ref, s_ref, o_ref):
        i = pl.program_id(2)

        @pl.when(i < n_cq)
        def _():
            if ctx_q:
                o_ref[...] = _swa_ctx_block(q_ref[...].astype(F32), k_ref[0:tc, :], v_ref[0:tc, :], s_ref[...])
            else:
                o_ref[...] = jnp.zeros_like(o_ref)

        @pl.when(i >= n_cq)
        def _():
            w0, mask = window(i)
            o_ref[...] = _swa_win_block(q_ref[...].astype(F32), k_ref[0:tc, :], k_ref[pl.ds(tc + w0, span), :],
                                        v_ref[0:tc, :], v_ref[pl.ds(tc + w0, span), :], s_ref[...], mask)

    (o,), gathered = _call_with_exchange(
        fwd_body, xchg, name=name, grid=grid, in_specs=[q_spec, k_spec, v_spec, s_spec], out_specs=[q_spec],
        out_shape=[jax.ShapeDtypeStruct(q.shape, F32)], operands=(q, k, p_all, sink_b))

    def bwd(do, xchg=None):
        def bwd_body(q_ref, k_ref, v_ref, s_ref, do_ref, dq_ref, dk_ref, dv_ref, ds_ref):
            i = pl.program_id(2)

            @pl.when(i == 0)
            def _():
                dk_ref[...] = jnp.zeros_like(dk_ref)
                dv_ref[...] = jnp.zeros_like(dv_ref)
                ds_ref[...] = jnp.zeros_like(ds_ref)

            @pl.when(i < n_cq)
            def _():
                if ctx_q:
                    _, vjp = jax.vjp(_swa_ctx_block, q_ref[...].astype(F32), k_ref[0:tc, :].astype(F32),
                                     v_ref[0:tc, :], s_ref[...])
                    dq, dk, dv, ds = vjp(do_ref[...])
                    dq_ref[...] = dq
                    dk_ref[0:tc, :] += dk
                    dv_ref[0:tc, :] += dv
                    ds_ref[...] += ds
                else:
                    dq_ref[...] = jnp.zeros_like(dq_ref)

            @pl.when(i >= n_cq)
            def _():
                w0, mask = window(i)
                win = pl.ds(tc + w0, span)
                _, vjp = jax.vjp(functools.partial(_swa_win_block, mask=mask), q_ref[...].astype(F32),
                                 k_ref[0:tc, :].astype(F32), k_ref[win, :].astype(F32),
                                 v_ref[0:tc, :], v_ref[win, :], s_ref[...])
                dq, dkc, dkw, dvc, dvw, ds = vjp(do_ref[...])
                dq_ref[...] = dq
                dk_ref[0:tc, :] += dkc
                dk_ref[win, :] += dkw
                dv_ref[0:tc, :] += dvc
                dv_ref[win, :] += dvw
                ds_ref[...] += ds

        kv_out = pl.BlockSpec((None, t_all, LANE), lambda b, g, i: (b, 0, g))
        ds_spec = pl.BlockSpec((None, None, SWA_GROUP * QB_SWA, LANE), lambda b, g, i: (b, g, 0, 0))
        kv_shape = jax.ShapeDtypeStruct((bsz, t_all, SWA_KV_HEADS * LANE), F32)
        return _call_with_exchange(
            bwd_body, xchg, name=name + "_bwd", grid=grid, in_specs=[q_spec, k_spec, v_spec, s_spec, q_spec],
            out_specs=[q_spec, kv_out, kv_out, ds_spec],
            out_shape=[jax.ShapeDtypeStruct(q.shape, F32), kv_shape, kv_shape,
                       jax.ShapeDtypeStruct((bsz,) + sink_b.shape, F32)],
            operands=(q, k, p_all, sink_b, do))

    return o, gathered, bwd


def _scan_rows(a, u, reverse, a_s, u_s, c_s):
    t_all, c = a.shape
    row8 = lax.broadcasted_iota(jnp.int32, a.shape, 0) % SUBLANE
    for d in (1, 2, 4):
        sh = d if not reverse else t_all - d
        ar, ur = pltpu.roll(a, sh, 0), pltpu.roll(u, sh, 0)
        m = (row8 >= d) if not reverse else (row8 < SUBLANE - d)
        u = jnp.where(m, a * ur + u, u)
        a = jnp.where(m, a * ar, a)
    a_s[...] = a
    u_s[...] = u
    n_tiles = t_all // SUBLANE

    def step(j, carry):
        tile = j if not reverse else n_tiles - 1 - j
        base = pl.multiple_of(tile * SUBLANE, SUBLANE)
        c_s[pl.ds(base, SUBLANE), :] = jnp.broadcast_to(carry, (SUBLANE, c))
        last = base + (0 if reverse else SUBLANE - 1)
        return a_s[pl.ds(last, 1), :] * carry + u_s[pl.ds(last, 1), :]

    lax.fori_loop(0, n_tiles, step, jnp.zeros((1, c), F32))
    return a_s[...] * c_s[...] + u_s[...]


def _shift_rows(x, reverse_src):
    t_all = x.shape[0]
    row = lax.broadcasted_iota(jnp.int32, x.shape, 0)
    if reverse_src:
        return jnp.where(row == t_all - 1, 0.0, pltpu.roll(x, t_all - 1, 0))
    return jnp.where(row == 0, 0.0, pltpu.roll(x, 1, 0))


def _lru_scan(a0, u0, a1, u1, name):
    bsz, t_all, w = a0.shape
    grid = (bsz, w // LANE)
    spec = pl.BlockSpec((None, t_all, LANE), lambda b, c: (b, 0, c))
    scratch = [pltpu.VMEM((t_all, LANE), F32)] * 3
    shape = jax.ShapeDtypeStruct(a0.shape, F32)

    def fwd_body(a0_ref, u0_ref, a1_ref, u1_ref, h0_ref, h1_ref, a_s, u_s, c_s):
        h0_ref[...] = _scan_rows(a0_ref[...], u0_ref[...], False, a_s, u_s, c_s)
        h1_ref[...] = _scan_rows(a1_ref[...], u1_ref[...], True, a_s, u_s, c_s)

    h0, h1 = _pcall(fwd_body, name=name, grid=grid, in_specs=[spec] * 4, out_specs=[spec] * 2,
                    out_shape=[shape] * 2, scratch_shapes=scratch, compiler_params=_cparams(2))(a0, u0, a1, u1)

    def bwd(dh0, dh1):
        def bwd_body(a0_ref, h0_ref, g0_ref, a1_ref, h1_ref, g1_ref, da0_ref, du0_ref, da1_ref, du1_ref,
                     a_s, u_s, c_s):
            g0 = _scan_rows(_shift_rows(a0_ref[...], True), g0_ref[...], True, a_s, u_s, c_s)
            du0_ref[...] = g0
            da0_ref[...] = g0 * _shift_rows(h0_ref[...], False)
            g1 = _scan_rows(_shift_rows(a1_ref[...], False), g1_ref[...], False, a_s, u_s, c_s)
            du1_ref[...] = g1
            da1_ref[...] = g1 * _shift_rows(h1_ref[...], True)

        return _pcall(bwd_body, name=name + "_bwd", grid=grid, in_specs=[spec] * 6, out_specs=[spec] * 4,
                      out_shape=[shape] * 4, scratch_shapes=scratch,
                      compiler_params=_cparams(2))(a0, h0, dh0, a1, h1, dh1)

    return h0, h1, bwd


def _f_mod(x, g, shift, scale):
    return (_rms(x, g, D_MODEL) * (1.0 + scale) + shift,)


def _f_mla_q(cq, ga, w, gh, cos, sa, sb):
    n = _rms(cq, ga, MLA_Q_RANK)
    outs = []
    for wh in _split(w, MLA_HEADS, 1):
        outs.append(_rope(_rms(_nn(n, wh), gh, MLA_QK), cos, sa, sb, MLA_ROPE // 4))
    return (jnp.concatenate(outs, axis=1),)


def _f_mla_kv(ckv, krp, ga, wk, wv, gh, cos, sa, sb):
    n = _rms(ckv, ga, MLA_KV_RANK)
    outs = []
    for wh in _split(wk, MLA_HEADS, 1):
        outs.append(_rope(_rms(_nn(n, wh) + krp, gh, MLA_QK), cos, sa, sb, MLA_ROPE // 4))
    return jnp.concatenate(outs, axis=1), _nn(n, wv)


def _f_conv(x, w0, w1, w2, w3, bias, tc):
    t_all = x.shape[0]
    row = lax.broadcasted_iota(jnp.int32, x.shape, 0)
    lo = jnp.where(row < tc, 0, tc)
    hi = jnp.where(row < tc, tc, t_all)
    y = bias + jnp.zeros_like(x)
    for kk, wk in enumerate((w0, w1, w2, w3)):
        src = row + (kk - 2)
        xs = x if kk == 2 else _roll(x, 2 - kk, 0)
        y = y + wk * jnp.where((src >= lo) & (src < hi), xs, 0.0)
    return (y,)


def _f_gates(xc, w16, b00, b01, b10, b11, sp0, sp1):
    ws = _unstack(w16)
    n_cb = LRU_WIDTH // LANE
    xcs = _split(xc, n_cb, 1)
    bias = [_split(b, n_cb, 1) for b in (b00, b01, b10, b11)]
    sps = [_split(s, n_cb, 1) for s in (sp0, sp1)]
    res = [[], [], [], []]
    for c in range(n_cb):
        for z in range(2):
            r = _sig(_nn(xcs[c], ws[c * 4 + 2 * z]) + bias[2 * z][c])
            i = _sig(_nn(xcs[c], ws[c * 4 + 2 * z + 1]) + bias[2 * z + 1][c])
            la = -LRU_C * r * sps[z][c]
            res[2 * z].append(jnp.exp(la))
            res[2 * z + 1].append(jnp.sqrt(-jnp.tanh(la) * (jnp.exp(2.0 * la) + 1.0)) * (i * xcs[c]))
    return tuple(jnp.concatenate(r, axis=1) for r in res)


def _f_lru_out(h0, h1, lg):
    return ((h0 + h1) * _gelu(lg),)


def _f_swa_qk(sq, sk, gq, gk, cos, sa, sb):
    qs = [_rope(_rms(x, gq, SWA_HEAD_DIM), cos, sa, sb, SWA_HEAD_DIM // 4) for x in _split(sq, SWA_HEADS, 1)]
    ks = [_rope(_rms(x, gk, SWA_HEAD_DIM), cos, sa, sb, SWA_HEAD_DIM // 4) for x in _split(sk, SWA_KV_HEADS, 1)]
    return jnp.concatenate(qs, axis=1), jnp.concatenate(ks, axis=1)


def _f_merge(oa, ob, oc, ga, gb, gc):
    return (jnp.concatenate([_rms(oa, ga, GROUP_WIDTH), _rms(ob, gb, GROUP_WIDTH), _rms(oc, gc, GROUP_WIDTH)],
                            axis=1),)


def _f_resid_mod(x, y, gate, g, shift, scale):
    x1 = x + gate * y
    return x1, _rms(x1, g, D_MODEL) * (1.0 + scale) + shift


def _f_resid(x, y, gate):
    return (x + gate * y,)


def _hosted(hooks, key, arg=None):
    make, done = hooks.get(key, (None, None))
    xchg = make(arg) if make is not None else None
    return xchg, (done if xchg is not None else lambda outs: None)


def _layer(li, x, mods, w, s, tabs, tc, ctx_q, hooks):
    bsz, t_all, _ = x.shape
    n_t = t_all // TB
    grid = (bsz, n_t)
    rows = lambda b, t: (b, t, 0)

    def row(arr, width=None, idx=0, gdtype=F32, gshape=None):
        width = width or arr.shape[-1]
        return _A(arr, (None, TB, width), lambda b, t: (b, t, idx), "row", gdtype=gdtype, gshape=gshape,
                  gimap=rows if gshape is not None else None)

    def out(width, dtype, imap=rows):
        return ((bsz, t_all, width), dtype, (None, TB, width), imap)

    def modarg(arr):
        return _A(arr, (None, None, 1, D_MODEL), lambda b, t: (b, jnp.minimum(t, 1), 0, 0), "acc",
                  first=lambda b, t: t <= 1)

    def tab(arr):
        return _A(arr, (TB, LANE), lambda b, t: (t, 0), "const")

    def pcol(p_all, col, width):
        return row(p_all, width, col // width, gdtype=BF16, gshape=(bsz, t_all, width))

    nm = lambda base: "%s_l%d" % (base, li)
    sh1, sc1, g1, sh2, sc2, g2 = mods
    m_all = bsz * t_all

    (h,), b_mod1 = _rowop(nm("mod1"), _f_mod, grid, [row(x), _par(s["norm1_g"]), modarg(sh1), modarg(sc1)],
                          [out(D_MODEL, BF16)])
    p_all = _mm(h.reshape(m_all, D_MODEL), w["win"], "nn", F32, nm("mm_in")).reshape(bsz, t_all, P_WIDTH)

    tq, tk_ = tabs["mla"], tabs["mla"]
    (q_a,), b_mq = _rowop(nm("mla_q"), _f_mla_q, grid,
                          [pcol(p_all, PC_CQ, 256), _par(s["q_a_g"]), _par(w["wuq"]), _par(s["mla_q_g"])]
                          + [tab(a) for a in tq], [out(MLA_HEADS * LANE, BF16)])
    (k_a, v_a), b_mkv = _rowop(nm("mla_kv"), _f_mla_kv, grid,
                               [pcol(p_all, PC_CKV, 128), pcol(p_all, PC_KR, 128), _par(s["kv_a_g"]), _par(w["wk"]),
                                _par(w["wv"]), _par(s["mla_k_g"])] + [tab(a) for a in tk_],
                               [out(MLA_HEADS * LANE, BF16), out(MLA_HEADS * LANE, BF16)])
    xchg, done = _hosted(hooks, "mla_fwd")
    o_a, got, b_attn_a = _mla_attn(q_a, k_a, v_a, tc, ctx_q, nm("mla_attn"), xchg)
    done(got)

    n_cb = LRU_WIDTH // LANE
    conv_grid = (n_cb, bsz)
    cpar = lambda arr: _A(arr, (1, LANE), lambda c, b: (0, c), "acc", first=lambda c, b: b == 0)
    conv_args = [_A(p_all, (None, t_all, LANE), lambda c, b: (b, 0, PC_LX // LANE + c), "row", gdtype=BF16,
                    gshape=(bsz, t_all, LRU_WIDTH), gimap=lambda c, b: (b, 0, c))]
    conv_args += [cpar(a) for a in s["conv_w"]] + [cpar(s["conv_b"])]
    conv_out = [((bsz, t_all, LRU_WIDTH), F32, (None, t_all, LANE), lambda c, b: (b, 0, c))]
    (xc,), b_conv = _rowop(nm("lru_conv"), functools.partial(_f_conv, tc=tc), conv_grid, conv_args, conv_out)
    rot = lambda b, t: (b, (t + n_t - 1) % n_t, 0)
    (a0, u0, a1, u1), b_gates = _rowop(
        nm("lru_gates"), _f_gates, grid,
        [row(xc), _par(s["wbd"])] + [_par(a) for a in s["gate_b"]] + [_par(a) for a in s["sp"]],
        [out(LRU_WIDTH, F32), out(LRU_WIDTH, F32), out(LRU_WIDTH, F32, rot), out(LRU_WIDTH, F32, rot)])
    h0, h1, b_scan = _lru_scan(a0, u0, a1, u1, nm("lru_scan"))
    h1_arg = _A(h1, (None, TB, LRU_WIDTH), rot, "row")
    (o_b,), b_lout = _rowop(nm("lru_out"), _f_lru_out, grid, [row(h0), h1_arg, pcol(p_all, PC_LG, 512)],
                            [out(LRU_WIDTH, F32)])

    ts = tabs["swa"]
    (q_c, k_c), b_sqk = _rowop(nm("swa_qk"), _f_swa_qk, grid,
                               [pcol(p_all, PC_SQ, 1024), pcol(p_all, PC_SK, 256), _par(s["swa_q_g"]),
                                _par(s["swa_k_g"])] + [tab(a) for a in ts],
                               [out(SWA_HEADS * LANE, BF16), out(SWA_KV_HEADS * LANE, BF16)])
    xchg, done = _hosted(hooks, "swa_fwd")
    o_c, got, b_attn_c = _swa_attn(q_c, k_c, p_all, s["sink_b"], tc, ctx_q, nm("swa_attn"), xchg)
    done(got)

    (y_in,), b_merge = _rowop(nm("merge"), _f_merge, grid,
                              [row(o_a), row(o_b), row(o_c), _par(s["g_a"]), _par(s["g_b"]), _par(s["g_c"])],
                              [out(MIX_P, BF16)])
    y = _mm(y_in.reshape(m_all, MIX_P), w["wout"], "nn", F32, nm("mm_out")).reshape(bsz, t_all, D_MODEL)
    (x1, hm), b_rm = _rowop(nm("resid_mod"), _f_resid_mod, grid,
                            [row(x), row(y, gdtype=BF16), modarg(g1), _par(s["norm2_g"]), modarg(sh2), modarg(sc2)],
                            [out(D_MODEL, F32), out(D_MODEL, BF16)])
    pre, act = _mm(hm.reshape(m_all, D_MODEL), w["ff1"], "nn", BF16, nm("mm_ff1"), epi="sqrelu")
    y2 = _mm(act, w["ff2"], "nn", F32, nm("mm_ff2")).reshape(bsz, t_all, D_MODEL)
    (x2,), b_res = _rowop(nm("resid"), _f_resid, grid, [row(x1), row(y2, gdtype=BF16), modarg(g2)],
                          [out(D_MODEL, F32)])

    def bwd(dx2, hooks):
        dw, ds = {}, {}
        dx1a, dy2, dg2 = b_res(dx2)
        dy2 = dy2.reshape(m_all, D_MODEL)
        dpre = _mm(dy2, w["ff2"], "nt", BF16, nm("mm_ff2_dx"), epi="dsqrelu", aux=pre)
        dw["ff2"] = _mm(act, dy2, "tn", F32, nm("mm_ff2_dw"))
        dhm = _mm(dpre, w["ff1"], "nt", F32, nm("mm_ff1_dx")).reshape(bsz, t_all, D_MODEL)
        dw["ff1"] = _mm(hm.reshape(m_all, D_MODEL), dpre, "tn", F32, nm("mm_ff1_dw"))
        dxa, dy, dg1, ds["norm2_g"], dsh2, dsc2 = b_rm(dx1a, dhm)
        dy = dy.reshape(m_all, D_MODEL)
        dy_in = _mm(dy, w["wout"], "nt", F32, nm("mm_out_dx")).reshape(bsz, t_all, MIX_P)
        dw["wout"] = _mm(y_in.reshape(m_all, MIX_P), dy, "tn", F32, nm("mm_out_dw"))
        do_a, do_b, do_c, ds["g_a"], ds["g_b"], ds["g_c"] = b_merge(dy_in)

        (dq_c, dk_c, dsv, dsink), _ = b_attn_c(do_c)
        ds["sink_b"] = jnp.sum(dsink, axis=0)
        dsq, dsk, ds["swa_q_g"], ds["swa_k_g"] = b_sqk(dq_c, dk_c)

        dh0, dh1, dlg = b_lout(do_b)
        da0, du0, da1, du1 = b_scan(dh0, dh1)
        gates_g = b_gates(da0, du0, da1, du1)
        dxc, ds["wbd"] = gates_g[0], gates_g[1]
        ds["gate_b"], ds["sp"] = list(gates_g[2:6]), list(gates_g[6:8])
        conv_g = b_conv(dxc)
        dlx, ds["conv_w"], ds["conv_b"] = conv_g[0], list(conv_g[1:5]), conv_g[5]

        xchg, done = _hosted(hooks, "mla_bwd", dw)
        (dq_a, dk_a, dv_a), got = b_attn_a(do_a, xchg)
        done(got)
        dcq, ds["q_a_g"], dw["wuq"], ds["mla_q_g"] = b_mq(dq_a)
        dckv, dkr, ds["kv_a_g"], dw["wk"], dw["wv"], ds["mla_k_g"] = b_mkv(dk_a, dv_a)

        dp = jnp.concatenate([dsq, dlx, dlg, dcq, dsk, dsv.astype(BF16), dckv, dkr], axis=-1)
        dp = dp.reshape(m_all, P_WIDTH)
        dh = _mm(dp, w["win"], "nt", F32, nm("mm_in_dx")).reshape(bsz, t_all, D_MODEL)
        dw["win"] = _mm(h.reshape(m_all, D_MODEL), dp, "tn", F32, nm("mm_in_dw"))
        dxb, ds["norm1_g"], dsh1, dsc1 = b_mod1(dh)
        return dxa + dxb, [dsh1, dsc1, dg1, dsh2, dsc2, dg2], dw, ds

    return x2, bwd


def _loss_and_grad(x2, target, tc):
    bsz, t_all, d = x2.shape
    n_t = t_all // TB
    n_c = tc // TB

    def body(x_ref, t_ref, l_ref, dx_ref):
        b, t = pl.program_id(0), pl.program_id(1)

        @pl.when((b == 0) & (t == 0))
        def _():
            l_ref[...] = jnp.zeros_like(l_ref)

        @pl.when(t < n_c)
        def _():
            dx_ref[...] = jnp.zeros_like(dx_ref)

        @pl.when(t >= n_c)
        def _():
            e = x_ref[...] - t_ref[...]
            dx_ref[...] = e * (1.0 / d)
            l_ref[...] += jnp.sum(e * e) * (0.5 / d)

    loss, dx = _pcall(
        body, name="loss", grid=(bsz, n_t),
        in_specs=[pl.BlockSpec((None, TB, d), lambda b, t: (b, t, 0)),
                  pl.BlockSpec((None, TB, d), lambda b, t: (b, jnp.maximum(t - n_c, 0), 0))],
        out_specs=[pl.BlockSpec((SUBLANE, LANE), lambda b, t: (0, 0)),
                   pl.BlockSpec((None, TB, d), lambda b, t: (b, t, 0))],
        out_shape=[jax.ShapeDtypeStruct((SUBLANE, LANE), F32), jax.ShapeDtypeStruct(x2.shape, F32)],
        compiler_params=_cparams(2))(x2, target)
    return loss[0, 0], dx


def _rope_tables(lat, tc, dim, lane0):
    quarter = dim // 4
    pos = jnp.arange(lat)
    grid_pos = jnp.stack([pos // GRID_W, pos % GRID_W], axis=-1).astype(F32)
    lane = jnp.arange(LANE)
    p = jnp.clip(lane - lane0, 0, dim - 1)
    active = (lane >= lane0) & (lane < lane0 + dim)
    axis, half, qi = p // (dim // 2), (p % (dim // 2)) // quarter, p % quarter
    inv = ROPE_THETA ** (-qi.astype(F32) / quarter)
    ang = jnp.where(axis[None, :] == 0, grid_pos[:, 0:1], grid_pos[:, 1:2]) * inv[None, :]
    cos = jnp.where(active, jnp.cos(ang), 1.0)
    sin = jnp.where(active, jnp.sin(ang), 0.0)
    sa = jnp.where(half == 0, -sin, 0.0)
    sb = jnp.where(half == 1, sin, 0.0)
    ctx1, ctx0 = jnp.ones((tc, LANE), F32), jnp.zeros((tc, LANE), F32)
    return (jnp.concatenate([ctx1, cos], 0), jnp.concatenate([ctx0, sa], 0), jnp.concatenate([ctx0, sb], 0))


_BIG = {"w_in": ((D_MODEL, IN_WIDTH // N_DEV), 1, ("win",)),
        "w_uq": ((MLA_Q_RANK, MLA_HEADS * MLA_QK // N_DEV), 1, ("wuq",)),
        "w_ukv": ((MLA_KV_RANK, MLA_HEADS * (MLA_NOPE + MLA_V) // N_DEV), 1, ("wk", "wv")),
        "w_out": ((3 * GROUP_WIDTH // N_DEV, D_MODEL), 0, ("wout",)),
        "w_ff1": ((D_MODEL, D_FF // N_DEV), 1, ("ff1",)),
        "w_ff2": ((D_FF // N_DEV, D_MODEL), 0, ("ff2",))}
_EARLY = ("w_in", "w_uq", "w_ukv")
_LATE = ("w_out", "w_ff1", "w_ff2")


def _pad_heads(wm, n_heads, dim, axis=-1):
    axis = axis % wm.ndim
    shp = wm.shape[:axis] + (n_heads, dim) + wm.shape[axis + 1:]
    pad = [(0, 0)] * len(shp)
    pad[axis + 1] = (0, LANE - dim)
    out = jnp.pad(wm.reshape(shp), pad)
    return out.reshape(wm.shape[:axis] + (n_heads * LANE,) + wm.shape[axis + 1:])


def _prep_weight(name, piece):
    shp, ax, _ = _BIG[name]
    full = jnp.moveaxis(piece, 0, ax).reshape(shp[:ax] + (N_DEV * shp[ax],) + shp[ax + 1:])
    if name == "w_in":
        cq, ckv, kr, lx, lg, sq, sk, sv = _split_cols(full)
        return {"win": jnp.concatenate(
            [_pad_heads(sq, SWA_HEADS, SWA_HEAD_DIM), lx, lg, cq, _pad_heads(sk, SWA_KV_HEADS, SWA_HEAD_DIM),
             _pad_heads(sv, SWA_KV_HEADS, SWA_HEAD_DIM), ckv, jnp.pad(kr, ((0, 0), (MLA_NOPE, LANE - MLA_QK)))], axis=1)}
    if name == "w_uq":
        return {"wuq": _pad_heads(full, MLA_HEADS, MLA_QK)}
    if name == "w_ukv":
        ukv = full.reshape(MLA_KV_RANK, MLA_HEADS, MLA_NOPE + MLA_V)
        return {"wk": _pad_heads(ukv[:, :, :MLA_NOPE].reshape(MLA_KV_RANK, -1), MLA_HEADS, MLA_NOPE),
                "wv": _pad_heads(ukv[:, :, MLA_NOPE:].reshape(MLA_KV_RANK, -1), MLA_HEADS, MLA_V)}
    if name == "w_out":
        return {"wout": jnp.concatenate(
            [_pad_heads(full[:GROUP_WIDTH], MLA_HEADS, MLA_V, axis=0), full[GROUP_WIDTH:2 * GROUP_WIDTH],
             _pad_heads(full[2 * GROUP_WIDTH:], SWA_HEADS, SWA_HEAD_DIM, axis=0)], axis=0)}
    return {_BIG[name][2][0]: full}


def _split_cols(wm):
    parts, start = [], 0
    for size in IN_SIZES:
        parts.append(wm[:, start:start + size])
        start += size
    return parts


def _prep_small(raw):
    r1 = lambda a: a.reshape(1, -1)
    gw = raw["lru_gate_w"].reshape(2, 2, 4, 2, 64, 64)
    wbd = jnp.einsum("zgknCm,nN->knCzgNm", gw, jnp.eye(2, dtype=F32)).reshape(4, LANE, 4, LANE)
    gg = raw["group_g"]
    sink = raw["swa_sink"].reshape(SWA_KV_HEADS, SWA_GROUP, 1, 1)
    return {
        "norm1_g": r1(raw["norm1_g"]), "norm2_g": r1(raw["norm2_g"]),
        "q_a_g": r1(raw["q_a_g"]), "kv_a_g": r1(raw["kv_a_g"]),
        "mla_q_g": jnp.pad(r1(raw["mla_q_g"]), ((0, 0), (0, LANE - MLA_QK))),
        "mla_k_g": jnp.pad(r1(raw["mla_k_g"]), ((0, 0), (0, LANE - MLA_QK))),
        "swa_q_g": jnp.pad(r1(raw["swa_q_g"]), ((0, 0), (0, LANE - SWA_HEAD_DIM))),
        "swa_k_g": jnp.pad(r1(raw["swa_k_g"]), ((0, 0), (0, LANE - SWA_HEAD_DIM))),
        "conv_w": [r1(raw["conv_w"][kk]) for kk in range(4)], "conv_b": r1(raw["conv_b"]),
        "wbd": wbd.transpose(0, 2, 1, 3).reshape(16, LANE, LANE),
        "gate_b": [r1(raw["lru_gate_b"][z, g]) for z in range(2) for g in range(2)],
        "sp": [r1(jax.nn.softplus(-raw["lru_lambda"][z])) for z in range(2)],
        "sink_b": jnp.broadcast_to(sink, (SWA_KV_HEADS, SWA_GROUP, QB_SWA, LANE)).reshape(
            SWA_KV_HEADS, SWA_GROUP * QB_SWA, LANE),
        "g_a": _pad_heads(r1(gg[:GROUP_WIDTH]), MLA_HEADS, MLA_V), "g_b": r1(gg[GROUP_WIDTH:2 * GROUP_WIDTH]),
        "g_c": _pad_heads(r1(gg[2 * GROUP_WIDTH:]), SWA_HEADS, SWA_HEAD_DIM)}


def _mesh_pos():
    return lax.axis_index("x"), lax.axis_index("y"), lax.axis_index("c")


def _peer(pos, k):
    return tuple(1 - p if (k >> s) & 1 else p for p, s in zip(pos, (2, 1, 0)))


def _dev_index(pos):
    return 4 * pos[0] + 2 * pos[1] + pos[2]


class _Exchange:
    def __init__(self, bufs, gather):
        self.bufs = list(bufs)
        self.n = len(self.bufs)
        self.gather = [gather] * self.n if isinstance(gather, bool) else list(gather)
        self.specs = [pl.BlockSpec(memory_space=pl.ANY)] * self.n
        self.out_shape = [jax.ShapeDtypeStruct((N_DEV,) + tuple(b.shape if g else b.shape[1:]), b.dtype)
                          for b, g in zip(self.bufs, self.gather)]
        self.scratch = [pltpu.SemaphoreType.DMA(((N_DEV - 1) * self.n,)),
                        pltpu.SemaphoreType.DMA(((N_DEV - 1) * self.n,)), pltpu.SemaphoreType.DMA((self.n,))]

    def _copies(self, x_refs, o_refs, sems, with_recvs):
        send_sems, recv_sems, local_sems = sems
        pos = _mesh_pos()
        me = _dev_index(pos)
        locals_, sends, recvs = [], [], []
        for j in range(self.n):
            src_mine = x_refs[j] if self.gather[j] else x_refs[j].at[me]
            locals_.append(pltpu.make_async_copy(src_mine, o_refs[j].at[me], local_sems.at[j]))
        for k in range(1, N_DEV):
            peer = _peer(pos, k)
            pidx = _dev_index(peer)
            for j in range(self.n):
                src = x_refs[j] if self.gather[j] else x_refs[j].at[pidx]
                sem = (k - 1) * self.n + j
                sends.append(pltpu.make_async_remote_copy(
                    src_ref=src, dst_ref=o_refs[j].at[me], send_sem=send_sems.at[sem], recv_sem=recv_sems.at[sem],
                    device_id=peer, device_id_type=pl.DeviceIdType.MESH))
                if with_recvs:
                    recvs.append(pltpu.make_async_remote_copy(
                        src_ref=src, dst_ref=o_refs[j].at[pidx], send_sem=send_sems.at[sem],
                        recv_sem=recv_sems.at[sem], device_id=peer, device_id_type=pl.DeviceIdType.MESH))
        return locals_, sends, recvs

    def start(self, x_refs, o_refs, sems):
        locals_, sends, _ = self._copies(x_refs, o_refs, sems, False)
        for cp in locals_ + sends:
            cp.start()

    def wait(self, x_refs, o_refs, sems):
        locals_, sends, recvs = self._copies(x_refs, o_refs, sems, True)
        for cp in recvs:
            cp.wait_recv()
        for cp in sends:
            cp.wait_send()
        for cp in locals_:
            cp.wait()


def _exchange(bufs, gather, name):
    xchg = _Exchange(bufs, gather)
    n = xchg.n

    def body(*refs):
        xchg.start(refs[:n], refs[n:2 * n], refs[2 * n:])
        xchg.wait(refs[:n], refs[n:2 * n], refs[2 * n:])

    return _pcall(body, name=name, out_shape=xchg.out_shape, in_specs=xchg.specs, out_specs=xchg.specs,
                  scratch_shapes=xchg.scratch)(*xchg.bufs)


def _pack(arrs, dtype):
    flat = jnp.concatenate([a.reshape(-1).astype(dtype) for a in arrs])
    rows = -(-flat.size // PACK_W)
    rows = -(-rows // 16) * 16
    return jnp.pad(flat, (0, rows * PACK_W - flat.size)).reshape(rows, PACK_W)


def _unpack(buf, shapes, lead=()):
    flat = buf.reshape(lead + (-1,))
    out, off = [], 0
    for shp in shapes:
        n = math.prod(shp)
        out.append(flat[..., off:off + n].reshape(lead + tuple(shp)))
        off += n
    return out


def _sum_sources(buf, name):
    _, r, c = buf.shape
    tr = _rows_tile(r)

    def body(x_ref, o_ref):
        acc = x_ref[0]
        for d in range(1, N_DEV):
            acc = acc + x_ref[d]
        o_ref[...] = acc

    return _pcall(body, name=name, grid=(r // tr,),
                  in_specs=[pl.BlockSpec((N_DEV, tr, c), lambda i: (0, i, 0))],
                  out_specs=pl.BlockSpec((tr, c), lambda i: (i, 0)),
                  out_shape=jax.ShapeDtypeStruct((r, c), F32), compiler_params=_cparams(1))(buf)


def _rows_tile(r):
    best = r
    for t in range(SUBLANE, 257, SUBLANE):
        if r % t == 0:
            best = t
    return best


def _adamw(grads, wgt, m, v, name):
    n_src, r, c = grads.shape
    tr = _rows_tile(r)
    bc1 = 1.0 - ADAM_B1 ** ADAM_STEP
    bc2 = 1.0 - ADAM_B2 ** ADAM_STEP

    def body(g_ref, w_ref, m_ref, v_ref, go_ref, d_ref, mo_ref, vo_ref):
        g = g_ref[0].astype(F32)
        for d in range(1, n_src):
            g = g + g_ref[d].astype(F32)
        m_new = ADAM_B1 * m_ref[...] + (1.0 - ADAM_B1) * g
        v_new = ADAM_B2 * v_ref[...] + (1.0 - ADAM_B2) * (g * g)
        go_ref[...] = g
        mo_ref[...] = m_new
        vo_ref[...] = v_new
        d_ref[...] = -ADAM_LR * ((m_new / bc1) / (jnp.sqrt(v_new / bc2) + ADAM_EPS) + ADAM_WD * w_ref[...])

    spec = pl.BlockSpec((tr, c), lambda i: (i, 0))
    return _pcall(body, name=name, grid=(r // tr,),
                  in_specs=[pl.BlockSpec((n_src, tr, c), lambda i: (0, i, 0)), spec, spec, spec],
                  out_specs=[spec] * 4, out_shape=[jax.ShapeDtypeStruct((r, c), F32)] * 4,
                  compiler_params=_cparams(1))(grads, wgt, m, v)


def _silu(z):
    return z * jax.nn.sigmoid(z)


_WEIGHTS = ("c_ctx", "w_mod", "b_mod", "norm1_g", "w_in", "q_a_g", "w_uq", "kv_a_g", "w_ukv", "mla_q_g", "mla_k_g",
            "conv_w", "conv_b", "lru_gate_w", "lru_gate_b", "lru_lambda", "swa_q_g", "swa_k_g", "swa_sink", "group_g",
            "w_out", "norm2_g", "w_ff1", "w_ff2")
_SHARDED_SMALL = ("conv_w", "lru_gate_b", "lru_lambda")
_REPL_RAW = ("norm1_g", "q_a_g", "kv_a_g", "mla_q_g", "mla_k_g", "conv_b", "lru_gate_w", "swa_q_g", "swa_k_g",
             "swa_sink", "group_g", "norm2_g")
MOD_ROWS = 32


def kernel(x, c, ctx, c_ctx, w_mod, b_mod, norm1_g, w_in, q_a_g, w_uq, kv_a_g, w_ukv, mla_q_g, mla_k_g, conv_w, conv_b, lru_gate_w, lru_gate_b, lru_lambda, swa_q_g, swa_k_g, swa_sink, group_g, w_out, norm2_g, w_ff1, w_ff2, loss_target, m_c_ctx, m_w_mod, m_b_mod, m_norm1_g, m_w_in, m_q_a_g, m_w_uq, m_kv_a_g, m_w_ukv, m_mla_q_g, m_mla_k_g, m_conv_w, m_conv_b, m_lru_gate_w, m_lru_gate_b, m_lru_lambda, m_swa_q_g, m_swa_k_g, m_swa_sink, m_group_g, m_w_out, m_norm2_g, m_w_ff1, m_w_ff2, v_c_ctx, v_w_mod, v_b_mod, v_norm1_g, v_w_in, v_q_a_g, v_w_uq, v_kv_a_g, v_w_ukv, v_mla_q_g, v_mla_k_g, v_conv_w, v_conv_b, v_lru_gate_w, v_lru_gate_b, v_lru_lambda, v_swa_q_g, v_swa_k_g, v_swa_sink, v_group_g, v_w_out, v_norm2_g, v_w_ff1, v_w_ff2):
    wts = dict(c_ctx=c_ctx, w_mod=w_mod, b_mod=b_mod, norm1_g=norm1_g, w_in=w_in, q_a_g=q_a_g, w_uq=w_uq,
               kv_a_g=kv_a_g, w_ukv=w_ukv, mla_q_g=mla_q_g, mla_k_g=mla_k_g, conv_w=conv_w, conv_b=conv_b,
               lru_gate_w=lru_gate_w, lru_gate_b=lru_gate_b, lru_lambda=lru_lambda, swa_q_g=swa_q_g, swa_k_g=swa_k_g,
               swa_sink=swa_sink, group_g=group_g, w_out=w_out, norm2_g=norm2_g, w_ff1=w_ff1, w_ff2=w_ff2)
    mom1 = dict(zip(_WEIGHTS, (m_c_ctx, m_w_mod, m_b_mod, m_norm1_g, m_w_in, m_q_a_g, m_w_uq, m_kv_a_g, m_w_ukv,
                               m_mla_q_g, m_mla_k_g, m_conv_w, m_conv_b, m_lru_gate_w, m_lru_gate_b, m_lru_lambda,
                               m_swa_q_g, m_swa_k_g, m_swa_sink, m_group_g, m_w_out, m_norm2_g, m_w_ff1, m_w_ff2)))
    mom2 = dict(zip(_WEIGHTS, (v_c_ctx, v_w_mod, v_b_mod, v_norm1_g, v_w_in, v_q_a_g, v_w_uq, v_kv_a_g, v_w_ukv,
                               v_mla_q_g, v_mla_k_g, v_conv_w, v_conv_b, v_lru_gate_w, v_lru_gate_b, v_lru_lambda,
                               v_swa_q_g, v_swa_k_g, v_swa_sink, v_group_g, v_w_out, v_norm2_g, v_w_ff1, v_w_ff2)))
    bsz = x.shape[0]
    n_ex = bsz * N_DEV
    me = _dev_index(_mesh_pos())
    mod_cols = w_mod.shape[-1]

    small_shapes = [c.shape, conv_w.shape, lru_gate_b.shape, lru_lambda.shape]
    (g_small,) = _exchange([_pack([c, conv_w, lru_gate_b, lru_lambda], F32)], True, "ag_small")
    c_all, conv_w_all, gate_b_all, lam_all = _unpack(g_small, small_shapes, lead=(N_DEV,))
    c_all = c_all.reshape(n_ex, D_MODEL)
    cat_last = lambda a: jnp.moveaxis(a, 0, -2).reshape(a.shape[1:-1] + (N_DEV * a.shape[-1],))
    conv_w_full, gate_b_full, lam_full = cat_last(conv_w_all), cat_last(gate_b_all), cat_last(lam_all)

    act = jnp.zeros((MOD_ROWS, D_MODEL), F32).at[:n_ex].set(_silu(c_all)).at[n_ex].set(_silu(c_ctx))
    mod_part = jnp.concatenate([_mm(act, w_mod[li], "nn", F32, "mm_mod_l%d" % li) for li in range(DEPTH)], axis=1)
    (mod_all,) = _exchange([mod_part], True, "ag_mod")
    mods = []
    for li in range(DEPTH):
        full = jnp.moveaxis(mod_all[:, :, li * mod_cols:(li + 1) * mod_cols], 0, 1).reshape(MOD_ROWS, -1) + b_mod[li]
        mine = lax.dynamic_slice_in_dim(full, me * bsz, bsz, axis=0)
        ctx_row = jnp.broadcast_to(full[n_ex], mine.shape)
        both = jnp.stack([ctx_row, mine], axis=1).reshape(bsz, 2, N_MOD, 1, D_MODEL)
        mods.append([both[:, :, j] for j in range(N_MOD)])

    raw = {n: wts[n] for n in _REPL_RAW}
    raw.update(conv_w=conv_w_full, lru_gate_b=gate_b_full, lru_lambda=lam_full)
    small_names = list(_REPL_RAW) + list(_SHARDED_SMALL)
    sp, small_vjp = [None] * DEPTH, [None] * DEPTH
    for li in range(DEPTH):
        sp[li], small_vjp[li] = jax.vjp(_prep_small, {n: raw[n][li] for n in small_names})

    w, w_vjp, g_recv, small_recv = [{} for _ in range(DEPTH)], {}, {}, {}
    shard = lambda n, li: wts[n][li].astype(BF16)

    def take(li, names, pieces):
        for n, piece in zip(names, pieces):
            out, w_vjp[n, li] = jax.vjp(functools.partial(_prep_weight, n), piece)
            w[li].update(out)

    def gather_hook(li, names):
        return (lambda _: _Exchange([shard(n, li) for n in names], True), lambda got: take(li, names, got))

    def wgrad(n, li, dwl):
        (g,) = w_vjp[n, li]({k: dwl[k].astype(BF16) for k in _BIG[n][2]})
        return g

    def small_pack(li, ds_l, extra=()):
        (d_raw,) = small_vjp[li](ds_l)
        return _pack([d_raw[n] for n in small_names] + list(extra), F32)

    take(0, _EARLY, _exchange([shard(n, 0) for n in _EARLY], True, "ag_early"))
    hooks_fwd = [{"mla_fwd": gather_hook(0, _LATE), "swa_fwd": gather_hook(1, _EARLY + ("w_out",))},
                 {"mla_fwd": gather_hook(1, ("w_ff1", "w_ff2"))}]
    bwd_state = {}

    def scatter_last_layer(dwl):
        return _Exchange([wgrad(n, 1, dwl) for n in _LATE], False)

    def scatter_first_layer(dwl):
        dw1, ds1 = bwd_state["dw1"], bwd_state["ds1"]
        bufs = [wgrad(n, 1, dw1) for n in _EARLY] + [wgrad(n, 0, dwl) for n in _LATE] + [small_pack(1, ds1)]
        return _Exchange(bufs, [False] * (len(_EARLY) + len(_LATE)) + [True])

    def scattered_first_layer(got):
        g_recv.update(zip([(n, 1) for n in _EARLY] + [(n, 0) for n in _LATE], got[:-1]))
        small_recv[1] = got[-1]

    hooks_bwd = [{"mla_bwd": (scatter_first_layer, scattered_first_layer)},
                 {"mla_bwd": (scatter_last_layer, lambda got: g_recv.update(zip([(n, 1) for n in _LATE], got)))}]

    tc, lat = ctx.shape[1], x.shape[1]
    tabs = {"mla": _rope_tables(lat, tc, MLA_ROPE, MLA_NOPE), "swa": _rope_tables(lat, tc, SWA_HEAD_DIM, 0)}
    stream = jnp.concatenate([ctx, x], axis=1)
    bwds = []
    for li in range(DEPTH):
        stream, bwd = _layer(li, stream, mods[li], w[li], sp[li], tabs, tc, li < DEPTH - 1, hooks_fwd[li])
        bwds.append(bwd)
    loss_part, dstream = _loss_and_grad(stream, loss_target, tc)
    dmods = [None] * DEPTH
    dstream, dmods[1], bwd_state["dw1"], bwd_state["ds1"] = bwds[1](dstream, hooks_bwd[1])
    dstream, dmods[0], dw0, ds0 = bwds[0](dstream, hooks_bwd[0])
    grad_x = dstream[:, tc:]

    dm_rows = []
    for li in range(DEPTH):
        dm = jnp.concatenate(dmods[li], axis=-1)
        dm_rows.append(jnp.concatenate([dm[:, 1, 0], jnp.sum(dm[:, 0, 0], axis=0, keepdims=True)], axis=0))
    dm_mine = jnp.concatenate(dm_rows, axis=1)
    dm_mine = jnp.pad(dm_mine, ((0, SUBLANE - bsz - 1), (0, 0)))
    (dm_all,) = _exchange([dm_mine], True, "ag_dmod")
    g_wmod, g_bmod, dact_ctx = [], [], jnp.zeros((D_MODEL,), F32)
    for li in range(DEPTH):
        part = dm_all[:, :, li * N_MOD * D_MODEL:(li + 1) * N_MOD * D_MODEL]
        dm32 = jnp.zeros((MOD_ROWS, N_MOD * D_MODEL), F32).at[:n_ex].set(part[:, :bsz].reshape(n_ex, -1))
        dm32 = dm32.at[n_ex].set(jnp.sum(part[:, bsz], axis=0))
        g_bmod.append(jnp.sum(dm32, axis=0))
        cols = lax.dynamic_slice_in_dim(dm32, me * mod_cols, mod_cols, axis=1)
        g_wmod.append(_mm(act, cols, "tn", F32, "mm_mod_dw_l%d" % li))
        dact_ctx = dact_ctx + _mm(cols, w_mod[li], "nt", F32, "mm_mod_dx_l%d" % li)[n_ex]
    sg = jax.nn.sigmoid(c_ctx)
    g_cctx_part = dact_ctx * (sg * (1.0 + c_ctx * (1.0 - sg)))

    last = _exchange([wgrad(n, 0, dw0) for n in _EARLY] + [small_pack(0, ds0, (g_cctx_part, loss_part.reshape(1)))],
                     [False] * len(_EARLY) + [True], "rs_early")
    g_recv.update(zip([(n, 0) for n in _EARLY], last[:-1]))
    small_recv[0] = last[-1]
    layer_shapes = [raw[n].shape[1:] for n in small_names]
    tot = [_unpack(_sum_sources(small_recv[li], "sum_grads_l%d" % li), layer_shapes + [(D_MODEL,), (1,)][:2 * (li == 0)])
           for li in range(DEPTH)]
    grads = {n: jnp.stack([tot[li][j] for li in range(DEPTH)], axis=0) for j, n in enumerate(small_names)}
    grads["c_ctx"], loss = tot[0][-2], tot[0][-1][0]
    for n in _SHARDED_SMALL:
        width = wts[n].shape[-1]
        grads[n] = lax.dynamic_slice_in_dim(grads[n], me * width, width, axis=grads[n].ndim - 1)
    grads["b_mod"] = jnp.stack(g_bmod, axis=0)

    delta, new_m, new_v = {}, {}, {}
    for n, (shp, _, _) in _BIG.items():
        two_d = (DEPTH * shp[0], shp[1])
        src = jnp.stack([g_recv[n, li] for li in range(DEPTH)], axis=1).reshape((N_DEV,) + two_d)
        res = _adamw(src, wts[n].reshape(two_d), mom1[n].reshape(two_d), mom2[n].reshape(two_d), "adamw_" + n)
        grads[n], delta[n], new_m[n], new_v[n] = [r.reshape(wts[n].shape) for r in res]
    two_d = (DEPTH * D_MODEL, mod_cols)
    res = _adamw(jnp.stack(g_wmod, axis=0).reshape((1,) + two_d), w_mod.reshape(two_d), mom1["w_mod"].reshape(two_d),
                 mom2["w_mod"].reshape(two_d), "adamw_w_mod")
    grads["w_mod"], delta["w_mod"], new_m["w_mod"], new_v["w_mod"] = [r.reshape(w_mod.shape) for r in res]
    rest = [n for n in _WEIGHTS if n not in delta]
    shapes = [wts[n].shape for n in rest]
    res = _adamw(_pack([grads[n] for n in rest], F32)[None], _pack([wts[n] for n in rest], F32),
                 _pack([mom1[n] for n in rest], F32), _pack([mom2[n] for n in rest], F32), "adamw_small")
    for tgt, buf in zip((delta, new_m, new_v), res[1:]):
        tgt.update(zip(rest, _unpack(buf, shapes)))

    return (loss, grad_x, *[grads[n] for n in _WEIGHTS], *[delta[n] for n in _WEIGHTS],
            *[new_m[n] for n in _WEIGHTS], *[new_v[n] for n in _WEIGHTS])
```

```python
import functools
import math

import jax
import jax.numpy as jnp
from jax import lax
from jax.experimental import pallas as pl
from jax.experimental.pallas import tpu as pltpu

F32, BF16 = jnp.float32, jnp.bfloat16

N_DEV = 8
DEPTH = 2
D_MODEL = 1024
D_FF = 4096
N_MOD = 6
GRID_W = 64
WINDOW = 128
ROPE_THETA = 10000.0
EPS = 1e-6
NEG_INF = -1e30
LRU_C = 8.0
LRU_WIDTH = 512
MLA_HEADS, MLA_NOPE, MLA_ROPE, MLA_V = 8, 64, 32, 64
MLA_QK = MLA_NOPE + MLA_ROPE
MLA_Q_RANK, MLA_KV_RANK = 256, 128
SWA_HEADS, SWA_KV_HEADS, SWA_GROUP, SWA_HEAD_DIM = 8, 2, 4, 64
GROUP_WIDTH = 512
IN_SIZES = (256, 128, 32, 512, 512, 512, 128, 128)
IN_WIDTH = sum(IN_SIZES)
ADAM_LR, ADAM_B1, ADAM_B2, ADAM_EPS, ADAM_WD, ADAM_STEP = 0.001, 0.9, 0.999, 1e-08, 0.01, 10

LANE = 128
SUBLANE = 8
TB = 256
QB_SWA = 128
PACK_W = 1024
MM_K_CAP = 4608
MLA_HPS = 2
VMEM_LIMIT = 56 * 1024 * 1024
P_WIDTH = 3072
PC_SQ, PC_LX, PC_LG, PC_CQ, PC_SK, PC_SV, PC_CKV, PC_KR = 0, 1024, 1536, 2048, 2304, 2560, 2816, 2944
MIX_P = 2560


def _pcall(body, **kw):
    return pl.pallas_call(body, **kw)


def _cparams(n_grid):
    return pltpu.CompilerParams(dimension_semantics=("arbitrary",) * n_grid, vmem_limit_bytes=VMEM_LIMIT)


def _dg(a, b, ca, cb):
    return lax.dot_general(a.astype(BF16), b.astype(BF16), (((ca,), (cb,)), ((), ())),
                           preferred_element_type=F32)


@jax.custom_vjp
def _nn(a, b):
    return _dg(a, b, 1, 0)


@jax.custom_vjp
def _nt(a, b):
    return _dg(a, b, 1, 1)


@jax.custom_vjp
def _tn(a, b):
    return _dg(a, b, 0, 0)


_nn.defvjp(lambda a, b: (_nn(a, b), (a, b)), lambda r, ct: (_nt(ct, r[1]), _tn(r[0], ct)))
_nt.defvjp(lambda a, b: (_nt(a, b), (a, b)), lambda r, ct: (_nn(ct, r[1]), _tn(ct, r[0])))
_tn.defvjp(lambda a, b: (_tn(a, b), (a, b)), lambda r, ct: (_nt(r[1], ct), _nn(r[0], ct)))


@functools.partial(jax.custom_vjp, nondiff_argnums=(1, 2))
def _roll(x, shift, axis):
    return pltpu.roll(x, shift % x.shape[axis], axis)


_roll.defvjp(lambda x, shift, axis: (_roll(x, shift, axis), None),
             lambda shift, axis, _, ct: (_roll(ct, -shift, axis),))


@functools.partial(jax.custom_vjp, nondiff_argnums=(1, 2))
def _split(x, n, axis):
    w = x.shape[axis] // n
    return tuple(lax.slice_in_dim(x, i * w, (i + 1) * w, axis=axis) for i in range(n))


_split.defvjp(lambda x, n, axis: (_split(x, n, axis), None),
              lambda n, axis, _, cts: (jnp.concatenate(cts, axis=axis),))


@jax.custom_vjp
def _unstack(x):
    return tuple(x[i] for i in range(x.shape[0]))


_unstack.defvjp(lambda x: (_unstack(x), None), lambda _, cts: (jnp.stack(cts, axis=0),))


def _sig(x):
    return 0.5 * (jnp.tanh(0.5 * x) + 1.0)


def _gelu(x):
    return 0.5 * x * (1.0 + jnp.tanh(math.sqrt(2.0 / math.pi) * (x + 0.044715 * (x * x * x))))


def _rms(x, g, n):
    ms = jnp.sum(x * x, axis=-1, keepdims=True) * (1.0 / n)
    return x * lax.rsqrt(ms + EPS) * g


def _rope(y, cos, sa, sb, quarter):
    return y * cos + _roll(y, -quarter, 1) * sa + _roll(y, quarter, 1) * sb


def _softmax_rows(s, extra=None):
    m = jnp.max(s, axis=-1, keepdims=True)
    if extra is not None:
        m = jnp.maximum(m, extra)
    m = lax.stop_gradient(m)
    e = jnp.exp(s - m)
    den = jnp.sum(e, axis=-1, keepdims=True)
    if extra is not None:
        den = den + jnp.exp(extra - m)
    return e / den


class _A:
    def __init__(self, arr, block, imap, kind="row", first=None, gdtype=F32, gshape=None, gimap=None):
        self.arr, self.block, self.imap, self.kind, self.first = arr, block, imap, kind, first
        self.gdtype, self.gshape, self.gimap = gdtype, gshape, gimap


def _all_zero(*ids):
    return functools.reduce(jnp.logical_and, [i == 0 for i in ids])


def _par(arr):
    nd = arr.ndim
    return _A(arr, arr.shape, lambda *ids: (0,) * nd, "acc", first=_all_zero)


def _op_fwd(name, fn, grid, args, outs):
    n_in = len(args)

    def body(*refs):
        vals = [r[...].astype(F32) for r in refs[:n_in]]
        for r, v in zip(refs[n_in:], fn(*vals)):
            r[...] = v.astype(r.dtype)

    return _pcall(
        body, name=name, grid=grid,
        in_specs=[pl.BlockSpec(a.block, a.imap) for a in args],
        out_specs=[pl.BlockSpec(o[2], o[3]) for o in outs],
        out_shape=[jax.ShapeDtypeStruct(o[0], o[1]) for o in outs],
        compiler_params=_cparams(len(grid)),
    )(*[a.arr for a in args])


def _op_bwd(name, fn, grid, args, outs, ct_arrays):
    n_in, n_ct = len(args), len(outs)
    didx = [i for i, a in enumerate(args) if a.kind != "const"]

    def body(*refs):
        ids = [pl.program_id(i) for i in range(len(grid))]
        vals = [r[...].astype(F32) for r in refs[:n_in]]

        def g(*dv):
            full = list(vals)
            for i, v in zip(didx, dv):
                full[i] = v
            return tuple(fn(*full))

        _, vjp = jax.vjp(g, *[vals[i] for i in didx])
        grads = vjp(tuple(r[...].astype(F32) for r in refs[n_in:n_in + n_ct]))
        for gr, i, r in zip(grads, didx, refs[n_in + n_ct:]):
            a = args[i]
            if a.kind == "row":
                r[...] = gr.astype(r.dtype)
            else:
                first = a.first(*ids)

                @pl.when(first)
                def _():
                    r[...] = gr

                @pl.when(jnp.logical_not(first))
                def _():
                    r[...] += gr

    g_specs, g_shapes = [], []
    for i in didx:
        a = args[i]
        if a.kind == "row":
            g_specs.append(pl.BlockSpec(a.block, a.gimap or a.imap))
            g_shapes.append(jax.ShapeDtypeStruct(a.gshape or a.arr.shape, a.gdtype))
        else:
            g_specs.append(pl.BlockSpec(a.block, a.imap))
            g_shapes.append(jax.ShapeDtypeStruct(a.arr.shape, F32))
    return _pcall(
        body, name=name, grid=grid,
        in_specs=[pl.BlockSpec(a.block, a.imap) for a in args] + [pl.BlockSpec(o[2], o[3]) for o in outs],
        out_specs=g_specs, out_shape=g_shapes,
        compiler_params=_cparams(len(grid)),
    )(*[a.arr for a in args], *ct_arrays)


def _rowop(name, fn, grid, args, outs):
    res = _op_fwd(name, fn, grid, args, outs)
    return res, lambda *cts: _op_bwd(name + "_bwd", fn, grid, args, outs, cts)


def _pick(n, cap):
    best = None
    for t in range(LANE, cap + 1, LANE):
        if n % t == 0:
            best = t
    return best or n


def _mm(a, b, mode, out_dtype, name, epi=None, aux=None):
    if mode == "nn":
        (m, k), n = a.shape, b.shape[1]
    elif mode == "nt":
        (m, k), n = a.shape, b.shape[0]
    else:
        (k, m), n = a.shape, b.shape[1]
    tm = 512 if m % 512 == 0 else m
    tn, tk = _pick(n, 1024), _pick(k, MM_K_CAP)
    nk = k // tk
    if mode == "tn":
        a_spec = pl.BlockSpec((tk, tm), lambda i, j, kk: (kk, i))
    else:
        a_spec = pl.BlockSpec((tm, tk), lambda i, j, kk: (i, kk))
    if mode == "nt":
        b_spec = pl.BlockSpec((tn, tk), lambda i, j, kk: (j, kk))
    else:
        b_spec = pl.BlockSpec((tk, tn), lambda i, j, kk: (kk, j))
    dims = {"nn": (1, 0), "nt": (1, 1), "tn": (0, 0)}[mode]
    o_spec = pl.BlockSpec((tm, tn), lambda i, j, kk: (i, j))
    n_aux = 0 if aux is None else 1
    n_out = 2 if epi == "sqrelu" else 1

    def body(*refs):
        a_ref, b_ref = refs[0], refs[1]
        o_refs = refs[2 + n_aux:2 + n_aux + n_out]
        acc = refs[-1]
        kk = pl.program_id(2)
        part = _dg(a_ref[...], b_ref[...], *dims)

        if nk > 1:
            @pl.when(kk == 0)
            def _():
                acc[...] = part

            @pl.when((kk > 0) & (kk < nk - 1))
            def _():
                acc[...] += part

        @pl.when(kk == nk - 1)
        def _():
            r = part if nk == 1 else acc[...] + part
            if epi == "sqrelu":
                o_refs[0][...] = r.astype(o_refs[0].dtype)
                rl = jnp.maximum(r, 0.0)
                o_refs[1][...] = (rl * rl).astype(o_refs[1].dtype)
            elif epi == "dsqrelu":
                pre = refs[2][...].astype(F32)
                o_refs[0][...] = (r * (2.0 * jnp.maximum(pre, 0.0))).astype(o_refs[0].dtype)
            else:
                o_refs[0][...] = r.astype(o_refs[0].dtype)

    res = _pcall(
        body, name=name, grid=(m // tm, n // tn, nk),
        in_specs=[a_spec, b_spec] + [o_spec] * n_aux,
        out_specs=[o_spec] * n_out,
        out_shape=[jax.ShapeDtypeStruct((m, n), out_dtype)] * n_out,
        scratch_shapes=[pltpu.VMEM((tm, tn), F32)],
        compiler_params=_cparams(3),
    )(a, b, *([aux] if aux is not None else []))
    return res if n_out == 2 else res[0]


def _mla_block(q, k, v):
    p = _softmax_rows(_nt(q, k) * (MLA_QK ** -0.5))
    return _nn(p, v)


def _call_with_exchange(body, xchg, *, name, grid, in_specs, out_specs, out_shape, operands, scratch_shapes=()):
    if xchg is None:
        res = _pcall(body, name=name, grid=grid, in_specs=in_specs, out_specs=out_specs, out_shape=out_shape,
                     scratch_shapes=list(scratch_shapes), compiler_params=_cparams(len(grid)))(*operands)
        return list(res), []
    n_in, n_out, n_sc, n = len(in_specs), len(out_specs), len(scratch_shapes), xchg.n

    def wrapped(*refs):
        ins, x_refs = refs[:n_in], refs[n_in:n_in + n]
        outs, xo_refs = refs[n_in + n:n_in + n + n_out], refs[n_in + n + n_out:n_in + 2 * n + n_out]
        scratch, sems = refs[n_in + 2 * n + n_out:n_in + 2 * n + n_out + n_sc], refs[n_in + 2 * n + n_out + n_sc:]
        ids = [pl.program_id(i) for i in range(len(grid))]

        @pl.when(functools.reduce(jnp.logical_and, [i == 0 for i in ids]))
        def _():
            xchg.start(x_refs, xo_refs, sems)

        body(*ins, *outs, *scratch)

        @pl.when(functools.reduce(jnp.logical_and, [i == g - 1 for i, g in zip(ids, grid)]))
        def _():
            xchg.wait(x_refs, xo_refs, sems)

    res = _pcall(wrapped, name=name, grid=grid, in_specs=list(in_specs) + xchg.specs,
                 out_specs=list(out_specs) + xchg.specs, out_shape=list(out_shape) + xchg.out_shape,
                 scratch_shapes=list(scratch_shapes) + xchg.scratch, compiler_params=_cparams(len(grid)),
                 )(*operands, *xchg.bufs)
    return list(res[:n_out]), list(res[n_out:])


def _mla_attn(q, k, v, tc, ctx_q, name, xchg=None):
    bsz, t_all, _ = q.shape
    n_t = t_all // TB
    grid = (bsz, MLA_HEADS // MLA_HPS, n_t)
    q_spec = pl.BlockSpec((None, TB, MLA_HPS * LANE), lambda b, h, t: (b, t, h))
    kv_spec = pl.BlockSpec((None, t_all, MLA_HPS * LANE), lambda b, h, t: (b, 0, h))
    heads = [slice(i * LANE, (i + 1) * LANE) for i in range(MLA_HPS)]

    def fwd_body(q_ref, k_ref, v_ref, o_ref):
        t = pl.program_id(2)

        @pl.when(t == 0)
        def _():
            if ctx_q:
                for hs in heads:
                    o_ref[:, hs] = _mla_block(q_ref[:, hs], k_ref[0:tc, hs], v_ref[0:tc, hs])
            else:
                o_ref[...] = jnp.zeros_like(o_ref)

        @pl.when(t > 0)
        def _():
            for hs in heads:
                o_ref[:, hs] = _mla_block(q_ref[:, hs], k_ref[:, hs], v_ref[:, hs])

    (o,), gathered = _call_with_exchange(
        fwd_body, xchg, name=name, grid=grid, in_specs=[q_spec, kv_spec, kv_spec], out_specs=[q_spec],
        out_shape=[jax.ShapeDtypeStruct(q.shape, F32)], operands=(q, k, v))

    def bwd(do, xchg=None):
        def bwd_body(q_ref, k_ref, v_ref, do_ref, dq_ref, dk_ref, dv_ref):
            t = pl.program_id(2)

            @pl.when(t == 0)
            def _():
                dk_ref[...] = jnp.zeros_like(dk_ref)
                dv_ref[...] = jnp.zeros_like(dv_ref)
                if ctx_q:
                    for hs in heads:
                        _, vjp = jax.vjp(_mla_block, q_ref[:, hs].astype(F32), k_ref[0:tc, hs].astype(F32),
                                         v_ref[0:tc, hs].astype(F32))
                        dq, dk, dv = vjp(do_ref[:, hs])
                        dq_ref[:, hs] = dq
                        dk_ref[0:tc, hs] = dk
                        dv_ref[0:tc, hs] = dv
                else:
                    dq_ref[...] = jnp.zeros_like(dq_ref)

            @pl.when(t > 0)
            def _():
                for hs in heads:
                    _, vjp = jax.vjp(_mla_block, q_ref[:, hs].astype(F32), k_ref[:, hs].astype(F32),
                                     v_ref[:, hs].astype(F32))
                    dq, dk, dv = vjp(do_ref[:, hs])
                    dq_ref[:, hs] = dq
                    dk_ref[:, hs] += dk
                    dv_ref[:, hs] += dv

        return _call_with_exchange(
            bwd_body, xchg, name=name + "_bwd", grid=grid, in_specs=[q_spec, kv_spec, kv_spec, q_spec],
            out_specs=[q_spec, kv_spec, kv_spec], out_shape=[jax.ShapeDtypeStruct(q.shape, F32)] * 3,
            operands=(q, k, v, do))

    return o, gathered, bwd


def _swa_block(q, keys, vals, sink, mask):
    qs = jnp.concatenate(list(_split(q, SWA_GROUP, 1)), axis=0)
    sk = jnp.sum(sink, axis=-1, keepdims=True) * (1.0 / LANE)
    s = _nt(qs, keys) * (SWA_HEAD_DIM ** -0.5)
    if mask is not None:
        s = jnp.where(mask, s, NEG_INF)
    o = _nn(_softmax_rows(s, sk), vals)
    return jnp.concatenate(list(_split(o, SWA_GROUP, 0)), axis=1)


def _swa_ctx_block(q, kc, vc, sink):
    return _swa_block(q, kc, vc, sink, None)


def _swa_win_block(q, kc, kw, vc, vw, sink, mask):
    return _swa_block(q, jnp.concatenate([kc, kw], axis=0), jnp.concatenate([vc, vw], axis=0), sink, mask)


def _swa_attn(q, k, p_all, sink_b, tc, ctx_q, name, xchg=None):
    bsz, t_all, _ = q.shape
    n_q = t_all // QB_SWA
    n_cq = tc // QB_SWA
    lat = t_all - tc
    span = QB_SWA + 2 * WINDOW
    gw = SWA_GROUP * LANE
    grid = (bsz, SWA_KV_HEADS, n_q)
    q_spec = pl.BlockSpec((None, QB_SWA, gw), lambda b, g, i: (b, i, g))
    k_spec = pl.BlockSpec((None, t_all, LANE), lambda b, g, i: (b, 0, g))
    v_spec = pl.BlockSpec((None, t_all, LANE), lambda b, g, i: (b, 0, PC_SV // LANE + g))
    s_spec = pl.BlockSpec((None, SWA_GROUP * QB_SWA, LANE), lambda b, g, i: (g, 0, 0))

    def window(i):
        q0 = (i - n_cq) * QB_SWA
        w0 = jnp.clip(q0 - WINDOW, 0, lat - span)
        w0 = pl.multiple_of(w0, QB_SWA)
        shape = (SWA_GROUP * QB_SWA, tc + span)
        qi = q0 + lax.broadcasted_iota(jnp.int32, shape, 0) % QB_SWA
        col = lax.broadcasted_iota(jnp.int32, shape, 1)
        kj = w0 + col - tc
        mask = (col < tc) | ((kj >= qi - WINDOW) & (kj <= qi + WINDOW))
        return w0, mask

    def fwd_body(q_ref, k_ref, v_ref, s_ref, o_ref):
        i = pl.program_id(2)

        @pl.when(i < n_cq)
        def _():
            if ctx_q:
                o_ref[...] = _swa_ctx_block(q_ref[...].astype(F32), k_ref[0:tc, :], v_ref[0:tc, :], s_ref[...])
            else:
                o_ref[...] = jnp.zeros_like(o_ref)

        @pl.when(i >= n_cq)
        def _():
            w0, mask = window(i)
            o_ref[...] = _swa_win_block(q_ref[...].astype(F32), k_ref[0:tc, :], k_ref[pl.ds(tc + w0, span), :],
                                        v_ref[0:tc, :], v_ref[pl.ds(tc + w0, span), :], s_ref[...], mask)

    (o,), gathered = _call_with_exchange(
        fwd_body, xchg, name=name, grid=grid, in_specs=[q_spec, k_spec, v_spec, s_spec], out_specs=[q_spec],
        out_shape=[jax.ShapeDtypeStruct(q.shape, F32)], operands=(q, k, p_all, sink_b))

    def bwd(do, xchg=None):
        def bwd_body(q_ref, k_ref, v_ref, s_ref, do_ref, dq_ref, dk_ref, dv_ref, ds_ref):
            i = pl.program_id(2)

            @pl.when(i == 0)
            def _():
                dk_ref[...] = jnp.zeros_like(dk_ref)
                dv_ref[...] = jnp.zeros_like(dv_ref)
                ds_ref[...] = jnp.zeros_like(ds_ref)

            @pl.when(i < n_cq)
            def _():
                if ctx_q:
                    _, vjp = jax.vjp(_swa_ctx_block, q_ref[...].astype(F32), k_ref[0:tc, :].astype(F32),
                                     v_ref[0:tc, :], s_ref[...])
                    dq, dk, dv, ds = vjp(do_ref[...])
                    dq_ref[...] = dq
                    dk_ref[0:tc, :] += dk
                    dv_ref[0:tc, :] += dv
                    ds_ref[...] += ds
                else:
                    dq_ref[...] = jnp.zeros_like(dq_ref)

            @pl.when(i >= n_cq)
            def _():
                w0, mask = window(i)
                win = pl.ds(tc + w0, span)
                _, vjp = jax.vjp(functools.partial(_swa_win_block, mask=mask), q_ref[...].astype(F32),
                                 k_ref[0:tc, :].astype(F32), k_ref[win, :].astype(F32),
                                 v_ref[0:tc, :], v_ref[win, :], s_ref[...])
                dq, dkc, dkw, dvc, dvw, ds = vjp(do_ref[...])
                dq_ref[...] = dq
                dk_ref[0:tc, :] += dkc
                dk_ref[win, :] += dkw
                dv_ref[0:tc, :] += dvc
                dv_ref[win, :] += dvw
                ds_ref[...] += ds

        kv_out = pl.BlockSpec((None, t_all, LANE), lambda b, g, i: (b, 0, g))
        ds_spec = pl.BlockSpec((None, None, SWA_GROUP * QB_SWA, LANE), lambda b, g, i: (b, g, 0, 0))
        kv_shape = jax.ShapeDtypeStruct((bsz, t_all, SWA_KV_HEADS * LANE), F32)
        return _call_with_exchange(
            bwd_body, xchg, name=name + "_bwd", grid=grid, in_specs=[q_spec, k_spec, v_spec, s_spec, q_spec],
            out_specs=[q_spec, kv_out, kv_out, ds_spec],
            out_shape=[jax.ShapeDtypeStruct(q.shape, F32), kv_shape, kv_shape,
                       jax.ShapeDtypeStruct((bsz,) + sink_b.shape, F32)],
            operands=(q, k, p_all, sink_b, do))

    return o, gathered, bwd


def _scan_rows(a, u, reverse, a_s, u_s, c_s):
    t_all, c = a.shape
    row8 = lax.broadcasted_iota(jnp.int32, a.shape, 0) % SUBLANE
    for d in (1, 2, 4):
        sh = d if not reverse else t_all - d
        ar, ur = pltpu.roll(a, sh, 0), pltpu.roll(u, sh, 0)
        m = (row8 >= d) if not reverse else (row8 < SUBLANE - d)
        u = jnp.where(m, a * ur + u, u)
        a = jnp.where(m, a * ar, a)
    a_s[...] = a
    u_s[...] = u
    n_tiles = t_all // SUBLANE

    def step(j, carry):
        tile = j if not reverse else n_tiles - 1 - j
        base = pl.multiple_of(tile * SUBLANE, SUBLANE)
        c_s[pl.ds(base, SUBLANE), :] = jnp.broadcast_to(carry, (SUBLANE, c))
        last = base + (0 if reverse else SUBLANE - 1)
        return a_s[pl.ds(last, 1), :] * carry + u_s[pl.ds(last, 1), :]

    lax.fori_loop(0, n_tiles, step, jnp.zeros((1, c), F32))
    return a_s[...] * c_s[...] + u_s[...]


def _shift_rows(x, reverse_src):
    t_all = x.shape[0]
    row = lax.broadcasted_iota(jnp.int32, x.shape, 0)
    if reverse_src:
        return jnp.where(row == t_all - 1, 0.0, pltpu.roll(x, t_all - 1, 0))
    return jnp.where(row == 0, 0.0, pltpu.roll(x, 1, 0))


def _lru_scan(a0, u0, a1, u1, name):
    bsz, t_all, w = a0.shape
    grid = (bsz, w // LANE)
    spec = pl.BlockSpec((None, t_all, LANE), lambda b, c: (b, 0, c))
    scratch = [pltpu.VMEM((t_all, LANE), F32)] * 3
    shape = jax.ShapeDtypeStruct(a0.shape, F32)

    def fwd_body(a0_ref, u0_ref, a1_ref, u1_ref, h0_ref, h1_ref, a_s, u_s, c_s):
        h0_ref[...] = _scan_rows(a0_ref[...], u0_ref[...], False, a_s, u_s, c_s)
        h1_ref[...] = _scan_rows(a1_ref[...], u1_ref[...], True, a_s, u_s, c_s)

    h0, h1 = _pcall(fwd_body, name=name, grid=grid, in_specs=[spec] * 4, out_specs=[spec] * 2,
                    out_shape=[shape] * 2, scratch_shapes=scratch, compiler_params=_cparams(2))(a0, u0, a1, u1)

    def bwd(dh0, dh1):
        def bwd_body(a0_ref, h0_ref, g0_ref, a1_ref, h1_ref, g1_ref, da0_ref, du0_ref, da1_ref, du1_ref,
                     a_s, u_s, c_s):
            g0 = _scan_rows(_shift_rows(a0_ref[...], True), g0_ref[...], True, a_s, u_s, c_s)
            du0_ref[...] = g0
            da0_ref[...] = g0 * _shift_rows(h0_ref[...], False)
            g1 = _scan_rows(_shift_rows(a1_ref[...], False), g1_ref[...], False, a_s, u_s, c_s)
            du1_ref[...] = g1
            da1_ref[...] = g1 * _shift_rows(h1_ref[...], True)

        return _pcall(bwd_body, name=name + "_bwd", grid=grid, in_specs=[spec] * 6, out_specs=[spec] * 4,
                      out_shape=[shape] * 4, scratch_shapes=scratch,
                      compiler_params=_cparams(2))(a0, h0, dh0, a1, h1, dh1)

    return h0, h1, bwd


def _f_mod(x, g, shift, scale):
    return (_rms(x, g, D_MODEL) * (1.0 + scale) + shift,)


def _f_mla_q(cq, ga, w, gh, cos, sa, sb):
    n = _rms(cq, ga, MLA_Q_RANK)
    outs = []
    for wh in _split(w, MLA_HEADS, 1):
        outs.append(_rope(_rms(_nn(n, wh), gh, MLA_QK), cos, sa, sb, MLA_ROPE // 4))
    return (jnp.concatenate(outs, axis=1),)


def _f_mla_kv(ckv, krp, ga, wk, wv, gh, cos, sa, sb):
    n = _rms(ckv, ga, MLA_KV_RANK)
    outs = []
    for wh in _split(wk, MLA_HEADS, 1):
        outs.append(_rope(_rms(_nn(n, wh) + krp, gh, MLA_QK), cos, sa, sb, MLA_ROPE // 4))
    return jnp.concatenate(outs, axis=1), _nn(n, wv)


def _f_conv(x, w0, w1, w2, w3, bias, tc):
    t_all = x.shape[0]
    row = lax.broadcasted_iota(jnp.int32, x.shape, 0)
    lo = jnp.where(row < tc, 0, tc)
    hi = jnp.where(row < tc, tc, t_all)
    y = bias + jnp.zeros_like(x)
    for kk, wk in enumerate((w0, w1, w2, w3)):
        src = row + (kk - 2)
        xs = x if kk == 2 else _roll(x, 2 - kk, 0)
        y = y + wk * jnp.where((src >= lo) & (src < hi), xs, 0.0)
    return (y,)


def _f_gates(xc, w16, b00, b01, b10, b11, sp0, sp1):
    ws = _unstack(w16)
    n_cb = LRU_WIDTH // LANE
    xcs = _split(xc, n_cb, 1)
    bias = [_split(b, n_cb, 1) for b in (b00, b01, b10, b11)]
    sps = [_split(s, n_cb, 1) for s in (sp0, sp1)]
    res = [[], [], [], []]
    for c in range(n_cb):
        for z in range(2):
            r = _sig(_nn(xcs[c], ws[c * 4 + 2 * z]) + bias[2 * z][c])
            i = _sig(_nn(xcs[c], ws[c * 4 + 2 * z + 1]) + bias[2 * z + 1][c])
            la = -LRU_C * r * sps[z][c]
            res[2 * z].append(jnp.exp(la))
            res[2 * z + 1].append(jnp.sqrt(-jnp.tanh(la) * (jnp.exp(2.0 * la) + 1.0)) * (i * xcs[c]))
    return tuple(jnp.concatenate(r, axis=1) for r in res)


def _f_lru_out(h0, h1, lg):
    return ((h0 + h1) * _gelu(lg),)


def _f_swa_qk(sq, sk, gq, gk, cos, sa, sb):
    qs = [_rope(_rms(x, gq, SWA_HEAD_DIM), cos, sa, sb, SWA_HEAD_DIM // 4) for x in _split(sq, SWA_HEADS, 1)]
    ks = [_rope(_rms(x, gk, SWA_HEAD_DIM), cos, sa, sb, SWA_HEAD_DIM // 4) for x in _split(sk, SWA_KV_HEADS, 1)]
    return jnp.concatenate(qs, axis=1), jnp.concatenate(ks, axis=1)


def _f_merge(oa, ob, oc, ga, gb, gc):
    return (jnp.concatenate([_rms(oa, ga, GROUP_WIDTH), _rms(ob, gb, GROUP_WIDTH), _rms(oc, gc, GROUP_WIDTH)],
                            axis=1),)


def _f_resid_mod(x, y, gate, g, shift, scale):
    x1 = x + gate * y
    return x1, _rms(x1, g, D_MODEL) * (1.0 + scale) + shift


def _f_resid(x, y, gate):
    return (x + gate * y,)


def _hosted(hooks, key, arg=None):
    make, done = hooks.get(key, (None, None))
    xchg = make(arg) if make is not None else None
    return xchg, (done if xchg is not None else lambda outs: None)


def _layer(li, x, mods, w, s, tabs, tc, ctx_q, hooks):
    bsz, t_all, _ = x.shape
    n_t = t_all // TB
    grid = (bsz, n_t)
    rows = lambda b, t: (b, t, 0)

    def row(arr, width=None, idx=0, gdtype=F32, gshape=None):
        width = width or arr.shape[-1]
        return _A(arr, (None, TB, width), lambda b, t: (b, t, idx), "row", gdtype=gdtype, gshape=gshape,
                  gimap=rows if gshape is not None else None)

    def out(width, dtype, imap=rows):
        return ((bsz, t_all, width), dtype, (None, TB, width), imap)

    def modarg(arr):
        return _A(arr, (None, None, 1, D_MODEL), lambda b, t: (b, jnp.minimum(t, 1), 0, 0), "acc",
                  first=lambda b, t: t <= 1)

    def tab(arr):
        return _A(arr, (TB, LANE), lambda b, t: (t, 0), "const")

    def pcol(p_all, col, width):
        return row(p_all, width, col // width, gdtype=BF16, gshape=(bsz, t_all, width))

    nm = lambda base: "%s_l%d" % (base, li)
    sh1, sc1, g1, sh2, sc2, g2 = mods
    m_all = bsz * t_all

    (h,), b_mod1 = _rowop(nm("mod1"), _f_mod, grid, [row(x), _par(s["norm1_g"]), modarg(sh1), modarg(sc1)],
                          [out(D_MODEL, BF16)])
    p_all = _mm(h.reshape(m_all, D_MODEL), w["win"], "nn", F32, nm("mm_in")).reshape(bsz, t_all, P_WIDTH)

    tq, tk_ = tabs["mla"], tabs["mla"]
    (q_a,), b_mq = _rowop(nm("mla_q"), _f_mla_q, grid,
                          [pcol(p_all, PC_CQ, 256), _par(s["q_a_g"]), _par(w["wuq"]), _par(s["mla_q_g"])]
                          + [tab(a) for a in tq], [out(MLA_HEADS * LANE, BF16)])
    (k_a, v_a), b_mkv = _rowop(nm("mla_kv"), _f_mla_kv, grid,
                               [pcol(p_all, PC_CKV, 128), pcol(p_all, PC_KR, 128), _par(s["kv_a_g"]), _par(w["wk"]),
                                _par(w["wv"]), _par(s["mla_k_g"])] + [tab(a) for a in tk_],
                               [out(MLA_HEADS * LANE, BF16), out(MLA_HEADS * LANE, BF16)])
    xchg, done = _hosted(hooks, "mla_fwd")
    o_a, got, b_attn_a = _mla_attn(q_a, k_a, v_a, tc, ctx_q, nm("mla_attn"), xchg)
    done(got)

    n_cb = LRU_WIDTH // LANE
    conv_grid = (n_cb, bsz)
    cpar = lambda arr: _A(arr, (1, LANE), lambda c, b: (0, c), "acc", first=lambda c, b: b == 0)
    conv_args = [_A(p_all, (None, t_all, LANE), lambda c, b: (b, 0, PC_LX // LANE + c), "row", gdtype=BF16,
                    gshape=(bsz, t_all, LRU_WIDTH), gimap=lambda c, b: (b, 0, c))]
    conv_args += [cpar(a) for a in s["conv_w"]] + [cpar(s["conv_b"])]
    conv_out = [((bsz, t_all, LRU_WIDTH), F32, (None, t_all, LANE), lambda c, b: (b, 0, c))]
    (xc,), b_conv = _rowop(nm("lru_conv"), functools.partial(_f_conv, tc=tc), conv_grid, conv_args, conv_out)
    rot = lambda b, t: (b, (t + n_t - 1) % n_t, 0)
    (a0, u0, a1, u1), b_gates = _rowop(
        nm("lru_gates"), _f_gates, grid,
        [row(xc), _par(s["wbd"])] + [_par(a) for a in s["gate_b"]] + [_par(a) for a in s["sp"]],
        [out(LRU_WIDTH, F32), out(LRU_WIDTH, F32), out(LRU_WIDTH, F32, rot), out(LRU_WIDTH, F32, rot)])
    h0, h1, b_scan = _lru_scan(a0, u0, a1, u1, nm("lru_scan"))
    h1_arg = _A(h1, (None, TB, LRU_WIDTH), rot, "row")
    (o_b,), b_lout = _rowop(nm("lru_out"), _f_lru_out, grid, [row(h0), h1_arg, pcol(p_all, PC_LG, 512)],
                            [out(LRU_WIDTH, F32)])

    ts = tabs["swa"]
    (q_c, k_c), b_sqk = _rowop(nm("swa_qk"), _f_swa_qk, grid,
                               [pcol(p_all, PC_SQ, 1024), pcol(p_all, PC_SK, 256), _par(s["swa_q_g"]),
                                _par(s["swa_k_g"])] + [tab(a) for a in ts],
                               [out(SWA_HEADS * LANE, BF16), out(SWA_KV_HEADS * LANE, BF16)])
    xchg, done = _hosted(hooks, "swa_fwd")
    o_c, got, b_attn_c = _swa_attn(q_c, k_c, p_all, s["sink_b"], tc, ctx_q, nm("swa_attn"), xchg)
    done(got)

    (y_in,), b_merge = _rowop(nm("merge"), _f_merge, grid,
                              [row(o_a), row(o_b), row(o_c), _par(s["g_a"]), _par(s["g_b"]), _par(s["g_c"])],
                              [out(MIX_P, BF16)])
    y = _mm(y_in.reshape(m_all, MIX_P), w["wout"], "nn", F32, nm("mm_out")).reshape(bsz, t_all, D_MODEL)
    (x1, hm), b_rm = _rowop(nm("resid_mod"), _f_resid_mod, grid,
                            [row(x), row(y, gdtype=BF16), modarg(g1), _par(s["norm2_g"]), modarg(sh2), modarg(sc2)],
                            [out(D_MODEL, F32), out(D_MODEL, BF16)])
    pre, act = _mm(hm.reshape(m_all, D_MODEL), w["ff1"], "nn", BF16, nm("mm_ff1"), epi="sqrelu")
    y2 = _mm(act, w["ff2"], "nn", F32, nm("mm_ff2")).reshape(bsz, t_all, D_MODEL)
    (x2,), b_res = _rowop(nm("resid"), _f_resid, grid, [row(x1), row(y2, gdtype=BF16), modarg(g2)],
                          [out(D_MODEL, F32)])

    def bwd(dx2, hooks):
        dw, ds = {}, {}
        dx1a, dy2, dg2 = b_res(dx2)
        dy2 = dy2.reshape(m_all, D_MODEL)
        dpre = _mm(dy2, w["ff2"], "nt", BF16, nm("mm_ff2_dx"), epi="dsqrelu", aux=pre)
        dw["ff2"] = _mm(act, dy2, "tn", F32, nm("mm_ff2_dw"))
        dhm = _mm(dpre, w["ff1"], "nt", F32, nm("mm_ff1_dx")).reshape(bsz, t_all, D_MODEL)
        dw["ff1"] = _mm(hm.reshape(m_all, D_MODEL), dpre, "tn", F32, nm("mm_ff1_dw"))
        dxa, dy, dg1, ds["norm2_g"], dsh2, dsc2 = b_rm(dx1a, dhm)
        dy = dy.reshape(m_all, D_MODEL)
        dy_in = _mm(dy, w["wout"], "nt", F32, nm("mm_out_dx")).reshape(bsz, t_all, MIX_P)
        dw["wout"] = _mm(y_in.reshape(m_all, MIX_P), dy, "tn", F32, nm("mm_out_dw"))
        do_a, do_b, do_c, ds["g_a"], ds["g_b"], ds["g_c"] = b_merge(dy_in)

        (dq_c, dk_c, dsv, dsink), _ = b_attn_c(do_c)
        ds["sink_b"] = jnp.sum(dsink, axis=0)
        dsq, dsk, ds["swa_q_g"], ds["swa_k_g"] = b_sqk(dq_c, dk_c)

        dh0, dh1, dlg = b_lout(do_b)
        da0, du0, da1, du1 = b_scan(dh0, dh1)
        gates_g = b_gates(da0, du0, da1, du1)
        dxc, ds["wbd"] = gates_g[0], gates_g[1]
        ds["gate_b"], ds["sp"] = list(gates_g[2:6]), list(gates_g[6:8])
        conv_g = b_conv(dxc)
        dlx, ds["conv_w"], ds["conv_b"] = conv_g[0], list(conv_g[1:5]), conv_g[5]

        xchg, done = _hosted(hooks, "mla_bwd", dw)
        (dq_a, dk_a, dv_a), got = b_attn_a(do_a, xchg)
        done(got)
        dcq, ds["q_a_g"], dw["wuq"], ds["mla_q_g"] = b_mq(dq_a)
        dckv, dkr, ds["kv_a_g"], dw["wk"], dw["wv"], ds["mla_k_g"] = b_mkv(dk_a, dv_a)

        dp = jnp.concatenate([dsq, dlx, dlg, dcq, dsk, dsv.astype(BF16), dckv, dkr], axis=-1)
        dp = dp.reshape(m_all, P_WIDTH)
        dh = _mm(dp, w["win"], "nt", F32, nm("mm_in_dx")).reshape(bsz, t_all, D_MODEL)
        dw["win"] = _mm(h.reshape(m_all, D_MODEL), dp, "tn", F32, nm("mm_in_dw"))
        dxb, ds["norm1_g"], dsh1, dsc1 = b_mod1(dh)
        return dxa + dxb, [dsh1, dsc1, dg1, dsh2, dsc2, dg2], dw, ds

    return x2, bwd


def _loss_and_grad(x2, target, tc):
    bsz, t_all, d = x2.shape
    n_t = t_all // TB
    n_c = tc // TB

    def body(x_ref, t_ref, l_ref, dx_ref):
        b, t = pl.program_id(0), pl.program_id(1)

        @pl.when((b == 0) & (t == 0))
        def _():
            l_ref[...] = jnp.zeros_like(l_ref)

        @pl.when(t < n_c)
        def _():
            dx_ref[...] = jnp.zeros_like(dx_ref)

        @pl.when(t >= n_c)
        def _():
            e = x_ref[...] - t_ref[...]
            dx_ref[...] = e * (1.0 / d)
            l_ref[...] += jnp.sum(e * e) * (0.5 / d)

    loss, dx = _pcall(
        body, name="loss", grid=(bsz, n_t),
        in_specs=[pl.BlockSpec((None, TB, d), lambda b, t: (b, t, 0)),
                  pl.BlockSpec((None, TB, d), lambda b, t: (b, jnp.maximum(t - n_c, 0), 0))],
        out_specs=[pl.BlockSpec((SUBLANE, LANE), lambda b, t: (0, 0)),
                   pl.BlockSpec((None, TB, d), lambda b, t: (b, t, 0))],
        out_shape=[jax.ShapeDtypeStruct((SUBLANE, LANE), F32), jax.ShapeDtypeStruct(x2.shape, F32)],
        compiler_params=_cparams(2))(x2, target)
    return loss[0, 0], dx


def _rope_tables(lat, tc, dim, lane0):
    quarter = dim // 4
    pos = jnp.arange(lat)
    grid_pos = jnp.stack([pos // GRID_W, pos % GRID_W], axis=-1).astype(F32)
    lane = jnp.arange(LANE)
    p = jnp.clip(lane - lane0, 0, dim - 1)
    active = (lane >= lane0) & (lane < lane0 + dim)
    axis, half, qi = p // (dim // 2), (p % (dim // 2)) // quarter, p % quarter
    inv = ROPE_THETA ** (-qi.astype(F32) / quarter)
    ang = jnp.where(axis[None, :] == 0, grid_pos[:, 0:1], grid_pos[:, 1:2]) * inv[None, :]
    cos = jnp.where(active, jnp.cos(ang), 1.0)
    sin = jnp.where(active, jnp.sin(ang), 0.0)
    sa = jnp.where(half == 0, -sin, 0.0)
    sb = jnp.where(half == 1, sin, 0.0)
    ctx1, ctx0 = jnp.ones((tc, LANE), F32), jnp.zeros((tc, LANE), F32)
    return (jnp.concatenate([ctx1, cos], 0), jnp.concatenate([ctx0, sa], 0), jnp.concatenate([ctx0, sb], 0))


_BIG = {"w_in": ((D_MODEL, IN_WIDTH // N_DEV), 1, ("win",)),
        "w_uq": ((MLA_Q_RANK, MLA_HEADS * MLA_QK // N_DEV), 1, ("wuq",)),
        "w_ukv": ((MLA_KV_RANK, MLA_HEADS * (MLA_NOPE + MLA_V) // N_DEV), 1, ("wk", "wv")),
        "w_out": ((3 * GROUP_WIDTH // N_DEV, D_MODEL), 0, ("wout",)),
        "w_ff1": ((D_MODEL, D_FF // N_DEV), 1, ("ff1",)),
        "w_ff2": ((D_FF // N_DEV, D_MODEL), 0, ("ff2",))}
_EARLY = ("w_in", "w_uq", "w_ukv")
_LATE = ("w_out", "w_ff1", "w_ff2")


def _pad_heads(wm, n_heads, dim, axis=-1):
    axis = axis % wm.ndim
    shp = wm.shape[:axis] + (n_heads, dim) + wm.shape[axis + 1:]
    pad = [(0, 0)] * len(shp)
    pad[axis + 1] = (0, LANE - dim)
    out = jnp.pad(wm.reshape(shp), pad)
    return out.reshape(wm.shape[:axis] + (n_heads * LANE,) + wm.shape[axis + 1:])


def _prep_weight(name, piece):
    shp, ax, _ = _BIG[name]
    full = jnp.moveaxis(piece, 0, ax).reshape(shp[:ax] + (N_DEV * shp[ax],) + shp[ax + 1:])
    if name == "w_in":
        cq, ckv, kr, lx, lg, sq, sk, sv = _split_cols(full)
        return {"win": jnp.concatenate(
            [_pad_heads(sq, SWA_HEADS, SWA_HEAD_DIM), lx, lg, cq, _pad_heads(sk, SWA_KV_HEADS, SWA_HEAD_DIM),
             _pad_heads(sv, SWA_KV_HEADS, SWA_HEAD_DIM), ckv, jnp.pad(kr, ((0, 0), (MLA_NOPE, LANE - MLA_QK)))], axis=1)}
    if name == "w_uq":
        return {"wuq": _pad_heads(full, MLA_HEADS, MLA_QK)}
    if name == "w_ukv":
        ukv = full.reshape(MLA_KV_RANK, MLA_HEADS, MLA_NOPE + MLA_V)
        return {"wk": _pad_heads(ukv[:, :, :MLA_NOPE].reshape(MLA_KV_RANK, -1), MLA_HEADS, MLA_NOPE),
                "wv": _pad_heads(ukv[:, :, MLA_NOPE:].reshape(MLA_KV_RANK, -1), MLA_HEADS, MLA_V)}
    if name == "w_out":
        return {"wout": jnp.concatenate(
            [_pad_heads(full[:GROUP_WIDTH], MLA_HEADS, MLA_V, axis=0), full[GROUP_WIDTH:2 * GROUP_WIDTH],
             _pad_heads(full[2 * GROUP_WIDTH:], SWA_HEADS, SWA_HEAD_DIM, axis=0)], axis=0)}
    return {_BIG[name][2][0]: full}


def _split_cols(wm):
    parts, start = [], 0
    for size in IN_SIZES:
        parts.append(wm[:, start:start + size])
        start += size
    return parts


def _prep_small(raw):
    r1 = lambda a: a.reshape(1, -1)
    gw = raw["lru_gate_w"].reshape(2, 2, 4, 2, 64, 64)
    wbd = jnp.einsum("zgknCm,nN->knCzgNm", gw, jnp.eye(2, dtype=F32)).reshape(4, LANE, 4, LANE)
    gg = raw["group_g"]
    sink = raw["swa_sink"].reshape(SWA_KV_HEADS, SWA_GROUP, 1, 1)
    return {
        "norm1_g": r1(raw["norm1_g"]), "norm2_g": r1(raw["norm2_g"]),
        "q_a_g": r1(raw["q_a_g"]), "kv_a_g": r1(raw["kv_a_g"]),
        "mla_q_g": jnp.pad(r1(raw["mla_q_g"]), ((0, 0), (0, LANE - MLA_QK))),
        "mla_k_g": jnp.pad(r1(raw["mla_k_g"]), ((0, 0), (0, LANE - MLA_QK))),
        "swa_q_g": jnp.pad(r1(raw["swa_q_g"]), ((0, 0), (0, LANE - SWA_HEAD_DIM))),
        "swa_k_g": jnp.pad(r1(raw["swa_k_g"]), ((0, 0), (0, LANE - SWA_HEAD_DIM))),
        "conv_w": [r1(raw["conv_w"][kk]) for kk in range(4)], "conv_b": r1(raw["conv_b"]),
        "wbd": wbd.transpose(0, 2, 1, 3).reshape(16, LANE, LANE),
        "gate_b": [r1(raw["lru_gate_b"][z, g]) for z in range(2) for g in range(2)],
        "sp": [r1(jax.nn.softplus(-raw["lru_lambda"][z])) for z in range(2)],
        "sink_b": jnp.broadcast_to(sink, (SWA_KV_HEADS, SWA_GROUP, QB_SWA, LANE)).reshape(
            SWA_KV_HEADS, SWA_GROUP * QB_SWA, LANE),
        "g_a": _pad_heads(r1(gg[:GROUP_WIDTH]), MLA_HEADS, MLA_V), "g_b": r1(gg[GROUP_WIDTH:2 * GROUP_WIDTH]),
        "g_c": _pad_heads(r1(gg[2 * GROUP_WIDTH:]), SWA_HEADS, SWA_HEAD_DIM)}


def _mesh_pos():
    return lax.axis_index("x"), lax.axis_index("y"), lax.axis_index("c")


def _peer(pos, k):
    return tuple(1 - p if (k >> s) & 1 else p for p, s in zip(pos, (2, 1, 0)))


def _dev_index(pos):
    return 4 * pos[0] + 2 * pos[1] + pos[2]


class _Exchange:
    def __init__(self, bufs, gather):
        self.bufs = list(bufs)
        self.n = len(self.bufs)
        self.gather = [gather] * self.n if isinstance(gather, bool) else list(gather)
        self.specs = [pl.BlockSpec(memory_space=pl.ANY)] * self.n
        self.out_shape = [jax.ShapeDtypeStruct((N_DEV,) + tuple(b.shape if g else b.shape[1:]), b.dtype)
                          for b, g in zip(self.bufs, self.gather)]
        self.scratch = [pltpu.SemaphoreType.DMA(((N_DEV - 1) * self.n,)),
                        pltpu.SemaphoreType.DMA(((N_DEV - 1) * self.n,)), pltpu.SemaphoreType.DMA((self.n,))]

    def _copies(self, x_refs, o_refs, sems, with_recvs):
        send_sems, recv_sems, local_sems = sems
        pos = _mesh_pos()
        me = _dev_index(pos)
        locals_, sends, recvs = [], [], []
        for j in range(self.n):
            src_mine = x_refs[j] if self.gather[j] else x_refs[j].at[me]
            locals_.append(pltpu.make_async_copy(src_mine, o_refs[j].at[me], local_sems.at[j]))
        for k in range(1, N_DEV):
            peer = _peer(pos, k)
            pidx = _dev_index(peer)
            for j in range(self.n):
                src = x_refs[j] if self.gather[j] else x_refs[j].at[pidx]
                sem = (k - 1) * self.n + j
                sends.append(pltpu.make_async_remote_copy(
                    src_ref=src, dst_ref=o_refs[j].at[me], send_sem=send_sems.at[sem], recv_sem=recv_sems.at[sem],
                    device_id=peer, device_id_type=pl.DeviceIdType.MESH))
                if with_recvs:
                    recvs.append(pltpu.make_async_remote_copy(
                        src_ref=src, dst_ref=o_refs[j].at[pidx], send_sem=send_sems.at[sem],
                        recv_sem=recv_sems.at[sem], device_id=peer, device_id_type=pl.DeviceIdType.MESH))
        return locals_, sends, recvs

    def start(self, x_refs, o_refs, sems):
        locals_, sends, _ = self._copies(x_refs, o_refs, sems, False)
        for cp in locals_ + sends:
            cp.start()

    def wait(self, x_refs, o_refs, sems):
        locals_, sends, recvs = self._copies(x_refs, o_refs, sems, True)
        for cp in recvs:
            cp.wait_recv()
        for cp in sends:
            cp.wait_send()
        for cp in locals_:
            cp.wait()


def _exchange(bufs, gather, name):
    xchg = _Exchange(bufs, gather)
    n = xchg.n

    def body(*refs):
        xchg.start(refs[:n], refs[n:2 * n], refs[2 * n:])
        xchg.wait(refs[:n], refs[n:2 * n], refs[2 * n:])

    return _pcall(body, name=name, out_shape=xchg.out_shape, in_specs=xchg.specs, out_specs=xchg.specs,
                  scratch_shapes=xchg.scratch)(*xchg.bufs)


def _pack(arrs, dtype):
    flat = jnp.concatenate([a.reshape(-1).astype(dtype) for a in arrs])
    rows = -(-flat.size // PACK_W)
    rows = -(-rows // 16) * 16
    return jnp.pad(flat, (0, rows * PACK_W - flat.size)).reshape(rows, PACK_W)


def _unpack(buf, shapes, lead=()):
    flat = buf.reshape(lead + (-1,))
    out, off = [], 0
    for shp in shapes:
        n = math.prod(shp)
        out.append(flat[..., off:off + n].reshape(lead + tuple(shp)))
        off += n
    return out


def _sum_sources(buf, name):
    _, r, c = buf.shape
    tr = _rows_tile(r)

    def body(x_ref, o_ref):
        acc = x_ref[0]
        for d in range(1, N_DEV):
            acc = acc + x_ref[d]
        o_ref[...] = acc

    return _pcall(body, name=name, grid=(r // tr,),
                  in_specs=[pl.BlockSpec((N_DEV, tr, c), lambda i: (0, i, 0))],
                  out_specs=pl.BlockSpec((tr, c), lambda i: (i, 0)),
                  out_shape=jax.ShapeDtypeStruct((r, c), F32), compiler_params=_cparams(1))(buf)


def _rows_tile(r):
    best = r
    for t in range(SUBLANE, 257, SUBLANE):
        if r % t == 0:
            best = t
    return best


def _adamw(grads, wgt, m, v, name):
    n_src, r, c = grads.shape
    tr = _rows_tile(r)
    bc1 = 1.0 - ADAM_B1 ** ADAM_STEP
    bc2 = 1.0 - ADAM_B2 ** ADAM_STEP

    def body(g_ref, w_ref, m_ref, v_ref, go_ref, d_ref, mo_ref, vo_ref):
        g = g_ref[0].astype(F32)
        for d in range(1, n_src):
            g = g + g_ref[d].astype(F32)
        m_new = ADAM_B1 * m_ref[...] + (1.0 - ADAM_B1) * g
        v_new = ADAM_B2 * v_ref[...] + (1.0 - ADAM_B2) * (g * g)
        go_ref[...] = g
        mo_ref[...] = m_new
        vo_ref[...] = v_new
        d_ref[...] = -ADAM_LR * ((m_new / bc1) / (jnp.sqrt(v_new / bc2) + ADAM_EPS) + ADAM_WD * w_ref[...])

    spec = pl.BlockSpec((tr, c), lambda i: (i, 0))
    return _pcall(body, name=name, grid=(r // tr,),
                  in_specs=[pl.BlockSpec((n_src, tr, c), lambda i: (0, i, 0)), spec, spec, spec],
                  out_specs=[spec] * 4, out_shape=[jax.ShapeDtypeStruct((r, c), F32)] * 4,
                  compiler_params=_cparams(1))(grads, wgt, m, v)


def _silu(z):
    return z * jax.nn.sigmoid(z)


_WEIGHTS = ("c_ctx", "w_mod", "b_mod", "norm1_g", "w_in", "q_a_g", "w_uq", "kv_a_g", "w_ukv", "mla_q_g", "mla_k_g",
            "conv_w", "conv_b", "lru_gate_w", "lru_gate_b", "lru_lambda", "swa_q_g", "swa_k_g", "swa_sink", "group_g",
            "w_out", "norm2_g", "w_ff1", "w_ff2")
_SHARDED_SMALL = ("conv_w", "lru_gate_b", "lru_lambda")
_REPL_RAW = ("norm1_g", "q_a_g", "kv_a_g", "mla_q_g", "mla_k_g", "conv_b", "lru_gate_w", "swa_q_g", "swa_k_g",
             "swa_sink", "group_g", "norm2_g")
MOD_ROWS = 32


def kernel(x, c, ctx, c_ctx, w_mod, b_mod, norm1_g, w_in, q_a_g, w_uq, kv_a_g, w_ukv, mla_q_g, mla_k_g, conv_w, conv_b, lru_gate_w, lru_gate_b, lru_lambda, swa_q_g, swa_k_g, swa_sink, group_g, w_out, norm2_g, w_ff1, w_ff2, loss_target, m_c_ctx, m_w_mod, m_b_mod, m_norm1_g, m_w_in, m_q_a_g, m_w_uq, m_kv_a_g, m_w_ukv, m_mla_q_g, m_mla_k_g, m_conv_w, m_conv_b, m_lru_gate_w, m_lru_gate_b, m_lru_lambda, m_swa_q_g, m_swa_k_g, m_swa_sink, m_group_g, m_w_out, m_norm2_g, m_w_ff1, m_w_ff2, v_c_ctx, v_w_mod, v_b_mod, v_norm1_g, v_w_in, v_q_a_g, v_w_uq, v_kv_a_g, v_w_ukv, v_mla_q_g, v_mla_k_g, v_conv_w, v_conv_b, v_lru_gate_w, v_lru_gate_b, v_lru_lambda, v_swa_q_g, v_swa_k_g, v_swa_sink, v_group_g, v_w_out, v_norm2_g, v_w_ff1, v_w_ff2):
    wts = dict(c_ctx=c_ctx, w_mod=w_mod, b_mod=b_mod, norm1_g=norm1_g, w_in=w_in, q_a_g=q_a_g, w_uq=w_uq,
               kv_a_g=kv_a_g, w_ukv=w_ukv, mla_q_g=mla_q_g, mla_k_g=mla_k_g, conv_w=conv_w, conv_b=conv_b,
               lru_gate_w=lru_gate_w, lru_gate_b=lru_gate_b, lru_lambda=lru_lambda, swa_q_g=swa_q_g, swa_k_g=swa_k_g,
               swa_sink=swa_sink, group_g=group_g, w_out=w_out, norm2_g=norm2_g, w_ff1=w_ff1, w_ff2=w_ff2)
    mom1 = dict(zip(_WEIGHTS, (m_c_ctx, m_w_mod, m_b_mod, m_norm1_g, m_w_in, m_q_a_g, m_w_uq, m_kv_a_g, m_w_ukv,
                               m_mla_q_g, m_mla_k_g, m_conv_w, m_conv_b, m_lru_gate_w, m_lru_gate_b, m_lru_lambda,
                               m_swa_q_g, m_swa_k_g, m_swa_sink, m_group_g, m_w_out, m_norm2_g, m_w_ff1, m_w_ff2)))
    mom2 = dict(zip(_WEIGHTS, (v_c_ctx, v_w_mod, v_b_mod, v_norm1_g, v_w_in, v_q_a_g, v_w_uq, v_kv_a_g, v_w_ukv,
                               v_mla_q_g, v_mla_k_g, v_conv_w, v_conv_b, v_lru_gate_w, v_lru_gate_b, v_lru_lambda,
                               v_swa_q_g, v_swa_k_g, v_swa_sink, v_group_g, v_w_out, v_norm2_g, v_w_ff1, v_w_ff2)))
    bsz = x.shape[0]
    n_ex = bsz * N_DEV
    me = _dev_index(_mesh_pos())
    mod_cols = w_mod.shape[-1]

    small_shapes = [c.shape, conv_w.shape, lru_gate_b.shape, lru_lambda.shape]
    (g_small,) = _exchange([_pack([c, conv_w, lru_gate_b, lru_lambda], F32)], True, "ag_small")
    c_all, conv_w_all, gate_b_all, lam_all = _unpack(g_small, small_shapes, lead=(N_DEV,))
    c_all = c_all.reshape(n_ex, D_MODEL)
    cat_last = lambda a: jnp.moveaxis(a, 0, -2).reshape(a.shape[1:-1] + (N_DEV * a.shape[-1],))
    conv_w_full, gate_b_full, lam_full = cat_last(conv_w_all), cat_last(gate_b_all), cat_last(lam_all)

    act = jnp.zeros((MOD_ROWS, D_MODEL), F32).at[:n_ex].set(_silu(c_all)).at[n_ex].set(_silu(c_ctx))
    mod_part = jnp.concatenate([_mm(act, w_mod[li], "nn", F32, "mm_mod_l%d" % li) for li in range(DEPTH)], axis=1)
    (mod_all,) = _exchange([mod_part], True, "ag_mod")
    mods = []
    for li in range(DEPTH):
        full = jnp.moveaxis(mod_all[:, :, li * mod_cols:(li + 1) * mod_cols], 0, 1).reshape(MOD_ROWS, -1) + b_mod[li]
        mine = lax.dynamic_slice_in_dim(full, me * bsz, bsz, axis=0)
        ctx_row = jnp.broadcast_to(full[n_ex], mine.shape)
        both = jnp.stack([ctx_row, mine], axis=1).reshape(bsz, 2, N_MOD, 1, D_MODEL)
        mods.append([both[:, :, j] for j in range(N_MOD)])

    raw = {n: wts[n] for n in _REPL_RAW}
    raw.update(conv_w=conv_w_full, lru_gate_b=gate_b_full, lru_lambda=lam_full)
    small_names = list(_REPL_RAW) + list(_SHARDED_SMALL)
    sp, small_vjp = [None] * DEPTH, [None] * DEPTH
    for li in range(DEPTH):
        sp[li], small_vjp[li] = jax.vjp(_prep_small, {n: raw[n][li] for n in small_names})

    w, w_vjp, g_recv, small_recv = [{} for _ in range(DEPTH)], {}, {}, {}
    shard = lambda n, li: wts[n][li].astype(BF16)

    def take(li, names, pieces):
        for n, piece in zip(names, pieces):
            out, w_vjp[n, li] = jax.vjp(functools.partial(_prep_weight, n), piece)
            w[li].update(out)

    def gather_hook(li, names):
        return (lambda _: _Exchange([shard(n, li) for n in names], True), lambda got: take(li, names, got))

    def wgrad(n, li, dwl):
        (g,) = w_vjp[n, li]({k: dwl[k].astype(BF16) for k in _BIG[n][2]})
        return g

    def small_pack(li, ds_l, extra=()):
        (d_raw,) = small_vjp[li](ds_l)
        return _pack([d_raw[n] for n in small_names] + list(extra), F32)

    take(0, _EARLY, _exchange([shard(n, 0) for n in _EARLY], True, "ag_early"))
    hooks_fwd = [{"mla_fwd": gather_hook(0, _LATE), "swa_fwd": gather_hook(1, _EARLY + ("w_out",))},
                 {"mla_fwd": gather_hook(1, ("w_ff1", "w_ff2"))}]
    bwd_state = {}

    def scatter_last_layer(dwl):
        return _Exchange([wgrad(n, 1, dwl) for n in _LATE], False)

    def scatter_first_layer(dwl):
        dw1, ds1 = bwd_state["dw1"], bwd_state["ds1"]
        bufs = [wgrad(n, 1, dw1) for n in _EARLY] + [wgrad(n, 0, dwl) for n in _LATE] + [small_pack(1, ds1)]
        return _Exchange(bufs, [False] * (len(_EARLY) + len(_LATE)) + [True])

    def scattered_first_layer(got):
        g_recv.update(zip([(n, 1) for n in _EARLY] + [(n, 0) for n in _LATE], got[:-1]))
        small_recv[1] = got[-1]

    hooks_bwd = [{"mla_bwd": (scatter_first_layer, scattered_first_layer)},
                 {"mla_bwd": (scatter_last_layer, lambda got: g_recv.update(zip([(n, 1) for n in _LATE], got)))}]

    tc, lat = ctx.shape[1], x.shape[1]
    tabs = {"mla": _rope_tables(lat, tc, MLA_ROPE, MLA_NOPE), "swa": _rope_tables(lat, tc, SWA_HEAD_DIM, 0)}
    stream = jnp.concatenate([ctx, x], axis=1)
    bwds = []
    for li in range(DEPTH):
        stream, bwd = _layer(li, stream, mods[li], w[li], sp[li], tabs, tc, li < DEPTH - 1, hooks_fwd[li])
        bwds.append(bwd)
    loss_part, dstream = _loss_and_grad(stream, loss_target, tc)
    dmods = [None] * DEPTH
    dstream, dmods[1], bwd_state["dw1"], bwd_state["ds1"] = bwds[1](dstream, hooks_bwd[1])
    dstream, dmods[0], dw0, ds0 = bwds[0](dstream, hooks_bwd[0])
    grad_x = dstream[:, tc:]

    dm_rows = []
    for li in range(DEPTH):
        dm = jnp.concatenate(dmods[li], axis=-1)
        dm_rows.append(jnp.concatenate([dm[:, 1, 0], jnp.sum(dm[:, 0, 0], axis=0, keepdims=True)], axis=0))
    dm_mine = jnp.concatenate(dm_rows, axis=1)
    dm_mine = jnp.pad(dm_mine, ((0, SUBLANE - bsz - 1), (0, 0)))
    (dm_all,) = _exchange([dm_mine], True, "ag_dmod")
    g_wmod, g_bmod, dact_ctx = [], [], jnp.zeros((D_MODEL,), F32)
    for li in range(DEPTH):
        part = dm_all[:, :, li * N_MOD * D_MODEL:(li + 1) * N_MOD * D_MODEL]
        dm32 = jnp.zeros((MOD_ROWS, N_MOD * D_MODEL), F32).at[:n_ex].set(part[:, :bsz].reshape(n_ex, -1))
        dm32 = dm32.at[n_ex].set(jnp.sum(part[:, bsz], axis=0))
        g_bmod.append(jnp.sum(dm32, axis=0))
        cols = lax.dynamic_slice_in_dim(dm32, me * mod_cols, mod_cols, axis=1)
        g_wmod.append(_mm(act, cols, "tn", F32, "mm_mod_dw_l%d" % li))
        dact_ctx = dact_ctx + _mm(cols, w_mod[li], "nt", F32, "mm_mod_dx_l%d" % li)[n_ex]
    sg = jax.nn.sigmoid(c_ctx)
    g_cctx_part = dact_ctx * (sg * (1.0 + c_ctx * (1.0 - sg)))

    last = _exchange([wgrad(n, 0, dw0) for n in _EARLY] + [small_pack(0, ds0, (g_cctx_part, loss_part.reshape(1)))],
                     [False] * len(_EARLY) + [True], "rs_early")
    g_recv.update(zip([(n, 0) for n in _EARLY], last[:-1]))
    small_recv[0] = last[-1]
    layer_shapes = [raw[n].shape[1:] for n in small_names]
    tot = [_unpack(_sum_sources(small_recv[li], "sum_grads_l%d" % li), layer_shapes + [(D_MODEL,), (1,)][:2 * (li == 0)])
           for li in range(DEPTH)]
    grads = {n: jnp.stack([tot[li][j] for li in range(DEPTH)], axis=0) for j, n in enumerate(small_names)}
    grads["c_ctx"], loss = tot[0][-2], tot[0][-1][0]
    for n in _SHARDED_SMALL:
        width = wts[n].shape[-1]
        grads[n] = lax.dynamic_slice_in_dim(grads[n], me * width, width, axis=grads[n].ndim - 1)
    grads["b_mod"] = jnp.stack(g_bmod, axis=0)

    delta, new_m, new_v = {}, {}, {}
    for n, (shp, _, _) in _BIG.items():
        two_d = (DEPTH * shp[0], shp[1])
        src = jnp.stack([g_recv[n, li] for li in range(DEPTH)], axis=1).reshape((N_DEV,) + two_d)
        res = _adamw(src, wts[n].reshape(two_d), mom1[n].reshape(two_d), mom2[n].reshape(two_d), "adamw_" + n)
        grads[n], delta[n], new_m[n], new_v[n] = [r.reshape(wts[n].shape) for r in res]
    two_d = (DEPTH * D_MODEL, mod_cols)
    res = _adamw(jnp.stack(g_wmod, axis=0).reshape((1,) + two_d), w_mod.reshape(two_d), mom1["w_mod"].reshape(two_d),
                 mom2["w_mod"].reshape(two_d), "adamw_w_mod")
    grads["w_mod"], delta["w_mod"], new_m["w_mod"], new_v["w_mod"] = [r.reshape(w_mod.shape) for r in res]
    rest = [n for n in _WEIGHTS if n not in delta]
    shapes = [wts[n].shape for n in rest]
    res = _adamw(_pack([grads[n] for n in rest], F32)[None], _pack([wts[n] for n in rest], F32),
                 _pack([mom1[n] for n in rest], F32), _pack([mom2[n] for n in rest], F32), "adamw_small")
    for tgt, buf in zip((delta, new_m, new_v), res[1:]):
        tgt.update(zip(rest, _unpack(buf, shapes)))

    return (loss, grad_x, *[grads[n] for n in _WEIGHTS], *[delta[n] for n in _WEIGHTS],
            *[new_m[n] for n in _WEIGHTS], *[new_v[n] for n in _WEIGHTS])
```

```python
import functools
import math

import jax
import jax.numpy as jnp
from jax import lax
from jax.experimental import pallas as pl
from jax.experimental.pallas import tpu as pltpu

F32, BF16 = jnp.float32, jnp.bfloat16

N_DEV = 8
DEPTH = 2
D_MODEL = 1024
D_FF = 4096
N_MOD = 6
GRID_W = 64
WINDOW = 128
ROPE_THETA = 10000.0
EPS = 1e-6
NEG_INF = -1e30
LRU_C = 8.0
LRU_WIDTH = 512
MLA_HEADS, MLA_NOPE, MLA_ROPE, MLA_V = 8, 64, 32, 64
MLA_QK = MLA_NOPE + MLA_ROPE
MLA_Q_RANK, MLA_KV_RANK = 256, 128
SWA_HEADS, SWA_KV_HEADS, SWA_GROUP, SWA_HEAD_DIM = 8, 2, 4, 64
GROUP_WIDTH = 512
IN_SIZES = (256, 128, 32, 512, 512, 512, 128, 128)
IN_WIDTH = sum(IN_SIZES)
ADAM_LR, ADAM_B1, ADAM_B2, ADAM_EPS, ADAM_WD, ADAM_STEP = 0.001, 0.9, 0.999, 1e-08, 0.01, 10

LANE = 128
SUBLANE = 8
TB = 256
QB_SWA = 128
PACK_W = 1024
MM_K_CAP = 4608
MLA_HPS = 2
VMEM_LIMIT = 56 * 1024 * 1024
P_WIDTH = 3072
PC_SQ, PC_LX, PC_LG, PC_CQ, PC_SK, PC_SV, PC_CKV, PC_KR = 0, 1024, 1536, 2048, 2304, 2560, 2816, 2944
MIX_P = 2560


def _pcall(body, **kw):
    return pl.pallas_call(body, **kw)


def _cparams(n_grid):
    return pltpu.CompilerParams(dimension_semantics=("arbitrary",) * n_grid, vmem_limit_bytes=VMEM_LIMIT)


def _dg(a, b, ca, cb):
    return lax.dot_general(a.astype(BF16), b.astype(BF16), (((ca,), (cb,)), ((), ())),
                           preferred_element_type=F32)


@jax.custom_vjp
def _nn(a, b):
    return _dg(a, b, 1, 0)


@jax.custom_vjp
def _nt(a, b):
    return _dg(a, b, 1, 1)


@jax.custom_vjp
def _tn(a, b):
    return _dg(a, b, 0, 0)


_nn.defvjp(lambda a, b: (_nn(a, b), (a, b)), lambda r, ct: (_nt(ct, r[1]), _tn(r[0], ct)))
_nt.defvjp(lambda a, b: (_nt(a, b), (a, b)), lambda r, ct: (_nn(ct, r[1]), _tn(ct, r[0])))
_tn.defvjp(lambda a, b: (_tn(a, b), (a, b)), lambda r, ct: (_nt(r[1], ct), _nn(r[0], ct)))


@functools.partial(jax.custom_vjp, nondiff_argnums=(1, 2))
def _roll(x, shift, axis):
    return pltpu.roll(x, shift % x.shape[axis], axis)


_roll.defvjp(lambda x, shift, axis: (_roll(x, shift, axis), None),
             lambda shift, axis, _, ct: (_roll(ct, -shift, axis),))


@functools.partial(jax.custom_vjp, nondiff_argnums=(1, 2))
def _split(x, n, axis):
    w = x.shape[axis] // n
    return tuple(lax.slice_in_dim(x, i * w, (i + 1) * w, axis=axis) for i in range(n))


_split.defvjp(lambda x, n, axis: (_split(x, n, axis), None),
              lambda n, axis, _, cts: (jnp.concatenate(cts, axis=axis),))


@jax.custom_vjp
def _unstack(x):
    return tuple(x[i] for i in range(x.shape[0]))


_unstack.defvjp(lambda x: (_unstack(x), None), lambda _, cts: (jnp.stack(cts, axis=0),))


def _sig(x):
    return 0.5 * (jnp.tanh(0.5 * x) + 1.0)


def _gelu(x):
    return 0.5 * x * (1.0 + jnp.tanh(math.sqrt(2.0 / math.pi) * (x + 0.044715 * (x * x * x))))


def _rms(x, g, n):
    ms = jnp.sum(x * x, axis=-1, keepdims=True) * (1.0 / n)
    return x * lax.rsqrt(ms + EPS) * g


def _rope(y, cos, sa, sb, quarter):
    return y * cos + _roll(y, -quarter, 1) * sa + _roll(y, quarter, 1) * sb


def _softmax_rows(s, extra=None):
    m = jnp.max(s, axis=-1, keepdims=True)
    if extra is not None:
        m = jnp.maximum(m, extra)
    m = lax.stop_gradient(m)
    e = jnp.exp(s - m)
    den = jnp.sum(e, axis=-1, keepdims=True)
    if extra is not None:
        den = den + jnp.exp(extra - m)
    return e / den


class _A:
    def __init__(self, arr, block, imap, kind="row", first=None, gdtype=F32, gshape=None, gimap=None):
        self.arr, self.block, self.imap, self.kind, self.first = arr, block, imap, kind, first
        self.gdtype, self.gshape, self.gimap = gdtype, gshape, gimap


def _all_zero(*ids):
    return functools.reduce(jnp.logical_and, [i == 0 for i in ids])


def _par(arr):
    nd = arr.ndim
    return _A(arr, arr.shape, lambda *ids: (0,) * nd, "acc", first=_all_zero)


def _op_fwd(name, fn, grid, args, outs):
    n_in = len(args)

    def body(*refs):
        vals = [r[...].astype(F32) for r in refs[:n_in]]
        for r, v in zip(refs[n_in:], fn(*vals)):
            r[...] = v.astype(r.dtype)

    return _pcall(
        body, name=name, grid=grid,
        in_specs=[pl.BlockSpec(a.block, a.imap) for a in args],
        out_specs=[pl.BlockSpec(o[2], o[3]) for o in outs],
        out_shape=[jax.ShapeDtypeStruct(o[0], o[1]) for o in outs],
        compiler_params=_cparams(len(grid)),
    )(*[a.arr for a in args])


def _op_bwd(name, fn, grid, args, outs, ct_arrays):
    n_in, n_ct = len(args), len(outs)
    didx = [i for i, a in enumerate(args) if a.kind != "const"]

    def body(*refs):
        ids = [pl.program_id(i) for i in range(len(grid))]
        vals = [r[...].astype(F32) for r in refs[:n_in]]

        def g(*dv):
            full = list(vals)
            for i, v in zip(didx, dv):
                full[i] = v
            return tuple(fn(*full))

        _, vjp = jax.vjp(g, *[vals[i] for i in didx])
        grads = vjp(tuple(r[...].astype(F32) for r in refs[n_in:n_in + n_ct]))
        for gr, i, r in zip(grads, didx, refs[n_in + n_ct:]):
            a = args[i]
            if a.kind == "row":
                r[...] = gr.astype(r.dtype)
            else:
                first = a.first(*ids)

                @pl.when(first)
                def _():
                    r[...] = gr

                @pl.when(jnp.logical_not(first))
                def _():
                    r[...] += gr

    g_specs, g_shapes = [], []
    for i in didx:
        a = args[i]
        if a.kind == "row":
            g_specs.append(pl.BlockSpec(a.block, a.gimap or a.imap))
            g_shapes.append(jax.ShapeDtypeStruct(a.gshape or a.arr.shape, a.gdtype))
        else:
            g_specs.append(pl.BlockSpec(a.block, a.imap))
            g_shapes.append(jax.ShapeDtypeStruct(a.arr.shape, F32))
    return _pcall(
        body, name=name, grid=grid,
        in_specs=[pl.BlockSpec(a.block, a.imap) for a in args] + [pl.BlockSpec(o[2], o[3]) for o in outs],
        out_specs=g_specs, out_shape=g_shapes,
        compiler_params=_cparams(len(grid)),
    )(*[a.arr for a in args], *ct_arrays)


def _rowop(name, fn, grid, args, outs):
    res = _op_fwd(name, fn, grid, args, outs)
    return res, lambda *cts: _op_bwd(name + "_bwd", fn, grid, args, outs, cts)


def _pick(n, cap):
    best = None
    for t in range(LANE, cap + 1, LANE):
        if n % t == 0:
            best = t
    return best or n


def _mm(a, b, mode, out_dtype, name, epi=None, aux=None):
    if mode == "nn":
        (m, k), n = a.shape, b.shape[1]
    elif mode == "nt":
        (m, k), n = a.shape, b.shape[0]
    else:
        (k, m), n = a.shape, b.shape[1]
    tm = 512 if m % 512 == 0 else m
    tn, tk = _pick(n, 1024), _pick(k, MM_K_CAP)
    nk = k // tk
    if mode == "tn":
        a_spec = pl.BlockSpec((tk, tm), lambda i, j, kk: (kk, i))
    else:
        a_spec = pl.BlockSpec((tm, tk), lambda i, j, kk: (i, kk))
    if mode == "nt":
        b_spec = pl.BlockSpec((tn, tk), lambda i, j, kk: (j, kk))
    else:
        b_spec = pl.BlockSpec((tk, tn), lambda i, j, kk: (kk, j))
    dims = {"nn": (1, 0), "nt": (1, 1), "tn": (0, 0)}[mode]
    o_spec = pl.BlockSpec((tm, tn), lambda i, j, kk: (i, j))
    n_aux = 0 if aux is None else 1
    n_out = 2 if epi == "sqrelu" else 1

    def body(*refs):
        a_ref, b_ref = refs[0], refs[1]
        o_refs = refs[2 + n_aux:2 + n_aux + n_out]
        acc = refs[-1]
        kk = pl.program_id(2)
        part = _dg(a_ref[...], b_ref[...], *dims)

        if nk > 1:
            @pl.when(kk == 0)
            def _():
                acc[...] = part

            @pl.when((kk > 0) & (kk < nk - 1))
            def _():
                acc[...] += part

        @pl.when(kk == nk - 1)
        def _():
            r = part if nk == 1 else acc[...] + part
            if epi == "sqrelu":
                o_refs[0][...] = r.astype(o_refs[0].dtype)
                rl = jnp.maximum(r, 0.0)
                o_refs[1][...] = (rl * rl).astype(o_refs[1].dtype)
            elif epi == "dsqrelu":
                pre = refs[2][...].astype(F32)
                o_refs[0][...] = (r * (2.0 * jnp.maximum(pre, 0.0))).astype(o_refs[0].dtype)
            else:
                o_refs[0][...] = r.astype(o_refs[0].dtype)

    res = _pcall(
        body, name=name, grid=(m // tm, n // tn, nk),
        in_specs=[a_spec, b_spec] + [o_spec] * n_aux,
        out_specs=[o_spec] * n_out,
        out_shape=[jax.ShapeDtypeStruct((m, n), out_dtype)] * n_out,
        scratch_shapes=[pltpu.VMEM((tm, tn), F32)],
        compiler_params=_cparams(3),
    )(a, b, *([aux] if aux is not None else []))
    return res if n_out == 2 else res[0]


ROW_CHUNK = 16


def _softmax_chunks(s_scr, n_keys, scale, emit):
    for r0 in range(0, s_scr.shape[0], ROW_CHUNK):
        rows = slice(r0, r0 + ROW_CHUNK)
        s = s_scr[rows, :n_keys]
        e = jnp.exp((s - jnp.max(s, axis=-1, keepdims=True)) * scale)
        emit(rows, e, 1.0 / jnp.sum(e, axis=-1, keepdims=True))


def _attn_fwd_block(q, k, v, scale, s_scr, e_scr, l_scr):
    n = k.shape[0]
    s_scr[:, :n] = _dg(q, k, 1, 1)

    def emit(rows, e, inv_l):
        e_scr[rows, :n] = e.astype(BF16)
        l_scr[rows, :] = jnp.broadcast_to(inv_l, (ROW_CHUNK, LANE))

    _softmax_chunks(s_scr, n, scale, emit)
    return _dg(e_scr[:, :n], v, 1, 0) * l_scr[...]


def _attn_bwd_block(q, k, v, o, do, scale, s_scr, dp_scr, p_scr, ds_scr):
    n = k.shape[0]
    s_scr[:, :n] = _dg(q, k, 1, 1)
    dp_scr[:, :n] = _dg(do, v, 1, 1)

    def emit(rows, e, inv_l):
        p = e * inv_l
        delta = jnp.sum(do[rows, :] * o[rows, :], axis=-1, keepdims=True)
        p_scr[rows, :n] = p.astype(BF16)
        ds_scr[rows, :n] = (p * (dp_scr[rows, :n] - delta) * scale).astype(BF16)

    _softmax_chunks(s_scr, n, scale, emit)
    ds = ds_scr[:, :n]
    return _dg(ds, k, 1, 0), _dg(ds, q, 0, 0), _dg(p_scr[:, :n], do, 0, 0)


def _call_with_exchange(body, xchg, *, name, grid, in_specs, out_specs, out_shape, operands, scratch_shapes=()):
    if xchg is None:
        res = _pcall(body, name=name, grid=grid, in_specs=in_specs, out_specs=out_specs, out_shape=out_shape,
                     scratch_shapes=list(scratch_shapes), compiler_params=_cparams(len(grid)))(*operands)
        return list(res), []
    n_in, n_out, n_sc, n = len(in_specs), len(out_specs), len(scratch_shapes), xchg.n

    def wrapped(*refs):
        ins, x_refs = refs[:n_in], refs[n_in:n_in + n]
        outs, xo_refs = refs[n_in + n:n_in + n + n_out], refs[n_in + n + n_out:n_in + 2 * n + n_out]
        scratch, sems = refs[n_in + 2 * n + n_out:n_in + 2 * n + n_out + n_sc], refs[n_in + 2 * n + n_out + n_sc:]
        ids = [pl.program_id(i) for i in range(len(grid))]

        @pl.when(functools.reduce(jnp.logical_and, [i == 0 for i in ids]))
        def _():
            xchg.start(x_refs, xo_refs, sems)

        body(*ins, *outs, *scratch)

        @pl.when(functools.reduce(jnp.logical_and, [i == g - 1 for i, g in zip(ids, grid)]))
        def _():
            xchg.wait(x_refs, xo_refs, sems)

    res = _pcall(wrapped, name=name, grid=grid, in_specs=list(in_specs) + xchg.specs,
                 out_specs=list(out_specs) + xchg.specs, out_shape=list(out_shape) + xchg.out_shape,
                 scratch_shapes=list(scratch_shapes) + xchg.scratch, compiler_params=_cparams(len(grid)),
                 )(*operands, *xchg.bufs)
    return list(res[:n_out]), list(res[n_out:])


def _mla_attn(q, k, v, tc, ctx_q, name, xchg=None):
    bsz, t_all, _ = q.shape
    n_t = t_all // TB
    grid = (bsz, MLA_HEADS // MLA_HPS, n_t)
    q_spec = pl.BlockSpec((None, TB, MLA_HPS * LANE), lambda b, h, t: (b, t, h))
    kv_spec = pl.BlockSpec((None, t_all, MLA_HPS * LANE), lambda b, h, t: (b, 0, h))
    heads = [slice(i * LANE, (i + 1) * LANE) for i in range(MLA_HPS)]
    scale = MLA_QK ** -0.5
    f32_scr, bf16_scr = pltpu.VMEM((TB, t_all), F32), pltpu.VMEM((TB, t_all), BF16)

    def fwd_body(q_ref, k_ref, v_ref, o_ref, *scr):
        t = pl.program_id(2)

        def run(keys):
            for i, hs in enumerate(heads):
                o_ref[:, hs] = _attn_fwd_block(q_ref[:, hs], k_ref[keys, hs], v_ref[keys, hs], scale,
                                               *scr[3 * i:3 * i + 3])

        @pl.when(t == 0)
        def _():
            if ctx_q:
                run(slice(0, tc))
            else:
                o_ref[...] = jnp.zeros_like(o_ref)

        @pl.when(t > 0)
        def _():
            run(slice(0, t_all))

    (o,), gathered = _call_with_exchange(
        fwd_body, xchg, name=name, grid=grid, in_specs=[q_spec, kv_spec, kv_spec], out_specs=[q_spec],
        out_shape=[jax.ShapeDtypeStruct(q.shape, F32)], operands=(q, k, v),
        scratch_shapes=[f32_scr, bf16_scr, pltpu.VMEM((TB, LANE), F32)] * MLA_HPS)

    def bwd(do, xchg=None):
        def bwd_body(q_ref, k_ref, v_ref, o_ref, do_ref, dq_ref, dk_ref, dv_ref, *scr):
            t = pl.program_id(2)

            def run(keys, first):
                for i, hs in enumerate(heads):
                    dq, dk, dv = _attn_bwd_block(q_ref[:, hs], k_ref[keys, hs], v_ref[keys, hs], o_ref[:, hs],
                                                 do_ref[:, hs], scale, *scr[4 * i:4 * i + 4])
                    dq_ref[:, hs] = dq
                    if first:
                        dk_ref[keys, hs] = dk
                        dv_ref[keys, hs] = dv
                    else:
                        dk_ref[keys, hs] += dk
                        dv_ref[keys, hs] += dv

            @pl.when(t == 0)
            def _():
                dk_ref[...] = jnp.zeros_like(dk_ref)
                dv_ref[...] = jnp.zeros_like(dv_ref)
                if ctx_q:
                    run(slice(0, tc), True)
                else:
                    dq_ref[...] = jnp.zeros_like(dq_ref)

            @pl.when(t > 0)
            def _():
                run(slice(0, t_all), False)

        return _call_with_exchange(
            bwd_body, xchg, name=name + "_bwd", grid=grid, in_specs=[q_spec, kv_spec, kv_spec, q_spec, q_spec],
            out_specs=[q_spec, kv_spec, kv_spec], out_shape=[jax.ShapeDtypeStruct(q.shape, F32)] * 3,
            operands=(q, k, v, o, do), scratch_shapes=[f32_scr, f32_scr, bf16_scr, bf16_scr] * MLA_HPS)

    return o, gathered, bwd


def _swa_block(q, keys, vals, sink, mask):
    qs = jnp.concatenate(list(_split(q, SWA_GROUP, 1)), axis=0)
    sk = jnp.sum(sink, axis=-1, keepdims=True) * (1.0 / LANE)
    s = _nt(qs, keys) * (SWA_HEAD_DIM ** -0.5)
    if mask is not None:
        s = jnp.where(mask, s, NEG_INF)
    o = _nn(_softmax_rows(s, sk), vals)
    return jnp.concatenate(list(_split(o, SWA_GROUP, 0)), axis=1)


def _swa_ctx_block(q, kc, vc, sink):
    return _swa_block(q, kc, vc, sink, None)


def _swa_win_block(q, kc, kw, vc, vw, sink, mask):
    return _swa_block(q, jnp.concatenate([kc, kw], axis=0), jnp.concatenate([vc, vw], axis=0), sink, mask)


def _swa_attn(q, k, p_all, sink_b, tc, ctx_q, name, xchg=None):
    bsz, t_all, _ = q.shape
    n_q = t_all // QB_SWA
    n_cq = tc // QB_SWA
    lat = t_all - tc
    span = QB_SWA + 2 * WINDOW
    gw = SWA_GROUP * LANE
    grid = (bsz, SWA_KV_HEADS, n_q)
    q_spec = pl.BlockSpec((None, QB_SWA, gw), lambda b, g, i: (b, i, g))
    k_spec = pl.BlockSpec((None, t_all, LANE), lambda b, g, i: (b, 0, g))
    v_spec = pl.BlockSpec((None, t_all, LANE), lambda b, g, i: (b, 0, PC_SV // LANE + g))
    s_spec = pl.BlockSpec((None, SWA_GROUP * QB_SWA, LANE), lambda b, g, i: (g, 0, 0))

    def window(i):
        q0 = (i - n_cq) * QB_SWA
        w0 = jnp.clip(q0 - WINDOW, 0, lat - span)
        w0 = pl.multiple_of(w0, QB_SWA)
        shape = (SWA_GROUP * QB_SWA, tc + span)
        qi = q0 + lax.broadcasted_iota(jnp.int32, shape, 0) % QB_SWA
        col = lax.broadcasted_iota(jnp.int32, shape, 1)
        kj = w0 + col - tc
        mask = (col < tc) | ((kj >= qi - WINDOW) & (kj <= qi + WINDOW))
        return w0, mask

    def fwd_body(q_ref, k_ref, v_ref, s_ref, o_ref):
        i = pl.program_id(2)

        @pl.when(i < n_cq)
        def _():
            if ctx_q:
                o_ref[...] = _swa_ctx_block(q_ref[...].astype(F32), k_ref[0:tc, :], v_ref[0:tc, :], s_ref[...])
            else:
                o_ref[...] = jnp.zeros_like(o_ref)

        @pl.when(i >= n_cq)
        def _():
            w0, mask = window(i)
            o_ref[...] = _swa_win_block(q_ref[...].astype(F32), k_ref[0:tc, :], k_ref[pl.ds(tc + w0, span), :],
                                        v_ref[0:tc, :], v_ref[pl.ds(tc + w0, span), :], s_ref[...], mask)

    (o,), gathered = _call_with_exchange(
        fwd_body, xchg, name=name, grid=grid, in_specs=[q_spec, k_spec, v_spec, s_spec], out_specs=[q_spec],
        out_shape=[jax.ShapeDtypeStruct(q.shape, F32)], operands=(q, k, p_all, sink_b))

    def bwd(do, xchg=None):
        def bwd_body(q_ref, k_ref, v_ref, s_ref, do_ref, dq_ref, dk_ref, dv_ref, ds_ref):
            i = pl.program_id(2)

            @pl.when(i == 0)
            def _():
                dk_ref[...] = jnp.zeros_like(dk_ref)
                dv_ref[...] = jnp.zeros_like(dv_ref)
                ds_ref[...] = jnp.zeros_like(ds_ref)

            @pl.when(i < n_cq)
            def _():
                if ctx_q:
                    _, vjp = jax.vjp(_swa_ctx_block, q_ref[...].astype(F32), k_ref[0:tc, :].astype(F32),
                                     v_ref[0:tc, :], s_ref[...])
                    dq, dk, dv, ds = vjp(do_ref[...])
                    dq_ref[...] = dq
                    dk_ref[0:tc, :] += dk
                    dv_ref[0:tc, :] += dv
                    ds_ref[...] += ds
                else:
                    dq_ref[...] = jnp.zeros_like(dq_ref)

            @pl.when(i >= n_cq)
            def _():
                w0, mask = window(i)
                win = pl.ds(tc + w0, span)
                _, vjp = jax.vjp(functools.partial(_swa_win_block, mask=mask), q_ref[...].astype(F32),
                                 k_ref[0:tc, :].astype(F32), k_ref[win, :].astype(F32),
                                 v_ref[0:tc, :], v_ref[win, :], s_ref[...])
                dq, dkc, dkw, dvc, dvw, ds = vjp(do_ref[...])
                dq_ref[...] = dq
                dk_ref[0:tc, :] += dkc
                dk_ref[win, :] += dkw
                dv_ref[0:tc, :] += dvc
                dv_ref[win, :] += dvw
                ds_ref[...] += ds

        kv_out = pl.BlockSpec((None, t_all, LANE), lambda b, g, i: (b, 0, g))
        ds_spec = pl.BlockSpec((None, None, SWA_GROUP * QB_SWA, LANE), lambda b, g, i: (b, g, 0, 0))
        kv_shape = jax.ShapeDtypeStruct((bsz, t_all, SWA_KV_HEADS * LANE), F32)
        return _call_with_exchange(
            bwd_body, xchg, name=name + "_bwd", grid=grid, in_specs=[q_spec, k_spec, v_spec, s_spec, q_spec],
            out_specs=[q_spec, kv_out, kv_out, ds_spec],
            out_shape=[jax.ShapeDtypeStruct(q.shape, F32), kv_shape, kv_shape,
                       jax.ShapeDtypeStruct((bsz,) + sink_b.shape, F32)],
            operands=(q, k, p_all, sink_b, do))

    return o, gathered, bwd


def _scan_rows(a, u, reverse, a_s, u_s, c_s):
    t_all, c = a.shape
    row8 = lax.broadcasted_iota(jnp.int32, a.shape, 0) % SUBLANE
    for d in (1, 2, 4):
        sh = d if not reverse else t_all - d
        ar, ur = pltpu.roll(a, sh, 0), pltpu.roll(u, sh, 0)
        m = (row8 >= d) if not reverse else (row8 < SUBLANE - d)
        u = jnp.where(m, a * ur + u, u)
        a = jnp.where(m, a * ar, a)
    a_s[...] = a
    u_s[...] = u
    n_tiles = t_all // SUBLANE

    def step(j, carry):
        tile = j if not reverse else n_tiles - 1 - j
        base = pl.multiple_of(tile * SUBLANE, SUBLANE)
        c_s[pl.ds(base, SUBLANE), :] = jnp.broadcast_to(carry, (SUBLANE, c))
        last = base + (0 if reverse else SUBLANE - 1)
        return a_s[pl.ds(last, 1), :] * carry + u_s[pl.ds(last, 1), :]

    lax.fori_loop(0, n_tiles, step, jnp.zeros((1, c), F32))
    return a_s[...] * c_s[...] + u_s[...]


def _shift_rows(x, reverse_src):
    t_all = x.shape[0]
    row = lax.broadcasted_iota(jnp.int32, x.shape, 0)
    if reverse_src:
        return jnp.where(row == t_all - 1, 0.0, pltpu.roll(x, t_all - 1, 0))
    return jnp.where(row == 0, 0.0, pltpu.roll(x, 1, 0))


def _lru_scan(a0, u0, a1, u1, name):
    bsz, t_all, w = a0.shape
    grid = (bsz, w // LANE)
    spec = pl.BlockSpec((None, t_all, LANE), lambda b, c: (b, 0, c))
    scratch = [pltpu.VMEM((t_all, LANE), F32)] * 3
    shape = jax.ShapeDtypeStruct(a0.shape, F32)

    def fwd_body(a0_ref, u0_ref, a1_ref, u1_ref, h0_ref, h1_ref, a_s, u_s, c_s):
        h0_ref[...] = _scan_rows(a0_ref[...], u0_ref[...], False, a_s, u_s, c_s)
        h1_ref[...] = _scan_rows(a1_ref[...], u1_ref[...], True, a_s, u_s, c_s)

    h0, h1 = _pcall(fwd_body, name=name, grid=grid, in_specs=[spec] * 4, out_specs=[spec] * 2,
                    out_shape=[shape] * 2, scratch_shapes=scratch, compiler_params=_cparams(2))(a0, u0, a1, u1)

    def bwd(dh0, dh1):
        def bwd_body(a0_ref, h0_ref, g0_ref, a1_ref, h1_ref, g1_ref, da0_ref, du0_ref, da1_ref, du1_ref,
                     a_s, u_s, c_s):
            g0 = _scan_rows(_shift_rows(a0_ref[...], True), g0_ref[...], True, a_s, u_s, c_s)
            du0_ref[...] = g0
            da0_ref[...] = g0 * _shift_rows(h0_ref[...], False)
            g1 = _scan_rows(_shift_rows(a1_ref[...], False), g1_ref[...], False, a_s, u_s, c_s)
            du1_ref[...] = g1
            da1_ref[...] = g1 * _shift_rows(h1_ref[...], True)

        return _pcall(bwd_body, name=name + "_bwd", grid=grid, in_specs=[spec] * 6, out_specs=[spec] * 4,
                      out_shape=[shape] * 4, scratch_shapes=scratch,
                      compiler_params=_cparams(2))(a0, h0, dh0, a1, h1, dh1)

    return h0, h1, bwd


def _f_mod(x, g, shift, scale):
    return (_rms(x, g, D_MODEL) * (1.0 + scale) + shift,)


def _f_mla_q(cq, ga, w, gh, cos, sa, sb):
    n = _rms(cq, ga, MLA_Q_RANK)
    outs = []
    for wh in _split(w, MLA_HEADS, 1):
        outs.append(_rope(_rms(_nn(n, wh), gh, MLA_QK), cos, sa, sb, MLA_ROPE // 4))
    return (jnp.concatenate(outs, axis=1),)


def _f_mla_kv(ckv, krp, ga, wk, wv, gh, cos, sa, sb):
    n = _rms(ckv, ga, MLA_KV_RANK)
    outs = []
    for wh in _split(wk, MLA_HEADS, 1):
        outs.append(_rope(_rms(_nn(n, wh) + krp, gh, MLA_QK), cos, sa, sb, MLA_ROPE // 4))
    return jnp.concatenate(outs, axis=1), _nn(n, wv)


def _f_conv(x, w0, w1, w2, w3, bias, tc):
    t_all = x.shape[0]
    row = lax.broadcasted_iota(jnp.int32, x.shape, 0)
    lo = jnp.where(row < tc, 0, tc)
    hi = jnp.where(row < tc, tc, t_all)
    y = bias + jnp.zeros_like(x)
    for kk, wk in enumerate((w0, w1, w2, w3)):
        src = row + (kk - 2)
        xs = x if kk == 2 else _roll(x, 2 - kk, 0)
        y = y + wk * jnp.where((src >= lo) & (src < hi), xs, 0.0)
    return (y,)


def _f_gates(xc, w16, b00, b01, b10, b11, sp0, sp1):
    ws = _unstack(w16)
    n_cb = LRU_WIDTH // LANE
    xcs = _split(xc, n_cb, 1)
    bias = [_split(b, n_cb, 1) for b in (b00, b01, b10, b11)]
    sps = [_split(s, n_cb, 1) for s in (sp0, sp1)]
    res = [[], [], [], []]
    for c in range(n_cb):
        for z in range(2):
            r = _sig(_nn(xcs[c], ws[c * 4 + 2 * z]) + bias[2 * z][c])
            i = _sig(_nn(xcs[c], ws[c * 4 + 2 * z + 1]) + bias[2 * z + 1][c])
            la = -LRU_C * r * sps[z][c]
            res[2 * z].append(jnp.exp(la))
            res[2 * z + 1].append(jnp.sqrt(-jnp.tanh(la) * (jnp.exp(2.0 * la) + 1.0)) * (i * xcs[c]))
    return tuple(jnp.concatenate(r, axis=1) for r in res)


def _f_lru_out(h0, h1, lg):
    return ((h0 + h1) * _gelu(lg),)


def _f_swa_qk(sq, sk, gq, gk, cos, sa, sb):
    qs = [_rope(_rms(x, gq, SWA_HEAD_DIM), cos, sa, sb, SWA_HEAD_DIM // 4) for x in _split(sq, SWA_HEADS, 1)]
    ks = [_rope(_rms(x, gk, SWA_HEAD_DIM), cos, sa, sb, SWA_HEAD_DIM // 4) for x in _split(sk, SWA_KV_HEADS, 1)]
    return jnp.concatenate(qs, axis=1), jnp.concatenate(ks, axis=1)


def _f_merge(oa, ob, oc, ga, gb, gc):
    return (jnp.concatenate([_rms(oa, ga, GROUP_WIDTH), _rms(ob, gb, GROUP_WIDTH), _rms(oc, gc, GROUP_WIDTH)],
                            axis=1),)


def _f_resid_mod(x, y, gate, g, shift, scale):
    x1 = x + gate * y
    return x1, _rms(x1, g, D_MODEL) * (1.0 + scale) + shift


def _f_resid(x, y, gate):
    return (x + gate * y,)


def _hosted(hooks, key, arg=None):
    make, done = hooks.get(key, (None, None))
    xchg = make(arg) if make is not None else None
    return xchg, (done if xchg is not None else lambda outs: None)


def _layer(li, x, mods, w, s, tabs, tc, ctx_q, hooks):
    bsz, t_all, _ = x.shape
    n_t = t_all // TB
    grid = (bsz, n_t)
    rows = lambda b, t: (b, t, 0)

    def row(arr, width=None, idx=0, gdtype=F32, gshape=None):
        width = width or arr.shape[-1]
        return _A(arr, (None, TB, width), lambda b, t: (b, t, idx), "row", gdtype=gdtype, gshape=gshape,
                  gimap=rows if gshape is not None else None)

    def out(width, dtype, imap=rows):
        return ((bsz, t_all, width), dtype, (None, TB, width), imap)

    def modarg(arr):
        return _A(arr, (None, None, 1, D_MODEL), lambda b, t: (b, jnp.minimum(t, 1), 0, 0), "acc",
                  first=lambda b, t: t <= 1)

    def tab(arr):
        return _A(arr, (TB, LANE), lambda b, t: (t, 0), "const")

    def pcol(p_all, col, width):
        return row(p_all, width, col // width, gdtype=BF16, gshape=(bsz, t_all, width))

    nm = lambda base: "%s_l%d" % (base, li)
    sh1, sc1, g1, sh2, sc2, g2 = mods
    m_all = bsz * t_all

    (h,), b_mod1 = _rowop(nm("mod1"), _f_mod, grid, [row(x), _par(s["norm1_g"]), modarg(sh1), modarg(sc1)],
                          [out(D_MODEL, BF16)])
    p_all = _mm(h.reshape(m_all, D_MODEL), w["win"], "nn", F32, nm("mm_in")).reshape(bsz, t_all, P_WIDTH)

    tq, tk_ = tabs["mla"], tabs["mla"]
    (q_a,), b_mq = _rowop(nm("mla_q"), _f_mla_q, grid,
                          [pcol(p_all, PC_CQ, 256), _par(s["q_a_g"]), _par(w["wuq"]), _par(s["mla_q_g"])]
                          + [tab(a) for a in tq], [out(MLA_HEADS * LANE, BF16)])
    (k_a, v_a), b_mkv = _rowop(nm("mla_kv"), _f_mla_kv, grid,
                               [pcol(p_all, PC_CKV, 128), pcol(p_all, PC_KR, 128), _par(s["kv_a_g"]), _par(w["wk"]),
                                _par(w["wv"]), _par(s["mla_k_g"])] + [tab(a) for a in tk_],
                               [out(MLA_HEADS * LANE, BF16), out(MLA_HEADS * LANE, BF16)])
    xchg, done = _hosted(hooks, "mla_fwd")
    o_a, got, b_attn_a = _mla_attn(q_a, k_a, v_a, tc, ctx_q, nm("mla_attn"), xchg)
    done(got)

    n_cb = LRU_WIDTH // LANE
    conv_grid = (n_cb, bsz)
    cpar = lambda arr: _A(arr, (1, LANE), lambda c, b: (0, c), "acc", first=lambda c, b: b == 0)
    conv_args = [_A(p_all, (None, t_all, LANE), lambda c, b: (b, 0, PC_LX // LANE + c), "row", gdtype=BF16,
                    gshape=(bsz, t_all, LRU_WIDTH), gimap=lambda c, b: (b, 0, c))]
    conv_args += [cpar(a) for a in s["conv_w"]] + [cpar(s["conv_b"])]
    conv_out = [((bsz, t_all, LRU_WIDTH), F32, (None, t_all, LANE), lambda c, b: (b, 0, c))]
    (xc,), b_conv = _rowop(nm("lru_conv"), functools.partial(_f_conv, tc=tc), conv_grid, conv_args, conv_out)
    rot = lambda b, t: (b, (t + n_t - 1) % n_t, 0)
    (a0, u0, a1, u1), b_gates = _rowop(
        nm("lru_gates"), _f_gates, grid,
        [row(xc), _par(s["wbd"])] + [_par(a) for a in s["gate_b"]] + [_par(a) for a in s["sp"]],
        [out(LRU_WIDTH, F32), out(LRU_WIDTH, F32), out(LRU_WIDTH, F32, rot), out(LRU_WIDTH, F32, rot)])
    h0, h1, b_scan = _lru_scan(a0, u0, a1, u1, nm("lru_scan"))
    h1_arg = _A(h1, (None, TB, LRU_WIDTH), rot, "row")
    (o_b,), b_lout = _rowop(nm("lru_out"), _f_lru_out, grid, [row(h0), h1_arg, pcol(p_all, PC_LG, 512)],
                            [out(LRU_WIDTH, F32)])

    ts = tabs["swa"]
    (q_c, k_c), b_sqk = _rowop(nm("swa_qk"), _f_swa_qk, grid,
                               [pcol(p_all, PC_SQ, 1024), pcol(p_all, PC_SK, 256), _par(s["swa_q_g"]),
                                _par(s["swa_k_g"])] + [tab(a) for a in ts],
                               [out(SWA_HEADS * LANE, BF16), out(SWA_KV_HEADS * LANE, BF16)])
    xchg, done = _hosted(hooks, "swa_fwd")
    o_c, got, b_attn_c = _swa_attn(q_c, k_c, p_all, s["sink_b"], tc, ctx_q, nm("swa_attn"), xchg)
    done(got)

    (y_in,), b_merge = _rowop(nm("merge"), _f_merge, grid,
                              [row(o_a), row(o_b), row(o_c), _par(s["g_a"]), _par(s["g_b"]), _par(s["g_c"])],
                              [out(MIX_P, BF16)])
    y = _mm(y_in.reshape(m_all, MIX_P), w["wout"], "nn", F32, nm("mm_out")).reshape(bsz, t_all, D_MODEL)
    (x1, hm), b_rm = _rowop(nm("resid_mod"), _f_resid_mod, grid,
                            [row(x), row(y, gdtype=BF16), modarg(g1), _par(s["norm2_g"]), modarg(sh2), modarg(sc2)],
                            [out(D_MODEL, F32), out(D_MODEL, BF16)])
    pre, act = _mm(hm.reshape(m_all, D_MODEL), w["ff1"], "nn", BF16, nm("mm_ff1"), epi="sqrelu")
    y2 = _mm(act, w["ff2"], "nn", F32, nm("mm_ff2")).reshape(bsz, t_all, D_MODEL)
    (x2,), b_res = _rowop(nm("resid"), _f_resid, grid, [row(x1), row(y2, gdtype=BF16), modarg(g2)],
                          [out(D_MODEL, F32)])

    def bwd(dx2, hooks):
        dw, ds = {}, {}
        dx1a, dy2, dg2 = b_res(dx2)
        dy2 = dy2.reshape(m_all, D_MODEL)
        dpre = _mm(dy2, w["ff2"], "nt", BF16, nm("mm_ff2_dx"), epi="dsqrelu", aux=pre)
        dw["ff2"] = _mm(act, dy2, "tn", F32, nm("mm_ff2_dw"))
        dhm = _mm(dpre, w["ff1"], "nt", F32, nm("mm_ff1_dx")).reshape(bsz, t_all, D_MODEL)
        dw["ff1"] = _mm(hm.reshape(m_all, D_MODEL), dpre, "tn", F32, nm("mm_ff1_dw"))
        dxa, dy, dg1, ds["norm2_g"], dsh2, dsc2 = b_rm(dx1a, dhm)
        dy = dy.reshape(m_all, D_MODEL)
        dy_in = _mm(dy, w["wout"], "nt", F32, nm("mm_out_dx")).reshape(bsz, t_all, MIX_P)
        dw["wout"] = _mm(y_in.reshape(m_all, MIX_P), dy, "tn", F32, nm("mm_out_dw"))
        do_a, do_b, do_c, ds["g_a"], ds["g_b"], ds["g_c"] = b_merge(dy_in)

        (dq_c, dk_c, dsv, dsink), _ = b_attn_c(do_c)
        ds["sink_b"] = jnp.sum(dsink, axis=0)
        dsq, dsk, ds["swa_q_g"], ds["swa_k_g"] = b_sqk(dq_c, dk_c)

        dh0, dh1, dlg = b_lout(do_b)
        da0, du0, da1, du1 = b_scan(dh0, dh1)
        gates_g = b_gates(da0, du0, da1, du1)
        dxc, ds["wbd"] = gates_g[0], gates_g[1]
        ds["gate_b"], ds["sp"] = list(gates_g[2:6]), list(gates_g[6:8])
        conv_g = b_conv(dxc)
        dlx, ds["conv_w"], ds["conv_b"] = conv_g[0], list(conv_g[1:5]), conv_g[5]

        xchg, done = _hosted(hooks, "mla_bwd", dw)
        (dq_a, dk_a, dv_a), got = b_attn_a(do_a, xchg)
        done(got)
        dcq, ds["q_a_g"], dw["wuq"], ds["mla_q_g"] = b_mq(dq_a)
        dckv, dkr, ds["kv_a_g"], dw["wk"], dw["wv"], ds["mla_k_g"] = b_mkv(dk_a, dv_a)

        dp = jnp.concatenate([dsq, dlx, dlg, dcq, dsk, dsv.astype(BF16), dckv, dkr], axis=-1)
        dp = dp.reshape(m_all, P_WIDTH)
        dh = _mm(dp, w["win"], "nt", F32, nm("mm_in_dx")).reshape(bsz, t_all, D_MODEL)
        dw["win"] = _mm(h.reshape(m_all, D_MODEL), dp, "tn", F32, nm("mm_in_dw"))
        dxb, ds["norm1_g"], dsh1, dsc1 = b_mod1(dh)
        return dxa + dxb, [dsh1, dsc1, dg1, dsh2, dsc2, dg2], dw, ds

    return x2, bwd


def _loss_and_grad(x2, target, tc):
    bsz, t_all, d = x2.shape
    n_t = t_all // TB
    n_c = tc // TB

    def body(x_ref, t_ref, l_ref, dx_ref):
        b, t = pl.program_id(0), pl.program_id(1)

        @pl.when((b == 0) & (t == 0))
        def _():
            l_ref[...] = jnp.zeros_like(l_ref)

        @pl.when(t < n_c)
        def _():
            dx_ref[...] = jnp.zeros_like(dx_ref)

        @pl.when(t >= n_c)
        def _():
            e = x_ref[...] - t_ref[...]
            dx_ref[...] = e * (1.0 / d)
            l_ref[...] += jnp.sum(e * e) * (0.5 / d)

    loss, dx = _pcall(
        body, name="loss", grid=(bsz, n_t),
        in_specs=[pl.BlockSpec((None, TB, d), lambda b, t: (b, t, 0)),
                  pl.BlockSpec((None, TB, d), lambda b, t: (b, jnp.maximum(t - n_c, 0), 0))],
        out_specs=[pl.BlockSpec((SUBLANE, LANE), lambda b, t: (0, 0)),
                   pl.BlockSpec((None, TB, d), lambda b, t: (b, t, 0))],
        out_shape=[jax.ShapeDtypeStruct((SUBLANE, LANE), F32), jax.ShapeDtypeStruct(x2.shape, F32)],
        compiler_params=_cparams(2))(x2, target)
    return loss[0, 0], dx


def _rope_tables(lat, tc, dim, lane0):
    quarter = dim // 4
    pos = jnp.arange(lat)
    grid_pos = jnp.stack([pos // GRID_W, pos % GRID_W], axis=-1).astype(F32)
    lane = jnp.arange(LANE)
    p = jnp.clip(lane - lane0, 0, dim - 1)
    active = (lane >= lane0) & (lane < lane0 + dim)
    axis, half, qi = p // (dim // 2), (p % (dim // 2)) // quarter, p % quarter
    inv = ROPE_THETA ** (-qi.astype(F32) / quarter)
    ang = jnp.where(axis[None, :] == 0, grid_pos[:, 0:1], grid_pos[:, 1:2]) * inv[None, :]
    cos = jnp.where(active, jnp.cos(ang), 1.0)
    sin = jnp.where(active, jnp.sin(ang), 0.0)
    sa = jnp.where(half == 0, -sin, 0.0)
    sb = jnp.where(half == 1, sin, 0.0)
    ctx1, ctx0 = jnp.ones((tc, LANE), F32), jnp.zeros((tc, LANE), F32)
    return (jnp.concatenate([ctx1, cos], 0), jnp.concatenate([ctx0, sa], 0), jnp.concatenate([ctx0, sb], 0))


_BIG = {"w_in": ((D_MODEL, IN_WIDTH // N_DEV), 1, ("win",)),
        "w_uq": ((MLA_Q_RANK, MLA_HEADS * MLA_QK // N_DEV), 1, ("wuq",)),
        "w_ukv": ((MLA_KV_RANK, MLA_HEADS * (MLA_NOPE + MLA_V) // N_DEV), 1, ("wk", "wv")),
        "w_out": ((3 * GROUP_WIDTH // N_DEV, D_MODEL), 0, ("wout",)),
        "w_ff1": ((D_MODEL, D_FF // N_DEV), 1, ("ff1",)),
        "w_ff2": ((D_FF // N_DEV, D_MODEL), 0, ("ff2",))}
_EARLY = ("w_in", "w_uq", "w_ukv")
_LATE = ("w_out", "w_ff1", "w_ff2")


def _pad_heads(wm, n_heads, dim, axis=-1):
    axis = axis % wm.ndim
    shp = wm.shape[:axis] + (n_heads, dim) + wm.shape[axis + 1:]
    pad = [(0, 0)] * len(shp)
    pad[axis + 1] = (0, LANE - dim)
    out = jnp.pad(wm.reshape(shp), pad)
    return out.reshape(wm.shape[:axis] + (n_heads * LANE,) + wm.shape[axis + 1:])


def _prep_weight(name, piece):
    shp, ax, _ = _BIG[name]
    full = jnp.moveaxis(piece, 0, ax).reshape(shp[:ax] + (N_DEV * shp[ax],) + shp[ax + 1:])
    if name == "w_in":
        cq, ckv, kr, lx, lg, sq, sk, sv = _split_cols(full)
        return {"win": jnp.concatenate(
            [_pad_heads(sq, SWA_HEADS, SWA_HEAD_DIM), lx, lg, cq, _pad_heads(sk, SWA_KV_HEADS, SWA_HEAD_DIM),
             _pad_heads(sv, SWA_KV_HEADS, SWA_HEAD_DIM), ckv, jnp.pad(kr, ((0, 0), (MLA_NOPE, LANE - MLA_QK)))], axis=1)}
    if name == "w_uq":
        return {"wuq": _pad_heads(full, MLA_HEADS, MLA_QK)}
    if name == "w_ukv":
        ukv = full.reshape(MLA_KV_RANK, MLA_HEADS, MLA_NOPE + MLA_V)
        return {"wk": _pad_heads(ukv[:, :, :MLA_NOPE].reshape(MLA_KV_RANK, -1), MLA_HEADS, MLA_NOPE),
                "wv": _pad_heads(ukv[:, :, MLA_NOPE:].reshape(MLA_KV_RANK, -1), MLA_HEADS, MLA_V)}
    if name == "w_out":
        return {"wout": jnp.concatenate(
            [_pad_heads(full[:GROUP_WIDTH], MLA_HEADS, MLA_V, axis=0), full[GROUP_WIDTH:2 * GROUP_WIDTH],
             _pad_heads(full[2 * GROUP_WIDTH:], SWA_HEADS, SWA_HEAD_DIM, axis=0)], axis=0)}
    return {_BIG[name][2][0]: full}


def _split_cols(wm):
    parts, start = [], 0
    for size in IN_SIZES:
        parts.append(wm[:, start:start + size])
        start += size
    return parts


def _prep_small(raw):
    r1 = lambda a: a.reshape(1, -1)
    gw = raw["lru_gate_w"].reshape(2, 2, 4, 2, 64, 64)
    wbd = jnp.einsum("zgknCm,nN->knCzgNm", gw, jnp.eye(2, dtype=F32)).reshape(4, LANE, 4, LANE)
    gg = raw["group_g"]
    sink = raw["swa_sink"].reshape(SWA_KV_HEADS, SWA_GROUP, 1, 1)
    return {
        "norm1_g": r1(raw["norm1_g"]), "norm2_g": r1(raw["norm2_g"]),
        "q_a_g": r1(raw["q_a_g"]), "kv_a_g": r1(raw["kv_a_g"]),
        "mla_q_g": jnp.pad(r1(raw["mla_q_g"]), ((0, 0), (0, LANE - MLA_QK))),
        "mla_k_g": jnp.pad(r1(raw["mla_k_g"]), ((0, 0), (0, LANE - MLA_QK))),
        "swa_q_g": jnp.pad(r1(raw["swa_q_g"]), ((0, 0), (0, LANE - SWA_HEAD_DIM))),
        "swa_k_g": jnp.pad(r1(raw["swa_k_g"]), ((0, 0), (0, LANE - SWA_HEAD_DIM))),
        "conv_w": [r1(raw["conv_w"][kk]) for kk in range(4)], "conv_b": r1(raw["conv_b"]),
        "wbd": wbd.transpose(0, 2, 1, 3).reshape(16, LANE, LANE),
        "gate_b": [r1(raw["lru_gate_b"][z, g]) for z in range(2) for g in range(2)],
        "sp": [r1(jax.nn.softplus(-raw["lru_lambda"][z])) for z in range(2)],
        "sink_b": jnp.broadcast_to(sink, (SWA_KV_HEADS, SWA_GROUP, QB_SWA, LANE)).reshape(
            SWA_KV_HEADS, SWA_GROUP * QB_SWA, LANE),
        "g_a": _pad_heads(r1(gg[:GROUP_WIDTH]), MLA_HEADS, MLA_V), "g_b": r1(gg[GROUP_WIDTH:2 * GROUP_WIDTH]),
        "g_c": _pad_heads(r1(gg[2 * GROUP_WIDTH:]), SWA_HEADS, SWA_HEAD_DIM)}


def _mesh_pos():
    return lax.axis_index("x"), lax.axis_index("y"), lax.axis_index("c")


def _peer(pos, k):
    return tuple(1 - p if (k >> s) & 1 else p for p, s in zip(pos, (2, 1, 0)))


def _dev_index(pos):
    return 4 * pos[0] + 2 * pos[1] + pos[2]


class _Exchange:
    def __init__(self, bufs, gather):
        self.bufs = list(bufs)
        self.n = len(self.bufs)
        self.gather = [gather] * self.n if isinstance(gather, bool) else list(gather)
        self.specs = [pl.BlockSpec(memory_space=pl.ANY)] * self.n
        self.out_shape = [jax.ShapeDtypeStruct((N_DEV,) + tuple(b.shape if g else b.shape[1:]), b.dtype)
                          for b, g in zip(self.bufs, self.gather)]
        self.scratch = [pltpu.SemaphoreType.DMA(((N_DEV - 1) * self.n,)),
                        pltpu.SemaphoreType.DMA(((N_DEV - 1) * self.n,)), pltpu.SemaphoreType.DMA((self.n,))]

    def _copies(self, x_refs, o_refs, sems, with_recvs):
        send_sems, recv_sems, local_sems = sems
        pos = _mesh_pos()
        me = _dev_index(pos)
        locals_, sends, recvs = [], [], []
        for j in range(self.n):
            src_mine = x_refs[j] if self.gather[j] else x_refs[j].at[me]
            locals_.append(pltpu.make_async_copy(src_mine, o_refs[j].at[me], local_sems.at[j]))
        for k in range(1, N_DEV):
            peer = _peer(pos, k)
            pidx = _dev_index(peer)
            for j in range(self.n):
                src = x_refs[j] if self.gather[j] else x_refs[j].at[pidx]
                sem = (k - 1) * self.n + j
                sends.append(pltpu.make_async_remote_copy(
                    src_ref=src, dst_ref=o_refs[j].at[me], send_sem=send_sems.at[sem], recv_sem=recv_sems.at[sem],
                    device_id=peer, device_id_type=pl.DeviceIdType.MESH))
                if with_recvs:
                    recvs.append(pltpu.make_async_remote_copy(
                        src_ref=src, dst_ref=o_refs[j].at[pidx], send_sem=send_sems.at[sem],
                        recv_sem=recv_sems.at[sem], device_id=peer, device_id_type=pl.DeviceIdType.MESH))
        return locals_, sends, recvs

    def start(self, x_refs, o_refs, sems):
        locals_, sends, _ = self._copies(x_refs, o_refs, sems, False)
        for cp in locals_ + sends:
            cp.start()

    def wait(self, x_refs, o_refs, sems):
        locals_, sends, recvs = self._copies(x_refs, o_refs, sems, True)
        for cp in recvs:
            cp.wait_recv()
        for cp in sends:
            cp.wait_send()
        for cp in locals_:
            cp.wait()


def _exchange(bufs, gather, name):
    xchg = _Exchange(bufs, gather)
    n = xchg.n

    def body(*refs):
        xchg.start(refs[:n], refs[n:2 * n], refs[2 * n:])
        xchg.wait(refs[:n], refs[n:2 * n], refs[2 * n:])

    return _pcall(body, name=name, out_shape=xchg.out_shape, in_specs=xchg.specs, out_specs=xchg.specs,
                  scratch_shapes=xchg.scratch)(*xchg.bufs)


def _pack(arrs, dtype):
    flat = jnp.concatenate([a.reshape(-1).astype(dtype) for a in arrs])
    rows = -(-flat.size // PACK_W)
    rows = -(-rows // 16) * 16
    return jnp.pad(flat, (0, rows * PACK_W - flat.size)).reshape(rows, PACK_W)


def _unpack(buf, shapes, lead=()):
    flat = buf.reshape(lead + (-1,))
    out, off = [], 0
    for shp in shapes:
        n = math.prod(shp)
        out.append(flat[..., off:off + n].reshape(lead + tuple(shp)))
        off += n
    return out


def _sum_sources(buf, name):
    _, r, c = buf.shape
    tr = _rows_tile(r)

    def body(x_ref, o_ref):
        acc = x_ref[0]
        for d in range(1, N_DEV):
            acc = acc + x_ref[d]
        o_ref[...] = acc

    return _pcall(body, name=name, grid=(r // tr,),
                  in_specs=[pl.BlockSpec((N_DEV, tr, c), lambda i: (0, i, 0))],
                  out_specs=pl.BlockSpec((tr, c), lambda i: (i, 0)),
                  out_shape=jax.ShapeDtypeStruct((r, c), F32), compiler_params=_cparams(1))(buf)


def _rows_tile(r):
    best = r
    for t in range(SUBLANE, 257, SUBLANE):
        if r % t == 0:
            best = t
    return best


def _adamw(grads, wgt, m, v, name):
    n_src, r, c = grads.shape
    tr = _rows_tile(r)
    bc1 = 1.0 - ADAM_B1 ** ADAM_STEP
    bc2 = 1.0 - ADAM_B2 ** ADAM_STEP

    def body(g_ref, w_ref, m_ref, v_ref, go_ref, d_ref, mo_ref, vo_ref):
        g = g_ref[0].astype(F32)
        for d in range(1, n_src):
            g = g + g_ref[d].astype(F32)
        m_new = ADAM_B1 * m_ref[...] + (1.0 - ADAM_B1) * g
        v_new = ADAM_B2 * v_ref[...] + (1.0 - ADAM_B2) * (g * g)
        go_ref[...] = g
        mo_ref[...] = m_new
        vo_ref[...] = v_new
        d_ref[...] = -ADAM_LR * ((m_new / bc1) / (jnp.sqrt(v_new / bc2) + ADAM_EPS) + ADAM_WD * w_ref[...])

    spec = pl.BlockSpec((tr, c), lambda i: (i, 0))
    return _pcall(body, name=name, grid=(r // tr,),
                  in_specs=[pl.BlockSpec((n_src, tr, c), lambda i: (0, i, 0)), spec, spec, spec],
                  out_specs=[spec] * 4, out_shape=[jax.ShapeDtypeStruct((r, c), F32)] * 4,
                  compiler_params=_cparams(1))(grads, wgt, m, v)


def _silu(z):
    return z * jax.nn.sigmoid(z)


_WEIGHTS = ("c_ctx", "w_mod", "b_mod", "norm1_g", "w_in", "q_a_g", "w_uq", "kv_a_g", "w_ukv", "mla_q_g", "mla_k_g",
            "conv_w", "conv_b", "lru_gate_w", "lru_gate_b", "lru_lambda", "swa_q_g", "swa_k_g", "swa_sink", "group_g",
            "w_out", "norm2_g", "w_ff1", "w_ff2")
_SHARDED_SMALL = ("conv_w", "lru_gate_b", "lru_lambda")
_REPL_RAW = ("norm1_g", "q_a_g", "kv_a_g", "mla_q_g", "mla_k_g", "conv_b", "lru_gate_w", "swa_q_g", "swa_k_g",
             "swa_sink", "group_g", "norm2_g")
MOD_ROWS = 32


def kernel(x, c, ctx, c_ctx, w_mod, b_mod, norm1_g, w_in, q_a_g, w_uq, kv_a_g, w_ukv, mla_q_g, mla_k_g, conv_w, conv_b, lru_gate_w, lru_gate_b, lru_lambda, swa_q_g, swa_k_g, swa_sink, group_g, w_out, norm2_g, w_ff1, w_ff2, loss_target, m_c_ctx, m_w_mod, m_b_mod, m_norm1_g, m_w_in, m_q_a_g, m_w_uq, m_kv_a_g, m_w_ukv, m_mla_q_g, m_mla_k_g, m_conv_w, m_conv_b, m_lru_gate_w, m_lru_gate_b, m_lru_lambda, m_swa_q_g, m_swa_k_g, m_swa_sink, m_group_g, m_w_out, m_norm2_g, m_w_ff1, m_w_ff2, v_c_ctx, v_w_mod, v_b_mod, v_norm1_g, v_w_in, v_q_a_g, v_w_uq, v_kv_a_g, v_w_ukv, v_mla_q_g, v_mla_k_g, v_conv_w, v_conv_b, v_lru_gate_w, v_lru_gate_b, v_lru_lambda, v_swa_q_g, v_swa_k_g, v_swa_sink, v_group_g, v_w_out, v_norm2_g, v_w_ff1, v_w_ff2):
    wts = dict(c_ctx=c_ctx, w_mod=w_mod, b_mod=b_mod, norm1_g=norm1_g, w_in=w_in, q_a_g=q_a_g, w_uq=w_uq,
               kv_a_g=kv_a_g, w_ukv=w_ukv, mla_q_g=mla_q_g, mla_k_g=mla_k_g, conv_w=conv_w, conv_b=conv_b,
               lru_gate_w=lru_gate_w, lru_gate_b=lru_gate_b, lru_lambda=lru_lambda, swa_q_g=swa_q_g, swa_k_g=swa_k_g,
               swa_sink=swa_sink, group_g=group_g, w_out=w_out, norm2_g=norm2_g, w_ff1=w_ff1, w_ff2=w_ff2)
    mom1 = dict(zip(_WEIGHTS, (m_c_ctx, m_w_mod, m_b_mod, m_norm1_g, m_w_in, m_q_a_g, m_w_uq, m_kv_a_g, m_w_ukv,
                               m_mla_q_g, m_mla_k_g, m_conv_w, m_conv_b, m_lru_gate_w, m_lru_gate_b, m_lru_lambda,
                               m_swa_q_g, m_swa_k_g, m_swa_sink, m_group_g, m_w_out, m_norm2_g, m_w_ff1, m_w_ff2)))
    mom2 = dict(zip(_WEIGHTS, (v_c_ctx, v_w_mod, v_b_mod, v_norm1_g, v_w_in, v_q_a_g, v_w_uq, v_kv_a_g, v_w_ukv,
                               v_mla_q_g, v_mla_k_g, v_conv_w, v_conv_b, v_lru_gate_w, v_lru_gate_b, v_lru_lambda,
                               v_swa_q_g, v_swa_k_g, v_swa_sink, v_group_g, v_w_out, v_norm2_g, v_w_ff1, v_w_ff2)))
    bsz = x.shape[0]
    n_ex = bsz * N_DEV
    me = _dev_index(_mesh_pos())
    mod_cols = w_mod.shape[-1]

    small_shapes = [c.shape, conv_w.shape, lru_gate_b.shape, lru_lambda.shape]
    (g_small,) = _exchange([_pack([c, conv_w, lru_gate_b, lru_lambda], F32)], True, "ag_small")
    c_all, conv_w_all, gate_b_all, lam_all = _unpack(g_small, small_shapes, lead=(N_DEV,))
    c_all = c_all.reshape(n_ex, D_MODEL)
    cat_last = lambda a: jnp.moveaxis(a, 0, -2).reshape(a.shape[1:-1] + (N_DEV * a.shape[-1],))
    conv_w_full, gate_b_full, lam_full = cat_last(conv_w_all), cat_last(gate_b_all), cat_last(lam_all)

    act = jnp.zeros((MOD_ROWS, D_MODEL), F32).at[:n_ex].set(_silu(c_all)).at[n_ex].set(_silu(c_ctx))
    mod_part = jnp.concatenate([_mm(act, w_mod[li], "nn", F32, "mm_mod_l%d" % li) for li in range(DEPTH)], axis=1)
    (mod_all,) = _exchange([mod_part], True, "ag_mod")
    mods = []
    for li in range(DEPTH):
        full = jnp.moveaxis(mod_all[:, :, li * mod_cols:(li + 1) * mod_cols], 0, 1).reshape(MOD_ROWS, -1) + b_mod[li]
        mine = lax.dynamic_slice_in_dim(full, me * bsz, bsz, axis=0)
        ctx_row = jnp.broadcast_to(full[n_ex], mine.shape)
        both = jnp.stack([ctx_row, mine], axis=1).reshape(bsz, 2, N_MOD, 1, D_MODEL)
        mods.append([both[:, :, j] for j in range(N_MOD)])

    raw = {n: wts[n] for n in _REPL_RAW}
    raw.update(conv_w=conv_w_full, lru_gate_b=gate_b_full, lru_lambda=lam_full)
    small_names = list(_REPL_RAW) + list(_SHARDED_SMALL)
    sp, small_vjp = [None] * DEPTH, [None] * DEPTH
    for li in range(DEPTH):
        sp[li], small_vjp[li] = jax.vjp(_prep_small, {n: raw[n][li] for n in small_names})

    w, w_vjp, g_recv, small_recv = [{} for _ in range(DEPTH)], {}, {}, {}
    shard = lambda n, li: wts[n][li].astype(BF16)

    def take(li, names, pieces):
        for n, piece in zip(names, pieces):
            out, w_vjp[n, li] = jax.vjp(functools.partial(_prep_weight, n), piece)
            w[li].update(out)

    def gather_hook(li, names):
        return (lambda _: _Exchange([shard(n, li) for n in names], True), lambda got: take(li, names, got))

    def wgrad(n, li, dwl):
        (g,) = w_vjp[n, li]({k: dwl[k].astype(BF16) for k in _BIG[n][2]})
        return g

    def small_pack(li, ds_l, extra=()):
        (d_raw,) = small_vjp[li](ds_l)
        return _pack([d_raw[n] for n in small_names] + list(extra), F32)

    take(0, _EARLY, _exchange([shard(n, 0) for n in _EARLY], True, "ag_early"))
    hooks_fwd = [{"mla_fwd": gather_hook(0, _LATE), "swa_fwd": gather_hook(1, _EARLY + ("w_out",))},
                 {"mla_fwd": gather_hook(1, ("w_ff1", "w_ff2"))}]
    bwd_state = {}

    def scatter_last_layer(dwl):
        return _Exchange([wgrad(n, 1, dwl) for n in _LATE], False)

    def scatter_first_layer(dwl):
        dw1, ds1 = bwd_state["dw1"], bwd_state["ds1"]
        bufs = [wgrad(n, 1, dw1) for n in _EARLY] + [wgrad(n, 0, dwl) for n in _LATE] + [small_pack(1, ds1)]
        return _Exchange(bufs, [False] * (len(_EARLY) + len(_LATE)) + [True])

    def scattered_first_layer(got):
        g_recv.update(zip([(n, 1) for n in _EARLY] + [(n, 0) for n in _LATE], got[:-1]))
        small_recv[1] = got[-1]

    hooks_bwd = [{"mla_bwd": (scatter_first_layer, scattered_first_layer)},
                 {"mla_bwd": (scatter_last_layer, lambda got: g_recv.update(zip([(n, 1) for n in _LATE], got)))}]

    tc, lat = ctx.shape[1], x.shape[1]
    tabs = {"mla": _rope_tables(lat, tc, MLA_ROPE, MLA_NOPE), "swa": _rope_tables(lat, tc, SWA_HEAD_DIM, 0)}
    stream = jnp.concatenate([ctx, x], axis=1)
    bwds = []
    for li in range(DEPTH):
        stream, bwd = _layer(li, stream, mods[li], w[li], sp[li], tabs, tc, li < DEPTH - 1, hooks_fwd[li])
        bwds.append(bwd)
    loss_part, dstream = _loss_and_grad(stream, loss_target, tc)
    dmods = [None] * DEPTH
    dstream, dmods[1], bwd_state["dw1"], bwd_state["ds1"] = bwds[1](dstream, hooks_bwd[1])
    dstream, dmods[0], dw0, ds0 = bwds[0](dstream, hooks_bwd[0])
    grad_x = dstream[:, tc:]

    dm_rows = []
    for li in range(DEPTH):
        dm = jnp.concatenate(dmods[li], axis=-1)
        dm_rows.append(jnp.concatenate([dm[:, 1, 0], jnp.sum(dm[:, 0, 0], axis=0, keepdims=True)], axis=0))
    dm_mine = jnp.concatenate(dm_rows, axis=1)
    dm_mine = jnp.pad(dm_mine, ((0, SUBLANE - bsz - 1), (0, 0)))
    (dm_all,) = _exchange([dm_mine], True, "ag_dmod")
    g_wmod, g_bmod, dact_ctx = [], [], jnp.zeros((D_MODEL,), F32)
    for li in range(DEPTH):
        part = dm_all[:, :, li * N_MOD * D_MODEL:(li + 1) * N_MOD * D_MODEL]
        dm32 = jnp.zeros((MOD_ROWS, N_MOD * D_MODEL), F32).at[:n_ex].set(part[:, :bsz].reshape(n_ex, -1))
        dm32 = dm32.at[n_ex].set(jnp.sum(part[:, bsz], axis=0))
        g_bmod.append(jnp.sum(dm32, axis=0))
        cols = lax.dynamic_slice_in_dim(dm32, me * mod_cols, mod_cols, axis=1)
        g_wmod.append(_mm(act, cols, "tn", F32, "mm_mod_dw_l%d" % li))
        dact_ctx = dact_ctx + _mm(cols, w_mod[li], "nt", F32, "mm_mod_dx_l%d" % li)[n_ex]
    sg = jax.nn.sigmoid(c_ctx)
    g_cctx_part = dact_ctx * (sg * (1.0 + c_ctx * (1.0 - sg)))

    last = _exchange([wgrad(n, 0, dw0) for n in _EARLY] + [small_pack(0, ds0, (g_cctx_part, loss_part.reshape(1)))],
                     [False] * len(_EARLY) + [True], "rs_early")
    g_recv.update(zip([(n, 0) for n in _EARLY], last[:-1]))
    small_recv[0] = last[-1]
    layer_shapes = [raw[n].shape[1:] for n in small_names]
    tot = [_unpack(_sum_sources(small_recv[li], "sum_grads_l%d" % li), layer_shapes + [(D_MODEL,), (1,)][:2 * (li == 0)])
           for li in range(DEPTH)]
    grads = {n: jnp.stack([tot[li][j] for li in range(DEPTH)], axis=0) for j, n in enumerate(small_names)}
    grads["c_ctx"], loss = tot[0][-2], tot[0][-1][0]
    for n in _SHARDED_SMALL:
        width = wts[n].shape[-1]
        grads[n] = lax.dynamic_slice_in_dim(grads[n], me * width, width, axis=grads[n].ndim - 1)
    grads["b_mod"] = jnp.stack(g_bmod, axis=0)

    delta, new_m, new_v = {}, {}, {}
    for n, (shp, _, _) in _BIG.items():
        two_d = (DEPTH * shp[0], shp[1])
        src = jnp.stack([g_recv[n, li] for li in range(DEPTH)], axis=1).reshape((N_DEV,) + two_d)
        res = _adamw(src, wts[n].reshape(two_d), mom1[n].reshape(two_d), mom2[n].reshape(two_d), "adamw_" + n)
        grads[n], delta[n], new_m[n], new_v[n] = [r.reshape(wts[n].shape) for r in res]
    two_d = (DEPTH * D_MODEL, mod_cols)
    res = _adamw(jnp.stack(g_wmod, axis=0).reshape((1,) + two_d), w_mod.reshape(two_d), mom1["w_mod"].reshape(two_d),
                 mom2["w_mod"].reshape(two_d), "adamw_w_mod")
    grads["w_mod"], delta["w_mod"], new_m["w_mod"], new_v["w_mod"] = [r.reshape(w_mod.shape) for r in res]
    rest = [n for n in _WEIGHTS if n not in delta]
    shapes = [wts[n].shape for n in rest]
    res = _adamw(_pack([grads[n] for n in rest], F32)[None], _pack([wts[n] for n in rest], F32),
                 _pack([mom1[n] for n in rest], F32), _pack([mom2[n] for n in rest], F32), "adamw_small")
    for tgt, buf in zip((delta, new_m, new_v), res[1:]):
        tgt.update(zip(rest, _unpack(buf, shapes)))

    return (loss, grad_x, *[grads[n] for n in _WEIGHTS], *[delta[n] for n in _WEIGHTS],
            *[new_m[n] for n in _WEIGHTS], *[new_v[n] for n in _WEIGHTS])
```

```python
import functools
import math

import jax
import jax.numpy as jnp
from jax import lax
from jax.experimental import pallas as pl
from jax.experimental.pallas import tpu as pltpu

F32, BF16 = jnp.float32, jnp.bfloat16

N_DEV = 8
DEPTH = 2
D_MODEL = 1024
D_FF = 4096
N_MOD = 6
GRID_W = 64
WINDOW = 128
ROPE_THETA = 10000.0
EPS = 1e-6
NEG_INF = -1e30
LRU_C = 8.0
LRU_WIDTH = 512
MLA_HEADS, MLA_NOPE, MLA_ROPE, MLA_V = 8, 64, 32, 64
MLA_QK = MLA_NOPE + MLA_ROPE
MLA_Q_RANK, MLA_KV_RANK = 256, 128
SWA_HEADS, SWA_KV_HEADS, SWA_GROUP, SWA_HEAD_DIM = 8, 2, 4, 64
GROUP_WIDTH = 512
IN_SIZES = (256, 128, 32, 512, 512, 512, 128, 128)
IN_WIDTH = sum(IN_SIZES)
ADAM_LR, ADAM_B1, ADAM_B2, ADAM_EPS, ADAM_WD, ADAM_STEP = 0.001, 0.9, 0.999, 1e-08, 0.01, 10

LANE = 128
SUBLANE = 8
TB = 256
QB_SWA = 128
PACK_W = 1024
MM_K_CAP = 4608
MLA_HPS = 2
VMEM_LIMIT = 56 * 1024 * 1024
P_WIDTH = 3072
PC_SQ, PC_LX, PC_LG, PC_CQ, PC_SK, PC_SV, PC_CKV, PC_KR = 0, 1024, 1536, 2048, 2304, 2560, 2816, 2944
MIX_P = 2560


def _pcall(body, **kw):
    return pl.pallas_call(body, **kw)


def _cparams(n_grid):
    return pltpu.CompilerParams(dimension_semantics=("arbitrary",) * n_grid, vmem_limit_bytes=VMEM_LIMIT)


def _dg(a, b, ca, cb):
    return lax.dot_general(a.astype(BF16), b.astype(BF16), (((ca,), (cb,)), ((), ())),
                           preferred_element_type=F32)


@jax.custom_vjp
def _nn(a, b):
    return _dg(a, b, 1, 0)


@jax.custom_vjp
def _nt(a, b):
    return _dg(a, b, 1, 1)


@jax.custom_vjp
def _tn(a, b):
    return _dg(a, b, 0, 0)


_nn.defvjp(lambda a, b: (_nn(a, b), (a, b)), lambda r, ct: (_nt(ct, r[1]), _tn(r[0], ct)))
_nt.defvjp(lambda a, b: (_nt(a, b), (a, b)), lambda r, ct: (_nn(ct, r[1]), _tn(ct, r[0])))
_tn.defvjp(lambda a, b: (_tn(a, b), (a, b)), lambda r, ct: (_nt(r[1], ct), _nn(r[0], ct)))


@functools.partial(jax.custom_vjp, nondiff_argnums=(1, 2))
def _roll(x, shift, axis):
    return pltpu.roll(x, shift % x.shape[axis], axis)


_roll.defvjp(lambda x, shift, axis: (_roll(x, shift, axis), None),
             lambda shift, axis, _, ct: (_roll(ct, -shift, axis),))


@functools.partial(jax.custom_vjp, nondiff_argnums=(1, 2))
def _split(x, n, axis):
    w = x.shape[axis] // n
    return tuple(lax.slice_in_dim(x, i * w, (i + 1) * w, axis=axis) for i in range(n))


_split.defvjp(lambda x, n, axis: (_split(x, n, axis), None),
              lambda n, axis, _, cts: (jnp.concatenate(cts, axis=axis),))


@jax.custom_vjp
def _unstack(x):
    return tuple(x[i] for i in range(x.shape[0]))


_unstack.defvjp(lambda x: (_unstack(x), None), lambda _, cts: (jnp.stack(cts, axis=0),))


def _sig(x):
    return 0.5 * (jnp.tanh(0.5 * x) + 1.0)


def _gelu(x):
    return 0.5 * x * (1.0 + jnp.tanh(math.sqrt(2.0 / math.pi) * (x + 0.044715 * (x * x * x))))


def _rms(x, g, n):
    ms = jnp.sum(x * x, axis=-1, keepdims=True) * (1.0 / n)
    return x * lax.rsqrt(ms + EPS) * g


def _rope(y, cos, sa, sb, quarter):
    return y * cos + _roll(y, -quarter, 1) * sa + _roll(y, quarter, 1) * sb


def _softmax_rows(s, extra=None):
    m = jnp.max(s, axis=-1, keepdims=True)
    if extra is not None:
        m = jnp.maximum(m, extra)
    m = lax.stop_gradient(m)
    e = jnp.exp(s - m)
    den = jnp.sum(e, axis=-1, keepdims=True)
    if extra is not None:
        den = den + jnp.exp(extra - m)
    return e / den


class _A:
    def __init__(self, arr, block, imap, kind="row", first=None, gdtype=F32, gshape=None, gimap=None):
        self.arr, self.block, self.imap, self.kind, self.first = arr, block, imap, kind, first
        self.gdtype, self.gshape, self.gimap = gdtype, gshape, gimap


def _all_zero(*ids):
    return functools.reduce(jnp.logical_and, [i == 0 for i in ids])


def _par(arr):
    nd = arr.ndim
    return _A(arr, arr.shape, lambda *ids: (0,) * nd, "acc", first=_all_zero)


def _op_fwd(name, fn, grid, args, outs):
    n_in = len(args)

    def body(*refs):
        vals = [r[...].astype(F32) for r in refs[:n_in]]
        for r, v in zip(refs[n_in:], fn(*vals)):
            r[...] = v.astype(r.dtype)

    return _pcall(
        body, name=name, grid=grid,
        in_specs=[pl.BlockSpec(a.block, a.imap) for a in args],
        out_specs=[pl.BlockSpec(o[2], o[3]) for o in outs],
        out_shape=[jax.ShapeDtypeStruct(o[0], o[1]) for o in outs],
        compiler_params=_cparams(len(grid)),
    )(*[a.arr for a in args])


def _op_bwd(name, fn, grid, args, outs, ct_arrays, add_to_first=None):
    didx = [i for i, a in enumerate(args) if a.kind not in ("const", "fwd")]
    read = [i for i, a in enumerate(args) if a.kind != "fwd"]
    n_in, n_ct = len(read), len(outs)
    n_add = 0 if add_to_first is None else 1

    def body(*refs):
        ids = [pl.program_id(i) for i in range(len(grid))]
        vals = [jnp.zeros([d for d in a.block if d is not None], F32) for a in args]
        for i, r in zip(read, refs[:n_in]):
            vals[i] = r[...].astype(F32)

        def g(*dv):
            full = list(vals)
            for i, v in zip(didx, dv):
                full[i] = v
            return tuple(fn(*full))

        _, vjp = jax.vjp(g, *[vals[i] for i in didx])
        grads = list(vjp(tuple(r[...].astype(F32) for r in refs[n_in:n_in + n_ct])))
        if n_add:
            grads[0] = grads[0] + refs[n_in + n_ct][...]
        for gr, i, r in zip(grads, didx, refs[n_in + n_ct + n_add:]):
            a = args[i]
            if a.kind == "row":
                r[...] = gr.astype(r.dtype)
            else:
                first = a.first(*ids)

                @pl.when(first)
                def _():
                    r[...] = gr

                @pl.when(jnp.logical_not(first))
                def _():
                    r[...] += gr

    g_specs, g_shapes = [], []
    for i in didx:
        a = args[i]
        if a.kind == "row":
            g_specs.append(pl.BlockSpec(a.block, a.gimap or a.imap))
            g_shapes.append(jax.ShapeDtypeStruct(a.gshape or a.arr.shape, a.gdtype))
        else:
            g_specs.append(pl.BlockSpec(a.block, a.imap))
            g_shapes.append(jax.ShapeDtypeStruct(a.arr.shape, F32))
    return _pcall(
        body, name=name, grid=grid,
        in_specs=[pl.BlockSpec(args[i].block, args[i].imap) for i in read] + [pl.BlockSpec(o[2], o[3]) for o in outs]
        + g_specs[:n_add],
        out_specs=g_specs, out_shape=g_shapes,
        compiler_params=_cparams(len(grid)),
    )(*[args[i].arr for i in read], *ct_arrays, *([add_to_first] if n_add else []))


def _rowop(name, fn, grid, args, outs):
    res = _op_fwd(name, fn, grid, args, outs)
    return res, lambda *cts, add_to_first=None: _op_bwd(name + "_bwd", fn, grid, args, outs, cts, add_to_first)


def _pick(n, cap):
    best = None
    for t in range(LANE, cap + 1, LANE):
        if n % t == 0:
            best = t
    return best or n


def _mm(a, b, mode, out_dtype, name, epi=None, aux=None):
    if mode == "nn":
        (m, k), n = a.shape, b.shape[1]
    elif mode == "nt":
        (m, k), n = a.shape, b.shape[0]
    else:
        (k, m), n = a.shape, b.shape[1]
    tm = 512 if m % 512 == 0 else m
    tn, tk = _pick(n, 1024), _pick(k, MM_K_CAP)
    nk = k // tk
    if mode == "tn":
        a_spec = pl.BlockSpec((tk, tm), lambda j, i, kk: (kk, i))
    else:
        a_spec = pl.BlockSpec((tm, tk), lambda j, i, kk: (i, kk))
    if mode == "nt":
        b_spec = pl.BlockSpec((tn, tk), lambda j, i, kk: (j, kk))
    else:
        b_spec = pl.BlockSpec((tk, tn), lambda j, i, kk: (kk, j))
    dims = {"nn": (1, 0), "nt": (1, 1), "tn": (0, 0)}[mode]
    o_spec = pl.BlockSpec((tm, tn), lambda j, i, kk: (i, j))
    n_aux = 0 if aux is None else 1
    n_out = 2 if epi == "sqrelu" else 1

    def body(*refs):
        a_ref, b_ref = refs[0], refs[1]
        o_refs = refs[2 + n_aux:2 + n_aux + n_out]
        acc = refs[-1]
        kk = pl.program_id(2)
        part = _dg(a_ref[...], b_ref[...], *dims)

        if nk > 1:
            @pl.when(kk == 0)
            def _():
                acc[...] = part

            @pl.when((kk > 0) & (kk < nk - 1))
            def _():
                acc[...] += part

        @pl.when(kk == nk - 1)
        def _():
            r = part if nk == 1 else acc[...] + part
            if epi == "sqrelu":
                o_refs[0][...] = r.astype(o_refs[0].dtype)
                rl = jnp.maximum(r, 0.0)
                o_refs[1][...] = (rl * rl).astype(o_refs[1].dtype)
            elif epi == "dsqrelu":
                pre = refs[2][...].astype(F32)
                o_refs[0][...] = (r * (2.0 * jnp.maximum(pre, 0.0))).astype(o_refs[0].dtype)
            else:
                o_refs[0][...] = r.astype(o_refs[0].dtype)

    res = _pcall(
        body, name=name, grid=(n // tn, m // tm, nk),
        in_specs=[a_spec, b_spec] + [o_spec] * n_aux,
        out_specs=[o_spec] * n_out,
        out_shape=[jax.ShapeDtypeStruct((m, n), out_dtype)] * n_out,
        scratch_shapes=[pltpu.VMEM((tm, tn), F32)],
        compiler_params=_cparams(3),
    )(a, b, *([aux] if aux is not None else []))
    return res if n_out == 2 else res[0]


ROW_CHUNK = 16


def _softmax_chunks(s_scr, n_keys, scale, emit):
    for r0 in range(0, s_scr.shape[0], ROW_CHUNK):
        rows = slice(r0, r0 + ROW_CHUNK)
        s = s_scr[rows, :n_keys]
        e = jnp.exp((s - jnp.max(s, axis=-1, keepdims=True)) * scale)
        emit(rows, e, 1.0 / jnp.sum(e, axis=-1, keepdims=True))


def _attn_fwd_block(q, k, v, scale, s_scr, e_scr, l_scr):
    n = k.shape[0]
    s_scr[:, :n] = _dg(q, k, 1, 1)

    def emit(rows, e, inv_l):
        e_scr[rows, :n] = e.astype(BF16)
        l_scr[rows, :] = jnp.broadcast_to(inv_l, (ROW_CHUNK, LANE))

    _softmax_chunks(s_scr, n, scale, emit)
    return _dg(e_scr[:, :n], v, 1, 0) * l_scr[...]


def _attn_bwd_block(q, k, v, o, do, scale, s_scr, dp_scr, p_scr, ds_scr):
    n = k.shape[0]
    s_scr[:, :n] = _dg(q, k, 1, 1)
    dp_scr[:, :n] = _dg(do, v, 1, 1)

    def emit(rows, e, inv_l):
        p = e * inv_l
        delta = jnp.sum(do[rows, :] * o[rows, :], axis=-1, keepdims=True)
        p_scr[rows, :n] = p.astype(BF16)
        ds_scr[rows, :n] = (p * (dp_scr[rows, :n] - delta) * scale).astype(BF16)

    _softmax_chunks(s_scr, n, scale, emit)
    ds = ds_scr[:, :n]
    return _dg(ds, k, 1, 0), _dg(ds, q, 0, 0), _dg(p_scr[:, :n], do, 0, 0)


def _call_with_exchange(body, xchg, *, name, grid, in_specs, out_specs, out_shape, operands, scratch_shapes=()):
    if xchg is None:
        res = _pcall(body, name=name, grid=grid, in_specs=in_specs, out_specs=out_specs, out_shape=out_shape,
                     scratch_shapes=list(scratch_shapes), compiler_params=_cparams(len(grid)))(*operands)
        return list(res), []
    n_in, n_out, n_sc, n = len(in_specs), len(out_specs), len(scratch_shapes), xchg.n

    def wrapped(*refs):
        ins, x_refs = refs[:n_in], refs[n_in:n_in + n]
        outs, xo_refs = refs[n_in + n:n_in + n + n_out], refs[n_in + n + n_out:n_in + 2 * n + n_out]
        scratch, sems = refs[n_in + 2 * n + n_out:n_in + 2 * n + n_out + n_sc], refs[n_in + 2 * n + n_out + n_sc:]
        ids = [pl.program_id(i) for i in range(len(grid))]

        @pl.when(functools.reduce(jnp.logical_and, [i == 0 for i in ids]))
        def _():
            xchg.start(x_refs, xo_refs, sems)

        body(*ins, *outs, *scratch)

        @pl.when(functools.reduce(jnp.logical_and, [i == g - 1 for i, g in zip(ids, grid)]))
        def _():
            xchg.wait(x_refs, xo_refs, sems)

    res = _pcall(wrapped, name=name, grid=grid, in_specs=list(in_specs) + xchg.specs,
                 out_specs=list(out_specs) + xchg.specs, out_shape=list(out_shape) + xchg.out_shape,
                 scratch_shapes=list(scratch_shapes) + xchg.scratch, compiler_params=_cparams(len(grid)),
                 )(*operands, *xchg.bufs)
    return list(res[:n_out]), list(res[n_out:])


def _mla_attn(q, k, v, tc, ctx_q, name, xchg=None):
    bsz, t_all, _ = q.shape
    n_t = t_all // TB
    grid = (bsz, MLA_HEADS // MLA_HPS, n_t)
    q_spec = pl.BlockSpec((None, TB, MLA_HPS * LANE), lambda b, h, t: (b, t, h))
    kv_spec = pl.BlockSpec((None, t_all, MLA_HPS * LANE), lambda b, h, t: (b, 0, h))
    heads = [slice(i * LANE, (i + 1) * LANE) for i in range(MLA_HPS)]
    scale = MLA_QK ** -0.5
    f32_scr, bf16_scr = pltpu.VMEM((TB, t_all), F32), pltpu.VMEM((TB, t_all), BF16)

    def fwd_body(q_ref, k_ref, v_ref, o_ref, *scr):
        t = pl.program_id(2)

        def run(keys):
            for i, hs in enumerate(heads):
                o_ref[:, hs] = _attn_fwd_block(q_ref[:, hs], k_ref[keys, hs], v_ref[keys, hs], scale,
                                               *scr[3 * i:3 * i + 3])

        @pl.when(t == 0)
        def _():
            if ctx_q:
                run(slice(0, tc))
            else:
                o_ref[...] = jnp.zeros_like(o_ref)

        @pl.when(t > 0)
        def _():
            run(slice(0, t_all))

    (o,), gathered = _call_with_exchange(
        fwd_body, xchg, name=name, grid=grid, in_specs=[q_spec, kv_spec, kv_spec], out_specs=[q_spec],
        out_shape=[jax.ShapeDtypeStruct(q.shape, F32)], operands=(q, k, v),
        scratch_shapes=[f32_scr, bf16_scr, pltpu.VMEM((TB, LANE), F32)] * MLA_HPS)

    def bwd(do, xchg=None):
        def bwd_body(q_ref, k_ref, v_ref, o_ref, do_ref, dq_ref, dk_ref, dv_ref, *scr):
            t = pl.program_id(2)

            def run(keys, first):
                for i, hs in enumerate(heads):
                    dq, dk, dv = _attn_bwd_block(q_ref[:, hs], k_ref[keys, hs], v_ref[keys, hs], o_ref[:, hs],
                                                 do_ref[:, hs], scale, *scr[4 * i:4 * i + 4])
                    dq_ref[:, hs] = dq
                    if first:
                        dk_ref[keys, hs] = dk
                        dv_ref[keys, hs] = dv
                    else:
                        dk_ref[keys, hs] += dk
                        dv_ref[keys, hs] += dv

            @pl.when(t == 0)
            def _():
                dk_ref[...] = jnp.zeros_like(dk_ref)
                dv_ref[...] = jnp.zeros_like(dv_ref)
                if ctx_q:
                    run(slice(0, tc), True)
                else:
                    dq_ref[...] = jnp.zeros_like(dq_ref)

            @pl.when(t > 0)
            def _():
                run(slice(0, t_all), False)

        return _call_with_exchange(
            bwd_body, xchg, name=name + "_bwd", grid=grid, in_specs=[q_spec, kv_spec, kv_spec, q_spec, q_spec],
            out_specs=[q_spec, kv_spec, kv_spec], out_shape=[jax.ShapeDtypeStruct(q.shape, F32)] * 3,
            operands=(q, k, v, o, do), scratch_shapes=[f32_scr, f32_scr, bf16_scr, bf16_scr] * MLA_HPS)

    return o, gathered, bwd


def _swa_block(q, keys, vals, sink, mask):
    qs = jnp.concatenate(list(_split(q, SWA_GROUP, 1)), axis=0)
    sk = jnp.sum(sink, axis=-1, keepdims=True) * (1.0 / LANE)
    s = _nt(qs, keys) * (SWA_HEAD_DIM ** -0.5)
    if mask is not None:
        s = jnp.where(mask, s, NEG_INF)
    o = _nn(_softmax_rows(s, sk), vals)
    return jnp.concatenate(list(_split(o, SWA_GROUP, 0)), axis=1)


def _swa_ctx_block(q, kc, vc, sink):
    return _swa_block(q, kc, vc, sink, None)


def _swa_win_block(q, kc, kw, vc, vw, sink, mask):
    return _swa_block(q, jnp.concatenate([kc, kw], axis=0), jnp.concatenate([vc, vw], axis=0), sink, mask)


def _swa_attn(q, k, p_all, sink_b, tc, ctx_q, name, xchg=None):
    bsz, t_all, _ = q.shape
    n_q = t_all // QB_SWA
    n_cq = tc // QB_SWA
    lat = t_all - tc
    span = QB_SWA + 2 * WINDOW
    gw = SWA_GROUP * LANE
    grid = (bsz, SWA_KV_HEADS, n_q)
    q_spec = pl.BlockSpec((None, QB_SWA, gw), lambda b, g, i: (b, i, g))
    k_spec = pl.BlockSpec((None, t_all, LANE), lambda b, g, i: (b, 0, g))
    v_spec = pl.BlockSpec((None, t_all, LANE), lambda b, g, i: (b, 0, PC_SV // LANE + g))
    s_spec = pl.BlockSpec((None, SWA_GROUP * QB_SWA, LANE), lambda b, g, i: (g, 0, 0))

    def window(i):
        q0 = (i - n_cq) * QB_SWA
        w0 = jnp.clip(q0 - WINDOW, 0, lat - span)
        w0 = pl.multiple_of(w0, QB_SWA)
        shape = (SWA_GROUP * QB_SWA, tc + span)
        qi = q0 + lax.broadcasted_iota(jnp.int32, shape, 0) % QB_SWA
        col = lax.broadcasted_iota(jnp.int32, shape, 1)
        kj = w0 + col - tc
        mask = (col < tc) | ((kj >= qi - WINDOW) & (kj <= qi + WINDOW))
        return w0, mask

    def fwd_body(q_ref, k_ref, v_ref, s_ref, o_ref):
        i = pl.program_id(2)

        @pl.when(i < n_cq)
        def _():
            if ctx_q:
                o_ref[...] = _swa_ctx_block(q_ref[...].astype(F32), k_ref[0:tc, :], v_ref[0:tc, :], s_ref[...])
            else:
                o_ref[...] = jnp.zeros_like(o_ref)

        @pl.when(i >= n_cq)
        def _():
            w0, mask = window(i)
            o_ref[...] = _swa_win_block(q_ref[...].astype(F32), k_ref[0:tc, :], k_ref[pl.ds(tc + w0, span), :],
                                        v_ref[0:tc, :], v_ref[pl.ds(tc + w0, span), :], s_ref[...], mask)

    (o,), gathered = _call_with_exchange(
        fwd_body, xchg, name=name, grid=grid, in_specs=[q_spec, k_spec, v_spec, s_spec], out_specs=[q_spec],
        out_shape=[jax.ShapeDtypeStruct(q.shape, F32)], operands=(q, k, p_all, sink_b))

    def bwd(do, xchg=None):
        def bwd_body(q_ref, k_ref, v_ref, s_ref, do_ref, dq_ref, dk_ref, dv_ref, ds_ref):
            i = pl.program_id(2)

            @pl.when(i == 0)
            def _():
                dk_ref[...] = jnp.zeros_like(dk_ref)
                dv_ref[...] = jnp.zeros_like(dv_ref)
                ds_ref[...] = jnp.zeros_like(ds_ref)

            @pl.when(i < n_cq)
            def _():
                if ctx_q:
                    _, vjp = jax.vjp(_swa_ctx_block, q_ref[...].astype(F32), k_ref[0:tc, :].astype(F32),
                                     v_ref[0:tc, :], s_ref[...])
                    dq, dk, dv, ds = vjp(do_ref[...])
                    dq_ref[...] = dq
                    dk_ref[0:tc, :] += dk
                    dv_ref[0:tc, :] += dv
                    ds_ref[...] += ds
                else:
                    dq_ref[...] = jnp.zeros_like(dq_ref)

            @pl.when(i >= n_cq)
            def _():
                w0, mask = window(i)
                win = pl.ds(tc + w0, span)
                _, vjp = jax.vjp(functools.partial(_swa_win_block, mask=mask), q_ref[...].astype(F32),
                                 k_ref[0:tc, :].astype(F32), k_ref[win, :].astype(F32),
                                 v_ref[0:tc, :], v_ref[win, :], s_ref[...])
                dq, dkc, dkw, dvc, dvw, ds = vjp(do_ref[...])
                dq_ref[...] = dq
                dk_ref[0:tc, :] += dkc
                dk_ref[win, :] += dkw
                dv_ref[0:tc, :] += dvc
                dv_ref[win, :] += dvw
                ds_ref[...] += ds

        kv_out = pl.BlockSpec((None, t_all, LANE), lambda b, g, i: (b, 0, g))
        ds_spec = pl.BlockSpec((None, None, SWA_GROUP * QB_SWA, LANE), lambda b, g, i: (b, g, 0, 0))
        kv_shape = jax.ShapeDtypeStruct((bsz, t_all, SWA_KV_HEADS * LANE), F32)
        return _call_with_exchange(
            bwd_body, xchg, name=name + "_bwd", grid=grid, in_specs=[q_spec, k_spec, v_spec, s_spec, q_spec],
            out_specs=[q_spec, kv_out, kv_out, ds_spec],
            out_shape=[jax.ShapeDtypeStruct(q.shape, F32), kv_shape, kv_shape,
                       jax.ShapeDtypeStruct((bsz,) + sink_b.shape, F32)],
            operands=(q, k, p_all, sink_b, do))

    return o, gathered, bwd


def _scan_rows(a, u, reverse, a_s, u_s, c_s):
    t_all, c = a.shape
    row8 = lax.broadcasted_iota(jnp.int32, a.shape, 0) % SUBLANE
    for d in (1, 2, 4):
        sh = d if not reverse else t_all - d
        ar, ur = pltpu.roll(a, sh, 0), pltpu.roll(u, sh, 0)
        m = (row8 >= d) if not reverse else (row8 < SUBLANE - d)
        u = jnp.where(m, a * ur + u, u)
        a = jnp.where(m, a * ar, a)
    a_s[...] = a
    u_s[...] = u
    n_tiles = t_all // SUBLANE

    def step(j, carry):
        tile = j if not reverse else n_tiles - 1 - j
        base = pl.multiple_of(tile * SUBLANE, SUBLANE)
        c_s[pl.ds(base, SUBLANE), :] = jnp.broadcast_to(carry, (SUBLANE, c))
        last = base + (0 if reverse else SUBLANE - 1)
        return a_s[pl.ds(last, 1), :] * carry + u_s[pl.ds(last, 1), :]

    lax.fori_loop(0, n_tiles, step, jnp.zeros((1, c), F32))
    return a_s[...] * c_s[...] + u_s[...]


def _shift_rows(x, reverse_src):
    t_all = x.shape[0]
    row = lax.broadcasted_iota(jnp.int32, x.shape, 0)
    if reverse_src:
        return jnp.where(row == t_all - 1, 0.0, pltpu.roll(x, t_all - 1, 0))
    return jnp.where(row == 0, 0.0, pltpu.roll(x, 1, 0))


def _lru_scan(a0, u0, a1, u1, name):
    bsz, t_all, w = a0.shape
    grid = (bsz, w // LANE)
    spec = pl.BlockSpec((None, t_all, LANE), lambda b, c: (b, 0, c))
    scratch = [pltpu.VMEM((t_all, LANE), F32)] * 3
    shape = jax.ShapeDtypeStruct(a0.shape, F32)

    def fwd_body(a0_ref, u0_ref, a1_ref, u1_ref, h0_ref, h1_ref, a_s, u_s, c_s):
        h0_ref[...] = _scan_rows(a0_ref[...], u0_ref[...], False, a_s, u_s, c_s)
        h1_ref[...] = _scan_rows(a1_ref[...], u1_ref[...], True, a_s, u_s, c_s)

    h0, h1 = _pcall(fwd_body, name=name, grid=grid, in_specs=[spec] * 4, out_specs=[spec] * 2,
                    out_shape=[shape] * 2, scratch_shapes=scratch, compiler_params=_cparams(2))(a0, u0, a1, u1)

    def bwd(dh0, dh1):
        def bwd_body(a0_ref, h0_ref, g0_ref, a1_ref, h1_ref, g1_ref, da0_ref, du0_ref, da1_ref, du1_ref,
                     a_s, u_s, c_s):
            g0 = _scan_rows(_shift_rows(a0_ref[...], True), g0_ref[...], True, a_s, u_s, c_s)
            du0_ref[...] = g0
            da0_ref[...] = g0 * _shift_rows(h0_ref[...], False)
            g1 = _scan_rows(_shift_rows(a1_ref[...], False), g1_ref[...], False, a_s, u_s, c_s)
            du1_ref[...] = g1
            da1_ref[...] = g1 * _shift_rows(h1_ref[...], True)

        return _pcall(bwd_body, name=name + "_bwd", grid=grid, in_specs=[spec] * 6, out_specs=[spec] * 4,
                      out_shape=[shape] * 4, scratch_shapes=scratch,
                      compiler_params=_cparams(2))(a0, h0, dh0, a1, h1, dh1)

    return h0, h1, bwd


def _f_mod(x, g, shift, scale):
    return (_rms(x, g, D_MODEL) * (1.0 + scale) + shift,)


def _f_mla_q(cq, ga, w, gh, cos, sa, sb):
    n = _rms(cq, ga, MLA_Q_RANK)
    outs = []
    for wh in _split(w, MLA_HEADS, 1):
        outs.append(_rope(_rms(_nn(n, wh), gh, MLA_QK), cos, sa, sb, MLA_ROPE // 4))
    return (jnp.concatenate(outs, axis=1),)


def _f_mla_kv(ckv, krp, ga, wk, wv, gh, cos, sa, sb):
    n = _rms(ckv, ga, MLA_KV_RANK)
    outs = []
    for wh in _split(wk, MLA_HEADS, 1):
        outs.append(_rope(_rms(_nn(n, wh) + krp, gh, MLA_QK), cos, sa, sb, MLA_ROPE // 4))
    return jnp.concatenate(outs, axis=1), _nn(n, wv)


def _f_conv(x, w0, w1, w2, w3, bias, tc):
    t_all = x.shape[0]
    row = lax.broadcasted_iota(jnp.int32, x.shape, 0)
    lo = jnp.where(row < tc, 0, tc)
    hi = jnp.where(row < tc, tc, t_all)
    y = bias + jnp.zeros_like(x)
    for kk, wk in enumerate((w0, w1, w2, w3)):
        src = row + (kk - 2)
        xs = x if kk == 2 else _roll(x, 2 - kk, 0)
        y = y + wk * jnp.where((src >= lo) & (src < hi), xs, 0.0)
    return (y,)


def _f_gates(xc, w16, b00, b01, b10, b11, sp0, sp1):
    ws = _unstack(w16)
    n_cb = LRU_WIDTH // LANE
    xcs = _split(xc, n_cb, 1)
    bias = [_split(b, n_cb, 1) for b in (b00, b01, b10, b11)]
    sps = [_split(s, n_cb, 1) for s in (sp0, sp1)]
    res = [[], [], [], []]
    for c in range(n_cb):
        for z in range(2):
            r = _sig(_nn(xcs[c], ws[c * 4 + 2 * z]) + bias[2 * z][c])
            i = _sig(_nn(xcs[c], ws[c * 4 + 2 * z + 1]) + bias[2 * z + 1][c])
            la = -LRU_C * r * sps[z][c]
            res[2 * z].append(jnp.exp(la))
            res[2 * z + 1].append(jnp.sqrt(-jnp.tanh(la) * (jnp.exp(2.0 * la) + 1.0)) * (i * xcs[c]))
    return tuple(jnp.concatenate(r, axis=1) for r in res)


def _f_lru_out(h0, h1, lg):
    return ((h0 + h1) * _gelu(lg),)


def _f_swa_qk(sq, sk, gq, gk, cos, sa, sb):
    qs = [_rope(_rms(x, gq, SWA_HEAD_DIM), cos, sa, sb, SWA_HEAD_DIM // 4) for x in _split(sq, SWA_HEADS, 1)]
    ks = [_rope(_rms(x, gk, SWA_HEAD_DIM), cos, sa, sb, SWA_HEAD_DIM // 4) for x in _split(sk, SWA_KV_HEADS, 1)]
    return jnp.concatenate(qs, axis=1), jnp.concatenate(ks, axis=1)


def _f_merge(oa, ob, oc, ga, gb, gc):
    return (jnp.concatenate([_rms(oa, ga, GROUP_WIDTH), _rms(ob, gb, GROUP_WIDTH), _rms(oc, gc, GROUP_WIDTH)],
                            axis=1),)


def _f_resid_mod(x, y, gate, g, shift, scale):
    x1 = x + gate * y
    return x1, _rms(x1, g, D_MODEL) * (1.0 + scale) + shift


def _f_resid(x, y, gate):
    return (x + gate * y,)


def _hosted(hooks, key, arg=None):
    make, done = hooks.get(key, (None, None))
    xchg = make(arg) if make is not None else None
    return xchg, (done if xchg is not None else lambda outs: None)


def _layer(li, x, mods, w, s, tabs, tc, ctx_q, hooks):
    bsz, t_all, _ = x.shape
    n_t = t_all // TB
    grid = (bsz, n_t)
    rows = lambda b, t: (b, t, 0)

    def row(arr, width=None, idx=0, gdtype=F32, gshape=None):
        width = width or arr.shape[-1]
        return _A(arr, (None, TB, width), lambda b, t: (b, t, idx), "row", gdtype=gdtype, gshape=gshape,
                  gimap=rows if gshape is not None else None)

    def out(width, dtype, imap=rows):
        return ((bsz, t_all, width), dtype, (None, TB, width), imap)

    def modarg(arr):
        return _A(arr, (None, None, 1, D_MODEL), lambda b, t: (b, jnp.minimum(t, 1), 0, 0), "acc",
                  first=lambda b, t: t <= 1)

    def tab(arr):
        return _A(arr, (TB, LANE), lambda b, t: (t, 0), "const")

    def pcol(p_all, col, width):
        return row(p_all, width, col // width, gdtype=BF16, gshape=(bsz, t_all, width))

    nm = lambda base: "%s_l%d" % (base, li)
    sh1, sc1, g1, sh2, sc2, g2 = mods
    m_all = bsz * t_all

    (h,), b_mod1 = _rowop(nm("mod1"), _f_mod, grid, [row(x), _par(s["norm1_g"]), modarg(sh1), modarg(sc1)],
                          [out(D_MODEL, BF16)])
    p_all = _mm(h.reshape(m_all, D_MODEL), w["win"], "nn", F32, nm("mm_in")).reshape(bsz, t_all, P_WIDTH)

    tq, tk_ = tabs["mla"], tabs["mla"]
    (q_a,), b_mq = _rowop(nm("mla_q"), _f_mla_q, grid,
                          [pcol(p_all, PC_CQ, 256), _par(s["q_a_g"]), _par(w["wuq"]), _par(s["mla_q_g"])]
                          + [tab(a) for a in tq], [out(MLA_HEADS * LANE, BF16)])
    (k_a, v_a), b_mkv = _rowop(nm("mla_kv"), _f_mla_kv, grid,
                               [pcol(p_all, PC_CKV, 128), pcol(p_all, PC_KR, 128), _par(s["kv_a_g"]), _par(w["wk"]),
                                _par(w["wv"]), _par(s["mla_k_g"])] + [tab(a) for a in tk_],
                               [out(MLA_HEADS * LANE, BF16), out(MLA_HEADS * LANE, BF16)])
    xchg, done = _hosted(hooks, "mla_fwd")
    o_a, got, b_attn_a = _mla_attn(q_a, k_a, v_a, tc, ctx_q, nm("mla_attn"), xchg)
    done(got)

    n_cb = LRU_WIDTH // LANE
    conv_grid = (n_cb, bsz)
    cpar = lambda arr: _A(arr, (1, LANE), lambda c, b: (0, c), "acc", first=lambda c, b: b == 0)
    conv_args = [_A(p_all, (None, t_all, LANE), lambda c, b: (b, 0, PC_LX // LANE + c), "row", gdtype=BF16,
                    gshape=(bsz, t_all, LRU_WIDTH), gimap=lambda c, b: (b, 0, c))]
    conv_args += [cpar(a) for a in s["conv_w"]] + [cpar(s["conv_b"])]
    conv_out = [((bsz, t_all, LRU_WIDTH), F32, (None, t_all, LANE), lambda c, b: (b, 0, c))]
    (xc,), b_conv = _rowop(nm("lru_conv"), functools.partial(_f_conv, tc=tc), conv_grid, conv_args, conv_out)
    rot = lambda b, t: (b, (t + n_t - 1) % n_t, 0)
    (a0, u0, a1, u1), b_gates = _rowop(
        nm("lru_gates"), _f_gates, grid,
        [row(xc), _par(s["wbd"])] + [_par(a) for a in s["gate_b"]] + [_par(a) for a in s["sp"]],
        [out(LRU_WIDTH, F32), out(LRU_WIDTH, F32), out(LRU_WIDTH, F32, rot), out(LRU_WIDTH, F32, rot)])
    h0, h1, b_scan = _lru_scan(a0, u0, a1, u1, nm("lru_scan"))
    h1_arg = _A(h1, (None, TB, LRU_WIDTH), rot, "row")
    (o_b,), b_lout = _rowop(nm("lru_out"), _f_lru_out, grid, [row(h0), h1_arg, pcol(p_all, PC_LG, 512)],
                            [out(LRU_WIDTH, F32)])

    ts = tabs["swa"]
    (q_c, k_c), b_sqk = _rowop(nm("swa_qk"), _f_swa_qk, grid,
                               [pcol(p_all, PC_SQ, 1024), pcol(p_all, PC_SK, 256), _par(s["swa_q_g"]),
                                _par(s["swa_k_g"])] + [tab(a) for a in ts],
                               [out(SWA_HEADS * LANE, BF16), out(SWA_KV_HEADS * LANE, BF16)])
    xchg, done = _hosted(hooks, "swa_fwd")
    o_c, got, b_attn_c = _swa_attn(q_c, k_c, p_all, s["sink_b"], tc, ctx_q, nm("swa_attn"), xchg)
    done(got)

    (y_in,), b_merge = _rowop(nm("merge"), _f_merge, grid,
                              [row(o_a), row(o_b), row(o_c), _par(s["g_a"]), _par(s["g_b"]), _par(s["g_c"])],
                              [out(MIX_P, BF16)])
    y = _mm(y_in.reshape(m_all, MIX_P), w["wout"], "nn", F32, nm("mm_out")).reshape(bsz, t_all, D_MODEL)
    (x1, hm), b_rm = _rowop(nm("resid_mod"), _f_resid_mod, grid,
                            [row(x), row(y, gdtype=BF16), modarg(g1), _par(s["norm2_g"]), modarg(sh2), modarg(sc2)],
                            [out(D_MODEL, F32), out(D_MODEL, BF16)])
    pre, act = _mm(hm.reshape(m_all, D_MODEL), w["ff1"], "nn", BF16, nm("mm_ff1"), epi="sqrelu")
    y2 = _mm(act, w["ff2"], "nn", F32, nm("mm_ff2")).reshape(bsz, t_all, D_MODEL)
    (x2,), b_res = _rowop(nm("resid"), _f_resid, grid,
                          [_A(x1, (None, TB, D_MODEL), rows, "fwd"), row(y2, gdtype=BF16), modarg(g2)],
                          [out(D_MODEL, F32)])

    def bwd(dx2, hooks):
        dw, ds = {}, {}
        dy2, dg2 = b_res(dx2)
        dy2 = dy2.reshape(m_all, D_MODEL)
        dpre = _mm(dy2, w["ff2"], "nt", BF16, nm("mm_ff2_dx"), epi="dsqrelu", aux=pre)
        dw["ff2"] = _mm(act, dy2, "tn", BF16, nm("mm_ff2_dw"))
        dhm = _mm(dpre, w["ff1"], "nt", F32, nm("mm_ff1_dx")).reshape(bsz, t_all, D_MODEL)
        dw["ff1"] = _mm(hm.reshape(m_all, D_MODEL), dpre, "tn", BF16, nm("mm_ff1_dw"))
        dxa, dy, dg1, ds["norm2_g"], dsh2, dsc2 = b_rm(dx2, dhm)
        dy = dy.reshape(m_all, D_MODEL)
        dy_in = _mm(dy, w["wout"], "nt", F32, nm("mm_out_dx")).reshape(bsz, t_all, MIX_P)
        dw["wout"] = _mm(y_in.reshape(m_all, MIX_P), dy, "tn", BF16, nm("mm_out_dw"))
        do_a, do_b, do_c, ds["g_a"], ds["g_b"], ds["g_c"] = b_merge(dy_in)

        (dq_c, dk_c, dsv, dsink), _ = b_attn_c(do_c)
        ds["sink_b"] = jnp.sum(dsink, axis=0)
        dsq, dsk, ds["swa_q_g"], ds["swa_k_g"] = b_sqk(dq_c, dk_c)

        dh0, dh1, dlg = b_lout(do_b)
        da0, du0, da1, du1 = b_scan(dh0, dh1)
        gates_g = b_gates(da0, du0, da1, du1)
        dxc, ds["wbd"] = gates_g[0], gates_g[1]
        ds["gate_b"], ds["sp"] = list(gates_g[2:6]), list(gates_g[6:8])
        conv_g = b_conv(dxc)
        dlx, ds["conv_w"], ds["conv_b"] = conv_g[0], list(conv_g[1:5]), conv_g[5]

        xchg, done = _hosted(hooks, "mla_bwd", dw)
        (dq_a, dk_a, dv_a), got = b_attn_a(do_a, xchg)
        done(got)
        dcq, ds["q_a_g"], dw["wuq"], ds["mla_q_g"] = b_mq(dq_a)
        dckv, dkr, ds["kv_a_g"], dw["wk"], dw["wv"], ds["mla_k_g"] = b_mkv(dk_a, dv_a)

        dp = jnp.concatenate([dsq, dlx, dlg, dcq, dsk, dsv.astype(BF16), dckv, dkr], axis=-1)
        dp = dp.reshape(m_all, P_WIDTH)
        dh = _mm(dp, w["win"], "nt", F32, nm("mm_in_dx")).reshape(bsz, t_all, D_MODEL)
        dw["win"] = _mm(h.reshape(m_all, D_MODEL), dp, "tn", BF16, nm("mm_in_dw"))
        dx, ds["norm1_g"], dsh1, dsc1 = b_mod1(dh, add_to_first=dxa)
        return dx, [dsh1, dsc1, dg1, dsh2, dsc2, dg2], dw, ds

    return x2, bwd


def _loss_and_grad(x2, target, tc):
    bsz, t_all, d = x2.shape
    n_t = t_all // TB
    n_c = tc // TB

    def body(x_ref, t_ref, l_ref, dx_ref):
        b, t = pl.program_id(0), pl.program_id(1)

        @pl.when((b == 0) & (t == 0))
        def _():
            l_ref[...] = jnp.zeros_like(l_ref)

        @pl.when(t < n_c)
        def _():
            dx_ref[...] = jnp.zeros_like(dx_ref)

        @pl.when(t >= n_c)
        def _():
            e = x_ref[...] - t_ref[...]
            dx_ref[...] = e * (1.0 / d)
            l_ref[...] += jnp.sum(e * e) * (0.5 / d)

    loss, dx = _pcall(
        body, name="loss", grid=(bsz, n_t),
        in_specs=[pl.BlockSpec((None, TB, d), lambda b, t: (b, t, 0)),
                  pl.BlockSpec((None, TB, d), lambda b, t: (b, jnp.maximum(t - n_c, 0), 0))],
        out_specs=[pl.BlockSpec((SUBLANE, LANE), lambda b, t: (0, 0)),
                   pl.BlockSpec((None, TB, d), lambda b, t: (b, t, 0))],
        out_shape=[jax.ShapeDtypeStruct((SUBLANE, LANE), F32), jax.ShapeDtypeStruct(x2.shape, F32)],
        compiler_params=_cparams(2))(x2, target)
    return loss[0, 0], dx


def _rope_tables(lat, tc, dim, lane0):
    quarter = dim // 4
    pos = jnp.arange(lat)
    grid_pos = jnp.stack([pos // GRID_W, pos % GRID_W], axis=-1).astype(F32)
    lane = jnp.arange(LANE)
    p = jnp.clip(lane - lane0, 0, dim - 1)
    active = (lane >= lane0) & (lane < lane0 + dim)
    axis, half, qi = p // (dim // 2), (p % (dim // 2)) // quarter, p % quarter
    inv = ROPE_THETA ** (-qi.astype(F32) / quarter)
    ang = jnp.where(axis[None, :] == 0, grid_pos[:, 0:1], grid_pos[:, 1:2]) * inv[None, :]
    cos = jnp.where(active, jnp.cos(ang), 1.0)
    sin = jnp.where(active, jnp.sin(ang), 0.0)
    sa = jnp.where(half == 0, -sin, 0.0)
    sb = jnp.where(half == 1, sin, 0.0)
    ctx1, ctx0 = jnp.ones((tc, LANE), F32), jnp.zeros((tc, LANE), F32)
    return (jnp.concatenate([ctx1, cos], 0), jnp.concatenate([ctx0, sa], 0), jnp.concatenate([ctx0, sb], 0))


_BIG = {"w_in": ((D_MODEL, IN_WIDTH // N_DEV), 1, ("win",)),
        "w_uq": ((MLA_Q_RANK, MLA_HEADS * MLA_QK // N_DEV), 1, ("wuq",)),
        "w_ukv": ((MLA_KV_RANK, MLA_HEADS * (MLA_NOPE + MLA_V) // N_DEV), 1, ("wk", "wv")),
        "w_out": ((3 * GROUP_WIDTH // N_DEV, D_MODEL), 0, ("wout",)),
        "w_ff1": ((D_MODEL, D_FF // N_DEV), 1, ("ff1",)),
        "w_ff2": ((D_FF // N_DEV, D_MODEL), 0, ("ff2",))}
_EARLY = ("w_in", "w_uq", "w_ukv")
_LATE = ("w_out", "w_ff1", "w_ff2")


def _pad_heads(wm, n_heads, dim, axis=-1):
    axis = axis % wm.ndim
    shp = wm.shape[:axis] + (n_heads, dim) + wm.shape[axis + 1:]
    pad = [(0, 0)] * len(shp)
    pad[axis + 1] = (0, LANE - dim)
    out = jnp.pad(wm.reshape(shp), pad)
    return out.reshape(wm.shape[:axis] + (n_heads * LANE,) + wm.shape[axis + 1:])


def _prep_weight(name, piece):
    shp, ax, _ = _BIG[name]
    full = jnp.moveaxis(piece, 0, ax).reshape(shp[:ax] + (N_DEV * shp[ax],) + shp[ax + 1:])
    if name == "w_in":
        cq, ckv, kr, lx, lg, sq, sk, sv = _split_cols(full)
        return {"win": jnp.concatenate(
            [_pad_heads(sq, SWA_HEADS, SWA_HEAD_DIM), lx, lg, cq, _pad_heads(sk, SWA_KV_HEADS, SWA_HEAD_DIM),
             _pad_heads(sv, SWA_KV_HEADS, SWA_HEAD_DIM), ckv, jnp.pad(kr, ((0, 0), (MLA_NOPE, LANE - MLA_QK)))], axis=1)}
    if name == "w_uq":
        return {"wuq": _pad_heads(full, MLA_HEADS, MLA_QK)}
    if name == "w_ukv":
        ukv = full.reshape(MLA_KV_RANK, MLA_HEADS, MLA_NOPE + MLA_V)
        return {"wk": _pad_heads(ukv[:, :, :MLA_NOPE].reshape(MLA_KV_RANK, -1), MLA_HEADS, MLA_NOPE),
                "wv": _pad_heads(ukv[:, :, MLA_NOPE:].reshape(MLA_KV_RANK, -1), MLA_HEADS, MLA_V)}
    if name == "w_out":
        return {"wout": jnp.concatenate(
            [_pad_heads(full[:GROUP_WIDTH], MLA_HEADS, MLA_V, axis=0), full[GROUP_WIDTH:2 * GROUP_WIDTH],
             _pad_heads(full[2 * GROUP_WIDTH:], SWA_HEADS, SWA_HEAD_DIM, axis=0)], axis=0)}
    return {_BIG[name][2][0]: full}


def _split_cols(wm):
    parts, start = [], 0
    for size in IN_SIZES:
        parts.append(wm[:, start:start + size])
        start += size
    return parts


def _prep_small(raw):
    r1 = lambda a: a.reshape(1, -1)
    gw = raw["lru_gate_w"].reshape(2, 2, 4, 2, 64, 64)
    wbd = jnp.einsum("zgknCm,nN->knCzgNm", gw, jnp.eye(2, dtype=F32)).reshape(4, LANE, 4, LANE)
    gg = raw["group_g"]
    sink = raw["swa_sink"].reshape(SWA_KV_HEADS, SWA_GROUP, 1, 1)
    return {
        "norm1_g": r1(raw["norm1_g"]), "norm2_g": r1(raw["norm2_g"]),
        "q_a_g": r1(raw["q_a_g"]), "kv_a_g": r1(raw["kv_a_g"]),
        "mla_q_g": jnp.pad(r1(raw["mla_q_g"]), ((0, 0), (0, LANE - MLA_QK))),
        "mla_k_g": jnp.pad(r1(raw["mla_k_g"]), ((0, 0), (0, LANE - MLA_QK))),
        "swa_q_g": jnp.pad(r1(raw["swa_q_g"]), ((0, 0), (0, LANE - SWA_HEAD_DIM))),
        "swa_k_g": jnp.pad(r1(raw["swa_k_g"]), ((0, 0), (0, LANE - SWA_HEAD_DIM))),
        "conv_w": [r1(raw["conv_w"][kk]) for kk in range(4)], "conv_b": r1(raw["conv_b"]),
        "wbd": wbd.transpose(0, 2, 1, 3).reshape(16, LANE, LANE),
        "gate_b": [r1(raw["lru_gate_b"][z, g]) for z in range(2) for g in range(2)],
        "sp": [r1(jax.nn.softplus(-raw["lru_lambda"][z])) for z in range(2)],
        "sink_b": jnp.broadcast_to(sink, (SWA_KV_HEADS, SWA_GROUP, QB_SWA, LANE)).reshape(
            SWA_KV_HEADS, SWA_GROUP * QB_SWA, LANE),
        "g_a": _pad_heads(r1(gg[:GROUP_WIDTH]), MLA_HEADS, MLA_V), "g_b": r1(gg[GROUP_WIDTH:2 * GROUP_WIDTH]),
        "g_c": _pad_heads(r1(gg[2 * GROUP_WIDTH:]), SWA_HEADS, SWA_HEAD_DIM)}


def _mesh_pos():
    return lax.axis_index("x"), lax.axis_index("y"), lax.axis_index("c")


def _peer(pos, k):
    return tuple(1 - p if (k >> s) & 1 else p for p, s in zip(pos, (2, 1, 0)))


def _dev_index(pos):
    return 4 * pos[0] + 2 * pos[1] + pos[2]


class _Exchange:
    def __init__(self, bufs, gather):
        self.bufs = list(bufs)
        self.n = len(self.bufs)
        self.gather = [gather] * self.n if isinstance(gather, bool) else list(gather)
        self.specs = [pl.BlockSpec(memory_space=pl.ANY)] * self.n
        self.out_shape = [jax.ShapeDtypeStruct((N_DEV,) + tuple(b.shape if g else b.shape[1:]), b.dtype)
                          for b, g in zip(self.bufs, self.gather)]
        self.scratch = [pltpu.SemaphoreType.DMA(((N_DEV - 1) * self.n,)),
                        pltpu.SemaphoreType.DMA(((N_DEV - 1) * self.n,)), pltpu.SemaphoreType.DMA((self.n,))]

    def _copies(self, x_refs, o_refs, sems, with_recvs):
        send_sems, recv_sems, local_sems = sems
        pos = _mesh_pos()
        me = _dev_index(pos)
        locals_, sends, recvs = [], [], []
        for j in range(self.n):
            src_mine = x_refs[j] if self.gather[j] else x_refs[j].at[me]
            locals_.append(pltpu.make_async_copy(src_mine, o_refs[j].at[me], local_sems.at[j]))
        for k in range(1, N_DEV):
            peer = _peer(pos, k)
            pidx = _dev_index(peer)
            for j in range(self.n):
                src = x_refs[j] if self.gather[j] else x_refs[j].at[pidx]
                sem = (k - 1) * self.n + j
                sends.append(pltpu.make_async_remote_copy(
                    src_ref=src, dst_ref=o_refs[j].at[me], send_sem=send_sems.at[sem], recv_sem=recv_sems.at[sem],
                    device_id=peer, device_id_type=pl.DeviceIdType.MESH))
                if with_recvs:
                    recvs.append(pltpu.make_async_remote_copy(
                        src_ref=src, dst_ref=o_refs[j].at[pidx], send_sem=send_sems.at[sem],
                        recv_sem=recv_sems.at[sem], device_id=peer, device_id_type=pl.DeviceIdType.MESH))
        return locals_, sends, recvs

    def start(self, x_refs, o_refs, sems):
        locals_, sends, _ = self._copies(x_refs, o_refs, sems, False)
        for cp in locals_ + sends:
            cp.start()

    def wait(self, x_refs, o_refs, sems):
        locals_, sends, recvs = self._copies(x_refs, o_refs, sems, True)
        for cp in recvs:
            cp.wait_recv()
        for cp in sends:
            cp.wait_send()
        for cp in locals_:
            cp.wait()


def _exchange(bufs, gather, name):
    xchg = _Exchange(bufs, gather)
    n = xchg.n

    def body(*refs):
        xchg.start(refs[:n], refs[n:2 * n], refs[2 * n:])
        xchg.wait(refs[:n], refs[n:2 * n], refs[2 * n:])

    return _pcall(body, name=name, out_shape=xchg.out_shape, in_specs=xchg.specs, out_specs=xchg.specs,
                  scratch_shapes=xchg.scratch)(*xchg.bufs)


def _pack(arrs, dtype):
    flat = jnp.concatenate([a.reshape(-1).astype(dtype) for a in arrs])
    rows = -(-flat.size // PACK_W)
    rows = -(-rows // 16) * 16
    return jnp.pad(flat, (0, rows * PACK_W - flat.size)).reshape(rows, PACK_W)


def _unpack(buf, shapes, lead=()):
    flat = buf.reshape(lead + (-1,))
    out, off = [], 0
    for shp in shapes:
        n = math.prod(shp)
        out.append(flat[..., off:off + n].reshape(lead + tuple(shp)))
        off += n
    return out


def _sum_sources(buf, name):
    _, r, c = buf.shape
    tr = _rows_tile(r)

    def body(x_ref, o_ref):
        acc = x_ref[0]
        for d in range(1, N_DEV):
            acc = acc + x_ref[d]
        o_ref[...] = acc

    return _pcall(body, name=name, grid=(r // tr,),
                  in_specs=[pl.BlockSpec((N_DEV, tr, c), lambda i: (0, i, 0))],
                  out_specs=pl.BlockSpec((tr, c), lambda i: (i, 0)),
                  out_shape=jax.ShapeDtypeStruct((r, c), F32), compiler_params=_cparams(1))(buf)


def _rows_tile(r):
    best = r
    for t in range(SUBLANE, 257, SUBLANE):
        if r % t == 0:
            best = t
    return best


def _adamw(grads, wgt, m, v, name):
    n_src, r, c = grads.shape
    tr = _rows_tile(r)
    bc1 = 1.0 - ADAM_B1 ** ADAM_STEP
    bc2 = 1.0 - ADAM_B2 ** ADAM_STEP

    def body(g_ref, w_ref, m_ref, v_ref, go_ref, d_ref, mo_ref, vo_ref):
        g = g_ref[0].astype(F32)
        for d in range(1, n_src):
            g = g + g_ref[d].astype(F32)
        m_new = ADAM_B1 * m_ref[...] + (1.0 - ADAM_B1) * g
        v_new = ADAM_B2 * v_ref[...] + (1.0 - ADAM_B2) * (g * g)
        go_ref[...] = g
        mo_ref[...] = m_new
        vo_ref[...] = v_new
        d_ref[...] = -ADAM_LR * ((m_new / bc1) / (jnp.sqrt(v_new / bc2) + ADAM_EPS) + ADAM_WD * w_ref[...])

    spec = pl.BlockSpec((tr, c), lambda i: (i, 0))
    return _pcall(body, name=name, grid=(r // tr,),
                  in_specs=[pl.BlockSpec((n_src, tr, c), lambda i: (0, i, 0)), spec, spec, spec],
                  out_specs=[spec] * 4, out_shape=[jax.ShapeDtypeStruct((r, c), F32)] * 4,
                  compiler_params=_cparams(1))(grads, wgt, m, v)


def _silu(z):
    return z * jax.nn.sigmoid(z)


_WEIGHTS = ("c_ctx", "w_mod", "b_mod", "norm1_g", "w_in", "q_a_g", "w_uq", "kv_a_g", "w_ukv", "mla_q_g", "mla_k_g",
            "conv_w", "conv_b", "lru_gate_w", "lru_gate_b", "lru_lambda", "swa_q_g", "swa_k_g", "swa_sink", "group_g",
            "w_out", "norm2_g", "w_ff1", "w_ff2")
_SHARDED_SMALL = ("conv_w", "lru_gate_b", "lru_lambda")
_REPL_RAW = ("norm1_g", "q_a_g", "kv_a_g", "mla_q_g", "mla_k_g", "conv_b", "lru_gate_w", "swa_q_g", "swa_k_g",
             "swa_sink", "group_g", "norm2_g")
MOD_ROWS = 32


def kernel(x, c, ctx, c_ctx, w_mod, b_mod, norm1_g, w_in, q_a_g, w_uq, kv_a_g, w_ukv, mla_q_g, mla_k_g, conv_w, conv_b, lru_gate_w, lru_gate_b, lru_lambda, swa_q_g, swa_k_g, swa_sink, group_g, w_out, norm2_g, w_ff1, w_ff2, loss_target, m_c_ctx, m_w_mod, m_b_mod, m_norm1_g, m_w_in, m_q_a_g, m_w_uq, m_kv_a_g, m_w_ukv, m_mla_q_g, m_mla_k_g, m_conv_w, m_conv_b, m_lru_gate_w, m_lru_gate_b, m_lru_lambda, m_swa_q_g, m_swa_k_g, m_swa_sink, m_group_g, m_w_out, m_norm2_g, m_w_ff1, m_w_ff2, v_c_ctx, v_w_mod, v_b_mod, v_norm1_g, v_w_in, v_q_a_g, v_w_uq, v_kv_a_g, v_w_ukv, v_mla_q_g, v_mla_k_g, v_conv_w, v_conv_b, v_lru_gate_w, v_lru_gate_b, v_lru_lambda, v_swa_q_g, v_swa_k_g, v_swa_sink, v_group_g, v_w_out, v_norm2_g, v_w_ff1, v_w_ff2):
    wts = dict(c_ctx=c_ctx, w_mod=w_mod, b_mod=b_mod, norm1_g=norm1_g, w_in=w_in, q_a_g=q_a_g, w_uq=w_uq,
               kv_a_g=kv_a_g, w_ukv=w_ukv, mla_q_g=mla_q_g, mla_k_g=mla_k_g, conv_w=conv_w, conv_b=conv_b,
               lru_gate_w=lru_gate_w, lru_gate_b=lru_gate_b, lru_lambda=lru_lambda, swa_q_g=swa_q_g, swa_k_g=swa_k_g,
               swa_sink=swa_sink, group_g=group_g, w_out=w_out, norm2_g=norm2_g, w_ff1=w_ff1, w_ff2=w_ff2)
    mom1 = dict(zip(_WEIGHTS, (m_c_ctx, m_w_mod, m_b_mod, m_norm1_g, m_w_in, m_q_a_g, m_w_uq, m_kv_a_g, m_w_ukv,
                               m_mla_q_g, m_mla_k_g, m_conv_w, m_conv_b, m_lru_gate_w, m_lru_gate_b, m_lru_lambda,
                               m_swa_q_g, m_swa_k_g, m_swa_sink, m_group_g, m_w_out, m_norm2_g, m_w_ff1, m_w_ff2)))
    mom2 = dict(zip(_WEIGHTS, (v_c_ctx, v_w_mod, v_b_mod, v_norm1_g, v_w_in, v_q_a_g, v_w_uq, v_kv_a_g, v_w_ukv,
                               v_mla_q_g, v_mla_k_g, v_conv_w, v_conv_b, v_lru_gate_w, v_lru_gate_b, v_lru_lambda,
                               v_swa_q_g, v_swa_k_g, v_swa_sink, v_group_g, v_w_out, v_norm2_g, v_w_ff1, v_w_ff2)))
    bsz = x.shape[0]
    n_ex = bsz * N_DEV
    me = _dev_index(_mesh_pos())
    mod_cols = w_mod.shape[-1]

    small_shapes = [c.shape, conv_w.shape, lru_gate_b.shape, lru_lambda.shape]
    (g_small,) = _exchange([_pack([c, conv_w, lru_gate_b, lru_lambda], F32)], True, "ag_small")
    c_all, conv_w_all, gate_b_all, lam_all = _unpack(g_small, small_shapes, lead=(N_DEV,))
    c_all = c_all.reshape(n_ex, D_MODEL)
    cat_last = lambda a: jnp.moveaxis(a, 0, -2).reshape(a.shape[1:-1] + (N_DEV * a.shape[-1],))
    conv_w_full, gate_b_full, lam_full = cat_last(conv_w_all), cat_last(gate_b_all), cat_last(lam_all)

    act = jnp.zeros((MOD_ROWS, D_MODEL), F32).at[:n_ex].set(_silu(c_all)).at[n_ex].set(_silu(c_ctx))
    mod_part = jnp.concatenate([_mm(act, w_mod[li], "nn", F32, "mm_mod_l%d" % li) for li in range(DEPTH)], axis=1)
    (mod_all,) = _exchange([mod_part], True, "ag_mod")
    mods = []
    for li in range(DEPTH):
        full = jnp.moveaxis(mod_all[:, :, li * mod_cols:(li + 1) * mod_cols], 0, 1).reshape(MOD_ROWS, -1) + b_mod[li]
        mine = lax.dynamic_slice_in_dim(full, me * bsz, bsz, axis=0)
        ctx_row = jnp.broadcast_to(full[n_ex], mine.shape)
        both = jnp.stack([ctx_row, mine], axis=1).reshape(bsz, 2, N_MOD, 1, D_MODEL)
        mods.append([both[:, :, j] for j in range(N_MOD)])

    raw = {n: wts[n] for n in _REPL_RAW}
    raw.update(conv_w=conv_w_full, lru_gate_b=gate_b_full, lru_lambda=lam_full)
    small_names = list(_REPL_RAW) + list(_SHARDED_SMALL)
    sp, small_vjp = [None] * DEPTH, [None] * DEPTH
    for li in range(DEPTH):
        sp[li], small_vjp[li] = jax.vjp(_prep_small, {n: raw[n][li] for n in small_names})

    w, w_vjp, g_recv, small_recv = [{} for _ in range(DEPTH)], {}, {}, {}
    shard = lambda n, li: wts[n][li].astype(BF16)

    def take(li, names, pieces):
        for n, piece in zip(names, pieces):
            out, w_vjp[n, li] = jax.vjp(functools.partial(_prep_weight, n), piece)
            w[li].update(out)

    def gather_hook(li, names):
        return (lambda _: _Exchange([shard(n, li) for n in names], True), lambda got: take(li, names, got))

    def wgrad(n, li, dwl):
        (g,) = w_vjp[n, li]({k: dwl[k].astype(BF16) for k in _BIG[n][2]})
        return g

    def small_pack(li, ds_l, extra=()):
        (d_raw,) = small_vjp[li](ds_l)
        return _pack([d_raw[n] for n in small_names] + list(extra), F32)

    take(0, _EARLY, _exchange([shard(n, 0) for n in _EARLY], True, "ag_early"))
    hooks_fwd = [{"mla_fwd": gather_hook(0, _LATE), "swa_fwd": gather_hook(1, _EARLY + ("w_out",))},
                 {"mla_fwd": gather_hook(1, ("w_ff1", "w_ff2"))}]
    bwd_state = {}

    def scatter_last_layer(dwl):
        return _Exchange([wgrad(n, 1, dwl) for n in _LATE], False)

    def scatter_first_layer(dwl):
        dw1, ds1 = bwd_state["dw1"], bwd_state["ds1"]
        bufs = [wgrad(n, 1, dw1) for n in _EARLY] + [wgrad(n, 0, dwl) for n in _LATE] + [small_pack(1, ds1)]
        return _Exchange(bufs, [False] * (len(_EARLY) + len(_LATE)) + [True])

    def scattered_first_layer(got):
        g_recv.update(zip([(n, 1) for n in _EARLY] + [(n, 0) for n in _LATE], got[:-1]))
        small_recv[1] = got[-1]

    hooks_bwd = [{"mla_bwd": (scatter_first_layer, scattered_first_layer)},
                 {"mla_bwd": (scatter_last_layer, lambda got: g_recv.update(zip([(n, 1) for n in _LATE], got)))}]

    tc, lat = ctx.shape[1], x.shape[1]
    tabs = {"mla": _rope_tables(lat, tc, MLA_ROPE, MLA_NOPE), "swa": _rope_tables(lat, tc, SWA_HEAD_DIM, 0)}
    stream = jnp.concatenate([ctx, x], axis=1)
    bwds = []
    for li in range(DEPTH):
        stream, bwd = _layer(li, stream, mods[li], w[li], sp[li], tabs, tc, li < DEPTH - 1, hooks_fwd[li])
        bwds.append(bwd)
    loss_part, dstream = _loss_and_grad(stream, loss_target, tc)
    dmods = [None] * DEPTH
    dstream, dmods[1], bwd_state["dw1"], bwd_state["ds1"] = bwds[1](dstream, hooks_bwd[1])
    dstream, dmods[0], dw0, ds0 = bwds[0](dstream, hooks_bwd[0])
    grad_x = dstream[:, tc:]

    dm_rows = []
    for li in range(DEPTH):
        dm = jnp.concatenate(dmods[li], axis=-1)
        dm_rows.append(jnp.concatenate([dm[:, 1, 0], jnp.sum(dm[:, 0, 0], axis=0, keepdims=True)], axis=0))
    dm_mine = jnp.concatenate(dm_rows, axis=1)
    dm_mine = jnp.pad(dm_mine, ((0, SUBLANE - bsz - 1), (0, 0)))
    (dm_all,) = _exchange([dm_mine], True, "ag_dmod")
    g_wmod, g_bmod, dact_ctx = [], [], jnp.zeros((D_MODEL,), F32)
    for li in range(DEPTH):
        part = dm_all[:, :, li * N_MOD * D_MODEL:(li + 1) * N_MOD * D_MODEL]
        dm32 = jnp.zeros((MOD_ROWS, N_MOD * D_MODEL), F32).at[:n_ex].set(part[:, :bsz].reshape(n_ex, -1))
        dm32 = dm32.at[n_ex].set(jnp.sum(part[:, bsz], axis=0))
        g_bmod.append(jnp.sum(dm32, axis=0))
        cols = lax.dynamic_slice_in_dim(dm32, me * mod_cols, mod_cols, axis=1)
        g_wmod.append(_mm(act, cols, "tn", F32, "mm_mod_dw_l%d" % li))
        dact_ctx = dact_ctx + _mm(cols, w_mod[li], "nt", F32, "mm_mod_dx_l%d" % li)[n_ex]
    sg = jax.nn.sigmoid(c_ctx)
    g_cctx_part = dact_ctx * (sg * (1.0 + c_ctx * (1.0 - sg)))

    last = _exchange([wgrad(n, 0, dw0) for n in _EARLY] + [small_pack(0, ds0, (g_cctx_part, loss_part.reshape(1)))],
                     [False] * len(_EARLY) + [True], "rs_early")
    g_recv.update(zip([(n, 0) for n in _EARLY], last[:-1]))
    small_recv[0] = last[-1]
    layer_shapes = [raw[n].shape[1:] for n in small_names]
    tot = [_unpack(_sum_sources(small_recv[li], "sum_grads_l%d" % li), layer_shapes + [(D_MODEL,), (1,)][:2 * (li == 0)])
           for li in range(DEPTH)]
    grads = {n: jnp.stack([tot[li][j] for li in range(DEPTH)], axis=0) for j, n in enumerate(small_names)}
    grads["c_ctx"], loss = tot[0][-2], tot[0][-1][0]
    for n in _SHARDED_SMALL:
        width = wts[n].shape[-1]
        grads[n] = lax.dynamic_slice_in_dim(grads[n], me * width, width, axis=grads[n].ndim - 1)
    grads["b_mod"] = jnp.stack(g_bmod, axis=0)

    delta, new_m, new_v = {}, {}, {}
    for n, (shp, _, _) in _BIG.items():
        two_d = (DEPTH * shp[0], shp[1])
        src = jnp.stack([g_recv[n, li] for li in range(DEPTH)], axis=1).reshape((N_DEV,) + two_d)
        res = _adamw(src, wts[n].reshape(two_d), mom1[n].reshape(two_d), mom2[n].reshape(two_d), "adamw_" + n)
        grads[n], delta[n], new_m[n], new_v[n] = [r.reshape(wts[n].shape) for r in res]
    two_d = (DEPTH * D_MODEL, mod_cols)
    res = _adamw(jnp.stack(g_wmod, axis=0).reshape((1,) + two_d), w_mod.reshape(two_d), mom1["w_mod"].reshape(two_d),
                 mom2["w_mod"].reshape(two_d), "adamw_w_mod")
    grads["w_mod"], delta["w_mod"], new_m["w_mod"], new_v["w_mod"] = [r.reshape(w_mod.shape) for r in res]
    rest = [n for n in _WEIGHTS if n not in delta]
    shapes = [wts[n].shape for n in rest]
    res = _adamw(_pack([grads[n] for n in rest], F32)[None], _pack([wts[n] for n in rest], F32),
                 _pack([mom1[n] for n in rest], F32), _pack([mom2[n] for n in rest], F32), "adamw_small")
    for tgt, buf in zip((delta, new_m, new_v), res[1:]):
        tgt.update(zip(rest, _unpack(buf, shapes)))

    return (loss, grad_x, *[grads[n] for n in _WEIGHTS], *[delta[n] for n in _WEIGHTS],
            *[new_m[n] for n in _WEIGHTS], *[new_v[n] for n in _WEIGHTS])
```

```python
import functools
import math

import jax
import jax.numpy as jnp
from jax import lax
from jax.experimental import pallas as pl
from jax.experimental.pallas import tpu as pltpu

F32, BF16 = jnp.float32, jnp.bfloat16

N_DEV = 8
DEPTH = 2
D_MODEL = 1024
D_FF = 4096
N_MOD = 6
GRID_W = 64
WINDOW = 128
ROPE_THETA = 10000.0
EPS = 1e-6
NEG_INF = -1e30
LRU_C = 8.0
LRU_WIDTH = 512
MLA_HEADS, MLA_NOPE, MLA_ROPE, MLA_V = 8, 64, 32, 64
MLA_QK = MLA_NOPE + MLA_ROPE
MLA_Q_RANK, MLA_KV_RANK = 256, 128
SWA_HEADS, SWA_KV_HEADS, SWA_GROUP, SWA_HEAD_DIM = 8, 2, 4, 64
GROUP_WIDTH = 512
IN_SIZES = (256, 128, 32, 512, 512, 512, 128, 128)
IN_WIDTH = sum(IN_SIZES)
ADAM_LR, ADAM_B1, ADAM_B2, ADAM_EPS, ADAM_WD, ADAM_STEP = 0.001, 0.9, 0.999, 1e-08, 0.01, 10

LANE = 128
SUBLANE = 8
TB = 256
QB_SWA = 128
PACK_W = 1024
MM_K_CAP = 4608
MLA_HPS = 2
VMEM_LIMIT = 56 * 1024 * 1024
P_WIDTH = 3072
PC_SQ, PC_LX, PC_LG, PC_CQ, PC_SK, PC_SV, PC_CKV, PC_KR = 0, 1024, 1536, 2048, 2304, 2560, 2816, 2944
MIX_P = 1536


def _pcall(body, **kw):
    return pl.pallas_call(body, **kw)


def _cparams(n_grid):
    return pltpu.CompilerParams(dimension_semantics=("arbitrary",) * n_grid, vmem_limit_bytes=VMEM_LIMIT)


def _dg(a, b, ca, cb):
    return lax.dot_general(a.astype(BF16), b.astype(BF16), (((ca,), (cb,)), ((), ())),
                           preferred_element_type=F32)


@jax.custom_vjp
def _nn(a, b):
    return _dg(a, b, 1, 0)


@jax.custom_vjp
def _nt(a, b):
    return _dg(a, b, 1, 1)


@jax.custom_vjp
def _tn(a, b):
    return _dg(a, b, 0, 0)


_nn.defvjp(lambda a, b: (_nn(a, b), (a, b)), lambda r, ct: (_nt(ct, r[1]), _tn(r[0], ct)))
_nt.defvjp(lambda a, b: (_nt(a, b), (a, b)), lambda r, ct: (_nn(ct, r[1]), _tn(ct, r[0])))
_tn.defvjp(lambda a, b: (_tn(a, b), (a, b)), lambda r, ct: (_nt(r[1], ct), _nn(r[0], ct)))


@functools.partial(jax.custom_vjp, nondiff_argnums=(1, 2))
def _roll(x, shift, axis):
    return pltpu.roll(x, shift % x.shape[axis], axis)


_roll.defvjp(lambda x, shift, axis: (_roll(x, shift, axis), None),
             lambda shift, axis, _, ct: (_roll(ct, -shift, axis),))


@functools.partial(jax.custom_vjp, nondiff_argnums=(1, 2))
def _split(x, n, axis):
    w = x.shape[axis] // n
    return tuple(lax.slice_in_dim(x, i * w, (i + 1) * w, axis=axis) for i in range(n))


_split.defvjp(lambda x, n, axis: (_split(x, n, axis), None),
              lambda n, axis, _, cts: (jnp.concatenate(cts, axis=axis),))


@jax.custom_vjp
def _unstack(x):
    return tuple(x[i] for i in range(x.shape[0]))


_unstack.defvjp(lambda x: (_unstack(x), None), lambda _, cts: (jnp.stack(cts, axis=0),))


def _sig(x):
    return 0.5 * (jnp.tanh(0.5 * x) + 1.0)


def _gelu(x):
    return 0.5 * x * (1.0 + jnp.tanh(math.sqrt(2.0 / math.pi) * (x + 0.044715 * (x * x * x))))


def _rms(x, g, n):
    ms = jnp.sum(x * x, axis=-1, keepdims=True) * (1.0 / n)
    return x * lax.rsqrt(ms + EPS) * g


def _rope(y, cos, sa, sb, quarter):
    return y * cos + _roll(y, -quarter, 1) * sa + _roll(y, quarter, 1) * sb


def _softmax_rows(s, extra=None):
    m = jnp.max(s, axis=-1, keepdims=True)
    if extra is not None:
        m = jnp.maximum(m, extra)
    m = lax.stop_gradient(m)
    e = jnp.exp(s - m)
    den = jnp.sum(e, axis=-1, keepdims=True)
    if extra is not None:
        den = den + jnp.exp(extra - m)
    return e / den


class _A:
    def __init__(self, arr, block, imap, kind="row", first=None, gdtype=F32, gshape=None, gimap=None):
        self.arr, self.block, self.imap, self.kind, self.first = arr, block, imap, kind, first
        self.gdtype, self.gshape, self.gimap = gdtype, gshape, gimap


def _all_zero(*ids):
    return functools.reduce(jnp.logical_and, [i == 0 for i in ids])


def _par(arr):
    nd = arr.ndim
    return _A(arr, arr.shape, lambda *ids: (0,) * nd, "acc", first=_all_zero)


def _op_fwd(name, fn, grid, args, outs):
    n_in = len(args)

    def body(*refs):
        vals = [r[...].astype(F32) for r in refs[:n_in]]
        for r, v in zip(refs[n_in:], fn(*vals)):
            r[...] = v.astype(r.dtype)

    return _pcall(
        body, name=name, grid=grid,
        in_specs=[pl.BlockSpec(a.block, a.imap) for a in args],
        out_specs=[pl.BlockSpec(o[2], o[3]) for o in outs],
        out_shape=[jax.ShapeDtypeStruct(o[0], o[1]) for o in outs],
        compiler_params=_cparams(len(grid)),
    )(*[a.arr for a in args])


def _op_bwd(name, fn, grid, args, outs, ct_arrays, add_to_first=None):
    didx = [i for i, a in enumerate(args) if a.kind not in ("const", "fwd")]
    read = [i for i, a in enumerate(args) if a.kind != "fwd"]
    n_in, n_ct = len(read), len(outs)
    n_add = 0 if add_to_first is None else 1

    def body(*refs):
        ids = [pl.program_id(i) for i in range(len(grid))]
        vals = [jnp.zeros([d for d in a.block if d is not None], F32) for a in args]
        for i, r in zip(read, refs[:n_in]):
            vals[i] = r[...].astype(F32)

        def g(*dv):
            full = list(vals)
            for i, v in zip(didx, dv):
                full[i] = v
            return tuple(fn(*full))

        _, vjp = jax.vjp(g, *[vals[i] for i in didx])
        grads = list(vjp(tuple(r[...].astype(F32) for r in refs[n_in:n_in + n_ct])))
        if n_add:
            grads[0] = grads[0] + refs[n_in + n_ct][...]
        for gr, i, r in zip(grads, didx, refs[n_in + n_ct + n_add:]):
            a = args[i]
            if a.kind == "row":
                r[...] = gr.astype(r.dtype)
            else:
                first = a.first(*ids)

                @pl.when(first)
                def _():
                    r[...] = gr

                @pl.when(jnp.logical_not(first))
                def _():
                    r[...] += gr

    g_specs, g_shapes = [], []
    for i in didx:
        a = args[i]
        if a.kind == "row":
            g_specs.append(pl.BlockSpec(a.block, a.gimap or a.imap))
            g_shapes.append(jax.ShapeDtypeStruct(a.gshape or a.arr.shape, a.gdtype))
        else:
            g_specs.append(pl.BlockSpec(a.block, a.imap))
            g_shapes.append(jax.ShapeDtypeStruct(a.arr.shape, F32))
    return _pcall(
        body, name=name, grid=grid,
        in_specs=[pl.BlockSpec(args[i].block, args[i].imap) for i in read] + [pl.BlockSpec(o[2], o[3]) for o in outs]
        + g_specs[:n_add],
        out_specs=g_specs, out_shape=g_shapes,
        compiler_params=_cparams(len(grid)),
    )(*[args[i].arr for i in read], *ct_arrays, *([add_to_first] if n_add else []))


def _rowop(name, fn, grid, args, outs):
    res = _op_fwd(name, fn, grid, args, outs)
    return res, lambda *cts, add_to_first=None: _op_bwd(name + "_bwd", fn, grid, args, outs, cts, add_to_first)


def _pick(n, cap):
    best = None
    for t in range(LANE, cap + 1, LANE):
        if n % t == 0:
            best = t
    return best or n


def _mm(a, b, mode, out_dtype, name, epi=None, aux=None):
    if mode == "nn":
        (m, k), n = a.shape, b.shape[1]
    elif mode == "nt":
        (m, k), n = a.shape, b.shape[0]
    else:
        (k, m), n = a.shape, b.shape[1]
    tm = 512 if m % 512 == 0 else m
    tn, tk = _pick(n, 1024), _pick(k, MM_K_CAP)
    nk = k // tk
    if mode == "tn":
        a_spec = pl.BlockSpec((tk, tm), lambda j, i, kk: (kk, i))
    else:
        a_spec = pl.BlockSpec((tm, tk), lambda j, i, kk: (i, kk))
    if mode == "nt":
        b_spec = pl.BlockSpec((tn, tk), lambda j, i, kk: (j, kk))
    else:
        b_spec = pl.BlockSpec((tk, tn), lambda j, i, kk: (kk, j))
    dims = {"nn": (1, 0), "nt": (1, 1), "tn": (0, 0)}[mode]
    o_spec = pl.BlockSpec((tm, tn), lambda j, i, kk: (i, j))
    n_aux = 0 if aux is None else 1
    n_out = 2 if epi == "sqrelu" else 1

    def body(*refs):
        a_ref, b_ref = refs[0], refs[1]
        o_refs = refs[2 + n_aux:2 + n_aux + n_out]
        acc = refs[-1]
        kk = pl.program_id(2)
        part = _dg(a_ref[...], b_ref[...], *dims)

        if nk > 1:
            @pl.when(kk == 0)
            def _():
                acc[...] = part

            @pl.when((kk > 0) & (kk < nk - 1))
            def _():
                acc[...] += part

        @pl.when(kk == nk - 1)
        def _():
            r = part if nk == 1 else acc[...] + part
            if epi == "sqrelu":
                o_refs[0][...] = r.astype(o_refs[0].dtype)
                rl = jnp.maximum(r, 0.0)
                o_refs[1][...] = (rl * rl).astype(o_refs[1].dtype)
            elif epi == "dsqrelu":
                pre = refs[2][...].astype(F32)
                o_refs[0][...] = (r * (2.0 * jnp.maximum(pre, 0.0))).astype(o_refs[0].dtype)
            else:
                o_refs[0][...] = r.astype(o_refs[0].dtype)

    res = _pcall(
        body, name=name, grid=(n // tn, m // tm, nk),
        in_specs=[a_spec, b_spec] + [o_spec] * n_aux,
        out_specs=[o_spec] * n_out,
        out_shape=[jax.ShapeDtypeStruct((m, n), out_dtype)] * n_out,
        scratch_shapes=[pltpu.VMEM((tm, tn), F32)],
        compiler_params=_cparams(3),
    )(a, b, *([aux] if aux is not None else []))
    return res if n_out == 2 else res[0]


ROW_CHUNK = 16


def _softmax_chunks(s_scr, n_keys, scale, emit):
    for r0 in range(0, s_scr.shape[0], ROW_CHUNK):
        rows = slice(r0, r0 + ROW_CHUNK)
        s = s_scr[rows, :n_keys]
        e = jnp.exp((s - jnp.max(s, axis=-1, keepdims=True)) * scale)
        emit(rows, e, 1.0 / jnp.sum(e, axis=-1, keepdims=True))


def _attn_fwd_block(q, k, v, scale, s_scr, e_scr, l_scr):
    n = k.shape[0]
    s_scr[:, :n] = _dg(q, k, 1, 1)

    def emit(rows, e, inv_l):
        e_scr[rows, :n] = e.astype(BF16)
        l_scr[rows, :] = jnp.broadcast_to(inv_l, (ROW_CHUNK, LANE))

    _softmax_chunks(s_scr, n, scale, emit)
    return _dg(e_scr[:, :n], v, 1, 0) * l_scr[...]


def _attn_bwd_block(q, k, v, o, do, scale, s_scr, dp_scr, p_scr, ds_scr):
    n = k.shape[0]
    s_scr[:, :n] = _dg(q, k, 1, 1)
    dp_scr[:, :n] = _dg(do, v, 1, 1)

    def emit(rows, e, inv_l):
        p = e * inv_l
        delta = jnp.sum(do[rows, :] * o[rows, :], axis=-1, keepdims=True)
        p_scr[rows, :n] = p.astype(BF16)
        ds_scr[rows, :n] = (p * (dp_scr[rows, :n] - delta) * scale).astype(BF16)

    _softmax_chunks(s_scr, n, scale, emit)
    ds = ds_scr[:, :n]
    return _dg(ds, k, 1, 0), _dg(ds, q, 0, 0), _dg(p_scr[:, :n], do, 0, 0)


def _call_with_exchange(body, xchg, *, name, grid, in_specs, out_specs, out_shape, operands, scratch_shapes=()):
    if xchg is None:
        res = _pcall(body, name=name, grid=grid, in_specs=in_specs, out_specs=out_specs, out_shape=out_shape,
                     scratch_shapes=list(scratch_shapes), compiler_params=_cparams(len(grid)))(*operands)
        return list(res), []
    n_in, n_out, n_sc, n = len(in_specs), len(out_specs), len(scratch_shapes), xchg.n

    def wrapped(*refs):
        ins, x_refs = refs[:n_in], refs[n_in:n_in + n]
        outs, xo_refs = refs[n_in + n:n_in + n + n_out], refs[n_in + n + n_out:n_in + 2 * n + n_out]
        scratch, sems = refs[n_in + 2 * n + n_out:n_in + 2 * n + n_out + n_sc], refs[n_in + 2 * n + n_out + n_sc:]
        ids = [pl.program_id(i) for i in range(len(grid))]

        @pl.when(functools.reduce(jnp.logical_and, [i == 0 for i in ids]))
        def _():
            xchg.start(x_refs, xo_refs, sems)

        body(*ins, *outs, *scratch)

        @pl.when(functools.reduce(jnp.logical_and, [i == g - 1 for i, g in zip(ids, grid)]))
        def _():
            xchg.wait(x_refs, xo_refs, sems)

    res = _pcall(wrapped, name=name, grid=grid, in_specs=list(in_specs) + xchg.specs,
                 out_specs=list(out_specs) + xchg.specs, out_shape=list(out_shape) + xchg.out_shape,
                 scratch_shapes=list(scratch_shapes) + xchg.scratch, compiler_params=_cparams(len(grid)),
                 )(*operands, *xchg.bufs)
    return list(res[:n_out]), list(res[n_out:])


def _head_half(i, shape):
    lane = lax.broadcasted_iota(jnp.int32, shape, len(shape) - 1)
    return (lane < LANE // 2) if i == 0 else (lane >= LANE // 2)


def _mla_attn(q, k, v, tc, ctx_q, name, xchg=None):
    assert MLA_HPS == 2 and MLA_V == LANE // 2
    bsz, t_all, _ = q.shape
    n_t = t_all // TB
    grid = (bsz, MLA_HEADS // MLA_HPS, n_t)
    q_spec = pl.BlockSpec((None, TB, MLA_HPS * LANE), lambda b, h, t: (b, t, h))
    k_spec = pl.BlockSpec((None, t_all, MLA_HPS * LANE), lambda b, h, t: (b, 0, h))
    v_spec = pl.BlockSpec((None, t_all, LANE), lambda b, h, t: (b, 0, h))
    o_spec = pl.BlockSpec((None, TB, LANE), lambda b, h, t: (b, t, h))
    heads = [slice(i * LANE, (i + 1) * LANE) for i in range(MLA_HPS)]
    scale = MLA_QK ** -0.5
    f32_scr, bf16_scr = pltpu.VMEM((TB, t_all), F32), pltpu.VMEM((TB, t_all), BF16)
    o_shape = jax.ShapeDtypeStruct(v.shape, F32)

    def fwd_body(q_ref, k_ref, v_ref, o_ref, *scr):
        t = pl.program_id(2)

        def run(keys):
            both = [_attn_fwd_block(q_ref[:, hs], k_ref[keys, hs], v_ref[keys, :], scale, *scr[3 * i:3 * i + 3])
                    for i, hs in enumerate(heads)]
            o_ref[...] = jnp.where(_head_half(0, both[0].shape), both[0], both[1])

        @pl.when(t == 0)
        def _():
            if ctx_q:
                run(slice(0, tc))
            else:
                o_ref[...] = jnp.zeros_like(o_ref)

        @pl.when(t > 0)
        def _():
            run(slice(0, t_all))

    (o,), gathered = _call_with_exchange(
        fwd_body, xchg, name=name, grid=grid, in_specs=[q_spec, k_spec, v_spec], out_specs=[o_spec],
        out_shape=[o_shape], operands=(q, k, v),
        scratch_shapes=[f32_scr, bf16_scr, pltpu.VMEM((TB, LANE), F32)] * MLA_HPS)

    def bwd(do, xchg=None):
        def bwd_body(q_ref, k_ref, v_ref, o_ref, do_ref, dq_ref, dk_ref, dv_ref, *scr):
            t = pl.program_id(2)

            def run(keys, first):
                dvs = []
                for i, hs in enumerate(heads):
                    do_i = jnp.where(_head_half(i, do_ref.shape), do_ref[...], 0.0)
                    dq, dk, dv = _attn_bwd_block(q_ref[:, hs], k_ref[keys, hs], v_ref[keys, :], o_ref[...], do_i,
                                                 scale, *scr[4 * i:4 * i + 4])
                    dq_ref[:, hs] = dq
                    dvs.append(dv)
                    if first:
                        dk_ref[keys, hs] = dk
                    else:
                        dk_ref[keys, hs] += dk
                if first:
                    dv_ref[keys, :] = dvs[0] + dvs[1]
                else:
                    dv_ref[keys, :] += dvs[0] + dvs[1]

            @pl.when(t == 0)
            def _():
                dk_ref[...] = jnp.zeros_like(dk_ref)
                dv_ref[...] = jnp.zeros_like(dv_ref)
                if ctx_q:
                    run(slice(0, tc), True)
                else:
                    dq_ref[...] = jnp.zeros_like(dq_ref)

            @pl.when(t > 0)
            def _():
                run(slice(0, t_all), False)

        return _call_with_exchange(
            bwd_body, xchg, name=name + "_bwd", grid=grid, in_specs=[q_spec, k_spec, v_spec, o_spec, o_spec],
            out_specs=[q_spec, k_spec, v_spec],
            out_shape=[jax.ShapeDtypeStruct(q.shape, F32), jax.ShapeDtypeStruct(q.shape, F32), o_shape],
            operands=(q, k, v, o, do), scratch_shapes=[f32_scr, f32_scr, bf16_scr, bf16_scr] * MLA_HPS)

    return o, gathered, bwd


def _swa_block(q, keys, vals, sink, mask):
    qs = jnp.concatenate(list(_split(q, SWA_GROUP, 1)), axis=0)
    sk = jnp.sum(sink, axis=-1, keepdims=True) * (1.0 / LANE)
    s = _nt(qs, keys) * (SWA_HEAD_DIM ** -0.5)
    if mask is not None:
        s = jnp.where(mask, s, NEG_INF)
    o = _split(_nn(_softmax_rows(s, sk), vals + _roll(vals, LANE // 2, 1)), SWA_GROUP, 0)
    low = _head_half(0, o[0].shape)
    return jnp.concatenate([jnp.where(low, o[0], o[1]), jnp.where(low, o[2], o[3])], axis=1)


def _swa_ctx_block(q, kc, vc, sink):
    return _swa_block(q, kc, vc, sink, None)


def _swa_win_block(q, kc, kw, vc, vw, sink, mask):
    return _swa_block(q, jnp.concatenate([kc, kw], axis=0), jnp.concatenate([vc, vw], axis=0), sink, mask)


def _swa_attn(q, k, p_all, sink_b, tc, ctx_q, name, xchg=None):
    bsz, t_all, _ = q.shape
    n_q = t_all // QB_SWA
    n_cq = tc // QB_SWA
    lat = t_all - tc
    span = QB_SWA + 2 * WINDOW
    gw = SWA_GROUP * LANE
    grid = (bsz, SWA_KV_HEADS, n_q)
    q_spec = pl.BlockSpec((None, QB_SWA, gw), lambda b, g, i: (b, i, g))
    k_spec = pl.BlockSpec((None, t_all, LANE), lambda b, g, i: (b, 0, g))
    v_spec = pl.BlockSpec((None, t_all, LANE), lambda b, g, i: (b, 0, PC_SV // LANE + g))
    s_spec = pl.BlockSpec((None, SWA_GROUP * QB_SWA, LANE), lambda b, g, i: (g, 0, 0))

    def window(i):
        q0 = (i - n_cq) * QB_SWA
        w0 = jnp.clip(q0 - WINDOW, 0, lat - span)
        w0 = pl.multiple_of(w0, QB_SWA)
        shape = (SWA_GROUP * QB_SWA, tc + span)
        qi = q0 + lax.broadcasted_iota(jnp.int32, shape, 0) % QB_SWA
        col = lax.broadcasted_iota(jnp.int32, shape, 1)
        kj = w0 + col - tc
        mask = (col < tc) | ((kj >= qi - WINDOW) & (kj <= qi + WINDOW))
        return w0, mask

    def fwd_body(q_ref, k_ref, v_ref, s_ref, o_ref):
        i = pl.program_id(2)

        @pl.when(i < n_cq)
        def _():
            if ctx_q:
                o_ref[...] = _swa_ctx_block(q_ref[...].astype(F32), k_ref[0:tc, :], v_ref[0:tc, :], s_ref[...])
            else:
                o_ref[...] = jnp.zeros_like(o_ref)

        @pl.when(i >= n_cq)
        def _():
            w0, mask = window(i)
            o_ref[...] = _swa_win_block(q_ref[...].astype(F32), k_ref[0:tc, :], k_ref[pl.ds(tc + w0, span), :],
                                        v_ref[0:tc, :], v_ref[pl.ds(tc + w0, span), :], s_ref[...], mask)

    o_spec = pl.BlockSpec((None, QB_SWA, SWA_GROUP * SWA_HEAD_DIM), lambda b, g, i: (b, i, g))
    (o,), gathered = _call_with_exchange(
        fwd_body, xchg, name=name, grid=grid, in_specs=[q_spec, k_spec, v_spec, s_spec], out_specs=[o_spec],
        out_shape=[jax.ShapeDtypeStruct((bsz, t_all, SWA_HEADS * SWA_HEAD_DIM), F32)],
        operands=(q, k, p_all, sink_b))

    def bwd(do, xchg=None):
        def bwd_body(q_ref, k_ref, v_ref, s_ref, do_ref, dq_ref, dk_ref, dv_ref, ds_ref):
            i = pl.program_id(2)

            @pl.when(i == 0)
            def _():
                dk_ref[...] = jnp.zeros_like(dk_ref)
                dv_ref[...] = jnp.zeros_like(dv_ref)
                ds_ref[...] = jnp.zeros_like(ds_ref)

            @pl.when(i < n_cq)
            def _():
                if ctx_q:
                    _, vjp = jax.vjp(_swa_ctx_block, q_ref[...].astype(F32), k_ref[0:tc, :].astype(F32),
                                     v_ref[0:tc, :], s_ref[...])
                    dq, dk, dv, ds = vjp(do_ref[...])
                    dq_ref[...] = dq
                    dk_ref[0:tc, :] += dk
                    dv_ref[0:tc, :] += dv
                    ds_ref[...] += ds
                else:
                    dq_ref[...] = jnp.zeros_like(dq_ref)

            @pl.when(i >= n_cq)
            def _():
                w0, mask = window(i)
                win = pl.ds(tc + w0, span)
                _, vjp = jax.vjp(functools.partial(_swa_win_block, mask=mask), q_ref[...].astype(F32),
                                 k_ref[0:tc, :].astype(F32), k_ref[win, :].astype(F32),
                                 v_ref[0:tc, :], v_ref[win, :], s_ref[...])
                dq, dkc, dkw, dvc, dvw, ds = vjp(do_ref[...])
                dq_ref[...] = dq
                dk_ref[0:tc, :] += dkc
                dk_ref[win, :] += dkw
                dv_ref[0:tc, :] += dvc
                dv_ref[win, :] += dvw
                ds_ref[...] += ds

        kv_out = pl.BlockSpec((None, t_all, LANE), lambda b, g, i: (b, 0, g))
        ds_spec = pl.BlockSpec((None, None, SWA_GROUP * QB_SWA, LANE), lambda b, g, i: (b, g, 0, 0))
        kv_shape = jax.ShapeDtypeStruct((bsz, t_all, SWA_KV_HEADS * LANE), F32)
        return _call_with_exchange(
            bwd_body, xchg, name=name + "_bwd", grid=grid, in_specs=[q_spec, k_spec, v_spec, s_spec, o_spec],
            out_specs=[q_spec, kv_out, kv_out, ds_spec],
            out_shape=[jax.ShapeDtypeStruct(q.shape, F32), kv_shape, kv_shape,
                       jax.ShapeDtypeStruct((bsz,) + sink_b.shape, F32)],
            operands=(q, k, p_all, sink_b, do))

    return o, gathered, bwd


def _scan_rows(a, u, reverse, a_s, u_s, c_s):
    t_all, c = a.shape
    row8 = lax.broadcasted_iota(jnp.int32, a.shape, 0) % SUBLANE
    for d in (1, 2, 4):
        sh = d if not reverse else t_all - d
        ar, ur = pltpu.roll(a, sh, 0), pltpu.roll(u, sh, 0)
        m = (row8 >= d) if not reverse else (row8 < SUBLANE - d)
        u = jnp.where(m, a * ur + u, u)
        a = jnp.where(m, a * ar, a)
    a_s[...] = a
    u_s[...] = u
    n_tiles = t_all // SUBLANE

    def step(j, carry):
        tile = j if not reverse else n_tiles - 1 - j
        base = pl.multiple_of(tile * SUBLANE, SUBLANE)
        c_s[pl.ds(base, SUBLANE), :] = jnp.broadcast_to(carry, (SUBLANE, c))
        last = base + (0 if reverse else SUBLANE - 1)
        return a_s[pl.ds(last, 1), :] * carry + u_s[pl.ds(last, 1), :]

    lax.fori_loop(0, n_tiles, step, jnp.zeros((1, c), F32))
    return a_s[...] * c_s[...] + u_s[...]


def _shift_rows(x, reverse_src):
    t_all = x.shape[0]
    row = lax.broadcasted_iota(jnp.int32, x.shape, 0)
    if reverse_src:
        return jnp.where(row == t_all - 1, 0.0, pltpu.roll(x, t_all - 1, 0))
    return jnp.where(row == 0, 0.0, pltpu.roll(x, 1, 0))


def _lru_scan(a0, u0, a1, u1, name):
    bsz, t_all, w = a0.shape
    grid = (bsz, w // LANE)
    spec = pl.BlockSpec((None, t_all, LANE), lambda b, c: (b, 0, c))
    scratch = [pltpu.VMEM((t_all, LANE), F32)] * 3
    shape = jax.ShapeDtypeStruct(a0.shape, F32)

    def fwd_body(a0_ref, u0_ref, a1_ref, u1_ref, h0_ref, h1_ref, a_s, u_s, c_s):
        h0_ref[...] = _scan_rows(a0_ref[...], u0_ref[...], False, a_s, u_s, c_s)
        h1_ref[...] = _scan_rows(a1_ref[...], u1_ref[...], True, a_s, u_s, c_s)

    h0, h1 = _pcall(fwd_body, name=name, grid=grid, in_specs=[spec] * 4, out_specs=[spec] * 2,
                    out_shape=[shape] * 2, scratch_shapes=scratch, compiler_params=_cparams(2))(a0, u0, a1, u1)

    def bwd(dh0, dh1):
        def bwd_body(a0_ref, h0_ref, g0_ref, a1_ref, h1_ref, g1_ref, da0_ref, du0_ref, da1_ref, du1_ref,
                     a_s, u_s, c_s):
            g0 = _scan_rows(_shift_rows(a0_ref[...], True), g0_ref[...], True, a_s, u_s, c_s)
            du0_ref[...] = g0
            da0_ref[...] = g0 * _shift_rows(h0_ref[...], False)
            g1 = _scan_rows(_shift_rows(a1_ref[...], False), g1_ref[...], False, a_s, u_s, c_s)
            du1_ref[...] = g1
            da1_ref[...] = g1 * _shift_rows(h1_ref[...], True)

        return _pcall(bwd_body, name=name + "_bwd", grid=grid, in_specs=[spec] * 6, out_specs=[spec] * 4,
                      out_shape=[shape] * 4, scratch_shapes=scratch,
                      compiler_params=_cparams(2))(a0, h0, dh0, a1, h1, dh1)

    return h0, h1, bwd


def _f_mod(x, g, shift, scale):
    return (_rms(x, g, D_MODEL) * (1.0 + scale) + shift,)


def _f_mla_q(cq, ga, w, gh, cos, sa, sb):
    n = _rms(cq, ga, MLA_Q_RANK)
    outs = []
    for wh in _split(w, MLA_HEADS, 1):
        outs.append(_rope(_rms(_nn(n, wh), gh, MLA_QK), cos, sa, sb, MLA_ROPE // 4))
    return (jnp.concatenate(outs, axis=1),)


def _f_mla_kv(ckv, krp, ga, wk, wv, gh, cos, sa, sb):
    n = _rms(ckv, ga, MLA_KV_RANK)
    outs = []
    for wh in _split(wk, MLA_HEADS, 1):
        outs.append(_rope(_rms(_nn(n, wh) + krp, gh, MLA_QK), cos, sa, sb, MLA_ROPE // 4))
    return jnp.concatenate(outs, axis=1), _nn(n, wv)


def _f_conv(x, w0, w1, w2, w3, bias, tc):
    t_all = x.shape[0]
    row = lax.broadcasted_iota(jnp.int32, x.shape, 0)
    lo = jnp.where(row < tc, 0, tc)
    hi = jnp.where(row < tc, tc, t_all)
    y = bias + jnp.zeros_like(x)
    for kk, wk in enumerate((w0, w1, w2, w3)):
        src = row + (kk - 2)
        xs = x if kk == 2 else _roll(x, 2 - kk, 0)
        y = y + wk * jnp.where((src >= lo) & (src < hi), xs, 0.0)
    return (y,)


def _f_gates(xc, w16, b00, b01, b10, b11, sp0, sp1):
    ws = _unstack(w16)
    n_cb = LRU_WIDTH // LANE
    xcs = _split(xc, n_cb, 1)
    bias = [_split(b, n_cb, 1) for b in (b00, b01, b10, b11)]
    sps = [_split(s, n_cb, 1) for s in (sp0, sp1)]
    res = [[], [], [], []]
    for c in range(n_cb):
        for z in range(2):
            r = _sig(_nn(xcs[c], ws[c * 4 + 2 * z]) + bias[2 * z][c])
            i = _sig(_nn(xcs[c], ws[c * 4 + 2 * z + 1]) + bias[2 * z + 1][c])
            la = -LRU_C * r * sps[z][c]
            res[2 * z].append(jnp.exp(la))
            res[2 * z + 1].append(jnp.sqrt(-jnp.tanh(la) * (jnp.exp(2.0 * la) + 1.0)) * (i * xcs[c]))
    return tuple(jnp.concatenate(r, axis=1) for r in res)


def _f_lru_out(h0, h1, lg):
    return ((h0 + h1) * _gelu(lg),)


def _f_swa_qk(sq, sk, gq, gk, cos, sa, sb):
    qs = [_rope(_rms(x, gq, SWA_HEAD_DIM), cos, sa, sb, SWA_HEAD_DIM // 4) for x in _split(sq, SWA_HEADS, 1)]
    ks = [_rope(_rms(x, gk, SWA_HEAD_DIM), cos, sa, sb, SWA_HEAD_DIM // 4) for x in _split(sk, SWA_KV_HEADS, 1)]
    return jnp.concatenate(qs, axis=1), jnp.concatenate(ks, axis=1)


def _f_merge(oa, ob, oc, ga, gb, gc):
    return (jnp.concatenate([_rms(oa, ga, GROUP_WIDTH), _rms(ob, gb, GROUP_WIDTH), _rms(oc, gc, GROUP_WIDTH)],
                            axis=1),)


def _f_resid_mod(x, y, gate, g, shift, scale):
    x1 = x + gate * y
    return x1, _rms(x1, g, D_MODEL) * (1.0 + scale) + shift


def _f_resid(x, y, gate):
    return (x + gate * y,)


def _hosted(hooks, key, arg=None):
    make, done = hooks.get(key, (None, None))
    xchg = make(arg) if make is not None else None
    return xchg, (done if xchg is not None else lambda outs: None)


def _layer(li, x, mods, w, s, tabs, tc, ctx_q, hooks):
    bsz, t_all, _ = x.shape
    n_t = t_all // TB
    grid = (bsz, n_t)
    rows = lambda b, t: (b, t, 0)

    def row(arr, width=None, idx=0, gdtype=F32, gshape=None):
        width = width or arr.shape[-1]
        return _A(arr, (None, TB, width), lambda b, t: (b, t, idx), "row", gdtype=gdtype, gshape=gshape,
                  gimap=rows if gshape is not None else None)

    def out(width, dtype, imap=rows):
        return ((bsz, t_all, width), dtype, (None, TB, width), imap)

    def modarg(arr):
        return _A(arr, (None, None, 1, D_MODEL), lambda b, t: (b, jnp.minimum(t, 1), 0, 0), "acc",
                  first=lambda b, t: t <= 1)

    def tab(arr):
        return _A(arr, (TB, LANE), lambda b, t: (t, 0), "const")

    def pcol(p_all, col, width):
        return row(p_all, width, col // width, gdtype=BF16, gshape=(bsz, t_all, width))

    nm = lambda base: "%s_l%d" % (base, li)
    sh1, sc1, g1, sh2, sc2, g2 = mods
    m_all = bsz * t_all

    (h,), b_mod1 = _rowop(nm("mod1"), _f_mod, grid, [row(x), _par(s["norm1_g"]), modarg(sh1), modarg(sc1)],
                          [out(D_MODEL, BF16)])
    p_all = _mm(h.reshape(m_all, D_MODEL), w["win"], "nn", F32, nm("mm_in")).reshape(bsz, t_all, P_WIDTH)

    tq, tk_ = tabs["mla"], tabs["mla"]
    (q_a,), b_mq = _rowop(nm("mla_q"), _f_mla_q, grid,
                          [pcol(p_all, PC_CQ, 256), _par(s["q_a_g"]), _par(w["wuq"]), _par(s["mla_q_g"])]
                          + [tab(a) for a in tq], [out(MLA_HEADS * LANE, BF16)])
    (k_a, v_a), b_mkv = _rowop(nm("mla_kv"), _f_mla_kv, grid,
                               [pcol(p_all, PC_CKV, 128), pcol(p_all, PC_KR, 128), _par(s["kv_a_g"]), _par(w["wk"]),
                                _par(w["wv"]), _par(s["mla_k_g"])] + [tab(a) for a in tk_],
                               [out(MLA_HEADS * LANE, BF16), out(MLA_HEADS * MLA_V, BF16)])
    xchg, done = _hosted(hooks, "mla_fwd")
    o_a, got, b_attn_a = _mla_attn(q_a, k_a, v_a, tc, ctx_q, nm("mla_attn"), xchg)
    done(got)

    n_cb = LRU_WIDTH // LANE
    conv_grid = (n_cb, bsz)
    cpar = lambda arr: _A(arr, (1, LANE), lambda c, b: (0, c), "acc", first=lambda c, b: b == 0)
    conv_args = [_A(p_all, (None, t_all, LANE), lambda c, b: (b, 0, PC_LX // LANE + c), "row", gdtype=BF16,
                    gshape=(bsz, t_all, LRU_WIDTH), gimap=lambda c, b: (b, 0, c))]
    conv_args += [cpar(a) for a in s["conv_w"]] + [cpar(s["conv_b"])]
    conv_out = [((bsz, t_all, LRU_WIDTH), F32, (None, t_all, LANE), lambda c, b: (b, 0, c))]
    (xc,), b_conv = _rowop(nm("lru_conv"), functools.partial(_f_conv, tc=tc), conv_grid, conv_args, conv_out)
    rot = lambda b, t: (b, (t + n_t - 1) % n_t, 0)
    (a0, u0, a1, u1), b_gates = _rowop(
        nm("lru_gates"), _f_gates, grid,
        [row(xc), _par(s["wbd"])] + [_par(a) for a in s["gate_b"]] + [_par(a) for a in s["sp"]],
        [out(LRU_WIDTH, F32), out(LRU_WIDTH, F32), out(LRU_WIDTH, F32, rot), out(LRU_WIDTH, F32, rot)])
    h0, h1, b_scan = _lru_scan(a0, u0, a1, u1, nm("lru_scan"))
    h1_arg = _A(h1, (None, TB, LRU_WIDTH), rot, "row")
    (o_b,), b_lout = _rowop(nm("lru_out"), _f_lru_out, grid, [row(h0), h1_arg, pcol(p_all, PC_LG, 512)],
                            [out(LRU_WIDTH, F32)])

    ts = tabs["swa"]
    (q_c, k_c), b_sqk = _rowop(nm("swa_qk"), _f_swa_qk, grid,
                               [pcol(p_all, PC_SQ, 1024), pcol(p_all, PC_SK, 256), _par(s["swa_q_g"]),
                                _par(s["swa_k_g"])] + [tab(a) for a in ts],
                               [out(SWA_HEADS * LANE, BF16), out(SWA_KV_HEADS * LANE, BF16)])
    xchg, done = _hosted(hooks, "swa_fwd")
    o_c, got, b_attn_c = _swa_attn(q_c, k_c, p_all, s["sink_b"], tc, ctx_q, nm("swa_attn"), xchg)
    done(got)

    (y_in,), b_merge = _rowop(nm("merge"), _f_merge, grid,
                              [row(o_a), row(o_b), row(o_c), _par(s["g_a"]), _par(s["g_b"]), _par(s["g_c"])],
                              [out(MIX_P, BF16)])
    y = _mm(y_in.reshape(m_all, MIX_P), w["wout"], "nn", F32, nm("mm_out")).reshape(bsz, t_all, D_MODEL)
    (x1, hm), b_rm = _rowop(nm("resid_mod"), _f_resid_mod, grid,
                            [row(x), row(y, gdtype=BF16), modarg(g1), _par(s["norm2_g"]), modarg(sh2), modarg(sc2)],
                            [out(D_MODEL, F32), out(D_MODEL, BF16)])
    pre, act = _mm(hm.reshape(m_all, D_MODEL), w["ff1"], "nn", BF16, nm("mm_ff1"), epi="sqrelu")
    y2 = _mm(act, w["ff2"], "nn", F32, nm("mm_ff2")).reshape(bsz, t_all, D_MODEL)
    (x2,), b_res = _rowop(nm("resid"), _f_resid, grid,
                          [_A(x1, (None, TB, D_MODEL), rows, "fwd"), row(y2, gdtype=BF16), modarg(g2)],
                          [out(D_MODEL, F32)])

    def bwd(dx2, hooks):
        dw, ds = {}, {}
        dy2, dg2 = b_res(dx2)
        dy2 = dy2.reshape(m_all, D_MODEL)
        dpre = _mm(dy2, w["ff2"], "nt", BF16, nm("mm_ff2_dx"), epi="dsqrelu", aux=pre)
        dw["ff2"] = _mm(act, dy2, "tn", BF16, nm("mm_ff2_dw"))
        dhm = _mm(dpre, w["ff1"], "nt", F32, nm("mm_ff1_dx")).reshape(bsz, t_all, D_MODEL)
        dw["ff1"] = _mm(hm.reshape(m_all, D_MODEL), dpre, "tn", BF16, nm("mm_ff1_dw"))
        dxa, dy, dg1, ds["norm2_g"], dsh2, dsc2 = b_rm(dx2, dhm)
        dy = dy.reshape(m_all, D_MODEL)
        dy_in = _mm(dy, w["wout"], "nt", F32, nm("mm_out_dx")).reshape(bsz, t_all, MIX_P)
        dw["wout"] = _mm(y_in.reshape(m_all, MIX_P), dy, "tn", BF16, nm("mm_out_dw"))
        do_a, do_b, do_c, ds["g_a"], ds["g_b"], ds["g_c"] = b_merge(dy_in)

        (dq_c, dk_c, dsv, dsink), _ = b_attn_c(do_c)
        ds["sink_b"] = jnp.sum(dsink, axis=0)
        dsq, dsk, ds["swa_q_g"], ds["swa_k_g"] = b_sqk(dq_c, dk_c)

        dh0, dh1, dlg = b_lout(do_b)
        da0, du0, da1, du1 = b_scan(dh0, dh1)
        gates_g = b_gates(da0, du0, da1, du1)
        dxc, ds["wbd"] = gates_g[0], gates_g[1]
        ds["gate_b"], ds["sp"] = list(gates_g[2:6]), list(gates_g[6:8])
        conv_g = b_conv(dxc)
        dlx, ds["conv_w"], ds["conv_b"] = conv_g[0], list(conv_g[1:5]), conv_g[5]

        xchg, done = _hosted(hooks, "mla_bwd", dw)
        (dq_a, dk_a, dv_a), got = b_attn_a(do_a, xchg)
        done(got)
        dcq, ds["q_a_g"], dw["wuq"], ds["mla_q_g"] = b_mq(dq_a)
        dckv, dkr, ds["kv_a_g"], dw["wk"], dw["wv"], ds["mla_k_g"] = b_mkv(dk_a, dv_a)

        dp = jnp.concatenate([dsq, dlx, dlg, dcq, dsk, dsv.astype(BF16), dckv, dkr], axis=-1)
        dp = dp.reshape(m_all, P_WIDTH)
        dh = _mm(dp, w["win"], "nt", F32, nm("mm_in_dx")).reshape(bsz, t_all, D_MODEL)
        dw["win"] = _mm(h.reshape(m_all, D_MODEL), dp, "tn", BF16, nm("mm_in_dw"))
        dx, ds["norm1_g"], dsh1, dsc1 = b_mod1(dh, add_to_first=dxa)
        return dx, [dsh1, dsc1, dg1, dsh2, dsc2, dg2], dw, ds

    return x2, bwd


def _loss_and_grad(x2, target, tc):
    bsz, t_all, d = x2.shape
    n_t = t_all // TB
    n_c = tc // TB

    def body(x_ref, t_ref, l_ref, dx_ref):
        b, t = pl.program_id(0), pl.program_id(1)

        @pl.when((b == 0) & (t == 0))
        def _():
            l_ref[...] = jnp.zeros_like(l_ref)

        @pl.when(t < n_c)
        def _():
            dx_ref[...] = jnp.zeros_like(dx_ref)

        @pl.when(t >= n_c)
        def _():
            e = x_ref[...] - t_ref[...]
            dx_ref[...] = e * (1.0 / d)
            l_ref[...] += jnp.sum(e * e) * (0.5 / d)

    loss, dx = _pcall(
        body, name="loss", grid=(bsz, n_t),
        in_specs=[pl.BlockSpec((None, TB, d), lambda b, t: (b, t, 0)),
                  pl.BlockSpec((None, TB, d), lambda b, t: (b, jnp.maximum(t - n_c, 0), 0))],
        out_specs=[pl.BlockSpec((SUBLANE, LANE), lambda b, t: (0, 0)),
                   pl.BlockSpec((None, TB, d), lambda b, t: (b, t, 0))],
        out_shape=[jax.ShapeDtypeStruct((SUBLANE, LANE), F32), jax.ShapeDtypeStruct(x2.shape, F32)],
        compiler_params=_cparams(2))(x2, target)
    return loss[0, 0], dx


def _rope_tables(lat, tc, dim, lane0):
    quarter = dim // 4
    pos = jnp.arange(lat)
    grid_pos = jnp.stack([pos // GRID_W, pos % GRID_W], axis=-1).astype(F32)
    lane = jnp.arange(LANE)
    p = jnp.clip(lane - lane0, 0, dim - 1)
    active = (lane >= lane0) & (lane < lane0 + dim)
    axis, half, qi = p // (dim // 2), (p % (dim // 2)) // quarter, p % quarter
    inv = ROPE_THETA ** (-qi.astype(F32) / quarter)
    ang = jnp.where(axis[None, :] == 0, grid_pos[:, 0:1], grid_pos[:, 1:2]) * inv[None, :]
    cos = jnp.where(active, jnp.cos(ang), 1.0)
    sin = jnp.where(active, jnp.sin(ang), 0.0)
    sa = jnp.where(half == 0, -sin, 0.0)
    sb = jnp.where(half == 1, sin, 0.0)
    ctx1, ctx0 = jnp.ones((tc, LANE), F32), jnp.zeros((tc, LANE), F32)
    return (jnp.concatenate([ctx1, cos], 0), jnp.concatenate([ctx0, sa], 0), jnp.concatenate([ctx0, sb], 0))


_BIG = {"w_in": ((D_MODEL, IN_WIDTH // N_DEV), 1, ("win",)),
        "w_uq": ((MLA_Q_RANK, MLA_HEADS * MLA_QK // N_DEV), 1, ("wuq",)),
        "w_ukv": ((MLA_KV_RANK, MLA_HEADS * (MLA_NOPE + MLA_V) // N_DEV), 1, ("wk", "wv")),
        "w_out": ((3 * GROUP_WIDTH // N_DEV, D_MODEL), 0, ("wout",)),
        "w_ff1": ((D_MODEL, D_FF // N_DEV), 1, ("ff1",)),
        "w_ff2": ((D_FF // N_DEV, D_MODEL), 0, ("ff2",))}
_EARLY = ("w_in", "w_uq", "w_ukv")
_LATE = ("w_out", "w_ff1", "w_ff2")


def _pad_heads(wm, n_heads, dim, axis=-1):
    axis = axis % wm.ndim
    shp = wm.shape[:axis] + (n_heads, dim) + wm.shape[axis + 1:]
    pad = [(0, 0)] * len(shp)
    pad[axis + 1] = (0, LANE - dim)
    out = jnp.pad(wm.reshape(shp), pad)
    return out.reshape(wm.shape[:axis] + (n_heads * LANE,) + wm.shape[axis + 1:])


def _prep_weight(name, piece):
    shp, ax, _ = _BIG[name]
    full = jnp.moveaxis(piece, 0, ax).reshape(shp[:ax] + (N_DEV * shp[ax],) + shp[ax + 1:])
    if name == "w_in":
        cq, ckv, kr, lx, lg, sq, sk, sv = _split_cols(full)
        return {"win": jnp.concatenate(
            [_pad_heads(sq, SWA_HEADS, SWA_HEAD_DIM), lx, lg, cq, _pad_heads(sk, SWA_KV_HEADS, SWA_HEAD_DIM),
             _pad_heads(sv, SWA_KV_HEADS, SWA_HEAD_DIM), ckv, jnp.pad(kr, ((0, 0), (MLA_NOPE, LANE - MLA_QK)))], axis=1)}
    if name == "w_uq":
        return {"wuq": _pad_heads(full, MLA_HEADS, MLA_QK)}
    if name == "w_ukv":
        ukv = full.reshape(MLA_KV_RANK, MLA_HEADS, MLA_NOPE + MLA_V)
        return {"wk": _pad_heads(ukv[:, :, :MLA_NOPE].reshape(MLA_KV_RANK, -1), MLA_HEADS, MLA_NOPE),
                "wv": ukv[:, :, MLA_NOPE:].reshape(MLA_KV_RANK, -1)}
    return {_BIG[name][2][0]: full}


def _split_cols(wm):
    parts, start = [], 0
    for size in IN_SIZES:
        parts.append(wm[:, start:start + size])
        start += size
    return parts


def _prep_small(raw):
    r1 = lambda a: a.reshape(1, -1)
    gw = raw["lru_gate_w"].reshape(2, 2, 4, 2, 64, 64)
    wbd = jnp.einsum("zgknCm,nN->knCzgNm", gw, jnp.eye(2, dtype=F32)).reshape(4, LANE, 4, LANE)
    gg = raw["group_g"]
    sink = raw["swa_sink"].reshape(SWA_KV_HEADS, SWA_GROUP, 1, 1)
    return {
        "norm1_g": r1(raw["norm1_g"]), "norm2_g": r1(raw["norm2_g"]),
        "q_a_g": r1(raw["q_a_g"]), "kv_a_g": r1(raw["kv_a_g"]),
        "mla_q_g": jnp.pad(r1(raw["mla_q_g"]), ((0, 0), (0, LANE - MLA_QK))),
        "mla_k_g": jnp.pad(r1(raw["mla_k_g"]), ((0, 0), (0, LANE - MLA_QK))),
        "swa_q_g": jnp.pad(r1(raw["swa_q_g"]), ((0, 0), (0, LANE - SWA_HEAD_DIM))),
        "swa_k_g": jnp.pad(r1(raw["swa_k_g"]), ((0, 0), (0, LANE - SWA_HEAD_DIM))),
        "conv_w": [r1(raw["conv_w"][kk]) for kk in range(4)], "conv_b": r1(raw["conv_b"]),
        "wbd": wbd.transpose(0, 2, 1, 3).reshape(16, LANE, LANE),
        "gate_b": [r1(raw["lru_gate_b"][z, g]) for z in range(2) for g in range(2)],
        "sp": [r1(jax.nn.softplus(-raw["lru_lambda"][z])) for z in range(2)],
        "sink_b": jnp.broadcast_to(sink, (SWA_KV_HEADS, SWA_GROUP, QB_SWA, LANE)).reshape(
            SWA_KV_HEADS, SWA_GROUP * QB_SWA, LANE),
        "g_a": r1(gg[:GROUP_WIDTH]), "g_b": r1(gg[GROUP_WIDTH:2 * GROUP_WIDTH]), "g_c": r1(gg[2 * GROUP_WIDTH:])}


def _mesh_pos():
    return lax.axis_index("x"), lax.axis_index("y"), lax.axis_index("c")


def _peer(pos, k):
    return tuple(1 - p if (k >> s) & 1 else p for p, s in zip(pos, (2, 1, 0)))


def _dev_index(pos):
    return 4 * pos[0] + 2 * pos[1] + pos[2]


class _Exchange:
    def __init__(self, bufs, gather):
        self.bufs = list(bufs)
        self.n = len(self.bufs)
        self.gather = [gather] * self.n if isinstance(gather, bool) else list(gather)
        self.specs = [pl.BlockSpec(memory_space=pl.ANY)] * self.n
        self.out_shape = [jax.ShapeDtypeStruct((N_DEV,) + tuple(b.shape if g else b.shape[1:]), b.dtype)
                          for b, g in zip(self.bufs, self.gather)]
        self.scratch = [pltpu.SemaphoreType.DMA(((N_DEV - 1) * self.n,)),
                        pltpu.SemaphoreType.DMA(((N_DEV - 1) * self.n,)), pltpu.SemaphoreType.DMA((self.n,))]

    def _copies(self, x_refs, o_refs, sems, with_recvs):
        send_sems, recv_sems, local_sems = sems
        pos = _mesh_pos()
        me = _dev_index(pos)
        locals_, sends, recvs = [], [], []
        for j in range(self.n):
            src_mine = x_refs[j] if self.gather[j] else x_refs[j].at[me]
            locals_.append(pltpu.make_async_copy(src_mine, o_refs[j].at[me], local_sems.at[j]))
        for k in range(1, N_DEV):
            peer = _peer(pos, k)
            pidx = _dev_index(peer)
            for j in range(self.n):
                src = x_refs[j] if self.gather[j] else x_refs[j].at[pidx]
                sem = (k - 1) * self.n + j
                sends.append(pltpu.make_async_remote_copy(
                    src_ref=src, dst_ref=o_refs[j].at[me], send_sem=send_sems.at[sem], recv_sem=recv_sems.at[sem],
                    device_id=peer, device_id_type=pl.DeviceIdType.MESH))
                if with_recvs:
                    recvs.append(pltpu.make_async_remote_copy(
                        src_ref=src, dst_ref=o_refs[j].at[pidx], send_sem=send_sems.at[sem],
                        recv_sem=recv_sems.at[sem], device_id=peer, device_id_type=pl.DeviceIdType.MESH))
        return locals_, sends, recvs

    def start(self, x_refs, o_refs, sems):
        locals_, sends, _ = self._copies(x_refs, o_refs, sems, False)
        for cp in locals_ + sends:
            cp.start()

    def wait(self, x_refs, o_refs, sems):
        locals_, sends, recvs = self._copies(x_refs, o_refs, sems, True)
        for cp in recvs:
            cp.wait_recv()
        for cp in sends:
            cp.wait_send()
        for cp in locals_:
            cp.wait()


def _exchange(bufs, gather, name):
    xchg = _Exchange(bufs, gather)
    n = xchg.n

    def body(*refs):
        xchg.start(refs[:n], refs[n:2 * n], refs[2 * n:])
        xchg.wait(refs[:n], refs[n:2 * n], refs[2 * n:])

    return _pcall(body, name=name, out_shape=xchg.out_shape, in_specs=xchg.specs, out_specs=xchg.specs,
                  scratch_shapes=xchg.scratch)(*xchg.bufs)


def _pack(arrs, dtype):
    flat = jnp.concatenate([a.reshape(-1).astype(dtype) for a in arrs])
    rows = -(-flat.size // PACK_W)
    rows = -(-rows // 16) * 16
    return jnp.pad(flat, (0, rows * PACK_W - flat.size)).reshape(rows, PACK_W)


def _unpack(buf, shapes, lead=()):
    flat = buf.reshape(lead + (-1,))
    out, off = [], 0
    for shp in shapes:
        n = math.prod(shp)
        out.append(flat[..., off:off + n].reshape(lead + tuple(shp)))
        off += n
    return out


def _sum_sources(buf, name):
    _, r, c = buf.shape
    tr = _rows_tile(r)

    def body(x_ref, o_ref):
        acc = x_ref[0]
        for d in range(1, N_DEV):
            acc = acc + x_ref[d]
        o_ref[...] = acc

    return _pcall(body, name=name, grid=(r // tr,),
                  in_specs=[pl.BlockSpec((N_DEV, tr, c), lambda i: (0, i, 0))],
                  out_specs=pl.BlockSpec((tr, c), lambda i: (i, 0)),
                  out_shape=jax.ShapeDtypeStruct((r, c), F32), compiler_params=_cparams(1))(buf)


def _rows_tile(r):
    best = r
    for t in range(SUBLANE, 257, SUBLANE):
        if r % t == 0:
            best = t
    return best


def _adamw(grads, wgt, m, v, name):
    n_src, r, c = grads.shape
    tr = _rows_tile(r)
    bc1 = 1.0 - ADAM_B1 ** ADAM_STEP
    bc2 = 1.0 - ADAM_B2 ** ADAM_STEP

    def body(g_ref, w_ref, m_ref, v_ref, go_ref, d_ref, mo_ref, vo_ref):
        g = g_ref[0].astype(F32)
        for d in range(1, n_src):
            g = g + g_ref[d].astype(F32)
        m_new = ADAM_B1 * m_ref[...] + (1.0 - ADAM_B1) * g
        v_new = ADAM_B2 * v_ref[...] + (1.0 - ADAM_B2) * (g * g)
        go_ref[...] = g
        mo_ref[...] = m_new
        vo_ref[...] = v_new
        d_ref[...] = -ADAM_LR * ((m_new / bc1) / (jnp.sqrt(v_new / bc2) + ADAM_EPS) + ADAM_WD * w_ref[...])

    spec = pl.BlockSpec((tr, c), lambda i: (i, 0))
    return _pcall(body, name=name, grid=(r // tr,),
                  in_specs=[pl.BlockSpec((n_src, tr, c), lambda i: (0, i, 0)), spec, spec, spec],
                  out_specs=[spec] * 4, out_shape=[jax.ShapeDtypeStruct((r, c), F32)] * 4,
                  compiler_params=_cparams(1))(grads, wgt, m, v)


def _silu(z):
    return z * jax.nn.sigmoid(z)


_WEIGHTS = ("c_ctx", "w_mod", "b_mod", "norm1_g", "w_in", "q_a_g", "w_uq", "kv_a_g", "w_ukv", "mla_q_g", "mla_k_g",
            "conv_w", "conv_b", "lru_gate_w", "lru_gate_b", "lru_lambda", "swa_q_g", "swa_k_g", "swa_sink", "group_g",
            "w_out", "norm2_g", "w_ff1", "w_ff2")
_SHARDED_SMALL = ("conv_w", "lru_gate_b", "lru_lambda")
_REPL_RAW = ("norm1_g", "q_a_g", "kv_a_g", "mla_q_g", "mla_k_g", "conv_b", "lru_gate_w", "swa_q_g", "swa_k_g",
             "swa_sink", "group_g", "norm2_g")
MOD_ROWS = 32


def kernel(x, c, ctx, c_ctx, w_mod, b_mod, norm1_g, w_in, q_a_g, w_uq, kv_a_g, w_ukv, mla_q_g, mla_k_g, conv_w, conv_b, lru_gate_w, lru_gate_b, lru_lambda, swa_q_g, swa_k_g, swa_sink, group_g, w_out, norm2_g, w_ff1, w_ff2, loss_target, m_c_ctx, m_w_mod, m_b_mod, m_norm1_g, m_w_in, m_q_a_g, m_w_uq, m_kv_a_g, m_w_ukv, m_mla_q_g, m_mla_k_g, m_conv_w, m_conv_b, m_lru_gate_w, m_lru_gate_b, m_lru_lambda, m_swa_q_g, m_swa_k_g, m_swa_sink, m_group_g, m_w_out, m_norm2_g, m_w_ff1, m_w_ff2, v_c_ctx, v_w_mod, v_b_mod, v_norm1_g, v_w_in, v_q_a_g, v_w_uq, v_kv_a_g, v_w_ukv, v_mla_q_g, v_mla_k_g, v_conv_w, v_conv_b, v_lru_gate_w, v_lru_gate_b, v_lru_lambda, v_swa_q_g, v_swa_k_g, v_swa_sink, v_group_g, v_w_out, v_norm2_g, v_w_ff1, v_w_ff2):
    wts = dict(c_ctx=c_ctx, w_mod=w_mod, b_mod=b_mod, norm1_g=norm1_g, w_in=w_in, q_a_g=q_a_g, w_uq=w_uq,
               kv_a_g=kv_a_g, w_ukv=w_ukv, mla_q_g=mla_q_g, mla_k_g=mla_k_g, conv_w=conv_w, conv_b=conv_b,
               lru_gate_w=lru_gate_w, lru_gate_b=lru_gate_b, lru_lambda=lru_lambda, swa_q_g=swa_q_g, swa_k_g=swa_k_g,
               swa_sink=swa_sink, group_g=group_g, w_out=w_out, norm2_g=norm2_g, w_ff1=w_ff1, w_ff2=w_ff2)
    mom1 = dict(zip(_WEIGHTS, (m_c_ctx, m_w_mod, m_b_mod, m_norm1_g, m_w_in, m_q_a_g, m_w_uq, m_kv_a_g, m_w_ukv,
                               m_mla_q_g, m_mla_k_g, m_conv_w, m_conv_b, m_lru_gate_w, m_lru_gate_b, m_lru_lambda,
                               m_swa_q_g, m_swa_k_g, m_swa_sink, m_group_g, m_w_out, m_norm2_g, m_w_ff1, m_w_ff2)))
    mom2 = dict(zip(_WEIGHTS, (v_c_ctx, v_w_mod, v_b_mod, v_norm1_g, v_w_in, v_q_a_g, v_w_uq, v_kv_a_g, v_w_ukv,
                               v_mla_q_g, v_mla_k_g, v_conv_w, v_conv_b, v_lru_gate_w, v_lru_gate_b, v_lru_lambda,
                               v_swa_q_g, v_swa_k_g, v_swa_sink, v_group_g, v_w_out, v_norm2_g, v_w_ff1, v_w_ff2)))
    bsz = x.shape[0]
    n_ex = bsz * N_DEV
    me = _dev_index(_mesh_pos())
    mod_cols = w_mod.shape[-1]

    small_shapes = [c.shape, conv_w.shape, lru_gate_b.shape, lru_lambda.shape]
    (g_small,) = _exchange([_pack([c, conv_w, lru_gate_b, lru_lambda], F32)], True, "ag_small")
    c_all, conv_w_all, gate_b_all, lam_all = _unpack(g_small, small_shapes, lead=(N_DEV,))
    c_all = c_all.reshape(n_ex, D_MODEL)
    cat_last = lambda a: jnp.moveaxis(a, 0, -2).reshape(a.shape[1:-1] + (N_DEV * a.shape[-1],))
    conv_w_full, gate_b_full, lam_full = cat_last(conv_w_all), cat_last(gate_b_all), cat_last(lam_all)

    act = jnp.zeros((MOD_ROWS, D_MODEL), F32).at[:n_ex].set(_silu(c_all)).at[n_ex].set(_silu(c_ctx))
    mod_part = jnp.concatenate([_mm(act, w_mod[li], "nn", F32, "mm_mod_l%d" % li) for li in range(DEPTH)], axis=1)
    (mod_all,) = _exchange([mod_part], True, "ag_mod")
    mods = []
    for li in range(DEPTH):
        full = jnp.moveaxis(mod_all[:, :, li * mod_cols:(li + 1) * mod_cols], 0, 1).reshape(MOD_ROWS, -1) + b_mod[li]
        mine = lax.dynamic_slice_in_dim(full, me * bsz, bsz, axis=0)
        ctx_row = jnp.broadcast_to(full[n_ex], mine.shape)
        both = jnp.stack([ctx_row, mine], axis=1).reshape(bsz, 2, N_MOD, 1, D_MODEL)
        mods.append([both[:, :, j] for j in range(N_MOD)])

    raw = {n: wts[n] for n in _REPL_RAW}
    raw.update(conv_w=conv_w_full, lru_gate_b=gate_b_full, lru_lambda=lam_full)
    small_names = list(_REPL_RAW) + list(_SHARDED_SMALL)
    sp, small_vjp = [None] * DEPTH, [None] * DEPTH
    for li in range(DEPTH):
        sp[li], small_vjp[li] = jax.vjp(_prep_small, {n: raw[n][li] for n in small_names})

    w, w_vjp, g_recv, small_recv = [{} for _ in range(DEPTH)], {}, {}, {}
    shard = lambda n, li: wts[n][li].astype(BF16)

    def take(li, names, pieces):
        for n, piece in zip(names, pieces):
            out, w_vjp[n, li] = jax.vjp(functools.partial(_prep_weight, n), piece)
            w[li].update(out)

    def gather_hook(li, names):
        return (lambda _: _Exchange([shard(n, li) for n in names], True), lambda got: take(li, names, got))

    def wgrad(n, li, dwl):
        (g,) = w_vjp[n, li]({k: dwl[k].astype(BF16) for k in _BIG[n][2]})
        return g

    def small_pack(li, ds_l, extra=()):
        (d_raw,) = small_vjp[li](ds_l)
        return _pack([d_raw[n] for n in small_names] + list(extra), F32)

    take(0, _EARLY, _exchange([shard(n, 0) for n in _EARLY], True, "ag_early"))
    hooks_fwd = [{"mla_fwd": gather_hook(0, _LATE), "swa_fwd": gather_hook(1, _EARLY + ("w_out",))},
                 {"mla_fwd": gather_hook(1, ("w_ff1", "w_ff2"))}]
    bwd_state = {}

    def scatter_last_layer(dwl):
        return _Exchange([wgrad(n, 1, dwl) for n in _LATE], False)

    def scatter_first_layer(dwl):
        dw1, ds1 = bwd_state["dw1"], bwd_state["ds1"]
        bufs = [wgrad(n, 1, dw1) for n in _EARLY] + [wgrad(n, 0, dwl) for n in _LATE] + [small_pack(1, ds1)]
        return _Exchange(bufs, [False] * (len(_EARLY) + len(_LATE)) + [True])

    def scattered_first_layer(got):
        g_recv.update(zip([(n, 1) for n in _EARLY] + [(n, 0) for n in _LATE], got[:-1]))
        small_recv[1] = got[-1]

    hooks_bwd = [{"mla_bwd": (scatter_first_layer, scattered_first_layer)},
                 {"mla_bwd": (scatter_last_layer, lambda got: g_recv.update(zip([(n, 1) for n in _LATE], got)))}]

    tc, lat = ctx.shape[1], x.shape[1]
    tabs = {"mla": _rope_tables(lat, tc, MLA_ROPE, MLA_NOPE), "swa": _rope_tables(lat, tc, SWA_HEAD_DIM, 0)}
    stream = jnp.concatenate([ctx, x], axis=1)
    bwds = []
    for li in range(DEPTH):
        stream, bwd = _layer(li, stream, mods[li], w[li], sp[li], tabs, tc, li < DEPTH - 1, hooks_fwd[li])
        bwds.append(bwd)
    loss_part, dstream = _loss_and_grad(stream, loss_target, tc)
    dmods = [None] * DEPTH
    dstream, dmods[1], bwd_state["dw1"], bwd_state["ds1"] = bwds[1](dstream, hooks_bwd[1])
    dstream, dmods[0], dw0, ds0 = bwds[0](dstream, hooks_bwd[0])
    grad_x = dstream[:, tc:]

    dm_rows = []
    for li in range(DEPTH):
        dm = jnp.concatenate(dmods[li], axis=-1)
        dm_rows.append(jnp.concatenate([dm[:, 1, 0], jnp.sum(dm[:, 0, 0], axis=0, keepdims=True)], axis=0))
    dm_mine = jnp.concatenate(dm_rows, axis=1)
    dm_mine = jnp.pad(dm_mine, ((0, SUBLANE - bsz - 1), (0, 0)))
    (dm_all,) = _exchange([dm_mine], True, "ag_dmod")
    g_wmod, g_bmod, dact_ctx = [], [], jnp.zeros((D_MODEL,), F32)
    for li in range(DEPTH):
        part = dm_all[:, :, li * N_MOD * D_MODEL:(li + 1) * N_MOD * D_MODEL]
        dm32 = jnp.zeros((MOD_ROWS, N_MOD * D_MODEL), F32).at[:n_ex].set(part[:, :bsz].reshape(n_ex, -1))
        dm32 = dm32.at[n_ex].set(jnp.sum(part[:, bsz], axis=0))
        g_bmod.append(jnp.sum(dm32, axis=0))
        cols = lax.dynamic_slice_in_dim(dm32, me * mod_cols, mod_cols, axis=1)
        g_wmod.append(_mm(act, cols, "tn", F32, "mm_mod_dw_l%d" % li))
        dact_ctx = dact_ctx + _mm(cols, w_mod[li], "nt", F32, "mm_mod_dx_l%d" % li)[n_ex]
    sg = jax.nn.sigmoid(c_ctx)
    g_cctx_part = dact_ctx * (sg * (1.0 + c_ctx * (1.0 - sg)))

    last = _exchange([wgrad(n, 0, dw0) for n in _EARLY] + [small_pack(0, ds0, (g_cctx_part, loss_part.reshape(1)))],
                     [False] * len(_EARLY) + [True], "rs_early")
    g_recv.update(zip([(n, 0) for n in _EARLY], last[:-1]))
    small_recv[0] = last[-1]
    layer_shapes = [raw[n].shape[1:] for n in small_names]
    tot = [_unpack(_sum_sources(small_recv[li], "sum_grads_l%d" % li), layer_shapes + [(D_MODEL,), (1,)][:2 * (li == 0)])
           for li in range(DEPTH)]
    grads = {n: jnp.stack([tot[li][j] for li in range(DEPTH)], axis=0) for j, n in enumerate(small_names)}
    grads["c_ctx"], loss = tot[0][-2], tot[0][-1][0]
    for n in _SHARDED_SMALL:
        width = wts[n].shape[-1]
        grads[n] = lax.dynamic_slice_in_dim(grads[n], me * width, width, axis=grads[n].ndim - 1)
    grads["b_mod"] = jnp.stack(g_bmod, axis=0)

    delta, new_m, new_v = {}, {}, {}
    for n, (shp, _, _) in _BIG.items():
        two_d = (DEPTH * shp[0], shp[1])
        src = jnp.stack([g_recv[n, li] for li in range(DEPTH)], axis=1).reshape((N_DEV,) + two_d)
        res = _adamw(src, wts[n].reshape(two_d), mom1[n].reshape(two_d), mom2[n].reshape(two_d), "adamw_" + n)
        grads[n], delta[n], new_m[n], new_v[n] = [r.reshape(wts[n].shape) for r in res]
    two_d = (DEPTH * D_MODEL, mod_cols)
    res = _adamw(jnp.stack(g_wmod, axis=0).reshape((1,) + two_d), w_mod.reshape(two_d), mom1["w_mod"].reshape(two_d),
                 mom2["w_mod"].reshape(two_d), "adamw_w_mod")
    grads["w_mod"], delta["w_mod"], new_m["w_mod"], new_v["w_mod"] = [r.reshape(w_mod.shape) for r in res]
    rest = [n for n in _WEIGHTS if n not in delta]
    shapes = [wts[n].shape for n in rest]
    res = _adamw(_pack([grads[n] for n in rest], F32)[None], _pack([wts[n] for n in rest], F32),
                 _pack([mom1[n] for n in rest], F32), _pack([mom2[n] for n in rest], F32), "adamw_small")
    for tgt, buf in zip((delta, new_m, new_v), res[1:]):
        tgt.update(zip(rest, _unpack(buf, shapes)))

    return (loss, grad_x, *[grads[n] for n in _WEIGHTS], *[delta[n] for n in _WEIGHTS],
            *[new_m[n] for n in _WEIGHTS], *[new_v[n] for n in _WEIGHTS])
```

```python
import functools
import math

import jax
import jax.numpy as jnp
from jax import lax
from jax.experimental import pallas as pl
from jax.experimental.pallas import tpu as pltpu

F32, BF16 = jnp.float32, jnp.bfloat16

N_DEV = 8
DEPTH = 2
D_MODEL = 1024
D_FF = 4096
N_MOD = 6
GRID_W = 64
WINDOW = 128
ROPE_THETA = 10000.0
EPS = 1e-6
NEG_INF = -1e30
LRU_C = 8.0
LRU_WIDTH = 512
MLA_HEADS, MLA_NOPE, MLA_ROPE, MLA_V = 8, 64, 32, 64
MLA_QK = MLA_NOPE + MLA_ROPE
MLA_Q_RANK, MLA_KV_RANK = 256, 128
SWA_HEADS, SWA_KV_HEADS, SWA_GROUP, SWA_HEAD_DIM = 8, 2, 4, 64
GROUP_WIDTH = 512
IN_SIZES = (256, 128, 32, 512, 512, 512, 128, 128)
IN_WIDTH = sum(IN_SIZES)
ADAM_LR, ADAM_B1, ADAM_B2, ADAM_EPS, ADAM_WD, ADAM_STEP = 0.001, 0.9, 0.999, 1e-08, 0.01, 10

LANE = 128
SUBLANE = 8
TB = 256
QB_SWA = 128
PACK_W = 1024
MM_K_CAP = 4608
MLA_HPS = 2
VMEM_LIMIT = 56 * 1024 * 1024
P_WIDTH = 3072
PC_SQ, PC_LX, PC_LG, PC_CQ, PC_SK, PC_SV, PC_CKV, PC_KR = 0, 1024, 1536, 2048, 2304, 2560, 2816, 2944
MIX_P = 1536


def _pcall(body, **kw):
    return pl.pallas_call(body, **kw)


def _cparams(n_grid):
    return pltpu.CompilerParams(dimension_semantics=("arbitrary",) * n_grid, vmem_limit_bytes=VMEM_LIMIT)


def _dg(a, b, ca, cb):
    return lax.dot_general(a.astype(BF16), b.astype(BF16), (((ca,), (cb,)), ((), ())),
                           preferred_element_type=F32)


@jax.custom_vjp
def _nn(a, b):
    return _dg(a, b, 1, 0)


@jax.custom_vjp
def _nt(a, b):
    return _dg(a, b, 1, 1)


@jax.custom_vjp
def _tn(a, b):
    return _dg(a, b, 0, 0)


_nn.defvjp(lambda a, b: (_nn(a, b), (a, b)), lambda r, ct: (_nt(ct, r[1]), _tn(r[0], ct)))
_nt.defvjp(lambda a, b: (_nt(a, b), (a, b)), lambda r, ct: (_nn(ct, r[1]), _tn(ct, r[0])))
_tn.defvjp(lambda a, b: (_tn(a, b), (a, b)), lambda r, ct: (_nt(r[1], ct), _nn(r[0], ct)))


@functools.partial(jax.custom_vjp, nondiff_argnums=(1, 2))
def _roll(x, shift, axis):
    return pltpu.roll(x, shift % x.shape[axis], axis)


_roll.defvjp(lambda x, shift, axis: (_roll(x, shift, axis), None),
             lambda shift, axis, _, ct: (_roll(ct, -shift, axis),))


@functools.partial(jax.custom_vjp, nondiff_argnums=(1, 2))
def _split(x, n, axis):
    w = x.shape[axis] // n
    return tuple(lax.slice_in_dim(x, i * w, (i + 1) * w, axis=axis) for i in range(n))


_split.defvjp(lambda x, n, axis: (_split(x, n, axis), None),
              lambda n, axis, _, cts: (jnp.concatenate(cts, axis=axis),))


@jax.custom_vjp
def _unstack(x):
    return tuple(x[i] for i in range(x.shape[0]))


_unstack.defvjp(lambda x: (_unstack(x), None), lambda _, cts: (jnp.stack(cts, axis=0),))


def _sig(x):
    return 0.5 * (jnp.tanh(0.5 * x) + 1.0)


def _gelu(x):
    return 0.5 * x * (1.0 + jnp.tanh(math.sqrt(2.0 / math.pi) * (x + 0.044715 * (x * x * x))))


def _rms(x, g, n):
    ms = jnp.sum(x * x, axis=-1, keepdims=True) * (1.0 / n)
    return x * lax.rsqrt(ms + EPS) * g


def _rope(y, cos, sa, sb, quarter):
    return y * cos + _roll(y, -quarter, 1) * sa + _roll(y, quarter, 1) * sb


def _softmax_rows(s, extra=None):
    m = jnp.max(s, axis=-1, keepdims=True)
    if extra is not None:
        m = jnp.maximum(m, extra)
    m = lax.stop_gradient(m)
    e = jnp.exp(s - m)
    den = jnp.sum(e, axis=-1, keepdims=True)
    if extra is not None:
        den = den + jnp.exp(extra - m)
    return e / den


class _A:
    def __init__(self, arr, block, imap, kind="row", first=None, gdtype=F32, gshape=None, gimap=None):
        self.arr, self.block, self.imap, self.kind, self.first = arr, block, imap, kind, first
        self.gdtype, self.gshape, self.gimap = gdtype, gshape, gimap


def _all_zero(*ids):
    return functools.reduce(jnp.logical_and, [i == 0 for i in ids])


def _par(arr):
    nd = arr.ndim
    return _A(arr, arr.shape, lambda *ids: (0,) * nd, "acc", first=_all_zero)


def _op_fwd(name, fn, grid, args, outs):
    n_in = len(args)

    def body(*refs):
        vals = [r[...].astype(F32) for r in refs[:n_in]]
        for r, v in zip(refs[n_in:], fn(*vals)):
            r[...] = v.astype(r.dtype)

    return _pcall(
        body, name=name, grid=grid,
        in_specs=[pl.BlockSpec(a.block, a.imap) for a in args],
        out_specs=[pl.BlockSpec(o[2], o[3]) for o in outs],
        out_shape=[jax.ShapeDtypeStruct(o[0], o[1]) for o in outs],
        compiler_params=_cparams(len(grid)),
    )(*[a.arr for a in args])


def _op_bwd(name, fn, grid, args, outs, ct_arrays, add_to_first=None):
    didx = [i for i, a in enumerate(args) if a.kind not in ("const", "fwd")]
    read = [i for i, a in enumerate(args) if a.kind != "fwd"]
    n_in, n_ct = len(read), len(outs)
    n_add = 0 if add_to_first is None else 1

    def body(*refs):
        ids = [pl.program_id(i) for i in range(len(grid))]
        vals = [jnp.zeros([d for d in a.block if d is not None], F32) for a in args]
        for i, r in zip(read, refs[:n_in]):
            vals[i] = r[...].astype(F32)

        def g(*dv):
            full = list(vals)
            for i, v in zip(didx, dv):
                full[i] = v
            return tuple(fn(*full))

        _, vjp = jax.vjp(g, *[vals[i] for i in didx])
        grads = list(vjp(tuple(r[...].astype(F32) for r in refs[n_in:n_in + n_ct])))
        if n_add:
            grads[0] = grads[0] + refs[n_in + n_ct][...]
        for gr, i, r in zip(grads, didx, refs[n_in + n_ct + n_add:]):
            a = args[i]
            if a.kind == "row":
                r[...] = gr.astype(r.dtype)
            else:
                first = a.first(*ids)

                @pl.when(first)
                def _():
                    r[...] = gr

                @pl.when(jnp.logical_not(first))
                def _():
                    r[...] += gr

    g_specs, g_shapes = [], []
    for i in didx:
        a = args[i]
        if a.kind == "row":
            g_specs.append(pl.BlockSpec(a.block, a.gimap or a.imap))
            g_shapes.append(jax.ShapeDtypeStruct(a.gshape or a.arr.shape, a.gdtype))
        else:
            g_specs.append(pl.BlockSpec(a.block, a.imap))
            g_shapes.append(jax.ShapeDtypeStruct(a.arr.shape, F32))
    return _pcall(
        body, name=name, grid=grid,
        in_specs=[pl.BlockSpec(args[i].block, args[i].imap) for i in read] + [pl.BlockSpec(o[2], o[3]) for o in outs]
        + g_specs[:n_add],
        out_specs=g_specs, out_shape=g_shapes,
        compiler_params=_cparams(len(grid)),
    )(*[args[i].arr for i in read], *ct_arrays, *([add_to_first] if n_add else []))


def _rowop(name, fn, grid, args, outs):
    res = _op_fwd(name, fn, grid, args, outs)
    return res, lambda *cts, add_to_first=None: _op_bwd(name + "_bwd", fn, grid, args, outs, cts, add_to_first)


def _pick(n, cap):
    best = None
    for t in range(LANE, cap + 1, LANE):
        if n % t == 0:
            best = t
    return best or n


def _mm(a, b, mode, out_dtype, name, epi=None, aux=None):
    if mode == "nn":
        (m, k), n = a.shape, b.shape[1]
    elif mode == "nt":
        (m, k), n = a.shape, b.shape[0]
    else:
        (k, m), n = a.shape, b.shape[1]
    tm = 512 if m % 512 == 0 else m
    tn, tk = _pick(n, 1024), _pick(k, MM_K_CAP)
    nk = k // tk
    if mode == "tn":
        a_spec = pl.BlockSpec((tk, tm), lambda j, i, kk: (kk, i))
    else:
        a_spec = pl.BlockSpec((tm, tk), lambda j, i, kk: (i, kk))
    if mode == "nt":
        b_spec = pl.BlockSpec((tn, tk), lambda j, i, kk: (j, kk))
    else:
        b_spec = pl.BlockSpec((tk, tn), lambda j, i, kk: (kk, j))
    dims = {"nn": (1, 0), "nt": (1, 1), "tn": (0, 0)}[mode]
    o_spec = pl.BlockSpec((tm, tn), lambda j, i, kk: (i, j))
    n_aux = 0 if aux is None else 1
    n_out = 2 if epi == "sqrelu" else 1

    def body(*refs):
        a_ref, b_ref = refs[0], refs[1]
        o_refs = refs[2 + n_aux:2 + n_aux + n_out]
        acc = refs[-1]
        kk = pl.program_id(2)
        part = _dg(a_ref[...], b_ref[...], *dims)

        if nk > 1:
            @pl.when(kk == 0)
            def _():
                acc[...] = part

            @pl.when((kk > 0) & (kk < nk - 1))
            def _():
                acc[...] += part

        @pl.when(kk == nk - 1)
        def _():
            r = part if nk == 1 else acc[...] + part
            if epi == "sqrelu":
                o_refs[0][...] = r.astype(o_refs[0].dtype)
                rl = jnp.maximum(r, 0.0)
                o_refs[1][...] = (rl * rl).astype(o_refs[1].dtype)
            elif epi == "dsqrelu":
                pre = refs[2][...].astype(F32)
                o_refs[0][...] = (r * (2.0 * jnp.maximum(pre, 0.0))).astype(o_refs[0].dtype)
            else:
                o_refs[0][...] = r.astype(o_refs[0].dtype)

    res = _pcall(
        body, name=name, grid=(n // tn, m // tm, nk),
        in_specs=[a_spec, b_spec] + [o_spec] * n_aux,
        out_specs=[o_spec] * n_out,
        out_shape=[jax.ShapeDtypeStruct((m, n), out_dtype)] * n_out,
        scratch_shapes=[pltpu.VMEM((tm, tn), F32)],
        compiler_params=_cparams(3),
    )(a, b, *([aux] if aux is not None else []))
    return res if n_out == 2 else res[0]


ROW_CHUNK = 16


def _softmax_chunks(s_scr, n_keys, scale, emit):
    for r0 in range(0, s_scr.shape[0], ROW_CHUNK):
        rows = slice(r0, r0 + ROW_CHUNK)
        s = s_scr[rows, :n_keys]
        e = jnp.exp((s - jnp.max(s, axis=-1, keepdims=True)) * scale)
        emit(rows, e, 1.0 / jnp.sum(e, axis=-1, keepdims=True))


def _attn_fwd_block(q, k, v, scale, s_scr, e_scr, l_scr):
    n = k.shape[0]
    s_scr[:, :n] = _dg(q, k, 1, 1)

    def emit(rows, e, inv_l):
        e_scr[rows, :n] = e.astype(BF16)
        l_scr[rows, :] = jnp.broadcast_to(inv_l, (ROW_CHUNK, LANE))

    _softmax_chunks(s_scr, n, scale, emit)
    return _dg(e_scr[:, :n], v, 1, 0) * l_scr[...]


def _attn_bwd_block(q, k, v, o, do, scale, s_scr, dp_scr, p_scr, ds_scr):
    n = k.shape[0]
    s_scr[:, :n] = _dg(q, k, 1, 1)
    dp_scr[:, :n] = _dg(do, v, 1, 1)

    def emit(rows, e, inv_l):
        p = e * inv_l
        delta = jnp.sum(do[rows, :] * o[rows, :], axis=-1, keepdims=True)
        p_scr[rows, :n] = p.astype(BF16)
        ds_scr[rows, :n] = (p * (dp_scr[rows, :n] - delta) * scale).astype(BF16)

    _softmax_chunks(s_scr, n, scale, emit)
    ds = ds_scr[:, :n]
    return _dg(ds, k, 1, 0), _dg(ds, q, 0, 0), _dg(p_scr[:, :n], do, 0, 0)


def _call_with_exchange(body, xchg, *, name, grid, in_specs, out_specs, out_shape, operands, scratch_shapes=()):
    if xchg is None:
        res = _pcall(body, name=name, grid=grid, in_specs=in_specs, out_specs=out_specs, out_shape=out_shape,
                     scratch_shapes=list(scratch_shapes), compiler_params=_cparams(len(grid)))(*operands)
        return list(res), []
    n_in, n_out, n_sc, n = len(in_specs), len(out_specs), len(scratch_shapes), xchg.n

    def wrapped(*refs):
        ins, x_refs = refs[:n_in], refs[n_in:n_in + n]
        outs, xo_refs = refs[n_in + n:n_in + n + n_out], refs[n_in + n + n_out:n_in + 2 * n + n_out]
        scratch, sems = refs[n_in + 2 * n + n_out:n_in + 2 * n + n_out + n_sc], refs[n_in + 2 * n + n_out + n_sc:]
        ids = [pl.program_id(i) for i in range(len(grid))]

        @pl.when(functools.reduce(jnp.logical_and, [i == 0 for i in ids]))
        def _():
            xchg.start(x_refs, xo_refs, sems)

        body(*ins, *outs, *scratch)

        @pl.when(functools.reduce(jnp.logical_and, [i == g - 1 for i, g in zip(ids, grid)]))
        def _():
            xchg.wait(x_refs, xo_refs, sems)

    res = _pcall(wrapped, name=name, grid=grid, in_specs=list(in_specs) + xchg.specs,
                 out_specs=list(out_specs) + xchg.specs, out_shape=list(out_shape) + xchg.out_shape,
                 scratch_shapes=list(scratch_shapes) + xchg.scratch, compiler_params=_cparams(len(grid)),
                 )(*operands, *xchg.bufs)
    return list(res[:n_out]), list(res[n_out:])


def _head_half(i, shape):
    lane = lax.broadcasted_iota(jnp.int32, shape, len(shape) - 1)
    return (lane < LANE // 2) if i == 0 else (lane >= LANE // 2)


def _mla_attn(q, k, v, tc, ctx_q, name, xchg=None):
    assert MLA_HPS == 2 and MLA_V == LANE // 2
    bsz, t_all, _ = q.shape
    n_t = t_all // TB
    grid = (bsz, MLA_HEADS // MLA_HPS, n_t)
    q_spec = pl.BlockSpec((None, TB, MLA_HPS * LANE), lambda b, h, t: (b, t, h))
    k_spec = pl.BlockSpec((None, t_all, MLA_HPS * LANE), lambda b, h, t: (b, 0, h))
    v_spec = pl.BlockSpec((None, t_all, LANE), lambda b, h, t: (b, 0, h))
    o_spec = pl.BlockSpec((None, TB, LANE), lambda b, h, t: (b, t, h))
    heads = [slice(i * LANE, (i + 1) * LANE) for i in range(MLA_HPS)]
    scale = MLA_QK ** -0.5
    f32_scr, bf16_scr = pltpu.VMEM((TB, t_all), F32), pltpu.VMEM((TB, t_all), BF16)
    o_shape = jax.ShapeDtypeStruct(v.shape, F32)

    def fwd_body(q_ref, k_ref, v_ref, o_ref, *scr):
        t = pl.program_id(2)

        def run(keys):
            both = [_attn_fwd_block(q_ref[:, hs], k_ref[keys, hs], v_ref[keys, :], scale, *scr[3 * i:3 * i + 3])
                    for i, hs in enumerate(heads)]
            o_ref[...] = jnp.where(_head_half(0, both[0].shape), both[0], both[1])

        @pl.when(t == 0)
        def _():
            if ctx_q:
                run(slice(0, tc))
            else:
                o_ref[...] = jnp.zeros_like(o_ref)

        @pl.when(t > 0)
        def _():
            run(slice(0, t_all))

    (o,), gathered = _call_with_exchange(
        fwd_body, xchg, name=name, grid=grid, in_specs=[q_spec, k_spec, v_spec], out_specs=[o_spec],
        out_shape=[o_shape], operands=(q, k, v),
        scratch_shapes=[f32_scr, bf16_scr, pltpu.VMEM((TB, LANE), F32)] * MLA_HPS)

    def bwd(do, xchg=None):
        def bwd_body(q_ref, k_ref, v_ref, o_ref, do_ref, dq_ref, dk_ref, dv_ref, *scr):
            t = pl.program_id(2)

            def run(keys, first):
                dvs = []
                for i, hs in enumerate(heads):
                    do_i = jnp.where(_head_half(i, do_ref.shape), do_ref[...], 0.0)
                    dq, dk, dv = _attn_bwd_block(q_ref[:, hs], k_ref[keys, hs], v_ref[keys, :], o_ref[...], do_i,
                                                 scale, *scr[4 * i:4 * i + 4])
                    dq_ref[:, hs] = dq
                    dvs.append(dv)
                    if first:
                        dk_ref[keys, hs] = dk
                    else:
                        dk_ref[keys, hs] += dk
                if first:
                    dv_ref[keys, :] = dvs[0] + dvs[1]
                else:
                    dv_ref[keys, :] += dvs[0] + dvs[1]

            @pl.when(t == 0)
            def _():
                dk_ref[...] = jnp.zeros_like(dk_ref)
                dv_ref[...] = jnp.zeros_like(dv_ref)
                if ctx_q:
                    run(slice(0, tc), True)
                else:
                    dq_ref[...] = jnp.zeros_like(dq_ref)

            @pl.when(t > 0)
            def _():
                run(slice(0, t_all), False)

        return _call_with_exchange(
            bwd_body, xchg, name=name + "_bwd", grid=grid, in_specs=[q_spec, k_spec, v_spec, o_spec, o_spec],
            out_specs=[q_spec, k_spec, v_spec],
            out_shape=[jax.ShapeDtypeStruct(q.shape, F32), jax.ShapeDtypeStruct(q.shape, F32), o_shape],
            operands=(q, k, v, o, do), scratch_shapes=[f32_scr, f32_scr, bf16_scr, bf16_scr] * MLA_HPS)

    return o, gathered, bwd


def _swa_block(q, keys, vals, sink, mask):
    qs = jnp.concatenate(list(_split(q, SWA_GROUP, 1)), axis=0)
    sk = jnp.sum(sink, axis=-1, keepdims=True) * (1.0 / LANE)
    s = _nt(qs, keys) * (SWA_HEAD_DIM ** -0.5)
    if mask is not None:
        s = jnp.where(mask, s, NEG_INF)
    o = _split(_nn(_softmax_rows(s, sk), vals + _roll(vals, LANE // 2, 1)), SWA_GROUP, 0)
    low = _head_half(0, o[0].shape)
    return jnp.concatenate([jnp.where(low, o[0], o[1]), jnp.where(low, o[2], o[3])], axis=1)


def _swa_ctx_block(q, kc, vc, sink):
    return _swa_block(q, kc, vc, sink, None)


def _swa_win_block(q, kc, kw, vc, vw, sink, mask):
    return _swa_block(q, jnp.concatenate([kc, kw], axis=0), jnp.concatenate([vc, vw], axis=0), sink, mask)


def _swa_attn(q, k, p_all, sink_b, tc, ctx_q, name, xchg=None):
    bsz, t_all, _ = q.shape
    n_q = t_all // QB_SWA
    n_cq = tc // QB_SWA
    lat = t_all - tc
    span = QB_SWA + 2 * WINDOW
    gw = SWA_GROUP * LANE
    grid = (bsz, SWA_KV_HEADS, n_q)
    q_spec = pl.BlockSpec((None, QB_SWA, gw), lambda b, g, i: (b, i, g))
    k_spec = pl.BlockSpec((None, t_all, LANE), lambda b, g, i: (b, 0, g))
    v_spec = pl.BlockSpec((None, t_all, LANE), lambda b, g, i: (b, 0, PC_SV // LANE + g))
    s_spec = pl.BlockSpec((None, SWA_GROUP * QB_SWA, LANE), lambda b, g, i: (g, 0, 0))

    def window(i):
        q0 = (i - n_cq) * QB_SWA
        w0 = jnp.clip(q0 - WINDOW, 0, lat - span)
        w0 = pl.multiple_of(w0, QB_SWA)
        shape = (SWA_GROUP * QB_SWA, tc + span)
        qi = q0 + lax.broadcasted_iota(jnp.int32, shape, 0) % QB_SWA
        col = lax.broadcasted_iota(jnp.int32, shape, 1)
        kj = w0 + col - tc
        mask = (col < tc) | ((kj >= qi - WINDOW) & (kj <= qi + WINDOW))
        return w0, mask

    def fwd_body(q_ref, k_ref, v_ref, s_ref, o_ref):
        i = pl.program_id(2)

        @pl.when(i < n_cq)
        def _():
            if ctx_q:
                o_ref[...] = _swa_ctx_block(q_ref[...].astype(F32), k_ref[0:tc, :], v_ref[0:tc, :], s_ref[...])
            else:
                o_ref[...] = jnp.zeros_like(o_ref)

        @pl.when(i >= n_cq)
        def _():
            w0, mask = window(i)
            o_ref[...] = _swa_win_block(q_ref[...].astype(F32), k_ref[0:tc, :], k_ref[pl.ds(tc + w0, span), :],
                                        v_ref[0:tc, :], v_ref[pl.ds(tc + w0, span), :], s_ref[...], mask)

    o_spec = pl.BlockSpec((None, QB_SWA, SWA_GROUP * SWA_HEAD_DIM), lambda b, g, i: (b, i, g))
    (o,), gathered = _call_with_exchange(
        fwd_body, xchg, name=name, grid=grid, in_specs=[q_spec, k_spec, v_spec, s_spec], out_specs=[o_spec],
        out_shape=[jax.ShapeDtypeStruct((bsz, t_all, SWA_HEADS * SWA_HEAD_DIM), F32)],
        operands=(q, k, p_all, sink_b))

    def bwd(do, xchg=None):
        def bwd_body(q_ref, k_ref, v_ref, s_ref, do_ref, dq_ref, dk_ref, dv_ref, ds_ref):
            i = pl.program_id(2)

            @pl.when(i == 0)
            def _():
                dk_ref[...] = jnp.zeros_like(dk_ref)
                dv_ref[...] = jnp.zeros_like(dv_ref)
                ds_ref[...] = jnp.zeros_like(ds_ref)

            @pl.when(i < n_cq)
            def _():
                if ctx_q:
                    _, vjp = jax.vjp(_swa_ctx_block, q_ref[...].astype(F32), k_ref[0:tc, :].astype(F32),
                                     v_ref[0:tc, :], s_ref[...])
                    dq, dk, dv, ds = vjp(do_ref[...])
                    dq_ref[...] = dq
                    dk_ref[0:tc, :] += dk
                    dv_ref[0:tc, :] += dv
                    ds_ref[...] += ds
                else:
                    dq_ref[...] = jnp.zeros_like(dq_ref)

            @pl.when(i >= n_cq)
            def _():
                w0, mask = window(i)
                win = pl.ds(tc + w0, span)
                _, vjp = jax.vjp(functools.partial(_swa_win_block, mask=mask), q_ref[...].astype(F32),
                                 k_ref[0:tc, :].astype(F32), k_ref[win, :].astype(F32),
                                 v_ref[0:tc, :], v_ref[win, :], s_ref[...])
                dq, dkc, dkw, dvc, dvw, ds = vjp(do_ref[...])
                dq_ref[...] = dq
                dk_ref[0:tc, :] += dkc
                dk_ref[win, :] += dkw
                dv_ref[0:tc, :] += dvc
                dv_ref[win, :] += dvw
                ds_ref[...] += ds

        kv_out = pl.BlockSpec((None, t_all, LANE), lambda b, g, i: (b, 0, g))
        ds_spec = pl.BlockSpec((None, None, SWA_GROUP * QB_SWA, LANE), lambda b, g, i: (b, g, 0, 0))
        kv_shape = jax.ShapeDtypeStruct((bsz, t_all, SWA_KV_HEADS * LANE), F32)
        return _call_with_exchange(
            bwd_body, xchg, name=name + "_bwd", grid=grid, in_specs=[q_spec, k_spec, v_spec, s_spec, o_spec],
            out_specs=[q_spec, kv_out, kv_out, ds_spec],
            out_shape=[jax.ShapeDtypeStruct(q.shape, F32), kv_shape, kv_shape,
                       jax.ShapeDtypeStruct((bsz,) + sink_b.shape, F32)],
            operands=(q, k, p_all, sink_b, do))

    return o, gathered, bwd


def _scan_rows(a, u, reverse, a_s, u_s, c_s):
    t_all, c = a.shape
    row8 = lax.broadcasted_iota(jnp.int32, a.shape, 0) % SUBLANE
    for d in (1, 2, 4):
        sh = d if not reverse else t_all - d
        ar, ur = pltpu.roll(a, sh, 0), pltpu.roll(u, sh, 0)
        m = (row8 >= d) if not reverse else (row8 < SUBLANE - d)
        u = jnp.where(m, a * ur + u, u)
        a = jnp.where(m, a * ar, a)
    a_s[...] = a
    u_s[...] = u
    n_tiles = t_all // SUBLANE

    def step(j, carry):
        tile = j if not reverse else n_tiles - 1 - j
        base = pl.multiple_of(tile * SUBLANE, SUBLANE)
        c_s[pl.ds(base, SUBLANE), :] = jnp.broadcast_to(carry, (SUBLANE, c))
        last = base + (0 if reverse else SUBLANE - 1)
        return a_s[pl.ds(last, 1), :] * carry + u_s[pl.ds(last, 1), :]

    lax.fori_loop(0, n_tiles, step, jnp.zeros((1, c), F32))
    return a_s[...] * c_s[...] + u_s[...]


def _shift_rows(x, reverse_src):
    t_all = x.shape[0]
    row = lax.broadcasted_iota(jnp.int32, x.shape, 0)
    if reverse_src:
        return jnp.where(row == t_all - 1, 0.0, pltpu.roll(x, t_all - 1, 0))
    return jnp.where(row == 0, 0.0, pltpu.roll(x, 1, 0))


def _lru_scan(a0, u0, a1, u1, name):
    bsz, t_all, w = a0.shape
    grid = (bsz, w // LANE)
    spec = pl.BlockSpec((None, t_all, LANE), lambda b, c: (b, 0, c))
    scratch = [pltpu.VMEM((t_all, LANE), F32)] * 3
    shape = jax.ShapeDtypeStruct(a0.shape, F32)

    def fwd_body(a0_ref, u0_ref, a1_ref, u1_ref, h0_ref, h1_ref, a_s, u_s, c_s):
        h0_ref[...] = _scan_rows(a0_ref[...], u0_ref[...], False, a_s, u_s, c_s)
        h1_ref[...] = _scan_rows(a1_ref[...], u1_ref[...], True, a_s, u_s, c_s)

    h0, h1 = _pcall(fwd_body, name=name, grid=grid, in_specs=[spec] * 4, out_specs=[spec] * 2,
                    out_shape=[shape] * 2, scratch_shapes=scratch, compiler_params=_cparams(2))(a0, u0, a1, u1)

    def bwd(dh0, dh1):
        def bwd_body(a0_ref, h0_ref, g0_ref, a1_ref, h1_ref, g1_ref, da0_ref, du0_ref, da1_ref, du1_ref,
                     a_s, u_s, c_s):
            g0 = _scan_rows(_shift_rows(a0_ref[...], True), g0_ref[...], True, a_s, u_s, c_s)
            du0_ref[...] = g0
            da0_ref[...] = g0 * _shift_rows(h0_ref[...], False)
            g1 = _scan_rows(_shift_rows(a1_ref[...], False), g1_ref[...], False, a_s, u_s, c_s)
            du1_ref[...] = g1
            da1_ref[...] = g1 * _shift_rows(h1_ref[...], True)

        return _pcall(bwd_body, name=name + "_bwd", grid=grid, in_specs=[spec] * 6, out_specs=[spec] * 4,
                      out_shape=[shape] * 4, scratch_shapes=scratch,
                      compiler_params=_cparams(2))(a0, h0, dh0, a1, h1, dh1)

    return h0, h1, bwd


def _f_mod(x, g, shift, scale):
    return (_rms(x, g, D_MODEL) * (1.0 + scale) + shift,)


def _f_mla_q(cq, ga, w, gh, cos, sa, sb):
    n = _rms(cq, ga, MLA_Q_RANK)
    outs = []
    for wh in _split(w, MLA_HEADS, 1):
        outs.append(_rope(_rms(_nn(n, wh), gh, MLA_QK), cos, sa, sb, MLA_ROPE // 4))
    return (jnp.concatenate(outs, axis=1),)


def _f_mla_kv(ckv, krp, ga, wk, wv, gh, cos, sa, sb):
    n = _rms(ckv, ga, MLA_KV_RANK)
    outs = []
    for wh in _split(wk, MLA_HEADS, 1):
        outs.append(_rope(_rms(_nn(n, wh) + krp, gh, MLA_QK), cos, sa, sb, MLA_ROPE // 4))
    return jnp.concatenate(outs, axis=1), _nn(n, wv)


def _f_conv(x, w0, w1, w2, w3, bias, tc):
    t_all = x.shape[0]
    row = lax.broadcasted_iota(jnp.int32, x.shape, 0)
    lo = jnp.where(row < tc, 0, tc)
    hi = jnp.where(row < tc, tc, t_all)
    y = bias + jnp.zeros_like(x)
    for kk, wk in enumerate((w0, w1, w2, w3)):
        src = row + (kk - 2)
        xs = x if kk == 2 else _roll(x, 2 - kk, 0)
        y = y + wk * jnp.where((src >= lo) & (src < hi), xs, 0.0)
    return (y,)


def _f_gates(xc, w16, b00, b01, b10, b11, sp0, sp1):
    ws = _unstack(w16)
    n_cb = LRU_WIDTH // LANE
    xcs = _split(xc, n_cb, 1)
    bias = [_split(b, n_cb, 1) for b in (b00, b01, b10, b11)]
    sps = [_split(s, n_cb, 1) for s in (sp0, sp1)]
    res = [[], [], [], []]
    for c in range(n_cb):
        for z in range(2):
            r = _sig(_nn(xcs[c], ws[c * 4 + 2 * z]) + bias[2 * z][c])
            i = _sig(_nn(xcs[c], ws[c * 4 + 2 * z + 1]) + bias[2 * z + 1][c])
            la = -LRU_C * r * sps[z][c]
            res[2 * z].append(jnp.exp(la))
            res[2 * z + 1].append(jnp.sqrt(-jnp.tanh(la) * (jnp.exp(2.0 * la) + 1.0)) * (i * xcs[c]))
    return tuple(jnp.concatenate(r, axis=1) for r in res)


def _f_swa_qk(sq, sk, gq, gk, cos, sa, sb):
    qs = [_rope(_rms(x, gq, SWA_HEAD_DIM), cos, sa, sb, SWA_HEAD_DIM // 4) for x in _split(sq, SWA_HEADS, 1)]
    ks = [_rope(_rms(x, gk, SWA_HEAD_DIM), cos, sa, sb, SWA_HEAD_DIM // 4) for x in _split(sk, SWA_KV_HEADS, 1)]
    return jnp.concatenate(qs, axis=1), jnp.concatenate(ks, axis=1)


def _f_qkv(cq, ckv, krp, sq, sk, q_a_g, wuq, mla_q_g, kv_a_g, wk, wv, mla_k_g, swa_q_g, swa_k_g,
           m_cos, m_sa, m_sb, s_cos, s_sa, s_sb):
    return (*_f_mla_q(cq, q_a_g, wuq, mla_q_g, m_cos, m_sa, m_sb),
            *_f_mla_kv(ckv, krp, kv_a_g, wk, wv, mla_k_g, m_cos, m_sa, m_sb),
            *_f_swa_qk(sq, sk, swa_q_g, swa_k_g, s_cos, s_sa, s_sb))


def _f_merge(oa, h0, h1, lg, oc, ga, gb, gc):
    ob = (h0 + h1) * _gelu(lg)
    return (jnp.concatenate([_rms(oa, ga, GROUP_WIDTH), _rms(ob, gb, GROUP_WIDTH), _rms(oc, gc, GROUP_WIDTH)],
                            axis=1),)


def _f_resid_mod(x, y, gate, g, shift, scale):
    x1 = x + gate * y
    return x1, _rms(x1, g, D_MODEL) * (1.0 + scale) + shift


def _f_resid(x, y, gate):
    return (x + gate * y,)


def _hosted(hooks, key, arg=None):
    make, done = hooks.get(key, (None, None))
    xchg = make(arg) if make is not None else None
    return xchg, (done if xchg is not None else lambda outs: None)


def _layer(li, x, mods, w, s, tabs, tc, ctx_q, hooks):
    bsz, t_all, _ = x.shape
    n_t = t_all // TB
    grid = (bsz, n_t)
    rows = lambda b, t: (b, t, 0)

    def row(arr, width=None, idx=0, gdtype=F32, gshape=None):
        width = width or arr.shape[-1]
        return _A(arr, (None, TB, width), lambda b, t: (b, t, idx), "row", gdtype=gdtype, gshape=gshape,
                  gimap=rows if gshape is not None else None)

    def out(width, dtype, imap=rows):
        return ((bsz, t_all, width), dtype, (None, TB, width), imap)

    def modarg(arr):
        return _A(arr, (None, None, 1, D_MODEL), lambda b, t: (b, jnp.minimum(t, 1), 0, 0), "acc",
                  first=lambda b, t: t <= 1)

    def tab(arr):
        return _A(arr, (TB, LANE), lambda b, t: (t, 0), "const")

    def pcol(p_all, col, width):
        return row(p_all, width, col // width, gdtype=BF16, gshape=(bsz, t_all, width))

    nm = lambda base: "%s_l%d" % (base, li)
    sh1, sc1, g1, sh2, sc2, g2 = mods
    m_all = bsz * t_all

    (h,), b_mod1 = _rowop(nm("mod1"), _f_mod, grid, [row(x), _par(s["norm1_g"]), modarg(sh1), modarg(sc1)],
                          [out(D_MODEL, BF16)])
    p_all = _mm(h.reshape(m_all, D_MODEL), w["win"], "nn", F32, nm("mm_in")).reshape(bsz, t_all, P_WIDTH)

    (q_a, k_a, v_a, q_c, k_c), b_qkv = _rowop(
        nm("qkv"), _f_qkv, grid,
        [pcol(p_all, PC_CQ, 256), pcol(p_all, PC_CKV, 128), pcol(p_all, PC_KR, 128), pcol(p_all, PC_SQ, 1024),
         pcol(p_all, PC_SK, 256)]
        + [_par(a) for a in (s["q_a_g"], w["wuq"], s["mla_q_g"], s["kv_a_g"], w["wk"], w["wv"], s["mla_k_g"],
                             s["swa_q_g"], s["swa_k_g"])]
        + [tab(a) for a in tabs["mla"] + tabs["swa"]],
        [out(MLA_HEADS * LANE, BF16), out(MLA_HEADS * LANE, BF16), out(MLA_HEADS * MLA_V, BF16),
         out(SWA_HEADS * LANE, BF16), out(SWA_KV_HEADS * LANE, BF16)])

    xchg, done = _hosted(hooks, "mla_fwd")
    o_a, got, b_attn_a = _mla_attn(q_a, k_a, v_a, tc, ctx_q, nm("mla_attn"), xchg)
    done(got)

    n_cb = LRU_WIDTH // LANE
    conv_grid = (n_cb, bsz)
    cpar = lambda arr: _A(arr, (1, LANE), lambda c, b: (0, c), "acc", first=lambda c, b: b == 0)
    conv_args = [_A(p_all, (None, t_all, LANE), lambda c, b: (b, 0, PC_LX // LANE + c), "row", gdtype=BF16,
                    gshape=(bsz, t_all, LRU_WIDTH), gimap=lambda c, b: (b, 0, c))]
    conv_args += [cpar(a) for a in s["conv_w"]] + [cpar(s["conv_b"])]
    conv_out = [((bsz, t_all, LRU_WIDTH), F32, (None, t_all, LANE), lambda c, b: (b, 0, c))]
    (xc,), b_conv = _rowop(nm("lru_conv"), functools.partial(_f_conv, tc=tc), conv_grid, conv_args, conv_out)
    rot = lambda b, t: (b, (t + n_t - 1) % n_t, 0)
    (a0, u0, a1, u1), b_gates = _rowop(
        nm("lru_gates"), _f_gates, grid,
        [row(xc), _par(s["wbd"])] + [_par(a) for a in s["gate_b"]] + [_par(a) for a in s["sp"]],
        [out(LRU_WIDTH, F32), out(LRU_WIDTH, F32), out(LRU_WIDTH, F32, rot), out(LRU_WIDTH, F32, rot)])
    h0, h1, b_scan = _lru_scan(a0, u0, a1, u1, nm("lru_scan"))
    h1_arg = _A(h1, (None, TB, LRU_WIDTH), rot, "row")

    xchg, done = _hosted(hooks, "swa_fwd")
    o_c, got, b_attn_c = _swa_attn(q_c, k_c, p_all, s["sink_b"], tc, ctx_q, nm("swa_attn"), xchg)
    done(got)

    (y_in,), b_merge = _rowop(nm("merge"), _f_merge, grid,
                              [row(o_a), row(h0), h1_arg, pcol(p_all, PC_LG, 512), row(o_c), _par(s["g_a"]),
                               _par(s["g_b"]), _par(s["g_c"])],
                              [out(MIX_P, BF16)])
    y = _mm(y_in.reshape(m_all, MIX_P), w["wout"], "nn", F32, nm("mm_out")).reshape(bsz, t_all, D_MODEL)
    (x1, hm), b_rm = _rowop(nm("resid_mod"), _f_resid_mod, grid,
                            [row(x), row(y, gdtype=BF16), modarg(g1), _par(s["norm2_g"]), modarg(sh2), modarg(sc2)],
                            [out(D_MODEL, F32), out(D_MODEL, BF16)])
    pre, act = _mm(hm.reshape(m_all, D_MODEL), w["ff1"], "nn", BF16, nm("mm_ff1"), epi="sqrelu")
    y2 = _mm(act, w["ff2"], "nn", F32, nm("mm_ff2")).reshape(bsz, t_all, D_MODEL)
    (x2,), b_res = _rowop(nm("resid"), _f_resid, grid,
                          [_A(x1, (None, TB, D_MODEL), rows, "fwd"), row(y2, gdtype=BF16), modarg(g2)],
                          [out(D_MODEL, F32)])

    def bwd(dx2, hooks):
        dw, ds = {}, {}
        dy2, dg2 = b_res(dx2)
        dy2 = dy2.reshape(m_all, D_MODEL)
        dpre = _mm(dy2, w["ff2"], "nt", BF16, nm("mm_ff2_dx"), epi="dsqrelu", aux=pre)
        dw["ff2"] = _mm(act, dy2, "tn", BF16, nm("mm_ff2_dw"))
        dhm = _mm(dpre, w["ff1"], "nt", F32, nm("mm_ff1_dx")).reshape(bsz, t_all, D_MODEL)
        dw["ff1"] = _mm(hm.reshape(m_all, D_MODEL), dpre, "tn", BF16, nm("mm_ff1_dw"))
        dxa, dy, dg1, ds["norm2_g"], dsh2, dsc2 = b_rm(dx2, dhm)
        dy = dy.reshape(m_all, D_MODEL)
        dy_in = _mm(dy, w["wout"], "nt", F32, nm("mm_out_dx")).reshape(bsz, t_all, MIX_P)
        dw["wout"] = _mm(y_in.reshape(m_all, MIX_P), dy, "tn", BF16, nm("mm_out_dw"))
        do_a, dh0, dh1, dlg, do_c, ds["g_a"], ds["g_b"], ds["g_c"] = b_merge(dy_in)

        (dq_c, dk_c, dsv, dsink), _ = b_attn_c(do_c)
        ds["sink_b"] = jnp.sum(dsink, axis=0)

        da0, du0, da1, du1 = b_scan(dh0, dh1)
        gates_g = b_gates(da0, du0, da1, du1)
        dxc, ds["wbd"] = gates_g[0], gates_g[1]
        ds["gate_b"], ds["sp"] = list(gates_g[2:6]), list(gates_g[6:8])
        conv_g = b_conv(dxc)
        dlx, ds["conv_w"], ds["conv_b"] = conv_g[0], list(conv_g[1:5]), conv_g[5]

        xchg, done = _hosted(hooks, "mla_bwd", dw)
        (dq_a, dk_a, dv_a), got = b_attn_a(do_a, xchg)
        done(got)
        (dcq, dckv, dkr, dsq, dsk, ds["q_a_g"], dw["wuq"], ds["mla_q_g"], ds["kv_a_g"], dw["wk"], dw["wv"],
         ds["mla_k_g"], ds["swa_q_g"], ds["swa_k_g"]) = b_qkv(dq_a, dk_a, dv_a, dq_c, dk_c)

        dp = jnp.concatenate([dsq, dlx, dlg, dcq, dsk, dsv.astype(BF16), dckv, dkr], axis=-1)
        dp = dp.reshape(m_all, P_WIDTH)
        dh = _mm(dp, w["win"], "nt", F32, nm("mm_in_dx")).reshape(bsz, t_all, D_MODEL)
        dw["win"] = _mm(h.reshape(m_all, D_MODEL), dp, "tn", BF16, nm("mm_in_dw"))
        dx, ds["norm1_g"], dsh1, dsc1 = b_mod1(dh, add_to_first=dxa)
        return dx, [dsh1, dsc1, dg1, dsh2, dsc2, dg2], dw, ds

    return x2, bwd


def _loss_and_grad(x2, target, tc):
    bsz, t_all, d = x2.shape
    n_t = t_all // TB
    n_c = tc // TB

    def body(x_ref, t_ref, l_ref, dx_ref):
        b, t = pl.program_id(0), pl.program_id(1)

        @pl.when((b == 0) & (t == 0))
        def _():
            l_ref[...] = jnp.zeros_like(l_ref)

        @pl.when(t < n_c)
        def _():
            dx_ref[...] = jnp.zeros_like(dx_ref)

        @pl.when(t >= n_c)
        def _():
            e = x_ref[...] - t_ref[...]
            dx_ref[...] = e * (1.0 / d)
            l_ref[...] += jnp.sum(e * e) * (0.5 / d)

    loss, dx = _pcall(
        body, name="loss", grid=(bsz, n_t),
        in_specs=[pl.BlockSpec((None, TB, d), lambda b, t: (b, t, 0)),
                  pl.BlockSpec((None, TB, d), lambda b, t: (b, jnp.maximum(t - n_c, 0), 0))],
        out_specs=[pl.BlockSpec((SUBLANE, LANE), lambda b, t: (0, 0)),
                   pl.BlockSpec((None, TB, d), lambda b, t: (b, t, 0))],
        out_shape=[jax.ShapeDtypeStruct((SUBLANE, LANE), F32), jax.ShapeDtypeStruct(x2.shape, F32)],
        compiler_params=_cparams(2))(x2, target)
    return loss[0, 0], dx


def _rope_tables(lat, tc, dim, lane0):
    quarter = dim // 4
    pos = jnp.arange(lat)
    grid_pos = jnp.stack([pos // GRID_W, pos % GRID_W], axis=-1).astype(F32)
    lane = jnp.arange(LANE)
    p = jnp.clip(lane - lane0, 0, dim - 1)
    active = (lane >= lane0) & (lane < lane0 + dim)
    axis, half, qi = p // (dim // 2), (p % (dim // 2)) // quarter, p % quarter
    inv = ROPE_THETA ** (-qi.astype(F32) / quarter)
    ang = jnp.where(axis[None, :] == 0, grid_pos[:, 0:1], grid_pos[:, 1:2]) * inv[None, :]
    cos = jnp.where(active, jnp.cos(ang), 1.0)
    sin = jnp.where(active, jnp.sin(ang), 0.0)
    sa = jnp.where(half == 0, -sin, 0.0)
    sb = jnp.where(half == 1, sin, 0.0)
    ctx1, ctx0 = jnp.ones((tc, LANE), F32), jnp.zeros((tc, LANE), F32)
    return (jnp.concatenate([ctx1, cos], 0), jnp.concatenate([ctx0, sa], 0), jnp.concatenate([ctx0, sb], 0))


_BIG = {"w_in": ((D_MODEL, IN_WIDTH // N_DEV), 1, ("win",)),
        "w_uq": ((MLA_Q_RANK, MLA_HEADS * MLA_QK // N_DEV), 1, ("wuq",)),
        "w_ukv": ((MLA_KV_RANK, MLA_HEADS * (MLA_NOPE + MLA_V) // N_DEV), 1, ("wk", "wv")),
        "w_out": ((3 * GROUP_WIDTH // N_DEV, D_MODEL), 0, ("wout",)),
        "w_ff1": ((D_MODEL, D_FF // N_DEV), 1, ("ff1",)),
        "w_ff2": ((D_FF // N_DEV, D_MODEL), 0, ("ff2",))}
_EARLY = ("w_in", "w_uq", "w_ukv")
_LATE = ("w_out", "w_ff1", "w_ff2")


def _pad_heads(wm, n_heads, dim, axis=-1):
    axis = axis % wm.ndim
    shp = wm.shape[:axis] + (n_heads, dim) + wm.shape[axis + 1:]
    pad = [(0, 0)] * len(shp)
    pad[axis + 1] = (0, LANE - dim)
    out = jnp.pad(wm.reshape(shp), pad)
    return out.reshape(wm.shape[:axis] + (n_heads * LANE,) + wm.shape[axis + 1:])


def _prep_weight(name, piece):
    shp, ax, _ = _BIG[name]
    full = jnp.moveaxis(piece, 0, ax).reshape(shp[:ax] + (N_DEV * shp[ax],) + shp[ax + 1:])
    if name == "w_in":
        cq, ckv, kr, lx, lg, sq, sk, sv = _split_cols(full)
        return {"win": jnp.concatenate(
            [_pad_heads(sq, SWA_HEADS, SWA_HEAD_DIM), lx, lg, cq, _pad_heads(sk, SWA_KV_HEADS, SWA_HEAD_DIM),
             _pad_heads(sv, SWA_KV_HEADS, SWA_HEAD_DIM), ckv, jnp.pad(kr, ((0, 0), (MLA_NOPE, LANE - MLA_QK)))], axis=1)}
    if name == "w_uq":
        return {"wuq": _pad_heads(full, MLA_HEADS, MLA_QK)}
    if name == "w_ukv":
        ukv = full.reshape(MLA_KV_RANK, MLA_HEADS, MLA_NOPE + MLA_V)
        return {"wk": _pad_heads(ukv[:, :, :MLA_NOPE].reshape(MLA_KV_RANK, -1), MLA_HEADS, MLA_NOPE),
                "wv": ukv[:, :, MLA_NOPE:].reshape(MLA_KV_RANK, -1)}
    return {_BIG[name][2][0]: full}


def _split_cols(wm):
    parts, start = [], 0
    for size in IN_SIZES:
        parts.append(wm[:, start:start + size])
        start += size
    return parts


def _prep_small(raw):
    r1 = lambda a: a.reshape(1, -1)
    gw = raw["lru_gate_w"].reshape(2, 2, 4, 2, 64, 64)
    wbd = jnp.einsum("zgknCm,nN->knCzgNm", gw, jnp.eye(2, dtype=F32)).reshape(4, LANE, 4, LANE)
    gg = raw["group_g"]
    sink = raw["swa_sink"].reshape(SWA_KV_HEADS, SWA_GROUP, 1, 1)
    return {
        "norm1_g": r1(raw["norm1_g"]), "norm2_g": r1(raw["norm2_g"]),
        "q_a_g": r1(raw["q_a_g"]), "kv_a_g": r1(raw["kv_a_g"]),
        "mla_q_g": jnp.pad(r1(raw["mla_q_g"]), ((0, 0), (0, LANE - MLA_QK))),
        "mla_k_g": jnp.pad(r1(raw["mla_k_g"]), ((0, 0), (0, LANE - MLA_QK))),
        "swa_q_g": jnp.pad(r1(raw["swa_q_g"]), ((0, 0), (0, LANE - SWA_HEAD_DIM))),
        "swa_k_g": jnp.pad(r1(raw["swa_k_g"]), ((0, 0), (0, LANE - SWA_HEAD_DIM))),
        "conv_w": [r1(raw["conv_w"][kk]) for kk in range(4)], "conv_b": r1(raw["conv_b"]),
        "wbd": wbd.transpose(0, 2, 1, 3).reshape(16, LANE, LANE),
        "gate_b": [r1(raw["lru_gate_b"][z, g]) for z in range(2) for g in range(2)],
        "sp": [r1(jax.nn.softplus(-raw["lru_lambda"][z])) for z in range(2)],
        "sink_b": jnp.broadcast_to(sink, (SWA_KV_HEADS, SWA_GROUP, QB_SWA, LANE)).reshape(
            SWA_KV_HEADS, SWA_GROUP * QB_SWA, LANE),
        "g_a": r1(gg[:GROUP_WIDTH]), "g_b": r1(gg[GROUP_WIDTH:2 * GROUP_WIDTH]), "g_c": r1(gg[2 * GROUP_WIDTH:])}


def _mesh_pos():
    return lax.axis_index("x"), lax.axis_index("y"), lax.axis_index("c")


def _peer(pos, k):
    return tuple(1 - p if (k >> s) & 1 else p for p, s in zip(pos, (2, 1, 0)))


def _dev_index(pos):
    return 4 * pos[0] + 2 * pos[1] + pos[2]


class _Exchange:
    def __init__(self, bufs, gather):
        self.bufs = list(bufs)
        self.n = len(self.bufs)
        self.gather = [gather] * self.n if isinstance(gather, bool) else list(gather)
        self.specs = [pl.BlockSpec(memory_space=pl.ANY)] * self.n
        self.out_shape = [jax.ShapeDtypeStruct((N_DEV,) + tuple(b.shape if g else b.shape[1:]), b.dtype)
                          for b, g in zip(self.bufs, self.gather)]
        self.scratch = [pltpu.SemaphoreType.DMA(((N_DEV - 1) * self.n,)),
                        pltpu.SemaphoreType.DMA(((N_DEV - 1) * self.n,)), pltpu.SemaphoreType.DMA((self.n,))]

    def _copies(self, x_refs, o_refs, sems, with_recvs):
        send_sems, recv_sems, local_sems = sems
        pos = _mesh_pos()
        me = _dev_index(pos)
        locals_, sends, recvs = [], [], []
        for j in range(self.n):
            src_mine = x_refs[j] if self.gather[j] else x_refs[j].at[me]
            locals_.append(pltpu.make_async_copy(src_mine, o_refs[j].at[me], local_sems.at[j]))
        for k in range(1, N_DEV):
            peer = _peer(pos, k)
            pidx = _dev_index(peer)
            for j in range(self.n):
                src = x_refs[j] if self.gather[j] else x_refs[j].at[pidx]
                sem = (k - 1) * self.n + j
                sends.append(pltpu.make_async_remote_copy(
                    src_ref=src, dst_ref=o_refs[j].at[me], send_sem=send_sems.at[sem], recv_sem=recv_sems.at[sem],
                    device_id=peer, device_id_type=pl.DeviceIdType.MESH))
                if with_recvs:
                    recvs.append(pltpu.make_async_remote_copy(
                        src_ref=src, dst_ref=o_refs[j].at[pidx], send_sem=send_sems.at[sem],
                        recv_sem=recv_sems.at[sem], device_id=peer, device_id_type=pl.DeviceIdType.MESH))
        return locals_, sends, recvs

    def start(self, x_refs, o_refs, sems):
        locals_, sends, _ = self._copies(x_refs, o_refs, sems, False)
        for cp in locals_ + sends:
            cp.start()

    def wait(self, x_refs, o_refs, sems):
        locals_, sends, recvs = self._copies(x_refs, o_refs, sems, True)
        for cp in recvs:
            cp.wait_recv()
        for cp in sends:
            cp.wait_send()
        for cp in locals_:
            cp.wait()


def _exchange(bufs, gather, name):
    xchg = _Exchange(bufs, gather)
    n = xchg.n

    def body(*refs):
        xchg.start(refs[:n], refs[n:2 * n], refs[2 * n:])
        xchg.wait(refs[:n], refs[n:2 * n], refs[2 * n:])

    return _pcall(body, name=name, out_shape=xchg.out_shape, in_specs=xchg.specs, out_specs=xchg.specs,
                  scratch_shapes=xchg.scratch)(*xchg.bufs)


def _pack(arrs, dtype):
    flat = jnp.concatenate([a.reshape(-1).astype(dtype) for a in arrs])
    rows = -(-flat.size // PACK_W)
    rows = -(-rows // 16) * 16
    return jnp.pad(flat, (0, rows * PACK_W - flat.size)).reshape(rows, PACK_W)


def _unpack(buf, shapes, lead=()):
    flat = buf.reshape(lead + (-1,))
    out, off = [], 0
    for shp in shapes:
        n = math.prod(shp)
        out.append(flat[..., off:off + n].reshape(lead + tuple(shp)))
        off += n
    return out


def _sum_sources(buf, name):
    _, r, c = buf.shape
    tr = _rows_tile(r)

    def body(x_ref, o_ref):
        acc = x_ref[0]
        for d in range(1, N_DEV):
            acc = acc + x_ref[d]
        o_ref[...] = acc

    return _pcall(body, name=name, grid=(r // tr,),
                  in_specs=[pl.BlockSpec((N_DEV, tr, c), lambda i: (0, i, 0))],
                  out_specs=pl.BlockSpec((tr, c), lambda i: (i, 0)),
                  out_shape=jax.ShapeDtypeStruct((r, c), F32), compiler_params=_cparams(1))(buf)


def _rows_tile(r):
    best = r
    for t in range(SUBLANE, 257, SUBLANE):
        if r % t == 0:
            best = t
    return best


def _adamw(grads, wgt, m, v, name):
    n_src, r, c = grads.shape
    tr = _rows_tile(r)
    bc1 = 1.0 - ADAM_B1 ** ADAM_STEP
    bc2 = 1.0 - ADAM_B2 ** ADAM_STEP

    def body(g_ref, w_ref, m_ref, v_ref, go_ref, d_ref, mo_ref, vo_ref):
        g = g_ref[0].astype(F32)
        for d in range(1, n_src):
            g = g + g_ref[d].astype(F32)
        m_new = ADAM_B1 * m_ref[...] + (1.0 - ADAM_B1) * g
        v_new = ADAM_B2 * v_ref[...] + (1.0 - ADAM_B2) * (g * g)
        go_ref[...] = g
        mo_ref[...] = m_new
        vo_ref[...] = v_new
        d_ref[...] = -ADAM_LR * ((m_new / bc1) / (jnp.sqrt(v_new / bc2) + ADAM_EPS) + ADAM_WD * w_ref[...])

    spec = pl.BlockSpec((tr, c), lambda i: (i, 0))
    return _pcall(body, name=name, grid=(r // tr,),
                  in_specs=[pl.BlockSpec((n_src, tr, c), lambda i: (0, i, 0)), spec, spec, spec],
                  out_specs=[spec] * 4, out_shape=[jax.ShapeDtypeStruct((r, c), F32)] * 4,
                  compiler_params=_cparams(1))(grads, wgt, m, v)


def _silu(z):
    return z * jax.nn.sigmoid(z)


_WEIGHTS = ("c_ctx", "w_mod", "b_mod", "norm1_g", "w_in", "q_a_g", "w_uq", "kv_a_g", "w_ukv", "mla_q_g", "mla_k_g",
            "conv_w", "conv_b", "lru_gate_w", "lru_gate_b", "lru_lambda", "swa_q_g", "swa_k_g", "swa_sink", "group_g",
            "w_out", "norm2_g", "w_ff1", "w_ff2")
_SHARDED_SMALL = ("conv_w", "lru_gate_b", "lru_lambda")
_REPL_RAW = ("norm1_g", "q_a_g", "kv_a_g", "mla_q_g", "mla_k_g", "conv_b", "lru_gate_w", "swa_q_g", "swa_k_g",
             "swa_sink", "group_g", "norm2_g")
MOD_ROWS = 32


def kernel(x, c, ctx, c_ctx, w_mod, b_mod, norm1_g, w_in, q_a_g, w_uq, kv_a_g, w_ukv, mla_q_g, mla_k_g, conv_w, conv_b, lru_gate_w, lru_gate_b, lru_lambda, swa_q_g, swa_k_g, swa_sink, group_g, w_out, norm2_g, w_ff1, w_ff2, loss_target, m_c_ctx, m_w_mod, m_b_mod, m_norm1_g, m_w_in, m_q_a_g, m_w_uq, m_kv_a_g, m_w_ukv, m_mla_q_g, m_mla_k_g, m_conv_w, m_conv_b, m_lru_gate_w, m_lru_gate_b, m_lru_lambda, m_swa_q_g, m_swa_k_g, m_swa_sink, m_group_g, m_w_out, m_norm2_g, m_w_ff1, m_w_ff2, v_c_ctx, v_w_mod, v_b_mod, v_norm1_g, v_w_in, v_q_a_g, v_w_uq, v_kv_a_g, v_w_ukv, v_mla_q_g, v_mla_k_g, v_conv_w, v_conv_b, v_lru_gate_w, v_lru_gate_b, v_lru_lambda, v_swa_q_g, v_swa_k_g, v_swa_sink, v_group_g, v_w_out, v_norm2_g, v_w_ff1, v_w_ff2):
    wts = dict(c_ctx=c_ctx, w_mod=w_mod, b_mod=b_mod, norm1_g=norm1_g, w_in=w_in, q_a_g=q_a_g, w_uq=w_uq,
               kv_a_g=kv_a_g, w_ukv=w_ukv, mla_q_g=mla_q_g, mla_k_g=mla_k_g, conv_w=conv_w, conv_b=conv_b,
               lru_gate_w=lru_gate_w, lru_gate_b=lru_gate_b, lru_lambda=lru_lambda, swa_q_g=swa_q_g, swa_k_g=swa_k_g,
               swa_sink=swa_sink, group_g=group_g, w_out=w_out, norm2_g=norm2_g, w_ff1=w_ff1, w_ff2=w_ff2)
    mom1 = dict(zip(_WEIGHTS, (m_c_ctx, m_w_mod, m_b_mod, m_norm1_g, m_w_in, m_q_a_g, m_w_uq, m_kv_a_g, m_w_ukv,
                               m_mla_q_g, m_mla_k_g, m_conv_w, m_conv_b, m_lru_gate_w, m_lru_gate_b, m_lru_lambda,
                               m_swa_q_g, m_swa_k_g, m_swa_sink, m_group_g, m_w_out, m_norm2_g, m_w_ff1, m_w_ff2)))
    mom2 = dict(zip(_WEIGHTS, (v_c_ctx, v_w_mod, v_b_mod, v_norm1_g, v_w_in, v_q_a_g, v_w_uq, v_kv_a_g, v_w_ukv,
                               v_mla_q_g, v_mla_k_g, v_conv_w, v_conv_b, v_lru_gate_w, v_lru_gate_b, v_lru_lambda,
                               v_swa_q_g, v_swa_k_g, v_swa_sink, v_group_g, v_w_out, v_norm2_g, v_w_ff1, v_w_ff2)))
    bsz = x.shape[0]
    n_ex = bsz * N_DEV
    me = _dev_index(_mesh_pos())
    mod_cols = w_mod.shape[-1]

    small_shapes = [c.shape, conv_w.shape, lru_gate_b.shape, lru_lambda.shape]
    (g_small,) = _exchange([_pack([c, conv_w, lru_gate_b, lru_lambda], F32)], True, "ag_small")
    c_all, conv_w_all, gate_b_all, lam_all = _unpack(g_small, small_shapes, lead=(N_DEV,))
    c_all = c_all.reshape(n_ex, D_MODEL)
    cat_last = lambda a: jnp.moveaxis(a, 0, -2).reshape(a.shape[1:-1] + (N_DEV * a.shape[-1],))
    conv_w_full, gate_b_full, lam_full = cat_last(conv_w_all), cat_last(gate_b_all), cat_last(lam_all)

    act = jnp.zeros((MOD_ROWS, D_MODEL), F32).at[:n_ex].set(_silu(c_all)).at[n_ex].set(_silu(c_ctx))
    mod_part = jnp.concatenate([_mm(act, w_mod[li], "nn", F32, "mm_mod_l%d" % li) for li in range(DEPTH)], axis=1)
    (mod_all,) = _exchange([mod_part], True, "ag_mod")
    mods = []
    for li in range(DEPTH):
        full = jnp.moveaxis(mod_all[:, :, li * mod_cols:(li + 1) * mod_cols], 0, 1).reshape(MOD_ROWS, -1) + b_mod[li]
        mine = lax.dynamic_slice_in_dim(full, me * bsz, bsz, axis=0)
        ctx_row = jnp.broadcast_to(full[n_ex], mine.shape)
        both = jnp.stack([ctx_row, mine], axis=1).reshape(bsz, 2, N_MOD, 1, D_MODEL)
        mods.append([both[:, :, j] for j in range(N_MOD)])

    raw = {n: wts[n] for n in _REPL_RAW}
    raw.update(conv_w=conv_w_full, lru_gate_b=gate_b_full, lru_lambda=lam_full)
    small_names = list(_REPL_RAW) + list(_SHARDED_SMALL)
    sp, small_vjp = [None] * DEPTH, [None] * DEPTH
    for li in range(DEPTH):
        sp[li], small_vjp[li] = jax.vjp(_prep_small, {n: raw[n][li] for n in small_names})

    w, w_vjp, g_recv, small_recv = [{} for _ in range(DEPTH)], {}, {}, {}
    shard = lambda n, li: wts[n][li].astype(BF16)

    def take(li, names, pieces):
        for n, piece in zip(names, pieces):
            out, w_vjp[n, li] = jax.vjp(functools.partial(_prep_weight, n), piece)
            w[li].update(out)

    def gather_hook(li, names):
        return (lambda _: _Exchange([shard(n, li) for n in names], True), lambda got: take(li, names, got))

    def wgrad(n, li, dwl):
        (g,) = w_vjp[n, li]({k: dwl[k].astype(BF16) for k in _BIG[n][2]})
        return g

    def small_pack(li, ds_l, extra=()):
        (d_raw,) = small_vjp[li](ds_l)
        return _pack([d_raw[n] for n in small_names] + list(extra), F32)

    take(0, _EARLY, _exchange([shard(n, 0) for n in _EARLY], True, "ag_early"))
    hooks_fwd = [{"mla_fwd": gather_hook(0, _LATE), "swa_fwd": gather_hook(1, _EARLY + ("w_out",))},
                 {"mla_fwd": gather_hook(1, ("w_ff1", "w_ff2"))}]
    bwd_state = {}

    def scatter_last_layer(dwl):
        return _Exchange([wgrad(n, 1, dwl) for n in _LATE], False)

    def scatter_first_layer(dwl):
        dw1, ds1 = bwd_state["dw1"], bwd_state["ds1"]
        bufs = [wgrad(n, 1, dw1) for n in _EARLY] + [wgrad(n, 0, dwl) for n in _LATE] + [small_pack(1, ds1)]
        return _Exchange(bufs, [False] * (len(_EARLY) + len(_LATE)) + [True])

    def scattered_first_layer(got):
        g_recv.update(zip([(n, 1) for n in _EARLY] + [(n, 0) for n in _LATE], got[:-1]))
        small_recv[1] = got[-1]

    hooks_bwd = [{"mla_bwd": (scatter_first_layer, scattered_first_layer)},
                 {"mla_bwd": (scatter_last_layer, lambda got: g_recv.update(zip([(n, 1) for n in _LATE], got)))}]

    tc, lat = ctx.shape[1], x.shape[1]
    tabs = {"mla": _rope_tables(lat, tc, MLA_ROPE, MLA_NOPE), "swa": _rope_tables(lat, tc, SWA_HEAD_DIM, 0)}
    stream = jnp.concatenate([ctx, x], axis=1)
    bwds = []
    for li in range(DEPTH):
        stream, bwd = _layer(li, stream, mods[li], w[li], sp[li], tabs, tc, li < DEPTH - 1, hooks_fwd[li])
        bwds.append(bwd)
    loss_part, dstream = _loss_and_grad(stream, loss_target, tc)
    dmods = [None] * DEPTH
    dstream, dmods[1], bwd_state["dw1"], bwd_state["ds1"] = bwds[1](dstream, hooks_bwd[1])
    dstream, dmods[0], dw0, ds0 = bwds[0](dstream, hooks_bwd[0])
    grad_x = dstream[:, tc:]

    dm_rows = []
    for li in range(DEPTH):
        dm = jnp.concatenate(dmods[li], axis=-1)
        dm_rows.append(jnp.concatenate([dm[:, 1, 0], jnp.sum(dm[:, 0, 0], axis=0, keepdims=True)], axis=0))
    dm_mine = jnp.concatenate(dm_rows, axis=1)
    dm_mine = jnp.pad(dm_mine, ((0, SUBLANE - bsz - 1), (0, 0)))
    (dm_all,) = _exchange([dm_mine], True, "ag_dmod")
    g_wmod, g_bmod, dact_ctx = [], [], jnp.zeros((D_MODEL,), F32)
    for li in range(DEPTH):
        part = dm_all[:, :, li * N_MOD * D_MODEL:(li + 1) * N_MOD * D_MODEL]
        dm32 = jnp.zeros((MOD_ROWS, N_MOD * D_MODEL), F32).at[:n_ex].set(part[:, :bsz].reshape(n_ex, -1))
        dm32 = dm32.at[n_ex].set(jnp.sum(part[:, bsz], axis=0))
        g_bmod.append(jnp.sum(dm32, axis=0))
        cols = lax.dynamic_slice_in_dim(dm32, me * mod_cols, mod_cols, axis=1)
        g_wmod.append(_mm(act, cols, "tn", F32, "mm_mod_dw_l%d" % li))
        dact_ctx = dact_ctx + _mm(cols, w_mod[li], "nt", F32, "mm_mod_dx_l%d" % li)[n_ex]
    sg = jax.nn.sigmoid(c_ctx)
    g_cctx_part = dact_ctx * (sg * (1.0 + c_ctx * (1.0 - sg)))

    last = _exchange([wgrad(n, 0, dw0) for n in _EARLY] + [small_pack(0, ds0, (g_cctx_part, loss_part.reshape(1)))],
                     [False] * len(_EARLY) + [True], "rs_early")
    g_recv.update(zip([(n, 0) for n in _EARLY], last[:-1]))
    small_recv[0] = last[-1]
    layer_shapes = [raw[n].shape[1:] for n in small_names]
    tot = [_unpack(_sum_sources(small_recv[li], "sum_grads_l%d" % li), layer_shapes + [(D_MODEL,), (1,)][:2 * (li == 0)])
           for li in range(DEPTH)]
    grads = {n: jnp.stack([tot[li][j] for li in range(DEPTH)], axis=0) for j, n in enumerate(small_names)}
    grads["c_ctx"], loss = tot[0][-2], tot[0][-1][0]
    for n in _SHARDED_SMALL:
        width = wts[n].shape[-1]
        grads[n] = lax.dynamic_slice_in_dim(grads[n], me * width, width, axis=grads[n].ndim - 1)
    grads["b_mod"] = jnp.stack(g_bmod, axis=0)

    delta, new_m, new_v = {}, {}, {}
    for n, (shp, _, _) in _BIG.items():
        two_d = (DEPTH * shp[0], shp[1])
        src = jnp.stack([g_recv[n, li] for li in range(DEPTH)], axis=1).reshape((N_DEV,) + two_d)
        res = _adamw(src, wts[n].reshape(two_d), mom1[n].reshape(two_d), mom2[n].reshape(two_d), "adamw_" + n)
        grads[n], delta[n], new_m[n], new_v[n] = [r.reshape(wts[n].shape) for r in res]
    two_d = (DEPTH * D_MODEL, mod_cols)
    res = _adamw(jnp.stack(g_wmod, axis=0).reshape((1,) + two_d), w_mod.reshape(two_d), mom1["w_mod"].reshape(two_d),
                 mom2["w_mod"].reshape(two_d), "adamw_w_mod")
    grads["w_mod"], delta["w_mod"], new_m["w_mod"], new_v["w_mod"] = [r.reshape(w_mod.shape) for r in res]
    rest = [n for n in _WEIGHTS if n not in delta]
    shapes = [wts[n].shape for n in rest]
    res = _adamw(_pack([grads[n] for n in rest], F32)[None], _pack([wts[n] for n in rest], F32),
                 _pack([mom1[n] for n in rest], F32), _pack([mom2[n] for n in rest], F32), "adamw_small")
    for tgt, buf in zip((delta, new_m, new_v), res[1:]):
        tgt.update(zip(rest, _unpack(buf, shapes)))

    return (loss, grad_x, *[grads[n] for n in _WEIGHTS], *[delta[n] for n in _WEIGHTS],
            *[new_m[n] for n in _WEIGHTS], *[new_v[n] for n in _WEIGHTS])
```

```python
import functools
import math

import jax
import jax.numpy as jnp
import numpy as np
from jax import lax
from jax.experimental import pallas as pl
from jax.experimental.pallas import tpu as pltpu

F32, BF16 = jnp.float32, jnp.bfloat16

N_DEV = 8
DEPTH = 2
D_MODEL = 1024
D_FF = 4096
N_MOD = 6
GRID_W = 64
WINDOW = 128
ROPE_THETA = 10000.0
EPS = 1e-6
NEG_INF = -1e30
LRU_C = 8.0
LRU_WIDTH = 512
MLA_HEADS, MLA_NOPE, MLA_ROPE, MLA_V = 8, 64, 32, 64
MLA_QK = MLA_NOPE + MLA_ROPE
MLA_Q_RANK, MLA_KV_RANK = 256, 128
SWA_HEADS, SWA_KV_HEADS, SWA_GROUP, SWA_HEAD_DIM = 8, 2, 4, 64
GROUP_WIDTH = 512
IN_SIZES = (256, 128, 32, 512, 512, 512, 128, 128)
IN_WIDTH = sum(IN_SIZES)
ADAM_LR, ADAM_B1, ADAM_B2, ADAM_EPS, ADAM_WD, ADAM_STEP = 0.001, 0.9, 0.999, 1e-08, 0.01, 10

LANE = 128
SUBLANE = 8
TB = 256
QB_SWA = 128
PACK_W = 1024
MM_K_CAP = 4608
MLA_HPS = 2
VMEM_LIMIT = 56 * 1024 * 1024
P_WIDTH = 3072
PC_SQ, PC_LX, PC_LG, PC_CQ, PC_SK, PC_SV, PC_CKV, PC_KR = 0, 1024, 1536, 2048, 2304, 2560, 2816, 2944
MIX_P = 1536


def _pcall(body, **kw):
    return pl.pallas_call(body, **kw)


def _cparams(n_grid):
    return pltpu.CompilerParams(dimension_semantics=("arbitrary",) * n_grid, vmem_limit_bytes=VMEM_LIMIT)


def _dg(a, b, ca, cb):
    return lax.dot_general(a.astype(BF16), b.astype(BF16), (((ca,), (cb,)), ((), ())),
                           preferred_element_type=F32)


@jax.custom_vjp
def _nn(a, b):
    return _dg(a, b, 1, 0)


@jax.custom_vjp
def _nt(a, b):
    return _dg(a, b, 1, 1)


@jax.custom_vjp
def _tn(a, b):
    return _dg(a, b, 0, 0)


_nn.defvjp(lambda a, b: (_nn(a, b), (a, b)), lambda r, ct: (_nt(ct, r[1]), _tn(r[0], ct)))
_nt.defvjp(lambda a, b: (_nt(a, b), (a, b)), lambda r, ct: (_nn(ct, r[1]), _tn(ct, r[0])))
_tn.defvjp(lambda a, b: (_tn(a, b), (a, b)), lambda r, ct: (_nt(r[1], ct), _nn(r[0], ct)))


@functools.partial(jax.custom_vjp, nondiff_argnums=(1, 2))
def _roll(x, shift, axis):
    return pltpu.roll(x, shift % x.shape[axis], axis)


_roll.defvjp(lambda x, shift, axis: (_roll(x, shift, axis), None),
             lambda shift, axis, _, ct: (_roll(ct, -shift, axis),))


@functools.partial(jax.custom_vjp, nondiff_argnums=(1, 2))
def _split(x, n, axis):
    w = x.shape[axis] // n
    return tuple(lax.slice_in_dim(x, i * w, (i + 1) * w, axis=axis) for i in range(n))


_split.defvjp(lambda x, n, axis: (_split(x, n, axis), None),
              lambda n, axis, _, cts: (jnp.concatenate(cts, axis=axis),))


@jax.custom_vjp
def _unstack(x):
    return tuple(x[i] for i in range(x.shape[0]))


_unstack.defvjp(lambda x: (_unstack(x), None), lambda _, cts: (jnp.stack(cts, axis=0),))


def _sig(x):
    return 0.5 * (jnp.tanh(0.5 * x) + 1.0)


def _gelu(x):
    return 0.5 * x * (1.0 + jnp.tanh(math.sqrt(2.0 / math.pi) * (x + 0.044715 * (x * x * x))))


def _rms(x, g, n):
    ms = jnp.sum(x * x, axis=-1, keepdims=True) * (1.0 / n)
    return x * lax.rsqrt(ms + EPS) * g


def _rope(y, cos, sa, sb, quarter):
    return y * cos + _roll(y, -quarter, 1) * sa + _roll(y, quarter, 1) * sb


def _softmax_rows(s, extra=None):
    m = jnp.max(s, axis=-1, keepdims=True)
    if extra is not None:
        m = jnp.maximum(m, extra)
    m = lax.stop_gradient(m)
    e = jnp.exp(s - m)
    den = jnp.sum(e, axis=-1, keepdims=True)
    if extra is not None:
        den = den + jnp.exp(extra - m)
    return e / den


class _A:
    def __init__(self, arr, block, imap, kind="row", first=None, gdtype=F32, gshape=None, gimap=None):
        self.arr, self.block, self.imap, self.kind, self.first = arr, block, imap, kind, first
        self.gdtype, self.gshape, self.gimap = gdtype, gshape, gimap


def _all_zero(*ids):
    return functools.reduce(jnp.logical_and, [i == 0 for i in ids])


def _par(arr):
    nd = arr.ndim
    return _A(arr, arr.shape, lambda *ids: (0,) * nd, "acc", first=_all_zero)


def _op_fwd(name, fn, grid, args, outs):
    n_in = len(args)

    def body(*refs):
        vals = [r[...].astype(F32) for r in refs[:n_in]]
        for r, v in zip(refs[n_in:], fn(*vals)):
            r[...] = v.astype(r.dtype)

    return _pcall(
        body, name=name, grid=grid,
        in_specs=[pl.BlockSpec(a.block, a.imap) for a in args],
        out_specs=[pl.BlockSpec(o[2], o[3]) for o in outs],
        out_shape=[jax.ShapeDtypeStruct(o[0], o[1]) for o in outs],
        compiler_params=_cparams(len(grid)),
    )(*[a.arr for a in args])


def _op_bwd(name, fn, grid, args, outs, ct_arrays, add_to_first=None):
    didx = [i for i, a in enumerate(args) if a.kind not in ("const", "fwd")]
    read = [i for i, a in enumerate(args) if a.kind != "fwd"]
    n_in, n_ct = len(read), len(outs)
    n_add = 0 if add_to_first is None else 1

    def body(*refs):
        ids = [pl.program_id(i) for i in range(len(grid))]
        vals = [jnp.zeros([d for d in a.block if d is not None], F32) for a in args]
        for i, r in zip(read, refs[:n_in]):
            vals[i] = r[...].astype(F32)

        def g(*dv):
            full = list(vals)
            for i, v in zip(didx, dv):
                full[i] = v
            return tuple(fn(*full))

        _, vjp = jax.vjp(g, *[vals[i] for i in didx])
        grads = list(vjp(tuple(r[...].astype(F32) for r in refs[n_in:n_in + n_ct])))
        if n_add:
            grads[0] = grads[0] + refs[n_in + n_ct][...]
        for gr, i, r in zip(grads, didx, refs[n_in + n_ct + n_add:]):
            a = args[i]
            if a.kind == "row":
                r[...] = gr.astype(r.dtype)
            else:
                first = a.first(*ids)

                @pl.when(first)
                def _():
                    r[...] = gr

                @pl.when(jnp.logical_not(first))
                def _():
                    r[...] += gr

    g_specs, g_shapes = [], []
    for i in didx:
        a = args[i]
        if a.kind == "row":
            g_specs.append(pl.BlockSpec(a.block, a.gimap or a.imap))
            g_shapes.append(jax.ShapeDtypeStruct(a.gshape or a.arr.shape, a.gdtype))
        else:
            g_specs.append(pl.BlockSpec(a.block, a.imap))
            g_shapes.append(jax.ShapeDtypeStruct(a.arr.shape, F32))
    return _pcall(
        body, name=name, grid=grid,
        in_specs=[pl.BlockSpec(args[i].block, args[i].imap) for i in read] + [pl.BlockSpec(o[2], o[3]) for o in outs]
        + g_specs[:n_add],
        out_specs=g_specs, out_shape=g_shapes,
        compiler_params=_cparams(len(grid)),
    )(*[args[i].arr for i in read], *ct_arrays, *([add_to_first] if n_add else []))


def _rowop(name, fn, grid, args, outs):
    res = _op_fwd(name, fn, grid, args, outs)
    return res, lambda *cts, add_to_first=None: _op_bwd(name + "_bwd", fn, grid, args, outs, cts, add_to_first)


def _pick(n, cap):
    best = None
    for t in range(LANE, cap + 1, LANE):
        if n % t == 0:
            best = t
    return best or n


def _mm(a, b, mode, out_dtype, name, epi=None, aux=None):
    if mode == "nn":
        (m, k), n = a.shape, b.shape[1]
    elif mode == "nt":
        (m, k), n = a.shape, b.shape[0]
    else:
        (k, m), n = a.shape, b.shape[1]
    tm = 512 if m % 512 == 0 else m
    tn, tk = _pick(n, 1024), _pick(k, MM_K_CAP)
    nk = k // tk
    if mode == "tn":
        a_spec = pl.BlockSpec((tk, tm), lambda j, i, kk: (kk, i))
    else:
        a_spec = pl.BlockSpec((tm, tk), lambda j, i, kk: (i, kk))
    if mode == "nt":
        b_spec = pl.BlockSpec((tn, tk), lambda j, i, kk: (j, kk))
    else:
        b_spec = pl.BlockSpec((tk, tn), lambda j, i, kk: (kk, j))
    dims = {"nn": (1, 0), "nt": (1, 1), "tn": (0, 0)}[mode]
    o_spec = pl.BlockSpec((tm, tn), lambda j, i, kk: (i, j))
    n_aux = 0 if aux is None else 1
    n_out = 2 if epi == "sqrelu" else 1

    def body(*refs):
        a_ref, b_ref = refs[0], refs[1]
        o_refs = refs[2 + n_aux:2 + n_aux + n_out]
        acc = refs[-1]
        kk = pl.program_id(2)
        part = _dg(a_ref[...], b_ref[...], *dims)

        if nk > 1:
            @pl.when(kk == 0)
            def _():
                acc[...] = part

            @pl.when((kk > 0) & (kk < nk - 1))
            def _():
                acc[...] += part

        @pl.when(kk == nk - 1)
        def _():
            r = part if nk == 1 else acc[...] + part
            if epi == "sqrelu":
                o_refs[0][...] = r.astype(o_refs[0].dtype)
                rl = jnp.maximum(r, 0.0)
                o_refs[1][...] = (rl * rl).astype(o_refs[1].dtype)
            elif epi == "dsqrelu":
                pre = refs[2][...].astype(F32)
                o_refs[0][...] = (r * (2.0 * jnp.maximum(pre, 0.0))).astype(o_refs[0].dtype)
            else:
                o_refs[0][...] = r.astype(o_refs[0].dtype)

    res = _pcall(
        body, name=name, grid=(n // tn, m // tm, nk),
        in_specs=[a_spec, b_spec] + [o_spec] * n_aux,
        out_specs=[o_spec] * n_out,
        out_shape=[jax.ShapeDtypeStruct((m, n), out_dtype)] * n_out,
        scratch_shapes=[pltpu.VMEM((tm, tn), F32)],
        compiler_params=_cparams(3),
    )(a, b, *([aux] if aux is not None else []))
    return res if n_out == 2 else res[0]


ROW_CHUNK = 16


def _softmax_chunks(s_scr, n_keys, scale, emit):
    for r0 in range(0, s_scr.shape[0], ROW_CHUNK):
        rows = slice(r0, r0 + ROW_CHUNK)
        s = s_scr[rows, :n_keys]
        e = jnp.exp((s - jnp.max(s, axis=-1, keepdims=True)) * scale)
        emit(rows, e, 1.0 / jnp.sum(e, axis=-1, keepdims=True))


def _attn_fwd_block(q, k, v, scale, s_scr, e_scr, l_scr):
    n = k.shape[0]
    s_scr[:, :n] = _dg(q, k, 1, 1)

    def emit(rows, e, inv_l):
        e_scr[rows, :n] = e.astype(BF16)
        l_scr[rows, :] = jnp.broadcast_to(inv_l, (ROW_CHUNK, LANE))

    _softmax_chunks(s_scr, n, scale, emit)
    return _dg(e_scr[:, :n], v, 1, 0) * l_scr[...]


def _attn_bwd_block(q, k, v, o, do, scale, s_scr, dp_scr, p_scr, ds_scr):
    n = k.shape[0]
    s_scr[:, :n] = _dg(q, k, 1, 1)
    dp_scr[:, :n] = _dg(do, v, 1, 1)

    def emit(rows, e, inv_l):
        p = e * inv_l
        delta = jnp.sum(do[rows, :] * o[rows, :], axis=-1, keepdims=True)
        p_scr[rows, :n] = p.astype(BF16)
        ds_scr[rows, :n] = (p * (dp_scr[rows, :n] - delta) * scale).astype(BF16)

    _softmax_chunks(s_scr, n, scale, emit)
    ds = ds_scr[:, :n]
    return _dg(ds, k, 1, 0), _dg(ds, q, 0, 0), _dg(p_scr[:, :n], do, 0, 0)


def _call_with_exchange(body, xchg, *, name, grid, in_specs, out_specs, out_shape, operands, scratch_shapes=()):
    if xchg is None:
        res = _pcall(body, name=name, grid=grid, in_specs=in_specs, out_specs=out_specs, out_shape=out_shape,
                     scratch_shapes=list(scratch_shapes), compiler_params=_cparams(len(grid)))(*operands)
        return list(res), []
    n_in, n_out, n_sc, n = len(in_specs), len(out_specs), len(scratch_shapes), xchg.n

    def wrapped(*refs):
        ins, x_refs = refs[:n_in], refs[n_in:n_in + n]
        outs, xo_refs = refs[n_in + n:n_in + n + n_out], refs[n_in + n + n_out:n_in + 2 * n + n_out]
        scratch, sems = refs[n_in + 2 * n + n_out:n_in + 2 * n + n_out + n_sc], refs[n_in + 2 * n + n_out + n_sc:]
        ids = [pl.program_id(i) for i in range(len(grid))]

        @pl.when(functools.reduce(jnp.logical_and, [i == 0 for i in ids]))
        def _():
            xchg.start(x_refs, xo_refs, sems)

        body(*ins, *outs, *scratch)

        @pl.when(functools.reduce(jnp.logical_and, [i == g - 1 for i, g in zip(ids, grid)]))
        def _():
            xchg.wait(x_refs, xo_refs, sems)

    res = _pcall(wrapped, name=name, grid=grid, in_specs=list(in_specs) + xchg.specs,
                 out_specs=list(out_specs) + xchg.specs, out_shape=list(out_shape) + xchg.out_shape,
                 scratch_shapes=list(scratch_shapes) + xchg.scratch, compiler_params=_cparams(len(grid)),
                 )(*operands, *xchg.bufs)
    return list(res[:n_out]), list(res[n_out:])


def _head_half(i, shape):
    lane = lax.broadcasted_iota(jnp.int32, shape, len(shape) - 1)
    return (lane < LANE // 2) if i == 0 else (lane >= LANE // 2)


def _mla_attn(q, k, v, tc, ctx_q, name, xchg=None):
    assert MLA_HPS == 2 and MLA_V == LANE // 2
    bsz, t_all, _ = q.shape
    n_t = t_all // TB
    grid = (bsz, MLA_HEADS // MLA_HPS, n_t)
    q_spec = pl.BlockSpec((None, TB, MLA_HPS * LANE), lambda b, h, t: (b, t, h))
    k_spec = pl.BlockSpec((None, t_all, MLA_HPS * LANE), lambda b, h, t: (b, 0, h))
    v_spec = pl.BlockSpec((None, t_all, LANE), lambda b, h, t: (b, 0, h))
    o_spec = pl.BlockSpec((None, TB, LANE), lambda b, h, t: (b, t, h))
    heads = [slice(i * LANE, (i + 1) * LANE) for i in range(MLA_HPS)]
    scale = MLA_QK ** -0.5
    f32_scr, bf16_scr = pltpu.VMEM((TB, t_all), F32), pltpu.VMEM((TB, t_all), BF16)
    o_shape = jax.ShapeDtypeStruct(v.shape, F32)

    def fwd_body(q_ref, k_ref, v_ref, o_ref, *scr):
        t = pl.program_id(2)

        def run(keys):
            both = [_attn_fwd_block(q_ref[:, hs], k_ref[keys, hs], v_ref[keys, :], scale, *scr[3 * i:3 * i + 3])
                    for i, hs in enumerate(heads)]
            o_ref[...] = jnp.where(_head_half(0, both[0].shape), both[0], both[1])

        @pl.when(t == 0)
        def _():
            if ctx_q:
                run(slice(0, tc))
            else:
                o_ref[...] = jnp.zeros_like(o_ref)

        @pl.when(t > 0)
        def _():
            run(slice(0, t_all))

    (o,), gathered = _call_with_exchange(
        fwd_body, xchg, name=name, grid=grid, in_specs=[q_spec, k_spec, v_spec], out_specs=[o_spec],
        out_shape=[o_shape], operands=(q, k, v),
        scratch_shapes=[f32_scr, bf16_scr, pltpu.VMEM((TB, LANE), F32)] * MLA_HPS)

    def bwd(do, xchg=None):
        def bwd_body(q_ref, k_ref, v_ref, o_ref, do_ref, dq_ref, dk_ref, dv_ref, *scr):
            t = pl.program_id(2)

            def run(keys, first):
                dvs = []
                for i, hs in enumerate(heads):
                    do_i = jnp.where(_head_half(i, do_ref.shape), do_ref[...], 0.0)
                    dq, dk, dv = _attn_bwd_block(q_ref[:, hs], k_ref[keys, hs], v_ref[keys, :], o_ref[...], do_i,
                                                 scale, *scr[4 * i:4 * i + 4])
                    dq_ref[:, hs] = dq
                    dvs.append(dv)
                    if first:
                        dk_ref[keys, hs] = dk
                    else:
                        dk_ref[keys, hs] += dk
                if first:
                    dv_ref[keys, :] = dvs[0] + dvs[1]
                else:
                    dv_ref[keys, :] += dvs[0] + dvs[1]

            @pl.when(t == 0)
            def _():
                dk_ref[...] = jnp.zeros_like(dk_ref)
                dv_ref[...] = jnp.zeros_like(dv_ref)
                if ctx_q:
                    run(slice(0, tc), True)
                else:
                    dq_ref[...] = jnp.zeros_like(dq_ref)

            @pl.when(t > 0)
            def _():
                run(slice(0, t_all), False)

        return _call_with_exchange(
            bwd_body, xchg, name=name + "_bwd", grid=grid, in_specs=[q_spec, k_spec, v_spec, o_spec, o_spec],
            out_specs=[q_spec, k_spec, v_spec],
            out_shape=[jax.ShapeDtypeStruct(q.shape, F32), jax.ShapeDtypeStruct(q.shape, F32), o_shape],
            operands=(q, k, v, o, do), scratch_shapes=[f32_scr, f32_scr, bf16_scr, bf16_scr] * MLA_HPS)

    return o, gathered, bwd


def _swa_block(q, keys, vals, sink, mask):
    qs = jnp.concatenate(list(_split(q, SWA_GROUP, 1)), axis=0)
    sk = jnp.sum(sink, axis=-1, keepdims=True) * (1.0 / LANE)
    s = _nt(qs, keys) * (SWA_HEAD_DIM ** -0.5)
    if mask is not None:
        s = jnp.where(mask, s, NEG_INF)
    o = _split(_nn(_softmax_rows(s, sk), vals + _roll(vals, LANE // 2, 1)), SWA_GROUP, 0)
    low = _head_half(0, o[0].shape)
    return jnp.concatenate([jnp.where(low, o[0], o[1]), jnp.where(low, o[2], o[3])], axis=1)


def _swa_ctx_block(q, kc, vc, sink):
    return _swa_block(q, kc, vc, sink, None)


def _swa_win_block(q, kc, kw, vc, vw, sink, mask):
    return _swa_block(q, jnp.concatenate([kc, kw], axis=0), jnp.concatenate([vc, vw], axis=0), sink, mask)


def _swa_attn(q, k, p_all, sink_b, tc, ctx_q, name, xchg=None):
    bsz, t_all, _ = q.shape
    n_q = t_all // QB_SWA
    n_cq = tc // QB_SWA
    lat = t_all - tc
    span = QB_SWA + 2 * WINDOW
    gw = SWA_GROUP * LANE
    grid = (bsz, SWA_KV_HEADS, n_q)
    q_spec = pl.BlockSpec((None, QB_SWA, gw), lambda b, g, i: (b, i, g))
    k_spec = pl.BlockSpec((None, t_all, LANE), lambda b, g, i: (b, 0, g))
    v_spec = pl.BlockSpec((None, t_all, LANE), lambda b, g, i: (b, 0, PC_SV // LANE + g))
    s_spec = pl.BlockSpec((None, SWA_GROUP * QB_SWA, LANE), lambda b, g, i: (g, 0, 0))

    def window(i):
        q0 = (i - n_cq) * QB_SWA
        w0 = jnp.clip(q0 - WINDOW, 0, lat - span)
        w0 = pl.multiple_of(w0, QB_SWA)
        shape = (SWA_GROUP * QB_SWA, tc + span)
        qi = q0 + lax.broadcasted_iota(jnp.int32, shape, 0) % QB_SWA
        col = lax.broadcasted_iota(jnp.int32, shape, 1)
        kj = w0 + col - tc
        mask = (col < tc) | ((kj >= qi - WINDOW) & (kj <= qi + WINDOW))
        return w0, mask

    def fwd_body(q_ref, k_ref, v_ref, s_ref, o_ref):
        i = pl.program_id(2)

        @pl.when(i < n_cq)
        def _():
            if ctx_q:
                o_ref[...] = _swa_ctx_block(q_ref[...].astype(F32), k_ref[0:tc, :], v_ref[0:tc, :], s_ref[...])
            else:
                o_ref[...] = jnp.zeros_like(o_ref)

        @pl.when(i >= n_cq)
        def _():
            w0, mask = window(i)
            o_ref[...] = _swa_win_block(q_ref[...].astype(F32), k_ref[0:tc, :], k_ref[pl.ds(tc + w0, span), :],
                                        v_ref[0:tc, :], v_ref[pl.ds(tc + w0, span), :], s_ref[...], mask)

    o_spec = pl.BlockSpec((None, QB_SWA, SWA_GROUP * SWA_HEAD_DIM), lambda b, g, i: (b, i, g))
    (o,), gathered = _call_with_exchange(
        fwd_body, xchg, name=name, grid=grid, in_specs=[q_spec, k_spec, v_spec, s_spec], out_specs=[o_spec],
        out_shape=[jax.ShapeDtypeStruct((bsz, t_all, SWA_HEADS * SWA_HEAD_DIM), F32)],
        operands=(q, k, p_all, sink_b))

    def bwd(do, xchg=None):
        def bwd_body(q_ref, k_ref, v_ref, s_ref, do_ref, dq_ref, dk_ref, dv_ref, ds_ref):
            i = pl.program_id(2)

            @pl.when(i == 0)
            def _():
                dk_ref[...] = jnp.zeros_like(dk_ref)
                dv_ref[...] = jnp.zeros_like(dv_ref)
                ds_ref[...] = jnp.zeros_like(ds_ref)

            @pl.when(i < n_cq)
            def _():
                if ctx_q:
                    _, vjp = jax.vjp(_swa_ctx_block, q_ref[...].astype(F32), k_ref[0:tc, :].astype(F32),
                                     v_ref[0:tc, :], s_ref[...])
                    dq, dk, dv, ds = vjp(do_ref[...])
                    dq_ref[...] = dq
                    dk_ref[0:tc, :] += dk
                    dv_ref[0:tc, :] += dv
                    ds_ref[...] += ds
                else:
                    dq_ref[...] = jnp.zeros_like(dq_ref)

            @pl.when(i >= n_cq)
            def _():
                w0, mask = window(i)
                win = pl.ds(tc + w0, span)
                _, vjp = jax.vjp(functools.partial(_swa_win_block, mask=mask), q_ref[...].astype(F32),
                                 k_ref[0:tc, :].astype(F32), k_ref[win, :].astype(F32),
                                 v_ref[0:tc, :], v_ref[win, :], s_ref[...])
                dq, dkc, dkw, dvc, dvw, ds = vjp(do_ref[...])
                dq_ref[...] = dq
                dk_ref[0:tc, :] += dkc
                dk_ref[win, :] += dkw
                dv_ref[0:tc, :] += dvc
                dv_ref[win, :] += dvw
                ds_ref[...] += ds

        kv_out = pl.BlockSpec((None, t_all, LANE), lambda b, g, i: (b, 0, g))
        ds_spec = pl.BlockSpec((None, None, SWA_GROUP * QB_SWA, LANE), lambda b, g, i: (b, g, 0, 0))
        kv_shape = jax.ShapeDtypeStruct((bsz, t_all, SWA_KV_HEADS * LANE), F32)
        return _call_with_exchange(
            bwd_body, xchg, name=name + "_bwd", grid=grid, in_specs=[q_spec, k_spec, v_spec, s_spec, o_spec],
            out_specs=[q_spec, kv_out, kv_out, ds_spec],
            out_shape=[jax.ShapeDtypeStruct(q.shape, F32), kv_shape, kv_shape,
                       jax.ShapeDtypeStruct((bsz,) + sink_b.shape, F32)],
            operands=(q, k, p_all, sink_b, do))

    return o, gathered, bwd


def _scan_pair(chains, scratch):
    t_all, c = chains[0][0].shape
    n_tiles = t_all // SUBLANE
    row8 = lax.broadcasted_iota(jnp.int32, (t_all, c), 0) % SUBLANE
    refs = [scratch[0:3], scratch[3:6]]
    for (a, u, reverse), (a_s, u_s, _) in zip(chains, refs):
        for d in (1, 2, 4):
            sh = d if not reverse else t_all - d
            ar, ur = pltpu.roll(a, sh, 0), pltpu.roll(u, sh, 0)
            m = (row8 >= d) if not reverse else (row8 < SUBLANE - d)
            u = jnp.where(m, a * ur + u, u)
            a = jnp.where(m, a * ar, a)
        a_s[...] = a
        u_s[...] = u

    def step(j, carries):
        out = []
        for (_, _, reverse), (a_s, u_s, c_s), carry in zip(chains, refs, carries):
            tile = j if not reverse else n_tiles - 1 - j
            base = pl.multiple_of(tile * SUBLANE, SUBLANE)
            c_s[pl.ds(base, SUBLANE), :] = jnp.broadcast_to(carry, (SUBLANE, c))
            last = base + (0 if reverse else SUBLANE - 1)
            out.append(a_s[pl.ds(last, 1), :] * carry + u_s[pl.ds(last, 1), :])
        return tuple(out)

    lax.fori_loop(0, n_tiles, step, (jnp.zeros((1, c), F32),) * 2, unroll=4)
    return [a_s[...] * c_s[...] + u_s[...] for a_s, u_s, c_s in refs]


def _shift_rows(x, reverse_src):
    t_all = x.shape[0]
    row = lax.broadcasted_iota(jnp.int32, x.shape, 0)
    if reverse_src:
        return jnp.where(row == t_all - 1, 0.0, pltpu.roll(x, t_all - 1, 0))
    return jnp.where(row == 0, 0.0, pltpu.roll(x, 1, 0))


def _lru_scan(a0, u0, a1, u1, name):
    bsz, t_all, w = a0.shape
    grid = (bsz, w // LANE)
    spec = pl.BlockSpec((None, t_all, LANE), lambda b, c: (b, 0, c))
    scratch = [pltpu.VMEM((t_all, LANE), F32)] * 6
    shape = jax.ShapeDtypeStruct(a0.shape, F32)

    def fwd_body(a0_ref, u0_ref, a1_ref, u1_ref, h0_ref, h1_ref, *scr):
        h0_ref[...], h1_ref[...] = _scan_pair([(a0_ref[...], u0_ref[...], False), (a1_ref[...], u1_ref[...], True)],
                                              scr)

    h0, h1 = _pcall(fwd_body, name=name, grid=grid, in_specs=[spec] * 4, out_specs=[spec] * 2,
                    out_shape=[shape] * 2, scratch_shapes=scratch, compiler_params=_cparams(2))(a0, u0, a1, u1)

    def bwd(dh0, dh1):
        def bwd_body(a0_ref, h0_ref, g0_ref, a1_ref, h1_ref, g1_ref, da0_ref, du0_ref, da1_ref, du1_ref, *scr):
            g0, g1 = _scan_pair([(_shift_rows(a0_ref[...], True), g0_ref[...], True),
                                 (_shift_rows(a1_ref[...], False), g1_ref[...], False)], scr)
            du0_ref[...] = g0
            da0_ref[...] = g0 * _shift_rows(h0_ref[...], False)
            du1_ref[...] = g1
            da1_ref[...] = g1 * _shift_rows(h1_ref[...], True)

        return _pcall(bwd_body, name=name + "_bwd", grid=grid, in_specs=[spec] * 6, out_specs=[spec] * 4,
                      out_shape=[shape] * 4, scratch_shapes=scratch,
                      compiler_params=_cparams(2))(a0, h0, dh0, a1, h1, dh1)

    return h0, h1, bwd


def _f_mod(x, g, shift, scale):
    return (_rms(x, g, D_MODEL) * (1.0 + scale) + shift,)


def _f_mla_q(cq, ga, w, gh, cos, sa, sb):
    n = _rms(cq, ga, MLA_Q_RANK)
    outs = []
    for wh in _split(w, MLA_HEADS, 1):
        outs.append(_rope(_rms(_nn(n, wh), gh, MLA_QK), cos, sa, sb, MLA_ROPE // 4))
    return (jnp.concatenate(outs, axis=1),)


def _f_mla_kv(ckv, krp, ga, wk, wv, gh, cos, sa, sb):
    n = _rms(ckv, ga, MLA_KV_RANK)
    outs = []
    for wh in _split(wk, MLA_HEADS, 1):
        outs.append(_rope(_rms(_nn(n, wh) + krp, gh, MLA_QK), cos, sa, sb, MLA_ROPE // 4))
    return jnp.concatenate(outs, axis=1), _nn(n, wv)


def _f_conv(x, w0, w1, w2, w3, bias, tc):
    t_all = x.shape[0]
    row = lax.broadcasted_iota(jnp.int32, x.shape, 0)
    lo = jnp.where(row < tc, 0, tc)
    hi = jnp.where(row < tc, tc, t_all)
    y = bias + jnp.zeros_like(x)
    for kk, wk in enumerate((w0, w1, w2, w3)):
        src = row + (kk - 2)
        xs = x if kk == 2 else _roll(x, 2 - kk, 0)
        y = y + wk * jnp.where((src >= lo) & (src < hi), xs, 0.0)
    return (y,)


def _f_gates(xc, w16, b00, b01, b10, b11, sp0, sp1):
    ws = _unstack(w16)
    n_cb = LRU_WIDTH // LANE
    xcs = _split(xc, n_cb, 1)
    bias = [_split(b, n_cb, 1) for b in (b00, b01, b10, b11)]
    sps = [_split(s, n_cb, 1) for s in (sp0, sp1)]
    res = [[], [], [], []]
    for c in range(n_cb):
        for z in range(2):
            r = _sig(_nn(xcs[c], ws[c * 4 + 2 * z]) + bias[2 * z][c])
            i = _sig(_nn(xcs[c], ws[c * 4 + 2 * z + 1]) + bias[2 * z + 1][c])
            la = -LRU_C * r * sps[z][c]
            res[2 * z].append(jnp.exp(la))
            res[2 * z + 1].append(jnp.sqrt(-jnp.tanh(la) * (jnp.exp(2.0 * la) + 1.0)) * (i * xcs[c]))
    return tuple(jnp.concatenate(r, axis=1) for r in res)


def _f_swa_qk(sq, sk, gq, gk, cos, sa, sb):
    qs = [_rope(_rms(x, gq, SWA_HEAD_DIM), cos, sa, sb, SWA_HEAD_DIM // 4) for x in _split(sq, SWA_HEADS, 1)]
    ks = [_rope(_rms(x, gk, SWA_HEAD_DIM), cos, sa, sb, SWA_HEAD_DIM // 4) for x in _split(sk, SWA_KV_HEADS, 1)]
    return jnp.concatenate(qs, axis=1), jnp.concatenate(ks, axis=1)


def _f_qkv(cq, ckv, krp, sq, sk, q_a_g, wuq, mla_q_g, kv_a_g, wk, wv, mla_k_g, swa_q_g, swa_k_g,
           m_cos, m_sa, m_sb, s_cos, s_sa, s_sb):
    return (*_f_mla_q(cq, q_a_g, wuq, mla_q_g, m_cos, m_sa, m_sb),
            *_f_mla_kv(ckv, krp, kv_a_g, wk, wv, mla_k_g, m_cos, m_sa, m_sb),
            *_f_swa_qk(sq, sk, swa_q_g, swa_k_g, s_cos, s_sa, s_sb))


def _f_merge(oa, h0, h1, lg, oc, ga, gb, gc):
    ob = (h0 + h1) * _gelu(lg)
    return (jnp.concatenate([_rms(oa, ga, GROUP_WIDTH), _rms(ob, gb, GROUP_WIDTH), _rms(oc, gc, GROUP_WIDTH)],
                            axis=1),)


def _f_resid_mod(x, y, gate, g, shift, scale):
    x1 = x + gate * y
    return x1, _rms(x1, g, D_MODEL) * (1.0 + scale) + shift


def _f_resid(x, y, gate):
    return (x + gate * y,)


def _hosted(hooks, key, arg=None):
    make, done = hooks.get(key, (None, None))
    xchg = make(arg) if make is not None else None
    return xchg, (done if xchg is not None else lambda outs: None)


def _layer(li, x, mods, w, s, tabs, tc, ctx_q, hooks):
    bsz, t_all, _ = x.shape
    n_t = t_all // TB
    grid = (bsz, n_t)
    rows = lambda b, t: (b, t, 0)

    def row(arr, width=None, idx=0, gdtype=F32, gshape=None):
        width = width or arr.shape[-1]
        return _A(arr, (None, TB, width), lambda b, t: (b, t, idx), "row", gdtype=gdtype, gshape=gshape,
                  gimap=rows if gshape is not None else None)

    def out(width, dtype, imap=rows):
        return ((bsz, t_all, width), dtype, (None, TB, width), imap)

    def modarg(arr):
        return _A(arr, (None, None, 1, D_MODEL), lambda b, t: (b, jnp.minimum(t, 1), 0, 0), "acc",
                  first=lambda b, t: t <= 1)

    def tab(arr):
        return _A(arr, (TB, LANE), lambda b, t: (t, 0), "const")

    def pcol(p_all, col, width):
        return row(p_all, width, col // width, gdtype=BF16, gshape=(bsz, t_all, width))

    nm = lambda base: "%s_l%d" % (base, li)
    sh1, sc1, g1, sh2, sc2, g2 = mods
    m_all = bsz * t_all

    (h,), b_mod1 = _rowop(nm("mod1"), _f_mod, grid, [row(x), _par(s["norm1_g"]), modarg(sh1), modarg(sc1)],
                          [out(D_MODEL, BF16)])
    p_all = _mm(h.reshape(m_all, D_MODEL), w["win"], "nn", F32, nm("mm_in")).reshape(bsz, t_all, P_WIDTH)

    (q_a, k_a, v_a, q_c, k_c), b_qkv = _rowop(
        nm("qkv"), _f_qkv, grid,
        [pcol(p_all, PC_CQ, 256), pcol(p_all, PC_CKV, 128), pcol(p_all, PC_KR, 128), pcol(p_all, PC_SQ, 1024),
         pcol(p_all, PC_SK, 256)]
        + [_par(a) for a in (s["q_a_g"], w["wuq"], s["mla_q_g"], s["kv_a_g"], w["wk"], w["wv"], s["mla_k_g"],
                             s["swa_q_g"], s["swa_k_g"])]
        + [tab(a) for a in tabs["mla"] + tabs["swa"]],
        [out(MLA_HEADS * LANE, BF16), out(MLA_HEADS * LANE, BF16), out(MLA_HEADS * MLA_V, BF16),
         out(SWA_HEADS * LANE, BF16), out(SWA_KV_HEADS * LANE, BF16)])

    xchg, done = _hosted(hooks, "mla_fwd")
    o_a, got, b_attn_a = _mla_attn(q_a, k_a, v_a, tc, ctx_q, nm("mla_attn"), xchg)
    done(got)

    n_cb = LRU_WIDTH // LANE
    conv_grid = (n_cb, bsz)
    cpar = lambda arr: _A(arr, (1, LANE), lambda c, b: (0, c), "acc", first=lambda c, b: b == 0)
    conv_args = [_A(p_all, (None, t_all, LANE), lambda c, b: (b, 0, PC_LX // LANE + c), "row", gdtype=BF16,
                    gshape=(bsz, t_all, LRU_WIDTH), gimap=lambda c, b: (b, 0, c))]
    conv_args += [cpar(a) for a in s["conv_w"]] + [cpar(s["conv_b"])]
    conv_out = [((bsz, t_all, LRU_WIDTH), F32, (None, t_all, LANE), lambda c, b: (b, 0, c))]
    (xc,), b_conv = _rowop(nm("lru_conv"), functools.partial(_f_conv, tc=tc), conv_grid, conv_args, conv_out)
    rot = lambda b, t: (b, (t + n_t - 1) % n_t, 0)
    (a0, u0, a1, u1), b_gates = _rowop(
        nm("lru_gates"), _f_gates, grid,
        [row(xc), _par(s["wbd"])] + [_par(a) for a in s["gate_b"]] + [_par(a) for a in s["sp"]],
        [out(LRU_WIDTH, F32), out(LRU_WIDTH, F32), out(LRU_WIDTH, F32, rot), out(LRU_WIDTH, F32, rot)])
    h0, h1, b_scan = _lru_scan(a0, u0, a1, u1, nm("lru_scan"))
    h1_arg = _A(h1, (None, TB, LRU_WIDTH), rot, "row")

    xchg, done = _hosted(hooks, "swa_fwd")
    o_c, got, b_attn_c = _swa_attn(q_c, k_c, p_all, s["sink_b"], tc, ctx_q, nm("swa_attn"), xchg)
    done(got)

    (y_in,), b_merge = _rowop(nm("merge"), _f_merge, grid,
                              [row(o_a), row(h0), h1_arg, pcol(p_all, PC_LG, 512), row(o_c), _par(s["g_a"]),
                               _par(s["g_b"]), _par(s["g_c"])],
                              [out(MIX_P, BF16)])
    y = _mm(y_in.reshape(m_all, MIX_P), w["wout"], "nn", F32, nm("mm_out")).reshape(bsz, t_all, D_MODEL)
    (x1, hm), b_rm = _rowop(nm("resid_mod"), _f_resid_mod, grid,
                            [row(x), row(y, gdtype=BF16), modarg(g1), _par(s["norm2_g"]), modarg(sh2), modarg(sc2)],
                            [out(D_MODEL, F32), out(D_MODEL, BF16)])
    pre, act = _mm(hm.reshape(m_all, D_MODEL), w["ff1"], "nn", BF16, nm("mm_ff1"), epi="sqrelu")
    y2 = _mm(act, w["ff2"], "nn", F32, nm("mm_ff2")).reshape(bsz, t_all, D_MODEL)
    (x2,), b_res = _rowop(nm("resid"), _f_resid, grid,
                          [_A(x1, (None, TB, D_MODEL), rows, "fwd"), row(y2, gdtype=BF16), modarg(g2)],
                          [out(D_MODEL, F32)])

    def bwd(dx2, hooks):
        dw, ds = {}, {}
        dy2, dg2 = b_res(dx2)
        dy2 = dy2.reshape(m_all, D_MODEL)
        dpre = _mm(dy2, w["ff2"], "nt", BF16, nm("mm_ff2_dx"), epi="dsqrelu", aux=pre)
        dw["ff2"] = _mm(act, dy2, "tn", BF16, nm("mm_ff2_dw"))
        dhm = _mm(dpre, w["ff1"], "nt", F32, nm("mm_ff1_dx")).reshape(bsz, t_all, D_MODEL)
        dw["ff1"] = _mm(hm.reshape(m_all, D_MODEL), dpre, "tn", BF16, nm("mm_ff1_dw"))
        dxa, dy, dg1, ds["norm2_g"], dsh2, dsc2 = b_rm(dx2, dhm)
        dy = dy.reshape(m_all, D_MODEL)
        dy_in = _mm(dy, w["wout"], "nt", F32, nm("mm_out_dx")).reshape(bsz, t_all, MIX_P)
        dw["wout"] = _mm(y_in.reshape(m_all, MIX_P), dy, "tn", BF16, nm("mm_out_dw"))
        do_a, dh0, dh1, dlg, do_c, ds["g_a"], ds["g_b"], ds["g_c"] = b_merge(dy_in)

        (dq_c, dk_c, dsv, dsink), _ = b_attn_c(do_c)
        ds["sink_b"] = jnp.sum(dsink, axis=0)

        da0, du0, da1, du1 = b_scan(dh0, dh1)
        gates_g = b_gates(da0, du0, da1, du1)
        dxc, ds["wbd"] = gates_g[0], gates_g[1]
        ds["gate_b"], ds["sp"] = list(gates_g[2:6]), list(gates_g[6:8])
        conv_g = b_conv(dxc)
        dlx, ds["conv_w"], ds["conv_b"] = conv_g[0], list(conv_g[1:5]), conv_g[5]

        xchg, done = _hosted(hooks, "mla_bwd", dw)
        (dq_a, dk_a, dv_a), got = b_attn_a(do_a, xchg)
        done(got)
        (dcq, dckv, dkr, dsq, dsk, ds["q_a_g"], dw["wuq"], ds["mla_q_g"], ds["kv_a_g"], dw["wk"], dw["wv"],
         ds["mla_k_g"], ds["swa_q_g"], ds["swa_k_g"]) = b_qkv(dq_a, dk_a, dv_a, dq_c, dk_c)

        dp = jnp.concatenate([dsq, dlx, dlg, dcq, dsk, dsv.astype(BF16), dckv, dkr], axis=-1)
        dp = dp.reshape(m_all, P_WIDTH)
        dh = _mm(dp, w["win"], "nt", F32, nm("mm_in_dx")).reshape(bsz, t_all, D_MODEL)
        dw["win"] = _mm(h.reshape(m_all, D_MODEL), dp, "tn", BF16, nm("mm_in_dw"))
        dx, ds["norm1_g"], dsh1, dsc1 = b_mod1(dh, add_to_first=dxa)
        return dx, [dsh1, dsc1, dg1, dsh2, dsc2, dg2], dw, ds

    return x2, bwd


def _loss_and_grad(x2, target, tc):
    bsz, t_all, d = x2.shape
    n_t = t_all // TB
    n_c = tc // TB

    def body(x_ref, t_ref, l_ref, dx_ref):
        b, t = pl.program_id(0), pl.program_id(1)

        @pl.when((b == 0) & (t == 0))
        def _():
            l_ref[...] = jnp.zeros_like(l_ref)

        @pl.when(t < n_c)
        def _():
            dx_ref[...] = jnp.zeros_like(dx_ref)

        @pl.when(t >= n_c)
        def _():
            e = x_ref[...] - t_ref[...]
            dx_ref[...] = e * (1.0 / d)
            l_ref[...] += jnp.sum(e * e) * (0.5 / d)

    loss, dx = _pcall(
        body, name="loss", grid=(bsz, n_t),
        in_specs=[pl.BlockSpec((None, TB, d), lambda b, t: (b, t, 0)),
                  pl.BlockSpec((None, TB, d), lambda b, t: (b, jnp.maximum(t - n_c, 0), 0))],
        out_specs=[pl.BlockSpec((SUBLANE, LANE), lambda b, t: (0, 0)),
                   pl.BlockSpec((None, TB, d), lambda b, t: (b, t, 0))],
        out_shape=[jax.ShapeDtypeStruct((SUBLANE, LANE), F32), jax.ShapeDtypeStruct(x2.shape, F32)],
        compiler_params=_cparams(2))(x2, target)
    return loss[0, 0], dx


def _rope_tables(lat, tc, dim, lane0):
    quarter = dim // 4
    pos = np.arange(lat)
    grid_pos = np.stack([pos // GRID_W, pos % GRID_W], axis=-1).astype(np.float32)
    lane = np.arange(LANE)
    p = np.clip(lane - lane0, 0, dim - 1)
    active = (lane >= lane0) & (lane < lane0 + dim)
    axis, half, qi = p // (dim // 2), (p % (dim // 2)) // quarter, p % quarter
    inv = (np.float32(ROPE_THETA) ** (-qi.astype(np.float32) / np.float32(quarter))).astype(np.float32)
    ang = (np.where(axis[None, :] == 0, grid_pos[:, 0:1], grid_pos[:, 1:2]) * inv[None, :]).astype(np.float32)
    cos = np.where(active, np.cos(ang), 1.0).astype(np.float32)
    sin = np.where(active, np.sin(ang), 0.0).astype(np.float32)
    sa = np.where(half == 0, -sin, 0.0).astype(np.float32)
    sb = np.where(half == 1, sin, 0.0).astype(np.float32)
    ctx1, ctx0 = np.ones((tc, LANE), np.float32), np.zeros((tc, LANE), np.float32)
    return tuple(jnp.asarray(np.concatenate([c, t], 0)) for c, t in ((ctx1, cos), (ctx0, sa), (ctx0, sb)))


_BIG = {"w_in": ((D_MODEL, IN_WIDTH // N_DEV), 1, ("win",)),
        "w_uq": ((MLA_Q_RANK, MLA_HEADS * MLA_QK // N_DEV), 1, ("wuq",)),
        "w_ukv": ((MLA_KV_RANK, MLA_HEADS * (MLA_NOPE + MLA_V) // N_DEV), 1, ("wk", "wv")),
        "w_out": ((3 * GROUP_WIDTH // N_DEV, D_MODEL), 0, ("wout",)),
        "w_ff1": ((D_MODEL, D_FF // N_DEV), 1, ("ff1",)),
        "w_ff2": ((D_FF // N_DEV, D_MODEL), 0, ("ff2",))}
_EARLY = ("w_in", "w_uq", "w_ukv")
_LATE = ("w_out", "w_ff1", "w_ff2")


def _pad_heads(wm, n_heads, dim, axis=-1):
    axis = axis % wm.ndim
    shp = wm.shape[:axis] + (n_heads, dim) + wm.shape[axis + 1:]
    pad = [(0, 0)] * len(shp)
    pad[axis + 1] = (0, LANE - dim)
    out = jnp.pad(wm.reshape(shp), pad)
    return out.reshape(wm.shape[:axis] + (n_heads * LANE,) + wm.shape[axis + 1:])


def _prep_weight(name, piece):
    shp, ax, _ = _BIG[name]
    full = jnp.moveaxis(piece, 0, ax).reshape(shp[:ax] + (N_DEV * shp[ax],) + shp[ax + 1:])
    if name == "w_in":
        cq, ckv, kr, lx, lg, sq, sk, sv = _split_cols(full)
        return {"win": jnp.concatenate(
            [_pad_heads(sq, SWA_HEADS, SWA_HEAD_DIM), lx, lg, cq, _pad_heads(sk, SWA_KV_HEADS, SWA_HEAD_DIM),
             _pad_heads(sv, SWA_KV_HEADS, SWA_HEAD_DIM), ckv, jnp.pad(kr, ((0, 0), (MLA_NOPE, LANE - MLA_QK)))], axis=1)}
    if name == "w_uq":
        return {"wuq": _pad_heads(full, MLA_HEADS, MLA_QK)}
    if name == "w_ukv":
        ukv = full.reshape(MLA_KV_RANK, MLA_HEADS, MLA_NOPE + MLA_V)
        return {"wk": _pad_heads(ukv[:, :, :MLA_NOPE].reshape(MLA_KV_RANK, -1), MLA_HEADS, MLA_NOPE),
                "wv": ukv[:, :, MLA_NOPE:].reshape(MLA_KV_RANK, -1)}
    return {_BIG[name][2][0]: full}


def _split_cols(wm):
    parts, start = [], 0
    for size in IN_SIZES:
        parts.append(wm[:, start:start + size])
        start += size
    return parts


def _prep_small(raw):
    r1 = lambda a: a.reshape(1, -1)
    gw = raw["lru_gate_w"].reshape(2, 2, 4, 2, 64, 64)
    wbd = jnp.einsum("zgknCm,nN->knCzgNm", gw, jnp.eye(2, dtype=F32)).reshape(4, LANE, 4, LANE)
    gg = raw["group_g"]
    sink = raw["swa_sink"].reshape(SWA_KV_HEADS, SWA_GROUP, 1, 1)
    return {
        "norm1_g": r1(raw["norm1_g"]), "norm2_g": r1(raw["norm2_g"]),
        "q_a_g": r1(raw["q_a_g"]), "kv_a_g": r1(raw["kv_a_g"]),
        "mla_q_g": jnp.pad(r1(raw["mla_q_g"]), ((0, 0), (0, LANE - MLA_QK))),
        "mla_k_g": jnp.pad(r1(raw["mla_k_g"]), ((0, 0), (0, LANE - MLA_QK))),
        "swa_q_g": jnp.pad(r1(raw["swa_q_g"]), ((0, 0), (0, LANE - SWA_HEAD_DIM))),
        "swa_k_g": jnp.pad(r1(raw["swa_k_g"]), ((0, 0), (0, LANE - SWA_HEAD_DIM))),
        "conv_w": [r1(raw["conv_w"][kk]) for kk in range(4)], "conv_b": r1(raw["conv_b"]),
        "wbd": wbd.transpose(0, 2, 1, 3).reshape(16, LANE, LANE),
        "gate_b": [r1(raw["lru_gate_b"][z, g]) for z in range(2) for g in range(2)],
        "sp": [r1(jax.nn.softplus(-raw["lru_lambda"][z])) for z in range(2)],
        "sink_b": jnp.broadcast_to(sink, (SWA_KV_HEADS, SWA_GROUP, QB_SWA, LANE)).reshape(
            SWA_KV_HEADS, SWA_GROUP * QB_SWA, LANE),
        "g_a": r1(gg[:GROUP_WIDTH]), "g_b": r1(gg[GROUP_WIDTH:2 * GROUP_WIDTH]), "g_c": r1(gg[2 * GROUP_WIDTH:])}


def _mesh_pos():
    return lax.axis_index("x"), lax.axis_index("y"), lax.axis_index("c")


def _peer(pos, k):
    return tuple(1 - p if (k >> s) & 1 else p for p, s in zip(pos, (2, 1, 0)))


def _dev_index(pos):
    return 4 * pos[0] + 2 * pos[1] + pos[2]


class _Exchange:
    def __init__(self, bufs, gather):
        self.bufs = list(bufs)
        self.n = len(self.bufs)
        self.gather = [gather] * self.n if isinstance(gather, bool) else list(gather)
        self.specs = [pl.BlockSpec(memory_space=pl.ANY)] * self.n
        self.out_shape = [jax.ShapeDtypeStruct((N_DEV,) + tuple(b.shape if g else b.shape[1:]), b.dtype)
                          for b, g in zip(self.bufs, self.gather)]
        self.scratch = [pltpu.SemaphoreType.DMA(((N_DEV - 1) * self.n,)),
                        pltpu.SemaphoreType.DMA(((N_DEV - 1) * self.n,)), pltpu.SemaphoreType.DMA((self.n,))]

    def _copies(self, x_refs, o_refs, sems, with_recvs):
        send_sems, recv_sems, local_sems = sems
        pos = _mesh_pos()
        me = _dev_index(pos)
        locals_, sends, recvs = [], [], []
        for j in range(self.n):
            src_mine = x_refs[j] if self.gather[j] else x_refs[j].at[me]
            locals_.append(pltpu.make_async_copy(src_mine, o_refs[j].at[me], local_sems.at[j]))
        for k in range(1, N_DEV):
            peer = _peer(pos, k)
            pidx = _dev_index(peer)
            for j in range(self.n):
                src = x_refs[j] if self.gather[j] else x_refs[j].at[pidx]
                sem = (k - 1) * self.n + j
                sends.append(pltpu.make_async_remote_copy(
                    src_ref=src, dst_ref=o_refs[j].at[me], send_sem=send_sems.at[sem], recv_sem=recv_sems.at[sem],
                    device_id=peer, device_id_type=pl.DeviceIdType.MESH))
                if with_recvs:
                    recvs.append(pltpu.make_async_remote_copy(
                        src_ref=src, dst_ref=o_refs[j].at[pidx], send_sem=send_sems.at[sem],
                        recv_sem=recv_sems.at[sem], device_id=peer, device_id_type=pl.DeviceIdType.MESH))
        return locals_, sends, recvs

    def start(self, x_refs, o_refs, sems):
        locals_, sends, _ = self._copies(x_refs, o_refs, sems, False)
        for cp in locals_ + sends:
            cp.start()

    def wait(self, x_refs, o_refs, sems):
        locals_, sends, recvs = self._copies(x_refs, o_refs, sems, True)
        for cp in recvs:
            cp.wait_recv()
        for cp in sends:
            cp.wait_send()
        for cp in locals_:
            cp.wait()


def _exchange(bufs, gather, name):
    xchg = _Exchange(bufs, gather)
    n = xchg.n

    def body(*refs):
        xchg.start(refs[:n], refs[n:2 * n], refs[2 * n:])
        xchg.wait(refs[:n], refs[n:2 * n], refs[2 * n:])

    return _pcall(body, name=name, out_shape=xchg.out_shape, in_specs=xchg.specs, out_specs=xchg.specs,
                  scratch_shapes=xchg.scratch)(*xchg.bufs)


def _pack(arrs, dtype):
    flat = jnp.concatenate([a.reshape(-1).astype(dtype) for a in arrs])
    rows = -(-flat.size // PACK_W)
    rows = -(-rows // 16) * 16
    return jnp.pad(flat, (0, rows * PACK_W - flat.size)).reshape(rows, PACK_W)


def _unpack(buf, shapes, lead=()):
    flat = buf.reshape(lead + (-1,))
    out, off = [], 0
    for shp in shapes:
        n = math.prod(shp)
        out.append(flat[..., off:off + n].reshape(lead + tuple(shp)))
        off += n
    return out


def _sum_sources(buf, name):
    _, r, c = buf.shape
    tr = _rows_tile(r)

    def body(x_ref, o_ref):
        acc = x_ref[0]
        for d in range(1, N_DEV):
            acc = acc + x_ref[d]
        o_ref[...] = acc

    return _pcall(body, name=name, grid=(r // tr,),
                  in_specs=[pl.BlockSpec((N_DEV, tr, c), lambda i: (0, i, 0))],
                  out_specs=pl.BlockSpec((tr, c), lambda i: (i, 0)),
                  out_shape=jax.ShapeDtypeStruct((r, c), F32), compiler_params=_cparams(1))(buf)


def _rows_tile(r):
    best = r
    for t in range(SUBLANE, 257, SUBLANE):
        if r % t == 0:
            best = t
    return best


def _adamw(grads, wgt, m, v, name):
    n_lay = len(grads)
    n_src, r, c = grads[0].shape
    tr = _rows_tile(r)
    n_blk = r // tr
    bc1 = 1.0 - ADAM_B1 ** ADAM_STEP
    bc2 = 1.0 - ADAM_B2 ** ADAM_STEP

    def body(*refs):
        g_refs, (w_ref, m_ref, v_ref, go_ref, d_ref, mo_ref, vo_ref) = refs[:n_lay], refs[n_lay:]
        for li, g_ref in enumerate(g_refs):
            @pl.when(pl.program_id(0) == li)
            def _():
                g = g_ref[0].astype(F32)
                for d in range(1, n_src):
                    g = g + g_ref[d].astype(F32)
                m_new = ADAM_B1 * m_ref[...] + (1.0 - ADAM_B1) * g
                v_new = ADAM_B2 * v_ref[...] + (1.0 - ADAM_B2) * (g * g)
                go_ref[...] = g
                mo_ref[...] = m_new
                vo_ref[...] = v_new
                d_ref[...] = -ADAM_LR * ((m_new / bc1) / (jnp.sqrt(v_new / bc2) + ADAM_EPS) + ADAM_WD * w_ref[...])

    g_specs = [pl.BlockSpec((n_src, tr, c),
                            lambda l, i, li=li: (0, jnp.where(l == li, i, jnp.where(l > li, n_blk - 1, 0)), 0))
               for li in range(n_lay)]
    spec = pl.BlockSpec((tr, c), lambda l, i: (l * n_blk + i, 0))
    return _pcall(body, name=name, grid=(n_lay, n_blk), in_specs=g_specs + [spec, spec, spec],
                  out_specs=[spec] * 4, out_shape=[jax.ShapeDtypeStruct((n_lay * r, c), F32)] * 4,
                  compiler_params=_cparams(2))(*grads, wgt, m, v)


def _silu(z):
    return z * jax.nn.sigmoid(z)


_WEIGHTS = ("c_ctx", "w_mod", "b_mod", "norm1_g", "w_in", "q_a_g", "w_uq", "kv_a_g", "w_ukv", "mla_q_g", "mla_k_g",
            "conv_w", "conv_b", "lru_gate_w", "lru_gate_b", "lru_lambda", "swa_q_g", "swa_k_g", "swa_sink", "group_g",
            "w_out", "norm2_g", "w_ff1", "w_ff2")
_SHARDED_SMALL = ("conv_w", "lru_gate_b", "lru_lambda")
_REPL_RAW = ("norm1_g", "q_a_g", "kv_a_g", "mla_q_g", "mla_k_g", "conv_b", "lru_gate_w", "swa_q_g", "swa_k_g",
             "swa_sink", "group_g", "norm2_g")
MOD_ROWS = 32


def kernel(x, c, ctx, c_ctx, w_mod, b_mod, norm1_g, w_in, q_a_g, w_uq, kv_a_g, w_ukv, mla_q_g, mla_k_g, conv_w, conv_b, lru_gate_w, lru_gate_b, lru_lambda, swa_q_g, swa_k_g, swa_sink, group_g, w_out, norm2_g, w_ff1, w_ff2, loss_target, m_c_ctx, m_w_mod, m_b_mod, m_norm1_g, m_w_in, m_q_a_g, m_w_uq, m_kv_a_g, m_w_ukv, m_mla_q_g, m_mla_k_g, m_conv_w, m_conv_b, m_lru_gate_w, m_lru_gate_b, m_lru_lambda, m_swa_q_g, m_swa_k_g, m_swa_sink, m_group_g, m_w_out, m_norm2_g, m_w_ff1, m_w_ff2, v_c_ctx, v_w_mod, v_b_mod, v_norm1_g, v_w_in, v_q_a_g, v_w_uq, v_kv_a_g, v_w_ukv, v_mla_q_g, v_mla_k_g, v_conv_w, v_conv_b, v_lru_gate_w, v_lru_gate_b, v_lru_lambda, v_swa_q_g, v_swa_k_g, v_swa_sink, v_group_g, v_w_out, v_norm2_g, v_w_ff1, v_w_ff2):
    wts = dict(c_ctx=c_ctx, w_mod=w_mod, b_mod=b_mod, norm1_g=norm1_g, w_in=w_in, q_a_g=q_a_g, w_uq=w_uq,
               kv_a_g=kv_a_g, w_ukv=w_ukv, mla_q_g=mla_q_g, mla_k_g=mla_k_g, conv_w=conv_w, conv_b=conv_b,
               lru_gate_w=lru_gate_w, lru_gate_b=lru_gate_b, lru_lambda=lru_lambda, swa_q_g=swa_q_g, swa_k_g=swa_k_g,
               swa_sink=swa_sink, group_g=group_g, w_out=w_out, norm2_g=norm2_g, w_ff1=w_ff1, w_ff2=w_ff2)
    mom1 = dict(zip(_WEIGHTS, (m_c_ctx, m_w_mod, m_b_mod, m_norm1_g, m_w_in, m_q_a_g, m_w_uq, m_kv_a_g, m_w_ukv,
                               m_mla_q_g, m_mla_k_g, m_conv_w, m_conv_b, m_lru_gate_w, m_lru_gate_b, m_lru_lambda,
                               m_swa_q_g, m_swa_k_g, m_swa_sink, m_group_g, m_w_out, m_norm2_g, m_w_ff1, m_w_ff2)))
    mom2 = dict(zip(_WEIGHTS, (v_c_ctx, v_w_mod, v_b_mod, v_norm1_g, v_w_in, v_q_a_g, v_w_uq, v_kv_a_g, v_w_ukv,
                               v_mla_q_g, v_mla_k_g, v_conv_w, v_conv_b, v_lru_gate_w, v_lru_gate_b, v_lru_lambda,
                               v_swa_q_g, v_swa_k_g, v_swa_sink, v_group_g, v_w_out, v_norm2_g, v_w_ff1, v_w_ff2)))
    bsz = x.shape[0]
    n_ex = bsz * N_DEV
    me = _dev_index(_mesh_pos())
    mod_cols = w_mod.shape[-1]

    small_shapes = [c.shape, conv_w.shape, lru_gate_b.shape, lru_lambda.shape]
    (g_small,) = _exchange([_pack([c, conv_w, lru_gate_b, lru_lambda], F32)], True, "ag_small")
    c_all, conv_w_all, gate_b_all, lam_all = _unpack(g_small, small_shapes, lead=(N_DEV,))
    c_all = c_all.reshape(n_ex, D_MODEL)
    cat_last = lambda a: jnp.moveaxis(a, 0, -2).reshape(a.shape[1:-1] + (N_DEV * a.shape[-1],))
    conv_w_full, gate_b_full, lam_full = cat_last(conv_w_all), cat_last(gate_b_all), cat_last(lam_all)

    act = jnp.zeros((MOD_ROWS, D_MODEL), F32).at[:n_ex].set(_silu(c_all)).at[n_ex].set(_silu(c_ctx))
    mod_part = jnp.concatenate([_mm(act, w_mod[li], "nn", F32, "mm_mod_l%d" % li) for li in range(DEPTH)], axis=1)
    (mod_all,) = _exchange([mod_part], True, "ag_mod")
    mods = []
    for li in range(DEPTH):
        full = jnp.moveaxis(mod_all[:, :, li * mod_cols:(li + 1) * mod_cols], 0, 1).reshape(MOD_ROWS, -1) + b_mod[li]
        mine = lax.dynamic_slice_in_dim(full, me * bsz, bsz, axis=0)
        ctx_row = jnp.broadcast_to(full[n_ex], mine.shape)
        both = jnp.stack([ctx_row, mine], axis=1).reshape(bsz, 2, N_MOD, 1, D_MODEL)
        mods.append([both[:, :, j] for j in range(N_MOD)])

    raw = {n: wts[n] for n in _REPL_RAW}
    raw.update(conv_w=conv_w_full, lru_gate_b=gate_b_full, lru_lambda=lam_full)
    small_names = list(_REPL_RAW) + list(_SHARDED_SMALL)
    sp, small_vjp = [None] * DEPTH, [None] * DEPTH
    for li in range(DEPTH):
        sp[li], small_vjp[li] = jax.vjp(_prep_small, {n: raw[n][li] for n in small_names})

    w, w_vjp, g_recv, small_recv = [{} for _ in range(DEPTH)], {}, {}, {}
    shard = lambda n, li: wts[n][li].astype(BF16)

    def take(li, names, pieces):
        for n, piece in zip(names, pieces):
            out, w_vjp[n, li] = jax.vjp(functools.partial(_prep_weight, n), piece)
            w[li].update(out)

    def gather_hook(li, names):
        return (lambda _: _Exchange([shard(n, li) for n in names], True), lambda got: take(li, names, got))

    def wgrad(n, li, dwl):
        (g,) = w_vjp[n, li]({k: dwl[k].astype(BF16) for k in _BIG[n][2]})
        return g

    pack_names = [n for n in small_names if n != "lru_gate_w"]
    gate_2d = (math.prod(lru_gate_w.shape[1:-1]), lru_gate_w.shape[-1])

    def small_bufs(li, ds_l, extra=()):
        (d_raw,) = small_vjp[li](ds_l)
        return [_pack([d_raw[n] for n in pack_names] + list(extra), F32), d_raw["lru_gate_w"].reshape(gate_2d)]

    take(0, _EARLY, _exchange([shard(n, 0) for n in _EARLY], True, "ag_early"))
    hooks_fwd = [{"mla_fwd": gather_hook(0, _LATE), "swa_fwd": gather_hook(1, _EARLY + ("w_out",))},
                 {"mla_fwd": gather_hook(1, ("w_ff1", "w_ff2"))}]
    bwd_state = {}

    def scatter_last_layer(dwl):
        return _Exchange([wgrad(n, 1, dwl) for n in _LATE], False)

    def scatter_first_layer(dwl):
        dw1, ds1 = bwd_state["dw1"], bwd_state["ds1"]
        bufs = [wgrad(n, 1, dw1) for n in _EARLY] + [wgrad(n, 0, dwl) for n in _LATE] + small_bufs(1, ds1)
        return _Exchange(bufs, [False] * (len(_EARLY) + len(_LATE)) + [True, True])

    def scattered_first_layer(got):
        g_recv.update(zip([(n, 1) for n in _EARLY] + [(n, 0) for n in _LATE], got[:-2]))
        small_recv[1], g_recv["lru_gate_w", 1] = got[-2:]

    hooks_bwd = [{"mla_bwd": (scatter_first_layer, scattered_first_layer)},
                 {"mla_bwd": (scatter_last_layer, lambda got: g_recv.update(zip([(n, 1) for n in _LATE], got)))}]

    tc, lat = ctx.shape[1], x.shape[1]
    tabs = {"mla": _rope_tables(lat, tc, MLA_ROPE, MLA_NOPE), "swa": _rope_tables(lat, tc, SWA_HEAD_DIM, 0)}
    stream = jnp.concatenate([ctx, x], axis=1)
    bwds = []
    for li in range(DEPTH):
        stream, bwd = _layer(li, stream, mods[li], w[li], sp[li], tabs, tc, li < DEPTH - 1, hooks_fwd[li])
        bwds.append(bwd)
    loss_part, dstream = _loss_and_grad(stream, loss_target, tc)
    dmods = [None] * DEPTH
    dstream, dmods[1], bwd_state["dw1"], bwd_state["ds1"] = bwds[1](dstream, hooks_bwd[1])
    dstream, dmods[0], dw0, ds0 = bwds[0](dstream, hooks_bwd[0])
    grad_x = dstream[:, tc:]

    dm_rows = []
    for li in range(DEPTH):
        dm = jnp.concatenate(dmods[li], axis=-1)
        dm_rows.append(jnp.concatenate([dm[:, 1, 0], jnp.sum(dm[:, 0, 0], axis=0, keepdims=True)], axis=0))
    dm_mine = jnp.concatenate(dm_rows, axis=1)
    dm_mine = jnp.pad(dm_mine, ((0, SUBLANE - bsz - 1), (0, 0)))
    (dm_all,) = _exchange([dm_mine], True, "ag_dmod")
    g_wmod, g_bmod, dact_ctx = [], [], jnp.zeros((D_MODEL,), F32)
    for li in range(DEPTH):
        part = dm_all[:, :, li * N_MOD * D_MODEL:(li + 1) * N_MOD * D_MODEL]
        dm32 = jnp.zeros((MOD_ROWS, N_MOD * D_MODEL), F32).at[:n_ex].set(part[:, :bsz].reshape(n_ex, -1))
        dm32 = dm32.at[n_ex].set(jnp.sum(part[:, bsz], axis=0))
        g_bmod.append(jnp.sum(dm32, axis=0))
        cols = lax.dynamic_slice_in_dim(dm32, me * mod_cols, mod_cols, axis=1)
        g_wmod.append(_mm(act, cols, "tn", F32, "mm_mod_dw_l%d" % li))
        dact_ctx = dact_ctx + _mm(cols, w_mod[li], "nt", F32, "mm_mod_dx_l%d" % li)[n_ex]
    sg = jax.nn.sigmoid(c_ctx)
    g_cctx_part = dact_ctx * (sg * (1.0 + c_ctx * (1.0 - sg)))

    last = _exchange([wgrad(n, 0, dw0) for n in _EARLY] + small_bufs(0, ds0, (g_cctx_part, loss_part.reshape(1))),
                     [False] * len(_EARLY) + [True, True], "rs_early")
    g_recv.update(zip([(n, 0) for n in _EARLY], last[:-2]))
    small_recv[0], g_recv["lru_gate_w", 0] = last[-2:]
    layer_shapes = [raw[n].shape[1:] for n in pack_names]
    tot = [_unpack(_sum_sources(small_recv[li], "sum_grads_l%d" % li), layer_shapes + [(D_MODEL,), (1,)][:2 * (li == 0)])
           for li in range(DEPTH)]
    grads = {n: jnp.stack([tot[li][j] for li in range(DEPTH)], axis=0) for j, n in enumerate(pack_names)}
    grads["c_ctx"], loss = tot[0][-2], tot[0][-1][0]
    for n in _SHARDED_SMALL:
        width = wts[n].shape[-1]
        grads[n] = lax.dynamic_slice_in_dim(grads[n], me * width, width, axis=grads[n].ndim - 1)
    grads["b_mod"] = jnp.stack(g_bmod, axis=0)

    delta, new_m, new_v = {}, {}, {}
    per_layer = {n: [g_recv[n, li] for li in range(DEPTH)] for n in list(_BIG) + ["lru_gate_w"]}
    per_layer["w_mod"] = [g[None] for g in g_wmod]
    for n, srcs in per_layer.items():
        two_d = (DEPTH * math.prod(wts[n].shape[1:-1]), wts[n].shape[-1])
        srcs = [s.reshape((s.shape[0], two_d[0] // DEPTH, two_d[1])) for s in srcs]
        res = _adamw(srcs, wts[n].reshape(two_d), mom1[n].reshape(two_d), mom2[n].reshape(two_d), "adamw_" + n)
        grads[n], delta[n], new_m[n], new_v[n] = [r.reshape(wts[n].shape) for r in res]
    rest = [n for n in _WEIGHTS if n not in delta]
    shapes = [wts[n].shape for n in rest]
    res = _adamw([_pack([grads[n] for n in rest], F32)[None]], _pack([wts[n] for n in rest], F32),
                 _pack([mom1[n] for n in rest], F32), _pack([mom2[n] for n in rest], F32), "adamw_small")
    for tgt, buf in zip((delta, new_m, new_v), res[1:]):
        tgt.update(zip(rest, _unpack(buf, shapes)))

    return (loss, grad_x, *[grads[n] for n in _WEIGHTS], *[delta[n] for n in _WEIGHTS],
            *[new_m[n] for n in _WEIGHTS], *[new_v[n] for n in _WEIGHTS])
```

```python
import functools
import math

import jax
import jax.numpy as jnp
import numpy as np
from jax import lax
from jax.experimental import pallas as pl
from jax.experimental.pallas import tpu as pltpu

F32, BF16 = jnp.float32, jnp.bfloat16

N_DEV = 8
DEPTH = 2
D_MODEL = 1024
D_FF = 4096
N_MOD = 6
GRID_W = 64
WINDOW = 128
ROPE_THETA = 10000.0
EPS = 1e-6
NEG_INF = -1e30
LRU_C = 8.0
LRU_WIDTH = 512
MLA_HEADS, MLA_NOPE, MLA_ROPE, MLA_V = 8, 64, 32, 64
MLA_QK = MLA_NOPE + MLA_ROPE
MLA_Q_RANK, MLA_KV_RANK = 256, 128
SWA_HEADS, SWA_KV_HEADS, SWA_GROUP, SWA_HEAD_DIM = 8, 2, 4, 64
GROUP_WIDTH = 512
IN_SIZES = (256, 128, 32, 512, 512, 512, 128, 128)
IN_WIDTH = sum(IN_SIZES)
ADAM_LR, ADAM_B1, ADAM_B2, ADAM_EPS, ADAM_WD, ADAM_STEP = 0.001, 0.9, 0.999, 1e-08, 0.01, 10

LANE = 128
SUBLANE = 8
TB = 256
QB_SWA = 256
PACK_W = 1024
MM_K_CAP = 4608
MLA_HPS = 2
VMEM_LIMIT = 56 * 1024 * 1024
P_WIDTH = 3072
PC_SQ, PC_LX, PC_LG, PC_CQ, PC_SK, PC_SV, PC_CKV, PC_KR = 0, 1024, 1536, 2048, 2304, 2560, 2816, 2944
MIX_P = 1536


def _pcall(body, **kw):
    return pl.pallas_call(body, **kw)


def _cparams(n_grid):
    return pltpu.CompilerParams(dimension_semantics=("arbitrary",) * n_grid, vmem_limit_bytes=VMEM_LIMIT)


def _dg(a, b, ca, cb):
    return lax.dot_general(a.astype(BF16), b.astype(BF16), (((ca,), (cb,)), ((), ())),
                           preferred_element_type=F32)


@jax.custom_vjp
def _nn(a, b):
    return _dg(a, b, 1, 0)


@jax.custom_vjp
def _nt(a, b):
    return _dg(a, b, 1, 1)


@jax.custom_vjp
def _tn(a, b):
    return _dg(a, b, 0, 0)


_nn.defvjp(lambda a, b: (_nn(a, b), (a, b)), lambda r, ct: (_nt(ct, r[1]), _tn(r[0], ct)))
_nt.defvjp(lambda a, b: (_nt(a, b), (a, b)), lambda r, ct: (_nn(ct, r[1]), _tn(ct, r[0])))
_tn.defvjp(lambda a, b: (_tn(a, b), (a, b)), lambda r, ct: (_nt(r[1], ct), _nn(r[0], ct)))


@functools.partial(jax.custom_vjp, nondiff_argnums=(1, 2))
def _roll(x, shift, axis):
    return pltpu.roll(x, shift % x.shape[axis], axis)


_roll.defvjp(lambda x, shift, axis: (_roll(x, shift, axis), None),
             lambda shift, axis, _, ct: (_roll(ct, -shift, axis),))


@functools.partial(jax.custom_vjp, nondiff_argnums=(1, 2))
def _split(x, n, axis):
    w = x.shape[axis] // n
    return tuple(lax.slice_in_dim(x, i * w, (i + 1) * w, axis=axis) for i in range(n))


_split.defvjp(lambda x, n, axis: (_split(x, n, axis), None),
              lambda n, axis, _, cts: (jnp.concatenate(cts, axis=axis),))


@jax.custom_vjp
def _unstack(x):
    return tuple(x[i] for i in range(x.shape[0]))


_unstack.defvjp(lambda x: (_unstack(x), None), lambda _, cts: (jnp.stack(cts, axis=0),))


def _sig(x):
    return 0.5 * (jnp.tanh(0.5 * x) + 1.0)


def _gelu(x):
    return 0.5 * x * (1.0 + jnp.tanh(math.sqrt(2.0 / math.pi) * (x + 0.044715 * (x * x * x))))


def _rms(x, g, n):
    ms = jnp.sum(x * x, axis=-1, keepdims=True) * (1.0 / n)
    return x * lax.rsqrt(ms + EPS) * g


def _rope(y, cos, sa, sb, quarter):
    return y * cos + _roll(y, -quarter, 1) * sa + _roll(y, quarter, 1) * sb


def _softmax_rows(s, extra=None):
    m = jnp.max(s, axis=-1, keepdims=True)
    if extra is not None:
        m = jnp.maximum(m, extra)
    m = lax.stop_gradient(m)
    e = jnp.exp(s - m)
    den = jnp.sum(e, axis=-1, keepdims=True)
    if extra is not None:
        den = den + jnp.exp(extra - m)
    return e / den


class _A:
    def __init__(self, arr, block, imap, kind="row", first=None, gdtype=F32, gshape=None, gimap=None):
        self.arr, self.block, self.imap, self.kind, self.first = arr, block, imap, kind, first
        self.gdtype, self.gshape, self.gimap = gdtype, gshape, gimap


def _all_zero(*ids):
    return functools.reduce(jnp.logical_and, [i == 0 for i in ids])


def _par(arr):
    nd = arr.ndim
    return _A(arr, arr.shape, lambda *ids: (0,) * nd, "acc", first=_all_zero)


def _op_fwd(name, fn, grid, args, outs):
    n_in = len(args)

    def body(*refs):
        vals = [r[...].astype(F32) for r in refs[:n_in]]
        for r, v in zip(refs[n_in:], fn(*vals)):
            r[...] = v.astype(r.dtype)

    return _pcall(
        body, name=name, grid=grid,
        in_specs=[pl.BlockSpec(a.block, a.imap) for a in args],
        out_specs=[pl.BlockSpec(o[2], o[3]) for o in outs],
        out_shape=[jax.ShapeDtypeStruct(o[0], o[1]) for o in outs],
        compiler_params=_cparams(len(grid)),
    )(*[a.arr for a in args])


def _op_bwd(name, fn, grid, args, outs, ct_arrays, add_to_first=None):
    didx = [i for i, a in enumerate(args) if a.kind not in ("const", "fwd")]
    read = [i for i, a in enumerate(args) if a.kind != "fwd"]
    n_in, n_ct = len(read), len(outs)
    n_add = 0 if add_to_first is None else 1

    def body(*refs):
        ids = [pl.program_id(i) for i in range(len(grid))]
        vals = [jnp.zeros([d for d in a.block if d is not None], F32) for a in args]
        for i, r in zip(read, refs[:n_in]):
            vals[i] = r[...].astype(F32)

        def g(*dv):
            full = list(vals)
            for i, v in zip(didx, dv):
                full[i] = v
            return tuple(fn(*full))

        _, vjp = jax.vjp(g, *[vals[i] for i in didx])
        grads = list(vjp(tuple(r[...].astype(F32) for r in refs[n_in:n_in + n_ct])))
        if n_add:
            grads[0] = grads[0] + refs[n_in + n_ct][...]
        for gr, i, r in zip(grads, didx, refs[n_in + n_ct + n_add:]):
            a = args[i]
            if a.kind == "row":
                r[...] = gr.astype(r.dtype)
            else:
                first = a.first(*ids)

                @pl.when(first)
                def _():
                    r[...] = gr

                @pl.when(jnp.logical_not(first))
                def _():
                    r[...] += gr

    g_specs, g_shapes = [], []
    for i in didx:
        a = args[i]
        if a.kind == "row":
            g_specs.append(pl.BlockSpec(a.block, a.gimap or a.imap))
            g_shapes.append(jax.ShapeDtypeStruct(a.gshape or a.arr.shape, a.gdtype))
        else:
            g_specs.append(pl.BlockSpec(a.block, a.imap))
            g_shapes.append(jax.ShapeDtypeStruct(a.arr.shape, F32))
    return _pcall(
        body, name=name, grid=grid,
        in_specs=[pl.BlockSpec(args[i].block, args[i].imap) for i in read] + [pl.BlockSpec(o[2], o[3]) for o in outs]
        + g_specs[:n_add],
        out_specs=g_specs, out_shape=g_shapes,
        compiler_params=_cparams(len(grid)),
    )(*[args[i].arr for i in read], *ct_arrays, *([add_to_first] if n_add else []))


def _rowop(name, fn, grid, args, outs):
    res = _op_fwd(name, fn, grid, args, outs)
    return res, lambda *cts, add_to_first=None: _op_bwd(name + "_bwd", fn, grid, args, outs, cts, add_to_first)


def _pick(n, cap):
    best = None
    for t in range(LANE, cap + 1, LANE):
        if n % t == 0:
            best = t
    return best or n


def _mm(a, b, mode, out_dtype, name, epi=None, aux=None):
    if mode == "nn":
        (m, k), n = a.shape, b.shape[1]
    elif mode == "nt":
        (m, k), n = a.shape, b.shape[0]
    else:
        (k, m), n = a.shape, b.shape[1]
    tm = 512 if m % 512 == 0 else m
    tn, tk = _pick(n, 1024), _pick(k, MM_K_CAP)
    nk = k // tk
    if mode == "tn":
        a_spec = pl.BlockSpec((tk, tm), lambda j, i, kk: (kk, i))
    else:
        a_spec = pl.BlockSpec((tm, tk), lambda j, i, kk: (i, kk))
    if mode == "nt":
        b_spec = pl.BlockSpec((tn, tk), lambda j, i, kk: (j, kk))
    else:
        b_spec = pl.BlockSpec((tk, tn), lambda j, i, kk: (kk, j))
    dims = {"nn": (1, 0), "nt": (1, 1), "tn": (0, 0)}[mode]
    o_spec = pl.BlockSpec((tm, tn), lambda j, i, kk: (i, j))
    n_aux = 0 if aux is None else 1
    n_out = 2 if epi == "sqrelu" else 1

    def body(*refs):
        a_ref, b_ref = refs[0], refs[1]
        o_refs = refs[2 + n_aux:2 + n_aux + n_out]
        acc = refs[-1]
        kk = pl.program_id(2)
        part = _dg(a_ref[...], b_ref[...], *dims)

        if nk > 1:
            @pl.when(kk == 0)
            def _():
                acc[...] = part

            @pl.when((kk > 0) & (kk < nk - 1))
            def _():
                acc[...] += part

        @pl.when(kk == nk - 1)
        def _():
            r = part if nk == 1 else acc[...] + part
            if epi == "sqrelu":
                o_refs[0][...] = r.astype(o_refs[0].dtype)
                rl = jnp.maximum(r, 0.0)
                o_refs[1][...] = (rl * rl).astype(o_refs[1].dtype)
            elif epi == "dsqrelu":
                pre = refs[2][...].astype(F32)
                o_refs[0][...] = (r * (2.0 * jnp.maximum(pre, 0.0))).astype(o_refs[0].dtype)
            else:
                o_refs[0][...] = r.astype(o_refs[0].dtype)

    res = _pcall(
        body, name=name, grid=(n // tn, m // tm, nk),
        in_specs=[a_spec, b_spec] + [o_spec] * n_aux,
        out_specs=[o_spec] * n_out,
        out_shape=[jax.ShapeDtypeStruct((m, n), out_dtype)] * n_out,
        scratch_shapes=[pltpu.VMEM((tm, tn), F32)],
        compiler_params=_cparams(3),
    )(a, b, *([aux] if aux is not None else []))
    return res if n_out == 2 else res[0]


ROW_CHUNK = 16


def _softmax_chunks(s_scr, n_keys, scale, emit):
    for r0 in range(0, s_scr.shape[0], ROW_CHUNK):
        rows = slice(r0, r0 + ROW_CHUNK)
        s = s_scr[rows, :n_keys]
        e = jnp.exp((s - jnp.max(s, axis=-1, keepdims=True)) * scale)
        emit(rows, e, 1.0 / jnp.sum(e, axis=-1, keepdims=True))


def _attn_fwd_block(q, k, v, scale, s_scr, e_scr, l_scr):
    n = k.shape[0]
    s_scr[:, :n] = _dg(q, k, 1, 1)

    def emit(rows, e, inv_l):
        e_scr[rows, :n] = e.astype(BF16)
        l_scr[rows, :] = jnp.broadcast_to(inv_l, (ROW_CHUNK, LANE))

    _softmax_chunks(s_scr, n, scale, emit)
    return _dg(e_scr[:, :n], v, 1, 0) * l_scr[...]


def _attn_bwd_block(q, k, v, o, do, scale, s_scr, dp_scr, p_scr, ds_scr):
    n = k.shape[0]
    s_scr[:, :n] = _dg(q, k, 1, 1)
    dp_scr[:, :n] = _dg(do, v, 1, 1)

    def emit(rows, e, inv_l):
        p = e * inv_l
        delta = jnp.sum(do[rows, :] * o[rows, :], axis=-1, keepdims=True)
        p_scr[rows, :n] = p.astype(BF16)
        ds_scr[rows, :n] = (p * (dp_scr[rows, :n] - delta) * scale).astype(BF16)

    _softmax_chunks(s_scr, n, scale, emit)
    ds = ds_scr[:, :n]
    return _dg(ds, k, 1, 0), _dg(ds, q, 0, 0), _dg(p_scr[:, :n], do, 0, 0)


def _call_with_exchange(body, xchg, *, name, grid, in_specs, out_specs, out_shape, operands, scratch_shapes=()):
    if xchg is None:
        res = _pcall(body, name=name, grid=grid, in_specs=in_specs, out_specs=out_specs, out_shape=out_shape,
                     scratch_shapes=list(scratch_shapes), compiler_params=_cparams(len(grid)))(*operands)
        return list(res), []
    n_in, n_out, n_sc, n = len(in_specs), len(out_specs), len(scratch_shapes), xchg.n

    def wrapped(*refs):
        ins, x_refs = refs[:n_in], refs[n_in:n_in + n]
        outs, xo_refs = refs[n_in + n:n_in + n + n_out], refs[n_in + n + n_out:n_in + 2 * n + n_out]
        scratch, sems = refs[n_in + 2 * n + n_out:n_in + 2 * n + n_out + n_sc], refs[n_in + 2 * n + n_out + n_sc:]
        ids = [pl.program_id(i) for i in range(len(grid))]

        @pl.when(functools.reduce(jnp.logical_and, [i == 0 for i in ids]))
        def _():
            xchg.start(x_refs, xo_refs, sems)

        body(*ins, *outs, *scratch)

        @pl.when(functools.reduce(jnp.logical_and, [i == g - 1 for i, g in zip(ids, grid)]))
        def _():
            xchg.wait(x_refs, xo_refs, sems)

    res = _pcall(wrapped, name=name, grid=grid, in_specs=list(in_specs) + xchg.specs,
                 out_specs=list(out_specs) + xchg.specs, out_shape=list(out_shape) + xchg.out_shape,
                 scratch_shapes=list(scratch_shapes) + xchg.scratch, compiler_params=_cparams(len(grid)),
                 )(*operands, *xchg.bufs)
    return list(res[:n_out]), list(res[n_out:])


def _head_half(i, shape):
    lane = lax.broadcasted_iota(jnp.int32, shape, len(shape) - 1)
    return (lane < LANE // 2) if i == 0 else (lane >= LANE // 2)


def _mla_attn(q, k, v, tc, ctx_q, name, xchg=None):
    assert MLA_HPS == 2 and MLA_V == LANE // 2
    bsz, t_all, _ = q.shape
    n_t = t_all // TB
    grid = (bsz, MLA_HEADS // MLA_HPS, n_t)
    q_spec = pl.BlockSpec((None, TB, MLA_HPS * LANE), lambda b, h, t: (b, t, h))
    k_spec = pl.BlockSpec((None, t_all, MLA_HPS * LANE), lambda b, h, t: (b, 0, h))
    v_spec = pl.BlockSpec((None, t_all, LANE), lambda b, h, t: (b, 0, h))
    o_spec = pl.BlockSpec((None, TB, LANE), lambda b, h, t: (b, t, h))
    heads = [slice(i * LANE, (i + 1) * LANE) for i in range(MLA_HPS)]
    scale = MLA_QK ** -0.5
    f32_scr, bf16_scr = pltpu.VMEM((TB, t_all), F32), pltpu.VMEM((TB, t_all), BF16)
    o_shape = jax.ShapeDtypeStruct(v.shape, F32)

    def fwd_body(q_ref, k_ref, v_ref, o_ref, *scr):
        t = pl.program_id(2)

        def run(keys):
            both = [_attn_fwd_block(q_ref[:, hs], k_ref[keys, hs], v_ref[keys, :], scale, *scr[3 * i:3 * i + 3])
                    for i, hs in enumerate(heads)]
            o_ref[...] = jnp.where(_head_half(0, both[0].shape), both[0], both[1])

        @pl.when(t == 0)
        def _():
            if ctx_q:
                run(slice(0, tc))
            else:
                o_ref[...] = jnp.zeros_like(o_ref)

        @pl.when(t > 0)
        def _():
            run(slice(0, t_all))

    (o,), gathered = _call_with_exchange(
        fwd_body, xchg, name=name, grid=grid, in_specs=[q_spec, k_spec, v_spec], out_specs=[o_spec],
        out_shape=[o_shape], operands=(q, k, v),
        scratch_shapes=[f32_scr, bf16_scr, pltpu.VMEM((TB, LANE), F32)] * MLA_HPS)

    def bwd(do, xchg=None):
        def bwd_body(q_ref, k_ref, v_ref, o_ref, do_ref, dq_ref, dk_ref, dv_ref, *scr):
            t = pl.program_id(2)

            def run(keys, first):
                dvs = []
                for i, hs in enumerate(heads):
                    do_i = jnp.where(_head_half(i, do_ref.shape), do_ref[...], 0.0)
                    dq, dk, dv = _attn_bwd_block(q_ref[:, hs], k_ref[keys, hs], v_ref[keys, :], o_ref[...], do_i,
                                                 scale, *scr[4 * i:4 * i + 4])
                    dq_ref[:, hs] = dq
                    dvs.append(dv)
                    if first:
                        dk_ref[keys, hs] = dk
                    else:
                        dk_ref[keys, hs] += dk
                if first:
                    dv_ref[keys, :] = dvs[0] + dvs[1]
                else:
                    dv_ref[keys, :] += dvs[0] + dvs[1]

            @pl.when(t == 0)
            def _():
                dk_ref[...] = jnp.zeros_like(dk_ref)
                dv_ref[...] = jnp.zeros_like(dv_ref)
                if ctx_q:
                    run(slice(0, tc), True)
                else:
                    dq_ref[...] = jnp.zeros_like(dq_ref)

            @pl.when(t > 0)
            def _():
                run(slice(0, t_all), False)

        return _call_with_exchange(
            bwd_body, xchg, name=name + "_bwd", grid=grid, in_specs=[q_spec, k_spec, v_spec, o_spec, o_spec],
            out_specs=[q_spec, k_spec, v_spec],
            out_shape=[jax.ShapeDtypeStruct(q.shape, F32), jax.ShapeDtypeStruct(q.shape, F32), o_shape],
            operands=(q, k, v, o, do), scratch_shapes=[f32_scr, f32_scr, bf16_scr, bf16_scr] * MLA_HPS)

    return o, gathered, bwd


def _swa_block(q, keys, vals, sink, mask):
    qs = jnp.concatenate(list(_split(q, SWA_GROUP, 1)), axis=0)
    sk = jnp.sum(sink, axis=-1, keepdims=True) * (1.0 / LANE)
    s = _nt(qs, keys) * (SWA_HEAD_DIM ** -0.5)
    if mask is not None:
        s = jnp.where(mask, s, NEG_INF)
    o = _split(_nn(_softmax_rows(s, sk), vals + _roll(vals, LANE // 2, 1)), SWA_GROUP, 0)
    low = _head_half(0, o[0].shape)
    return jnp.concatenate([jnp.where(low, o[0], o[1]), jnp.where(low, o[2], o[3])], axis=1)


def _swa_ctx_block(q, kc, vc, sink):
    return _swa_block(q, kc, vc, sink, None)


def _swa_win_block(q, kc, kw, vc, vw, sink, mask):
    return _swa_block(q, jnp.concatenate([kc, kw], axis=0), jnp.concatenate([vc, vw], axis=0), sink, mask)


def _swa_attn(q, k, p_all, sink_b, tc, ctx_q, name, xchg=None):
    bsz, t_all, _ = q.shape
    n_q = t_all // QB_SWA
    n_cq = tc // QB_SWA
    lat = t_all - tc
    span = QB_SWA + 2 * WINDOW
    gw = SWA_GROUP * LANE
    grid = (bsz, SWA_KV_HEADS, n_q)
    q_spec = pl.BlockSpec((None, QB_SWA, gw), lambda b, g, i: (b, i, g))
    k_spec = pl.BlockSpec((None, t_all, LANE), lambda b, g, i: (b, 0, g))
    v_spec = pl.BlockSpec((None, t_all, LANE), lambda b, g, i: (b, 0, PC_SV // LANE + g))
    s_spec = pl.BlockSpec((None, SWA_GROUP * QB_SWA, LANE), lambda b, g, i: (g, 0, 0))

    def window(i):
        q0 = (i - n_cq) * QB_SWA
        w0 = jnp.clip(q0 - WINDOW, 0, lat - span)
        w0 = pl.multiple_of(w0, WINDOW)
        shape = (SWA_GROUP * QB_SWA, tc + span)
        qi = q0 + lax.broadcasted_iota(jnp.int32, shape, 0) % QB_SWA
        col = lax.broadcasted_iota(jnp.int32, shape, 1)
        kj = w0 + col - tc
        mask = (col < tc) | ((kj >= qi - WINDOW) & (kj <= qi + WINDOW))
        return w0, mask

    def fwd_body(q_ref, k_ref, v_ref, s_ref, o_ref):
        i = pl.program_id(2)

        @pl.when(i < n_cq)
        def _():
            if ctx_q:
                o_ref[...] = _swa_ctx_block(q_ref[...].astype(F32), k_ref[0:tc, :], v_ref[0:tc, :], s_ref[...])
            else:
                o_ref[...] = jnp.zeros_like(o_ref)

        @pl.when(i >= n_cq)
        def _():
            w0, mask = window(i)
            o_ref[...] = _swa_win_block(q_ref[...].astype(F32), k_ref[0:tc, :], k_ref[pl.ds(tc + w0, span), :],
                                        v_ref[0:tc, :], v_ref[pl.ds(tc + w0, span), :], s_ref[...], mask)

    o_spec = pl.BlockSpec((None, QB_SWA, SWA_GROUP * SWA_HEAD_DIM), lambda b, g, i: (b, i, g))
    (o,), gathered = _call_with_exchange(
        fwd_body, xchg, name=name, grid=grid, in_specs=[q_spec, k_spec, v_spec, s_spec], out_specs=[o_spec],
        out_shape=[jax.ShapeDtypeStruct((bsz, t_all, SWA_HEADS * SWA_HEAD_DIM), F32)],
        operands=(q, k, p_all, sink_b))

    def bwd(do, xchg=None):
        def bwd_body(q_ref, k_ref, v_ref, s_ref, do_ref, dq_ref, dk_ref, dv_ref, ds_ref):
            i = pl.program_id(2)

            @pl.when(i == 0)
            def _():
                dk_ref[...] = jnp.zeros_like(dk_ref)
                dv_ref[...] = jnp.zeros_like(dv_ref)
                ds_ref[...] = jnp.zeros_like(ds_ref)

            @pl.when(i < n_cq)
            def _():
                if ctx_q:
                    _, vjp = jax.vjp(_swa_ctx_block, q_ref[...].astype(F32), k_ref[0:tc, :].astype(F32),
                                     v_ref[0:tc, :], s_ref[...])
                    dq, dk, dv, ds = vjp(do_ref[...])
                    dq_ref[...] = dq
                    dk_ref[0:tc, :] += dk
                    dv_ref[0:tc, :] += dv
                    ds_ref[...] += ds
                else:
                    dq_ref[...] = jnp.zeros_like(dq_ref)

            @pl.when(i >= n_cq)
            def _():
                w0, mask = window(i)
                win = pl.ds(tc + w0, span)
                _, vjp = jax.vjp(functools.partial(_swa_win_block, mask=mask), q_ref[...].astype(F32),
                                 k_ref[0:tc, :].astype(F32), k_ref[win, :].astype(F32),
                                 v_ref[0:tc, :], v_ref[win, :], s_ref[...])
                dq, dkc, dkw, dvc, dvw, ds = vjp(do_ref[...])
                dq_ref[...] = dq
                dk_ref[0:tc, :] += dkc
                dk_ref[win, :] += dkw
                dv_ref[0:tc, :] += dvc
                dv_ref[win, :] += dvw
                ds_ref[...] += ds

        kv_out = pl.BlockSpec((None, t_all, LANE), lambda b, g, i: (b, 0, g))
        ds_spec = pl.BlockSpec((None, None, SWA_GROUP * QB_SWA, LANE), lambda b, g, i: (b, g, 0, 0))
        kv_shape = jax.ShapeDtypeStruct((bsz, t_all, SWA_KV_HEADS * LANE), F32)
        return _call_with_exchange(
            bwd_body, xchg, name=name + "_bwd", grid=grid, in_specs=[q_spec, k_spec, v_spec, s_spec, o_spec],
            out_specs=[q_spec, kv_out, kv_out, ds_spec],
            out_shape=[jax.ShapeDtypeStruct(q.shape, F32), kv_shape, kv_shape,
                       jax.ShapeDtypeStruct((bsz,) + sink_b.shape, F32)],
            operands=(q, k, p_all, sink_b, do))

    return o, gathered, bwd


def _scan_pair(chains, scratch):
    t_all, c = chains[0][0].shape
    n_tiles = t_all // SUBLANE
    row8 = lax.broadcasted_iota(jnp.int32, (t_all, c), 0) % SUBLANE
    refs = [scratch[0:3], scratch[3:6]]
    for (a, u, reverse), (a_s, u_s, _) in zip(chains, refs):
        for d in (1, 2, 4):
            sh = d if not reverse else t_all - d
            ar, ur = pltpu.roll(a, sh, 0), pltpu.roll(u, sh, 0)
            m = (row8 >= d) if not reverse else (row8 < SUBLANE - d)
            u = jnp.where(m, a * ur + u, u)
            a = jnp.where(m, a * ar, a)
        a_s[...] = a
        u_s[...] = u

    def step(j, carries):
        out = []
        for (_, _, reverse), (a_s, u_s, c_s), carry in zip(chains, refs, carries):
            tile = j if not reverse else n_tiles - 1 - j
            base = pl.multiple_of(tile * SUBLANE, SUBLANE)
            c_s[pl.ds(base, SUBLANE), :] = jnp.broadcast_to(carry, (SUBLANE, c))
            last = base + (0 if reverse else SUBLANE - 1)
            out.append(a_s[pl.ds(last, 1), :] * carry + u_s[pl.ds(last, 1), :])
        return tuple(out)

    lax.fori_loop(0, n_tiles, step, (jnp.zeros((1, c), F32),) * 2, unroll=4)
    return [a_s[...] * c_s[...] + u_s[...] for a_s, u_s, c_s in refs]


def _shift_rows(x, reverse_src):
    t_all = x.shape[0]
    row = lax.broadcasted_iota(jnp.int32, x.shape, 0)
    if reverse_src:
        return jnp.where(row == t_all - 1, 0.0, pltpu.roll(x, t_all - 1, 0))
    return jnp.where(row == 0, 0.0, pltpu.roll(x, 1, 0))


def _lru_scan(a0, u0, a1, u1, name):
    bsz, t_all, w = a0.shape
    grid = (bsz, w // LANE)
    spec = pl.BlockSpec((None, t_all, LANE), lambda b, c: (b, 0, c))
    scratch = [pltpu.VMEM((t_all, LANE), F32)] * 6
    shape = jax.ShapeDtypeStruct(a0.shape, F32)

    def fwd_body(a0_ref, u0_ref, a1_ref, u1_ref, h0_ref, h1_ref, *scr):
        h0_ref[...], h1_ref[...] = _scan_pair([(a0_ref[...], u0_ref[...], False), (a1_ref[...], u1_ref[...], True)],
                                              scr)

    h0, h1 = _pcall(fwd_body, name=name, grid=grid, in_specs=[spec] * 4, out_specs=[spec] * 2,
                    out_shape=[shape] * 2, scratch_shapes=scratch, compiler_params=_cparams(2))(a0, u0, a1, u1)

    def bwd(dh0, dh1):
        def bwd_body(a0_ref, h0_ref, g0_ref, a1_ref, h1_ref, g1_ref, da0_ref, du0_ref, da1_ref, du1_ref, *scr):
            g0, g1 = _scan_pair([(_shift_rows(a0_ref[...], True), g0_ref[...], True),
                                 (_shift_rows(a1_ref[...], False), g1_ref[...], False)], scr)
            du0_ref[...] = g0
            da0_ref[...] = g0 * _shift_rows(h0_ref[...], False)
            du1_ref[...] = g1
            da1_ref[...] = g1 * _shift_rows(h1_ref[...], True)

        return _pcall(bwd_body, name=name + "_bwd", grid=grid, in_specs=[spec] * 6, out_specs=[spec] * 4,
                      out_shape=[shape] * 4, scratch_shapes=scratch,
                      compiler_params=_cparams(2))(a0, h0, dh0, a1, h1, dh1)

    return h0, h1, bwd


def _f_mod(x, g, shift, scale):
    return (_rms(x, g, D_MODEL) * (1.0 + scale) + shift,)


def _f_mla_q(cq, ga, w, gh, cos, sa, sb):
    n = _rms(cq, ga, MLA_Q_RANK)
    outs = []
    for wh in _split(w, MLA_HEADS, 1):
        outs.append(_rope(_rms(_nn(n, wh), gh, MLA_QK), cos, sa, sb, MLA_ROPE // 4))
    return (jnp.concatenate(outs, axis=1),)


def _f_mla_kv(ckv, krp, ga, wk, wv, gh, cos, sa, sb):
    n = _rms(ckv, ga, MLA_KV_RANK)
    outs = []
    for wh in _split(wk, MLA_HEADS, 1):
        outs.append(_rope(_rms(_nn(n, wh) + krp, gh, MLA_QK), cos, sa, sb, MLA_ROPE // 4))
    return jnp.concatenate(outs, axis=1), _nn(n, wv)


def _f_conv(x, w0, w1, w2, w3, bias, tc):
    t_all = x.shape[0]
    row = lax.broadcasted_iota(jnp.int32, x.shape, 0)
    lo = jnp.where(row < tc, 0, tc)
    hi = jnp.where(row < tc, tc, t_all)
    y = bias + jnp.zeros_like(x)
    for kk, wk in enumerate((w0, w1, w2, w3)):
        src = row + (kk - 2)
        xs = x if kk == 2 else _roll(x, 2 - kk, 0)
        y = y + wk * jnp.where((src >= lo) & (src < hi), xs, 0.0)
    return (y,)


def _f_gates(xc, w16, b00, b01, b10, b11, sp0, sp1):
    ws = _unstack(w16)
    n_cb = LRU_WIDTH // LANE
    xcs = _split(xc, n_cb, 1)
    bias = [_split(b, n_cb, 1) for b in (b00, b01, b10, b11)]
    sps = [_split(s, n_cb, 1) for s in (sp0, sp1)]
    res = [[], [], [], []]
    for c in range(n_cb):
        for z in range(2):
            r = _sig(_nn(xcs[c], ws[c * 4 + 2 * z]) + bias[2 * z][c])
            i = _sig(_nn(xcs[c], ws[c * 4 + 2 * z + 1]) + bias[2 * z + 1][c])
            la = -LRU_C * r * sps[z][c]
            res[2 * z].append(jnp.exp(la))
            res[2 * z + 1].append(jnp.sqrt(-jnp.tanh(la) * (jnp.exp(2.0 * la) + 1.0)) * (i * xcs[c]))
    return tuple(jnp.concatenate(r, axis=1) for r in res)


def _f_swa_qk(sq, sk, gq, gk, cos, sa, sb):
    qs = [_rope(_rms(x, gq, SWA_HEAD_DIM), cos, sa, sb, SWA_HEAD_DIM // 4) for x in _split(sq, SWA_HEADS, 1)]
    ks = [_rope(_rms(x, gk, SWA_HEAD_DIM), cos, sa, sb, SWA_HEAD_DIM // 4) for x in _split(sk, SWA_KV_HEADS, 1)]
    return jnp.concatenate(qs, axis=1), jnp.concatenate(ks, axis=1)


def _f_qkv(cq, ckv, krp, sq, sk, q_a_g, wuq, mla_q_g, kv_a_g, wk, wv, mla_k_g, swa_q_g, swa_k_g,
           m_cos, m_sa, m_sb, s_cos, s_sa, s_sb):
    return (*_f_mla_q(cq, q_a_g, wuq, mla_q_g, m_cos, m_sa, m_sb),
            *_f_mla_kv(ckv, krp, kv_a_g, wk, wv, mla_k_g, m_cos, m_sa, m_sb),
            *_f_swa_qk(sq, sk, swa_q_g, swa_k_g, s_cos, s_sa, s_sb))


def _f_merge(oa, h0, h1, lg, oc, ga, gb, gc):
    ob = (h0 + h1) * _gelu(lg)
    return (jnp.concatenate([_rms(oa, ga, GROUP_WIDTH), _rms(ob, gb, GROUP_WIDTH), _rms(oc, gc, GROUP_WIDTH)],
                            axis=1),)


def _f_resid_mod(x, y, gate, g, shift, scale):
    x1 = x + gate * y
    return x1, _rms(x1, g, D_MODEL) * (1.0 + scale) + shift


def _f_resid(x, y, gate):
    return (x + gate * y,)


def _hosted(hooks, key, arg=None):
    make, done = hooks.get(key, (None, None))
    xchg = make(arg) if make is not None else None
    return xchg, (done if xchg is not None else lambda outs: None)


def _layer(li, x, mods, w, s, tabs, tc, ctx_q, hooks):
    bsz, t_all, _ = x.shape
    n_t = t_all // TB
    grid = (bsz, n_t)
    rows = lambda b, t: (b, t, 0)

    def row(arr, width=None, idx=0, gdtype=F32, gshape=None):
        width = width or arr.shape[-1]
        return _A(arr, (None, TB, width), lambda b, t: (b, t, idx), "row", gdtype=gdtype, gshape=gshape,
                  gimap=rows if gshape is not None else None)

    def out(width, dtype, imap=rows):
        return ((bsz, t_all, width), dtype, (None, TB, width), imap)

    def modarg(arr):
        return _A(arr, (None, None, 1, D_MODEL), lambda b, t: (b, jnp.minimum(t, 1), 0, 0), "acc",
                  first=lambda b, t: t <= 1)

    def tab(arr):
        return _A(arr, (TB, LANE), lambda b, t: (t, 0), "const")

    def pcol(p_all, col, width):
        return row(p_all, width, col // width, gdtype=BF16, gshape=(bsz, t_all, width))

    nm = lambda base: "%s_l%d" % (base, li)
    sh1, sc1, g1, sh2, sc2, g2 = mods
    m_all = bsz * t_all

    (h,), b_mod1 = _rowop(nm("mod1"), _f_mod, grid, [row(x), _par(s["norm1_g"]), modarg(sh1), modarg(sc1)],
                          [out(D_MODEL, BF16)])
    p_all = _mm(h.reshape(m_all, D_MODEL), w["win"], "nn", F32, nm("mm_in")).reshape(bsz, t_all, P_WIDTH)

    (q_a, k_a, v_a, q_c, k_c), b_qkv = _rowop(
        nm("qkv"), _f_qkv, grid,
        [pcol(p_all, PC_CQ, 256), pcol(p_all, PC_CKV, 128), pcol(p_all, PC_KR, 128), pcol(p_all, PC_SQ, 1024),
         pcol(p_all, PC_SK, 256)]
        + [_par(a) for a in (s["q_a_g"], w["wuq"], s["mla_q_g"], s["kv_a_g"], w["wk"], w["wv"], s["mla_k_g"],
                             s["swa_q_g"], s["swa_k_g"])]
        + [tab(a) for a in tabs["mla"] + tabs["swa"]],
        [out(MLA_HEADS * LANE, BF16), out(MLA_HEADS * LANE, BF16), out(MLA_HEADS * MLA_V, BF16),
         out(SWA_HEADS * LANE, BF16), out(SWA_KV_HEADS * LANE, BF16)])

    xchg, done = _hosted(hooks, "mla_fwd")
    o_a, got, b_attn_a = _mla_attn(q_a, k_a, v_a, tc, ctx_q, nm("mla_attn"), xchg)
    done(got)

    n_cb = LRU_WIDTH // LANE
    conv_grid = (n_cb, bsz)
    cpar = lambda arr: _A(arr, (1, LANE), lambda c, b: (0, c), "acc", first=lambda c, b: b == 0)
    conv_args = [_A(p_all, (None, t_all, LANE), lambda c, b: (b, 0, PC_LX // LANE + c), "row", gdtype=BF16,
                    gshape=(bsz, t_all, LRU_WIDTH), gimap=lambda c, b: (b, 0, c))]
    conv_args += [cpar(a) for a in s["conv_w"]] + [cpar(s["conv_b"])]
    conv_out = [((bsz, t_all, LRU_WIDTH), F32, (None, t_all, LANE), lambda c, b: (b, 0, c))]
    (xc,), b_conv = _rowop(nm("lru_conv"), functools.partial(_f_conv, tc=tc), conv_grid, conv_args, conv_out)
    rot = lambda b, t: (b, (t + n_t - 1) % n_t, 0)
    (a0, u0, a1, u1), b_gates = _rowop(
        nm("lru_gates"), _f_gates, grid,
        [row(xc), _par(s["wbd"])] + [_par(a) for a in s["gate_b"]] + [_par(a) for a in s["sp"]],
        [out(LRU_WIDTH, F32), out(LRU_WIDTH, F32), out(LRU_WIDTH, F32, rot), out(LRU_WIDTH, F32, rot)])
    h0, h1, b_scan = _lru_scan(a0, u0, a1, u1, nm("lru_scan"))
    h1_arg = _A(h1, (None, TB, LRU_WIDTH), rot, "row")

    xchg, done = _hosted(hooks, "swa_fwd")
    o_c, got, b_attn_c = _swa_attn(q_c, k_c, p_all, s["sink_b"], tc, ctx_q, nm("swa_attn"), xchg)
    done(got)

    (y_in,), b_merge = _rowop(nm("merge"), _f_merge, grid,
                              [row(o_a), row(h0), h1_arg, pcol(p_all, PC_LG, 512), row(o_c), _par(s["g_a"]),
                               _par(s["g_b"]), _par(s["g_c"])],
                              [out(MIX_P, BF16)])
    y = _mm(y_in.reshape(m_all, MIX_P), w["wout"], "nn", F32, nm("mm_out")).reshape(bsz, t_all, D_MODEL)
    (x1, hm), b_rm = _rowop(nm("resid_mod"), _f_resid_mod, grid,
                            [row(x), row(y, gdtype=BF16), modarg(g1), _par(s["norm2_g"]), modarg(sh2), modarg(sc2)],
                            [out(D_MODEL, F32), out(D_MODEL, BF16)])
    pre, act = _mm(hm.reshape(m_all, D_MODEL), w["ff1"], "nn", BF16, nm("mm_ff1"), epi="sqrelu")
    y2 = _mm(act, w["ff2"], "nn", F32, nm("mm_ff2")).reshape(bsz, t_all, D_MODEL)
    (x2,), b_res = _rowop(nm("resid"), _f_resid, grid,
                          [_A(x1, (None, TB, D_MODEL), rows, "fwd"), row(y2, gdtype=BF16), modarg(g2)],
                          [out(D_MODEL, F32)])

    def bwd(dx2, hooks):
        dw, ds = {}, {}
        dy2, dg2 = b_res(dx2)
        dy2 = dy2.reshape(m_all, D_MODEL)
        dpre = _mm(dy2, w["ff2"], "nt", BF16, nm("mm_ff2_dx"), epi="dsqrelu", aux=pre)
        dw["ff2"] = _mm(act, dy2, "tn", BF16, nm("mm_ff2_dw"))
        dhm = _mm(dpre, w["ff1"], "nt", F32, nm("mm_ff1_dx")).reshape(bsz, t_all, D_MODEL)
        dw["ff1"] = _mm(hm.reshape(m_all, D_MODEL), dpre, "tn", BF16, nm("mm_ff1_dw"))
        dxa, dy, dg1, ds["norm2_g"], dsh2, dsc2 = b_rm(dx2, dhm)
        dy = dy.reshape(m_all, D_MODEL)
        dy_in = _mm(dy, w["wout"], "nt", F32, nm("mm_out_dx")).reshape(bsz, t_all, MIX_P)
        dw["wout"] = _mm(y_in.reshape(m_all, MIX_P), dy, "tn", BF16, nm("mm_out_dw"))
        do_a, dh0, dh1, dlg, do_c, ds["g_a"], ds["g_b"], ds["g_c"] = b_merge(dy_in)

        (dq_c, dk_c, dsv, dsink), _ = b_attn_c(do_c)
        ds["sink_b"] = jnp.sum(dsink, axis=0)

        da0, du0, da1, du1 = b_scan(dh0, dh1)
        gates_g = b_gates(da0, du0, da1, du1)
        dxc, ds["wbd"] = gates_g[0], gates_g[1]
        ds["gate_b"], ds["sp"] = list(gates_g[2:6]), list(gates_g[6:8])
        conv_g = b_conv(dxc)
        dlx, ds["conv_w"], ds["conv_b"] = conv_g[0], list(conv_g[1:5]), conv_g[5]

        xchg, done = _hosted(hooks, "mla_bwd", dw)
        (dq_a, dk_a, dv_a), got = b_attn_a(do_a, xchg)
        done(got)
        (dcq, dckv, dkr, dsq, dsk, ds["q_a_g"], dw["wuq"], ds["mla_q_g"], ds["kv_a_g"], dw["wk"], dw["wv"],
         ds["mla_k_g"], ds["swa_q_g"], ds["swa_k_g"]) = b_qkv(dq_a, dk_a, dv_a, dq_c, dk_c)

        dp = jnp.concatenate([dsq, dlx, dlg, dcq, dsk, dsv.astype(BF16), dckv, dkr], axis=-1)
        dp = dp.reshape(m_all, P_WIDTH)
        dh = _mm(dp, w["win"], "nt", F32, nm("mm_in_dx")).reshape(bsz, t_all, D_MODEL)
        dw["win"] = _mm(h.reshape(m_all, D_MODEL), dp, "tn", BF16, nm("mm_in_dw"))
        dx, ds["norm1_g"], dsh1, dsc1 = b_mod1(dh, add_to_first=dxa)
        return dx, [dsh1, dsc1, dg1, dsh2, dsc2, dg2], dw, ds

    return x2, bwd


def _loss_and_grad(x2, target, tc):
    bsz, t_all, d = x2.shape
    n_t = t_all // TB
    n_c = tc // TB

    def body(x_ref, t_ref, l_ref, dx_ref):
        b, t = pl.program_id(0), pl.program_id(1)

        @pl.when((b == 0) & (t == 0))
        def _():
            l_ref[...] = jnp.zeros_like(l_ref)

        @pl.when(t < n_c)
        def _():
            dx_ref[...] = jnp.zeros_like(dx_ref)

        @pl.when(t >= n_c)
        def _():
            e = x_ref[...] - t_ref[...]
            dx_ref[...] = e * (1.0 / d)
            l_ref[...] += jnp.sum(e * e) * (0.5 / d)

    loss, dx = _pcall(
        body, name="loss", grid=(bsz, n_t),
        in_specs=[pl.BlockSpec((None, TB, d), lambda b, t: (b, t, 0)),
                  pl.BlockSpec((None, TB, d), lambda b, t: (b, jnp.maximum(t - n_c, 0), 0))],
        out_specs=[pl.BlockSpec((SUBLANE, LANE), lambda b, t: (0, 0)),
                   pl.BlockSpec((None, TB, d), lambda b, t: (b, t, 0))],
        out_shape=[jax.ShapeDtypeStruct((SUBLANE, LANE), F32), jax.ShapeDtypeStruct(x2.shape, F32)],
        compiler_params=_cparams(2))(x2, target)
    return loss[0, 0], dx


def _rope_tables(lat, tc, dim, lane0):
    quarter = dim // 4
    pos = np.arange(lat)
    grid_pos = np.stack([pos // GRID_W, pos % GRID_W], axis=-1).astype(np.float32)
    lane = np.arange(LANE)
    p = np.clip(lane - lane0, 0, dim - 1)
    active = (lane >= lane0) & (lane < lane0 + dim)
    axis, half, qi = p // (dim // 2), (p % (dim // 2)) // quarter, p % quarter
    inv = (np.float32(ROPE_THETA) ** (-qi.astype(np.float32) / np.float32(quarter))).astype(np.float32)
    ang = (np.where(axis[None, :] == 0, grid_pos[:, 0:1], grid_pos[:, 1:2]) * inv[None, :]).astype(np.float32)
    cos = np.where(active, np.cos(ang), 1.0).astype(np.float32)
    sin = np.where(active, np.sin(ang), 0.0).astype(np.float32)
    sa = np.where(half == 0, -sin, 0.0).astype(np.float32)
    sb = np.where(half == 1, sin, 0.0).astype(np.float32)
    ctx1, ctx0 = np.ones((tc, LANE), np.float32), np.zeros((tc, LANE), np.float32)
    return tuple(jnp.asarray(np.concatenate([c, t], 0)) for c, t in ((ctx1, cos), (ctx0, sa), (ctx0, sb)))


_BIG = {"w_in": ((D_MODEL, IN_WIDTH // N_DEV), 1, ("win",)),
        "w_uq": ((MLA_Q_RANK, MLA_HEADS * MLA_QK // N_DEV), 1, ("wuq",)),
        "w_ukv": ((MLA_KV_RANK, MLA_HEADS * (MLA_NOPE + MLA_V) // N_DEV), 1, ("wk", "wv")),
        "w_out": ((3 * GROUP_WIDTH // N_DEV, D_MODEL), 0, ("wout",)),
        "w_ff1": ((D_MODEL, D_FF // N_DEV), 1, ("ff1",)),
        "w_ff2": ((D_FF // N_DEV, D_MODEL), 0, ("ff2",))}
_EARLY = ("w_in", "w_uq", "w_ukv")
_LATE = ("w_out", "w_ff1", "w_ff2")


def _pad_heads(wm, n_heads, dim, axis=-1):
    axis = axis % wm.ndim
    shp = wm.shape[:axis] + (n_heads, dim) + wm.shape[axis + 1:]
    pad = [(0, 0)] * len(shp)
    pad[axis + 1] = (0, LANE - dim)
    out = jnp.pad(wm.reshape(shp), pad)
    return out.reshape(wm.shape[:axis] + (n_heads * LANE,) + wm.shape[axis + 1:])


def _prep_weight(name, piece):
    shp, ax, _ = _BIG[name]
    full = jnp.moveaxis(piece, 0, ax).reshape(shp[:ax] + (N_DEV * shp[ax],) + shp[ax + 1:])
    if name == "w_in":
        cq, ckv, kr, lx, lg, sq, sk, sv = _split_cols(full)
        return {"win": jnp.concatenate(
            [_pad_heads(sq, SWA_HEADS, SWA_HEAD_DIM), lx, lg, cq, _pad_heads(sk, SWA_KV_HEADS, SWA_HEAD_DIM),
             _pad_heads(sv, SWA_KV_HEADS, SWA_HEAD_DIM), ckv, jnp.pad(kr, ((0, 0), (MLA_NOPE, LANE - MLA_QK)))], axis=1)}
    if name == "w_uq":
        return {"wuq": _pad_heads(full, MLA_HEADS, MLA_QK)}
    if name == "w_ukv":
        ukv = full.reshape(MLA_KV_RANK, MLA_HEADS, MLA_NOPE + MLA_V)
        return {"wk": _pad_heads(ukv[:, :, :MLA_NOPE].reshape(MLA_KV_RANK, -1), MLA_HEADS, MLA_NOPE),
                "wv": ukv[:, :, MLA_NOPE:].reshape(MLA_KV_RANK, -1)}
    return {_BIG[name][2][0]: full}


def _split_cols(wm):
    parts, start = [], 0
    for size in IN_SIZES:
        parts.append(wm[:, start:start + size])
        start += size
    return parts


def _prep_small(raw):
    r1 = lambda a: a.reshape(1, -1)
    gw = raw["lru_gate_w"].reshape(2, 2, 4, 2, 64, 64)
    wbd = jnp.einsum("zgknCm,nN->knCzgNm", gw, jnp.eye(2, dtype=F32)).reshape(4, LANE, 4, LANE)
    gg = raw["group_g"]
    sink = raw["swa_sink"].reshape(SWA_KV_HEADS, SWA_GROUP, 1, 1)
    return {
        "norm1_g": r1(raw["norm1_g"]), "norm2_g": r1(raw["norm2_g"]),
        "q_a_g": r1(raw["q_a_g"]), "kv_a_g": r1(raw["kv_a_g"]),
        "mla_q_g": jnp.pad(r1(raw["mla_q_g"]), ((0, 0), (0, LANE - MLA_QK))),
        "mla_k_g": jnp.pad(r1(raw["mla_k_g"]), ((0, 0), (0, LANE - MLA_QK))),
        "swa_q_g": jnp.pad(r1(raw["swa_q_g"]), ((0, 0), (0, LANE - SWA_HEAD_DIM))),
        "swa_k_g": jnp.pad(r1(raw["swa_k_g"]), ((0, 0), (0, LANE - SWA_HEAD_DIM))),
        "conv_w": [r1(raw["conv_w"][kk]) for kk in range(4)], "conv_b": r1(raw["conv_b"]),
        "wbd": wbd.transpose(0, 2, 1, 3).reshape(16, LANE, LANE),
        "gate_b": [r1(raw["lru_gate_b"][z, g]) for z in range(2) for g in range(2)],
        "sp": [r1(jax.nn.softplus(-raw["lru_lambda"][z])) for z in range(2)],
        "sink_b": jnp.broadcast_to(sink, (SWA_KV_HEADS, SWA_GROUP, QB_SWA, LANE)).reshape(
            SWA_KV_HEADS, SWA_GROUP * QB_SWA, LANE),
        "g_a": r1(gg[:GROUP_WIDTH]), "g_b": r1(gg[GROUP_WIDTH:2 * GROUP_WIDTH]), "g_c": r1(gg[2 * GROUP_WIDTH:])}


def _mesh_pos():
    return lax.axis_index("x"), lax.axis_index("y"), lax.axis_index("c")


def _peer(pos, k):
    return tuple(1 - p if (k >> s) & 1 else p for p, s in zip(pos, (2, 1, 0)))


def _dev_index(pos):
    return 4 * pos[0] + 2 * pos[1] + pos[2]


class _Exchange:
    def __init__(self, bufs, gather):
        self.bufs = list(bufs)
        self.n = len(self.bufs)
        self.gather = [gather] * self.n if isinstance(gather, bool) else list(gather)
        self.specs = [pl.BlockSpec(memory_space=pl.ANY)] * self.n
        self.out_shape = [jax.ShapeDtypeStruct((N_DEV,) + tuple(b.shape if g else b.shape[1:]), b.dtype)
                          for b, g in zip(self.bufs, self.gather)]
        self.scratch = [pltpu.SemaphoreType.DMA(((N_DEV - 1) * self.n,)),
                        pltpu.SemaphoreType.DMA(((N_DEV - 1) * self.n,)), pltpu.SemaphoreType.DMA((self.n,))]

    def _copies(self, x_refs, o_refs, sems, with_recvs):
        send_sems, recv_sems, local_sems = sems
        pos = _mesh_pos()
        me = _dev_index(pos)
        locals_, sends, recvs = [], [], []
        for j in range(self.n):
            src_mine = x_refs[j] if self.gather[j] else x_refs[j].at[me]
            locals_.append(pltpu.make_async_copy(src_mine, o_refs[j].at[me], local_sems.at[j]))
        for k in range(1, N_DEV):
            peer = _peer(pos, k)
            pidx = _dev_index(peer)
            for j in range(self.n):
                src = x_refs[j] if self.gather[j] else x_refs[j].at[pidx]
                sem = (k - 1) * self.n + j
                sends.append(pltpu.make_async_remote_copy(
                    src_ref=src, dst_ref=o_refs[j].at[me], send_sem=send_sems.at[sem], recv_sem=recv_sems.at[sem],
                    device_id=peer, device_id_type=pl.DeviceIdType.MESH))
                if with_recvs:
                    recvs.append(pltpu.make_async_remote_copy(
                        src_ref=src, dst_ref=o_refs[j].at[pidx], send_sem=send_sems.at[sem],
                        recv_sem=recv_sems.at[sem], device_id=peer, device_id_type=pl.DeviceIdType.MESH))
        return locals_, sends, recvs

    def start(self, x_refs, o_refs, sems):
        locals_, sends, _ = self._copies(x_refs, o_refs, sems, False)
        for cp in locals_ + sends:
            cp.start()

    def wait(self, x_refs, o_refs, sems):
        locals_, sends, recvs = self._copies(x_refs, o_refs, sems, True)
        for cp in recvs:
            cp.wait_recv()
        for cp in sends:
            cp.wait_send()
        for cp in locals_:
            cp.wait()


def _exchange(bufs, gather, name):
    xchg = _Exchange(bufs, gather)
    n = xchg.n

    def body(*refs):
        xchg.start(refs[:n], refs[n:2 * n], refs[2 * n:])
        xchg.wait(refs[:n], refs[n:2 * n], refs[2 * n:])

    return _pcall(body, name=name, out_shape=xchg.out_shape, in_specs=xchg.specs, out_specs=xchg.specs,
                  scratch_shapes=xchg.scratch)(*xchg.bufs)


def _pack(arrs, dtype):
    flat = jnp.concatenate([a.reshape(-1).astype(dtype) for a in arrs])
    rows = -(-flat.size // PACK_W)
    rows = -(-rows // 16) * 16
    return jnp.pad(flat, (0, rows * PACK_W - flat.size)).reshape(rows, PACK_W)


def _unpack(buf, shapes, lead=()):
    flat = buf.reshape(lead + (-1,))
    out, off = [], 0
    for shp in shapes:
        n = math.prod(shp)
        out.append(flat[..., off:off + n].reshape(lead + tuple(shp)))
        off += n
    return out


def _sum_sources(buf, name):
    _, r, c = buf.shape
    tr = _rows_tile(r)

    def body(x_ref, o_ref):
        acc = x_ref[0]
        for d in range(1, N_DEV):
            acc = acc + x_ref[d]
        o_ref[...] = acc

    return _pcall(body, name=name, grid=(r // tr,),
                  in_specs=[pl.BlockSpec((N_DEV, tr, c), lambda i: (0, i, 0))],
                  out_specs=pl.BlockSpec((tr, c), lambda i: (i, 0)),
                  out_shape=jax.ShapeDtypeStruct((r, c), F32), compiler_params=_cparams(1))(buf)


def _rows_tile(r):
    best = r
    for t in range(SUBLANE, 257, SUBLANE):
        if r % t == 0:
            best = t
    return best


def _adamw(grads, wgt, m, v, name):
    n_lay = len(grads)
    n_src, r, c = grads[0].shape
    tr = _rows_tile(r)
    n_blk = r // tr
    bc1 = 1.0 - ADAM_B1 ** ADAM_STEP
    bc2 = 1.0 - ADAM_B2 ** ADAM_STEP

    def body(*refs):
        g_refs, (w_ref, m_ref, v_ref, go_ref, d_ref, mo_ref, vo_ref) = refs[:n_lay], refs[n_lay:]
        for li, g_ref in enumerate(g_refs):
            @pl.when(pl.program_id(0) == li)
            def _():
                g = g_ref[0].astype(F32)
                for d in range(1, n_src):
                    g = g + g_ref[d].astype(F32)
                m_new = ADAM_B1 * m_ref[...] + (1.0 - ADAM_B1) * g
                v_new = ADAM_B2 * v_ref[...] + (1.0 - ADAM_B2) * (g * g)
                go_ref[...] = g
                mo_ref[...] = m_new
                vo_ref[...] = v_new
                d_ref[...] = -ADAM_LR * ((m_new / bc1) / (jnp.sqrt(v_new / bc2) + ADAM_EPS) + ADAM_WD * w_ref[...])

    g_specs = [pl.BlockSpec((n_src, tr, c),
                            lambda l, i, li=li: (0, jnp.where(l == li, i, jnp.where(l > li, n_blk - 1, 0)), 0))
               for li in range(n_lay)]
    spec = pl.BlockSpec((tr, c), lambda l, i: (l * n_blk + i, 0))
    return _pcall(body, name=name, grid=(n_lay, n_blk), in_specs=g_specs + [spec, spec, spec],
                  out_specs=[spec] * 4, out_shape=[jax.ShapeDtypeStruct((n_lay * r, c), F32)] * 4,
                  compiler_params=_cparams(2))(*grads, wgt, m, v)


def _silu(z):
    return z * jax.nn.sigmoid(z)


_WEIGHTS = ("c_ctx", "w_mod", "b_mod", "norm1_g", "w_in", "q_a_g", "w_uq", "kv_a_g", "w_ukv", "mla_q_g", "mla_k_g",
            "conv_w", "conv_b", "lru_gate_w", "lru_gate_b", "lru_lambda", "swa_q_g", "swa_k_g", "swa_sink", "group_g",
            "w_out", "norm2_g", "w_ff1", "w_ff2")
_SHARDED_SMALL = ("conv_w", "lru_gate_b", "lru_lambda")
_REPL_RAW = ("norm1_g", "q_a_g", "kv_a_g", "mla_q_g", "mla_k_g", "conv_b", "lru_gate_w", "swa_q_g", "swa_k_g",
             "swa_sink", "group_g", "norm2_g")
MOD_ROWS = 32


def kernel(x, c, ctx, c_ctx, w_mod, b_mod, norm1_g, w_in, q_a_g, w_uq, kv_a_g, w_ukv, mla_q_g, mla_k_g, conv_w, conv_b, lru_gate_w, lru_gate_b, lru_lambda, swa_q_g, swa_k_g, swa_sink, group_g, w_out, norm2_g, w_ff1, w_ff2, loss_target, m_c_ctx, m_w_mod, m_b_mod, m_norm1_g, m_w_in, m_q_a_g, m_w_uq, m_kv_a_g, m_w_ukv, m_mla_q_g, m_mla_k_g, m_conv_w, m_conv_b, m_lru_gate_w, m_lru_gate_b, m_lru_lambda, m_swa_q_g, m_swa_k_g, m_swa_sink, m_group_g, m_w_out, m_norm2_g, m_w_ff1, m_w_ff2, v_c_ctx, v_w_mod, v_b_mod, v_norm1_g, v_w_in, v_q_a_g, v_w_uq, v_kv_a_g, v_w_ukv, v_mla_q_g, v_mla_k_g, v_conv_w, v_conv_b, v_lru_gate_w, v_lru_gate_b, v_lru_lambda, v_swa_q_g, v_swa_k_g, v_swa_sink, v_group_g, v_w_out, v_norm2_g, v_w_ff1, v_w_ff2):
    wts = dict(c_ctx=c_ctx, w_mod=w_mod, b_mod=b_mod, norm1_g=norm1_g, w_in=w_in, q_a_g=q_a_g, w_uq=w_uq,
               kv_a_g=kv_a_g, w_ukv=w_ukv, mla_q_g=mla_q_g, mla_k_g=mla_k_g, conv_w=conv_w, conv_b=conv_b,
               lru_gate_w=lru_gate_w, lru_gate_b=lru_gate_b, lru_lambda=lru_lambda, swa_q_g=swa_q_g, swa_k_g=swa_k_g,
               swa_sink=swa_sink, group_g=group_g, w_out=w_out, norm2_g=norm2_g, w_ff1=w_ff1, w_ff2=w_ff2)
    mom1 = dict(zip(_WEIGHTS, (m_c_ctx, m_w_mod, m_b_mod, m_norm1_g, m_w_in, m_q_a_g, m_w_uq, m_kv_a_g, m_w_ukv,
                               m_mla_q_g, m_mla_k_g, m_conv_w, m_conv_b, m_lru_gate_w, m_lru_gate_b, m_lru_lambda,
                               m_swa_q_g, m_swa_k_g, m_swa_sink, m_group_g, m_w_out, m_norm2_g, m_w_ff1, m_w_ff2)))
    mom2 = dict(zip(_WEIGHTS, (v_c_ctx, v_w_mod, v_b_mod, v_norm1_g, v_w_in, v_q_a_g, v_w_uq, v_kv_a_g, v_w_ukv,
                               v_mla_q_g, v_mla_k_g, v_conv_w, v_conv_b, v_lru_gate_w, v_lru_gate_b, v_lru_lambda,
                               v_swa_q_g, v_swa_k_g, v_swa_sink, v_group_g, v_w_out, v_norm2_g, v_w_ff1, v_w_ff2)))
    bsz = x.shape[0]
    n_ex = bsz * N_DEV
    me = _dev_index(_mesh_pos())
    mod_cols = w_mod.shape[-1]

    small_shapes = [c.shape, conv_w.shape, lru_gate_b.shape, lru_lambda.shape]
    shard = lambda n, li: wts[n][li].astype(BF16)
    g_small, *early_pieces = _exchange([_pack([c, conv_w, lru_gate_b, lru_lambda], F32)] + [shard(n, 0) for n in _EARLY],
                                       True, "ag_first")
    c_all, conv_w_all, gate_b_all, lam_all = _unpack(g_small, small_shapes, lead=(N_DEV,))
    c_all = c_all.reshape(n_ex, D_MODEL)
    cat_last = lambda a: jnp.moveaxis(a, 0, -2).reshape(a.shape[1:-1] + (N_DEV * a.shape[-1],))
    conv_w_full, gate_b_full, lam_full = cat_last(conv_w_all), cat_last(gate_b_all), cat_last(lam_all)

    act = jnp.zeros((MOD_ROWS, D_MODEL), F32).at[:n_ex].set(_silu(c_all)).at[n_ex].set(_silu(c_ctx))
    mod_part = jnp.concatenate([_mm(act, w_mod[li], "nn", F32, "mm_mod_l%d" % li) for li in range(DEPTH)], axis=1)
    (mod_all,) = _exchange([mod_part], True, "ag_mod")
    mods = []
    for li in range(DEPTH):
        full = jnp.moveaxis(mod_all[:, :, li * mod_cols:(li + 1) * mod_cols], 0, 1).reshape(MOD_ROWS, -1) + b_mod[li]
        mine = lax.dynamic_slice_in_dim(full, me * bsz, bsz, axis=0)
        ctx_row = jnp.broadcast_to(full[n_ex], mine.shape)
        both = jnp.stack([ctx_row, mine], axis=1).reshape(bsz, 2, N_MOD, 1, D_MODEL)
        mods.append([both[:, :, j] for j in range(N_MOD)])

    raw = {n: wts[n] for n in _REPL_RAW}
    raw.update(conv_w=conv_w_full, lru_gate_b=gate_b_full, lru_lambda=lam_full)
    small_names = list(_REPL_RAW) + list(_SHARDED_SMALL)
    sp, small_vjp = [None] * DEPTH, [None] * DEPTH
    for li in range(DEPTH):
        sp[li], small_vjp[li] = jax.vjp(_prep_small, {n: raw[n][li] for n in small_names})

    w, w_vjp, g_recv, small_recv = [{} for _ in range(DEPTH)], {}, {}, {}

    def take(li, names, pieces):
        for n, piece in zip(names, pieces):
            out, w_vjp[n, li] = jax.vjp(functools.partial(_prep_weight, n), piece)
            w[li].update(out)

    def gather_hook(li, names):
        return (lambda _: _Exchange([shard(n, li) for n in names], True), lambda got: take(li, names, got))

    def wgrad(n, li, dwl):
        (g,) = w_vjp[n, li]({k: dwl[k].astype(BF16) for k in _BIG[n][2]})
        return g

    pack_names = [n for n in small_names if n != "lru_gate_w"]

    def small_bufs(li, ds_l, extra=()):
        (d_raw,) = small_vjp[li](ds_l)
        return [_pack([d_raw[n] for n in pack_names] + list(extra), F32), d_raw["lru_gate_w"].reshape(-1, LANE)]

    take(0, _EARLY, early_pieces)
    hooks_fwd = [{"mla_fwd": gather_hook(0, _LATE), "swa_fwd": gather_hook(1, _EARLY + ("w_out",))},
                 {"mla_fwd": gather_hook(1, ("w_ff1", "w_ff2"))}]
    bwd_state = {}

    def scatter_last_layer(dwl):
        return _Exchange([wgrad(n, 1, dwl) for n in _LATE], False)

    def scatter_first_layer(dwl):
        dw1, ds1 = bwd_state["dw1"], bwd_state["ds1"]
        bufs = [wgrad(n, 1, dw1) for n in _EARLY] + [wgrad(n, 0, dwl) for n in _LATE] + small_bufs(1, ds1)
        return _Exchange(bufs, [False] * (len(_EARLY) + len(_LATE)) + [True, True])

    def scattered_first_layer(got):
        g_recv.update(zip([(n, 1) for n in _EARLY] + [(n, 0) for n in _LATE], got[:-2]))
        small_recv[1], g_recv["lru_gate_w", 1] = got[-2:]

    hooks_bwd = [{"mla_bwd": (scatter_first_layer, scattered_first_layer)},
                 {"mla_bwd": (scatter_last_layer, lambda got: g_recv.update(zip([(n, 1) for n in _LATE], got)))}]

    tc, lat = ctx.shape[1], x.shape[1]
    tabs = {"mla": _rope_tables(lat, tc, MLA_ROPE, MLA_NOPE), "swa": _rope_tables(lat, tc, SWA_HEAD_DIM, 0)}
    stream = jnp.concatenate([ctx, x], axis=1)
    bwds = []
    for li in range(DEPTH):
        stream, bwd = _layer(li, stream, mods[li], w[li], sp[li], tabs, tc, li < DEPTH - 1, hooks_fwd[li])
        bwds.append(bwd)
    loss_part, dstream = _loss_and_grad(stream, loss_target, tc)
    dmods = [None] * DEPTH
    dstream, dmods[1], bwd_state["dw1"], bwd_state["ds1"] = bwds[1](dstream, hooks_bwd[1])
    dstream, dmods[0], dw0, ds0 = bwds[0](dstream, hooks_bwd[0])
    grad_x = dstream[:, tc:]

    dm_rows = []
    for li in range(DEPTH):
        dm = jnp.concatenate(dmods[li], axis=-1)
        dm_rows.append(jnp.concatenate([dm[:, 1, 0], jnp.sum(dm[:, 0, 0], axis=0, keepdims=True)], axis=0))
    dm_mine = jnp.concatenate(dm_rows, axis=1)
    dm_mine = jnp.pad(dm_mine, ((0, SUBLANE - bsz - 1), (0, 0)))
    (dm_all,) = _exchange([dm_mine], True, "ag_dmod")
    g_wmod, g_bmod, dact_ctx = [], [], jnp.zeros((D_MODEL,), F32)
    for li in range(DEPTH):
        part = dm_all[:, :, li * N_MOD * D_MODEL:(li + 1) * N_MOD * D_MODEL]
        dm32 = jnp.zeros((MOD_ROWS, N_MOD * D_MODEL), F32).at[:n_ex].set(part[:, :bsz].reshape(n_ex, -1))
        dm32 = dm32.at[n_ex].set(jnp.sum(part[:, bsz], axis=0))
        g_bmod.append(jnp.sum(dm32, axis=0))
        cols = lax.dynamic_slice_in_dim(dm32, me * mod_cols, mod_cols, axis=1)
        g_wmod.append(_mm(act, cols, "tn", F32, "mm_mod_dw_l%d" % li))
        dact_ctx = dact_ctx + _mm(cols, w_mod[li], "nt", F32, "mm_mod_dx_l%d" % li)[n_ex]
    sg = jax.nn.sigmoid(c_ctx)
    g_cctx_part = dact_ctx * (sg * (1.0 + c_ctx * (1.0 - sg)))

    last = _exchange([wgrad(n, 0, dw0) for n in _EARLY] + small_bufs(0, ds0, (g_cctx_part, loss_part.reshape(1))),
                     [False] * len(_EARLY) + [True, True], "rs_early")
    g_recv.update(zip([(n, 0) for n in _EARLY], last[:-2]))
    small_recv[0], g_recv["lru_gate_w", 0] = last[-2:]
    layer_shapes = [raw[n].shape[1:] for n in pack_names]
    tot = [_unpack(_sum_sources(small_recv[li], "sum_grads_l%d" % li), layer_shapes + [(D_MODEL,), (1,)][:2 * (li == 0)])
           for li in range(DEPTH)]
    grads = {n: jnp.stack([tot[li][j] for li in range(DEPTH)], axis=0) for j, n in enumerate(pack_names)}
    grads["c_ctx"], loss = tot[0][-2], tot[0][-1][0]
    for n in _SHARDED_SMALL:
        width = wts[n].shape[-1]
        grads[n] = lax.dynamic_slice_in_dim(grads[n], me * width, width, axis=grads[n].ndim - 1)
    grads["b_mod"] = jnp.stack(g_bmod, axis=0)

    delta, new_m, new_v = {}, {}, {}
    per_layer = {n: [g_recv[n, li] for li in range(DEPTH)] for n in list(_BIG) + ["lru_gate_w"]}
    per_layer["w_mod"] = [g[None] for g in g_wmod]
    for n, srcs in per_layer.items():
        two_d = (DEPTH * math.prod(wts[n].shape[1:-1]), wts[n].shape[-1])
        srcs = [s.reshape((s.shape[0], two_d[0] // DEPTH, two_d[1])) for s in srcs]
        res = _adamw(srcs, wts[n].reshape(two_d), mom1[n].reshape(two_d), mom2[n].reshape(two_d), "adamw_" + n)
        grads[n], delta[n], new_m[n], new_v[n] = [r.reshape(wts[n].shape) for r in res]
    rest = [n for n in _WEIGHTS if n not in delta]
    shapes = [wts[n].shape for n in rest]
    res = _adamw([_pack([grads[n] for n in rest], F32)[None]], _pack([wts[n] for n in rest], F32),
                 _pack([mom1[n] for n in rest], F32), _pack([mom2[n] for n in rest], F32), "adamw_small")
    for tgt, buf in zip((delta, new_m, new_v), res[1:]):
        tgt.update(zip(rest, _unpack(buf, shapes)))

    return (loss, grad_x, *[grads[n] for n in _WEIGHTS], *[delta[n] for n in _WEIGHTS],
            *[new_m[n] for n in _WEIGHTS], *[new_v[n] for n in _WEIGHTS])
```

```python
import functools
import math

import jax
import jax.numpy as jnp
import numpy as np
from jax import lax
from jax.experimental import pallas as pl
from jax.experimental.pallas import tpu as pltpu

F32, BF16 = jnp.float32, jnp.bfloat16

N_DEV = 8
DEPTH = 2
D_MODEL = 1024
D_FF = 4096
N_MOD = 6
GRID_W = 64
WINDOW = 128
ROPE_THETA = 10000.0
EPS = 1e-6
NEG_INF = -1e30
LRU_C = 8.0
LRU_WIDTH = 512
MLA_HEADS, MLA_NOPE, MLA_ROPE, MLA_V = 8, 64, 32, 64
MLA_QK = MLA_NOPE + MLA_ROPE
MLA_Q_RANK, MLA_KV_RANK = 256, 128
SWA_HEADS, SWA_KV_HEADS, SWA_GROUP, SWA_HEAD_DIM = 8, 2, 4, 64
GROUP_WIDTH = 512
IN_SIZES = (256, 128, 32, 512, 512, 512, 128, 128)
IN_WIDTH = sum(IN_SIZES)
ADAM_LR, ADAM_B1, ADAM_B2, ADAM_EPS, ADAM_WD, ADAM_STEP = 0.001, 0.9, 0.999, 1e-08, 0.01, 10

LANE = 128
SUBLANE = 8
TB = 256
QB_SWA = 256
PACK_W = 1024
MM_K_CAP = 4608
MLA_HPS = 2
VMEM_LIMIT = 56 * 1024 * 1024
P_WIDTH = 3072
PC_SQ, PC_LX, PC_LG, PC_CQ, PC_SK, PC_SV, PC_CKV, PC_KR = 0, 1024, 1536, 2048, 2304, 2560, 2816, 2944
MIX_P = 1536


def _pcall(body, **kw):
    return pl.pallas_call(body, **kw)


def _cparams(n_grid):
    return pltpu.CompilerParams(dimension_semantics=("arbitrary",) * n_grid, vmem_limit_bytes=VMEM_LIMIT)


def _dg(a, b, ca, cb):
    return lax.dot_general(a.astype(BF16), b.astype(BF16), (((ca,), (cb,)), ((), ())),
                           preferred_element_type=F32)


@jax.custom_vjp
def _nn(a, b):
    return _dg(a, b, 1, 0)


@jax.custom_vjp
def _nt(a, b):
    return _dg(a, b, 1, 1)


@jax.custom_vjp
def _tn(a, b):
    return _dg(a, b, 0, 0)


_nn.defvjp(lambda a, b: (_nn(a, b), (a, b)), lambda r, ct: (_nt(ct, r[1]), _tn(r[0], ct)))
_nt.defvjp(lambda a, b: (_nt(a, b), (a, b)), lambda r, ct: (_nn(ct, r[1]), _tn(ct, r[0])))
_tn.defvjp(lambda a, b: (_tn(a, b), (a, b)), lambda r, ct: (_nt(r[1], ct), _nn(r[0], ct)))


@functools.partial(jax.custom_vjp, nondiff_argnums=(1, 2))
def _roll(x, shift, axis):
    return pltpu.roll(x, shift % x.shape[axis], axis)


_roll.defvjp(lambda x, shift, axis: (_roll(x, shift, axis), None),
             lambda shift, axis, _, ct: (_roll(ct, -shift, axis),))


@functools.partial(jax.custom_vjp, nondiff_argnums=(1, 2))
def _split(x, n, axis):
    w = x.shape[axis] // n
    return tuple(lax.slice_in_dim(x, i * w, (i + 1) * w, axis=axis) for i in range(n))


_split.defvjp(lambda x, n, axis: (_split(x, n, axis), None),
              lambda n, axis, _, cts: (jnp.concatenate(cts, axis=axis),))


@jax.custom_vjp
def _unstack(x):
    return tuple(x[i] for i in range(x.shape[0]))


_unstack.defvjp(lambda x: (_unstack(x), None), lambda _, cts: (jnp.stack(cts, axis=0),))


def _sig(x):
    return 0.5 * (jnp.tanh(0.5 * x) + 1.0)


def _gelu(x):
    return 0.5 * x * (1.0 + jnp.tanh(math.sqrt(2.0 / math.pi) * (x + 0.044715 * (x * x * x))))


def _rms(x, g, n):
    ms = jnp.sum(x * x, axis=-1, keepdims=True) * (1.0 / n)
    return x * lax.rsqrt(ms + EPS) * g


def _rope(y, cos, sa, sb, quarter):
    return y * cos + _roll(y, -quarter, 1) * sa + _roll(y, quarter, 1) * sb


def _softmax_rows(s, extra=None):
    m = jnp.max(s, axis=-1, keepdims=True)
    if extra is not None:
        m = jnp.maximum(m, extra)
    m = lax.stop_gradient(m)
    e = jnp.exp(s - m)
    den = jnp.sum(e, axis=-1, keepdims=True)
    if extra is not None:
        den = den + jnp.exp(extra - m)
    return e / den


class _A:
    def __init__(self, arr, block, imap, kind="row", first=None, gdtype=F32, gshape=None, gimap=None):
        self.arr, self.block, self.imap, self.kind, self.first = arr, block, imap, kind, first
        self.gdtype, self.gshape, self.gimap = gdtype, gshape, gimap


def _all_zero(*ids):
    return functools.reduce(jnp.logical_and, [i == 0 for i in ids])


def _par(arr):
    nd = arr.ndim
    return _A(arr, arr.shape, lambda *ids: (0,) * nd, "acc", first=_all_zero)


def _op_fwd(name, fn, grid, args, outs):
    n_in = len(args)

    def body(*refs):
        vals = [r[...].astype(F32) for r in refs[:n_in]]
        for r, v in zip(refs[n_in:], fn(*vals)):
            r[...] = v.astype(r.dtype)

    return _pcall(
        body, name=name, grid=grid,
        in_specs=[pl.BlockSpec(a.block, a.imap) for a in args],
        out_specs=[pl.BlockSpec(o[2], o[3]) for o in outs],
        out_shape=[jax.ShapeDtypeStruct(o[0], o[1]) for o in outs],
        compiler_params=_cparams(len(grid)),
    )(*[a.arr for a in args])


def _op_bwd(name, fn, grid, args, outs, ct_arrays, add_to_first=None):
    didx = [i for i, a in enumerate(args) if a.kind not in ("const", "fwd")]
    read = [i for i, a in enumerate(args) if a.kind != "fwd"]
    n_in, n_ct = len(read), len(outs)
    n_add = 0 if add_to_first is None else 1

    def body(*refs):
        ids = [pl.program_id(i) for i in range(len(grid))]
        vals = [jnp.zeros([d for d in a.block if d is not None], F32) for a in args]
        for i, r in zip(read, refs[:n_in]):
            vals[i] = r[...].astype(F32)

        def g(*dv):
            full = list(vals)
            for i, v in zip(didx, dv):
                full[i] = v
            return tuple(fn(*full))

        _, vjp = jax.vjp(g, *[vals[i] for i in didx])
        grads = list(vjp(tuple(r[...].astype(F32) for r in refs[n_in:n_in + n_ct])))
        if n_add:
            grads[0] = grads[0] + refs[n_in + n_ct][...]
        for gr, i, r in zip(grads, didx, refs[n_in + n_ct + n_add:]):
            a = args[i]
            if a.kind == "row":
                r[...] = gr.astype(r.dtype)
            else:
                first = a.first(*ids)

                @pl.when(first)
                def _():
                    r[...] = gr

                @pl.when(jnp.logical_not(first))
                def _():
                    r[...] += gr

    g_specs, g_shapes = [], []
    for i in didx:
        a = args[i]
        if a.kind == "row":
            g_specs.append(pl.BlockSpec(a.block, a.gimap or a.imap))
            g_shapes.append(jax.ShapeDtypeStruct(a.gshape or a.arr.shape, a.gdtype))
        else:
            g_specs.append(pl.BlockSpec(a.block, a.imap))
            g_shapes.append(jax.ShapeDtypeStruct(a.arr.shape, F32))
    return _pcall(
        body, name=name, grid=grid,
        in_specs=[pl.BlockSpec(args[i].block, args[i].imap) for i in read] + [pl.BlockSpec(o[2], o[3]) for o in outs]
        + g_specs[:n_add],
        out_specs=g_specs, out_shape=g_shapes,
        compiler_params=_cparams(len(grid)),
    )(*[args[i].arr for i in read], *ct_arrays, *([add_to_first] if n_add else []))


def _rowop(name, fn, grid, args, outs):
    res = _op_fwd(name, fn, grid, args, outs)
    return res, lambda *cts, add_to_first=None: _op_bwd(name + "_bwd", fn, grid, args, outs, cts, add_to_first)


def _pick(n, cap):
    best = None
    for t in range(LANE, cap + 1, LANE):
        if n % t == 0:
            best = t
    return best or n


def _mm(a, b, mode, out_dtype, name, epi=None, aux=None):
    if mode == "nn":
        (m, k), n = a.shape, b.shape[1]
    elif mode == "nt":
        (m, k), n = a.shape, b.shape[0]
    else:
        (k, m), n = a.shape, b.shape[1]
    tm = 512 if m % 512 == 0 else m
    tn, tk = _pick(n, 1024), _pick(k, MM_K_CAP)
    nk = k // tk
    if mode == "tn":
        a_spec = pl.BlockSpec((tk, tm), lambda j, i, kk: (kk, i))
    else:
        a_spec = pl.BlockSpec((tm, tk), lambda j, i, kk: (i, kk))
    if mode == "nt":
        b_spec = pl.BlockSpec((tn, tk), lambda j, i, kk: (j, kk))
    else:
        b_spec = pl.BlockSpec((tk, tn), lambda j, i, kk: (kk, j))
    dims = {"nn": (1, 0), "nt": (1, 1), "tn": (0, 0)}[mode]
    o_spec = pl.BlockSpec((tm, tn), lambda j, i, kk: (i, j))
    n_aux = 0 if aux is None else 1
    n_out = 2 if epi == "sqrelu" else 1

    def body(*refs):
        a_ref, b_ref = refs[0], refs[1]
        o_refs = refs[2 + n_aux:2 + n_aux + n_out]
        acc = refs[-1]
        kk = pl.program_id(2)
        part = _dg(a_ref[...], b_ref[...], *dims)

        if nk > 1:
            @pl.when(kk == 0)
            def _():
                acc[...] = part

            @pl.when((kk > 0) & (kk < nk - 1))
            def _():
                acc[...] += part

        @pl.when(kk == nk - 1)
        def _():
            r = part if nk == 1 else acc[...] + part
            if epi == "sqrelu":
                o_refs[0][...] = r.astype(o_refs[0].dtype)
                rl = jnp.maximum(r, 0.0)
                o_refs[1][...] = (rl * rl).astype(o_refs[1].dtype)
            elif epi == "dsqrelu":
                pre = refs[2][...].astype(F32)
                o_refs[0][...] = (r * (2.0 * jnp.maximum(pre, 0.0))).astype(o_refs[0].dtype)
            else:
                o_refs[0][...] = r.astype(o_refs[0].dtype)

    res = _pcall(
        body, name=name, grid=(n // tn, m // tm, nk),
        in_specs=[a_spec, b_spec] + [o_spec] * n_aux,
        out_specs=[o_spec] * n_out,
        out_shape=[jax.ShapeDtypeStruct((m, n), out_dtype)] * n_out,
        scratch_shapes=[pltpu.VMEM((tm, tn), F32)],
        compiler_params=_cparams(3),
    )(a, b, *([aux] if aux is not None else []))
    return res if n_out == 2 else res[0]


ROW_CHUNK = 16


def _softmax_chunks(s_scr, n_keys, scale, emit):
    for r0 in range(0, s_scr.shape[0], ROW_CHUNK):
        rows = slice(r0, r0 + ROW_CHUNK)
        s = s_scr[rows, :n_keys]
        e = jnp.exp((s - jnp.max(s, axis=-1, keepdims=True)) * scale)
        emit(rows, e, 1.0 / jnp.sum(e, axis=-1, keepdims=True))


def _attn_fwd_block(v, n, scale, s_scr, e_scr, l_scr):
    def emit(rows, e, inv_l):
        e_scr[rows, :n] = e.astype(BF16)
        l_scr[rows, :] = jnp.broadcast_to(inv_l, (ROW_CHUNK, LANE))

    _softmax_chunks(s_scr, n, scale, emit)
    return _dg(e_scr[:, :n], v, 1, 0) * l_scr[...]


def _attn_bwd_block(q, k, o, do, scale, s_scr, dp_scr, p_scr, ds_scr):
    n = k.shape[0]

    def emit(rows, e, inv_l):
        p = e * inv_l
        delta = jnp.sum(do[rows, :] * o[rows, :], axis=-1, keepdims=True)
        p_scr[rows, :n] = p.astype(BF16)
        ds_scr[rows, :n] = (p * (dp_scr[rows, :n] - delta) * scale).astype(BF16)

    _softmax_chunks(s_scr, n, scale, emit)
    ds = ds_scr[:, :n]
    return _dg(ds, k, 1, 0), _dg(ds, q, 0, 0), _dg(p_scr[:, :n], do, 0, 0)


def _call_with_exchange(body, xchg, *, name, grid, in_specs, out_specs, out_shape, operands, scratch_shapes=()):
    if xchg is None:
        res = _pcall(body, name=name, grid=grid, in_specs=in_specs, out_specs=out_specs, out_shape=out_shape,
                     scratch_shapes=list(scratch_shapes), compiler_params=_cparams(len(grid)))(*operands)
        return list(res), []
    n_in, n_out, n_sc, n = len(in_specs), len(out_specs), len(scratch_shapes), xchg.n

    def wrapped(*refs):
        ins, x_refs = refs[:n_in], refs[n_in:n_in + n]
        outs, xo_refs = refs[n_in + n:n_in + n + n_out], refs[n_in + n + n_out:n_in + 2 * n + n_out]
        scratch, sems = refs[n_in + 2 * n + n_out:n_in + 2 * n + n_out + n_sc], refs[n_in + 2 * n + n_out + n_sc:]
        ids = [pl.program_id(i) for i in range(len(grid))]

        @pl.when(functools.reduce(jnp.logical_and, [i == 0 for i in ids]))
        def _():
            xchg.start(x_refs, xo_refs, sems)

        body(*ins, *outs, *scratch)

        @pl.when(functools.reduce(jnp.logical_and, [i == g - 1 for i, g in zip(ids, grid)]))
        def _():
            xchg.wait(x_refs, xo_refs, sems)

    res = _pcall(wrapped, name=name, grid=grid, in_specs=list(in_specs) + xchg.specs,
                 out_specs=list(out_specs) + xchg.specs, out_shape=list(out_shape) + xchg.out_shape,
                 scratch_shapes=list(scratch_shapes) + xchg.scratch, compiler_params=_cparams(len(grid)),
                 )(*operands, *xchg.bufs)
    return list(res[:n_out]), list(res[n_out:])


def _head_half(i, shape):
    lane = lax.broadcasted_iota(jnp.int32, shape, len(shape) - 1)
    return (lane < LANE // 2) if i == 0 else (lane >= LANE // 2)


def _mla_attn(q, k, v, tc, ctx_q, name, xchg=None):
    assert MLA_HPS == 2 and MLA_V == LANE // 2
    bsz, t_all, _ = q.shape
    n_t = t_all // TB
    grid = (bsz, MLA_HEADS // MLA_HPS, n_t)
    q_spec = pl.BlockSpec((None, TB, MLA_HPS * LANE), lambda b, h, t: (b, t, h))
    k_spec = pl.BlockSpec((None, t_all, MLA_HPS * LANE), lambda b, h, t: (b, 0, h))
    v_spec = pl.BlockSpec((None, t_all, LANE), lambda b, h, t: (b, 0, h))
    o_spec = pl.BlockSpec((None, TB, LANE), lambda b, h, t: (b, t, h))
    heads = [slice(i * LANE, (i + 1) * LANE) for i in range(MLA_HPS)]
    scale = MLA_QK ** -0.5
    f32_scr, bf16_scr = pltpu.VMEM((TB, t_all), F32), pltpu.VMEM((TB, t_all), BF16)
    o_shape = jax.ShapeDtypeStruct(v.shape, F32)

    def fwd_body(q_ref, k_ref, v_ref, o_ref, *scr):
        t = pl.program_id(2)

        def run(keys):
            n = keys.stop
            for i, hs in enumerate(heads):
                scr[3 * i][:, :n] = _dg(q_ref[:, hs], k_ref[keys, hs], 1, 1)
            both = [_attn_fwd_block(v_ref[keys, :], n, scale, *scr[3 * i:3 * i + 3]) for i in range(MLA_HPS)]
            o_ref[...] = jnp.where(_head_half(0, both[0].shape), both[0], both[1])

        @pl.when(t == 0)
        def _():
            if ctx_q:
                run(slice(0, tc))
            else:
                o_ref[...] = jnp.zeros_like(o_ref)

        @pl.when(t > 0)
        def _():
            run(slice(0, t_all))

    (o,), gathered = _call_with_exchange(
        fwd_body, xchg, name=name, grid=grid, in_specs=[q_spec, k_spec, v_spec], out_specs=[o_spec],
        out_shape=[o_shape], operands=(q, k, v),
        scratch_shapes=[f32_scr, bf16_scr, pltpu.VMEM((TB, LANE), F32)] * MLA_HPS)

    def bwd(do, xchg=None):
        def bwd_body(q_ref, k_ref, v_ref, o_ref, do_ref, dq_ref, dk_ref, dv_ref, *scr):
            t = pl.program_id(2)

            def run(keys, first):
                n = keys.stop
                dos = [jnp.where(_head_half(i, do_ref.shape), do_ref[...], 0.0) for i in range(MLA_HPS)]
                for i, hs in enumerate(heads):
                    scr[4 * i][:, :n] = _dg(q_ref[:, hs], k_ref[keys, hs], 1, 1)
                    scr[4 * i + 1][:, :n] = _dg(dos[i], v_ref[keys, :], 1, 1)
                dvs = []
                for i, hs in enumerate(heads):
                    dq, dk, dv = _attn_bwd_block(q_ref[:, hs], k_ref[keys, hs], o_ref[...], dos[i], scale,
                                                 *scr[4 * i:4 * i + 4])
                    dq_ref[:, hs] = dq
                    dvs.append(dv)
                    if first:
                        dk_ref[keys, hs] = dk
                    else:
                        dk_ref[keys, hs] += dk
                if first:
                    dv_ref[keys, :] = dvs[0] + dvs[1]
                else:
                    dv_ref[keys, :] += dvs[0] + dvs[1]

            @pl.when(t == 0)
            def _():
                dk_ref[...] = jnp.zeros_like(dk_ref)
                dv_ref[...] = jnp.zeros_like(dv_ref)
                if ctx_q:
                    run(slice(0, tc), True)
                else:
                    dq_ref[...] = jnp.zeros_like(dq_ref)

            @pl.when(t > 0)
            def _():
                run(slice(0, t_all), False)

        return _call_with_exchange(
            bwd_body, xchg, name=name + "_bwd", grid=grid, in_specs=[q_spec, k_spec, v_spec, o_spec, o_spec],
            out_specs=[q_spec, k_spec, v_spec],
            out_shape=[jax.ShapeDtypeStruct(q.shape, F32), jax.ShapeDtypeStruct(q.shape, F32), o_shape],
            operands=(q, k, v, o, do), scratch_shapes=[f32_scr, f32_scr, bf16_scr, bf16_scr] * MLA_HPS)

    return o, gathered, bwd


def _swa_block(q, keys, vals, sink, mask):
    qs = jnp.concatenate(list(_split(q, SWA_GROUP, 1)), axis=0)
    sk = jnp.sum(sink, axis=-1, keepdims=True) * (1.0 / LANE)
    s = _nt(qs, keys) * (SWA_HEAD_DIM ** -0.5)
    if mask is not None:
        s = jnp.where(mask, s, NEG_INF)
    o = _split(_nn(_softmax_rows(s, sk), vals + _roll(vals, LANE // 2, 1)), SWA_GROUP, 0)
    low = _head_half(0, o[0].shape)
    return jnp.concatenate([jnp.where(low, o[0], o[1]), jnp.where(low, o[2], o[3])], axis=1)


def _swa_ctx_block(q, kc, vc, sink):
    return _swa_block(q, kc, vc, sink, None)


def _swa_win_block(q, kc, kw, vc, vw, sink, mask):
    return _swa_block(q, jnp.concatenate([kc, kw], axis=0), jnp.concatenate([vc, vw], axis=0), sink, mask)


def _swa_attn(q, k, p_all, sink_b, tc, ctx_q, name, xchg=None):
    bsz, t_all, _ = q.shape
    n_q = t_all // QB_SWA
    n_cq = tc // QB_SWA
    lat = t_all - tc
    span = QB_SWA + 2 * WINDOW
    gw = SWA_GROUP * LANE
    grid = (bsz, SWA_KV_HEADS, n_q)
    q_spec = pl.BlockSpec((None, QB_SWA, gw), lambda b, g, i: (b, i, g))
    k_spec = pl.BlockSpec((None, t_all, LANE), lambda b, g, i: (b, 0, g))
    v_spec = pl.BlockSpec((None, t_all, LANE), lambda b, g, i: (b, 0, PC_SV // LANE + g))
    s_spec = pl.BlockSpec((None, SWA_GROUP * QB_SWA, LANE), lambda b, g, i: (g, 0, 0))

    def window(i):
        q0 = (i - n_cq) * QB_SWA
        w0 = jnp.clip(q0 - WINDOW, 0, lat - span)
        w0 = pl.multiple_of(w0, WINDOW)
        shape = (SWA_GROUP * QB_SWA, tc + span)
        qi = q0 + lax.broadcasted_iota(jnp.int32, shape, 0) % QB_SWA
        col = lax.broadcasted_iota(jnp.int32, shape, 1)
        kj = w0 + col - tc
        mask = (col < tc) | ((kj >= qi - WINDOW) & (kj <= qi + WINDOW))
        return w0, mask

    def fwd_body(q_ref, k_ref, v_ref, s_ref, o_ref):
        i = pl.program_id(2)

        @pl.when(i < n_cq)
        def _():
            if ctx_q:
                o_ref[...] = _swa_ctx_block(q_ref[...].astype(F32), k_ref[0:tc, :], v_ref[0:tc, :], s_ref[...])
            else:
                o_ref[...] = jnp.zeros_like(o_ref)

        @pl.when(i >= n_cq)
        def _():
            w0, mask = window(i)
            o_ref[...] = _swa_win_block(q_ref[...].astype(F32), k_ref[0:tc, :], k_ref[pl.ds(tc + w0, span), :],
                                        v_ref[0:tc, :], v_ref[pl.ds(tc + w0, span), :], s_ref[...], mask)

    o_spec = pl.BlockSpec((None, QB_SWA, SWA_GROUP * SWA_HEAD_DIM), lambda b, g, i: (b, i, g))
    (o,), gathered = _call_with_exchange(
        fwd_body, xchg, name=name, grid=grid, in_specs=[q_spec, k_spec, v_spec, s_spec], out_specs=[o_spec],
        out_shape=[jax.ShapeDtypeStruct((bsz, t_all, SWA_HEADS * SWA_HEAD_DIM), F32)],
        operands=(q, k, p_all, sink_b))

    def bwd(do, xchg=None):
        def bwd_body(q_ref, k_ref, v_ref, s_ref, do_ref, dq_ref, dk_ref, dv_ref, ds_ref):
            i = pl.program_id(2)

            @pl.when(i == 0)
            def _():
                dk_ref[...] = jnp.zeros_like(dk_ref)
                dv_ref[...] = jnp.zeros_like(dv_ref)
                ds_ref[...] = jnp.zeros_like(ds_ref)

            @pl.when(i < n_cq)
            def _():
                if ctx_q:
                    _, vjp = jax.vjp(_swa_ctx_block, q_ref[...].astype(F32), k_ref[0:tc, :].astype(F32),
                                     v_ref[0:tc, :], s_ref[...])
                    dq, dk, dv, ds = vjp(do_ref[...])
                    dq_ref[...] = dq
                    dk_ref[0:tc, :] += dk
                    dv_ref[0:tc, :] += dv
                    ds_ref[...] += ds
                else:
                    dq_ref[...] = jnp.zeros_like(dq_ref)

            @pl.when(i >= n_cq)
            def _():
                w0, mask = window(i)
                win = pl.ds(tc + w0, span)
                _, vjp = jax.vjp(functools.partial(_swa_win_block, mask=mask), q_ref[...].astype(F32),
                                 k_ref[0:tc, :].astype(F32), k_ref[win, :].astype(F32),
                                 v_ref[0:tc, :], v_ref[win, :], s_ref[...])
                dq, dkc, dkw, dvc, dvw, ds = vjp(do_ref[...])
                dq_ref[...] = dq
                dk_ref[0:tc, :] += dkc
                dk_ref[win, :] += dkw
                dv_ref[0:tc, :] += dvc
                dv_ref[win, :] += dvw
                ds_ref[...] += ds

        kv_out = pl.BlockSpec((None, t_all, LANE), lambda b, g, i: (b, 0, g))
        ds_spec = pl.BlockSpec((None, None, SWA_GROUP * QB_SWA, LANE), lambda b, g, i: (b, g, 0, 0))
        kv_shape = jax.ShapeDtypeStruct((bsz, t_all, SWA_KV_HEADS * LANE), F32)
        return _call_with_exchange(
            bwd_body, xchg, name=name + "_bwd", grid=grid, in_specs=[q_spec, k_spec, v_spec, s_spec, o_spec],
            out_specs=[q_spec, kv_out, kv_out, ds_spec],
            out_shape=[jax.ShapeDtypeStruct(q.shape, F32), kv_shape, kv_shape,
                       jax.ShapeDtypeStruct((bsz,) + sink_b.shape, F32)],
            operands=(q, k, p_all, sink_b, do))

    return o, gathered, bwd


def _scan_pair(chains, scratch):
    t_all, c = chains[0][0].shape
    n_tiles = t_all // SUBLANE
    row8 = lax.broadcasted_iota(jnp.int32, (t_all, c), 0) % SUBLANE
    refs = [scratch[0:3], scratch[3:6]]
    for (a, u, reverse), (a_s, u_s, _) in zip(chains, refs):
        for d in (1, 2, 4):
            sh = d if not reverse else t_all - d
            ar, ur = pltpu.roll(a, sh, 0), pltpu.roll(u, sh, 0)
            m = (row8 >= d) if not reverse else (row8 < SUBLANE - d)
            u = jnp.where(m, a * ur + u, u)
            a = jnp.where(m, a * ar, a)
        a_s[...] = a
        u_s[...] = u

    def step(j, carries):
        out = []
        for (_, _, reverse), (a_s, u_s, c_s), carry in zip(chains, refs, carries):
            tile = j if not reverse else n_tiles - 1 - j
            base = pl.multiple_of(tile * SUBLANE, SUBLANE)
            c_s[pl.ds(base, SUBLANE), :] = jnp.broadcast_to(carry, (SUBLANE, c))
            last = base + (0 if reverse else SUBLANE - 1)
            out.append(a_s[pl.ds(last, 1), :] * carry + u_s[pl.ds(last, 1), :])
        return tuple(out)

    lax.fori_loop(0, n_tiles, step, (jnp.zeros((1, c), F32),) * 2, unroll=4)
    return [a_s[...] * c_s[...] + u_s[...] for a_s, u_s, c_s in refs]


def _shift_rows(x, reverse_src):
    t_all = x.shape[0]
    row = lax.broadcasted_iota(jnp.int32, x.shape, 0)
    if reverse_src:
        return jnp.where(row == t_all - 1, 0.0, pltpu.roll(x, t_all - 1, 0))
    return jnp.where(row == 0, 0.0, pltpu.roll(x, 1, 0))


def _lru_scan(a0, u0, a1, u1, name):
    bsz, t_all, w = a0.shape
    grid = (bsz, w // LANE)
    spec = pl.BlockSpec((None, t_all, LANE), lambda b, c: (b, 0, c))
    scratch = [pltpu.VMEM((t_all, LANE), F32)] * 6
    shape = jax.ShapeDtypeStruct(a0.shape, F32)

    def fwd_body(a0_ref, u0_ref, a1_ref, u1_ref, h0_ref, h1_ref, *scr):
        h0_ref[...], h1_ref[...] = _scan_pair([(a0_ref[...], u0_ref[...], False), (a1_ref[...], u1_ref[...], True)],
                                              scr)

    h0, h1 = _pcall(fwd_body, name=name, grid=grid, in_specs=[spec] * 4, out_specs=[spec] * 2,
                    out_shape=[shape] * 2, scratch_shapes=scratch, compiler_params=_cparams(2))(a0, u0, a1, u1)

    def bwd(dh0, dh1):
        def bwd_body(a0_ref, h0_ref, g0_ref, a1_ref, h1_ref, g1_ref, da0_ref, du0_ref, da1_ref, du1_ref, *scr):
            g0, g1 = _scan_pair([(_shift_rows(a0_ref[...], True), g0_ref[...], True),
                                 (_shift_rows(a1_ref[...], False), g1_ref[...], False)], scr)
            du0_ref[...] = g0
            da0_ref[...] = g0 * _shift_rows(h0_ref[...], False)
            du1_ref[...] = g1
            da1_ref[...] = g1 * _shift_rows(h1_ref[...], True)

        return _pcall(bwd_body, name=name + "_bwd", grid=grid, in_specs=[spec] * 6, out_specs=[spec] * 4,
                      out_shape=[shape] * 4, scratch_shapes=scratch,
                      compiler_params=_cparams(2))(a0, h0, dh0, a1, h1, dh1)

    return h0, h1, bwd


def _f_mod(x, g, shift, scale):
    return (_rms(x, g, D_MODEL) * (1.0 + scale) + shift,)


def _f_mla_q(cq, ga, w, gh, cos, sa, sb):
    n = _rms(cq, ga, MLA_Q_RANK)
    outs = []
    for wh in _split(w, MLA_HEADS, 1):
        outs.append(_rope(_rms(_nn(n, wh), gh, MLA_QK), cos, sa, sb, MLA_ROPE // 4))
    return (jnp.concatenate(outs, axis=1),)


def _f_mla_kv(ckv, krp, ga, wk, wv, gh, cos, sa, sb):
    n = _rms(ckv, ga, MLA_KV_RANK)
    outs = []
    for wh in _split(wk, MLA_HEADS, 1):
        outs.append(_rope(_rms(_nn(n, wh) + krp, gh, MLA_QK), cos, sa, sb, MLA_ROPE // 4))
    return jnp.concatenate(outs, axis=1), _nn(n, wv)


def _f_conv(x, w0, w1, w2, w3, bias, tc):
    t_all = x.shape[0]
    row = lax.broadcasted_iota(jnp.int32, x.shape, 0)
    lo = jnp.where(row < tc, 0, tc)
    hi = jnp.where(row < tc, tc, t_all)
    y = bias + jnp.zeros_like(x)
    for kk, wk in enumerate((w0, w1, w2, w3)):
        src = row + (kk - 2)
        xs = x if kk == 2 else _roll(x, 2 - kk, 0)
        y = y + wk * jnp.where((src >= lo) & (src < hi), xs, 0.0)
    return (y,)


def _f_gates(xc, w16, b00, b01, b10, b11, sp0, sp1):
    ws = _unstack(w16)
    n_cb = LRU_WIDTH // LANE
    xcs = _split(xc, n_cb, 1)
    bias = [_split(b, n_cb, 1) for b in (b00, b01, b10, b11)]
    sps = [_split(s, n_cb, 1) for s in (sp0, sp1)]
    res = [[], [], [], []]
    for c in range(n_cb):
        for z in range(2):
            r = _sig(_nn(xcs[c], ws[c * 4 + 2 * z]) + bias[2 * z][c])
            i = _sig(_nn(xcs[c], ws[c * 4 + 2 * z + 1]) + bias[2 * z + 1][c])
            la = -LRU_C * r * sps[z][c]
            res[2 * z].append(jnp.exp(la))
            res[2 * z + 1].append(jnp.sqrt(-jnp.tanh(la) * (jnp.exp(2.0 * la) + 1.0)) * (i * xcs[c]))
    return tuple(jnp.concatenate(r, axis=1) for r in res)


def _f_swa_qk(sq, sk, gq, gk, cos, sa, sb):
    qs = [_rope(_rms(x, gq, SWA_HEAD_DIM), cos, sa, sb, SWA_HEAD_DIM // 4) for x in _split(sq, SWA_HEADS, 1)]
    ks = [_rope(_rms(x, gk, SWA_HEAD_DIM), cos, sa, sb, SWA_HEAD_DIM // 4) for x in _split(sk, SWA_KV_HEADS, 1)]
    return jnp.concatenate(qs, axis=1), jnp.concatenate(ks, axis=1)


def _f_qkv(cq, ckv, krp, sq, sk, q_a_g, wuq, mla_q_g, kv_a_g, wk, wv, mla_k_g, swa_q_g, swa_k_g,
           m_cos, m_sa, m_sb, s_cos, s_sa, s_sb):
    return (*_f_mla_q(cq, q_a_g, wuq, mla_q_g, m_cos, m_sa, m_sb),
            *_f_mla_kv(ckv, krp, kv_a_g, wk, wv, mla_k_g, m_cos, m_sa, m_sb),
            *_f_swa_qk(sq, sk, swa_q_g, swa_k_g, s_cos, s_sa, s_sb))


def _f_merge(oa, h0, h1, lg, oc, ga, gb, gc):
    ob = (h0 + h1) * _gelu(lg)
    return (jnp.concatenate([_rms(oa, ga, GROUP_WIDTH), _rms(ob, gb, GROUP_WIDTH), _rms(oc, gc, GROUP_WIDTH)],
                            axis=1),)


def _f_resid_mod(x, y, gate, g, shift, scale):
    x1 = x + gate * y
    return x1, _rms(x1, g, D_MODEL) * (1.0 + scale) + shift


def _f_resid(x, y, gate):
    return (x + gate * y,)


def _hosted(hooks, key, arg=None):
    make, done = hooks.get(key, (None, None))
    xchg = make(arg) if make is not None else None
    return xchg, (done if xchg is not None else lambda outs: None)


def _layer(li, x, mods, w, s, tabs, tc, ctx_q, hooks):
    bsz, t_all, _ = x.shape
    n_t = t_all // TB
    grid = (bsz, n_t)
    rows = lambda b, t: (b, t, 0)

    def row(arr, width=None, idx=0, gdtype=F32, gshape=None):
        width = width or arr.shape[-1]
        return _A(arr, (None, TB, width), lambda b, t: (b, t, idx), "row", gdtype=gdtype, gshape=gshape,
                  gimap=rows if gshape is not None else None)

    def out(width, dtype, imap=rows):
        return ((bsz, t_all, width), dtype, (None, TB, width), imap)

    def modarg(arr):
        return _A(arr, (None, None, 1, D_MODEL), lambda b, t: (b, jnp.minimum(t, 1), 0, 0), "acc",
                  first=lambda b, t: t <= 1)

    def tab(arr):
        return _A(arr, (TB, LANE), lambda b, t: (t, 0), "const")

    def pcol(p_all, col, width):
        return row(p_all, width, col // width, gdtype=BF16, gshape=(bsz, t_all, width))

    nm = lambda base: "%s_l%d" % (base, li)
    sh1, sc1, g1, sh2, sc2, g2 = mods
    m_all = bsz * t_all

    (h,), b_mod1 = _rowop(nm("mod1"), _f_mod, grid, [row(x), _par(s["norm1_g"]), modarg(sh1), modarg(sc1)],
                          [out(D_MODEL, BF16)])
    p_all = _mm(h.reshape(m_all, D_MODEL), w["win"], "nn", F32, nm("mm_in")).reshape(bsz, t_all, P_WIDTH)

    (q_a, k_a, v_a, q_c, k_c), b_qkv = _rowop(
        nm("qkv"), _f_qkv, grid,
        [pcol(p_all, PC_CQ, 256), pcol(p_all, PC_CKV, 128), pcol(p_all, PC_KR, 128), pcol(p_all, PC_SQ, 1024),
         pcol(p_all, PC_SK, 256)]
        + [_par(a) for a in (s["q_a_g"], w["wuq"], s["mla_q_g"], s["kv_a_g"], w["wk"], w["wv"], s["mla_k_g"],
                             s["swa_q_g"], s["swa_k_g"])]
        + [tab(a) for a in tabs["mla"] + tabs["swa"]],
        [out(MLA_HEADS * LANE, BF16), out(MLA_HEADS * LANE, BF16), out(MLA_HEADS * MLA_V, BF16),
         out(SWA_HEADS * LANE, BF16), out(SWA_KV_HEADS * LANE, BF16)])

    xchg, done = _hosted(hooks, "mla_fwd")
    o_a, got, b_attn_a = _mla_attn(q_a, k_a, v_a, tc, ctx_q, nm("mla_attn"), xchg)
    done(got)

    n_cb = LRU_WIDTH // LANE
    conv_grid = (n_cb, bsz)
    cpar = lambda arr: _A(arr, (1, LANE), lambda c, b: (0, c), "acc", first=lambda c, b: b == 0)
    conv_args = [_A(p_all, (None, t_all, LANE), lambda c, b: (b, 0, PC_LX // LANE + c), "row", gdtype=BF16,
                    gshape=(bsz, t_all, LRU_WIDTH), gimap=lambda c, b: (b, 0, c))]
    conv_args += [cpar(a) for a in s["conv_w"]] + [cpar(s["conv_b"])]
    conv_out = [((bsz, t_all, LRU_WIDTH), F32, (None, t_all, LANE), lambda c, b: (b, 0, c))]
    (xc,), b_conv = _rowop(nm("lru_conv"), functools.partial(_f_conv, tc=tc), conv_grid, conv_args, conv_out)
    rot = lambda b, t: (b, (t + n_t - 1) % n_t, 0)
    (a0, u0, a1, u1), b_gates = _rowop(
        nm("lru_gates"), _f_gates, grid,
        [row(xc), _par(s["wbd"])] + [_par(a) for a in s["gate_b"]] + [_par(a) for a in s["sp"]],
        [out(LRU_WIDTH, F32), out(LRU_WIDTH, F32), out(LRU_WIDTH, F32, rot), out(LRU_WIDTH, F32, rot)])
    h0, h1, b_scan = _lru_scan(a0, u0, a1, u1, nm("lru_scan"))
    h1_arg = _A(h1, (None, TB, LRU_WIDTH), rot, "row")

    xchg, done = _hosted(hooks, "swa_fwd")
    o_c, got, b_attn_c = _swa_attn(q_c, k_c, p_all, s["sink_b"], tc, ctx_q, nm("swa_attn"), xchg)
    done(got)

    (y_in,), b_merge = _rowop(nm("merge"), _f_merge, grid,
                              [row(o_a), row(h0), h1_arg, pcol(p_all, PC_LG, 512), row(o_c), _par(s["g_a"]),
                               _par(s["g_b"]), _par(s["g_c"])],
                              [out(MIX_P, BF16)])
    y = _mm(y_in.reshape(m_all, MIX_P), w["wout"], "nn", F32, nm("mm_out")).reshape(bsz, t_all, D_MODEL)
    (x1, hm), b_rm = _rowop(nm("resid_mod"), _f_resid_mod, grid,
                            [row(x), row(y, gdtype=BF16), modarg(g1), _par(s["norm2_g"]), modarg(sh2), modarg(sc2)],
                            [out(D_MODEL, F32), out(D_MODEL, BF16)])
    pre, act = _mm(hm.reshape(m_all, D_MODEL), w["ff1"], "nn", BF16, nm("mm_ff1"), epi="sqrelu")
    y2 = _mm(act, w["ff2"], "nn", F32, nm("mm_ff2")).reshape(bsz, t_all, D_MODEL)
    (x2,), b_res = _rowop(nm("resid"), _f_resid, grid,
                          [_A(x1, (None, TB, D_MODEL), rows, "fwd"), row(y2, gdtype=BF16), modarg(g2)],
                          [out(D_MODEL, F32)])

    def bwd(dx2, hooks):
        dw, ds = {}, {}
        dy2, dg2 = b_res(dx2)
        dy2 = dy2.reshape(m_all, D_MODEL)
        dpre = _mm(dy2, w["ff2"], "nt", BF16, nm("mm_ff2_dx"), epi="dsqrelu", aux=pre)
        dw["ff2"] = _mm(act, dy2, "tn", BF16, nm("mm_ff2_dw"))
        dhm = _mm(dpre, w["ff1"], "nt", F32, nm("mm_ff1_dx")).reshape(bsz, t_all, D_MODEL)
        dw["ff1"] = _mm(hm.reshape(m_all, D_MODEL), dpre, "tn", BF16, nm("mm_ff1_dw"))
        dxa, dy, dg1, ds["norm2_g"], dsh2, dsc2 = b_rm(dx2, dhm)
        dy = dy.reshape(m_all, D_MODEL)
        dy_in = _mm(dy, w["wout"], "nt", F32, nm("mm_out_dx")).reshape(bsz, t_all, MIX_P)
        dw["wout"] = _mm(y_in.reshape(m_all, MIX_P), dy, "tn", BF16, nm("mm_out_dw"))
        do_a, dh0, dh1, dlg, do_c, ds["g_a"], ds["g_b"], ds["g_c"] = b_merge(dy_in)

        (dq_c, dk_c, dsv, dsink), _ = b_attn_c(do_c)
        ds["sink_b"] = jnp.sum(dsink, axis=0)

        da0, du0, da1, du1 = b_scan(dh0, dh1)
        gates_g = b_gates(da0, du0, da1, du1)
        dxc, ds["wbd"] = gates_g[0], gates_g[1]
        ds["gate_b"], ds["sp"] = list(gates_g[2:6]), list(gates_g[6:8])
        conv_g = b_conv(dxc)
        dlx, ds["conv_w"], ds["conv_b"] = conv_g[0], list(conv_g[1:5]), conv_g[5]

        xchg, done = _hosted(hooks, "mla_bwd", dw)
        (dq_a, dk_a, dv_a), got = b_attn_a(do_a, xchg)
        done(got)
        (dcq, dckv, dkr, dsq, dsk, ds["q_a_g"], dw["wuq"], ds["mla_q_g"], ds["kv_a_g"], dw["wk"], dw["wv"],
         ds["mla_k_g"], ds["swa_q_g"], ds["swa_k_g"]) = b_qkv(dq_a, dk_a, dv_a, dq_c, dk_c)

        dp = jnp.concatenate([dsq, dlx, dlg, dcq, dsk, dsv.astype(BF16), dckv, dkr], axis=-1)
        dp = dp.reshape(m_all, P_WIDTH)
        dh = _mm(dp, w["win"], "nt", F32, nm("mm_in_dx")).reshape(bsz, t_all, D_MODEL)
        dw["win"] = _mm(h.reshape(m_all, D_MODEL), dp, "tn", BF16, nm("mm_in_dw"))
        dx, ds["norm1_g"], dsh1, dsc1 = b_mod1(dh, add_to_first=dxa)
        return dx, [dsh1, dsc1, dg1, dsh2, dsc2, dg2], dw, ds

    return x2, bwd


def _loss_and_grad(x2, target, tc):
    bsz, t_all, d = x2.shape
    n_t = t_all // TB
    n_c = tc // TB

    def body(x_ref, t_ref, l_ref, dx_ref):
        b, t = pl.program_id(0), pl.program_id(1)

        @pl.when((b == 0) & (t == 0))
        def _():
            l_ref[...] = jnp.zeros_like(l_ref)

        @pl.when(t < n_c)
        def _():
            dx_ref[...] = jnp.zeros_like(dx_ref)

        @pl.when(t >= n_c)
        def _():
            e = x_ref[...] - t_ref[...]
            dx_ref[...] = e * (1.0 / d)
            l_ref[...] += jnp.sum(e * e) * (0.5 / d)

    loss, dx = _pcall(
        body, name="loss", grid=(bsz, n_t),
        in_specs=[pl.BlockSpec((None, TB, d), lambda b, t: (b, t, 0)),
                  pl.BlockSpec((None, TB, d), lambda b, t: (b, jnp.maximum(t - n_c, 0), 0))],
        out_specs=[pl.BlockSpec((SUBLANE, LANE), lambda b, t: (0, 0)),
                   pl.BlockSpec((None, TB, d), lambda b, t: (b, t, 0))],
        out_shape=[jax.ShapeDtypeStruct((SUBLANE, LANE), F32), jax.ShapeDtypeStruct(x2.shape, F32)],
        compiler_params=_cparams(2))(x2, target)
    return loss[0, 0], dx


def _rope_tables(lat, tc, dim, lane0):
    quarter = dim // 4
    pos = np.arange(lat)
    grid_pos = np.stack([pos // GRID_W, pos % GRID_W], axis=-1).astype(np.float32)
    lane = np.arange(LANE)
    p = np.clip(lane - lane0, 0, dim - 1)
    active = (lane >= lane0) & (lane < lane0 + dim)
    axis, half, qi = p // (dim // 2), (p % (dim // 2)) // quarter, p % quarter
    inv = (np.float32(ROPE_THETA) ** (-qi.astype(np.float32) / np.float32(quarter))).astype(np.float32)
    ang = (np.where(axis[None, :] == 0, grid_pos[:, 0:1], grid_pos[:, 1:2]) * inv[None, :]).astype(np.float32)
    cos = np.where(active, np.cos(ang), 1.0).astype(np.float32)
    sin = np.where(active, np.sin(ang), 0.0).astype(np.float32)
    sa = np.where(half == 0, -sin, 0.0).astype(np.float32)
    sb = np.where(half == 1, sin, 0.0).astype(np.float32)
    ctx1, ctx0 = np.ones((tc, LANE), np.float32), np.zeros((tc, LANE), np.float32)
    return tuple(jnp.asarray(np.concatenate([c, t], 0)) for c, t in ((ctx1, cos), (ctx0, sa), (ctx0, sb)))


_BIG = {"w_in": ((D_MODEL, IN_WIDTH // N_DEV), 1, ("win",)),
        "w_uq": ((MLA_Q_RANK, MLA_HEADS * MLA_QK // N_DEV), 1, ("wuq",)),
        "w_ukv": ((MLA_KV_RANK, MLA_HEADS * (MLA_NOPE + MLA_V) // N_DEV), 1, ("wk", "wv")),
        "w_out": ((3 * GROUP_WIDTH // N_DEV, D_MODEL), 0, ("wout",)),
        "w_ff1": ((D_MODEL, D_FF // N_DEV), 1, ("ff1",)),
        "w_ff2": ((D_FF // N_DEV, D_MODEL), 0, ("ff2",))}
_EARLY = ("w_in", "w_uq", "w_ukv")
_LATE = ("w_out", "w_ff1", "w_ff2")


def _pad_heads(wm, n_heads, dim, axis=-1):
    axis = axis % wm.ndim
    shp = wm.shape[:axis] + (n_heads, dim) + wm.shape[axis + 1:]
    pad = [(0, 0)] * len(shp)
    pad[axis + 1] = (0, LANE - dim)
    out = jnp.pad(wm.reshape(shp), pad)
    return out.reshape(wm.shape[:axis] + (n_heads * LANE,) + wm.shape[axis + 1:])


def _prep_weight(name, piece):
    shp, ax, _ = _BIG[name]
    full = jnp.moveaxis(piece, 0, ax).reshape(shp[:ax] + (N_DEV * shp[ax],) + shp[ax + 1:])
    if name == "w_in":
        cq, ckv, kr, lx, lg, sq, sk, sv = _split_cols(full)
        return {"win": jnp.concatenate(
            [_pad_heads(sq, SWA_HEADS, SWA_HEAD_DIM), lx, lg, cq, _pad_heads(sk, SWA_KV_HEADS, SWA_HEAD_DIM),
             _pad_heads(sv, SWA_KV_HEADS, SWA_HEAD_DIM), ckv, jnp.pad(kr, ((0, 0), (MLA_NOPE, LANE - MLA_QK)))], axis=1)}
    if name == "w_uq":
        return {"wuq": _pad_heads(full, MLA_HEADS, MLA_QK)}
    if name == "w_ukv":
        ukv = full.reshape(MLA_KV_RANK, MLA_HEADS, MLA_NOPE + MLA_V)
        return {"wk": _pad_heads(ukv[:, :, :MLA_NOPE].reshape(MLA_KV_RANK, -1), MLA_HEADS, MLA_NOPE),
                "wv": ukv[:, :, MLA_NOPE:].reshape(MLA_KV_RANK, -1)}
    return {_BIG[name][2][0]: full}


def _split_cols(wm):
    parts, start = [], 0
    for size in IN_SIZES:
        parts.append(wm[:, start:start + size])
        start += size
    return parts


def _prep_small(raw):
    r1 = lambda a: a.reshape(1, -1)
    gw = raw["lru_gate_w"].reshape(2, 2, 4, 2, 64, 64)
    wbd = jnp.einsum("zgknCm,nN->knCzgNm", gw, jnp.eye(2, dtype=F32)).reshape(4, LANE, 4, LANE)
    gg = raw["group_g"]
    sink = raw["swa_sink"].reshape(SWA_KV_HEADS, SWA_GROUP, 1, 1)
    return {
        "norm1_g": r1(raw["norm1_g"]), "norm2_g": r1(raw["norm2_g"]),
        "q_a_g": r1(raw["q_a_g"]), "kv_a_g": r1(raw["kv_a_g"]),
        "mla_q_g": jnp.pad(r1(raw["mla_q_g"]), ((0, 0), (0, LANE - MLA_QK))),
        "mla_k_g": jnp.pad(r1(raw["mla_k_g"]), ((0, 0), (0, LANE - MLA_QK))),
        "swa_q_g": jnp.pad(r1(raw["swa_q_g"]), ((0, 0), (0, LANE - SWA_HEAD_DIM))),
        "swa_k_g": jnp.pad(r1(raw["swa_k_g"]), ((0, 0), (0, LANE - SWA_HEAD_DIM))),
        "conv_w": [r1(raw["conv_w"][kk]) for kk in range(4)], "conv_b": r1(raw["conv_b"]),
        "wbd": wbd.transpose(0, 2, 1, 3).reshape(16, LANE, LANE),
        "gate_b": [r1(raw["lru_gate_b"][z, g]) for z in range(2) for g in range(2)],
        "sp": [r1(jax.nn.softplus(-raw["lru_lambda"][z])) for z in range(2)],
        "sink_b": jnp.broadcast_to(sink, (SWA_KV_HEADS, SWA_GROUP, QB_SWA, LANE)).reshape(
            SWA_KV_HEADS, SWA_GROUP * QB_SWA, LANE),
        "g_a": r1(gg[:GROUP_WIDTH]), "g_b": r1(gg[GROUP_WIDTH:2 * GROUP_WIDTH]), "g_c": r1(gg[2 * GROUP_WIDTH:])}


def _mesh_pos():
    return lax.axis_index("x"), lax.axis_index("y"), lax.axis_index("c")


def _peer(pos, k):
    return tuple(1 - p if (k >> s) & 1 else p for p, s in zip(pos, (2, 1, 0)))


def _dev_index(pos):
    return 4 * pos[0] + 2 * pos[1] + pos[2]


class _Exchange:
    def __init__(self, bufs, gather):
        self.bufs = list(bufs)
        self.n = len(self.bufs)
        self.gather = [gather] * self.n if isinstance(gather, bool) else list(gather)
        self.specs = [pl.BlockSpec(memory_space=pl.ANY)] * self.n
        self.out_shape = [jax.ShapeDtypeStruct((N_DEV,) + tuple(b.shape if g else b.shape[1:]), b.dtype)
                          for b, g in zip(self.bufs, self.gather)]
        self.scratch = [pltpu.SemaphoreType.DMA(((N_DEV - 1) * self.n,)),
                        pltpu.SemaphoreType.DMA(((N_DEV - 1) * self.n,)), pltpu.SemaphoreType.DMA((self.n,))]

    def _copies(self, x_refs, o_refs, sems, with_recvs):
        send_sems, recv_sems, local_sems = sems
        pos = _mesh_pos()
        me = _dev_index(pos)
        locals_, sends, recvs = [], [], []
        for j in range(self.n):
            src_mine = x_refs[j] if self.gather[j] else x_refs[j].at[me]
            locals_.append(pltpu.make_async_copy(src_mine, o_refs[j].at[me], local_sems.at[j]))
        for k in range(1, N_DEV):
            peer = _peer(pos, k)
            pidx = _dev_index(peer)
            for j in range(self.n):
                src = x_refs[j] if self.gather[j] else x_refs[j].at[pidx]
                sem = (k - 1) * self.n + j
                sends.append(pltpu.make_async_remote_copy(
                    src_ref=src, dst_ref=o_refs[j].at[me], send_sem=send_sems.at[sem], recv_sem=recv_sems.at[sem],
                    device_id=peer, device_id_type=pl.DeviceIdType.MESH))
                if with_recvs:
                    recvs.append(pltpu.make_async_remote_copy(
                        src_ref=src, dst_ref=o_refs[j].at[pidx], send_sem=send_sems.at[sem],
                        recv_sem=recv_sems.at[sem], device_id=peer, device_id_type=pl.DeviceIdType.MESH))
        return locals_, sends, recvs

    def start(self, x_refs, o_refs, sems):
        locals_, sends, _ = self._copies(x_refs, o_refs, sems, False)
        for cp in locals_ + sends:
            cp.start()

    def wait(self, x_refs, o_refs, sems):
        locals_, sends, recvs = self._copies(x_refs, o_refs, sems, True)
        for cp in recvs:
            cp.wait_recv()
        for cp in sends:
            cp.wait_send()
        for cp in locals_:
            cp.wait()


def _exchange(bufs, gather, name):
    xchg = _Exchange(bufs, gather)
    n = xchg.n

    def body(*refs):
        xchg.start(refs[:n], refs[n:2 * n], refs[2 * n:])
        xchg.wait(refs[:n], refs[n:2 * n], refs[2 * n:])

    return _pcall(body, name=name, out_shape=xchg.out_shape, in_specs=xchg.specs, out_specs=xchg.specs,
                  scratch_shapes=xchg.scratch)(*xchg.bufs)


def _pack(arrs, dtype):
    flat = jnp.concatenate([a.reshape(-1).astype(dtype) for a in arrs])
    rows = -(-flat.size // PACK_W)
    rows = -(-rows // 16) * 16
    return jnp.pad(flat, (0, rows * PACK_W - flat.size)).reshape(rows, PACK_W)


def _unpack(buf, shapes, lead=()):
    flat = buf.reshape(lead + (-1,))
    out, off = [], 0
    for shp in shapes:
        n = math.prod(shp)
        out.append(flat[..., off:off + n].reshape(lead + tuple(shp)))
        off += n
    return out


def _sum_sources(buf, name):
    _, r, c = buf.shape
    tr = _rows_tile(r)

    def body(x_ref, o_ref):
        acc = x_ref[0]
        for d in range(1, N_DEV):
            acc = acc + x_ref[d]
        o_ref[...] = acc

    return _pcall(body, name=name, grid=(r // tr,),
                  in_specs=[pl.BlockSpec((N_DEV, tr, c), lambda i: (0, i, 0))],
                  out_specs=pl.BlockSpec((tr, c), lambda i: (i, 0)),
                  out_shape=jax.ShapeDtypeStruct((r, c), F32), compiler_params=_cparams(1))(buf)


def _rows_tile(r):
    best = r
    for t in range(SUBLANE, 257, SUBLANE):
        if r % t == 0:
            best = t
    return best


def _adamw(grads, wgt, m, v, name):
    n_lay = len(grads)
    n_src, r, c = grads[0].shape
    tr = _rows_tile(r)
    n_blk = r // tr
    bc1 = 1.0 - ADAM_B1 ** ADAM_STEP
    bc2 = 1.0 - ADAM_B2 ** ADAM_STEP

    def body(*refs):
        g_refs, (w_ref, m_ref, v_ref, go_ref, d_ref, mo_ref, vo_ref) = refs[:n_lay], refs[n_lay:]
        for li, g_ref in enumerate(g_refs):
            @pl.when(pl.program_id(0) == li)
            def _():
                g = g_ref[0].astype(F32)
                for d in range(1, n_src):
                    g = g + g_ref[d].astype(F32)
                m_new = ADAM_B1 * m_ref[...] + (1.0 - ADAM_B1) * g
                v_new = ADAM_B2 * v_ref[...] + (1.0 - ADAM_B2) * (g * g)
                go_ref[...] = g
                mo_ref[...] = m_new
                vo_ref[...] = v_new
                d_ref[...] = -ADAM_LR * ((m_new / bc1) / (jnp.sqrt(v_new / bc2) + ADAM_EPS) + ADAM_WD * w_ref[...])

    g_specs = [pl.BlockSpec((n_src, tr, c),
                            lambda l, i, li=li: (0, jnp.where(l == li, i, jnp.where(l > li, n_blk - 1, 0)), 0))
               for li in range(n_lay)]
    spec = pl.BlockSpec((tr, c), lambda l, i: (l * n_blk + i, 0))
    return _pcall(body, name=name, grid=(n_lay, n_blk), in_specs=g_specs + [spec, spec, spec],
                  out_specs=[spec] * 4, out_shape=[jax.ShapeDtypeStruct((n_lay * r, c), F32)] * 4,
                  compiler_params=_cparams(2))(*grads, wgt, m, v)


def _silu(z):
    return z * jax.nn.sigmoid(z)


_WEIGHTS = ("c_ctx", "w_mod", "b_mod", "norm1_g", "w_in", "q_a_g", "w_uq", "kv_a_g", "w_ukv", "mla_q_g", "mla_k_g",
            "conv_w", "conv_b", "lru_gate_w", "lru_gate_b", "lru_lambda", "swa_q_g", "swa_k_g", "swa_sink", "group_g",
            "w_out", "norm2_g", "w_ff1", "w_ff2")
_SHARDED_SMALL = ("conv_w", "lru_gate_b", "lru_lambda")
_REPL_RAW = ("norm1_g", "q_a_g", "kv_a_g", "mla_q_g", "mla_k_g", "conv_b", "lru_gate_w", "swa_q_g", "swa_k_g",
             "swa_sink", "group_g", "norm2_g")
MOD_ROWS = 32


def kernel(x, c, ctx, c_ctx, w_mod, b_mod, norm1_g, w_in, q_a_g, w_uq, kv_a_g, w_ukv, mla_q_g, mla_k_g, conv_w, conv_b, lru_gate_w, lru_gate_b, lru_lambda, swa_q_g, swa_k_g, swa_sink, group_g, w_out, norm2_g, w_ff1, w_ff2, loss_target, m_c_ctx, m_w_mod, m_b_mod, m_norm1_g, m_w_in, m_q_a_g, m_w_uq, m_kv_a_g, m_w_ukv, m_mla_q_g, m_mla_k_g, m_conv_w, m_conv_b, m_lru_gate_w, m_lru_gate_b, m_lru_lambda, m_swa_q_g, m_swa_k_g, m_swa_sink, m_group_g, m_w_out, m_norm2_g, m_w_ff1, m_w_ff2, v_c_ctx, v_w_mod, v_b_mod, v_norm1_g, v_w_in, v_q_a_g, v_w_uq, v_kv_a_g, v_w_ukv, v_mla_q_g, v_mla_k_g, v_conv_w, v_conv_b, v_lru_gate_w, v_lru_gate_b, v_lru_lambda, v_swa_q_g, v_swa_k_g, v_swa_sink, v_group_g, v_w_out, v_norm2_g, v_w_ff1, v_w_ff2):
    wts = dict(c_ctx=c_ctx, w_mod=w_mod, b_mod=b_mod, norm1_g=norm1_g, w_in=w_in, q_a_g=q_a_g, w_uq=w_uq,
               kv_a_g=kv_a_g, w_ukv=w_ukv, mla_q_g=mla_q_g, mla_k_g=mla_k_g, conv_w=conv_w, conv_b=conv_b,
               lru_gate_w=lru_gate_w, lru_gate_b=lru_gate_b, lru_lambda=lru_lambda, swa_q_g=swa_q_g, swa_k_g=swa_k_g,
               swa_sink=swa_sink, group_g=group_g, w_out=w_out, norm2_g=norm2_g, w_ff1=w_ff1, w_ff2=w_ff2)
    mom1 = dict(zip(_WEIGHTS, (m_c_ctx, m_w_mod, m_b_mod, m_norm1_g, m_w_in, m_q_a_g, m_w_uq, m_kv_a_g, m_w_ukv,
                               m_mla_q_g, m_mla_k_g, m_conv_w, m_conv_b, m_lru_gate_w, m_lru_gate_b, m_lru_lambda,
                               m_swa_q_g, m_swa_k_g, m_swa_sink, m_group_g, m_w_out, m_norm2_g, m_w_ff1, m_w_ff2)))
    mom2 = dict(zip(_WEIGHTS, (v_c_ctx, v_w_mod, v_b_mod, v_norm1_g, v_w_in, v_q_a_g, v_w_uq, v_kv_a_g, v_w_ukv,
                               v_mla_q_g, v_mla_k_g, v_conv_w, v_conv_b, v_lru_gate_w, v_lru_gate_b, v_lru_lambda,
                               v_swa_q_g, v_swa_k_g, v_swa_sink, v_group_g, v_w_out, v_norm2_g, v_w_ff1, v_w_ff2)))
    bsz = x.shape[0]
    n_ex = bsz * N_DEV
    me = _dev_index(_mesh_pos())
    mod_cols = w_mod.shape[-1]

    small_shapes = [c.shape, conv_w.shape, lru_gate_b.shape, lru_lambda.shape]
    shard = lambda n, li: wts[n][li].astype(BF16)
    g_small, *early_pieces = _exchange([_pack([c, conv_w, lru_gate_b, lru_lambda], F32)] + [shard(n, 0) for n in _EARLY],
                                       True, "ag_first")
    c_all, conv_w_all, gate_b_all, lam_all = _unpack(g_small, small_shapes, lead=(N_DEV,))
    c_all = c_all.reshape(n_ex, D_MODEL)
    cat_last = lambda a: jnp.moveaxis(a, 0, -2).reshape(a.shape[1:-1] + (N_DEV * a.shape[-1],))
    conv_w_full, gate_b_full, lam_full = cat_last(conv_w_all), cat_last(gate_b_all), cat_last(lam_all)

    act = jnp.zeros((MOD_ROWS, D_MODEL), F32).at[:n_ex].set(_silu(c_all)).at[n_ex].set(_silu(c_ctx))
    mod_part = jnp.concatenate([_mm(act, w_mod[li], "nn", F32, "mm_mod_l%d" % li) for li in range(DEPTH)], axis=1)
    (mod_all,) = _exchange([mod_part], True, "ag_mod")
    mods = []
    for li in range(DEPTH):
        full = jnp.moveaxis(mod_all[:, :, li * mod_cols:(li + 1) * mod_cols], 0, 1).reshape(MOD_ROWS, -1) + b_mod[li]
        mine = lax.dynamic_slice_in_dim(full, me * bsz, bsz, axis=0)
        ctx_row = jnp.broadcast_to(full[n_ex], mine.shape)
        both = jnp.stack([ctx_row, mine], axis=1).reshape(bsz, 2, N_MOD, 1, D_MODEL)
        mods.append([both[:, :, j] for j in range(N_MOD)])

    raw = {n: wts[n] for n in _REPL_RAW}
    raw.update(conv_w=conv_w_full, lru_gate_b=gate_b_full, lru_lambda=lam_full)
    small_names = list(_REPL_RAW) + list(_SHARDED_SMALL)
    sp, small_vjp = [None] * DEPTH, [None] * DEPTH
    for li in range(DEPTH):
        sp[li], small_vjp[li] = jax.vjp(_prep_small, {n: raw[n][li] for n in small_names})

    w, w_vjp, g_recv, small_recv = [{} for _ in range(DEPTH)], {}, {}, {}

    def take(li, names, pieces):
        for n, piece in zip(names, pieces):
            out, w_vjp[n, li] = jax.vjp(functools.partial(_prep_weight, n), piece)
            w[li].update(out)

    def gather_hook(li, names):
        return (lambda _: _Exchange([shard(n, li) for n in names], True), lambda got: take(li, names, got))

    def wgrad(n, li, dwl):
        (g,) = w_vjp[n, li]({k: dwl[k].astype(BF16) for k in _BIG[n][2]})
        return g

    pack_names = [n for n in small_names if n != "lru_gate_w"]

    def small_bufs(li, ds_l, extra=()):
        (d_raw,) = small_vjp[li](ds_l)
        return [_pack([d_raw[n] for n in pack_names] + list(extra), F32), d_raw["lru_gate_w"].reshape(-1, LANE)]

    take(0, _EARLY, early_pieces)
    hooks_fwd = [{"mla_fwd": gather_hook(0, _LATE), "swa_fwd": gather_hook(1, _EARLY + ("w_out",))},
                 {"mla_fwd": gather_hook(1, ("w_ff1", "w_ff2"))}]
    bwd_state = {}

    def scatter_last_layer(dwl):
        return _Exchange([wgrad(n, 1, dwl) for n in _LATE], False)

    def scatter_first_layer(dwl):
        dw1, ds1 = bwd_state["dw1"], bwd_state["ds1"]
        bufs = [wgrad(n, 1, dw1) for n in _EARLY] + [wgrad(n, 0, dwl) for n in _LATE] + small_bufs(1, ds1)
        return _Exchange(bufs, [False] * (len(_EARLY) + len(_LATE)) + [True, True])

    def scattered_first_layer(got):
        g_recv.update(zip([(n, 1) for n in _EARLY] + [(n, 0) for n in _LATE], got[:-2]))
        small_recv[1], g_recv["lru_gate_w", 1] = got[-2:]

    hooks_bwd = [{"mla_bwd": (scatter_first_layer, scattered_first_layer)},
                 {"mla_bwd": (scatter_last_layer, lambda got: g_recv.update(zip([(n, 1) for n in _LATE], got)))}]

    tc, lat = ctx.shape[1], x.shape[1]
    tabs = {"mla": _rope_tables(lat, tc, MLA_ROPE, MLA_NOPE), "swa": _rope_tables(lat, tc, SWA_HEAD_DIM, 0)}
    stream = jnp.concatenate([ctx, x], axis=1)
    bwds = []
    for li in range(DEPTH):
        stream, bwd = _layer(li, stream, mods[li], w[li], sp[li], tabs, tc, li < DEPTH - 1, hooks_fwd[li])
        bwds.append(bwd)
    loss_part, dstream = _loss_and_grad(stream, loss_target, tc)
    dmods = [None] * DEPTH
    dstream, dmods[1], bwd_state["dw1"], bwd_state["ds1"] = bwds[1](dstream, hooks_bwd[1])
    dstream, dmods[0], dw0, ds0 = bwds[0](dstream, hooks_bwd[0])
    grad_x = dstream[:, tc:]

    dm_rows = []
    for li in range(DEPTH):
        dm = jnp.concatenate(dmods[li], axis=-1)
        dm_rows.append(jnp.concatenate([dm[:, 1, 0], jnp.sum(dm[:, 0, 0], axis=0, keepdims=True)], axis=0))
    dm_mine = jnp.concatenate(dm_rows, axis=1)
    dm_mine = jnp.pad(dm_mine, ((0, SUBLANE - bsz - 1), (0, 0)))
    (dm_all,) = _exchange([dm_mine], True, "ag_dmod")
    g_wmod, g_bmod, dact_ctx = [], [], jnp.zeros((D_MODEL,), F32)
    for li in range(DEPTH):
        part = dm_all[:, :, li * N_MOD * D_MODEL:(li + 1) * N_MOD * D_MODEL]
        dm32 = jnp.zeros((MOD_ROWS, N_MOD * D_MODEL), F32).at[:n_ex].set(part[:, :bsz].reshape(n_ex, -1))
        dm32 = dm32.at[n_ex].set(jnp.sum(part[:, bsz], axis=0))
        g_bmod.append(jnp.sum(dm32, axis=0))
        cols = lax.dynamic_slice_in_dim(dm32, me * mod_cols, mod_cols, axis=1)
        g_wmod.append(_mm(act, cols, "tn", F32, "mm_mod_dw_l%d" % li))
        dact_ctx = dact_ctx + _mm(cols, w_mod[li], "nt", F32, "mm_mod_dx_l%d" % li)[n_ex]
    sg = jax.nn.sigmoid(c_ctx)
    g_cctx_part = dact_ctx * (sg * (1.0 + c_ctx * (1.0 - sg)))

    last = _exchange([wgrad(n, 0, dw0) for n in _EARLY] + small_bufs(0, ds0, (g_cctx_part, loss_part.reshape(1))),
                     [False] * len(_EARLY) + [True, True], "rs_early")
    g_recv.update(zip([(n, 0) for n in _EARLY], last[:-2]))
    small_recv[0], g_recv["lru_gate_w", 0] = last[-2:]
    layer_shapes = [raw[n].shape[1:] for n in pack_names]
    tot = [_unpack(_sum_sources(small_recv[li], "sum_grads_l%d" % li), layer_shapes + [(D_MODEL,), (1,)][:2 * (li == 0)])
           for li in range(DEPTH)]
    grads = {n: jnp.stack([tot[li][j] for li in range(DEPTH)], axis=0) for j, n in enumerate(pack_names)}
    grads["c_ctx"], loss = tot[0][-2], tot[0][-1][0]
    for n in _SHARDED_SMALL:
        width = wts[n].shape[-1]
        grads[n] = lax.dynamic_slice_in_dim(grads[n], me * width, width, axis=grads[n].ndim - 1)
    grads["b_mod"] = jnp.stack(g_bmod, axis=0)

    delta, new_m, new_v = {}, {}, {}
    per_layer = {n: [g_recv[n, li] for li in range(DEPTH)] for n in list(_BIG) + ["lru_gate_w"]}
    per_layer["w_mod"] = [g[None] for g in g_wmod]
    for n, srcs in per_layer.items():
        two_d = (DEPTH * math.prod(wts[n].shape[1:-1]), wts[n].shape[-1])
        srcs = [s.reshape((s.shape[0], two_d[0] // DEPTH, two_d[1])) for s in srcs]
        res = _adamw(srcs, wts[n].reshape(two_d), mom1[n].reshape(two_d), mom2[n].reshape(two_d), "adamw_" + n)
        grads[n], delta[n], new_m[n], new_v[n] = [r.reshape(wts[n].shape) for r in res]
    rest = [n for n in _WEIGHTS if n not in delta]
    shapes = [wts[n].shape for n in rest]
    res = _adamw([_pack([grads[n] for n in rest], F32)[None]], _pack([wts[n] for n in rest], F32),
                 _pack([mom1[n] for n in rest], F32), _pack([mom2[n] for n in rest], F32), "adamw_small")
    for tgt, buf in zip((delta, new_m, new_v), res[1:]):
        tgt.update(zip(rest, _unpack(buf, shapes)))

    return (loss, grad_x, *[grads[n] for n in _WEIGHTS], *[delta[n] for n in _WEIGHTS],
            *[new_m[n] for n in _WEIGHTS], *[new_v[n] for n in _WEIGHTS])
```

```python
import functools
import math

import jax
import jax.numpy as jnp
import numpy as np
from jax import lax
from jax.experimental import pallas as pl
from jax.experimental.pallas import tpu as pltpu

F32, BF16 = jnp.float32, jnp.bfloat16

N_DEV = 8
DEPTH = 2
D_MODEL = 1024
D_FF = 4096
N_MOD = 6
GRID_W = 64
WINDOW = 128
ROPE_THETA = 10000.0
EPS = 1e-6
NEG_INF = -1e30
LRU_C = 8.0
LRU_WIDTH = 512
MLA_HEADS, MLA_NOPE, MLA_ROPE, MLA_V = 8, 64, 32, 64
MLA_QK = MLA_NOPE + MLA_ROPE
MLA_Q_RANK, MLA_KV_RANK = 256, 128
SWA_HEADS, SWA_KV_HEADS, SWA_GROUP, SWA_HEAD_DIM = 8, 2, 4, 64
GROUP_WIDTH = 512
IN_SIZES = (256, 128, 32, 512, 512, 512, 128, 128)
IN_WIDTH = sum(IN_SIZES)
ADAM_LR, ADAM_B1, ADAM_B2, ADAM_EPS, ADAM_WD, ADAM_STEP = 0.001, 0.9, 0.999, 1e-08, 0.01, 10

LANE = 128
SUBLANE = 8
TB = 256
QB_SWA = 256
PACK_W = 1024
MM_K_MAX = 4608
MLA_HPS = 2
VMEM_LIMIT = 56 * 1024 * 1024
P_WIDTH = 3072
PC_SQ, PC_LX, PC_LG, PC_CQ, PC_SK, PC_SV, PC_CKV, PC_KR = 0, 1024, 1536, 2048, 2304, 2560, 2816, 2944
MIX_P = 1536


def _pcall(body, **kw):
    return pl.pallas_call(body, **kw)


def _cparams(n_grid):
    return pltpu.CompilerParams(dimension_semantics=("arbitrary",) * n_grid, vmem_limit_bytes=VMEM_LIMIT)


def _dg(a, b, ca, cb):
    return lax.dot_general(a.astype(BF16), b.astype(BF16), (((ca,), (cb,)), ((), ())),
                           preferred_element_type=F32)


@jax.custom_vjp
def _nn(a, b):
    return _dg(a, b, 1, 0)


@jax.custom_vjp
def _nt(a, b):
    return _dg(a, b, 1, 1)


@jax.custom_vjp
def _tn(a, b):
    return _dg(a, b, 0, 0)


_nn.defvjp(lambda a, b: (_nn(a, b), (a, b)), lambda r, ct: (_nt(ct, r[1]), _tn(r[0], ct)))
_nt.defvjp(lambda a, b: (_nt(a, b), (a, b)), lambda r, ct: (_nn(ct, r[1]), _tn(ct, r[0])))
_tn.defvjp(lambda a, b: (_tn(a, b), (a, b)), lambda r, ct: (_nt(r[1], ct), _nn(r[0], ct)))


@functools.partial(jax.custom_vjp, nondiff_argnums=(1, 2))
def _roll(x, shift, axis):
    return pltpu.roll(x, shift % x.shape[axis], axis)


_roll.defvjp(lambda x, shift, axis: (_roll(x, shift, axis), None),
             lambda shift, axis, _, ct: (_roll(ct, -shift, axis),))


@functools.partial(jax.custom_vjp, nondiff_argnums=(1, 2))
def _split(x, n, axis):
    w = x.shape[axis] // n
    return tuple(lax.slice_in_dim(x, i * w, (i + 1) * w, axis=axis) for i in range(n))


_split.defvjp(lambda x, n, axis: (_split(x, n, axis), None),
              lambda n, axis, _, cts: (jnp.concatenate(cts, axis=axis),))


@jax.custom_vjp
def _unstack(x):
    return tuple(x[i] for i in range(x.shape[0]))


_unstack.defvjp(lambda x: (_unstack(x), None), lambda _, cts: (jnp.stack(cts, axis=0),))


def _sig(x):
    return 0.5 * (jnp.tanh(0.5 * x) + 1.0)


def _gelu(x):
    return 0.5 * x * (1.0 + jnp.tanh(math.sqrt(2.0 / math.pi) * (x + 0.044715 * (x * x * x))))


def _rms(x, g, n):
    ms = jnp.sum(x * x, axis=-1, keepdims=True) * (1.0 / n)
    return x * lax.rsqrt(ms + EPS) * g


def _rope(y, cos, sa, sb, quarter):
    return y * cos + _roll(y, -quarter, 1) * sa + _roll(y, quarter, 1) * sb


def _softmax_rows(s, extra=None):
    m = jnp.max(s, axis=-1, keepdims=True)
    if extra is not None:
        m = jnp.maximum(m, extra)
    m = lax.stop_gradient(m)
    e = jnp.exp(s - m)
    den = jnp.sum(e, axis=-1, keepdims=True)
    if extra is not None:
        den = den + jnp.exp(extra - m)
    return e / den


class _A:
    def __init__(self, arr, block, imap, kind="row", first=None, gdtype=F32, gshape=None, gimap=None):
        self.arr, self.block, self.imap, self.kind, self.first = arr, block, imap, kind, first
        self.gdtype, self.gshape, self.gimap = gdtype, gshape, gimap


def _all_zero(*ids):
    return functools.reduce(jnp.logical_and, [i == 0 for i in ids])


def _par(arr):
    nd = arr.ndim
    return _A(arr, arr.shape, lambda *ids: (0,) * nd, "acc", first=_all_zero)


def _op_fwd(name, fn, grid, args, outs):
    n_in = len(args)

    def body(*refs):
        vals = [r[...].astype(F32) for r in refs[:n_in]]
        for r, v in zip(refs[n_in:], fn(*vals)):
            r[...] = v.astype(r.dtype)

    return _pcall(
        body, name=name, grid=grid,
        in_specs=[pl.BlockSpec(a.block, a.imap) for a in args],
        out_specs=[pl.BlockSpec(o[2], o[3]) for o in outs],
        out_shape=[jax.ShapeDtypeStruct(o[0], o[1]) for o in outs],
        compiler_params=_cparams(len(grid)),
    )(*[a.arr for a in args])


def _op_bwd(name, fn, grid, args, outs, ct_arrays, add_to_first=None):
    didx = [i for i, a in enumerate(args) if a.kind not in ("const", "fwd")]
    read = [i for i, a in enumerate(args) if a.kind != "fwd"]
    n_in, n_ct = len(read), len(outs)
    n_add = 0 if add_to_first is None else 1

    def body(*refs):
        ids = [pl.program_id(i) for i in range(len(grid))]
        vals = [jnp.zeros([d for d in a.block if d is not None], F32) for a in args]
        for i, r in zip(read, refs[:n_in]):
            vals[i] = r[...].astype(F32)

        def g(*dv):
            full = list(vals)
            for i, v in zip(didx, dv):
                full[i] = v
            return tuple(fn(*full))

        _, vjp = jax.vjp(g, *[vals[i] for i in didx])
        grads = list(vjp(tuple(r[...].astype(F32) for r in refs[n_in:n_in + n_ct])))
        if n_add:
            grads[0] = grads[0] + refs[n_in + n_ct][...]
        for gr, i, r in zip(grads, didx, refs[n_in + n_ct + n_add:]):
            a = args[i]
            if a.kind == "row":
                r[...] = gr.astype(r.dtype)
            else:
                first = a.first(*ids)

                @pl.when(first)
                def _():
                    r[...] = gr

                @pl.when(jnp.logical_not(first))
                def _():
                    r[...] += gr

    g_specs, g_shapes = [], []
    for i in didx:
        a = args[i]
        if a.kind == "row":
            g_specs.append(pl.BlockSpec(a.block, a.gimap or a.imap))
            g_shapes.append(jax.ShapeDtypeStruct(a.gshape or a.arr.shape, a.gdtype))
        else:
            g_specs.append(pl.BlockSpec(a.block, a.imap))
            g_shapes.append(jax.ShapeDtypeStruct(a.arr.shape, F32))
    return _pcall(
        body, name=name, grid=grid,
        in_specs=[pl.BlockSpec(args[i].block, args[i].imap) for i in read] + [pl.BlockSpec(o[2], o[3]) for o in outs]
        + g_specs[:n_add],
        out_specs=g_specs, out_shape=g_shapes,
        compiler_params=_cparams(len(grid)),
    )(*[args[i].arr for i in read], *ct_arrays, *([add_to_first] if n_add else []))


def _rowop(name, fn, grid, args, outs):
    res = _op_fwd(name, fn, grid, args, outs)
    return res, lambda *cts, add_to_first=None: _op_bwd(name + "_bwd", fn, grid, args, outs, cts, add_to_first)


def _pick(n, cap):
    best = None
    for t in range(LANE, cap + 1, LANE):
        if n % t == 0:
            best = t
    return best or n


def _mm(a, b, mode, out_dtype, name, epi=None, aux=None, out_split=None):
    if mode == "nn":
        (m, k), n = a.shape, b.shape[1]
    elif mode == "nt":
        (m, k), n = a.shape, b.shape[0]
    else:
        (k, m), n = a.shape, b.shape[1]
    assert k <= MM_K_MAX
    tm = 512 if m % 512 == 0 else m
    tn = n // out_split if out_split else _pick(n, 1024)
    a_spec = pl.BlockSpec((k, tm), lambda j, i: (0, i)) if mode == "tn" else pl.BlockSpec((tm, k), lambda j, i: (i, 0))
    b_spec = pl.BlockSpec((tn, k), lambda j, i: (j, 0)) if mode == "nt" else pl.BlockSpec((k, tn), lambda j, i: (0, j))
    dims = {"nn": (1, 0), "nt": (1, 1), "tn": (0, 0)}[mode]
    aux_spec = pl.BlockSpec((tm, tn), lambda j, i: (i, j))
    if out_split:
        o_spec, o_shape = pl.BlockSpec((None, tm, tn), lambda j, i: (j, i, 0)), (out_split, m, tn)
    else:
        o_spec, o_shape = aux_spec, (m, n)
    n_aux = 0 if aux is None else 1
    n_out = 2 if epi == "sqrelu" else 1

    def body(*refs):
        o_refs = refs[2 + n_aux:]
        r = _dg(refs[0][...], refs[1][...], *dims)
        if epi == "sqrelu":
            o_refs[0][...] = r.astype(o_refs[0].dtype)
            rl = jnp.maximum(r, 0.0)
            o_refs[1][...] = (rl * rl).astype(o_refs[1].dtype)
        elif epi == "dsqrelu":
            pre = refs[2][...].astype(F32)
            o_refs[0][...] = (r * (2.0 * jnp.maximum(pre, 0.0))).astype(o_refs[0].dtype)
        else:
            o_refs[0][...] = r.astype(o_refs[0].dtype)

    res = _pcall(
        body, name=name, grid=(n // tn, m // tm),
        in_specs=[a_spec, b_spec] + [aux_spec] * n_aux, out_specs=[o_spec] * n_out,
        out_shape=[jax.ShapeDtypeStruct(o_shape, out_dtype)] * n_out, compiler_params=_cparams(2),
    )(a, b, *([aux] if aux is not None else []))
    return res if n_out == 2 else res[0]


ROW_CHUNK = 16


def _softmax_chunks(s_scr, n_keys, scale, emit):
    for r0 in range(0, s_scr.shape[0], ROW_CHUNK):
        rows = slice(r0, r0 + ROW_CHUNK)
        s = s_scr[rows, :n_keys]
        e = jnp.exp((s - jnp.max(s, axis=-1, keepdims=True)) * scale)
        emit(rows, e, 1.0 / jnp.sum(e, axis=-1, keepdims=True))


def _attn_fwd_block(v, n, scale, s_scr, e_scr, l_scr):
    def emit(rows, e, inv_l):
        e_scr[rows, :n] = e.astype(BF16)
        l_scr[rows, :] = jnp.broadcast_to(inv_l, (ROW_CHUNK, LANE))

    _softmax_chunks(s_scr, n, scale, emit)
    return _dg(e_scr[:, :n], v, 1, 0) * l_scr[...]


def _attn_bwd_block(q, k, o, do, scale, s_scr, dp_scr, p_scr, ds_scr):
    n = k.shape[0]

    def emit(rows, e, inv_l):
        p = e * inv_l
        delta = jnp.sum(do[rows, :] * o[rows, :], axis=-1, keepdims=True)
        p_scr[rows, :n] = p.astype(BF16)
        ds_scr[rows, :n] = (p * (dp_scr[rows, :n] - delta) * scale).astype(BF16)

    _softmax_chunks(s_scr, n, scale, emit)
    ds = ds_scr[:, :n]
    return _dg(ds, k, 1, 0), _dg(ds, q, 0, 0), _dg(p_scr[:, :n], do, 0, 0)


def _call_with_exchange(body, xchg, *, name, grid, in_specs, out_specs, out_shape, operands, scratch_shapes=()):
    if xchg is None:
        res = _pcall(body, name=name, grid=grid, in_specs=in_specs, out_specs=out_specs, out_shape=out_shape,
                     scratch_shapes=list(scratch_shapes), compiler_params=_cparams(len(grid)))(*operands)
        return list(res), []
    n_in, n_out, n_sc, n = len(in_specs), len(out_specs), len(scratch_shapes), xchg.n

    def wrapped(*refs):
        ins, x_refs = refs[:n_in], refs[n_in:n_in + n]
        outs, xo_refs = refs[n_in + n:n_in + n + n_out], refs[n_in + n + n_out:n_in + 2 * n + n_out]
        scratch, sems = refs[n_in + 2 * n + n_out:n_in + 2 * n + n_out + n_sc], refs[n_in + 2 * n + n_out + n_sc:]
        ids = [pl.program_id(i) for i in range(len(grid))]

        @pl.when(functools.reduce(jnp.logical_and, [i == 0 for i in ids]))
        def _():
            xchg.start(x_refs, xo_refs, sems)

        body(*ins, *outs, *scratch)

        @pl.when(functools.reduce(jnp.logical_and, [i == g - 1 for i, g in zip(ids, grid)]))
        def _():
            xchg.wait(x_refs, xo_refs, sems)

    res = _pcall(wrapped, name=name, grid=grid, in_specs=list(in_specs) + xchg.specs,
                 out_specs=list(out_specs) + xchg.specs, out_shape=list(out_shape) + xchg.out_shape,
                 scratch_shapes=list(scratch_shapes) + xchg.scratch, compiler_params=_cparams(len(grid)),
                 )(*operands, *xchg.bufs)
    return list(res[:n_out]), list(res[n_out:])


def _head_half(i, shape):
    lane = lax.broadcasted_iota(jnp.int32, shape, len(shape) - 1)
    return (lane < LANE // 2) if i == 0 else (lane >= LANE // 2)


def _mla_attn(q, k, v, tc, ctx_q, name, xchg=None):
    assert MLA_HPS == 2 and MLA_V == LANE // 2
    bsz, t_all, _ = q.shape
    n_t = t_all // TB
    grid = (bsz, MLA_HEADS // MLA_HPS, n_t)
    q_spec = pl.BlockSpec((None, TB, MLA_HPS * LANE), lambda b, h, t: (b, t, h))
    k_spec = pl.BlockSpec((None, t_all, MLA_HPS * LANE), lambda b, h, t: (b, 0, h))
    v_spec = pl.BlockSpec((None, t_all, LANE), lambda b, h, t: (b, 0, h))
    o_spec = pl.BlockSpec((None, TB, LANE), lambda b, h, t: (b, t, h))
    heads = [slice(i * LANE, (i + 1) * LANE) for i in range(MLA_HPS)]
    scale = MLA_QK ** -0.5
    f32_scr, bf16_scr = pltpu.VMEM((TB, t_all), F32), pltpu.VMEM((TB, t_all), BF16)
    o_shape = jax.ShapeDtypeStruct(v.shape, F32)

    def fwd_body(q_ref, k_ref, v_ref, o_ref, *scr):
        t = pl.program_id(2)

        def run(keys):
            n = keys.stop
            for i, hs in enumerate(heads):
                scr[3 * i][:, :n] = _dg(q_ref[:, hs], k_ref[keys, hs], 1, 1)
            both = [_attn_fwd_block(v_ref[keys, :], n, scale, *scr[3 * i:3 * i + 3]) for i in range(MLA_HPS)]
            o_ref[...] = jnp.where(_head_half(0, both[0].shape), both[0], both[1])

        @pl.when(t == 0)
        def _():
            if ctx_q:
                run(slice(0, tc))
            else:
                o_ref[...] = jnp.zeros_like(o_ref)

        @pl.when(t > 0)
        def _():
            run(slice(0, t_all))

    (o,), gathered = _call_with_exchange(
        fwd_body, xchg, name=name, grid=grid, in_specs=[q_spec, k_spec, v_spec], out_specs=[o_spec],
        out_shape=[o_shape], operands=(q, k, v),
        scratch_shapes=[f32_scr, bf16_scr, pltpu.VMEM((TB, LANE), F32)] * MLA_HPS)

    def bwd(do, xchg=None):
        def bwd_body(q_ref, k_ref, v_ref, o_ref, do_ref, dq_ref, dk_ref, dv_ref, *scr):
            t = pl.program_id(2)

            def run(keys, first):
                n = keys.stop
                dos = [jnp.where(_head_half(i, do_ref.shape), do_ref[...], 0.0) for i in range(MLA_HPS)]
                for i, hs in enumerate(heads):
                    scr[4 * i][:, :n] = _dg(q_ref[:, hs], k_ref[keys, hs], 1, 1)
                    scr[4 * i + 1][:, :n] = _dg(dos[i], v_ref[keys, :], 1, 1)
                dvs = []
                for i, hs in enumerate(heads):
                    dq, dk, dv = _attn_bwd_block(q_ref[:, hs], k_ref[keys, hs], o_ref[...], dos[i], scale,
                                                 *scr[4 * i:4 * i + 4])
                    dq_ref[:, hs] = dq
                    dvs.append(dv)
                    if first:
                        dk_ref[keys, hs] = dk
                    else:
                        dk_ref[keys, hs] += dk
                if first:
                    dv_ref[keys, :] = dvs[0] + dvs[1]
                else:
                    dv_ref[keys, :] += dvs[0] + dvs[1]

            @pl.when(t == 0)
            def _():
                dk_ref[...] = jnp.zeros_like(dk_ref)
                dv_ref[...] = jnp.zeros_like(dv_ref)
                if ctx_q:
                    run(slice(0, tc), True)
                else:
                    dq_ref[...] = jnp.zeros_like(dq_ref)

            @pl.when(t > 0)
            def _():
                run(slice(0, t_all), False)

        return _call_with_exchange(
            bwd_body, xchg, name=name + "_bwd", grid=grid, in_specs=[q_spec, k_spec, v_spec, o_spec, o_spec],
            out_specs=[q_spec, k_spec, v_spec],
            out_shape=[jax.ShapeDtypeStruct(q.shape, F32), jax.ShapeDtypeStruct(q.shape, F32), o_shape],
            operands=(q, k, v, o, do), scratch_shapes=[f32_scr, f32_scr, bf16_scr, bf16_scr] * MLA_HPS)

    return o, gathered, bwd


def _swa_block(q, keys, vals, sink, mask):
    qs = jnp.concatenate(list(_split(q, SWA_GROUP, 1)), axis=0)
    sk = jnp.sum(sink, axis=-1, keepdims=True) * (1.0 / LANE)
    s = _nt(qs, keys) * (SWA_HEAD_DIM ** -0.5)
    if mask is not None:
        s = jnp.where(mask, s, NEG_INF)
    o = _split(_nn(_softmax_rows(s, sk), vals + _roll(vals, LANE // 2, 1)), SWA_GROUP, 0)
    low = _head_half(0, o[0].shape)
    return jnp.concatenate([jnp.where(low, o[0], o[1]), jnp.where(low, o[2], o[3])], axis=1)


def _swa_ctx_block(q, kc, vc, sink):
    return _swa_block(q, kc, vc, sink, None)


def _swa_win_block(q, kc, kw, vc, vw, sink, mask):
    return _swa_block(q, jnp.concatenate([kc, kw], axis=0), jnp.concatenate([vc, vw], axis=0), sink, mask)


def _swa_attn(q, k, p_all, sink_b, tc, ctx_q, name, xchg=None):
    bsz, t_all, _ = q.shape
    n_q = t_all // QB_SWA
    n_cq = tc // QB_SWA
    lat = t_all - tc
    span = QB_SWA + 2 * WINDOW
    gw = SWA_GROUP * LANE
    grid = (bsz, SWA_KV_HEADS, n_q)
    q_spec = pl.BlockSpec((None, QB_SWA, gw), lambda b, g, i: (b, i, g))
    k_spec = pl.BlockSpec((None, t_all, LANE), lambda b, g, i: (b, 0, g))
    v_spec = pl.BlockSpec((None, t_all, LANE), lambda b, g, i: (b, 0, PC_SV // LANE + g))
    s_spec = pl.BlockSpec((None, SWA_GROUP * QB_SWA, LANE), lambda b, g, i: (g, 0, 0))

    def window(i):
        q0 = (i - n_cq) * QB_SWA
        w0 = jnp.clip(q0 - WINDOW, 0, lat - span)
        w0 = pl.multiple_of(w0, WINDOW)
        shape = (SWA_GROUP * QB_SWA, tc + span)
        qi = q0 + lax.broadcasted_iota(jnp.int32, shape, 0) % QB_SWA
        col = lax.broadcasted_iota(jnp.int32, shape, 1)
        kj = w0 + col - tc
        mask = (col < tc) | ((kj >= qi - WINDOW) & (kj <= qi + WINDOW))
        return w0, mask

    def fwd_body(q_ref, k_ref, v_ref, s_ref, o_ref):
        i = pl.program_id(2)

        @pl.when(i < n_cq)
        def _():
            if ctx_q:
                o_ref[...] = _swa_ctx_block(q_ref[...].astype(F32), k_ref[0:tc, :], v_ref[0:tc, :], s_ref[...])
            else:
                o_ref[...] = jnp.zeros_like(o_ref)

        @pl.when(i >= n_cq)
        def _():
            w0, mask = window(i)
            o_ref[...] = _swa_win_block(q_ref[...].astype(F32), k_ref[0:tc, :], k_ref[pl.ds(tc + w0, span), :],
                                        v_ref[0:tc, :], v_ref[pl.ds(tc + w0, span), :], s_ref[...], mask)

    o_spec = pl.BlockSpec((None, QB_SWA, SWA_GROUP * SWA_HEAD_DIM), lambda b, g, i: (b, i, g))
    (o,), gathered = _call_with_exchange(
        fwd_body, xchg, name=name, grid=grid, in_specs=[q_spec, k_spec, v_spec, s_spec], out_specs=[o_spec],
        out_shape=[jax.ShapeDtypeStruct((bsz, t_all, SWA_HEADS * SWA_HEAD_DIM), F32)],
        operands=(q, k, p_all, sink_b))

    def bwd(do, xchg=None):
        def bwd_body(q_ref, k_ref, v_ref, s_ref, do_ref, dq_ref, dk_ref, dv_ref, ds_ref):
            i = pl.program_id(2)

            @pl.when(i == 0)
            def _():
                dk_ref[...] = jnp.zeros_like(dk_ref)
                dv_ref[...] = jnp.zeros_like(dv_ref)
                ds_ref[...] = jnp.zeros_like(ds_ref)

            @pl.when(i < n_cq)
            def _():
                if ctx_q:
                    _, vjp = jax.vjp(_swa_ctx_block, q_ref[...].astype(F32), k_ref[0:tc, :].astype(F32),
                                     v_ref[0:tc, :], s_ref[...])
                    dq, dk, dv, ds = vjp(do_ref[...])
                    dq_ref[...] = dq
                    dk_ref[0:tc, :] += dk
                    dv_ref[0:tc, :] += dv
                    ds_ref[...] += ds
                else:
                    dq_ref[...] = jnp.zeros_like(dq_ref)

            @pl.when(i >= n_cq)
            def _():
                w0, mask = window(i)
                win = pl.ds(tc + w0, span)
                _, vjp = jax.vjp(functools.partial(_swa_win_block, mask=mask), q_ref[...].astype(F32),
                                 k_ref[0:tc, :].astype(F32), k_ref[win, :].astype(F32),
                                 v_ref[0:tc, :], v_ref[win, :], s_ref[...])
                dq, dkc, dkw, dvc, dvw, ds = vjp(do_ref[...])
                dq_ref[...] = dq
                dk_ref[0:tc, :] += dkc
                dk_ref[win, :] += dkw
                dv_ref[0:tc, :] += dvc
                dv_ref[win, :] += dvw
                ds_ref[...] += ds

        kv_out = pl.BlockSpec((None, t_all, LANE), lambda b, g, i: (b, 0, g))
        ds_spec = pl.BlockSpec((None, None, SWA_GROUP * QB_SWA, LANE), lambda b, g, i: (b, g, 0, 0))
        kv_shape = jax.ShapeDtypeStruct((bsz, t_all, SWA_KV_HEADS * LANE), F32)
        return _call_with_exchange(
            bwd_body, xchg, name=name + "_bwd", grid=grid, in_specs=[q_spec, k_spec, v_spec, s_spec, o_spec],
            out_specs=[q_spec, kv_out, kv_out, ds_spec],
            out_shape=[jax.ShapeDtypeStruct(q.shape, F32), kv_shape, kv_shape,
                       jax.ShapeDtypeStruct((bsz,) + sink_b.shape, F32)],
            operands=(q, k, p_all, sink_b, do))

    return o, gathered, bwd


def _scan_pair(chains, scratch):
    t_all, c = chains[0][0].shape
    n_tiles = t_all // SUBLANE
    row8 = lax.broadcasted_iota(jnp.int32, (t_all, c), 0) % SUBLANE
    refs = [scratch[0:3], scratch[3:6]]
    for (a, u, reverse), (a_s, u_s, _) in zip(chains, refs):
        for d in (1, 2, 4):
            sh = d if not reverse else t_all - d
            ar, ur = pltpu.roll(a, sh, 0), pltpu.roll(u, sh, 0)
            m = (row8 >= d) if not reverse else (row8 < SUBLANE - d)
            u = jnp.where(m, a * ur + u, u)
            a = jnp.where(m, a * ar, a)
        a_s[...] = a
        u_s[...] = u

    def step(j, carries):
        out = []
        for (_, _, reverse), (a_s, u_s, c_s), carry in zip(chains, refs, carries):
            tile = j if not reverse else n_tiles - 1 - j
            base = pl.multiple_of(tile * SUBLANE, SUBLANE)
            c_s[pl.ds(base, SUBLANE), :] = jnp.broadcast_to(carry, (SUBLANE, c))
            last = base + (0 if reverse else SUBLANE - 1)
            out.append(a_s[pl.ds(last, 1), :] * carry + u_s[pl.ds(last, 1), :])
        return tuple(out)

    lax.fori_loop(0, n_tiles, step, (jnp.zeros((1, c), F32),) * 2, unroll=4)
    return [a_s[...] * c_s[...] + u_s[...] for a_s, u_s, c_s in refs]


def _shift_rows(x, reverse_src):
    t_all = x.shape[0]
    row = lax.broadcasted_iota(jnp.int32, x.shape, 0)
    if reverse_src:
        return jnp.where(row == t_all - 1, 0.0, pltpu.roll(x, t_all - 1, 0))
    return jnp.where(row == 0, 0.0, pltpu.roll(x, 1, 0))


def _lru_scan(a0, u0, a1, u1, name):
    bsz, t_all, w = a0.shape
    grid = (bsz, w // LANE)
    spec = pl.BlockSpec((None, t_all, LANE), lambda b, c: (b, 0, c))
    scratch = [pltpu.VMEM((t_all, LANE), F32)] * 6
    shape = jax.ShapeDtypeStruct(a0.shape, F32)

    def fwd_body(a0_ref, u0_ref, a1_ref, u1_ref, h0_ref, h1_ref, *scr):
        h0_ref[...], h1_ref[...] = _scan_pair([(a0_ref[...], u0_ref[...], False), (a1_ref[...], u1_ref[...], True)],
                                              scr)

    h0, h1 = _pcall(fwd_body, name=name, grid=grid, in_specs=[spec] * 4, out_specs=[spec] * 2,
                    out_shape=[shape] * 2, scratch_shapes=scratch, compiler_params=_cparams(2))(a0, u0, a1, u1)

    def bwd(dh0, dh1):
        def bwd_body(a0_ref, h0_ref, g0_ref, a1_ref, h1_ref, g1_ref, da0_ref, du0_ref, da1_ref, du1_ref, *scr):
            g0, g1 = _scan_pair([(_shift_rows(a0_ref[...], True), g0_ref[...], True),
                                 (_shift_rows(a1_ref[...], False), g1_ref[...], False)], scr)
            du0_ref[...] = g0
            da0_ref[...] = g0 * _shift_rows(h0_ref[...], False)
            du1_ref[...] = g1
            da1_ref[...] = g1 * _shift_rows(h1_ref[...], True)

        return _pcall(bwd_body, name=name + "_bwd", grid=grid, in_specs=[spec] * 6, out_specs=[spec] * 4,
                      out_shape=[shape] * 4, scratch_shapes=scratch,
                      compiler_params=_cparams(2))(a0, h0, dh0, a1, h1, dh1)

    return h0, h1, bwd


def _f_mod(x, g, shift, scale):
    return (_rms(x, g, D_MODEL) * (1.0 + scale) + shift,)


def _f_mla_q(cq, ga, w, gh, cos, sa, sb):
    n = _rms(cq, ga, MLA_Q_RANK)
    outs = []
    for wh in _split(w, MLA_HEADS, 1):
        outs.append(_rope(_rms(_nn(n, wh), gh, MLA_QK), cos, sa, sb, MLA_ROPE // 4))
    return (jnp.concatenate(outs, axis=1),)


def _f_mla_kv(ckv, krp, ga, wk, wv, gh, cos, sa, sb):
    n = _rms(ckv, ga, MLA_KV_RANK)
    outs = []
    for wh in _split(wk, MLA_HEADS, 1):
        outs.append(_rope(_rms(_nn(n, wh) + krp, gh, MLA_QK), cos, sa, sb, MLA_ROPE // 4))
    return jnp.concatenate(outs, axis=1), _nn(n, wv)


def _f_conv(x, w0, w1, w2, w3, bias, tc):
    t_all = x.shape[0]
    row = lax.broadcasted_iota(jnp.int32, x.shape, 0)
    lo = jnp.where(row < tc, 0, tc)
    hi = jnp.where(row < tc, tc, t_all)
    y = bias + jnp.zeros_like(x)
    for kk, wk in enumerate((w0, w1, w2, w3)):
        src = row + (kk - 2)
        xs = x if kk == 2 else _roll(x, 2 - kk, 0)
        y = y + wk * jnp.where((src >= lo) & (src < hi), xs, 0.0)
    return (y,)


def _f_gates(xc, w16, b00, b01, b10, b11, sp0, sp1):
    ws = _unstack(w16)
    n_cb = LRU_WIDTH // LANE
    xcs = _split(xc, n_cb, 1)
    bias = [_split(b, n_cb, 1) for b in (b00, b01, b10, b11)]
    sps = [_split(s, n_cb, 1) for s in (sp0, sp1)]
    res = [[], [], [], []]
    for c in range(n_cb):
        for z in range(2):
            r = _sig(_nn(xcs[c], ws[c * 4 + 2 * z]) + bias[2 * z][c])
            i = _sig(_nn(xcs[c], ws[c * 4 + 2 * z + 1]) + bias[2 * z + 1][c])
            la = -LRU_C * r * sps[z][c]
            res[2 * z].append(jnp.exp(la))
            res[2 * z + 1].append(jnp.sqrt(-jnp.tanh(la) * (jnp.exp(2.0 * la) + 1.0)) * (i * xcs[c]))
    return tuple(jnp.concatenate(r, axis=1) for r in res)


def _f_swa_qk(sq, sk, gq, gk, cos, sa, sb):
    qs = [_rope(_rms(x, gq, SWA_HEAD_DIM), cos, sa, sb, SWA_HEAD_DIM // 4) for x in _split(sq, SWA_HEADS, 1)]
    ks = [_rope(_rms(x, gk, SWA_HEAD_DIM), cos, sa, sb, SWA_HEAD_DIM // 4) for x in _split(sk, SWA_KV_HEADS, 1)]
    return jnp.concatenate(qs, axis=1), jnp.concatenate(ks, axis=1)


def _f_qkv(cq, ckv, krp, sq, sk, q_a_g, wuq, mla_q_g, kv_a_g, wk, wv, mla_k_g, swa_q_g, swa_k_g,
           m_cos, m_sa, m_sb, s_cos, s_sa, s_sb):
    return (*_f_mla_q(cq, q_a_g, wuq, mla_q_g, m_cos, m_sa, m_sb),
            *_f_mla_kv(ckv, krp, kv_a_g, wk, wv, mla_k_g, m_cos, m_sa, m_sb),
            *_f_swa_qk(sq, sk, swa_q_g, swa_k_g, s_cos, s_sa, s_sb))


def _f_merge(oa, h0, h1, lg, oc, ga, gb, gc):
    ob = (h0 + h1) * _gelu(lg)
    return (jnp.concatenate([_rms(oa, ga, GROUP_WIDTH), _rms(ob, gb, GROUP_WIDTH), _rms(oc, gc, GROUP_WIDTH)],
                            axis=1),)


def _f_resid_mod(x, y, gate, g, shift, scale):
    x1 = x + gate * y
    return x1, _rms(x1, g, D_MODEL) * (1.0 + scale) + shift


def _f_resid(x, y, gate):
    return (x + gate * y,)


def _hosted(hooks, key, arg=None):
    make, done = hooks.get(key, (None, None))
    xchg = make(arg) if make is not None else None
    return xchg, (done if xchg is not None else lambda outs: None)


def _layer(li, x, mods, w, s, tabs, tc, ctx_q, hooks):
    bsz, t_all, _ = x.shape
    n_t = t_all // TB
    grid = (bsz, n_t)
    rows = lambda b, t: (b, t, 0)

    def row(arr, width=None, idx=0, gdtype=F32, gshape=None):
        width = width or arr.shape[-1]
        return _A(arr, (None, TB, width), lambda b, t: (b, t, idx), "row", gdtype=gdtype, gshape=gshape,
                  gimap=rows if gshape is not None else None)

    def out(width, dtype, imap=rows):
        return ((bsz, t_all, width), dtype, (None, TB, width), imap)

    def modarg(arr):
        return _A(arr, (None, None, 1, D_MODEL), lambda b, t: (b, jnp.minimum(t, 1), 0, 0), "acc",
                  first=lambda b, t: t <= 1)

    def tab(arr):
        return _A(arr, (TB, LANE), lambda b, t: (t, 0), "const")

    def pcol(p_all, col, width):
        return row(p_all, width, col // width, gdtype=BF16, gshape=(bsz, t_all, width))

    nm = lambda base: "%s_l%d" % (base, li)
    sh1, sc1, g1, sh2, sc2, g2 = mods
    m_all = bsz * t_all

    (h,), b_mod1 = _rowop(nm("mod1"), _f_mod, grid, [row(x), _par(s["norm1_g"]), modarg(sh1), modarg(sc1)],
                          [out(D_MODEL, BF16)])
    p_all = _mm(h.reshape(m_all, D_MODEL), w["win"], "nn", F32, nm("mm_in")).reshape(bsz, t_all, P_WIDTH)

    (q_a, k_a, v_a, q_c, k_c), b_qkv = _rowop(
        nm("qkv"), _f_qkv, grid,
        [pcol(p_all, PC_CQ, 256), pcol(p_all, PC_CKV, 128), pcol(p_all, PC_KR, 128), pcol(p_all, PC_SQ, 1024),
         pcol(p_all, PC_SK, 256)]
        + [_par(a) for a in (s["q_a_g"], w["wuq"], s["mla_q_g"], s["kv_a_g"], w["wk"], w["wv"], s["mla_k_g"],
                             s["swa_q_g"], s["swa_k_g"])]
        + [tab(a) for a in tabs["mla"] + tabs["swa"]],
        [out(MLA_HEADS * LANE, BF16), out(MLA_HEADS * LANE, BF16), out(MLA_HEADS * MLA_V, BF16),
         out(SWA_HEADS * LANE, BF16), out(SWA_KV_HEADS * LANE, BF16)])

    xchg, done = _hosted(hooks, "mla_fwd")
    o_a, got, b_attn_a = _mla_attn(q_a, k_a, v_a, tc, ctx_q, nm("mla_attn"), xchg)
    done(got)

    n_cb = LRU_WIDTH // LANE
    conv_grid = (n_cb, bsz)
    cpar = lambda arr: _A(arr, (1, LANE), lambda c, b: (0, c), "acc", first=lambda c, b: b == 0)
    conv_args = [_A(p_all, (None, t_all, LANE), lambda c, b: (b, 0, PC_LX // LANE + c), "row", gdtype=BF16,
                    gshape=(bsz, t_all, LRU_WIDTH), gimap=lambda c, b: (b, 0, c))]
    conv_args += [cpar(a) for a in s["conv_w"]] + [cpar(s["conv_b"])]
    conv_out = [((bsz, t_all, LRU_WIDTH), F32, (None, t_all, LANE), lambda c, b: (b, 0, c))]
    (xc,), b_conv = _rowop(nm("lru_conv"), functools.partial(_f_conv, tc=tc), conv_grid, conv_args, conv_out)
    rot = lambda b, t: (b, (t + n_t - 1) % n_t, 0)
    (a0, u0, a1, u1), b_gates = _rowop(
        nm("lru_gates"), _f_gates, grid,
        [row(xc), _par(s["wbd"])] + [_par(a) for a in s["gate_b"]] + [_par(a) for a in s["sp"]],
        [out(LRU_WIDTH, F32), out(LRU_WIDTH, F32), out(LRU_WIDTH, F32, rot), out(LRU_WIDTH, F32, rot)])
    h0, h1, b_scan = _lru_scan(a0, u0, a1, u1, nm("lru_scan"))
    h1_arg = _A(h1, (None, TB, LRU_WIDTH), rot, "row")

    xchg, done = _hosted(hooks, "swa_fwd")
    o_c, got, b_attn_c = _swa_attn(q_c, k_c, p_all, s["sink_b"], tc, ctx_q, nm("swa_attn"), xchg)
    done(got)

    (y_in,), b_merge = _rowop(nm("merge"), _f_merge, grid,
                              [row(o_a), row(h0), h1_arg, pcol(p_all, PC_LG, 512), row(o_c), _par(s["g_a"]),
                               _par(s["g_b"]), _par(s["g_c"])],
                              [out(MIX_P, BF16)])
    y = _mm(y_in.reshape(m_all, MIX_P), w["wout"], "nn", F32, nm("mm_out")).reshape(bsz, t_all, D_MODEL)
    (x1, hm), b_rm = _rowop(nm("resid_mod"), _f_resid_mod, grid,
                            [row(x), row(y, gdtype=BF16), modarg(g1), _par(s["norm2_g"]), modarg(sh2), modarg(sc2)],
                            [out(D_MODEL, F32), out(D_MODEL, BF16)])
    pre, act = _mm(hm.reshape(m_all, D_MODEL), w["ff1"], "nn", BF16, nm("mm_ff1"), epi="sqrelu")
    y2 = _mm(act, w["ff2"], "nn", F32, nm("mm_ff2")).reshape(bsz, t_all, D_MODEL)
    (x2,), b_res = _rowop(nm("resid"), _f_resid, grid,
                          [_A(x1, (None, TB, D_MODEL), rows, "fwd"), row(y2, gdtype=BF16), modarg(g2)],
                          [out(D_MODEL, F32)])

    def bwd(dx2, hooks):
        dw, ds = {}, {}
        dy2, dg2 = b_res(dx2)
        dy2 = dy2.reshape(m_all, D_MODEL)
        dpre = _mm(dy2, w["ff2"], "nt", BF16, nm("mm_ff2_dx"), epi="dsqrelu", aux=pre)
        dw["ff2"] = _mm(act, dy2, "tn", BF16, nm("mm_ff2_dw"))
        dhm = _mm(dpre, w["ff1"], "nt", F32, nm("mm_ff1_dx")).reshape(bsz, t_all, D_MODEL)
        dw["ff1"] = _mm(hm.reshape(m_all, D_MODEL), dpre, "tn", BF16, nm("mm_ff1_dw"), out_split=N_DEV)
        dxa, dy, dg1, ds["norm2_g"], dsh2, dsc2 = b_rm(dx2, dhm)
        dy = dy.reshape(m_all, D_MODEL)
        dy_in = _mm(dy, w["wout"], "nt", F32, nm("mm_out_dx")).reshape(bsz, t_all, MIX_P)
        dw["wout"] = _mm(y_in.reshape(m_all, MIX_P), dy, "tn", BF16, nm("mm_out_dw"))
        do_a, dh0, dh1, dlg, do_c, ds["g_a"], ds["g_b"], ds["g_c"] = b_merge(dy_in)

        (dq_c, dk_c, dsv, dsink), _ = b_attn_c(do_c)
        ds["sink_b"] = jnp.sum(dsink, axis=0)

        da0, du0, da1, du1 = b_scan(dh0, dh1)
        gates_g = b_gates(da0, du0, da1, du1)
        dxc, ds["wbd"] = gates_g[0], gates_g[1]
        ds["gate_b"], ds["sp"] = list(gates_g[2:6]), list(gates_g[6:8])
        conv_g = b_conv(dxc)
        dlx, ds["conv_w"], ds["conv_b"] = conv_g[0], list(conv_g[1:5]), conv_g[5]

        xchg, done = _hosted(hooks, "mla_bwd", (dw, ds))
        (dq_a, dk_a, dv_a), got = b_attn_a(do_a, xchg)
        done(got)
        (dcq, dckv, dkr, dsq, dsk, ds["q_a_g"], dw["wuq"], ds["mla_q_g"], ds["kv_a_g"], dw["wk"], dw["wv"],
         ds["mla_k_g"], ds["swa_q_g"], ds["swa_k_g"]) = b_qkv(dq_a, dk_a, dv_a, dq_c, dk_c)

        dp = jnp.concatenate([dsq, dlx, dlg, dcq, dsk, dsv.astype(BF16), dckv, dkr], axis=-1)
        dp = dp.reshape(m_all, P_WIDTH)
        dh = _mm(dp, w["win"], "nt", F32, nm("mm_in_dx")).reshape(bsz, t_all, D_MODEL)
        dw["win"] = _mm(h.reshape(m_all, D_MODEL), dp, "tn", BF16, nm("mm_in_dw"))
        dx, ds["norm1_g"], dsh1, dsc1 = b_mod1(dh, add_to_first=dxa)
        return dx, [dsh1, dsc1, dg1, dsh2, dsc2, dg2], dw, ds

    return x2, bwd


def _loss_and_grad(x2, target, tc):
    bsz, t_all, d = x2.shape
    n_t = t_all // TB
    n_c = tc // TB

    def body(x_ref, t_ref, l_ref, dx_ref):
        b, t = pl.program_id(0), pl.program_id(1)

        @pl.when((b == 0) & (t == 0))
        def _():
            l_ref[...] = jnp.zeros_like(l_ref)

        @pl.when(t < n_c)
        def _():
            dx_ref[...] = jnp.zeros_like(dx_ref)

        @pl.when(t >= n_c)
        def _():
            e = x_ref[...] - t_ref[...]
            dx_ref[...] = e * (1.0 / d)
            l_ref[...] += jnp.sum(e * e) * (0.5 / d)

    loss, dx = _pcall(
        body, name="loss", grid=(bsz, n_t),
        in_specs=[pl.BlockSpec((None, TB, d), lambda b, t: (b, t, 0)),
                  pl.BlockSpec((None, TB, d), lambda b, t: (b, jnp.maximum(t - n_c, 0), 0))],
        out_specs=[pl.BlockSpec((SUBLANE, LANE), lambda b, t: (0, 0)),
                   pl.BlockSpec((None, TB, d), lambda b, t: (b, t, 0))],
        out_shape=[jax.ShapeDtypeStruct((SUBLANE, LANE), F32), jax.ShapeDtypeStruct(x2.shape, F32)],
        compiler_params=_cparams(2))(x2, target)
    return loss[0, 0], dx


def _rope_tables(lat, tc, dim, lane0):
    quarter = dim // 4
    pos = np.arange(lat)
    grid_pos = np.stack([pos // GRID_W, pos % GRID_W], axis=-1).astype(np.float32)
    lane = np.arange(LANE)
    p = np.clip(lane - lane0, 0, dim - 1)
    active = (lane >= lane0) & (lane < lane0 + dim)
    axis, half, qi = p // (dim // 2), (p % (dim // 2)) // quarter, p % quarter
    inv = (np.float32(ROPE_THETA) ** (-qi.astype(np.float32) / np.float32(quarter))).astype(np.float32)
    ang = (np.where(axis[None, :] == 0, grid_pos[:, 0:1], grid_pos[:, 1:2]) * inv[None, :]).astype(np.float32)
    cos = np.where(active, np.cos(ang), 1.0).astype(np.float32)
    sin = np.where(active, np.sin(ang), 0.0).astype(np.float32)
    sa = np.where(half == 0, -sin, 0.0).astype(np.float32)
    sb = np.where(half == 1, sin, 0.0).astype(np.float32)
    ctx1, ctx0 = np.ones((tc, LANE), np.float32), np.zeros((tc, LANE), np.float32)
    return tuple(jnp.asarray(np.concatenate([c, t], 0)) for c, t in ((ctx1, cos), (ctx0, sa), (ctx0, sb)))


_BIG = {"w_in": ((D_MODEL, IN_WIDTH // N_DEV), 1, ("win",)),
        "w_uq": ((MLA_Q_RANK, MLA_HEADS * MLA_QK // N_DEV), 1, ("wuq",)),
        "w_ukv": ((MLA_KV_RANK, MLA_HEADS * (MLA_NOPE + MLA_V) // N_DEV), 1, ("wk", "wv")),
        "w_out": ((3 * GROUP_WIDTH // N_DEV, D_MODEL), 0, ("wout",)),
        "w_ff1": ((D_MODEL, D_FF // N_DEV), 1, ("ff1",)),
        "w_ff2": ((D_FF // N_DEV, D_MODEL), 0, ("ff2",))}
_EARLY = ("w_in", "w_uq", "w_ukv")
_LATE = ("w_out", "w_ff1", "w_ff2")


def _pad_heads(wm, n_heads, dim, axis=-1):
    axis = axis % wm.ndim
    shp = wm.shape[:axis] + (n_heads, dim) + wm.shape[axis + 1:]
    pad = [(0, 0)] * len(shp)
    pad[axis + 1] = (0, LANE - dim)
    out = jnp.pad(wm.reshape(shp), pad)
    return out.reshape(wm.shape[:axis] + (n_heads * LANE,) + wm.shape[axis + 1:])


def _prep_weight(name, piece):
    shp, ax, _ = _BIG[name]
    full = jnp.moveaxis(piece, 0, ax).reshape(shp[:ax] + (N_DEV * shp[ax],) + shp[ax + 1:])
    if name == "w_in":
        cq, ckv, kr, lx, lg, sq, sk, sv = _split_cols(full)
        return {"win": jnp.concatenate(
            [_pad_heads(sq, SWA_HEADS, SWA_HEAD_DIM), lx, lg, cq, _pad_heads(sk, SWA_KV_HEADS, SWA_HEAD_DIM),
             _pad_heads(sv, SWA_KV_HEADS, SWA_HEAD_DIM), ckv, jnp.pad(kr, ((0, 0), (MLA_NOPE, LANE - MLA_QK)))], axis=1)}
    if name == "w_uq":
        return {"wuq": _pad_heads(full, MLA_HEADS, MLA_QK)}
    if name == "w_ukv":
        ukv = full.reshape(MLA_KV_RANK, MLA_HEADS, MLA_NOPE + MLA_V)
        return {"wk": _pad_heads(ukv[:, :, :MLA_NOPE].reshape(MLA_KV_RANK, -1), MLA_HEADS, MLA_NOPE),
                "wv": ukv[:, :, MLA_NOPE:].reshape(MLA_KV_RANK, -1)}
    return {_BIG[name][2][0]: full}


def _split_cols(wm):
    parts, start = [], 0
    for size in IN_SIZES:
        parts.append(wm[:, start:start + size])
        start += size
    return parts


def _prep_gates(gate_w):
    gw = gate_w.reshape(2, 2, 4, 2, 64, 64)
    wbd = jnp.einsum("zgknCm,nN->knCzgNm", gw, jnp.eye(2, dtype=F32)).reshape(4, LANE, 4, LANE)
    return wbd.transpose(0, 2, 1, 3).reshape(16, LANE, LANE)


def _prep_small(raw):
    r1 = lambda a: a.reshape(1, -1)
    gg = raw["group_g"]
    sink = raw["swa_sink"].reshape(SWA_KV_HEADS, SWA_GROUP, 1, 1)
    return {
        "norm1_g": r1(raw["norm1_g"]), "norm2_g": r1(raw["norm2_g"]),
        "q_a_g": r1(raw["q_a_g"]), "kv_a_g": r1(raw["kv_a_g"]),
        "mla_q_g": jnp.pad(r1(raw["mla_q_g"]), ((0, 0), (0, LANE - MLA_QK))),
        "mla_k_g": jnp.pad(r1(raw["mla_k_g"]), ((0, 0), (0, LANE - MLA_QK))),
        "swa_q_g": jnp.pad(r1(raw["swa_q_g"]), ((0, 0), (0, LANE - SWA_HEAD_DIM))),
        "swa_k_g": jnp.pad(r1(raw["swa_k_g"]), ((0, 0), (0, LANE - SWA_HEAD_DIM))),
        "conv_w": [r1(raw["conv_w"][kk]) for kk in range(4)], "conv_b": r1(raw["conv_b"]),
        "gate_b": [r1(raw["lru_gate_b"][z, g]) for z in range(2) for g in range(2)],
        "sp": [r1(jax.nn.softplus(-raw["lru_lambda"][z])) for z in range(2)],
        "sink_b": jnp.broadcast_to(sink, (SWA_KV_HEADS, SWA_GROUP, QB_SWA, LANE)).reshape(
            SWA_KV_HEADS, SWA_GROUP * QB_SWA, LANE),
        "g_a": r1(gg[:GROUP_WIDTH]), "g_b": r1(gg[GROUP_WIDTH:2 * GROUP_WIDTH]), "g_c": r1(gg[2 * GROUP_WIDTH:])}


def _mesh_pos():
    return lax.axis_index("x"), lax.axis_index("y"), lax.axis_index("c")


def _peer(pos, k):
    return tuple(1 - p if (k >> s) & 1 else p for p, s in zip(pos, (2, 1, 0)))


def _dev_index(pos):
    return 4 * pos[0] + 2 * pos[1] + pos[2]


class _Exchange:
    def __init__(self, bufs, gather):
        self.bufs = list(bufs)
        self.n = len(self.bufs)
        self.gather = [gather] * self.n if isinstance(gather, bool) else list(gather)
        self.specs = [pl.BlockSpec(memory_space=pl.ANY)] * self.n
        self.out_shape = [jax.ShapeDtypeStruct((N_DEV,) + tuple(b.shape if g else b.shape[1:]), b.dtype)
                          for b, g in zip(self.bufs, self.gather)]
        self.scratch = [pltpu.SemaphoreType.DMA(((N_DEV - 1) * self.n,)),
                        pltpu.SemaphoreType.DMA(((N_DEV - 1) * self.n,)), pltpu.SemaphoreType.DMA((self.n,))]

    def _copies(self, x_refs, o_refs, sems, with_recvs):
        send_sems, recv_sems, local_sems = sems
        pos = _mesh_pos()
        me = _dev_index(pos)
        locals_, sends, recvs = [], [], []
        for j in range(self.n):
            src_mine = x_refs[j] if self.gather[j] else x_refs[j].at[me]
            locals_.append(pltpu.make_async_copy(src_mine, o_refs[j].at[me], local_sems.at[j]))
        for k in range(1, N_DEV):
            peer = _peer(pos, k)
            pidx = _dev_index(peer)
            for j in range(self.n):
                src = x_refs[j] if self.gather[j] else x_refs[j].at[pidx]
                sem = (k - 1) * self.n + j
                sends.append(pltpu.make_async_remote_copy(
                    src_ref=src, dst_ref=o_refs[j].at[me], send_sem=send_sems.at[sem], recv_sem=recv_sems.at[sem],
                    device_id=peer, device_id_type=pl.DeviceIdType.MESH))
                if with_recvs:
                    recvs.append(pltpu.make_async_remote_copy(
                        src_ref=src, dst_ref=o_refs[j].at[pidx], send_sem=send_sems.at[sem],
                        recv_sem=recv_sems.at[sem], device_id=peer, device_id_type=pl.DeviceIdType.MESH))
        return locals_, sends, recvs

    def start(self, x_refs, o_refs, sems):
        locals_, sends, _ = self._copies(x_refs, o_refs, sems, False)
        for cp in locals_ + sends:
            cp.start()

    def wait(self, x_refs, o_refs, sems):
        locals_, sends, recvs = self._copies(x_refs, o_refs, sems, True)
        for cp in recvs:
            cp.wait_recv()
        for cp in sends:
            cp.wait_send()
        for cp in locals_:
            cp.wait()


def _exchange(bufs, gather, name):
    xchg = _Exchange(bufs, gather)
    n = xchg.n

    def body(*refs):
        xchg.start(refs[:n], refs[n:2 * n], refs[2 * n:])
        xchg.wait(refs[:n], refs[n:2 * n], refs[2 * n:])

    return _pcall(body, name=name, out_shape=xchg.out_shape, in_specs=xchg.specs, out_specs=xchg.specs,
                  scratch_shapes=xchg.scratch)(*xchg.bufs)


def _pack(arrs, dtype):
    flat = jnp.concatenate([a.reshape(-1).astype(dtype) for a in arrs])
    rows = -(-flat.size // PACK_W)
    rows = -(-rows // 16) * 16
    return jnp.pad(flat, (0, rows * PACK_W - flat.size)).reshape(rows, PACK_W)


def _unpack(buf, shapes, lead=()):
    flat = buf.reshape(lead + (-1,))
    out, off = [], 0
    for shp in shapes:
        n = math.prod(shp)
        out.append(flat[..., off:off + n].reshape(lead + tuple(shp)))
        off += n
    return out


def _sum_sources(buf, name):
    _, r, c = buf.shape
    tr = _rows_tile(r)

    def body(x_ref, o_ref):
        acc = x_ref[0]
        for d in range(1, N_DEV):
            acc = acc + x_ref[d]
        o_ref[...] = acc

    return _pcall(body, name=name, grid=(r // tr,),
                  in_specs=[pl.BlockSpec((N_DEV, tr, c), lambda i: (0, i, 0))],
                  out_specs=pl.BlockSpec((tr, c), lambda i: (i, 0)),
                  out_shape=jax.ShapeDtypeStruct((r, c), F32), compiler_params=_cparams(1))(buf)


def _rows_tile(r):
    best = r
    for t in range(SUBLANE, 257, SUBLANE):
        if r % t == 0:
            best = t
    return best


def _adamw(grads, wgt, m, v, name):
    n_lay = len(grads)
    n_src, r, c = grads[0].shape
    tr = _rows_tile(r)
    n_blk = r // tr
    bc1 = 1.0 - ADAM_B1 ** ADAM_STEP
    bc2 = 1.0 - ADAM_B2 ** ADAM_STEP

    def body(*refs):
        g_refs, (w_ref, m_ref, v_ref, go_ref, d_ref, mo_ref, vo_ref) = refs[:n_lay], refs[n_lay:]
        for li, g_ref in enumerate(g_refs):
            @pl.when(pl.program_id(0) == li)
            def _():
                g = g_ref[0].astype(F32)
                for d in range(1, n_src):
                    g = g + g_ref[d].astype(F32)
                m_new = ADAM_B1 * m_ref[...] + (1.0 - ADAM_B1) * g
                v_new = ADAM_B2 * v_ref[...] + (1.0 - ADAM_B2) * (g * g)
                go_ref[...] = g
                mo_ref[...] = m_new
                vo_ref[...] = v_new
                d_ref[...] = -ADAM_LR * ((m_new / bc1) / (jnp.sqrt(v_new / bc2) + ADAM_EPS) + ADAM_WD * w_ref[...])

    g_specs = [pl.BlockSpec((n_src, tr, c),
                            lambda l, i, li=li: (0, jnp.where(l == li, i, jnp.where(l > li, n_blk - 1, 0)), 0))
               for li in range(n_lay)]
    spec = pl.BlockSpec((tr, c), lambda l, i: (l * n_blk + i, 0))
    return _pcall(body, name=name, grid=(n_lay, n_blk), in_specs=g_specs + [spec, spec, spec],
                  out_specs=[spec] * 4, out_shape=[jax.ShapeDtypeStruct((n_lay * r, c), F32)] * 4,
                  compiler_params=_cparams(2))(*grads, wgt, m, v)


def _silu(z):
    return z * jax.nn.sigmoid(z)


_WEIGHTS = ("c_ctx", "w_mod", "b_mod", "norm1_g", "w_in", "q_a_g", "w_uq", "kv_a_g", "w_ukv", "mla_q_g", "mla_k_g",
            "conv_w", "conv_b", "lru_gate_w", "lru_gate_b", "lru_lambda", "swa_q_g", "swa_k_g", "swa_sink", "group_g",
            "w_out", "norm2_g", "w_ff1", "w_ff2")
_SHARDED_SMALL = ("conv_w", "lru_gate_b", "lru_lambda")
_REPL_RAW = ("norm1_g", "q_a_g", "kv_a_g", "mla_q_g", "mla_k_g", "conv_b", "swa_q_g", "swa_k_g",
             "swa_sink", "group_g", "norm2_g")
MOD_ROWS = 32


def kernel(x, c, ctx, c_ctx, w_mod, b_mod, norm1_g, w_in, q_a_g, w_uq, kv_a_g, w_ukv, mla_q_g, mla_k_g, conv_w, conv_b, lru_gate_w, lru_gate_b, lru_lambda, swa_q_g, swa_k_g, swa_sink, group_g, w_out, norm2_g, w_ff1, w_ff2, loss_target, m_c_ctx, m_w_mod, m_b_mod, m_norm1_g, m_w_in, m_q_a_g, m_w_uq, m_kv_a_g, m_w_ukv, m_mla_q_g, m_mla_k_g, m_conv_w, m_conv_b, m_lru_gate_w, m_lru_gate_b, m_lru_lambda, m_swa_q_g, m_swa_k_g, m_swa_sink, m_group_g, m_w_out, m_norm2_g, m_w_ff1, m_w_ff2, v_c_ctx, v_w_mod, v_b_mod, v_norm1_g, v_w_in, v_q_a_g, v_w_uq, v_kv_a_g, v_w_ukv, v_mla_q_g, v_mla_k_g, v_conv_w, v_conv_b, v_lru_gate_w, v_lru_gate_b, v_lru_lambda, v_swa_q_g, v_swa_k_g, v_swa_sink, v_group_g, v_w_out, v_norm2_g, v_w_ff1, v_w_ff2):
    wts = dict(c_ctx=c_ctx, w_mod=w_mod, b_mod=b_mod, norm1_g=norm1_g, w_in=w_in, q_a_g=q_a_g, w_uq=w_uq,
               kv_a_g=kv_a_g, w_ukv=w_ukv, mla_q_g=mla_q_g, mla_k_g=mla_k_g, conv_w=conv_w, conv_b=conv_b,
               lru_gate_w=lru_gate_w, lru_gate_b=lru_gate_b, lru_lambda=lru_lambda, swa_q_g=swa_q_g, swa_k_g=swa_k_g,
               swa_sink=swa_sink, group_g=group_g, w_out=w_out, norm2_g=norm2_g, w_ff1=w_ff1, w_ff2=w_ff2)
    mom1 = dict(zip(_WEIGHTS, (m_c_ctx, m_w_mod, m_b_mod, m_norm1_g, m_w_in, m_q_a_g, m_w_uq, m_kv_a_g, m_w_ukv,
                               m_mla_q_g, m_mla_k_g, m_conv_w, m_conv_b, m_lru_gate_w, m_lru_gate_b, m_lru_lambda,
                               m_swa_q_g, m_swa_k_g, m_swa_sink, m_group_g, m_w_out, m_norm2_g, m_w_ff1, m_w_ff2)))
    mom2 = dict(zip(_WEIGHTS, (v_c_ctx, v_w_mod, v_b_mod, v_norm1_g, v_w_in, v_q_a_g, v_w_uq, v_kv_a_g, v_w_ukv,
                               v_mla_q_g, v_mla_k_g, v_conv_w, v_conv_b, v_lru_gate_w, v_lru_gate_b, v_lru_lambda,
                               v_swa_q_g, v_swa_k_g, v_swa_sink, v_group_g, v_w_out, v_norm2_g, v_w_ff1, v_w_ff2)))
    bsz = x.shape[0]
    n_ex = bsz * N_DEV
    me = _dev_index(_mesh_pos())
    mod_cols = w_mod.shape[-1]

    small_shapes = [c.shape, conv_w.shape, lru_gate_b.shape, lru_lambda.shape]
    shard = lambda n, li: wts[n][li].astype(BF16)
    g_small, *early_pieces = _exchange([_pack([c, conv_w, lru_gate_b, lru_lambda], F32)] + [shard(n, 0) for n in _EARLY],
                                       True, "ag_first")
    c_all, conv_w_all, gate_b_all, lam_all = _unpack(g_small, small_shapes, lead=(N_DEV,))
    c_all = c_all.reshape(n_ex, D_MODEL)
    cat_last = lambda a: jnp.moveaxis(a, 0, -2).reshape(a.shape[1:-1] + (N_DEV * a.shape[-1],))
    conv_w_full, gate_b_full, lam_full = cat_last(conv_w_all), cat_last(gate_b_all), cat_last(lam_all)

    act = jnp.zeros((MOD_ROWS, D_MODEL), F32).at[:n_ex].set(_silu(c_all)).at[n_ex].set(_silu(c_ctx))
    mod_part = jnp.concatenate([_mm(act, w_mod[li], "nn", F32, "mm_mod_l%d" % li) for li in range(DEPTH)], axis=1)
    (mod_all,) = _exchange([mod_part], True, "ag_mod")
    mods = []
    for li in range(DEPTH):
        full = jnp.moveaxis(mod_all[:, :, li * mod_cols:(li + 1) * mod_cols], 0, 1).reshape(MOD_ROWS, -1) + b_mod[li]
        mine = lax.dynamic_slice_in_dim(full, me * bsz, bsz, axis=0)
        ctx_row = jnp.broadcast_to(full[n_ex], mine.shape)
        both = jnp.stack([ctx_row, mine], axis=1).reshape(bsz, 2, N_MOD, 1, D_MODEL)
        mods.append([both[:, :, j] for j in range(N_MOD)])

    raw = {n: wts[n] for n in _REPL_RAW}
    raw.update(conv_w=conv_w_full, lru_gate_b=gate_b_full, lru_lambda=lam_full)
    small_names = list(_REPL_RAW) + list(_SHARDED_SMALL)
    sp, small_vjp, gates_vjp = [None] * DEPTH, [None] * DEPTH, [None] * DEPTH
    for li in range(DEPTH):
        sp[li], small_vjp[li] = jax.vjp(_prep_small, {n: raw[n][li] for n in small_names})
        sp[li]["wbd"], gates_vjp[li] = jax.vjp(_prep_gates, lru_gate_w[li])

    w, w_vjp, g_recv, small_recv = [{} for _ in range(DEPTH)], {}, {}, {}

    def take(li, names, pieces):
        for n, piece in zip(names, pieces):
            out, w_vjp[n, li] = jax.vjp(functools.partial(_prep_weight, n), piece)
            w[li].update(out)

    def gather_hook(li, names):
        return (lambda _: _Exchange([shard(n, li) for n in names], True), lambda got: take(li, names, got))

    def wgrad(n, li, dwl):
        if n == "w_ff1":
            return dwl["ff1"]
        (g,) = w_vjp[n, li]({k: dwl[k].astype(BF16) for k in _BIG[n][2]})
        return g

    def small_pack(li, ds_l, extra=()):
        (d_raw,) = small_vjp[li]({k: v for k, v in ds_l.items() if k != "wbd"})
        return _pack([d_raw[n] for n in small_names] + list(extra), F32)

    def gates_grad(li, ds_l):
        return gates_vjp[li](ds_l["wbd"])[0].reshape(-1, LANE)

    take(0, _EARLY, early_pieces)
    hooks_fwd = [{"mla_fwd": gather_hook(0, _LATE), "swa_fwd": gather_hook(1, _EARLY + ("w_out",))},
                 {"mla_fwd": gather_hook(1, ("w_ff1", "w_ff2"))}]
    bwd_state = {}

    def scatter_last_layer(grads_so_far):
        return _Exchange([wgrad(n, 1, grads_so_far[0]) for n in _LATE], False)

    def scatter_first_layer(grads_so_far):
        dwl, dsl = grads_so_far
        dw1, ds1 = bwd_state["dw1"], bwd_state["ds1"]
        bufs = [wgrad(n, 1, dw1) for n in _EARLY] + [wgrad(n, 0, dwl) for n in _LATE]
        bufs += [small_pack(1, ds1), gates_grad(1, ds1), gates_grad(0, dsl)]
        return _Exchange(bufs, [False] * (len(_EARLY) + len(_LATE)) + [True] * 3)

    def scattered_first_layer(got):
        g_recv.update(zip([(n, 1) for n in _EARLY] + [(n, 0) for n in _LATE], got[:-3]))
        small_recv[1], g_recv["lru_gate_w", 1], g_recv["lru_gate_w", 0] = got[-3:]

    hooks_bwd = [{"mla_bwd": (scatter_first_layer, scattered_first_layer)},
                 {"mla_bwd": (scatter_last_layer, lambda got: g_recv.update(zip([(n, 1) for n in _LATE], got)))}]

    tc, lat = ctx.shape[1], x.shape[1]
    tabs = {"mla": _rope_tables(lat, tc, MLA_ROPE, MLA_NOPE), "swa": _rope_tables(lat, tc, SWA_HEAD_DIM, 0)}
    stream = jnp.concatenate([ctx, x], axis=1)
    bwds = []
    for li in range(DEPTH):
        stream, bwd = _layer(li, stream, mods[li], w[li], sp[li], tabs, tc, li < DEPTH - 1, hooks_fwd[li])
        bwds.append(bwd)
    loss_part, dstream = _loss_and_grad(stream, loss_target, tc)
    dmods = [None] * DEPTH
    dstream, dmods[1], bwd_state["dw1"], bwd_state["ds1"] = bwds[1](dstream, hooks_bwd[1])
    dstream, dmods[0], dw0, ds0 = bwds[0](dstream, hooks_bwd[0])
    grad_x = dstream[:, tc:]

    dm_rows = []
    for li in range(DEPTH):
        dm = jnp.concatenate(dmods[li], axis=-1)
        dm_rows.append(jnp.concatenate([dm[:, 1, 0], jnp.sum(dm[:, 0, 0], axis=0, keepdims=True)], axis=0))
    dm_mine = jnp.concatenate(dm_rows, axis=1)
    dm_mine = jnp.pad(dm_mine, ((0, SUBLANE - bsz - 1), (0, 0)))
    (dm_all,) = _exchange([dm_mine], True, "ag_dmod")
    g_wmod, g_bmod, dact_ctx = [], [], jnp.zeros((D_MODEL,), F32)
    for li in range(DEPTH):
        part = dm_all[:, :, li * N_MOD * D_MODEL:(li + 1) * N_MOD * D_MODEL]
        dm32 = jnp.zeros((MOD_ROWS, N_MOD * D_MODEL), F32).at[:n_ex].set(part[:, :bsz].reshape(n_ex, -1))
        dm32 = dm32.at[n_ex].set(jnp.sum(part[:, bsz], axis=0))
        g_bmod.append(jnp.sum(dm32, axis=0))
        cols = lax.dynamic_slice_in_dim(dm32, me * mod_cols, mod_cols, axis=1)
        g_wmod.append(_mm(act, cols, "tn", F32, "mm_mod_dw_l%d" % li))
        dact_ctx = dact_ctx + _mm(cols, w_mod[li], "nt", F32, "mm_mod_dx_l%d" % li)[n_ex]
    sg = jax.nn.sigmoid(c_ctx)
    g_cctx_part = dact_ctx * (sg * (1.0 + c_ctx * (1.0 - sg)))

    last = _exchange([wgrad(n, 0, dw0) for n in _EARLY] + [small_pack(0, ds0, (g_cctx_part, loss_part.reshape(1)))],
                     [False] * len(_EARLY) + [True], "rs_early")
    g_recv.update(zip([(n, 0) for n in _EARLY], last[:-1]))
    small_recv[0] = last[-1]
    layer_shapes = [raw[n].shape[1:] for n in small_names]
    tot = [_unpack(_sum_sources(small_recv[li], "sum_grads_l%d" % li), layer_shapes + [(D_MODEL,), (1,)][:2 * (li == 0)])
           for li in range(DEPTH)]
    grads = {n: jnp.stack([tot[li][j] for li in range(DEPTH)], axis=0) for j, n in enumerate(small_names)}
    grads["c_ctx"], loss = tot[0][-2], tot[0][-1][0]
    for n in _SHARDED_SMALL:
        width = wts[n].shape[-1]
        grads[n] = lax.dynamic_slice_in_dim(grads[n], me * width, width, axis=grads[n].ndim - 1)
    grads["b_mod"] = jnp.stack(g_bmod, axis=0)

    delta, new_m, new_v = {}, {}, {}
    per_layer = {n: [g_recv[n, li] for li in range(DEPTH)] for n in list(_BIG) + ["lru_gate_w"]}
    per_layer["w_mod"] = [g[None] for g in g_wmod]
    for n, srcs in per_layer.items():
        two_d = (DEPTH * math.prod(wts[n].shape[1:-1]), wts[n].shape[-1])
        srcs = [s.reshape((s.shape[0], two_d[0] // DEPTH, two_d[1])) for s in srcs]
        res = _adamw(srcs, wts[n].reshape(two_d), mom1[n].reshape(two_d), mom2[n].reshape(two_d), "adamw_" + n)
        grads[n], delta[n], new_m[n], new_v[n] = [r.reshape(wts[n].shape) for r in res]
    rest = [n for n in _WEIGHTS if n not in delta]
    shapes = [wts[n].shape for n in rest]
    res = _adamw([_pack([grads[n] for n in rest], F32)[None]], _pack([wts[n] for n in rest], F32),
                 _pack([mom1[n] for n in rest], F32), _pack([mom2[n] for n in rest], F32), "adamw_small")
    for tgt, buf in zip((delta, new_m, new_v), res[1:]):
        tgt.update(zip(rest, _unpack(buf, shapes)))

    return (loss, grad_x, *[grads[n] for n in _WEIGHTS], *[delta[n] for n in _WEIGHTS],
            *[new_m[n] for n in _WEIGHTS], *[new_v[n] for n in _WEIGHTS])
```

```python
import functools
import math

import jax
import jax.numpy as jnp
import numpy as np
from jax import lax
from jax.experimental import pallas as pl
from jax.experimental.pallas import tpu as pltpu

F32, BF16 = jnp.float32, jnp.bfloat16

N_DEV = 8
DEPTH = 2
D_MODEL = 1024
D_FF = 4096
N_MOD = 6
GRID_W = 64
WINDOW = 128
ROPE_THETA = 10000.0
EPS = 1e-6
NEG_INF = -1e30
LRU_C = 8.0
LRU_WIDTH = 512
MLA_HEADS, MLA_NOPE, MLA_ROPE, MLA_V = 8, 64, 32, 64
MLA_QK = MLA_NOPE + MLA_ROPE
MLA_Q_RANK, MLA_KV_RANK = 256, 128
SWA_HEADS, SWA_KV_HEADS, SWA_GROUP, SWA_HEAD_DIM = 8, 2, 4, 64
GROUP_WIDTH = 512
IN_SIZES = (256, 128, 32, 512, 512, 512, 128, 128)
IN_WIDTH = sum(IN_SIZES)
ADAM_LR, ADAM_B1, ADAM_B2, ADAM_EPS, ADAM_WD, ADAM_STEP = 0.001, 0.9, 0.999, 1e-08, 0.01, 10

LANE = 128
SUBLANE = 8
TB = 256
QB_SWA = 256
PACK_W = 1024
MM_K_MAX = 4608
MM_ROWS, MM_COLS_MAX = 512, 1024
ELEMWISE_ROWS_MAX = 256
MLA_HPS = 2
VMEM_LIMIT = 56 * 1024 * 1024
P_WIDTH = 3072
PC_SQ, PC_LX, PC_LG, PC_CQ, PC_SK, PC_SV, PC_CKV, PC_KR = 0, 1024, 1536, 2048, 2304, 2560, 2816, 2944
MIX_P = 1536


def _pcall(body, **kw):
    return pl.pallas_call(body, **kw)


def _cparams(n_grid):
    return pltpu.CompilerParams(dimension_semantics=("arbitrary",) * n_grid, vmem_limit_bytes=VMEM_LIMIT)


def _dg(a, b, ca, cb):
    return lax.dot_general(a.astype(BF16), b.astype(BF16), (((ca,), (cb,)), ((), ())),
                           preferred_element_type=F32)


@jax.custom_vjp
def _nn(a, b):
    return _dg(a, b, 1, 0)


@jax.custom_vjp
def _nt(a, b):
    return _dg(a, b, 1, 1)


@jax.custom_vjp
def _tn(a, b):
    return _dg(a, b, 0, 0)


_nn.defvjp(lambda a, b: (_nn(a, b), (a, b)), lambda r, ct: (_nt(ct, r[1]), _tn(r[0], ct)))
_nt.defvjp(lambda a, b: (_nt(a, b), (a, b)), lambda r, ct: (_nn(ct, r[1]), _tn(ct, r[0])))
_tn.defvjp(lambda a, b: (_tn(a, b), (a, b)), lambda r, ct: (_nt(r[1], ct), _nn(r[0], ct)))


@functools.partial(jax.custom_vjp, nondiff_argnums=(1, 2))
def _roll(x, shift, axis):
    return pltpu.roll(x, shift % x.shape[axis], axis)


_roll.defvjp(lambda x, shift, axis: (_roll(x, shift, axis), None),
             lambda shift, axis, _, ct: (_roll(ct, -shift, axis),))


@functools.partial(jax.custom_vjp, nondiff_argnums=(1, 2))
def _split(x, n, axis):
    w = x.shape[axis] // n
    return tuple(lax.slice_in_dim(x, i * w, (i + 1) * w, axis=axis) for i in range(n))


_split.defvjp(lambda x, n, axis: (_split(x, n, axis), None),
              lambda n, axis, _, cts: (jnp.concatenate(cts, axis=axis),))


@jax.custom_vjp
def _unstack(x):
    return tuple(x[i] for i in range(x.shape[0]))


_unstack.defvjp(lambda x: (_unstack(x), None), lambda _, cts: (jnp.stack(cts, axis=0),))


def _sig(x):
    return 0.5 * (jnp.tanh(0.5 * x) + 1.0)


def _gelu(x):
    return 0.5 * x * (1.0 + jnp.tanh(math.sqrt(2.0 / math.pi) * (x + 0.044715 * (x * x * x))))


def _rms(x, g, n):
    ms = jnp.sum(x * x, axis=-1, keepdims=True) * (1.0 / n)
    return x * lax.rsqrt(ms + EPS) * g


def _rope(y, cos, sa, sb, quarter):
    return y * cos + _roll(y, -quarter, 1) * sa + _roll(y, quarter, 1) * sb


def _softmax_rows(s, extra=None):
    m = jnp.max(s, axis=-1, keepdims=True)
    if extra is not None:
        m = jnp.maximum(m, extra)
    m = lax.stop_gradient(m)
    e = jnp.exp(s - m)
    den = jnp.sum(e, axis=-1, keepdims=True)
    if extra is not None:
        den = den + jnp.exp(extra - m)
    return e / den


class _A:
    def __init__(self, arr, block, imap, kind="row", first=None, gdtype=F32, gshape=None, gimap=None):
        self.arr, self.block, self.imap, self.kind, self.first = arr, block, imap, kind, first
        self.gdtype, self.gshape, self.gimap = gdtype, gshape, gimap


def _all_zero(*ids):
    return functools.reduce(jnp.logical_and, [i == 0 for i in ids])


def _par(arr):
    nd = arr.ndim
    return _A(arr, arr.shape, lambda *ids: (0,) * nd, "acc", first=_all_zero)


def _op_fwd(name, fn, grid, args, outs):
    n_in = len(args)

    def body(*refs):
        vals = [r[...].astype(F32) for r in refs[:n_in]]
        for r, v in zip(refs[n_in:], fn(*vals)):
            r[...] = v.astype(r.dtype)

    return _pcall(
        body, name=name, grid=grid,
        in_specs=[pl.BlockSpec(a.block, a.imap) for a in args],
        out_specs=[pl.BlockSpec(o[2], o[3]) for o in outs],
        out_shape=[jax.ShapeDtypeStruct(o[0], o[1]) for o in outs],
        compiler_params=_cparams(len(grid)),
    )(*[a.arr for a in args])


def _op_bwd(name, fn, grid, args, outs, ct_arrays, add_to_first=None):
    didx = [i for i, a in enumerate(args) if a.kind not in ("const", "fwd")]
    read = [i for i, a in enumerate(args) if a.kind != "fwd"]
    n_in, n_ct = len(read), len(outs)
    n_add = 0 if add_to_first is None else 1

    def body(*refs):
        ids = [pl.program_id(i) for i in range(len(grid))]
        vals = [jnp.zeros([d for d in a.block if d is not None], F32) for a in args]
        for i, r in zip(read, refs[:n_in]):
            vals[i] = r[...].astype(F32)

        def g(*dv):
            full = list(vals)
            for i, v in zip(didx, dv):
                full[i] = v
            return tuple(fn(*full))

        _, vjp = jax.vjp(g, *[vals[i] for i in didx])
        grads = list(vjp(tuple(r[...].astype(F32) for r in refs[n_in:n_in + n_ct])))
        if n_add:
            grads[0] = grads[0] + refs[n_in + n_ct][...]
        for gr, i, r in zip(grads, didx, refs[n_in + n_ct + n_add:]):
            a = args[i]
            if a.kind == "row":
                r[...] = gr.astype(r.dtype)
            else:
                first = a.first(*ids)

                @pl.when(first)
                def _():
                    r[...] = gr

                @pl.when(jnp.logical_not(first))
                def _():
                    r[...] += gr

    g_specs, g_shapes = [], []
    for i in didx:
        a = args[i]
        if a.kind == "row":
            g_specs.append(pl.BlockSpec(a.block, a.gimap or a.imap))
            g_shapes.append(jax.ShapeDtypeStruct(a.gshape or a.arr.shape, a.gdtype))
        else:
            g_specs.append(pl.BlockSpec(a.block, a.imap))
            g_shapes.append(jax.ShapeDtypeStruct(a.arr.shape, F32))
    return _pcall(
        body, name=name, grid=grid,
        in_specs=[pl.BlockSpec(args[i].block, args[i].imap) for i in read] + [pl.BlockSpec(o[2], o[3]) for o in outs]
        + g_specs[:n_add],
        out_specs=g_specs, out_shape=g_shapes,
        compiler_params=_cparams(len(grid)),
    )(*[args[i].arr for i in read], *ct_arrays, *([add_to_first] if n_add else []))


def _rowop(name, fn, grid, args, outs):
    res = _op_fwd(name, fn, grid, args, outs)
    return res, lambda *cts, add_to_first=None: _op_bwd(name + "_bwd", fn, grid, args, outs, cts, add_to_first)


def _pick(n, cap):
    best = None
    for t in range(LANE, cap + 1, LANE):
        if n % t == 0:
            best = t
    return best or n


def _mm(a, b, mode, out_dtype, name, epi=None, aux=None, out_split=None):
    if mode == "nn":
        (m, k), n = a.shape, b.shape[1]
    elif mode == "nt":
        (m, k), n = a.shape, b.shape[0]
    else:
        (k, m), n = a.shape, b.shape[1]
    assert k <= MM_K_MAX
    tm = MM_ROWS if m % MM_ROWS == 0 else m
    tn = n // out_split if out_split else _pick(n, MM_COLS_MAX)
    a_spec = pl.BlockSpec((k, tm), lambda j, i: (0, i)) if mode == "tn" else pl.BlockSpec((tm, k), lambda j, i: (i, 0))
    b_spec = pl.BlockSpec((tn, k), lambda j, i: (j, 0)) if mode == "nt" else pl.BlockSpec((k, tn), lambda j, i: (0, j))
    dims = {"nn": (1, 0), "nt": (1, 1), "tn": (0, 0)}[mode]
    aux_spec = pl.BlockSpec((tm, tn), lambda j, i: (i, j))
    if out_split:
        o_spec, o_shape = pl.BlockSpec((None, tm, tn), lambda j, i: (j, i, 0)), (out_split, m, tn)
    else:
        o_spec, o_shape = aux_spec, (m, n)
    n_aux = 0 if aux is None else 1
    n_out = 2 if epi == "sqrelu" else 1

    def body(*refs):
        o_refs = refs[2 + n_aux:]
        r = _dg(refs[0][...], refs[1][...], *dims)
        if epi == "sqrelu":
            o_refs[0][...] = r.astype(o_refs[0].dtype)
            rl = jnp.maximum(r, 0.0)
            o_refs[1][...] = (rl * rl).astype(o_refs[1].dtype)
        elif epi == "dsqrelu":
            pre = refs[2][...].astype(F32)
            o_refs[0][...] = (r * (2.0 * jnp.maximum(pre, 0.0))).astype(o_refs[0].dtype)
        else:
            o_refs[0][...] = r.astype(o_refs[0].dtype)

    res = _pcall(
        body, name=name, grid=(n // tn, m // tm),
        in_specs=[a_spec, b_spec] + [aux_spec] * n_aux, out_specs=[o_spec] * n_out,
        out_shape=[jax.ShapeDtypeStruct(o_shape, out_dtype)] * n_out, compiler_params=_cparams(2),
    )(a, b, *([aux] if aux is not None else []))
    return res if n_out == 2 else res[0]


ROW_CHUNK = 16


def _softmax_chunks(s_scr, n_keys, scale, emit):
    for r0 in range(0, s_scr.shape[0], ROW_CHUNK):
        rows = slice(r0, r0 + ROW_CHUNK)
        s = s_scr[rows, :n_keys]
        e = jnp.exp((s - jnp.max(s, axis=-1, keepdims=True)) * scale)
        emit(rows, e, 1.0 / jnp.sum(e, axis=-1, keepdims=True))


def _attn_fwd_block(v, n, scale, s_scr, e_scr, l_scr):
    def emit(rows, e, inv_l):
        e_scr[rows, :n] = e.astype(BF16)
        l_scr[rows, :] = jnp.broadcast_to(inv_l, (ROW_CHUNK, LANE))

    _softmax_chunks(s_scr, n, scale, emit)
    return _dg(e_scr[:, :n], v, 1, 0) * l_scr[...]


def _attn_bwd_block(q, k, o, do, scale, s_scr, dp_scr, p_scr, ds_scr):
    n = k.shape[0]

    def emit(rows, e, inv_l):
        p = e * inv_l
        delta = jnp.sum(do[rows, :] * o[rows, :], axis=-1, keepdims=True)
        p_scr[rows, :n] = p.astype(BF16)
        ds_scr[rows, :n] = (p * (dp_scr[rows, :n] - delta) * scale).astype(BF16)

    _softmax_chunks(s_scr, n, scale, emit)
    ds = ds_scr[:, :n]
    return _dg(ds, k, 1, 0), _dg(ds, q, 0, 0), _dg(p_scr[:, :n], do, 0, 0)


def _call_with_exchange(body, xchg, *, name, grid, in_specs, out_specs, out_shape, operands, scratch_shapes=()):
    if xchg is None:
        res = _pcall(body, name=name, grid=grid, in_specs=in_specs, out_specs=out_specs, out_shape=out_shape,
                     scratch_shapes=list(scratch_shapes), compiler_params=_cparams(len(grid)))(*operands)
        return list(res), []
    n_in, n_out, n_sc, n = len(in_specs), len(out_specs), len(scratch_shapes), xchg.n

    def wrapped(*refs):
        ins, x_refs = refs[:n_in], refs[n_in:n_in + n]
        outs, xo_refs = refs[n_in + n:n_in + n + n_out], refs[n_in + n + n_out:n_in + 2 * n + n_out]
        scratch, sems = refs[n_in + 2 * n + n_out:n_in + 2 * n + n_out + n_sc], refs[n_in + 2 * n + n_out + n_sc:]
        ids = [pl.program_id(i) for i in range(len(grid))]

        @pl.when(functools.reduce(jnp.logical_and, [i == 0 for i in ids]))
        def _():
            xchg.start(x_refs, xo_refs, sems)

        body(*ins, *outs, *scratch)

        @pl.when(functools.reduce(jnp.logical_and, [i == g - 1 for i, g in zip(ids, grid)]))
        def _():
            xchg.wait(x_refs, xo_refs, sems)

    res = _pcall(wrapped, name=name, grid=grid, in_specs=list(in_specs) + xchg.specs,
                 out_specs=list(out_specs) + xchg.specs, out_shape=list(out_shape) + xchg.out_shape,
                 scratch_shapes=list(scratch_shapes) + xchg.scratch, compiler_params=_cparams(len(grid)),
                 )(*operands, *xchg.bufs)
    return list(res[:n_out]), list(res[n_out:])


def _head_half(i, shape):
    lane = lax.broadcasted_iota(jnp.int32, shape, len(shape) - 1)
    return (lane < LANE // 2) if i == 0 else (lane >= LANE // 2)


def _mla_attn(q, k, v, tc, ctx_q, name, xchg=None):
    assert MLA_HPS == 2 and MLA_V == LANE // 2
    bsz, t_all, _ = q.shape
    n_t = t_all // TB
    grid = (bsz, MLA_HEADS // MLA_HPS, n_t)
    q_spec = pl.BlockSpec((None, TB, MLA_HPS * LANE), lambda b, h, t: (b, t, h))
    k_spec = pl.BlockSpec((None, t_all, MLA_HPS * LANE), lambda b, h, t: (b, 0, h))
    v_spec = pl.BlockSpec((None, t_all, LANE), lambda b, h, t: (b, 0, h))
    o_spec = pl.BlockSpec((None, TB, LANE), lambda b, h, t: (b, t, h))
    heads = [slice(i * LANE, (i + 1) * LANE) for i in range(MLA_HPS)]
    scale = MLA_QK ** -0.5
    f32_scr, bf16_scr = pltpu.VMEM((TB, t_all), F32), pltpu.VMEM((TB, t_all), BF16)
    o_shape = jax.ShapeDtypeStruct(v.shape, F32)

    def fwd_body(q_ref, k_ref, v_ref, o_ref, *scr):
        t = pl.program_id(2)

        def run(keys):
            n = keys.stop
            for i, hs in enumerate(heads):
                scr[3 * i][:, :n] = _dg(q_ref[:, hs], k_ref[keys, hs], 1, 1)
            both = [_attn_fwd_block(v_ref[keys, :], n, scale, *scr[3 * i:3 * i + 3]) for i in range(MLA_HPS)]
            o_ref[...] = jnp.where(_head_half(0, both[0].shape), both[0], both[1])

        @pl.when(t == 0)
        def _():
            if ctx_q:
                run(slice(0, tc))
            else:
                o_ref[...] = jnp.zeros_like(o_ref)

        @pl.when(t > 0)
        def _():
            run(slice(0, t_all))

    (o,), gathered = _call_with_exchange(
        fwd_body, xchg, name=name, grid=grid, in_specs=[q_spec, k_spec, v_spec], out_specs=[o_spec],
        out_shape=[o_shape], operands=(q, k, v),
        scratch_shapes=[f32_scr, bf16_scr, pltpu.VMEM((TB, LANE), F32)] * MLA_HPS)

    def bwd(do, xchg=None):
        def bwd_body(q_ref, k_ref, v_ref, o_ref, do_ref, dq_ref, dk_ref, dv_ref, *scr):
            t = pl.program_id(2)

            def run(keys, first):
                n = keys.stop
                dos = [jnp.where(_head_half(i, do_ref.shape), do_ref[...], 0.0) for i in range(MLA_HPS)]
                for i, hs in enumerate(heads):
                    scr[4 * i][:, :n] = _dg(q_ref[:, hs], k_ref[keys, hs], 1, 1)
                    scr[4 * i + 1][:, :n] = _dg(dos[i], v_ref[keys, :], 1, 1)
                dvs = []
                for i, hs in enumerate(heads):
                    dq, dk, dv = _attn_bwd_block(q_ref[:, hs], k_ref[keys, hs], o_ref[...], dos[i], scale,
                                                 *scr[4 * i:4 * i + 4])
                    dq_ref[:, hs] = dq
                    dvs.append(dv)
                    if first:
                        dk_ref[keys, hs] = dk
                    else:
                        dk_ref[keys, hs] += dk
                if first:
                    dv_ref[keys, :] = dvs[0] + dvs[1]
                else:
                    dv_ref[keys, :] += dvs[0] + dvs[1]

            @pl.when(t == 0)
            def _():
                dk_ref[...] = jnp.zeros_like(dk_ref)
                dv_ref[...] = jnp.zeros_like(dv_ref)
                if ctx_q:
                    run(slice(0, tc), True)
                else:
                    dq_ref[...] = jnp.zeros_like(dq_ref)

            @pl.when(t > 0)
            def _():
                run(slice(0, t_all), False)

        return _call_with_exchange(
            bwd_body, xchg, name=name + "_bwd", grid=grid, in_specs=[q_spec, k_spec, v_spec, o_spec, o_spec],
            out_specs=[q_spec, k_spec, v_spec],
            out_shape=[jax.ShapeDtypeStruct(q.shape, F32), jax.ShapeDtypeStruct(q.shape, F32), o_shape],
            operands=(q, k, v, o, do), scratch_shapes=[f32_scr, f32_scr, bf16_scr, bf16_scr] * MLA_HPS)

    return o, gathered, bwd


def _swa_block(q, keys, vals, sink, mask):
    qs = jnp.concatenate(list(_split(q, SWA_GROUP, 1)), axis=0)
    sk = jnp.sum(sink, axis=-1, keepdims=True) * (1.0 / LANE)
    s = _nt(qs, keys) * (SWA_HEAD_DIM ** -0.5)
    if mask is not None:
        s = jnp.where(mask, s, NEG_INF)
    o = _split(_nn(_softmax_rows(s, sk), vals + _roll(vals, LANE // 2, 1)), SWA_GROUP, 0)
    low = _head_half(0, o[0].shape)
    return jnp.concatenate([jnp.where(low, o[0], o[1]), jnp.where(low, o[2], o[3])], axis=1)


def _swa_ctx_block(q, kc, vc, sink):
    return _swa_block(q, kc, vc, sink, None)


def _swa_win_block(q, kc, kw, vc, vw, sink, mask):
    return _swa_block(q, jnp.concatenate([kc, kw], axis=0), jnp.concatenate([vc, vw], axis=0), sink, mask)


def _swa_attn(q, k, p_all, sink_b, tc, ctx_q, name, xchg=None):
    bsz, t_all, _ = q.shape
    n_q = t_all // QB_SWA
    n_cq = tc // QB_SWA
    lat = t_all - tc
    span = QB_SWA + 2 * WINDOW
    gw = SWA_GROUP * LANE
    grid = (bsz, SWA_KV_HEADS, n_q)
    q_spec = pl.BlockSpec((None, QB_SWA, gw), lambda b, g, i: (b, i, g))
    k_spec = pl.BlockSpec((None, t_all, LANE), lambda b, g, i: (b, 0, g))
    v_spec = pl.BlockSpec((None, t_all, LANE), lambda b, g, i: (b, 0, PC_SV // LANE + g))
    s_spec = pl.BlockSpec((None, SWA_GROUP * QB_SWA, LANE), lambda b, g, i: (g, 0, 0))

    def window(i):
        q0 = (i - n_cq) * QB_SWA
        w0 = jnp.clip(q0 - WINDOW, 0, lat - span)
        w0 = pl.multiple_of(w0, WINDOW)
        shape = (SWA_GROUP * QB_SWA, tc + span)
        qi = q0 + lax.broadcasted_iota(jnp.int32, shape, 0) % QB_SWA
        col = lax.broadcasted_iota(jnp.int32, shape, 1)
        kj = w0 + col - tc
        mask = (col < tc) | ((kj >= qi - WINDOW) & (kj <= qi + WINDOW))
        return w0, mask

    def fwd_body(q_ref, k_ref, v_ref, s_ref, o_ref):
        i = pl.program_id(2)

        @pl.when(i < n_cq)
        def _():
            if ctx_q:
                o_ref[...] = _swa_ctx_block(q_ref[...].astype(F32), k_ref[0:tc, :], v_ref[0:tc, :], s_ref[...])
            else:
                o_ref[...] = jnp.zeros_like(o_ref)

        @pl.when(i >= n_cq)
        def _():
            w0, mask = window(i)
            o_ref[...] = _swa_win_block(q_ref[...].astype(F32), k_ref[0:tc, :], k_ref[pl.ds(tc + w0, span), :],
                                        v_ref[0:tc, :], v_ref[pl.ds(tc + w0, span), :], s_ref[...], mask)

    o_spec = pl.BlockSpec((None, QB_SWA, SWA_GROUP * SWA_HEAD_DIM), lambda b, g, i: (b, i, g))
    (o,), gathered = _call_with_exchange(
        fwd_body, xchg, name=name, grid=grid, in_specs=[q_spec, k_spec, v_spec, s_spec], out_specs=[o_spec],
        out_shape=[jax.ShapeDtypeStruct((bsz, t_all, SWA_HEADS * SWA_HEAD_DIM), F32)],
        operands=(q, k, p_all, sink_b))

    def bwd(do, xchg=None):
        def bwd_body(q_ref, k_ref, v_ref, s_ref, do_ref, dq_ref, dk_ref, dv_ref, ds_ref):
            i = pl.program_id(2)

            @pl.when(i == 0)
            def _():
                dk_ref[...] = jnp.zeros_like(dk_ref)
                dv_ref[...] = jnp.zeros_like(dv_ref)
                ds_ref[...] = jnp.zeros_like(ds_ref)

            @pl.when(i < n_cq)
            def _():
                if ctx_q:
                    _, vjp = jax.vjp(_swa_ctx_block, q_ref[...].astype(F32), k_ref[0:tc, :].astype(F32),
                                     v_ref[0:tc, :], s_ref[...])
                    dq, dk, dv, ds = vjp(do_ref[...])
                    dq_ref[...] = dq
                    dk_ref[0:tc, :] += dk
                    dv_ref[0:tc, :] += dv
                    ds_ref[...] += ds
                else:
                    dq_ref[...] = jnp.zeros_like(dq_ref)

            @pl.when(i >= n_cq)
            def _():
                w0, mask = window(i)
                win = pl.ds(tc + w0, span)
                _, vjp = jax.vjp(functools.partial(_swa_win_block, mask=mask), q_ref[...].astype(F32),
                                 k_ref[0:tc, :].astype(F32), k_ref[win, :].astype(F32),
                                 v_ref[0:tc, :], v_ref[win, :], s_ref[...])
                dq, dkc, dkw, dvc, dvw, ds = vjp(do_ref[...])
                dq_ref[...] = dq
                dk_ref[0:tc, :] += dkc
                dk_ref[win, :] += dkw
                dv_ref[0:tc, :] += dvc
                dv_ref[win, :] += dvw
                ds_ref[...] += ds

        kv_out = pl.BlockSpec((None, t_all, LANE), lambda b, g, i: (b, 0, g))
        ds_spec = pl.BlockSpec((None, None, SWA_GROUP * QB_SWA, LANE), lambda b, g, i: (b, g, 0, 0))
        kv_shape = jax.ShapeDtypeStruct((bsz, t_all, SWA_KV_HEADS * LANE), F32)
        return _call_with_exchange(
            bwd_body, xchg, name=name + "_bwd", grid=grid, in_specs=[q_spec, k_spec, v_spec, s_spec, o_spec],
            out_specs=[q_spec, kv_out, kv_out, ds_spec],
            out_shape=[jax.ShapeDtypeStruct(q.shape, F32), kv_shape, kv_shape,
                       jax.ShapeDtypeStruct((bsz,) + sink_b.shape, F32)],
            operands=(q, k, p_all, sink_b, do))

    return o, gathered, bwd


def _scan_pair(chains, scratch):
    t_all, c = chains[0][0].shape
    n_tiles = t_all // SUBLANE
    row8 = lax.broadcasted_iota(jnp.int32, (t_all, c), 0) % SUBLANE
    refs = [scratch[0:3], scratch[3:6]]
    for (a, u, reverse), (a_s, u_s, _) in zip(chains, refs):
        for d in (1, 2, 4):
            sh = d if not reverse else t_all - d
            ar, ur = pltpu.roll(a, sh, 0), pltpu.roll(u, sh, 0)
            m = (row8 >= d) if not reverse else (row8 < SUBLANE - d)
            u = jnp.where(m, a * ur + u, u)
            a = jnp.where(m, a * ar, a)
        a_s[...] = a
        u_s[...] = u

    def step(j, carries):
        out = []
        for (_, _, reverse), (a_s, u_s, c_s), carry in zip(chains, refs, carries):
            tile = j if not reverse else n_tiles - 1 - j
            base = pl.multiple_of(tile * SUBLANE, SUBLANE)
            c_s[pl.ds(base, SUBLANE), :] = jnp.broadcast_to(carry, (SUBLANE, c))
            last = base + (0 if reverse else SUBLANE - 1)
            out.append(a_s[pl.ds(last, 1), :] * carry + u_s[pl.ds(last, 1), :])
        return tuple(out)

    lax.fori_loop(0, n_tiles, step, (jnp.zeros((1, c), F32),) * 2, unroll=4)
    return [a_s[...] * c_s[...] + u_s[...] for a_s, u_s, c_s in refs]


def _shift_rows(x, reverse_src):
    t_all = x.shape[0]
    row = lax.broadcasted_iota(jnp.int32, x.shape, 0)
    if reverse_src:
        return jnp.where(row == t_all - 1, 0.0, pltpu.roll(x, t_all - 1, 0))
    return jnp.where(row == 0, 0.0, pltpu.roll(x, 1, 0))


def _lru_scan(a0, u0, a1, u1, name):
    bsz, t_all, w = a0.shape
    grid = (bsz, w // LANE)
    spec = pl.BlockSpec((None, t_all, LANE), lambda b, c: (b, 0, c))
    scratch = [pltpu.VMEM((t_all, LANE), F32)] * 6
    shape = jax.ShapeDtypeStruct(a0.shape, F32)

    def fwd_body(a0_ref, u0_ref, a1_ref, u1_ref, h0_ref, h1_ref, *scr):
        h0_ref[...], h1_ref[...] = _scan_pair([(a0_ref[...], u0_ref[...], False), (a1_ref[...], u1_ref[...], True)],
                                              scr)

    h0, h1 = _pcall(fwd_body, name=name, grid=grid, in_specs=[spec] * 4, out_specs=[spec] * 2,
                    out_shape=[shape] * 2, scratch_shapes=scratch, compiler_params=_cparams(2))(a0, u0, a1, u1)

    def bwd(dh0, dh1):
        def bwd_body(a0_ref, h0_ref, g0_ref, a1_ref, h1_ref, g1_ref, da0_ref, du0_ref, da1_ref, du1_ref, *scr):
            g0, g1 = _scan_pair([(_shift_rows(a0_ref[...], True), g0_ref[...], True),
                                 (_shift_rows(a1_ref[...], False), g1_ref[...], False)], scr)
            du0_ref[...] = g0
            da0_ref[...] = g0 * _shift_rows(h0_ref[...], False)
            du1_ref[...] = g1
            da1_ref[...] = g1 * _shift_rows(h1_ref[...], True)

        return _pcall(bwd_body, name=name + "_bwd", grid=grid, in_specs=[spec] * 6, out_specs=[spec] * 4,
                      out_shape=[shape] * 4, scratch_shapes=scratch,
                      compiler_params=_cparams(2))(a0, h0, dh0, a1, h1, dh1)

    return h0, h1, bwd


def _f_mod(x, g, shift, scale):
    return (_rms(x, g, D_MODEL) * (1.0 + scale) + shift,)


def _f_mla_q(cq, ga, w, gh, cos, sa, sb):
    n = _rms(cq, ga, MLA_Q_RANK)
    outs = []
    for wh in _split(w, MLA_HEADS, 1):
        outs.append(_rope(_rms(_nn(n, wh), gh, MLA_QK), cos, sa, sb, MLA_ROPE // 4))
    return (jnp.concatenate(outs, axis=1),)


def _f_mla_kv(ckv, krp, ga, wk, wv, gh, cos, sa, sb):
    n = _rms(ckv, ga, MLA_KV_RANK)
    outs = []
    for wh in _split(wk, MLA_HEADS, 1):
        outs.append(_rope(_rms(_nn(n, wh) + krp, gh, MLA_QK), cos, sa, sb, MLA_ROPE // 4))
    return jnp.concatenate(outs, axis=1), _nn(n, wv)


def _f_conv(x, w0, w1, w2, w3, bias, tc):
    t_all = x.shape[0]
    row = lax.broadcasted_iota(jnp.int32, x.shape, 0)
    lo = jnp.where(row < tc, 0, tc)
    hi = jnp.where(row < tc, tc, t_all)
    y = bias + jnp.zeros_like(x)
    for kk, wk in enumerate((w0, w1, w2, w3)):
        src = row + (kk - 2)
        xs = x if kk == 2 else _roll(x, 2 - kk, 0)
        y = y + wk * jnp.where((src >= lo) & (src < hi), xs, 0.0)
    return (y,)


def _f_gates(xc, w16, b00, b01, b10, b11, sp0, sp1):
    ws = _unstack(w16)
    n_cb = LRU_WIDTH // LANE
    xcs = _split(xc, n_cb, 1)
    bias = [_split(b, n_cb, 1) for b in (b00, b01, b10, b11)]
    sps = [_split(s, n_cb, 1) for s in (sp0, sp1)]
    res = [[], [], [], []]
    for c in range(n_cb):
        for z in range(2):
            r = _sig(_nn(xcs[c], ws[c * 4 + 2 * z]) + bias[2 * z][c])
            i = _sig(_nn(xcs[c], ws[c * 4 + 2 * z + 1]) + bias[2 * z + 1][c])
            la = -LRU_C * r * sps[z][c]
            res[2 * z].append(jnp.exp(la))
            res[2 * z + 1].append(jnp.sqrt(-jnp.tanh(la) * (jnp.exp(2.0 * la) + 1.0)) * (i * xcs[c]))
    return tuple(jnp.concatenate(r, axis=1) for r in res)


def _f_swa_qk(sq, sk, gq, gk, cos, sa, sb):
    qs = [_rope(_rms(x, gq, SWA_HEAD_DIM), cos, sa, sb, SWA_HEAD_DIM // 4) for x in _split(sq, SWA_HEADS, 1)]
    ks = [_rope(_rms(x, gk, SWA_HEAD_DIM), cos, sa, sb, SWA_HEAD_DIM // 4) for x in _split(sk, SWA_KV_HEADS, 1)]
    return jnp.concatenate(qs, axis=1), jnp.concatenate(ks, axis=1)


def _f_qkv(cq, ckv, krp, sq, sk, q_a_g, wuq, mla_q_g, kv_a_g, wk, wv, mla_k_g, swa_q_g, swa_k_g,
           m_cos, m_sa, m_sb, s_cos, s_sa, s_sb):
    return (*_f_mla_q(cq, q_a_g, wuq, mla_q_g, m_cos, m_sa, m_sb),
            *_f_mla_kv(ckv, krp, kv_a_g, wk, wv, mla_k_g, m_cos, m_sa, m_sb),
            *_f_swa_qk(sq, sk, swa_q_g, swa_k_g, s_cos, s_sa, s_sb))


def _f_merge(oa, h0, h1, lg, oc, ga, gb, gc):
    ob = (h0 + h1) * _gelu(lg)
    return (jnp.concatenate([_rms(oa, ga, GROUP_WIDTH), _rms(ob, gb, GROUP_WIDTH), _rms(oc, gc, GROUP_WIDTH)],
                            axis=1),)


def _f_resid_mod(x, y, gate, g, shift, scale):
    x1 = x + gate * y
    return x1, _rms(x1, g, D_MODEL) * (1.0 + scale) + shift


def _f_resid(x, y, gate):
    return (x + gate * y,)


def _hosted(hooks, key, arg=None):
    make, done = hooks.get(key, (None, None))
    xchg = make(arg) if make is not None else None
    return xchg, (done if xchg is not None else lambda outs: None)


def _layer(li, x, mods, w, s, tabs, tc, ctx_q, hooks):
    bsz, t_all, _ = x.shape
    n_t = t_all // TB
    grid = (bsz, n_t)
    rows = lambda b, t: (b, t, 0)

    def row(arr, width=None, idx=0, gdtype=F32, gshape=None):
        width = width or arr.shape[-1]
        return _A(arr, (None, TB, width), lambda b, t: (b, t, idx), "row", gdtype=gdtype, gshape=gshape,
                  gimap=rows if gshape is not None else None)

    def out(width, dtype, imap=rows):
        return ((bsz, t_all, width), dtype, (None, TB, width), imap)

    def modarg(arr):
        return _A(arr, (None, None, 1, D_MODEL), lambda b, t: (b, jnp.minimum(t, 1), 0, 0), "acc",
                  first=lambda b, t: t <= 1)

    def tab(arr):
        return _A(arr, (TB, LANE), lambda b, t: (t, 0), "const")

    def pcol(p_all, col, width):
        return row(p_all, width, col // width, gdtype=BF16, gshape=(bsz, t_all, width))

    nm = lambda base: "%s_l%d" % (base, li)
    sh1, sc1, g1, sh2, sc2, g2 = mods
    m_all = bsz * t_all

    (h,), b_mod1 = _rowop(nm("mod1"), _f_mod, grid, [row(x), _par(s["norm1_g"]), modarg(sh1), modarg(sc1)],
                          [out(D_MODEL, BF16)])
    p_all = _mm(h.reshape(m_all, D_MODEL), w["win"], "nn", F32, nm("mm_in")).reshape(bsz, t_all, P_WIDTH)

    (q_a, k_a, v_a, q_c, k_c), b_qkv = _rowop(
        nm("qkv"), _f_qkv, grid,
        [pcol(p_all, PC_CQ, 256), pcol(p_all, PC_CKV, 128), pcol(p_all, PC_KR, 128), pcol(p_all, PC_SQ, 1024),
         pcol(p_all, PC_SK, 256)]
        + [_par(a) for a in (s["q_a_g"], w["wuq"], s["mla_q_g"], s["kv_a_g"], w["wk"], w["wv"], s["mla_k_g"],
                             s["swa_q_g"], s["swa_k_g"])]
        + [tab(a) for a in tabs["mla"] + tabs["swa"]],
        [out(MLA_HEADS * LANE, BF16), out(MLA_HEADS * LANE, BF16), out(MLA_HEADS * MLA_V, BF16),
         out(SWA_HEADS * LANE, BF16), out(SWA_KV_HEADS * LANE, BF16)])

    xchg, done = _hosted(hooks, "mla_fwd")
    o_a, got, b_attn_a = _mla_attn(q_a, k_a, v_a, tc, ctx_q, nm("mla_attn"), xchg)
    done(got)

    n_cb = LRU_WIDTH // LANE
    conv_grid = (n_cb, bsz)
    cpar = lambda arr: _A(arr, (1, LANE), lambda c, b: (0, c), "acc", first=lambda c, b: b == 0)
    conv_args = [_A(p_all, (None, t_all, LANE), lambda c, b: (b, 0, PC_LX // LANE + c), "row", gdtype=BF16,
                    gshape=(bsz, t_all, LRU_WIDTH), gimap=lambda c, b: (b, 0, c))]
    conv_args += [cpar(a) for a in s["conv_w"]] + [cpar(s["conv_b"])]
    conv_out = [((bsz, t_all, LRU_WIDTH), F32, (None, t_all, LANE), lambda c, b: (b, 0, c))]
    (xc,), b_conv = _rowop(nm("lru_conv"), functools.partial(_f_conv, tc=tc), conv_grid, conv_args, conv_out)
    rot = lambda b, t: (b, (t + n_t - 1) % n_t, 0)
    (a0, u0, a1, u1), b_gates = _rowop(
        nm("lru_gates"), _f_gates, grid,
        [row(xc), _par(s["wbd"])] + [_par(a) for a in s["gate_b"]] + [_par(a) for a in s["sp"]],
        [out(LRU_WIDTH, F32), out(LRU_WIDTH, F32), out(LRU_WIDTH, F32, rot), out(LRU_WIDTH, F32, rot)])
    h0, h1, b_scan = _lru_scan(a0, u0, a1, u1, nm("lru_scan"))
    h1_arg = _A(h1, (None, TB, LRU_WIDTH), rot, "row")

    xchg, done = _hosted(hooks, "swa_fwd")
    o_c, got, b_attn_c = _swa_attn(q_c, k_c, p_all, s["sink_b"], tc, ctx_q, nm("swa_attn"), xchg)
    done(got)

    (y_in,), b_merge = _rowop(nm("merge"), _f_merge, grid,
                              [row(o_a), row(h0), h1_arg, pcol(p_all, PC_LG, 512), row(o_c), _par(s["g_a"]),
                               _par(s["g_b"]), _par(s["g_c"])],
                              [out(MIX_P, BF16)])
    y = _mm(y_in.reshape(m_all, MIX_P), w["wout"], "nn", F32, nm("mm_out")).reshape(bsz, t_all, D_MODEL)
    (x1, hm), b_rm = _rowop(nm("resid_mod"), _f_resid_mod, grid,
                            [row(x), row(y, gdtype=BF16), modarg(g1), _par(s["norm2_g"]), modarg(sh2), modarg(sc2)],
                            [out(D_MODEL, F32), out(D_MODEL, BF16)])
    pre, act = _mm(hm.reshape(m_all, D_MODEL), w["ff1"], "nn", BF16, nm("mm_ff1"), epi="sqrelu")
    y2 = _mm(act, w["ff2"], "nn", F32, nm("mm_ff2")).reshape(bsz, t_all, D_MODEL)
    (x2,), b_res = _rowop(nm("resid"), _f_resid, grid,
                          [_A(x1, (None, TB, D_MODEL), rows, "fwd"), row(y2, gdtype=BF16), modarg(g2)],
                          [out(D_MODEL, F32)])

    def bwd(dx2, hooks):
        dw, ds = {}, {}
        dy2, dg2 = b_res(dx2)
        dy2 = dy2.reshape(m_all, D_MODEL)
        dpre = _mm(dy2, w["ff2"], "nt", BF16, nm("mm_ff2_dx"), epi="dsqrelu", aux=pre)
        dw["ff2"] = _mm(act, dy2, "tn", BF16, nm("mm_ff2_dw"))
        dhm = _mm(dpre, w["ff1"], "nt", F32, nm("mm_ff1_dx")).reshape(bsz, t_all, D_MODEL)
        dw["ff1"] = _mm(hm.reshape(m_all, D_MODEL), dpre, "tn", BF16, nm("mm_ff1_dw"), out_split=N_DEV)
        dxa, dy, dg1, ds["norm2_g"], dsh2, dsc2 = b_rm(dx2, dhm)
        dy = dy.reshape(m_all, D_MODEL)
        dy_in = _mm(dy, w["wout"], "nt", F32, nm("mm_out_dx")).reshape(bsz, t_all, MIX_P)
        dw["wout"] = _mm(y_in.reshape(m_all, MIX_P), dy, "tn", BF16, nm("mm_out_dw"))
        do_a, dh0, dh1, dlg, do_c, ds["g_a"], ds["g_b"], ds["g_c"] = b_merge(dy_in)

        (dq_c, dk_c, dsv, dsink), _ = b_attn_c(do_c)
        ds["sink_b"] = jnp.sum(dsink, axis=0)

        da0, du0, da1, du1 = b_scan(dh0, dh1)
        gates_g = b_gates(da0, du0, da1, du1)
        dxc, ds["wbd"] = gates_g[0], gates_g[1]
        ds["gate_b"], ds["sp"] = list(gates_g[2:6]), list(gates_g[6:8])
        conv_g = b_conv(dxc)
        dlx, ds["conv_w"], ds["conv_b"] = conv_g[0], list(conv_g[1:5]), conv_g[5]

        xchg, done = _hosted(hooks, "mla_bwd", (dw, ds))
        (dq_a, dk_a, dv_a), got = b_attn_a(do_a, xchg)
        done(got)
        (dcq, dckv, dkr, dsq, dsk, ds["q_a_g"], dw["wuq"], ds["mla_q_g"], ds["kv_a_g"], dw["wk"], dw["wv"],
         ds["mla_k_g"], ds["swa_q_g"], ds["swa_k_g"]) = b_qkv(dq_a, dk_a, dv_a, dq_c, dk_c)

        dp = jnp.concatenate([dsq, dlx, dlg, dcq, dsk, dsv.astype(BF16), dckv, dkr], axis=-1)
        dp = dp.reshape(m_all, P_WIDTH)
        dh = _mm(dp, w["win"], "nt", F32, nm("mm_in_dx")).reshape(bsz, t_all, D_MODEL)
        dw["win"] = _mm(h.reshape(m_all, D_MODEL), dp, "tn", BF16, nm("mm_in_dw"))
        dx, ds["norm1_g"], dsh1, dsc1 = b_mod1(dh, add_to_first=dxa)
        return dx, [dsh1, dsc1, dg1, dsh2, dsc2, dg2], dw, ds

    return x2, bwd


def _loss_and_grad(x2, target, tc):
    bsz, t_all, d = x2.shape
    n_t = t_all // TB
    n_c = tc // TB

    def body(x_ref, t_ref, l_ref, dx_ref):
        b, t = pl.program_id(0), pl.program_id(1)

        @pl.when((b == 0) & (t == 0))
        def _():
            l_ref[...] = jnp.zeros_like(l_ref)

        @pl.when(t < n_c)
        def _():
            dx_ref[...] = jnp.zeros_like(dx_ref)

        @pl.when(t >= n_c)
        def _():
            e = x_ref[...] - t_ref[...]
            dx_ref[...] = e * (1.0 / d)
            l_ref[...] += jnp.sum(e * e) * (0.5 / d)

    loss, dx = _pcall(
        body, name="loss", grid=(bsz, n_t),
        in_specs=[pl.BlockSpec((None, TB, d), lambda b, t: (b, t, 0)),
                  pl.BlockSpec((None, TB, d), lambda b, t: (b, jnp.maximum(t - n_c, 0), 0))],
        out_specs=[pl.BlockSpec((SUBLANE, LANE), lambda b, t: (0, 0)),
                   pl.BlockSpec((None, TB, d), lambda b, t: (b, t, 0))],
        out_shape=[jax.ShapeDtypeStruct((SUBLANE, LANE), F32), jax.ShapeDtypeStruct(x2.shape, F32)],
        compiler_params=_cparams(2))(x2, target)
    return loss[0, 0], dx


def _rope_tables(lat, tc, dim, lane0):
    quarter = dim // 4
    pos = np.arange(lat)
    grid_pos = np.stack([pos // GRID_W, pos % GRID_W], axis=-1).astype(np.float32)
    lane = np.arange(LANE)
    p = np.clip(lane - lane0, 0, dim - 1)
    active = (lane >= lane0) & (lane < lane0 + dim)
    axis, half, qi = p // (dim // 2), (p % (dim // 2)) // quarter, p % quarter
    inv = (np.float32(ROPE_THETA) ** (-qi.astype(np.float32) / np.float32(quarter))).astype(np.float32)
    ang = (np.where(axis[None, :] == 0, grid_pos[:, 0:1], grid_pos[:, 1:2]) * inv[None, :]).astype(np.float32)
    cos = np.where(active, np.cos(ang), 1.0).astype(np.float32)
    sin = np.where(active, np.sin(ang), 0.0).astype(np.float32)
    sa = np.where(half == 0, -sin, 0.0).astype(np.float32)
    sb = np.where(half == 1, sin, 0.0).astype(np.float32)
    ctx1, ctx0 = np.ones((tc, LANE), np.float32), np.zeros((tc, LANE), np.float32)
    return tuple(jnp.asarray(np.concatenate([c, t], 0)) for c, t in ((ctx1, cos), (ctx0, sa), (ctx0, sb)))


_BIG = {"w_in": ((D_MODEL, IN_WIDTH // N_DEV), 1, ("win",)),
        "w_uq": ((MLA_Q_RANK, MLA_HEADS * MLA_QK // N_DEV), 1, ("wuq",)),
        "w_ukv": ((MLA_KV_RANK, MLA_HEADS * (MLA_NOPE + MLA_V) // N_DEV), 1, ("wk", "wv")),
        "w_out": ((3 * GROUP_WIDTH // N_DEV, D_MODEL), 0, ("wout",)),
        "w_ff1": ((D_MODEL, D_FF // N_DEV), 1, ("ff1",)),
        "w_ff2": ((D_FF // N_DEV, D_MODEL), 0, ("ff2",))}
_EARLY = ("w_in", "w_uq", "w_ukv")
_LATE = ("w_out", "w_ff1", "w_ff2")


def _pad_heads(wm, n_heads, dim):
    out = jnp.pad(wm.reshape(wm.shape[0], n_heads, dim), ((0, 0), (0, 0), (0, LANE - dim)))
    return out.reshape(wm.shape[0], n_heads * LANE)


def _prep_weight(name, piece):
    shp, ax, _ = _BIG[name]
    full = jnp.moveaxis(piece, 0, ax).reshape(shp[:ax] + (N_DEV * shp[ax],) + shp[ax + 1:])
    if name == "w_in":
        cq, ckv, kr, lx, lg, sq, sk, sv = _split_cols(full)
        return {"win": jnp.concatenate(
            [_pad_heads(sq, SWA_HEADS, SWA_HEAD_DIM), lx, lg, cq, _pad_heads(sk, SWA_KV_HEADS, SWA_HEAD_DIM),
             _pad_heads(sv, SWA_KV_HEADS, SWA_HEAD_DIM), ckv, jnp.pad(kr, ((0, 0), (MLA_NOPE, LANE - MLA_QK)))], axis=1)}
    if name == "w_uq":
        return {"wuq": _pad_heads(full, MLA_HEADS, MLA_QK)}
    if name == "w_ukv":
        ukv = full.reshape(MLA_KV_RANK, MLA_HEADS, MLA_NOPE + MLA_V)
        return {"wk": _pad_heads(ukv[:, :, :MLA_NOPE].reshape(MLA_KV_RANK, -1), MLA_HEADS, MLA_NOPE),
                "wv": ukv[:, :, MLA_NOPE:].reshape(MLA_KV_RANK, -1)}
    return {_BIG[name][2][0]: full}


def _split_cols(wm):
    parts, start = [], 0
    for size in IN_SIZES:
        parts.append(wm[:, start:start + size])
        start += size
    return parts


def _prep_gates(gate_w):
    gw = gate_w.reshape(2, 2, 4, 2, 64, 64)
    wbd = jnp.einsum("zgknCm,nN->knCzgNm", gw, jnp.eye(2, dtype=F32)).reshape(4, LANE, 4, LANE)
    return wbd.transpose(0, 2, 1, 3).reshape(16, LANE, LANE)


def _prep_small(raw):
    r1 = lambda a: a.reshape(1, -1)
    gg = raw["group_g"]
    sink = raw["swa_sink"].reshape(SWA_KV_HEADS, SWA_GROUP, 1, 1)
    return {
        "norm1_g": r1(raw["norm1_g"]), "norm2_g": r1(raw["norm2_g"]),
        "q_a_g": r1(raw["q_a_g"]), "kv_a_g": r1(raw["kv_a_g"]),
        "mla_q_g": jnp.pad(r1(raw["mla_q_g"]), ((0, 0), (0, LANE - MLA_QK))),
        "mla_k_g": jnp.pad(r1(raw["mla_k_g"]), ((0, 0), (0, LANE - MLA_QK))),
        "swa_q_g": jnp.pad(r1(raw["swa_q_g"]), ((0, 0), (0, LANE - SWA_HEAD_DIM))),
        "swa_k_g": jnp.pad(r1(raw["swa_k_g"]), ((0, 0), (0, LANE - SWA_HEAD_DIM))),
        "conv_w": [r1(raw["conv_w"][kk]) for kk in range(4)], "conv_b": r1(raw["conv_b"]),
        "gate_b": [r1(raw["lru_gate_b"][z, g]) for z in range(2) for g in range(2)],
        "sp": [r1(jax.nn.softplus(-raw["lru_lambda"][z])) for z in range(2)],
        "sink_b": jnp.broadcast_to(sink, (SWA_KV_HEADS, SWA_GROUP, QB_SWA, LANE)).reshape(
            SWA_KV_HEADS, SWA_GROUP * QB_SWA, LANE),
        "g_a": r1(gg[:GROUP_WIDTH]), "g_b": r1(gg[GROUP_WIDTH:2 * GROUP_WIDTH]), "g_c": r1(gg[2 * GROUP_WIDTH:])}


def _mesh_pos():
    return lax.axis_index("x"), lax.axis_index("y"), lax.axis_index("c")


def _peer(pos, k):
    return tuple(1 - p if (k >> s) & 1 else p for p, s in zip(pos, (2, 1, 0)))


def _dev_index(pos):
    return 4 * pos[0] + 2 * pos[1] + pos[2]


class _Exchange:
    def __init__(self, bufs, gather):
        self.bufs = list(bufs)
        self.n = len(self.bufs)
        self.gather = [gather] * self.n if isinstance(gather, bool) else list(gather)
        self.specs = [pl.BlockSpec(memory_space=pl.ANY)] * self.n
        self.out_shape = [jax.ShapeDtypeStruct((N_DEV,) + tuple(b.shape if g else b.shape[1:]), b.dtype)
                          for b, g in zip(self.bufs, self.gather)]
        self.scratch = [pltpu.SemaphoreType.DMA(((N_DEV - 1) * self.n,)),
                        pltpu.SemaphoreType.DMA(((N_DEV - 1) * self.n,)), pltpu.SemaphoreType.DMA((self.n,))]

    def _copies(self, x_refs, o_refs, sems, with_recvs):
        send_sems, recv_sems, local_sems = sems
        pos = _mesh_pos()
        me = _dev_index(pos)
        locals_, sends, recvs = [], [], []
        for j in range(self.n):
            src_mine = x_refs[j] if self.gather[j] else x_refs[j].at[me]
            locals_.append(pltpu.make_async_copy(src_mine, o_refs[j].at[me], local_sems.at[j]))
        for k in range(1, N_DEV):
            peer = _peer(pos, k)
            pidx = _dev_index(peer)
            for j in range(self.n):
                src = x_refs[j] if self.gather[j] else x_refs[j].at[pidx]
                sem = (k - 1) * self.n + j
                sends.append(pltpu.make_async_remote_copy(
                    src_ref=src, dst_ref=o_refs[j].at[me], send_sem=send_sems.at[sem], recv_sem=recv_sems.at[sem],
                    device_id=peer, device_id_type=pl.DeviceIdType.MESH))
                if with_recvs:
                    recvs.append(pltpu.make_async_remote_copy(
                        src_ref=src, dst_ref=o_refs[j].at[pidx], send_sem=send_sems.at[sem],
                        recv_sem=recv_sems.at[sem], device_id=peer, device_id_type=pl.DeviceIdType.MESH))
        return locals_, sends, recvs

    def start(self, x_refs, o_refs, sems):
        locals_, sends, _ = self._copies(x_refs, o_refs, sems, False)
        for cp in locals_ + sends:
            cp.start()

    def wait(self, x_refs, o_refs, sems):
        locals_, sends, recvs = self._copies(x_refs, o_refs, sems, True)
        for cp in recvs:
            cp.wait_recv()
        for cp in sends:
            cp.wait_send()
        for cp in locals_:
            cp.wait()


def _exchange(bufs, gather, name):
    xchg = _Exchange(bufs, gather)
    n = xchg.n

    def body(*refs):
        xchg.start(refs[:n], refs[n:2 * n], refs[2 * n:])
        xchg.wait(refs[:n], refs[n:2 * n], refs[2 * n:])

    return _pcall(body, name=name, out_shape=xchg.out_shape, in_specs=xchg.specs, out_specs=xchg.specs,
                  scratch_shapes=xchg.scratch)(*xchg.bufs)


def _pack(arrs, dtype):
    flat = jnp.concatenate([a.reshape(-1).astype(dtype) for a in arrs])
    rows = -(-flat.size // PACK_W)
    rows = -(-rows // 16) * 16
    return jnp.pad(flat, (0, rows * PACK_W - flat.size)).reshape(rows, PACK_W)


def _unpack(buf, shapes, lead=()):
    flat = buf.reshape(lead + (-1,))
    out, off = [], 0
    for shp in shapes:
        n = math.prod(shp)
        out.append(flat[..., off:off + n].reshape(lead + tuple(shp)))
        off += n
    return out


def _sum_sources(buf, name):
    _, r, c = buf.shape
    tr = _rows_tile(r)

    def body(x_ref, o_ref):
        acc = x_ref[0]
        for d in range(1, N_DEV):
            acc = acc + x_ref[d]
        o_ref[...] = acc

    return _pcall(body, name=name, grid=(r // tr,),
                  in_specs=[pl.BlockSpec((N_DEV, tr, c), lambda i: (0, i, 0))],
                  out_specs=pl.BlockSpec((tr, c), lambda i: (i, 0)),
                  out_shape=jax.ShapeDtypeStruct((r, c), F32), compiler_params=_cparams(1))(buf)


def _rows_tile(r):
    best = r
    for t in range(SUBLANE, ELEMWISE_ROWS_MAX + 1, SUBLANE):
        if r % t == 0:
            best = t
    return best


def _adamw(grads, wgt, m, v, name):
    n_lay = len(grads)
    n_src, r, c = grads[0].shape
    tr = _rows_tile(r)
    n_blk = r // tr
    bc1 = 1.0 - ADAM_B1 ** ADAM_STEP
    bc2 = 1.0 - ADAM_B2 ** ADAM_STEP

    def body(*refs):
        g_refs, (w_ref, m_ref, v_ref, go_ref, d_ref, mo_ref, vo_ref) = refs[:n_lay], refs[n_lay:]
        for li, g_ref in enumerate(g_refs):
            @pl.when(pl.program_id(0) == li)
            def _():
                g = g_ref[0].astype(F32)
                for d in range(1, n_src):
                    g = g + g_ref[d].astype(F32)
                m_new = ADAM_B1 * m_ref[...] + (1.0 - ADAM_B1) * g
                v_new = ADAM_B2 * v_ref[...] + (1.0 - ADAM_B2) * (g * g)
                go_ref[...] = g
                mo_ref[...] = m_new
                vo_ref[...] = v_new
                d_ref[...] = -ADAM_LR * ((m_new / bc1) / (jnp.sqrt(v_new / bc2) + ADAM_EPS) + ADAM_WD * w_ref[...])

    g_specs = [pl.BlockSpec((n_src, tr, c),
                            lambda l, i, li=li: (0, jnp.where(l == li, i, jnp.where(l > li, n_blk - 1, 0)), 0))
               for li in range(n_lay)]
    spec = pl.BlockSpec((tr, c), lambda l, i: (l * n_blk + i, 0))
    return _pcall(body, name=name, grid=(n_lay, n_blk), in_specs=g_specs + [spec, spec, spec],
                  out_specs=[spec] * 4, out_shape=[jax.ShapeDtypeStruct((n_lay * r, c), F32)] * 4,
                  compiler_params=_cparams(2))(*grads, wgt, m, v)


def _silu(z):
    return z * jax.nn.sigmoid(z)


_WEIGHTS = ("c_ctx", "w_mod", "b_mod", "norm1_g", "w_in", "q_a_g", "w_uq", "kv_a_g", "w_ukv", "mla_q_g", "mla_k_g",
            "conv_w", "conv_b", "lru_gate_w", "lru_gate_b", "lru_lambda", "swa_q_g", "swa_k_g", "swa_sink", "group_g",
            "w_out", "norm2_g", "w_ff1", "w_ff2")
_SHARDED_SMALL = ("conv_w", "lru_gate_b", "lru_lambda")
_REPL_RAW = ("norm1_g", "q_a_g", "kv_a_g", "mla_q_g", "mla_k_g", "conv_b", "swa_q_g", "swa_k_g",
             "swa_sink", "group_g", "norm2_g")
MOD_ROWS = 32


def kernel(x, c, ctx, c_ctx, w_mod, b_mod, norm1_g, w_in, q_a_g, w_uq, kv_a_g, w_ukv, mla_q_g, mla_k_g, conv_w, conv_b, lru_gate_w, lru_gate_b, lru_lambda, swa_q_g, swa_k_g, swa_sink, group_g, w_out, norm2_g, w_ff1, w_ff2, loss_target, m_c_ctx, m_w_mod, m_b_mod, m_norm1_g, m_w_in, m_q_a_g, m_w_uq, m_kv_a_g, m_w_ukv, m_mla_q_g, m_mla_k_g, m_conv_w, m_conv_b, m_lru_gate_w, m_lru_gate_b, m_lru_lambda, m_swa_q_g, m_swa_k_g, m_swa_sink, m_group_g, m_w_out, m_norm2_g, m_w_ff1, m_w_ff2, v_c_ctx, v_w_mod, v_b_mod, v_norm1_g, v_w_in, v_q_a_g, v_w_uq, v_kv_a_g, v_w_ukv, v_mla_q_g, v_mla_k_g, v_conv_w, v_conv_b, v_lru_gate_w, v_lru_gate_b, v_lru_lambda, v_swa_q_g, v_swa_k_g, v_swa_sink, v_group_g, v_w_out, v_norm2_g, v_w_ff1, v_w_ff2):
    wts = dict(c_ctx=c_ctx, w_mod=w_mod, b_mod=b_mod, norm1_g=norm1_g, w_in=w_in, q_a_g=q_a_g, w_uq=w_uq,
               kv_a_g=kv_a_g, w_ukv=w_ukv, mla_q_g=mla_q_g, mla_k_g=mla_k_g, conv_w=conv_w, conv_b=conv_b,
               lru_gate_w=lru_gate_w, lru_gate_b=lru_gate_b, lru_lambda=lru_lambda, swa_q_g=swa_q_g, swa_k_g=swa_k_g,
               swa_sink=swa_sink, group_g=group_g, w_out=w_out, norm2_g=norm2_g, w_ff1=w_ff1, w_ff2=w_ff2)
    mom1 = dict(zip(_WEIGHTS, (m_c_ctx, m_w_mod, m_b_mod, m_norm1_g, m_w_in, m_q_a_g, m_w_uq, m_kv_a_g, m_w_ukv,
                               m_mla_q_g, m_mla_k_g, m_conv_w, m_conv_b, m_lru_gate_w, m_lru_gate_b, m_lru_lambda,
                               m_swa_q_g, m_swa_k_g, m_swa_sink, m_group_g, m_w_out, m_norm2_g, m_w_ff1, m_w_ff2)))
    mom2 = dict(zip(_WEIGHTS, (v_c_ctx, v_w_mod, v_b_mod, v_norm1_g, v_w_in, v_q_a_g, v_w_uq, v_kv_a_g, v_w_ukv,
                               v_mla_q_g, v_mla_k_g, v_conv_w, v_conv_b, v_lru_gate_w, v_lru_gate_b, v_lru_lambda,
                               v_swa_q_g, v_swa_k_g, v_swa_sink, v_group_g, v_w_out, v_norm2_g, v_w_ff1, v_w_ff2)))
    bsz = x.shape[0]
    n_ex = bsz * N_DEV
    me = _dev_index(_mesh_pos())
    mod_cols = w_mod.shape[-1]

    small_shapes = [c.shape, conv_w.shape, lru_gate_b.shape, lru_lambda.shape]
    shard = lambda n, li: wts[n][li].astype(BF16)
    g_small, *early_pieces = _exchange([_pack([c, conv_w, lru_gate_b, lru_lambda], F32)] + [shard(n, 0) for n in _EARLY],
                                       True, "ag_first")
    c_all, conv_w_all, gate_b_all, lam_all = _unpack(g_small, small_shapes, lead=(N_DEV,))
    c_all = c_all.reshape(n_ex, D_MODEL)
    cat_last = lambda a: jnp.moveaxis(a, 0, -2).reshape(a.shape[1:-1] + (N_DEV * a.shape[-1],))
    conv_w_full, gate_b_full, lam_full = cat_last(conv_w_all), cat_last(gate_b_all), cat_last(lam_all)

    act = jnp.zeros((MOD_ROWS, D_MODEL), F32).at[:n_ex].set(_silu(c_all)).at[n_ex].set(_silu(c_ctx))
    mod_part = jnp.concatenate([_mm(act, w_mod[li], "nn", F32, "mm_mod_l%d" % li) for li in range(DEPTH)], axis=1)
    (mod_all,) = _exchange([mod_part], True, "ag_mod")
    mods = []
    for li in range(DEPTH):
        full = jnp.moveaxis(mod_all[:, :, li * mod_cols:(li + 1) * mod_cols], 0, 1).reshape(MOD_ROWS, -1) + b_mod[li]
        mine = lax.dynamic_slice_in_dim(full, me * bsz, bsz, axis=0)
        ctx_row = jnp.broadcast_to(full[n_ex], mine.shape)
        both = jnp.stack([ctx_row, mine], axis=1).reshape(bsz, 2, N_MOD, 1, D_MODEL)
        mods.append([both[:, :, j] for j in range(N_MOD)])

    raw = {n: wts[n] for n in _REPL_RAW}
    raw.update(conv_w=conv_w_full, lru_gate_b=gate_b_full, lru_lambda=lam_full)
    small_names = list(_REPL_RAW) + list(_SHARDED_SMALL)
    sp, small_vjp, gates_vjp = [None] * DEPTH, [None] * DEPTH, [None] * DEPTH
    for li in range(DEPTH):
        sp[li], small_vjp[li] = jax.vjp(_prep_small, {n: raw[n][li] for n in small_names})
        sp[li]["wbd"], gates_vjp[li] = jax.vjp(_prep_gates, lru_gate_w[li])

    w, w_vjp, g_recv, small_recv = [{} for _ in range(DEPTH)], {}, {}, {}

    def take(li, names, pieces):
        for n, piece in zip(names, pieces):
            out, w_vjp[n, li] = jax.vjp(functools.partial(_prep_weight, n), piece)
            w[li].update(out)

    def gather_hook(li, names):
        return (lambda _: _Exchange([shard(n, li) for n in names], True), lambda got: take(li, names, got))

    def wgrad(n, li, dwl):
        if n == "w_ff1":
            return dwl["ff1"]
        (g,) = w_vjp[n, li]({k: dwl[k].astype(BF16) for k in _BIG[n][2]})
        return g

    def small_pack(li, ds_l, extra=()):
        (d_raw,) = small_vjp[li]({k: v for k, v in ds_l.items() if k != "wbd"})
        return _pack([d_raw[n] for n in small_names] + list(extra), F32)

    def gates_grad(li, ds_l):
        return gates_vjp[li](ds_l["wbd"])[0].reshape(-1, LANE)

    take(0, _EARLY, early_pieces)
    hooks_fwd = [{"mla_fwd": gather_hook(0, _LATE), "swa_fwd": gather_hook(1, _EARLY + ("w_out",))},
                 {"mla_fwd": gather_hook(1, ("w_ff1", "w_ff2"))}]
    bwd_state = {}

    def scatter_last_layer(grads_so_far):
        dwl, dsl = grads_so_far
        return _Exchange([wgrad(n, 1, dwl) for n in _LATE] + [gates_grad(1, dsl)], [False] * len(_LATE) + [True])

    def scatter_first_layer(grads_so_far):
        dwl, dsl = grads_so_far
        dw1, ds1 = bwd_state["dw1"], bwd_state["ds1"]
        bufs = [wgrad(n, 1, dw1) for n in _EARLY] + [wgrad(n, 0, dwl) for n in _LATE]
        bufs += [small_pack(1, ds1), gates_grad(0, dsl)]
        return _Exchange(bufs, [False] * (len(_EARLY) + len(_LATE)) + [True] * 2)

    def scattered_first_layer(got):
        g_recv.update(zip([(n, 1) for n in _EARLY] + [(n, 0) for n in _LATE], got[:-2]))
        small_recv[1], g_recv["lru_gate_w", 0] = got[-2:]

    def scattered_last_layer(got):
        g_recv.update(zip([(n, 1) for n in _LATE], got[:-1]))
        g_recv["lru_gate_w", 1] = got[-1]

    hooks_bwd = [{"mla_bwd": (scatter_first_layer, scattered_first_layer)},
                 {"mla_bwd": (scatter_last_layer, scattered_last_layer)}]

    tc, lat = ctx.shape[1], x.shape[1]
    tabs = {"mla": _rope_tables(lat, tc, MLA_ROPE, MLA_NOPE), "swa": _rope_tables(lat, tc, SWA_HEAD_DIM, 0)}
    stream = jnp.concatenate([ctx, x], axis=1)
    bwds = []
    for li in range(DEPTH):
        stream, bwd = _layer(li, stream, mods[li], w[li], sp[li], tabs, tc, li < DEPTH - 1, hooks_fwd[li])
        bwds.append(bwd)
    loss_part, dstream = _loss_and_grad(stream, loss_target, tc)
    dmods = [None] * DEPTH
    dstream, dmods[1], bwd_state["dw1"], bwd_state["ds1"] = bwds[1](dstream, hooks_bwd[1])
    dstream, dmods[0], dw0, ds0 = bwds[0](dstream, hooks_bwd[0])
    grad_x = dstream[:, tc:]

    dm_rows = []
    for li in range(DEPTH):
        dm = jnp.concatenate(dmods[li], axis=-1)
        dm_rows.append(jnp.concatenate([dm[:, 1, 0], jnp.sum(dm[:, 0, 0], axis=0, keepdims=True)], axis=0))
    dm_mine = jnp.concatenate(dm_rows, axis=1)
    dm_mine = jnp.pad(dm_mine, ((0, SUBLANE - bsz - 1), (0, 0)))
    (dm_all,) = _exchange([dm_mine], True, "ag_dmod")
    g_wmod, g_bmod, dact_ctx = [], [], jnp.zeros((D_MODEL,), F32)
    for li in range(DEPTH):
        part = dm_all[:, :, li * N_MOD * D_MODEL:(li + 1) * N_MOD * D_MODEL]
        dm32 = jnp.zeros((MOD_ROWS, N_MOD * D_MODEL), F32).at[:n_ex].set(part[:, :bsz].reshape(n_ex, -1))
        dm32 = dm32.at[n_ex].set(jnp.sum(part[:, bsz], axis=0))
        g_bmod.append(jnp.sum(dm32, axis=0))
        cols = lax.dynamic_slice_in_dim(dm32, me * mod_cols, mod_cols, axis=1)
        g_wmod.append(_mm(act, cols, "tn", F32, "mm_mod_dw_l%d" % li))
        dact_ctx = dact_ctx + _mm(cols, w_mod[li], "nt", F32, "mm_mod_dx_l%d" % li)[n_ex]
    sg = jax.nn.sigmoid(c_ctx)
    g_cctx_part = dact_ctx * (sg * (1.0 + c_ctx * (1.0 - sg)))

    last = _exchange([wgrad(n, 0, dw0) for n in _EARLY] + [small_pack(0, ds0, (g_cctx_part, loss_part.reshape(1)))],
                     [False] * len(_EARLY) + [True], "rs_early")
    g_recv.update(zip([(n, 0) for n in _EARLY], last[:-1]))
    small_recv[0] = last[-1]
    layer_shapes = [raw[n].shape[1:] for n in small_names]
    tot = [_unpack(_sum_sources(small_recv[li], "sum_grads_l%d" % li), layer_shapes + [(D_MODEL,), (1,)][:2 * (li == 0)])
           for li in range(DEPTH)]
    grads = {n: jnp.stack([tot[li][j] for li in range(DEPTH)], axis=0) for j, n in enumerate(small_names)}
    grads["c_ctx"], loss = tot[0][-2], tot[0][-1][0]
    for n in _SHARDED_SMALL:
        width = wts[n].shape[-1]
        grads[n] = lax.dynamic_slice_in_dim(grads[n], me * width, width, axis=grads[n].ndim - 1)
    grads["b_mod"] = jnp.stack(g_bmod, axis=0)

    delta, new_m, new_v = {}, {}, {}
    per_layer = {n: [g_recv[n, li] for li in range(DEPTH)] for n in list(_BIG) + ["lru_gate_w"]}
    per_layer["w_mod"] = [g[None] for g in g_wmod]
    for n, srcs in per_layer.items():
        two_d = (DEPTH * math.prod(wts[n].shape[1:-1]), wts[n].shape[-1])
        srcs = [s.reshape((s.shape[0], two_d[0] // DEPTH, two_d[1])) for s in srcs]
        res = _adamw(srcs, wts[n].reshape(two_d), mom1[n].reshape(two_d), mom2[n].reshape(two_d), "adamw_" + n)
        grads[n], delta[n], new_m[n], new_v[n] = [r.reshape(wts[n].shape) for r in res]
    rest = [n for n in _WEIGHTS if n not in delta]
    shapes = [wts[n].shape for n in rest]
    res = _adamw([_pack([grads[n] for n in rest], F32)[None]], _pack([wts[n] for n in rest], F32),
                 _pack([mom1[n] for n in rest], F32), _pack([mom2[n] for n in rest], F32), "adamw_small")
    for tgt, buf in zip((delta, new_m, new_v), res[1:]):
        tgt.update(zip(rest, _unpack(buf, shapes)))

    return (loss, grad_x, *[grads[n] for n in _WEIGHTS], *[delta[n] for n in _WEIGHTS],
            *[new_m[n] for n in _WEIGHTS], *[new_v[n] for n in _WEIGHTS])
```

```python
import functools
import math

import jax
import jax.numpy as jnp
import numpy as np
from jax import lax
from jax.experimental import pallas as pl
from jax.experimental.pallas import tpu as pltpu

F32, BF16 = jnp.float32, jnp.bfloat16

N_DEV = 8
DEPTH = 2
D_MODEL = 1024
D_FF = 4096
N_MOD = 6
GRID_W = 64
WINDOW = 128
ROPE_THETA = 10000.0
EPS = 1e-6
NEG_INF = -1e30
LRU_C = 8.0
LRU_WIDTH = 512
MLA_HEADS, MLA_NOPE, MLA_ROPE, MLA_V = 8, 64, 32, 64
MLA_QK = MLA_NOPE + MLA_ROPE
MLA_Q_RANK, MLA_KV_RANK = 256, 128
SWA_HEADS, SWA_KV_HEADS, SWA_GROUP, SWA_HEAD_DIM = 8, 2, 4, 64
GROUP_WIDTH = 512
IN_SIZES = (256, 128, 32, 512, 512, 512, 128, 128)
IN_WIDTH = sum(IN_SIZES)
ADAM_LR, ADAM_B1, ADAM_B2, ADAM_EPS, ADAM_WD, ADAM_STEP = 0.001, 0.9, 0.999, 1e-08, 0.01, 10

LANE = 128
SUBLANE = 8
TB = 256
QB_SWA = 256
PACK_W = 1024
MM_K_MAX = 4608
MM_ROWS, MM_COLS_MAX = 512, 1024
ELEMWISE_ROWS_MAX = 256
MLA_HPS = 2
VMEM_LIMIT = 56 * 1024 * 1024
P_WIDTH = 3072
PC_SQ, PC_LX, PC_LG, PC_CQ, PC_SK, PC_SV, PC_CKV, PC_KR = 0, 1024, 1536, 2048, 2304, 2560, 2816, 2944
MIX_P = 1536


def _pcall(body, **kw):
    return pl.pallas_call(body, **kw)


def _cparams(n_grid):
    return pltpu.CompilerParams(dimension_semantics=("arbitrary",) * n_grid, vmem_limit_bytes=VMEM_LIMIT)


def _dg(a, b, ca, cb):
    return lax.dot_general(a.astype(BF16), b.astype(BF16), (((ca,), (cb,)), ((), ())),
                           preferred_element_type=F32)


@jax.custom_vjp
def _nn(a, b):
    return _dg(a, b, 1, 0)


@jax.custom_vjp
def _nt(a, b):
    return _dg(a, b, 1, 1)


@jax.custom_vjp
def _tn(a, b):
    return _dg(a, b, 0, 0)


_nn.defvjp(lambda a, b: (_nn(a, b), (a, b)), lambda r, ct: (_nt(ct, r[1]), _tn(r[0], ct)))
_nt.defvjp(lambda a, b: (_nt(a, b), (a, b)), lambda r, ct: (_nn(ct, r[1]), _tn(ct, r[0])))
_tn.defvjp(lambda a, b: (_tn(a, b), (a, b)), lambda r, ct: (_nt(r[1], ct), _nn(r[0], ct)))


@functools.partial(jax.custom_vjp, nondiff_argnums=(1, 2))
def _roll(x, shift, axis):
    return pltpu.roll(x, shift % x.shape[axis], axis)


_roll.defvjp(lambda x, shift, axis: (_roll(x, shift, axis), None),
             lambda shift, axis, _, ct: (_roll(ct, -shift, axis),))


@functools.partial(jax.custom_vjp, nondiff_argnums=(1, 2))
def _split(x, n, axis):
    w = x.shape[axis] // n
    return tuple(lax.slice_in_dim(x, i * w, (i + 1) * w, axis=axis) for i in range(n))


_split.defvjp(lambda x, n, axis: (_split(x, n, axis), None),
              lambda n, axis, _, cts: (jnp.concatenate(cts, axis=axis),))


@jax.custom_vjp
def _unstack(x):
    return tuple(x[i] for i in range(x.shape[0]))


_unstack.defvjp(lambda x: (_unstack(x), None), lambda _, cts: (jnp.stack(cts, axis=0),))


def _sig(x):
    return 0.5 * (jnp.tanh(0.5 * x) + 1.0)


def _gelu(x):
    return 0.5 * x * (1.0 + jnp.tanh(math.sqrt(2.0 / math.pi) * (x + 0.044715 * (x * x * x))))


def _rms(x, g, n):
    ms = jnp.sum(x * x, axis=-1, keepdims=True) * (1.0 / n)
    return x * lax.rsqrt(ms + EPS) * g


def _rope(y, cos, sa, sb, quarter):
    return y * cos + _roll(y, -quarter, 1) * sa + _roll(y, quarter, 1) * sb


def _softmax_rows(s, extra=None):
    m = jnp.max(s, axis=-1, keepdims=True)
    if extra is not None:
        m = jnp.maximum(m, extra)
    m = lax.stop_gradient(m)
    e = jnp.exp(s - m)
    den = jnp.sum(e, axis=-1, keepdims=True)
    if extra is not None:
        den = den + jnp.exp(extra - m)
    return e / den


class _A:
    def __init__(self, arr, block, imap, kind="row", first=None, gdtype=F32, gshape=None, gimap=None):
        self.arr, self.block, self.imap, self.kind, self.first = arr, block, imap, kind, first
        self.gdtype, self.gshape, self.gimap = gdtype, gshape, gimap


def _all_zero(*ids):
    return functools.reduce(jnp.logical_and, [i == 0 for i in ids])


def _par(arr):
    nd = arr.ndim
    return _A(arr, arr.shape, lambda *ids: (0,) * nd, "acc", first=_all_zero)


def _op_fwd(name, fn, grid, args, outs):
    n_in = len(args)

    def body(*refs):
        vals = [r[...].astype(F32) for r in refs[:n_in]]
        for r, v in zip(refs[n_in:], fn(*vals)):
            r[...] = v.astype(r.dtype)

    return _pcall(
        body, name=name, grid=grid,
        in_specs=[pl.BlockSpec(a.block, a.imap) for a in args],
        out_specs=[pl.BlockSpec(o[2], o[3]) for o in outs],
        out_shape=[jax.ShapeDtypeStruct(o[0], o[1]) for o in outs],
        compiler_params=_cparams(len(grid)),
    )(*[a.arr for a in args])


def _op_bwd(name, fn, grid, args, outs, ct_arrays, add_to_first=None):
    didx = [i for i, a in enumerate(args) if a.kind not in ("const", "fwd")]
    read = [i for i, a in enumerate(args) if a.kind != "fwd"]
    n_in, n_ct = len(read), len(outs)
    n_add = 0 if add_to_first is None else 1

    def body(*refs):
        ids = [pl.program_id(i) for i in range(len(grid))]
        vals = [jnp.zeros([d for d in a.block if d is not None], F32) for a in args]
        for i, r in zip(read, refs[:n_in]):
            vals[i] = r[...].astype(F32)

        def g(*dv):
            full = list(vals)
            for i, v in zip(didx, dv):
                full[i] = v
            return tuple(fn(*full))

        _, vjp = jax.vjp(g, *[vals[i] for i in didx])
        grads = list(vjp(tuple(r[...].astype(F32) for r in refs[n_in:n_in + n_ct])))
        if n_add:
            grads[0] = grads[0] + refs[n_in + n_ct][...]
        for gr, i, r in zip(grads, didx, refs[n_in + n_ct + n_add:]):
            a = args[i]
            if a.kind == "row":
                r[...] = gr.astype(r.dtype)
            else:
                first = a.first(*ids)

                @pl.when(first)
                def _():
                    r[...] = gr

                @pl.when(jnp.logical_not(first))
                def _():
                    r[...] += gr

    g_specs, g_shapes = [], []
    for i in didx:
        a = args[i]
        if a.kind == "row":
            g_specs.append(pl.BlockSpec(a.block, a.gimap or a.imap))
            g_shapes.append(jax.ShapeDtypeStruct(a.gshape or a.arr.shape, a.gdtype))
        else:
            g_specs.append(pl.BlockSpec(a.block, a.imap))
            g_shapes.append(jax.ShapeDtypeStruct(a.arr.shape, F32))
    return _pcall(
        body, name=name, grid=grid,
        in_specs=[pl.BlockSpec(args[i].block, args[i].imap) for i in read] + [pl.BlockSpec(o[2], o[3]) for o in outs]
        + [pl.BlockSpec(args[didx[0]].block, args[didx[0]].imap)] * n_add,
        out_specs=g_specs, out_shape=g_shapes,
        compiler_params=_cparams(len(grid)),
    )(*[args[i].arr for i in read], *ct_arrays, *([add_to_first] if n_add else []))


def _rowop(name, fn, grid, args, outs):
    res = _op_fwd(name, fn, grid, args, outs)
    return res, lambda *cts, add_to_first=None: _op_bwd(name + "_bwd", fn, grid, args, outs, cts, add_to_first)


def _pick(n, cap):
    best = None
    for t in range(LANE, cap + 1, LANE):
        if n % t == 0:
            best = t
    return best or n


def _mm(a, b, mode, out_dtype, name, epi=None, aux=None, out_split=None):
    if mode == "nn":
        (m, k), n = a.shape, b.shape[1]
    elif mode == "nt":
        (m, k), n = a.shape, b.shape[0]
    else:
        (k, m), n = a.shape, b.shape[1]
    assert k <= MM_K_MAX
    tm = MM_ROWS if m % MM_ROWS == 0 else m
    tn = n // out_split if out_split else _pick(n, MM_COLS_MAX)
    a_spec = pl.BlockSpec((k, tm), lambda j, i: (0, i)) if mode == "tn" else pl.BlockSpec((tm, k), lambda j, i: (i, 0))
    b_spec = pl.BlockSpec((tn, k), lambda j, i: (j, 0)) if mode == "nt" else pl.BlockSpec((k, tn), lambda j, i: (0, j))
    dims = {"nn": (1, 0), "nt": (1, 1), "tn": (0, 0)}[mode]
    aux_spec = pl.BlockSpec((tm, tn), lambda j, i: (i, j))
    if out_split:
        o_spec, o_shape = pl.BlockSpec((None, tm, tn), lambda j, i: (j, i, 0)), (out_split, m, tn)
    else:
        o_spec, o_shape = aux_spec, (m, n)
    n_aux = 0 if aux is None else 1
    n_out = 2 if epi == "sqrelu" else 1

    def body(*refs):
        o_refs = refs[2 + n_aux:]
        r = _dg(refs[0][...], refs[1][...], *dims)
        if epi == "sqrelu":
            o_refs[0][...] = r.astype(o_refs[0].dtype)
            rl = jnp.maximum(r, 0.0)
            o_refs[1][...] = (rl * rl).astype(o_refs[1].dtype)
        elif epi == "dsqrelu":
            pre = refs[2][...].astype(F32)
            o_refs[0][...] = (r * (2.0 * jnp.maximum(pre, 0.0))).astype(o_refs[0].dtype)
        else:
            o_refs[0][...] = r.astype(o_refs[0].dtype)

    res = _pcall(
        body, name=name, grid=(n // tn, m // tm),
        in_specs=[a_spec, b_spec] + [aux_spec] * n_aux, out_specs=[o_spec] * n_out,
        out_shape=[jax.ShapeDtypeStruct(o_shape, out_dtype)] * n_out, compiler_params=_cparams(2),
    )(a, b, *([aux] if aux is not None else []))
    return res if n_out == 2 else res[0]


ROW_CHUNK = 16


def _softmax_chunks(s_scr, n_keys, scale, emit):
    for r0 in range(0, s_scr.shape[0], ROW_CHUNK):
        rows = slice(r0, r0 + ROW_CHUNK)
        s = s_scr[rows, :n_keys]
        e = jnp.exp((s - jnp.max(s, axis=-1, keepdims=True)) * scale)
        emit(rows, e, 1.0 / jnp.sum(e, axis=-1, keepdims=True))


def _attn_fwd_block(v, n, scale, s_scr, e_scr, l_scr):
    def emit(rows, e, inv_l):
        e_scr[rows, :n] = e.astype(BF16)
        l_scr[rows, :] = jnp.broadcast_to(inv_l, (ROW_CHUNK, LANE))

    _softmax_chunks(s_scr, n, scale, emit)
    return _dg(e_scr[:, :n], v, 1, 0) * l_scr[...]


def _attn_bwd_block(q, k, o, do, scale, s_scr, dp_scr, p_scr, ds_scr):
    n = k.shape[0]

    def emit(rows, e, inv_l):
        p = e * inv_l
        delta = jnp.sum(do[rows, :] * o[rows, :], axis=-1, keepdims=True)
        p_scr[rows, :n] = p.astype(BF16)
        ds_scr[rows, :n] = (p * (dp_scr[rows, :n] - delta) * scale).astype(BF16)

    _softmax_chunks(s_scr, n, scale, emit)
    ds = ds_scr[:, :n]
    return _dg(ds, k, 1, 0), _dg(ds, q, 0, 0), _dg(p_scr[:, :n], do, 0, 0)


def _call_with_exchange(body, xchg, *, name, grid, in_specs, out_specs, out_shape, operands, scratch_shapes=()):
    if xchg is None:
        res = _pcall(body, name=name, grid=grid, in_specs=in_specs, out_specs=out_specs, out_shape=out_shape,
                     scratch_shapes=list(scratch_shapes), compiler_params=_cparams(len(grid)))(*operands)
        return list(res), []
    n_in, n_out, n_sc, n = len(in_specs), len(out_specs), len(scratch_shapes), xchg.n

    def wrapped(*refs):
        ins, x_refs = refs[:n_in], refs[n_in:n_in + n]
        outs, xo_refs = refs[n_in + n:n_in + n + n_out], refs[n_in + n + n_out:n_in + 2 * n + n_out]
        scratch, sems = refs[n_in + 2 * n + n_out:n_in + 2 * n + n_out + n_sc], refs[n_in + 2 * n + n_out + n_sc:]
        ids = [pl.program_id(i) for i in range(len(grid))]

        @pl.when(functools.reduce(jnp.logical_and, [i == 0 for i in ids]))
        def _():
            xchg.start(x_refs, xo_refs, sems)

        body(*ins, *outs, *scratch)

        @pl.when(functools.reduce(jnp.logical_and, [i == g - 1 for i, g in zip(ids, grid)]))
        def _():
            xchg.wait(x_refs, xo_refs, sems)

    res = _pcall(wrapped, name=name, grid=grid, in_specs=list(in_specs) + xchg.specs,
                 out_specs=list(out_specs) + xchg.specs, out_shape=list(out_shape) + xchg.out_shape,
                 scratch_shapes=list(scratch_shapes) + xchg.scratch, compiler_params=_cparams(len(grid)),
                 )(*operands, *xchg.bufs)
    return list(res[:n_out]), list(res[n_out:])


def _head_half(i, shape):
    lane = lax.broadcasted_iota(jnp.int32, shape, len(shape) - 1)
    return (lane < LANE // 2) if i == 0 else (lane >= LANE // 2)


def _mla_attn(q, k, v, tc, ctx_q, name, xchg=None):
    assert MLA_HPS == 2 and MLA_V == LANE // 2
    bsz, t_all, _ = q.shape
    n_t = t_all // TB
    grid = (bsz, MLA_HEADS // MLA_HPS, n_t)
    q_spec = pl.BlockSpec((None, TB, MLA_HPS * LANE), lambda b, h, t: (b, t, h))
    k_spec = pl.BlockSpec((None, t_all, MLA_HPS * LANE), lambda b, h, t: (b, 0, h))
    v_spec = pl.BlockSpec((None, t_all, LANE), lambda b, h, t: (b, 0, h))
    o_spec = pl.BlockSpec((None, TB, LANE), lambda b, h, t: (b, t, h))
    heads = [slice(i * LANE, (i + 1) * LANE) for i in range(MLA_HPS)]
    scale = MLA_QK ** -0.5
    f32_scr, bf16_scr = pltpu.VMEM((TB, t_all), F32), pltpu.VMEM((TB, t_all), BF16)
    o_shape = jax.ShapeDtypeStruct(v.shape, F32)

    def fwd_body(q_ref, k_ref, v_ref, o_ref, *scr):
        t = pl.program_id(2)

        def run(keys):
            n = keys.stop
            for i, hs in enumerate(heads):
                scr[3 * i][:, :n] = _dg(q_ref[:, hs], k_ref[keys, hs], 1, 1)
            both = [_attn_fwd_block(v_ref[keys, :], n, scale, *scr[3 * i:3 * i + 3]) for i in range(MLA_HPS)]
            o_ref[...] = jnp.where(_head_half(0, both[0].shape), both[0], both[1])

        @pl.when(t == 0)
        def _():
            if ctx_q:
                run(slice(0, tc))
            else:
                o_ref[...] = jnp.zeros_like(o_ref)

        @pl.when(t > 0)
        def _():
            run(slice(0, t_all))

    (o,), gathered = _call_with_exchange(
        fwd_body, xchg, name=name, grid=grid, in_specs=[q_spec, k_spec, v_spec], out_specs=[o_spec],
        out_shape=[o_shape], operands=(q, k, v),
        scratch_shapes=[f32_scr, bf16_scr, pltpu.VMEM((TB, LANE), F32)] * MLA_HPS)

    def bwd(do, xchg=None):
        def bwd_body(q_ref, k_ref, v_ref, o_ref, do_ref, dq_ref, dk_ref, dv_ref, *scr):
            t = pl.program_id(2)

            def run(keys, first):
                n = keys.stop
                dos = [jnp.where(_head_half(i, do_ref.shape), do_ref[...], 0.0) for i in range(MLA_HPS)]
                for i, hs in enumerate(heads):
                    scr[4 * i][:, :n] = _dg(q_ref[:, hs], k_ref[keys, hs], 1, 1)
                    scr[4 * i + 1][:, :n] = _dg(dos[i], v_ref[keys, :], 1, 1)
                dvs = []
                for i, hs in enumerate(heads):
                    dq, dk, dv = _attn_bwd_block(q_ref[:, hs], k_ref[keys, hs], o_ref[...], dos[i], scale,
                                                 *scr[4 * i:4 * i + 4])
                    dq_ref[:, hs] = dq
                    dvs.append(dv)
                    if first:
                        dk_ref[keys, hs] = dk
                    else:
                        dk_ref[keys, hs] += dk
                if first:
                    dv_ref[keys, :] = dvs[0] + dvs[1]
                else:
                    dv_ref[keys, :] += dvs[0] + dvs[1]

            @pl.when(t == 0)
            def _():
                dk_ref[...] = jnp.zeros_like(dk_ref)
                dv_ref[...] = jnp.zeros_like(dv_ref)
                if ctx_q:
                    run(slice(0, tc), True)
                else:
                    dq_ref[...] = jnp.zeros_like(dq_ref)

            @pl.when(t > 0)
            def _():
                run(slice(0, t_all), False)

        return _call_with_exchange(
            bwd_body, xchg, name=name + "_bwd", grid=grid, in_specs=[q_spec, k_spec, v_spec, o_spec, o_spec],
            out_specs=[q_spec, k_spec, v_spec],
            out_shape=[jax.ShapeDtypeStruct(q.shape, F32), jax.ShapeDtypeStruct(q.shape, F32), o_shape],
            operands=(q, k, v, o, do), scratch_shapes=[f32_scr, f32_scr, bf16_scr, bf16_scr] * MLA_HPS)

    return o, gathered, bwd


def _swa_block(q, keys, vals, sink, mask):
    qs = jnp.concatenate(list(_split(q, SWA_GROUP, 1)), axis=0)
    sk = jnp.sum(sink, axis=-1, keepdims=True) * (1.0 / LANE)
    s = _nt(qs, keys) * (SWA_HEAD_DIM ** -0.5)
    if mask is not None:
        s = jnp.where(mask, s, NEG_INF)
    o = _split(_nn(_softmax_rows(s, sk), vals + _roll(vals, LANE // 2, 1)), SWA_GROUP, 0)
    low = _head_half(0, o[0].shape)
    return jnp.concatenate([jnp.where(low, o[0], o[1]), jnp.where(low, o[2], o[3])], axis=1)


def _swa_ctx_block(q, kc, vc, sink):
    return _swa_block(q, kc, vc, sink, None)


def _swa_win_block(q, kc, kw, vc, vw, sink, mask):
    return _swa_block(q, jnp.concatenate([kc, kw], axis=0), jnp.concatenate([vc, vw], axis=0), sink, mask)


def _swa_attn(q, k, p_all, sink_b, tc, ctx_q, name, xchg=None):
    bsz, t_all, _ = q.shape
    n_q = t_all // QB_SWA
    n_cq = tc // QB_SWA
    lat = t_all - tc
    span = QB_SWA + 2 * WINDOW
    gw = SWA_GROUP * LANE
    grid = (bsz, SWA_KV_HEADS, n_q)
    q_spec = pl.BlockSpec((None, QB_SWA, gw), lambda b, g, i: (b, i, g))
    k_spec = pl.BlockSpec((None, t_all, LANE), lambda b, g, i: (b, 0, g))
    v_spec = pl.BlockSpec((None, t_all, LANE), lambda b, g, i: (b, 0, PC_SV // LANE + g))
    s_spec = pl.BlockSpec((None, SWA_GROUP * QB_SWA, LANE), lambda b, g, i: (g, 0, 0))

    def window(i):
        q0 = (i - n_cq) * QB_SWA
        w0 = jnp.clip(q0 - WINDOW, 0, lat - span)
        w0 = pl.multiple_of(w0, WINDOW)
        shape = (SWA_GROUP * QB_SWA, tc + span)
        qi = q0 + lax.broadcasted_iota(jnp.int32, shape, 0) % QB_SWA
        col = lax.broadcasted_iota(jnp.int32, shape, 1)
        kj = w0 + col - tc
        mask = (col < tc) | ((kj >= qi - WINDOW) & (kj <= qi + WINDOW))
        return w0, mask

    def fwd_body(q_ref, k_ref, v_ref, s_ref, o_ref):
        i = pl.program_id(2)

        @pl.when(i < n_cq)
        def _():
            if ctx_q:
                o_ref[...] = _swa_ctx_block(q_ref[...].astype(F32), k_ref[0:tc, :], v_ref[0:tc, :].astype(F32),
                                            s_ref[...])
            else:
                o_ref[...] = jnp.zeros_like(o_ref)

        @pl.when(i >= n_cq)
        def _():
            w0, mask = window(i)
            o_ref[...] = _swa_win_block(q_ref[...].astype(F32), k_ref[0:tc, :], k_ref[pl.ds(tc + w0, span), :],
                                        v_ref[0:tc, :].astype(F32), v_ref[pl.ds(tc + w0, span), :].astype(F32),
                                        s_ref[...], mask)

    o_spec = pl.BlockSpec((None, QB_SWA, SWA_GROUP * SWA_HEAD_DIM), lambda b, g, i: (b, i, g))
    (o,), gathered = _call_with_exchange(
        fwd_body, xchg, name=name, grid=grid, in_specs=[q_spec, k_spec, v_spec, s_spec], out_specs=[o_spec],
        out_shape=[jax.ShapeDtypeStruct((bsz, t_all, SWA_HEADS * SWA_HEAD_DIM), F32)],
        operands=(q, k, p_all, sink_b))

    def bwd(do, xchg=None):
        def bwd_body(q_ref, k_ref, v_ref, s_ref, do_ref, dq_ref, dk_ref, dv_ref, ds_ref):
            i = pl.program_id(2)

            @pl.when(i == 0)
            def _():
                dk_ref[...] = jnp.zeros_like(dk_ref)
                dv_ref[...] = jnp.zeros_like(dv_ref)
                ds_ref[...] = jnp.zeros_like(ds_ref)

            @pl.when(i < n_cq)
            def _():
                if ctx_q:
                    _, vjp = jax.vjp(_swa_ctx_block, q_ref[...].astype(F32), k_ref[0:tc, :].astype(F32),
                                     v_ref[0:tc, :].astype(F32), s_ref[...])
                    dq, dk, dv, ds = vjp(do_ref[...])
                    dq_ref[...] = dq
                    dk_ref[0:tc, :] += dk
                    dv_ref[0:tc, :] += dv
                    ds_ref[...] += ds
                else:
                    dq_ref[...] = jnp.zeros_like(dq_ref)

            @pl.when(i >= n_cq)
            def _():
                w0, mask = window(i)
                win = pl.ds(tc + w0, span)
                _, vjp = jax.vjp(functools.partial(_swa_win_block, mask=mask), q_ref[...].astype(F32),
                                 k_ref[0:tc, :].astype(F32), k_ref[win, :].astype(F32),
                                 v_ref[0:tc, :].astype(F32), v_ref[win, :].astype(F32), s_ref[...])
                dq, dkc, dkw, dvc, dvw, ds = vjp(do_ref[...])
                dq_ref[...] = dq
                dk_ref[0:tc, :] += dkc
                dk_ref[win, :] += dkw
                dv_ref[0:tc, :] += dvc
                dv_ref[win, :] += dvw
                ds_ref[...] += ds

        kv_out = pl.BlockSpec((None, t_all, LANE), lambda b, g, i: (b, 0, g))
        ds_spec = pl.BlockSpec((None, None, SWA_GROUP * QB_SWA, LANE), lambda b, g, i: (b, g, 0, 0))
        kv_shape = jax.ShapeDtypeStruct((bsz, t_all, SWA_KV_HEADS * LANE), F32)
        return _call_with_exchange(
            bwd_body, xchg, name=name + "_bwd", grid=grid, in_specs=[q_spec, k_spec, v_spec, s_spec, o_spec],
            out_specs=[q_spec, kv_out, kv_out, ds_spec],
            out_shape=[jax.ShapeDtypeStruct(q.shape, F32), kv_shape, kv_shape,
                       jax.ShapeDtypeStruct((bsz,) + sink_b.shape, F32)],
            operands=(q, k, p_all, sink_b, do))

    return o, gathered, bwd


def _scan_pair(chains, scratch):
    t_all, c = chains[0][0].shape
    n_tiles = t_all // SUBLANE
    row8 = lax.broadcasted_iota(jnp.int32, (t_all, c), 0) % SUBLANE
    refs = [scratch[0:3], scratch[3:6]]
    for (a, u, reverse), (a_s, u_s, _) in zip(chains, refs):
        for d in (1, 2, 4):
            sh = d if not reverse else t_all - d
            ar, ur = pltpu.roll(a, sh, 0), pltpu.roll(u, sh, 0)
            m = (row8 >= d) if not reverse else (row8 < SUBLANE - d)
            u = jnp.where(m, a * ur + u, u)
            a = jnp.where(m, a * ar, a)
        a_s[...] = a
        u_s[...] = u

    def step(j, carries):
        out = []
        for (_, _, reverse), (a_s, u_s, c_s), carry in zip(chains, refs, carries):
            tile = j if not reverse else n_tiles - 1 - j
            base = pl.multiple_of(tile * SUBLANE, SUBLANE)
            c_s[pl.ds(base, SUBLANE), :] = jnp.broadcast_to(carry, (SUBLANE, c))
            last = base + (0 if reverse else SUBLANE - 1)
            out.append(a_s[pl.ds(last, 1), :] * carry + u_s[pl.ds(last, 1), :])
        return tuple(out)

    lax.fori_loop(0, n_tiles, step, (jnp.zeros((1, c), F32),) * 2, unroll=4)
    return [a_s[...] * c_s[...] + u_s[...] for a_s, u_s, c_s in refs]


def _shift_rows(x, reverse_src):
    t_all = x.shape[0]
    row = lax.broadcasted_iota(jnp.int32, x.shape, 0)
    if reverse_src:
        return jnp.where(row == t_all - 1, 0.0, pltpu.roll(x, t_all - 1, 0))
    return jnp.where(row == 0, 0.0, pltpu.roll(x, 1, 0))


def _lru_scan(a0, u0, a1, u1, name):
    bsz, t_all, w = a0.shape
    grid = (bsz, w // LANE)
    spec = pl.BlockSpec((None, t_all, LANE), lambda b, c: (b, 0, c))
    scratch = [pltpu.VMEM((t_all, LANE), F32)] * 6
    shape = jax.ShapeDtypeStruct(a0.shape, F32)

    def fwd_body(a0_ref, u0_ref, a1_ref, u1_ref, h0_ref, h1_ref, *scr):
        h0_ref[...], h1_ref[...] = _scan_pair([(a0_ref[...], u0_ref[...], False), (a1_ref[...], u1_ref[...], True)],
                                              scr)

    h0, h1 = _pcall(fwd_body, name=name, grid=grid, in_specs=[spec] * 4, out_specs=[spec] * 2,
                    out_shape=[shape] * 2, scratch_shapes=scratch, compiler_params=_cparams(2))(a0, u0, a1, u1)

    def bwd(dh0, dh1):
        def bwd_body(a0_ref, h0_ref, g0_ref, a1_ref, h1_ref, g1_ref, da0_ref, du0_ref, da1_ref, du1_ref, *scr):
            g0, g1 = _scan_pair([(_shift_rows(a0_ref[...], True), g0_ref[...], True),
                                 (_shift_rows(a1_ref[...], False), g1_ref[...], False)], scr)
            du0_ref[...] = g0
            da0_ref[...] = g0 * _shift_rows(h0_ref[...], False)
            du1_ref[...] = g1
            da1_ref[...] = g1 * _shift_rows(h1_ref[...], True)

        return _pcall(bwd_body, name=name + "_bwd", grid=grid, in_specs=[spec] * 6, out_specs=[spec] * 4,
                      out_shape=[shape] * 4, scratch_shapes=scratch,
                      compiler_params=_cparams(2))(a0, h0, dh0, a1, h1, dh1)

    return h0, h1, bwd


def _f_mod(x, g, shift, scale):
    return (_rms(x, g, D_MODEL) * (1.0 + scale) + shift,)


def _f_mla_q(cq, ga, w, gh, cos, sa, sb):
    n = _rms(cq, ga, MLA_Q_RANK)
    outs = []
    for wh in _split(w, MLA_HEADS, 1):
        outs.append(_rope(_rms(_nn(n, wh), gh, MLA_QK), cos, sa, sb, MLA_ROPE // 4))
    return (jnp.concatenate(outs, axis=1),)


def _f_mla_kv(ckv, krp, ga, wk, wv, gh, cos, sa, sb):
    n = _rms(ckv, ga, MLA_KV_RANK)
    outs = []
    for wh in _split(wk, MLA_HEADS, 1):
        outs.append(_rope(_rms(_nn(n, wh) + krp, gh, MLA_QK), cos, sa, sb, MLA_ROPE // 4))
    return jnp.concatenate(outs, axis=1), _nn(n, wv)


def _f_conv(x, w0, w1, w2, w3, bias, tc):
    t_all = x.shape[0]
    row = lax.broadcasted_iota(jnp.int32, x.shape, 0)
    lo = jnp.where(row < tc, 0, tc)
    hi = jnp.where(row < tc, tc, t_all)
    y = bias + jnp.zeros_like(x)
    for kk, wk in enumerate((w0, w1, w2, w3)):
        src = row + (kk - 2)
        xs = x if kk == 2 else _roll(x, 2 - kk, 0)
        y = y + wk * jnp.where((src >= lo) & (src < hi), xs, 0.0)
    return (y,)


def _f_gates(xc, w16, b00, b01, b10, b11, sp0, sp1):
    ws = _unstack(w16)
    n_cb = LRU_WIDTH // LANE
    xcs = _split(xc, n_cb, 1)
    bias = [_split(b, n_cb, 1) for b in (b00, b01, b10, b11)]
    sps = [_split(s, n_cb, 1) for s in (sp0, sp1)]
    res = [[], [], [], []]
    for c in range(n_cb):
        for z in range(2):
            r = _sig(_nn(xcs[c], ws[c * 4 + 2 * z]) + bias[2 * z][c])
            i = _sig(_nn(xcs[c], ws[c * 4 + 2 * z + 1]) + bias[2 * z + 1][c])
            la = -LRU_C * r * sps[z][c]
            res[2 * z].append(jnp.exp(la))
            res[2 * z + 1].append(jnp.sqrt(-jnp.tanh(la) * (jnp.exp(2.0 * la) + 1.0)) * (i * xcs[c]))
    return tuple(jnp.concatenate(r, axis=1) for r in res)


def _f_swa_qk(sq, sk, gq, gk, cos, sa, sb):
    qs = [_rope(_rms(x, gq, SWA_HEAD_DIM), cos, sa, sb, SWA_HEAD_DIM // 4) for x in _split(sq, SWA_HEADS, 1)]
    ks = [_rope(_rms(x, gk, SWA_HEAD_DIM), cos, sa, sb, SWA_HEAD_DIM // 4) for x in _split(sk, SWA_KV_HEADS, 1)]
    return jnp.concatenate(qs, axis=1), jnp.concatenate(ks, axis=1)


def _f_qkv(cq, ckv, krp, sq, sk, q_a_g, wuq, mla_q_g, kv_a_g, wk, wv, mla_k_g, swa_q_g, swa_k_g,
           m_cos, m_sa, m_sb, s_cos, s_sa, s_sb):
    return (*_f_mla_q(cq, q_a_g, wuq, mla_q_g, m_cos, m_sa, m_sb),
            *_f_mla_kv(ckv, krp, kv_a_g, wk, wv, mla_k_g, m_cos, m_sa, m_sb),
            *_f_swa_qk(sq, sk, swa_q_g, swa_k_g, s_cos, s_sa, s_sb))


def _f_merge(oa, h0, h1, lg, oc, ga, gb, gc):
    ob = (h0 + h1) * _gelu(lg)
    return (jnp.concatenate([_rms(oa, ga, GROUP_WIDTH), _rms(ob, gb, GROUP_WIDTH), _rms(oc, gc, GROUP_WIDTH)],
                            axis=1),)


def _f_resid_mod(x, y, gate, g, shift, scale):
    x1 = x + gate * y
    return x1, _rms(x1, g, D_MODEL) * (1.0 + scale) + shift


def _f_resid(x, y, gate):
    return (x + gate * y,)


def _hosted(hooks, key, arg=None):
    make, done = hooks.get(key, (None, None))
    xchg = make(arg) if make is not None else None
    return xchg, (done if xchg is not None else lambda outs: None)


def _layer(li, x, mods, w, s, tabs, tc, ctx_q, hooks, latent_dx_only):
    bsz, t_all, _ = x.shape
    n_t = t_all // TB
    grid = (bsz, n_t)
    rows = lambda b, t: (b, t, 0)

    def row(arr, width=None, idx=0, gdtype=F32, gshape=None):
        width = width or arr.shape[-1]
        return _A(arr, (None, TB, width), lambda b, t: (b, t, idx), "row", gdtype=gdtype, gshape=gshape,
                  gimap=rows if gshape is not None else None)

    def out(width, dtype, imap=rows):
        return ((bsz, t_all, width), dtype, (None, TB, width), imap)

    def modarg(arr):
        return _A(arr, (None, None, 1, D_MODEL), lambda b, t: (b, jnp.minimum(t, 1), 0, 0), "acc",
                  first=lambda b, t: t <= 1)

    def tab(arr):
        return _A(arr, (TB, LANE), lambda b, t: (t, 0), "const")

    def pcol(p_all, col, width):
        return row(p_all, width, col // width, gdtype=BF16, gshape=(bsz, t_all, width))

    nm = lambda base: "%s_l%d" % (base, li)
    sh1, sc1, g1, sh2, sc2, g2 = mods
    m_all = bsz * t_all

    x_arg = row(x)
    if latent_dx_only:
        n_c = tc // TB
        x_arg.gshape, x_arg.gimap = (bsz, t_all - tc, D_MODEL), lambda b, t: (b, jnp.maximum(t - n_c, 0), 0)
    (h,), b_mod1 = _rowop(nm("mod1"), _f_mod, grid, [x_arg, _par(s["norm1_g"]), modarg(sh1), modarg(sc1)],
                          [out(D_MODEL, BF16)])
    p_all = _mm(h.reshape(m_all, D_MODEL), w["win"], "nn", BF16, nm("mm_in")).reshape(bsz, t_all, P_WIDTH)

    (q_a, k_a, v_a, q_c, k_c), b_qkv = _rowop(
        nm("qkv"), _f_qkv, grid,
        [pcol(p_all, PC_CQ, 256), pcol(p_all, PC_CKV, 128), pcol(p_all, PC_KR, 128), pcol(p_all, PC_SQ, 1024),
         pcol(p_all, PC_SK, 256)]
        + [_par(a) for a in (s["q_a_g"], w["wuq"], s["mla_q_g"], s["kv_a_g"], w["wk"], w["wv"], s["mla_k_g"],
                             s["swa_q_g"], s["swa_k_g"])]
        + [tab(a) for a in tabs["mla"] + tabs["swa"]],
        [out(MLA_HEADS * LANE, BF16), out(MLA_HEADS * LANE, BF16), out(MLA_HEADS * MLA_V, BF16),
         out(SWA_HEADS * LANE, BF16), out(SWA_KV_HEADS * LANE, BF16)])

    xchg, done = _hosted(hooks, "mla_fwd")
    o_a, got, b_attn_a = _mla_attn(q_a, k_a, v_a, tc, ctx_q, nm("mla_attn"), xchg)
    done(got)

    n_cb = LRU_WIDTH // LANE
    conv_grid = (n_cb, bsz)
    cpar = lambda arr: _A(arr, (1, LANE), lambda c, b: (0, c), "acc", first=lambda c, b: b == 0)
    conv_args = [_A(p_all, (None, t_all, LANE), lambda c, b: (b, 0, PC_LX // LANE + c), "row", gdtype=BF16,
                    gshape=(bsz, t_all, LRU_WIDTH), gimap=lambda c, b: (b, 0, c))]
    conv_args += [cpar(a) for a in s["conv_w"]] + [cpar(s["conv_b"])]
    conv_out = [((bsz, t_all, LRU_WIDTH), F32, (None, t_all, LANE), lambda c, b: (b, 0, c))]
    (xc,), b_conv = _rowop(nm("lru_conv"), functools.partial(_f_conv, tc=tc), conv_grid, conv_args, conv_out)
    rot = lambda b, t: (b, (t + n_t - 1) % n_t, 0)
    (a0, u0, a1, u1), b_gates = _rowop(
        nm("lru_gates"), _f_gates, grid,
        [row(xc), _par(s["wbd"])] + [_par(a) for a in s["gate_b"]] + [_par(a) for a in s["sp"]],
        [out(LRU_WIDTH, F32), out(LRU_WIDTH, F32), out(LRU_WIDTH, F32, rot), out(LRU_WIDTH, F32, rot)])
    h0, h1, b_scan = _lru_scan(a0, u0, a1, u1, nm("lru_scan"))
    h1_arg = _A(h1, (None, TB, LRU_WIDTH), rot, "row")

    xchg, done = _hosted(hooks, "swa_fwd")
    o_c, got, b_attn_c = _swa_attn(q_c, k_c, p_all, s["sink_b"], tc, ctx_q, nm("swa_attn"), xchg)
    done(got)

    (y_in,), b_merge = _rowop(nm("merge"), _f_merge, grid,
                              [row(o_a), row(h0), h1_arg, pcol(p_all, PC_LG, 512), row(o_c), _par(s["g_a"]),
                               _par(s["g_b"]), _par(s["g_c"])],
                              [out(MIX_P, BF16)])
    y = _mm(y_in.reshape(m_all, MIX_P), w["wout"], "nn", F32, nm("mm_out")).reshape(bsz, t_all, D_MODEL)
    (x1, hm), b_rm = _rowop(nm("resid_mod"), _f_resid_mod, grid,
                            [row(x), row(y, gdtype=BF16), modarg(g1), _par(s["norm2_g"]), modarg(sh2), modarg(sc2)],
                            [out(D_MODEL, F32), out(D_MODEL, BF16)])
    pre, act = _mm(hm.reshape(m_all, D_MODEL), w["ff1"], "nn", BF16, nm("mm_ff1"), epi="sqrelu")
    y2 = _mm(act, w["ff2"], "nn", F32, nm("mm_ff2")).reshape(bsz, t_all, D_MODEL)
    (x2,), b_res = _rowop(nm("resid"), _f_resid, grid,
                          [_A(x1, (None, TB, D_MODEL), rows, "fwd"), row(y2, gdtype=BF16), modarg(g2)],
                          [out(D_MODEL, F32)])

    def bwd(dx2, hooks):
        dw, ds = {}, {}
        dy2, dg2 = b_res(dx2)
        dy2 = dy2.reshape(m_all, D_MODEL)
        dpre = _mm(dy2, w["ff2"], "nt", BF16, nm("mm_ff2_dx"), epi="dsqrelu", aux=pre)
        dw["ff2"] = _mm(act, dy2, "tn", BF16, nm("mm_ff2_dw"))
        dhm = _mm(dpre, w["ff1"], "nt", F32, nm("mm_ff1_dx")).reshape(bsz, t_all, D_MODEL)
        dw["ff1"] = _mm(hm.reshape(m_all, D_MODEL), dpre, "tn", BF16, nm("mm_ff1_dw"), out_split=N_DEV)
        dxa, dy, dg1, ds["norm2_g"], dsh2, dsc2 = b_rm(dx2, dhm)
        dy = dy.reshape(m_all, D_MODEL)
        dy_in = _mm(dy, w["wout"], "nt", F32, nm("mm_out_dx")).reshape(bsz, t_all, MIX_P)
        dw["wout"] = _mm(y_in.reshape(m_all, MIX_P), dy, "tn", BF16, nm("mm_out_dw"))
        do_a, dh0, dh1, dlg, do_c, ds["g_a"], ds["g_b"], ds["g_c"] = b_merge(dy_in)

        (dq_c, dk_c, dsv, dsink), _ = b_attn_c(do_c)
        ds["sink_b"] = jnp.sum(dsink, axis=0)

        da0, du0, da1, du1 = b_scan(dh0, dh1)
        gates_g = b_gates(da0, du0, da1, du1)
        dxc, ds["wbd"] = gates_g[0], gates_g[1]
        ds["gate_b"], ds["sp"] = list(gates_g[2:6]), list(gates_g[6:8])
        conv_g = b_conv(dxc)
        dlx, ds["conv_w"], ds["conv_b"] = conv_g[0], list(conv_g[1:5]), conv_g[5]

        xchg, done = _hosted(hooks, "mla_bwd", (dw, ds))
        (dq_a, dk_a, dv_a), got = b_attn_a(do_a, xchg)
        done(got)
        (dcq, dckv, dkr, dsq, dsk, ds["q_a_g"], dw["wuq"], ds["mla_q_g"], ds["kv_a_g"], dw["wk"], dw["wv"],
         ds["mla_k_g"], ds["swa_q_g"], ds["swa_k_g"]) = b_qkv(dq_a, dk_a, dv_a, dq_c, dk_c)

        dp = jnp.concatenate([dsq, dlx, dlg, dcq, dsk, dsv.astype(BF16), dckv, dkr], axis=-1)
        dp = dp.reshape(m_all, P_WIDTH)
        dh = _mm(dp, w["win"], "nt", F32, nm("mm_in_dx")).reshape(bsz, t_all, D_MODEL)
        dw["win"] = _mm(h.reshape(m_all, D_MODEL), dp, "tn", BF16, nm("mm_in_dw"))
        dx, ds["norm1_g"], dsh1, dsc1 = b_mod1(dh, add_to_first=dxa)
        return dx, [dsh1, dsc1, dg1, dsh2, dsc2, dg2], dw, ds

    return x2, bwd


def _loss_and_grad(x2, target, tc):
    bsz, t_all, d = x2.shape
    n_t = t_all // TB
    n_c = tc // TB

    def body(x_ref, t_ref, l_ref, dx_ref):
        b, t = pl.program_id(0), pl.program_id(1)

        @pl.when((b == 0) & (t == 0))
        def _():
            l_ref[...] = jnp.zeros_like(l_ref)

        @pl.when(t < n_c)
        def _():
            dx_ref[...] = jnp.zeros_like(dx_ref)

        @pl.when(t >= n_c)
        def _():
            e = x_ref[...] - t_ref[...]
            dx_ref[...] = e * (1.0 / d)
            l_ref[...] += jnp.sum(e * e) * (0.5 / d)

    loss, dx = _pcall(
        body, name="loss", grid=(bsz, n_t),
        in_specs=[pl.BlockSpec((None, TB, d), lambda b, t: (b, t, 0)),
                  pl.BlockSpec((None, TB, d), lambda b, t: (b, jnp.maximum(t - n_c, 0), 0))],
        out_specs=[pl.BlockSpec((SUBLANE, LANE), lambda b, t: (0, 0)),
                   pl.BlockSpec((None, TB, d), lambda b, t: (b, t, 0))],
        out_shape=[jax.ShapeDtypeStruct((SUBLANE, LANE), F32), jax.ShapeDtypeStruct(x2.shape, F32)],
        compiler_params=_cparams(2))(x2, target)
    return loss[0, 0], dx


def _rope_tables(lat, tc, dim, lane0):
    quarter = dim // 4
    pos = np.arange(lat)
    grid_pos = np.stack([pos // GRID_W, pos % GRID_W], axis=-1).astype(np.float32)
    lane = np.arange(LANE)
    p = np.clip(lane - lane0, 0, dim - 1)
    active = (lane >= lane0) & (lane < lane0 + dim)
    axis, half, qi = p // (dim // 2), (p % (dim // 2)) // quarter, p % quarter
    inv = (np.float32(ROPE_THETA) ** (-qi.astype(np.float32) / np.float32(quarter))).astype(np.float32)
    ang = (np.where(axis[None, :] == 0, grid_pos[:, 0:1], grid_pos[:, 1:2]) * inv[None, :]).astype(np.float32)
    cos = np.where(active, np.cos(ang), 1.0).astype(np.float32)
    sin = np.where(active, np.sin(ang), 0.0).astype(np.float32)
    sa = np.where(half == 0, -sin, 0.0).astype(np.float32)
    sb = np.where(half == 1, sin, 0.0).astype(np.float32)
    ctx1, ctx0 = np.ones((tc, LANE), np.float32), np.zeros((tc, LANE), np.float32)
    return tuple(jnp.asarray(np.concatenate([c, t], 0)) for c, t in ((ctx1, cos), (ctx0, sa), (ctx0, sb)))


_BIG = {"w_in": ((D_MODEL, IN_WIDTH // N_DEV), 1, ("win",)),
        "w_uq": ((MLA_Q_RANK, MLA_HEADS * MLA_QK // N_DEV), 1, ("wuq",)),
        "w_ukv": ((MLA_KV_RANK, MLA_HEADS * (MLA_NOPE + MLA_V) // N_DEV), 1, ("wk", "wv")),
        "w_out": ((3 * GROUP_WIDTH // N_DEV, D_MODEL), 0, ("wout",)),
        "w_ff1": ((D_MODEL, D_FF // N_DEV), 1, ("ff1",)),
        "w_ff2": ((D_FF // N_DEV, D_MODEL), 0, ("ff2",))}
_EARLY = ("w_in", "w_uq", "w_ukv")
_LATE = ("w_out", "w_ff1", "w_ff2")


def _pad_heads(wm, n_heads, dim):
    out = jnp.pad(wm.reshape(wm.shape[0], n_heads, dim), ((0, 0), (0, 0), (0, LANE - dim)))
    return out.reshape(wm.shape[0], n_heads * LANE)


def _prep_weight(name, piece):
    shp, ax, _ = _BIG[name]
    full = jnp.moveaxis(piece, 0, ax).reshape(shp[:ax] + (N_DEV * shp[ax],) + shp[ax + 1:])
    if name == "w_in":
        cq, ckv, kr, lx, lg, sq, sk, sv = _split_cols(full)
        return {"win": jnp.concatenate(
            [_pad_heads(sq, SWA_HEADS, SWA_HEAD_DIM), lx, lg, cq, _pad_heads(sk, SWA_KV_HEADS, SWA_HEAD_DIM),
             _pad_heads(sv, SWA_KV_HEADS, SWA_HEAD_DIM), ckv, jnp.pad(kr, ((0, 0), (MLA_NOPE, LANE - MLA_QK)))], axis=1)}
    if name == "w_uq":
        return {"wuq": _pad_heads(full, MLA_HEADS, MLA_QK)}
    if name == "w_ukv":
        ukv = full.reshape(MLA_KV_RANK, MLA_HEADS, MLA_NOPE + MLA_V)
        return {"wk": _pad_heads(ukv[:, :, :MLA_NOPE].reshape(MLA_KV_RANK, -1), MLA_HEADS, MLA_NOPE),
                "wv": ukv[:, :, MLA_NOPE:].reshape(MLA_KV_RANK, -1)}
    return {_BIG[name][2][0]: full}


def _split_cols(wm):
    parts, start = [], 0
    for size in IN_SIZES:
        parts.append(wm[:, start:start + size])
        start += size
    return parts


def _prep_gates(gate_w):
    gw = gate_w.reshape(2, 2, 4, 2, 64, 64)
    wbd = jnp.einsum("zgknCm,nN->knCzgNm", gw, jnp.eye(2, dtype=F32)).reshape(4, LANE, 4, LANE)
    return wbd.transpose(0, 2, 1, 3).reshape(16, LANE, LANE)


def _prep_small(raw):
    r1 = lambda a: a.reshape(1, -1)
    gg = raw["group_g"]
    sink = raw["swa_sink"].reshape(SWA_KV_HEADS, SWA_GROUP, 1, 1)
    return {
        "norm1_g": r1(raw["norm1_g"]), "norm2_g": r1(raw["norm2_g"]),
        "q_a_g": r1(raw["q_a_g"]), "kv_a_g": r1(raw["kv_a_g"]),
        "mla_q_g": jnp.pad(r1(raw["mla_q_g"]), ((0, 0), (0, LANE - MLA_QK))),
        "mla_k_g": jnp.pad(r1(raw["mla_k_g"]), ((0, 0), (0, LANE - MLA_QK))),
        "swa_q_g": jnp.pad(r1(raw["swa_q_g"]), ((0, 0), (0, LANE - SWA_HEAD_DIM))),
        "swa_k_g": jnp.pad(r1(raw["swa_k_g"]), ((0, 0), (0, LANE - SWA_HEAD_DIM))),
        "conv_w": [r1(raw["conv_w"][kk]) for kk in range(4)], "conv_b": r1(raw["conv_b"]),
        "gate_b": [r1(raw["lru_gate_b"][z, g]) for z in range(2) for g in range(2)],
        "sp": [r1(jax.nn.softplus(-raw["lru_lambda"][z])) for z in range(2)],
        "sink_b": jnp.broadcast_to(sink, (SWA_KV_HEADS, SWA_GROUP, QB_SWA, LANE)).reshape(
            SWA_KV_HEADS, SWA_GROUP * QB_SWA, LANE),
        "g_a": r1(gg[:GROUP_WIDTH]), "g_b": r1(gg[GROUP_WIDTH:2 * GROUP_WIDTH]), "g_c": r1(gg[2 * GROUP_WIDTH:])}


def _mesh_pos():
    return lax.axis_index("x"), lax.axis_index("y"), lax.axis_index("c")


def _peer(pos, k):
    return tuple(1 - p if (k >> s) & 1 else p for p, s in zip(pos, (2, 1, 0)))


def _dev_index(pos):
    return 4 * pos[0] + 2 * pos[1] + pos[2]


class _Exchange:
    def __init__(self, bufs, gather):
        self.bufs = list(bufs)
        self.n = len(self.bufs)
        self.gather = [gather] * self.n if isinstance(gather, bool) else list(gather)
        self.specs = [pl.BlockSpec(memory_space=pl.ANY)] * self.n
        self.out_shape = [jax.ShapeDtypeStruct((N_DEV,) + tuple(b.shape if g else b.shape[1:]), b.dtype)
                          for b, g in zip(self.bufs, self.gather)]
        self.scratch = [pltpu.SemaphoreType.DMA(((N_DEV - 1) * self.n,)),
                        pltpu.SemaphoreType.DMA(((N_DEV - 1) * self.n,)), pltpu.SemaphoreType.DMA((self.n,))]

    def _copies(self, x_refs, o_refs, sems, with_recvs):
        send_sems, recv_sems, local_sems = sems
        pos = _mesh_pos()
        me = _dev_index(pos)
        locals_, sends, recvs = [], [], []
        for j in range(self.n):
            src_mine = x_refs[j] if self.gather[j] else x_refs[j].at[me]
            locals_.append(pltpu.make_async_copy(src_mine, o_refs[j].at[me], local_sems.at[j]))
        for k in range(1, N_DEV):
            peer = _peer(pos, k)
            pidx = _dev_index(peer)
            for j in range(self.n):
                src = x_refs[j] if self.gather[j] else x_refs[j].at[pidx]
                sem = (k - 1) * self.n + j
                sends.append(pltpu.make_async_remote_copy(
                    src_ref=src, dst_ref=o_refs[j].at[me], send_sem=send_sems.at[sem], recv_sem=recv_sems.at[sem],
                    device_id=peer, device_id_type=pl.DeviceIdType.MESH))
                if with_recvs:
                    recvs.append(pltpu.make_async_remote_copy(
                        src_ref=src, dst_ref=o_refs[j].at[pidx], send_sem=send_sems.at[sem],
                        recv_sem=recv_sems.at[sem], device_id=peer, device_id_type=pl.DeviceIdType.MESH))
        return locals_, sends, recvs

    def start(self, x_refs, o_refs, sems):
        locals_, sends, _ = self._copies(x_refs, o_refs, sems, False)
        for cp in locals_ + sends:
            cp.start()

    def wait(self, x_refs, o_refs, sems):
        locals_, sends, recvs = self._copies(x_refs, o_refs, sems, True)
        for cp in recvs:
            cp.wait_recv()
        for cp in sends:
            cp.wait_send()
        for cp in locals_:
            cp.wait()


def _exchange(bufs, gather, name):
    xchg = _Exchange(bufs, gather)
    n = xchg.n

    def body(*refs):
        xchg.start(refs[:n], refs[n:2 * n], refs[2 * n:])
        xchg.wait(refs[:n], refs[n:2 * n], refs[2 * n:])

    return _pcall(body, name=name, out_shape=xchg.out_shape, in_specs=xchg.specs, out_specs=xchg.specs,
                  scratch_shapes=xchg.scratch)(*xchg.bufs)


def _pack(arrs, dtype):
    flat = jnp.concatenate([a.reshape(-1).astype(dtype) for a in arrs])
    rows = -(-flat.size // PACK_W)
    rows = -(-rows // 16) * 16
    return jnp.pad(flat, (0, rows * PACK_W - flat.size)).reshape(rows, PACK_W)


def _unpack(buf, shapes, lead=()):
    flat = buf.reshape(lead + (-1,))
    out, off = [], 0
    for shp in shapes:
        n = math.prod(shp)
        out.append(flat[..., off:off + n].reshape(lead + tuple(shp)))
        off += n
    return out


def _sum_sources(buf, name):
    _, r, c = buf.shape
    tr = _rows_tile(r)

    def body(x_ref, o_ref):
        acc = x_ref[0]
        for d in range(1, N_DEV):
            acc = acc + x_ref[d]
        o_ref[...] = acc

    return _pcall(body, name=name, grid=(r // tr,),
                  in_specs=[pl.BlockSpec((N_DEV, tr, c), lambda i: (0, i, 0))],
                  out_specs=pl.BlockSpec((tr, c), lambda i: (i, 0)),
                  out_shape=jax.ShapeDtypeStruct((r, c), F32), compiler_params=_cparams(1))(buf)


def _rows_tile(r):
    best = r
    for t in range(SUBLANE, ELEMWISE_ROWS_MAX + 1, SUBLANE):
        if r % t == 0:
            best = t
    return best


def _adamw(grads, wgt, m, v, name):
    n_lay = len(grads)
    n_src, r, c = grads[0].shape
    tr = _rows_tile(r)
    n_blk = r // tr
    bc1 = 1.0 - ADAM_B1 ** ADAM_STEP
    bc2 = 1.0 - ADAM_B2 ** ADAM_STEP

    def body(*refs):
        g_refs, (w_ref, m_ref, v_ref, go_ref, d_ref, mo_ref, vo_ref) = refs[:n_lay], refs[n_lay:]
        for li, g_ref in enumerate(g_refs):
            @pl.when(pl.program_id(0) == li)
            def _():
                g = g_ref[0].astype(F32)
                for d in range(1, n_src):
                    g = g + g_ref[d].astype(F32)
                m_new = ADAM_B1 * m_ref[...] + (1.0 - ADAM_B1) * g
                v_new = ADAM_B2 * v_ref[...] + (1.0 - ADAM_B2) * (g * g)
                go_ref[...] = g
                mo_ref[...] = m_new
                vo_ref[...] = v_new
                d_ref[...] = -ADAM_LR * ((m_new / bc1) / (jnp.sqrt(v_new / bc2) + ADAM_EPS) + ADAM_WD * w_ref[...])

    g_specs = [pl.BlockSpec((n_src, tr, c),
                            lambda l, i, li=li: (0, jnp.where(l == li, i, jnp.where(l > li, n_blk - 1, 0)), 0))
               for li in range(n_lay)]
    spec = pl.BlockSpec((tr, c), lambda l, i: (l * n_blk + i, 0))
    return _pcall(body, name=name, grid=(n_lay, n_blk), in_specs=g_specs + [spec, spec, spec],
                  out_specs=[spec] * 4, out_shape=[jax.ShapeDtypeStruct((n_lay * r, c), F32)] * 4,
                  compiler_params=_cparams(2))(*grads, wgt, m, v)


def _silu(z):
    return z * jax.nn.sigmoid(z)


_WEIGHTS = ("c_ctx", "w_mod", "b_mod", "norm1_g", "w_in", "q_a_g", "w_uq", "kv_a_g", "w_ukv", "mla_q_g", "mla_k_g",
            "conv_w", "conv_b", "lru_gate_w", "lru_gate_b", "lru_lambda", "swa_q_g", "swa_k_g", "swa_sink", "group_g",
            "w_out", "norm2_g", "w_ff1", "w_ff2")
_SHARDED_SMALL = ("conv_w", "lru_gate_b", "lru_lambda")
_REPL_RAW = ("norm1_g", "q_a_g", "kv_a_g", "mla_q_g", "mla_k_g", "conv_b", "swa_q_g", "swa_k_g",
             "swa_sink", "group_g", "norm2_g")
MOD_ROWS = 32


def kernel(x, c, ctx, c_ctx, w_mod, b_mod, norm1_g, w_in, q_a_g, w_uq, kv_a_g, w_ukv, mla_q_g, mla_k_g, conv_w, conv_b, lru_gate_w, lru_gate_b, lru_lambda, swa_q_g, swa_k_g, swa_sink, group_g, w_out, norm2_g, w_ff1, w_ff2, loss_target, m_c_ctx, m_w_mod, m_b_mod, m_norm1_g, m_w_in, m_q_a_g, m_w_uq, m_kv_a_g, m_w_ukv, m_mla_q_g, m_mla_k_g, m_conv_w, m_conv_b, m_lru_gate_w, m_lru_gate_b, m_lru_lambda, m_swa_q_g, m_swa_k_g, m_swa_sink, m_group_g, m_w_out, m_norm2_g, m_w_ff1, m_w_ff2, v_c_ctx, v_w_mod, v_b_mod, v_norm1_g, v_w_in, v_q_a_g, v_w_uq, v_kv_a_g, v_w_ukv, v_mla_q_g, v_mla_k_g, v_conv_w, v_conv_b, v_lru_gate_w, v_lru_gate_b, v_lru_lambda, v_swa_q_g, v_swa_k_g, v_swa_sink, v_group_g, v_w_out, v_norm2_g, v_w_ff1, v_w_ff2):
    wts = dict(c_ctx=c_ctx, w_mod=w_mod, b_mod=b_mod, norm1_g=norm1_g, w_in=w_in, q_a_g=q_a_g, w_uq=w_uq,
               kv_a_g=kv_a_g, w_ukv=w_ukv, mla_q_g=mla_q_g, mla_k_g=mla_k_g, conv_w=conv_w, conv_b=conv_b,
               lru_gate_w=lru_gate_w, lru_gate_b=lru_gate_b, lru_lambda=lru_lambda, swa_q_g=swa_q_g, swa_k_g=swa_k_g,
               swa_sink=swa_sink, group_g=group_g, w_out=w_out, norm2_g=norm2_g, w_ff1=w_ff1, w_ff2=w_ff2)
    mom1 = dict(zip(_WEIGHTS, (m_c_ctx, m_w_mod, m_b_mod, m_norm1_g, m_w_in, m_q_a_g, m_w_uq, m_kv_a_g, m_w_ukv,
                               m_mla_q_g, m_mla_k_g, m_conv_w, m_conv_b, m_lru_gate_w, m_lru_gate_b, m_lru_lambda,
                               m_swa_q_g, m_swa_k_g, m_swa_sink, m_group_g, m_w_out, m_norm2_g, m_w_ff1, m_w_ff2)))
    mom2 = dict(zip(_WEIGHTS, (v_c_ctx, v_w_mod, v_b_mod, v_norm1_g, v_w_in, v_q_a_g, v_w_uq, v_kv_a_g, v_w_ukv,
                               v_mla_q_g, v_mla_k_g, v_conv_w, v_conv_b, v_lru_gate_w, v_lru_gate_b, v_lru_lambda,
                               v_swa_q_g, v_swa_k_g, v_swa_sink, v_group_g, v_w_out, v_norm2_g, v_w_ff1, v_w_ff2)))
    bsz = x.shape[0]
    n_ex = bsz * N_DEV
    me = _dev_index(_mesh_pos())
    mod_cols = w_mod.shape[-1]

    small_shapes = [c.shape, conv_w.shape, lru_gate_b.shape, lru_lambda.shape]
    shard = lambda n, li: wts[n][li].astype(BF16)
    g_small, *early_pieces = _exchange([_pack([c, conv_w, lru_gate_b, lru_lambda], F32)] + [shard(n, 0) for n in _EARLY],
                                       True, "ag_first")
    c_all, conv_w_all, gate_b_all, lam_all = _unpack(g_small, small_shapes, lead=(N_DEV,))
    c_all = c_all.reshape(n_ex, D_MODEL)
    cat_last = lambda a: jnp.moveaxis(a, 0, -2).reshape(a.shape[1:-1] + (N_DEV * a.shape[-1],))
    conv_w_full, gate_b_full, lam_full = cat_last(conv_w_all), cat_last(gate_b_all), cat_last(lam_all)

    act = jnp.zeros((MOD_ROWS, D_MODEL), F32).at[:n_ex].set(_silu(c_all)).at[n_ex].set(_silu(c_ctx))
    mod_part = jnp.concatenate([_mm(act, w_mod[li], "nn", F32, "mm_mod_l%d" % li) for li in range(DEPTH)], axis=1)
    (mod_all,) = _exchange([mod_part], True, "ag_mod")
    mods = []
    for li in range(DEPTH):
        full = jnp.moveaxis(mod_all[:, :, li * mod_cols:(li + 1) * mod_cols], 0, 1).reshape(MOD_ROWS, -1) + b_mod[li]
        mine = lax.dynamic_slice_in_dim(full, me * bsz, bsz, axis=0)
        ctx_row = jnp.broadcast_to(full[n_ex], mine.shape)
        both = jnp.stack([ctx_row, mine], axis=1).reshape(bsz, 2, N_MOD, 1, D_MODEL)
        mods.append([both[:, :, j] for j in range(N_MOD)])

    raw = {n: wts[n] for n in _REPL_RAW}
    raw.update(conv_w=conv_w_full, lru_gate_b=gate_b_full, lru_lambda=lam_full)
    small_names = list(_REPL_RAW) + list(_SHARDED_SMALL)
    sp, small_vjp, gates_vjp = [None] * DEPTH, [None] * DEPTH, [None] * DEPTH
    for li in range(DEPTH):
        sp[li], small_vjp[li] = jax.vjp(_prep_small, {n: raw[n][li] for n in small_names})
        sp[li]["wbd"], gates_vjp[li] = jax.vjp(_prep_gates, lru_gate_w[li])

    w, w_vjp, g_recv, small_recv = [{} for _ in range(DEPTH)], {}, {}, {}

    def take(li, names, pieces):
        for n, piece in zip(names, pieces):
            out, w_vjp[n, li] = jax.vjp(functools.partial(_prep_weight, n), piece)
            w[li].update(out)

    def gather_hook(li, names):
        return (lambda _: _Exchange([shard(n, li) for n in names], True), lambda got: take(li, names, got))

    def wgrad(n, li, dwl):
        if n == "w_ff1":
            return dwl["ff1"]
        (g,) = w_vjp[n, li]({k: dwl[k].astype(BF16) for k in _BIG[n][2]})
        return g

    def small_pack(li, ds_l, extra=()):
        (d_raw,) = small_vjp[li]({k: v for k, v in ds_l.items() if k != "wbd"})
        return _pack([d_raw[n] for n in small_names] + list(extra), F32)

    def gates_grad(li, ds_l):
        return gates_vjp[li](ds_l["wbd"])[0].reshape(-1, LANE)

    take(0, _EARLY, early_pieces)
    hooks_fwd = [{"mla_fwd": gather_hook(0, _LATE), "swa_fwd": gather_hook(1, _EARLY + ("w_out",))},
                 {"mla_fwd": gather_hook(1, ("w_ff1", "w_ff2"))}]
    bwd_state = {}

    def scatter_last_layer(grads_so_far):
        dwl, dsl = grads_so_far
        return _Exchange([wgrad(n, 1, dwl) for n in _LATE] + [gates_grad(1, dsl)], [False] * len(_LATE) + [True])

    def scatter_first_layer(grads_so_far):
        dwl, dsl = grads_so_far
        dw1, ds1 = bwd_state["dw1"], bwd_state["ds1"]
        bufs = [wgrad(n, 1, dw1) for n in _EARLY] + [wgrad(n, 0, dwl) for n in _LATE]
        bufs += [small_pack(1, ds1), gates_grad(0, dsl)]
        return _Exchange(bufs, [False] * (len(_EARLY) + len(_LATE)) + [True] * 2)

    def scattered_first_layer(got):
        g_recv.update(zip([(n, 1) for n in _EARLY] + [(n, 0) for n in _LATE], got[:-2]))
        small_recv[1], g_recv["lru_gate_w", 0] = got[-2:]

    def scattered_last_layer(got):
        g_recv.update(zip([(n, 1) for n in _LATE], got[:-1]))
        g_recv["lru_gate_w", 1] = got[-1]

    hooks_bwd = [{"mla_bwd": (scatter_first_layer, scattered_first_layer)},
                 {"mla_bwd": (scatter_last_layer, scattered_last_layer)}]

    tc, lat = ctx.shape[1], x.shape[1]
    tabs = {"mla": _rope_tables(lat, tc, MLA_ROPE, MLA_NOPE), "swa": _rope_tables(lat, tc, SWA_HEAD_DIM, 0)}
    stream = jnp.concatenate([ctx, x], axis=1)
    bwds = []
    for li in range(DEPTH):
        stream, bwd = _layer(li, stream, mods[li], w[li], sp[li], tabs, tc, li < DEPTH - 1, hooks_fwd[li], li == 0)
        bwds.append(bwd)
    loss_part, dstream = _loss_and_grad(stream, loss_target, tc)
    dmods = [None] * DEPTH
    dstream, dmods[1], bwd_state["dw1"], bwd_state["ds1"] = bwds[1](dstream, hooks_bwd[1])
    grad_x, dmods[0], dw0, ds0 = bwds[0](dstream, hooks_bwd[0])

    dm_rows = []
    for li in range(DEPTH):
        dm = jnp.concatenate(dmods[li], axis=-1)
        dm_rows.append(jnp.concatenate([dm[:, 1, 0], jnp.sum(dm[:, 0, 0], axis=0, keepdims=True)], axis=0))
    dm_mine = jnp.concatenate(dm_rows, axis=1)
    dm_mine = jnp.pad(dm_mine, ((0, SUBLANE - bsz - 1), (0, 0)))
    (dm_all,) = _exchange([dm_mine], True, "ag_dmod")
    g_wmod, g_bmod, dact_ctx = [], [], jnp.zeros((D_MODEL,), F32)
    for li in range(DEPTH):
        part = dm_all[:, :, li * N_MOD * D_MODEL:(li + 1) * N_MOD * D_MODEL]
        dm32 = jnp.zeros((MOD_ROWS, N_MOD * D_MODEL), F32).at[:n_ex].set(part[:, :bsz].reshape(n_ex, -1))
        dm32 = dm32.at[n_ex].set(jnp.sum(part[:, bsz], axis=0))
        g_bmod.append(jnp.sum(dm32, axis=0))
        cols = lax.dynamic_slice_in_dim(dm32, me * mod_cols, mod_cols, axis=1)
        g_wmod.append(_mm(act, cols, "tn", F32, "mm_mod_dw_l%d" % li))
        dact_ctx = dact_ctx + _mm(cols, w_mod[li], "nt", F32, "mm_mod_dx_l%d" % li)[n_ex]
    sg = jax.nn.sigmoid(c_ctx)
    g_cctx_part = dact_ctx * (sg * (1.0 + c_ctx * (1.0 - sg)))

    last = _exchange([wgrad(n, 0, dw0) for n in _EARLY] + [small_pack(0, ds0, (g_cctx_part, loss_part.reshape(1)))],
                     [False] * len(_EARLY) + [True], "rs_early")
    g_recv.update(zip([(n, 0) for n in _EARLY], last[:-1]))
    small_recv[0] = last[-1]
    layer_shapes = [raw[n].shape[1:] for n in small_names]
    tot = [_unpack(_sum_sources(small_recv[li], "sum_grads_l%d" % li), layer_shapes + [(D_MODEL,), (1,)][:2 * (li == 0)])
           for li in range(DEPTH)]
    grads = {n: jnp.stack([tot[li][j] for li in range(DEPTH)], axis=0) for j, n in enumerate(small_names)}
    grads["c_ctx"], loss = tot[0][-2], tot[0][-1][0]
    for n in _SHARDED_SMALL:
        width = wts[n].shape[-1]
        grads[n] = lax.dynamic_slice_in_dim(grads[n], me * width, width, axis=grads[n].ndim - 1)
    grads["b_mod"] = jnp.stack(g_bmod, axis=0)

    delta, new_m, new_v = {}, {}, {}
    per_layer = {n: [g_recv[n, li] for li in range(DEPTH)] for n in list(_BIG) + ["lru_gate_w"]}
    per_layer["w_mod"] = [g[None] for g in g_wmod]
    for n, srcs in per_layer.items():
        two_d = (DEPTH * math.prod(wts[n].shape[1:-1]), wts[n].shape[-1])
        srcs = [s.reshape((s.shape[0], two_d[0] // DEPTH, two_d[1])) for s in srcs]
        res = _adamw(srcs, wts[n].reshape(two_d), mom1[n].reshape(two_d), mom2[n].reshape(two_d), "adamw_" + n)
        grads[n], delta[n], new_m[n], new_v[n] = [r.reshape(wts[n].shape) for r in res]
    rest = [n for n in _WEIGHTS if n not in delta]
    shapes = [wts[n].shape for n in rest]
    res = _adamw([_pack([grads[n] for n in rest], F32)[None]], _pack([wts[n] for n in rest], F32),
                 _pack([mom1[n] for n in rest], F32), _pack([mom2[n] for n in rest], F32), "adamw_small")
    for tgt, buf in zip((delta, new_m, new_v), res[1:]):
        tgt.update(zip(rest, _unpack(buf, shapes)))

    return (loss, grad_x, *[grads[n] for n in _WEIGHTS], *[delta[n] for n in _WEIGHTS],
            *[new_m[n] for n in _WEIGHTS], *[new_v[n] for n in _WEIGHTS])
```

```python
import functools
import math

import jax
import jax.numpy as jnp
import numpy as np
from jax import lax
from jax.experimental import pallas as pl
from jax.experimental.pallas import tpu as pltpu

F32, BF16 = jnp.float32, jnp.bfloat16

N_DEV = 8
DEPTH = 2
D_MODEL = 1024
D_FF = 4096
N_MOD = 6
GRID_W = 64
WINDOW = 128
ROPE_THETA = 10000.0
EPS = 1e-6
NEG_INF = -1e30
LRU_C = 8.0
LRU_WIDTH = 512
MLA_HEADS, MLA_NOPE, MLA_ROPE, MLA_V = 8, 64, 32, 64
MLA_QK = MLA_NOPE + MLA_ROPE
MLA_Q_RANK, MLA_KV_RANK = 256, 128
SWA_HEADS, SWA_KV_HEADS, SWA_GROUP, SWA_HEAD_DIM = 8, 2, 4, 64
GROUP_WIDTH = 512
IN_SIZES = (256, 128, 32, 512, 512, 512, 128, 128)
IN_WIDTH = sum(IN_SIZES)
ADAM_LR, ADAM_B1, ADAM_B2, ADAM_EPS, ADAM_WD, ADAM_STEP = 0.001, 0.9, 0.999, 1e-08, 0.01, 10

LANE = 128
SUBLANE = 8
TB = 256
QB_SWA = 256
PACK_W = 1024
MM_K_MAX = 4608
MM_ROWS, MM_COLS_MAX = 512, 1024
ELEMWISE_ROWS_MAX = 256
MLA_HPS = 2
VMEM_LIMIT = 56 * 1024 * 1024
P_WIDTH = 3072
PC_SQ, PC_LX, PC_LG, PC_CQ, PC_SK, PC_SV, PC_CKV, PC_KR = 0, 1024, 1536, 2048, 2304, 2560, 2816, 2944
MIX_P = 1536


def _pcall(body, **kw):
    return pl.pallas_call(body, **kw)


def _cparams(n_grid):
    return pltpu.CompilerParams(dimension_semantics=("arbitrary",) * n_grid, vmem_limit_bytes=VMEM_LIMIT)


def _dg(a, b, ca, cb):
    return lax.dot_general(a.astype(BF16), b.astype(BF16), (((ca,), (cb,)), ((), ())),
                           preferred_element_type=F32)


@jax.custom_vjp
def _nn(a, b):
    return _dg(a, b, 1, 0)


@jax.custom_vjp
def _nt(a, b):
    return _dg(a, b, 1, 1)


@jax.custom_vjp
def _tn(a, b):
    return _dg(a, b, 0, 0)


_nn.defvjp(lambda a, b: (_nn(a, b), (a, b)), lambda r, ct: (_nt(ct, r[1]), _tn(r[0], ct)))
_nt.defvjp(lambda a, b: (_nt(a, b), (a, b)), lambda r, ct: (_nn(ct, r[1]), _tn(ct, r[0])))
_tn.defvjp(lambda a, b: (_tn(a, b), (a, b)), lambda r, ct: (_nt(r[1], ct), _nn(r[0], ct)))


@functools.partial(jax.custom_vjp, nondiff_argnums=(1, 2))
def _roll(x, shift, axis):
    return pltpu.roll(x, shift % x.shape[axis], axis)


_roll.defvjp(lambda x, shift, axis: (_roll(x, shift, axis), None),
             lambda shift, axis, _, ct: (_roll(ct, -shift, axis),))


@functools.partial(jax.custom_vjp, nondiff_argnums=(1, 2))
def _split(x, n, axis):
    w = x.shape[axis] // n
    return tuple(lax.slice_in_dim(x, i * w, (i + 1) * w, axis=axis) for i in range(n))


_split.defvjp(lambda x, n, axis: (_split(x, n, axis), None),
              lambda n, axis, _, cts: (jnp.concatenate(cts, axis=axis),))


@jax.custom_vjp
def _unstack(x):
    return tuple(x[i] for i in range(x.shape[0]))


_unstack.defvjp(lambda x: (_unstack(x), None), lambda _, cts: (jnp.stack(cts, axis=0),))


def _sig(x):
    return 0.5 * (jnp.tanh(0.5 * x) + 1.0)


def _gelu(x):
    return 0.5 * x * (1.0 + jnp.tanh(math.sqrt(2.0 / math.pi) * (x + 0.044715 * (x * x * x))))


def _rms(x, g, n):
    ms = jnp.sum(x * x, axis=-1, keepdims=True) * (1.0 / n)
    return x * lax.rsqrt(ms + EPS) * g


def _rope(y, cos, sa, sb, quarter):
    return y * cos + _roll(y, -quarter, 1) * sa + _roll(y, quarter, 1) * sb


def _softmax_rows(s, extra=None):
    m = jnp.max(s, axis=-1, keepdims=True)
    if extra is not None:
        m = jnp.maximum(m, extra)
    m = lax.stop_gradient(m)
    e = jnp.exp(s - m)
    den = jnp.sum(e, axis=-1, keepdims=True)
    if extra is not None:
        den = den + jnp.exp(extra - m)
    return e / den


class _A:
    def __init__(self, arr, block, imap, kind="row", first=None, gdtype=F32, gshape=None, gimap=None):
        self.arr, self.block, self.imap, self.kind, self.first = arr, block, imap, kind, first
        self.gdtype, self.gshape, self.gimap = gdtype, gshape, gimap


def _all_zero(*ids):
    return functools.reduce(jnp.logical_and, [i == 0 for i in ids])


def _par(arr):
    nd = arr.ndim
    return _A(arr, arr.shape, lambda *ids: (0,) * nd, "acc", first=_all_zero)


def _op_fwd(name, fn, grid, args, outs):
    n_in = len(args)

    def body(*refs):
        vals = [r[...].astype(F32) for r in refs[:n_in]]
        for r, v in zip(refs[n_in:], fn(*vals)):
            r[...] = v.astype(r.dtype)

    return _pcall(
        body, name=name, grid=grid,
        in_specs=[pl.BlockSpec(a.block, a.imap) for a in args],
        out_specs=[pl.BlockSpec(o[2], o[3]) for o in outs],
        out_shape=[jax.ShapeDtypeStruct(o[0], o[1]) for o in outs],
        compiler_params=_cparams(len(grid)),
    )(*[a.arr for a in args])


def _op_bwd(name, fn, grid, args, outs, ct_arrays, add_to_first=None):
    didx = [i for i, a in enumerate(args) if a.kind not in ("const", "fwd")]
    read = [i for i, a in enumerate(args) if a.kind != "fwd"]
    n_in, n_ct = len(read), len(outs)
    n_add = 0 if add_to_first is None else 1

    def body(*refs):
        ids = [pl.program_id(i) for i in range(len(grid))]
        vals = [jnp.zeros([d for d in a.block if d is not None], F32) for a in args]
        for i, r in zip(read, refs[:n_in]):
            vals[i] = r[...].astype(F32)

        def g(*dv):
            full = list(vals)
            for i, v in zip(didx, dv):
                full[i] = v
            return tuple(fn(*full))

        _, vjp = jax.vjp(g, *[vals[i] for i in didx])
        grads = list(vjp(tuple(r[...].astype(F32) for r in refs[n_in:n_in + n_ct])))
        if n_add:
            grads[0] = grads[0] + refs[n_in + n_ct][...]
        for gr, i, r in zip(grads, didx, refs[n_in + n_ct + n_add:]):
            a = args[i]
            if a.kind == "row":
                r[...] = gr.astype(r.dtype)
            else:
                first = a.first(*ids)

                @pl.when(first)
                def _():
                    r[...] = gr

                @pl.when(jnp.logical_not(first))
                def _():
                    r[...] += gr

    g_specs, g_shapes = [], []
    for i in didx:
        a = args[i]
        if a.kind == "row":
            g_specs.append(pl.BlockSpec(a.block, a.gimap or a.imap))
            g_shapes.append(jax.ShapeDtypeStruct(a.gshape or a.arr.shape, a.gdtype))
        else:
            g_specs.append(pl.BlockSpec(a.block, a.imap))
            g_shapes.append(jax.ShapeDtypeStruct(a.arr.shape, F32))
    return _pcall(
        body, name=name, grid=grid,
        in_specs=[pl.BlockSpec(args[i].block, args[i].imap) for i in read] + [pl.BlockSpec(o[2], o[3]) for o in outs]
        + [pl.BlockSpec(args[didx[0]].block, args[didx[0]].imap)] * n_add,
        out_specs=g_specs, out_shape=g_shapes,
        compiler_params=_cparams(len(grid)),
    )(*[args[i].arr for i in read], *ct_arrays, *([add_to_first] if n_add else []))


def _rowop(name, fn, grid, args, outs):
    res = _op_fwd(name, fn, grid, args, outs)
    return res, lambda *cts, add_to_first=None: _op_bwd(name + "_bwd", fn, grid, args, outs, cts, add_to_first)


def _pick(n, cap):
    best = None
    for t in range(LANE, cap + 1, LANE):
        if n % t == 0:
            best = t
    return best or n


def _mm(a, b, mode, out_dtype, name, epi=None, aux=None, out_split=None):
    if mode == "nn":
        (m, k), n = a.shape, b.shape[1]
    elif mode == "nt":
        (m, k), n = a.shape, b.shape[0]
    else:
        (k, m), n = a.shape, b.shape[1]
    assert k <= MM_K_MAX
    tm = MM_ROWS if m % MM_ROWS == 0 else m
    tn = n // out_split if out_split else _pick(n, MM_COLS_MAX)
    a_spec = pl.BlockSpec((k, tm), lambda j, i: (0, i)) if mode == "tn" else pl.BlockSpec((tm, k), lambda j, i: (i, 0))
    b_spec = pl.BlockSpec((tn, k), lambda j, i: (j, 0)) if mode == "nt" else pl.BlockSpec((k, tn), lambda j, i: (0, j))
    dims = {"nn": (1, 0), "nt": (1, 1), "tn": (0, 0)}[mode]
    aux_spec = pl.BlockSpec((tm, tn), lambda j, i: (i, j))
    if out_split:
        o_spec, o_shape = pl.BlockSpec((None, tm, tn), lambda j, i: (j, i, 0)), (out_split, m, tn)
    else:
        o_spec, o_shape = aux_spec, (m, n)
    n_aux = 0 if aux is None else 1

    def body(*refs):
        o_ref = refs[2 + n_aux]
        r = _dg(refs[0][...], refs[1][...], *dims)
        if epi == "sqrelu":
            r = jnp.maximum(r, 0.0)
            r = r * r
        elif epi == "dsqrelu":
            r = r * (2.0 * jnp.sqrt(refs[2][...].astype(F32)))
        o_ref[...] = r.astype(o_ref.dtype)

    return _pcall(
        body, name=name, grid=(n // tn, m // tm),
        in_specs=[a_spec, b_spec] + [aux_spec] * n_aux, out_specs=o_spec,
        out_shape=jax.ShapeDtypeStruct(o_shape, out_dtype), compiler_params=_cparams(2),
    )(a, b, *([aux] if aux is not None else []))


ROW_CHUNK = 16


def _softmax_chunks(s_scr, n_keys, scale, emit):
    for r0 in range(0, s_scr.shape[0], ROW_CHUNK):
        rows = slice(r0, r0 + ROW_CHUNK)
        s = s_scr[rows, :n_keys]
        e = jnp.exp((s - jnp.max(s, axis=-1, keepdims=True)) * scale)
        emit(rows, e, 1.0 / jnp.sum(e, axis=-1, keepdims=True))


def _attn_fwd_block(v, n, scale, s_scr, e_scr, l_scr):
    def emit(rows, e, inv_l):
        e_scr[rows, :n] = e.astype(BF16)
        l_scr[rows, :] = jnp.broadcast_to(inv_l, (ROW_CHUNK, LANE))

    _softmax_chunks(s_scr, n, scale, emit)
    return _dg(e_scr[:, :n], v, 1, 0) * l_scr[...]


def _attn_bwd_block(q, k, o, do, scale, s_scr, dp_scr, p_scr, ds_scr):
    n = k.shape[0]

    def emit(rows, e, inv_l):
        p = e * inv_l
        delta = jnp.sum(do[rows, :] * o[rows, :], axis=-1, keepdims=True)
        p_scr[rows, :n] = p.astype(BF16)
        ds_scr[rows, :n] = (p * (dp_scr[rows, :n] - delta) * scale).astype(BF16)

    _softmax_chunks(s_scr, n, scale, emit)
    ds = ds_scr[:, :n]
    return _dg(ds, k, 1, 0), _dg(ds, q, 0, 0), _dg(p_scr[:, :n], do, 0, 0)


def _call_with_exchange(body, xchg, *, name, grid, in_specs, out_specs, out_shape, operands, scratch_shapes=()):
    if xchg is None:
        res = _pcall(body, name=name, grid=grid, in_specs=in_specs, out_specs=out_specs, out_shape=out_shape,
                     scratch_shapes=list(scratch_shapes), compiler_params=_cparams(len(grid)))(*operands)
        return list(res), []
    n_in, n_out, n_sc, n = len(in_specs), len(out_specs), len(scratch_shapes), xchg.n

    def wrapped(*refs):
        ins, x_refs = refs[:n_in], refs[n_in:n_in + n]
        outs, xo_refs = refs[n_in + n:n_in + n + n_out], refs[n_in + n + n_out:n_in + 2 * n + n_out]
        scratch, sems = refs[n_in + 2 * n + n_out:n_in + 2 * n + n_out + n_sc], refs[n_in + 2 * n + n_out + n_sc:]
        ids = [pl.program_id(i) for i in range(len(grid))]

        @pl.when(functools.reduce(jnp.logical_and, [i == 0 for i in ids]))
        def _():
            xchg.start(x_refs, xo_refs, sems)

        body(*ins, *outs, *scratch)

        @pl.when(functools.reduce(jnp.logical_and, [i == g - 1 for i, g in zip(ids, grid)]))
        def _():
            xchg.wait(x_refs, xo_refs, sems)

    res = _pcall(wrapped, name=name, grid=grid, in_specs=list(in_specs) + xchg.specs,
                 out_specs=list(out_specs) + xchg.specs, out_shape=list(out_shape) + xchg.out_shape,
                 scratch_shapes=list(scratch_shapes) + xchg.scratch, compiler_params=_cparams(len(grid)),
                 )(*operands, *xchg.bufs)
    return list(res[:n_out]), list(res[n_out:])


def _head_half(i, shape):
    lane = lax.broadcasted_iota(jnp.int32, shape, len(shape) - 1)
    return (lane < LANE // 2) if i == 0 else (lane >= LANE // 2)


def _mla_attn(q, k, v, tc, ctx_q, name, xchg=None):
    assert MLA_HPS == 2 and MLA_V == LANE // 2
    bsz, t_all, _ = q.shape
    n_t = t_all // TB
    grid = (bsz, MLA_HEADS // MLA_HPS, n_t)
    q_spec = pl.BlockSpec((None, TB, MLA_HPS * LANE), lambda b, h, t: (b, t, h))
    k_spec = pl.BlockSpec((None, t_all, MLA_HPS * LANE), lambda b, h, t: (b, 0, h))
    v_spec = pl.BlockSpec((None, t_all, LANE), lambda b, h, t: (b, 0, h))
    o_spec = pl.BlockSpec((None, TB, LANE), lambda b, h, t: (b, t, h))
    heads = [slice(i * LANE, (i + 1) * LANE) for i in range(MLA_HPS)]
    scale = MLA_QK ** -0.5
    f32_scr, bf16_scr = pltpu.VMEM((TB, t_all), F32), pltpu.VMEM((TB, t_all), BF16)
    o_shape = jax.ShapeDtypeStruct(v.shape, F32)

    def fwd_body(q_ref, k_ref, v_ref, o_ref, *scr):
        t = pl.program_id(2)

        def run(keys):
            n = keys.stop
            for i, hs in enumerate(heads):
                scr[3 * i][:, :n] = _dg(q_ref[:, hs], k_ref[keys, hs], 1, 1)
            both = [_attn_fwd_block(v_ref[keys, :], n, scale, *scr[3 * i:3 * i + 3]) for i in range(MLA_HPS)]
            o_ref[...] = jnp.where(_head_half(0, both[0].shape), both[0], both[1])

        @pl.when(t == 0)
        def _():
            if ctx_q:
                run(slice(0, tc))
            else:
                o_ref[...] = jnp.zeros_like(o_ref)

        @pl.when(t > 0)
        def _():
            run(slice(0, t_all))

    (o,), gathered = _call_with_exchange(
        fwd_body, xchg, name=name, grid=grid, in_specs=[q_spec, k_spec, v_spec], out_specs=[o_spec],
        out_shape=[o_shape], operands=(q, k, v),
        scratch_shapes=[f32_scr, bf16_scr, pltpu.VMEM((TB, LANE), F32)] * MLA_HPS)

    def bwd(do, xchg=None):
        def bwd_body(q_ref, k_ref, v_ref, o_ref, do_ref, dq_ref, dk_ref, dv_ref, *scr):
            t = pl.program_id(2)

            def run(keys, first):
                n = keys.stop
                dos = [jnp.where(_head_half(i, do_ref.shape), do_ref[...], 0.0) for i in range(MLA_HPS)]
                for i, hs in enumerate(heads):
                    scr[4 * i][:, :n] = _dg(q_ref[:, hs], k_ref[keys, hs], 1, 1)
                    scr[4 * i + 1][:, :n] = _dg(dos[i], v_ref[keys, :], 1, 1)
                dvs = []
                for i, hs in enumerate(heads):
                    dq, dk, dv = _attn_bwd_block(q_ref[:, hs], k_ref[keys, hs], o_ref[...], dos[i], scale,
                                                 *scr[4 * i:4 * i + 4])
                    dq_ref[:, hs] = dq
                    dvs.append(dv)
                    if first:
                        dk_ref[keys, hs] = dk
                    else:
                        dk_ref[keys, hs] += dk
                if first:
                    dv_ref[keys, :] = dvs[0] + dvs[1]
                else:
                    dv_ref[keys, :] += dvs[0] + dvs[1]

            @pl.when(t == 0)
            def _():
                dk_ref[...] = jnp.zeros_like(dk_ref)
                dv_ref[...] = jnp.zeros_like(dv_ref)
                if ctx_q:
                    run(slice(0, tc), True)
                else:
                    dq_ref[...] = jnp.zeros_like(dq_ref)

            @pl.when(t > 0)
            def _():
                run(slice(0, t_all), False)

        return _call_with_exchange(
            bwd_body, xchg, name=name + "_bwd", grid=grid, in_specs=[q_spec, k_spec, v_spec, o_spec, o_spec],
            out_specs=[q_spec, k_spec, v_spec],
            out_shape=[jax.ShapeDtypeStruct(q.shape, F32), jax.ShapeDtypeStruct(q.shape, F32), o_shape],
            operands=(q, k, v, o, do), scratch_shapes=[f32_scr, f32_scr, bf16_scr, bf16_scr] * MLA_HPS)

    return o, gathered, bwd


def _swa_block(q, keys, vals, sink, mask):
    qs = jnp.concatenate(list(_split(q, SWA_GROUP, 1)), axis=0)
    sk = jnp.sum(sink, axis=-1, keepdims=True) * (1.0 / LANE)
    s = _nt(qs, keys) * (SWA_HEAD_DIM ** -0.5)
    if mask is not None:
        s = jnp.where(mask, s, NEG_INF)
    o = _split(_nn(_softmax_rows(s, sk), vals + _roll(vals, LANE // 2, 1)), SWA_GROUP, 0)
    low = _head_half(0, o[0].shape)
    return jnp.concatenate([jnp.where(low, o[0], o[1]), jnp.where(low, o[2], o[3])], axis=1)


def _swa_ctx_block(q, kc, vc, sink):
    return _swa_block(q, kc, vc, sink, None)


def _swa_win_block(q, kc, kw, vc, vw, sink, mask):
    return _swa_block(q, jnp.concatenate([kc, kw], axis=0), jnp.concatenate([vc, vw], axis=0), sink, mask)


def _swa_attn(q, k, p_all, sink_b, tc, ctx_q, name, xchg=None):
    bsz, t_all, _ = q.shape
    n_q = t_all // QB_SWA
    n_cq = tc // QB_SWA
    lat = t_all - tc
    span = QB_SWA + 2 * WINDOW
    gw = SWA_GROUP * LANE
    grid = (bsz, SWA_KV_HEADS, n_q)
    q_spec = pl.BlockSpec((None, QB_SWA, gw), lambda b, g, i: (b, i, g))
    k_spec = pl.BlockSpec((None, t_all, LANE), lambda b, g, i: (b, 0, g))
    v_spec = pl.BlockSpec((None, t_all, LANE), lambda b, g, i: (b, 0, PC_SV // LANE + g))
    s_spec = pl.BlockSpec((None, SWA_GROUP * QB_SWA, LANE), lambda b, g, i: (g, 0, 0))

    def window(i):
        q0 = (i - n_cq) * QB_SWA
        w0 = jnp.clip(q0 - WINDOW, 0, lat - span)
        w0 = pl.multiple_of(w0, WINDOW)
        shape = (SWA_GROUP * QB_SWA, tc + span)
        qi = q0 + lax.broadcasted_iota(jnp.int32, shape, 0) % QB_SWA
        col = lax.broadcasted_iota(jnp.int32, shape, 1)
        kj = w0 + col - tc
        mask = (col < tc) | ((kj >= qi - WINDOW) & (kj <= qi + WINDOW))
        return w0, mask

    def fwd_body(q_ref, k_ref, v_ref, s_ref, o_ref):
        i = pl.program_id(2)

        @pl.when(i < n_cq)
        def _():
            if ctx_q:
                o_ref[...] = _swa_ctx_block(q_ref[...].astype(F32), k_ref[0:tc, :], v_ref[0:tc, :].astype(F32),
                                            s_ref[...])
            else:
                o_ref[...] = jnp.zeros_like(o_ref)

        @pl.when(i >= n_cq)
        def _():
            w0, mask = window(i)
            o_ref[...] = _swa_win_block(q_ref[...].astype(F32), k_ref[0:tc, :], k_ref[pl.ds(tc + w0, span), :],
                                        v_ref[0:tc, :].astype(F32), v_ref[pl.ds(tc + w0, span), :].astype(F32),
                                        s_ref[...], mask)

    o_spec = pl.BlockSpec((None, QB_SWA, SWA_GROUP * SWA_HEAD_DIM), lambda b, g, i: (b, i, g))
    (o,), gathered = _call_with_exchange(
        fwd_body, xchg, name=name, grid=grid, in_specs=[q_spec, k_spec, v_spec, s_spec], out_specs=[o_spec],
        out_shape=[jax.ShapeDtypeStruct((bsz, t_all, SWA_HEADS * SWA_HEAD_DIM), F32)],
        operands=(q, k, p_all, sink_b))

    def bwd(do, xchg=None):
        def bwd_body(q_ref, k_ref, v_ref, s_ref, do_ref, dq_ref, dk_ref, dv_ref, ds_ref):
            i = pl.program_id(2)

            @pl.when(i == 0)
            def _():
                dk_ref[...] = jnp.zeros_like(dk_ref)
                dv_ref[...] = jnp.zeros_like(dv_ref)
                ds_ref[...] = jnp.zeros_like(ds_ref)

            @pl.when(i < n_cq)
            def _():
                if ctx_q:
                    _, vjp = jax.vjp(_swa_ctx_block, q_ref[...].astype(F32), k_ref[0:tc, :].astype(F32),
                                     v_ref[0:tc, :].astype(F32), s_ref[...])
                    dq, dk, dv, ds = vjp(do_ref[...])
                    dq_ref[...] = dq
                    dk_ref[0:tc, :] += dk
                    dv_ref[0:tc, :] += dv
                    ds_ref[...] += ds
                else:
                    dq_ref[...] = jnp.zeros_like(dq_ref)

            @pl.when(i >= n_cq)
            def _():
                w0, mask = window(i)
                win = pl.ds(tc + w0, span)
                _, vjp = jax.vjp(functools.partial(_swa_win_block, mask=mask), q_ref[...].astype(F32),
                                 k_ref[0:tc, :].astype(F32), k_ref[win, :].astype(F32),
                                 v_ref[0:tc, :].astype(F32), v_ref[win, :].astype(F32), s_ref[...])
                dq, dkc, dkw, dvc, dvw, ds = vjp(do_ref[...])
                dq_ref[...] = dq
                dk_ref[0:tc, :] += dkc
                dk_ref[win, :] += dkw
                dv_ref[0:tc, :] += dvc
                dv_ref[win, :] += dvw
                ds_ref[...] += ds

        kv_out = pl.BlockSpec((None, t_all, LANE), lambda b, g, i: (b, 0, g))
        ds_spec = pl.BlockSpec((None, None, SWA_GROUP * QB_SWA, LANE), lambda b, g, i: (b, g, 0, 0))
        kv_shape = jax.ShapeDtypeStruct((bsz, t_all, SWA_KV_HEADS * LANE), F32)
        return _call_with_exchange(
            bwd_body, xchg, name=name + "_bwd", grid=grid, in_specs=[q_spec, k_spec, v_spec, s_spec, o_spec],
            out_specs=[q_spec, kv_out, kv_out, ds_spec],
            out_shape=[jax.ShapeDtypeStruct(q.shape, F32), kv_shape, kv_shape,
                       jax.ShapeDtypeStruct((bsz,) + sink_b.shape, F32)],
            operands=(q, k, p_all, sink_b, do))

    return o, gathered, bwd


def _scan_pair(chains, scratch):
    t_all, c = chains[0][0].shape
    n_tiles = t_all // SUBLANE
    row8 = lax.broadcasted_iota(jnp.int32, (t_all, c), 0) % SUBLANE
    refs = [scratch[0:3], scratch[3:6]]
    for (a, u, reverse), (a_s, u_s, _) in zip(chains, refs):
        for d in (1, 2, 4):
            sh = d if not reverse else t_all - d
            ar, ur = pltpu.roll(a, sh, 0), pltpu.roll(u, sh, 0)
            m = (row8 >= d) if not reverse else (row8 < SUBLANE - d)
            u = jnp.where(m, a * ur + u, u)
            a = jnp.where(m, a * ar, a)
        a_s[...] = a
        u_s[...] = u

    def step(j, carries):
        out = []
        for (_, _, reverse), (a_s, u_s, c_s), carry in zip(chains, refs, carries):
            tile = j if not reverse else n_tiles - 1 - j
            base = pl.multiple_of(tile * SUBLANE, SUBLANE)
            c_s[pl.ds(base, SUBLANE), :] = jnp.broadcast_to(carry, (SUBLANE, c))
            last = base + (0 if reverse else SUBLANE - 1)
            out.append(a_s[pl.ds(last, 1), :] * carry + u_s[pl.ds(last, 1), :])
        return tuple(out)

    lax.fori_loop(0, n_tiles, step, (jnp.zeros((1, c), F32),) * 2, unroll=4)
    return [a_s[...] * c_s[...] + u_s[...] for a_s, u_s, c_s in refs]


def _shift_rows(x, reverse_src):
    t_all = x.shape[0]
    row = lax.broadcasted_iota(jnp.int32, x.shape, 0)
    if reverse_src:
        return jnp.where(row == t_all - 1, 0.0, pltpu.roll(x, t_all - 1, 0))
    return jnp.where(row == 0, 0.0, pltpu.roll(x, 1, 0))


def _lru_scan(a0, u0, a1, u1, name):
    bsz, t_all, w = a0.shape
    grid = (bsz, w // LANE)
    spec = pl.BlockSpec((None, t_all, LANE), lambda b, c: (b, 0, c))
    scratch = [pltpu.VMEM((t_all, LANE), F32)] * 6
    shape = jax.ShapeDtypeStruct(a0.shape, F32)

    def fwd_body(a0_ref, u0_ref, a1_ref, u1_ref, h0_ref, h1_ref, *scr):
        h0_ref[...], h1_ref[...] = _scan_pair([(a0_ref[...], u0_ref[...], False), (a1_ref[...], u1_ref[...], True)],
                                              scr)

    h0, h1 = _pcall(fwd_body, name=name, grid=grid, in_specs=[spec] * 4, out_specs=[spec] * 2,
                    out_shape=[shape] * 2, scratch_shapes=scratch, compiler_params=_cparams(2))(a0, u0, a1, u1)

    def bwd(dh0, dh1):
        def bwd_body(a0_ref, h0_ref, g0_ref, a1_ref, h1_ref, g1_ref, da0_ref, du0_ref, da1_ref, du1_ref, *scr):
            g0, g1 = _scan_pair([(_shift_rows(a0_ref[...], True), g0_ref[...], True),
                                 (_shift_rows(a1_ref[...], False), g1_ref[...], False)], scr)
            du0_ref[...] = g0
            da0_ref[...] = g0 * _shift_rows(h0_ref[...], False)
            du1_ref[...] = g1
            da1_ref[...] = g1 * _shift_rows(h1_ref[...], True)

        return _pcall(bwd_body, name=name + "_bwd", grid=grid, in_specs=[spec] * 6, out_specs=[spec] * 4,
                      out_shape=[shape] * 4, scratch_shapes=scratch,
                      compiler_params=_cparams(2))(a0, h0, dh0, a1, h1, dh1)

    return h0, h1, bwd


def _f_mod(x, g, shift, scale):
    return (_rms(x, g, D_MODEL) * (1.0 + scale) + shift,)


def _f_mla_q(cq, ga, w, gh, cos, sa, sb):
    n = _rms(cq, ga, MLA_Q_RANK)
    outs = []
    for wh in _split(w, MLA_HEADS, 1):
        outs.append(_rope(_rms(_nn(n, wh), gh, MLA_QK), cos, sa, sb, MLA_ROPE // 4))
    return (jnp.concatenate(outs, axis=1),)


def _f_mla_kv(ckv, krp, ga, wk, wv, gh, cos, sa, sb):
    n = _rms(ckv, ga, MLA_KV_RANK)
    outs = []
    for wh in _split(wk, MLA_HEADS, 1):
        outs.append(_rope(_rms(_nn(n, wh) + krp, gh, MLA_QK), cos, sa, sb, MLA_ROPE // 4))
    return jnp.concatenate(outs, axis=1), _nn(n, wv)


def _f_conv(x, w0, w1, w2, w3, bias, tc):
    t_all = x.shape[0]
    row = lax.broadcasted_iota(jnp.int32, x.shape, 0)
    lo = jnp.where(row < tc, 0, tc)
    hi = jnp.where(row < tc, tc, t_all)
    y = bias + jnp.zeros_like(x)
    for kk, wk in enumerate((w0, w1, w2, w3)):
        src = row + (kk - 2)
        xs = x if kk == 2 else _roll(x, 2 - kk, 0)
        y = y + wk * jnp.where((src >= lo) & (src < hi), xs, 0.0)
    return (y,)


def _f_gates(xc, w16, b00, b01, b10, b11, sp0, sp1):
    ws = _unstack(w16)
    n_cb = LRU_WIDTH // LANE
    xcs = _split(xc, n_cb, 1)
    bias = [_split(b, n_cb, 1) for b in (b00, b01, b10, b11)]
    sps = [_split(s, n_cb, 1) for s in (sp0, sp1)]
    res = [[], [], [], []]
    for c in range(n_cb):
        for z in range(2):
            r = _sig(_nn(xcs[c], ws[c * 4 + 2 * z]) + bias[2 * z][c])
            i = _sig(_nn(xcs[c], ws[c * 4 + 2 * z + 1]) + bias[2 * z + 1][c])
            la = -LRU_C * r * sps[z][c]
            res[2 * z].append(jnp.exp(la))
            res[2 * z + 1].append(jnp.sqrt(-jnp.tanh(la) * (jnp.exp(2.0 * la) + 1.0)) * (i * xcs[c]))
    return tuple(jnp.concatenate(r, axis=1) for r in res)


def _f_swa_qk(sq, sk, gq, gk, cos, sa, sb):
    qs = [_rope(_rms(x, gq, SWA_HEAD_DIM), cos, sa, sb, SWA_HEAD_DIM // 4) for x in _split(sq, SWA_HEADS, 1)]
    ks = [_rope(_rms(x, gk, SWA_HEAD_DIM), cos, sa, sb, SWA_HEAD_DIM // 4) for x in _split(sk, SWA_KV_HEADS, 1)]
    return jnp.concatenate(qs, axis=1), jnp.concatenate(ks, axis=1)


def _f_qkv(cq, ckv, krp, sq, sk, q_a_g, wuq, mla_q_g, kv_a_g, wk, wv, mla_k_g, swa_q_g, swa_k_g,
           m_cos, m_sa, m_sb, s_cos, s_sa, s_sb):
    return (*_f_mla_q(cq, q_a_g, wuq, mla_q_g, m_cos, m_sa, m_sb),
            *_f_mla_kv(ckv, krp, kv_a_g, wk, wv, mla_k_g, m_cos, m_sa, m_sb),
            *_f_swa_qk(sq, sk, swa_q_g, swa_k_g, s_cos, s_sa, s_sb))


def _f_merge(oa, h0, h1, lg, oc, ga, gb, gc):
    ob = (h0 + h1) * _gelu(lg)
    return (jnp.concatenate([_rms(oa, ga, GROUP_WIDTH), _rms(ob, gb, GROUP_WIDTH), _rms(oc, gc, GROUP_WIDTH)],
                            axis=1),)


def _f_resid_mod(x, y, gate, g, shift, scale):
    x1 = x + gate * y
    return x1, _rms(x1, g, D_MODEL) * (1.0 + scale) + shift


def _f_resid(x, y, gate):
    return (x + gate * y,)


def _hosted(hooks, key, arg=None):
    make, done = hooks.get(key, (None, None))
    xchg = make(arg) if make is not None else None
    return xchg, (done if xchg is not None else lambda outs: None)


def _layer(li, x, mods, w, s, tabs, tc, ctx_q, hooks, latent_dx_only):
    bsz, t_all, _ = x.shape
    n_t = t_all // TB
    grid = (bsz, n_t)
    rows = lambda b, t: (b, t, 0)

    def row(arr, width=None, idx=0, gdtype=F32, gshape=None):
        width = width or arr.shape[-1]
        return _A(arr, (None, TB, width), lambda b, t: (b, t, idx), "row", gdtype=gdtype, gshape=gshape,
                  gimap=rows if gshape is not None else None)

    def out(width, dtype, imap=rows):
        return ((bsz, t_all, width), dtype, (None, TB, width), imap)

    def modarg(arr):
        return _A(arr, (None, None, 1, D_MODEL), lambda b, t: (b, jnp.minimum(t, 1), 0, 0), "acc",
                  first=lambda b, t: t <= 1)

    def tab(arr):
        return _A(arr, (TB, LANE), lambda b, t: (t, 0), "const")

    def pcol(p_all, col, width):
        return row(p_all, width, col // width, gdtype=BF16, gshape=(bsz, t_all, width))

    nm = lambda base: "%s_l%d" % (base, li)
    sh1, sc1, g1, sh2, sc2, g2 = mods
    m_all = bsz * t_all

    x_arg = row(x)
    if latent_dx_only:
        n_c = tc // TB
        x_arg.gshape, x_arg.gimap = (bsz, t_all - tc, D_MODEL), lambda b, t: (b, jnp.maximum(t - n_c, 0), 0)
    (h,), b_mod1 = _rowop(nm("mod1"), _f_mod, grid, [x_arg, _par(s["norm1_g"]), modarg(sh1), modarg(sc1)],
                          [out(D_MODEL, BF16)])
    p_all = _mm(h.reshape(m_all, D_MODEL), w["win"], "nn", BF16, nm("mm_in")).reshape(bsz, t_all, P_WIDTH)

    (q_a, k_a, v_a, q_c, k_c), b_qkv = _rowop(
        nm("qkv"), _f_qkv, grid,
        [pcol(p_all, PC_CQ, 256), pcol(p_all, PC_CKV, 128), pcol(p_all, PC_KR, 128), pcol(p_all, PC_SQ, 1024),
         pcol(p_all, PC_SK, 256)]
        + [_par(a) for a in (s["q_a_g"], w["wuq"], s["mla_q_g"], s["kv_a_g"], w["wk"], w["wv"], s["mla_k_g"],
                             s["swa_q_g"], s["swa_k_g"])]
        + [tab(a) for a in tabs["mla"] + tabs["swa"]],
        [out(MLA_HEADS * LANE, BF16), out(MLA_HEADS * LANE, BF16), out(MLA_HEADS * MLA_V, BF16),
         out(SWA_HEADS * LANE, BF16), out(SWA_KV_HEADS * LANE, BF16)])

    xchg, done = _hosted(hooks, "mla_fwd")
    o_a, got, b_attn_a = _mla_attn(q_a, k_a, v_a, tc, ctx_q, nm("mla_attn"), xchg)
    done(got)

    n_cb = LRU_WIDTH // LANE
    conv_grid = (n_cb, bsz)
    cpar = lambda arr: _A(arr, (1, LANE), lambda c, b: (0, c), "acc", first=lambda c, b: b == 0)
    conv_args = [_A(p_all, (None, t_all, LANE), lambda c, b: (b, 0, PC_LX // LANE + c), "row", gdtype=BF16,
                    gshape=(bsz, t_all, LRU_WIDTH), gimap=lambda c, b: (b, 0, c))]
    conv_args += [cpar(a) for a in s["conv_w"]] + [cpar(s["conv_b"])]
    conv_out = [((bsz, t_all, LRU_WIDTH), F32, (None, t_all, LANE), lambda c, b: (b, 0, c))]
    (xc,), b_conv = _rowop(nm("lru_conv"), functools.partial(_f_conv, tc=tc), conv_grid, conv_args, conv_out)
    rot = lambda b, t: (b, (t + n_t - 1) % n_t, 0)
    (a0, u0, a1, u1), b_gates = _rowop(
        nm("lru_gates"), _f_gates, grid,
        [row(xc), _par(s["wbd"])] + [_par(a) for a in s["gate_b"]] + [_par(a) for a in s["sp"]],
        [out(LRU_WIDTH, F32), out(LRU_WIDTH, F32), out(LRU_WIDTH, F32, rot), out(LRU_WIDTH, F32, rot)])
    h0, h1, b_scan = _lru_scan(a0, u0, a1, u1, nm("lru_scan"))
    h1_arg = _A(h1, (None, TB, LRU_WIDTH), rot, "row")

    xchg, done = _hosted(hooks, "swa_fwd")
    o_c, got, b_attn_c = _swa_attn(q_c, k_c, p_all, s["sink_b"], tc, ctx_q, nm("swa_attn"), xchg)
    done(got)

    (y_in,), b_merge = _rowop(nm("merge"), _f_merge, grid,
                              [row(o_a), row(h0), h1_arg, pcol(p_all, PC_LG, 512), row(o_c), _par(s["g_a"]),
                               _par(s["g_b"]), _par(s["g_c"])],
                              [out(MIX_P, BF16)])
    y = _mm(y_in.reshape(m_all, MIX_P), w["wout"], "nn", F32, nm("mm_out")).reshape(bsz, t_all, D_MODEL)
    (x1, hm), b_rm = _rowop(nm("resid_mod"), _f_resid_mod, grid,
                            [row(x), row(y, gdtype=BF16), modarg(g1), _par(s["norm2_g"]), modarg(sh2), modarg(sc2)],
                            [out(D_MODEL, F32), out(D_MODEL, BF16)])
    act = _mm(hm.reshape(m_all, D_MODEL), w["ff1"], "nn", BF16, nm("mm_ff1"), epi="sqrelu")
    y2 = _mm(act, w["ff2"], "nn", F32, nm("mm_ff2")).reshape(bsz, t_all, D_MODEL)
    (x2,), b_res = _rowop(nm("resid"), _f_resid, grid,
                          [_A(x1, (None, TB, D_MODEL), rows, "fwd"), row(y2, gdtype=BF16), modarg(g2)],
                          [out(D_MODEL, F32)])

    def bwd(dx2, hooks):
        dw, ds = {}, {}
        dy2, dg2 = b_res(dx2)
        dy2 = dy2.reshape(m_all, D_MODEL)
        dpre = _mm(dy2, w["ff2"], "nt", BF16, nm("mm_ff2_dx"), epi="dsqrelu", aux=act)
        dw["ff2"] = _mm(act, dy2, "tn", BF16, nm("mm_ff2_dw"))
        dhm = _mm(dpre, w["ff1"], "nt", F32, nm("mm_ff1_dx")).reshape(bsz, t_all, D_MODEL)
        dw["ff1"] = _mm(hm.reshape(m_all, D_MODEL), dpre, "tn", BF16, nm("mm_ff1_dw"), out_split=N_DEV)
        dxa, dy, dg1, ds["norm2_g"], dsh2, dsc2 = b_rm(dx2, dhm)
        dy = dy.reshape(m_all, D_MODEL)
        dy_in = _mm(dy, w["wout"], "nt", F32, nm("mm_out_dx")).reshape(bsz, t_all, MIX_P)
        dw["wout"] = _mm(y_in.reshape(m_all, MIX_P), dy, "tn", BF16, nm("mm_out_dw"))
        do_a, dh0, dh1, dlg, do_c, ds["g_a"], ds["g_b"], ds["g_c"] = b_merge(dy_in)

        (dq_c, dk_c, dsv, dsink), _ = b_attn_c(do_c)
        ds["sink_b"] = jnp.sum(dsink, axis=0)

        da0, du0, da1, du1 = b_scan(dh0, dh1)
        gates_g = b_gates(da0, du0, da1, du1)
        dxc, ds["wbd"] = gates_g[0], gates_g[1]
        ds["gate_b"], ds["sp"] = list(gates_g[2:6]), list(gates_g[6:8])
        conv_g = b_conv(dxc)
        dlx, ds["conv_w"], ds["conv_b"] = conv_g[0], list(conv_g[1:5]), conv_g[5]

        xchg, done = _hosted(hooks, "mla_bwd", (dw, ds))
        (dq_a, dk_a, dv_a), got = b_attn_a(do_a, xchg)
        done(got)
        (dcq, dckv, dkr, dsq, dsk, ds["q_a_g"], dw["wuq"], ds["mla_q_g"], ds["kv_a_g"], dw["wk"], dw["wv"],
         ds["mla_k_g"], ds["swa_q_g"], ds["swa_k_g"]) = b_qkv(dq_a, dk_a, dv_a, dq_c, dk_c)

        dp = jnp.concatenate([dsq, dlx, dlg, dcq, dsk, dsv.astype(BF16), dckv, dkr], axis=-1)
        dp = dp.reshape(m_all, P_WIDTH)
        dh = _mm(dp, w["win"], "nt", F32, nm("mm_in_dx")).reshape(bsz, t_all, D_MODEL)
        dw["win"] = _mm(h.reshape(m_all, D_MODEL), dp, "tn", BF16, nm("mm_in_dw"))
        dx, ds["norm1_g"], dsh1, dsc1 = b_mod1(dh, add_to_first=dxa)
        return dx, [dsh1, dsc1, dg1, dsh2, dsc2, dg2], dw, ds

    return x2, bwd


def _loss_and_grad(x2, target, tc):
    bsz, t_all, d = x2.shape
    n_t = t_all // TB
    n_c = tc // TB

    def body(x_ref, t_ref, l_ref, dx_ref):
        b, t = pl.program_id(0), pl.program_id(1)

        @pl.when((b == 0) & (t == 0))
        def _():
            l_ref[...] = jnp.zeros_like(l_ref)

        @pl.when(t < n_c)
        def _():
            dx_ref[...] = jnp.zeros_like(dx_ref)

        @pl.when(t >= n_c)
        def _():
            e = x_ref[...] - t_ref[...]
            dx_ref[...] = e * (1.0 / d)
            l_ref[...] += jnp.sum(e * e) * (0.5 / d)

    loss, dx = _pcall(
        body, name="loss", grid=(bsz, n_t),
        in_specs=[pl.BlockSpec((None, TB, d), lambda b, t: (b, t, 0)),
                  pl.BlockSpec((None, TB, d), lambda b, t: (b, jnp.maximum(t - n_c, 0), 0))],
        out_specs=[pl.BlockSpec((SUBLANE, LANE), lambda b, t: (0, 0)),
                   pl.BlockSpec((None, TB, d), lambda b, t: (b, t, 0))],
        out_shape=[jax.ShapeDtypeStruct((SUBLANE, LANE), F32), jax.ShapeDtypeStruct(x2.shape, F32)],
        compiler_params=_cparams(2))(x2, target)
    return loss[0, 0], dx


def _rope_tables(lat, tc, dim, lane0):
    quarter = dim // 4
    pos = np.arange(lat)
    grid_pos = np.stack([pos // GRID_W, pos % GRID_W], axis=-1).astype(np.float32)
    lane = np.arange(LANE)
    p = np.clip(lane - lane0, 0, dim - 1)
    active = (lane >= lane0) & (lane < lane0 + dim)
    axis, half, qi = p // (dim // 2), (p % (dim // 2)) // quarter, p % quarter
    inv = (np.float32(ROPE_THETA) ** (-qi.astype(np.float32) / np.float32(quarter))).astype(np.float32)
    ang = (np.where(axis[None, :] == 0, grid_pos[:, 0:1], grid_pos[:, 1:2]) * inv[None, :]).astype(np.float32)
    cos = np.where(active, np.cos(ang), 1.0).astype(np.float32)
    sin = np.where(active, np.sin(ang), 0.0).astype(np.float32)
    sa = np.where(half == 0, -sin, 0.0).astype(np.float32)
    sb = np.where(half == 1, sin, 0.0).astype(np.float32)
    ctx1, ctx0 = np.ones((tc, LANE), np.float32), np.zeros((tc, LANE), np.float32)
    return tuple(jnp.asarray(np.concatenate([c, t], 0)) for c, t in ((ctx1, cos), (ctx0, sa), (ctx0, sb)))


_BIG = {"w_in": ((D_MODEL, IN_WIDTH // N_DEV), 1, ("win",)),
        "w_uq": ((MLA_Q_RANK, MLA_HEADS * MLA_QK // N_DEV), 1, ("wuq",)),
        "w_ukv": ((MLA_KV_RANK, MLA_HEADS * (MLA_NOPE + MLA_V) // N_DEV), 1, ("wk", "wv")),
        "w_out": ((3 * GROUP_WIDTH // N_DEV, D_MODEL), 0, ("wout",)),
        "w_ff1": ((D_MODEL, D_FF // N_DEV), 1, ("ff1",)),
        "w_ff2": ((D_FF // N_DEV, D_MODEL), 0, ("ff2",))}
_EARLY = ("w_in", "w_uq", "w_ukv")
_LATE = ("w_out", "w_ff1", "w_ff2")


def _pad_heads(wm, n_heads, dim):
    out = jnp.pad(wm.reshape(wm.shape[0], n_heads, dim), ((0, 0), (0, 0), (0, LANE - dim)))
    return out.reshape(wm.shape[0], n_heads * LANE)


def _prep_weight(name, piece):
    shp, ax, _ = _BIG[name]
    full = jnp.moveaxis(piece, 0, ax).reshape(shp[:ax] + (N_DEV * shp[ax],) + shp[ax + 1:])
    if name == "w_in":
        cq, ckv, kr, lx, lg, sq, sk, sv = _split_cols(full)
        return {"win": jnp.concatenate(
            [_pad_heads(sq, SWA_HEADS, SWA_HEAD_DIM), lx, lg, cq, _pad_heads(sk, SWA_KV_HEADS, SWA_HEAD_DIM),
             _pad_heads(sv, SWA_KV_HEADS, SWA_HEAD_DIM), ckv, jnp.pad(kr, ((0, 0), (MLA_NOPE, LANE - MLA_QK)))], axis=1)}
    if name == "w_uq":
        return {"wuq": _pad_heads(full, MLA_HEADS, MLA_QK)}
    if name == "w_ukv":
        ukv = full.reshape(MLA_KV_RANK, MLA_HEADS, MLA_NOPE + MLA_V)
        return {"wk": _pad_heads(ukv[:, :, :MLA_NOPE].reshape(MLA_KV_RANK, -1), MLA_HEADS, MLA_NOPE),
                "wv": ukv[:, :, MLA_NOPE:].reshape(MLA_KV_RANK, -1)}
    return {_BIG[name][2][0]: full}


def _split_cols(wm):
    parts, start = [], 0
    for size in IN_SIZES:
        parts.append(wm[:, start:start + size])
        start += size
    return parts


def _prep_gates(gate_w):
    gw = gate_w.reshape(2, 2, 4, 2, 64, 64)
    wbd = jnp.einsum("zgknCm,nN->knCzgNm", gw, jnp.eye(2, dtype=F32)).reshape(4, LANE, 4, LANE)
    return wbd.transpose(0, 2, 1, 3).reshape(16, LANE, LANE)


def _prep_small(raw):
    r1 = lambda a: a.reshape(1, -1)
    gg = raw["group_g"]
    sink = raw["swa_sink"].reshape(SWA_KV_HEADS, SWA_GROUP, 1, 1)
    return {
        "norm1_g": r1(raw["norm1_g"]), "norm2_g": r1(raw["norm2_g"]),
        "q_a_g": r1(raw["q_a_g"]), "kv_a_g": r1(raw["kv_a_g"]),
        "mla_q_g": jnp.pad(r1(raw["mla_q_g"]), ((0, 0), (0, LANE - MLA_QK))),
        "mla_k_g": jnp.pad(r1(raw["mla_k_g"]), ((0, 0), (0, LANE - MLA_QK))),
        "swa_q_g": jnp.pad(r1(raw["swa_q_g"]), ((0, 0), (0, LANE - SWA_HEAD_DIM))),
        "swa_k_g": jnp.pad(r1(raw["swa_k_g"]), ((0, 0), (0, LANE - SWA_HEAD_DIM))),
        "conv_w": [r1(raw["conv_w"][kk]) for kk in range(4)], "conv_b": r1(raw["conv_b"]),
        "gate_b": [r1(raw["lru_gate_b"][z, g]) for z in range(2) for g in range(2)],
        "sp": [r1(jax.nn.softplus(-raw["lru_lambda"][z])) for z in range(2)],
        "sink_b": jnp.broadcast_to(sink, (SWA_KV_HEADS, SWA_GROUP, QB_SWA, LANE)).reshape(
            SWA_KV_HEADS, SWA_GROUP * QB_SWA, LANE),
        "g_a": r1(gg[:GROUP_WIDTH]), "g_b": r1(gg[GROUP_WIDTH:2 * GROUP_WIDTH]), "g_c": r1(gg[2 * GROUP_WIDTH:])}


def _mesh_pos():
    return lax.axis_index("x"), lax.axis_index("y"), lax.axis_index("c")


def _peer(pos, k):
    return tuple(1 - p if (k >> s) & 1 else p for p, s in zip(pos, (2, 1, 0)))


def _dev_index(pos):
    return 4 * pos[0] + 2 * pos[1] + pos[2]


class _Exchange:
    def __init__(self, bufs, gather):
        self.bufs = list(bufs)
        self.n = len(self.bufs)
        self.gather = [gather] * self.n if isinstance(gather, bool) else list(gather)
        self.specs = [pl.BlockSpec(memory_space=pl.ANY)] * self.n
        self.out_shape = [jax.ShapeDtypeStruct((N_DEV,) + tuple(b.shape if g else b.shape[1:]), b.dtype)
                          for b, g in zip(self.bufs, self.gather)]
        self.scratch = [pltpu.SemaphoreType.DMA(((N_DEV - 1) * self.n,)),
                        pltpu.SemaphoreType.DMA(((N_DEV - 1) * self.n,)), pltpu.SemaphoreType.DMA((self.n,))]

    def _copies(self, x_refs, o_refs, sems, with_recvs):
        send_sems, recv_sems, local_sems = sems
        pos = _mesh_pos()
        me = _dev_index(pos)
        locals_, sends, recvs = [], [], []
        for j in range(self.n):
            src_mine = x_refs[j] if self.gather[j] else x_refs[j].at[me]
            locals_.append(pltpu.make_async_copy(src_mine, o_refs[j].at[me], local_sems.at[j]))
        for k in range(1, N_DEV):
            peer = _peer(pos, k)
            pidx = _dev_index(peer)
            for j in range(self.n):
                src = x_refs[j] if self.gather[j] else x_refs[j].at[pidx]
                sem = (k - 1) * self.n + j
                sends.append(pltpu.make_async_remote_copy(
                    src_ref=src, dst_ref=o_refs[j].at[me], send_sem=send_sems.at[sem], recv_sem=recv_sems.at[sem],
                    device_id=peer, device_id_type=pl.DeviceIdType.MESH))
                if with_recvs:
                    recvs.append(pltpu.make_async_remote_copy(
                        src_ref=src, dst_ref=o_refs[j].at[pidx], send_sem=send_sems.at[sem],
                        recv_sem=recv_sems.at[sem], device_id=peer, device_id_type=pl.DeviceIdType.MESH))
        return locals_, sends, recvs

    def start(self, x_refs, o_refs, sems):
        locals_, sends, _ = self._copies(x_refs, o_refs, sems, False)
        for cp in locals_ + sends:
            cp.start()

    def wait(self, x_refs, o_refs, sems):
        locals_, sends, recvs = self._copies(x_refs, o_refs, sems, True)
        for cp in recvs:
            cp.wait_recv()
        for cp in sends:
            cp.wait_send()
        for cp in locals_:
            cp.wait()


def _exchange(bufs, gather, name):
    xchg = _Exchange(bufs, gather)
    n = xchg.n

    def body(*refs):
        xchg.start(refs[:n], refs[n:2 * n], refs[2 * n:])
        xchg.wait(refs[:n], refs[n:2 * n], refs[2 * n:])

    return _pcall(body, name=name, out_shape=xchg.out_shape, in_specs=xchg.specs, out_specs=xchg.specs,
                  scratch_shapes=xchg.scratch)(*xchg.bufs)


def _pack(arrs, dtype):
    flat = jnp.concatenate([a.reshape(-1).astype(dtype) for a in arrs])
    rows = -(-flat.size // PACK_W)
    rows = -(-rows // 16) * 16
    return jnp.pad(flat, (0, rows * PACK_W - flat.size)).reshape(rows, PACK_W)


def _unpack(buf, shapes, lead=()):
    flat = buf.reshape(lead + (-1,))
    out, off = [], 0
    for shp in shapes:
        n = math.prod(shp)
        out.append(flat[..., off:off + n].reshape(lead + tuple(shp)))
        off += n
    return out


def _sum_sources(buf, name):
    _, r, c = buf.shape
    tr = _rows_tile(r)

    def body(x_ref, o_ref):
        acc = x_ref[0]
        for d in range(1, N_DEV):
            acc = acc + x_ref[d]
        o_ref[...] = acc

    return _pcall(body, name=name, grid=(r // tr,),
                  in_specs=[pl.BlockSpec((N_DEV, tr, c), lambda i: (0, i, 0))],
                  out_specs=pl.BlockSpec((tr, c), lambda i: (i, 0)),
                  out_shape=jax.ShapeDtypeStruct((r, c), F32), compiler_params=_cparams(1))(buf)


def _rows_tile(r):
    best = r
    for t in range(SUBLANE, ELEMWISE_ROWS_MAX + 1, SUBLANE):
        if r % t == 0:
            best = t
    return best


def _adamw(grads, wgt, m, v, name):
    n_lay = len(grads)
    n_src, r, c = grads[0].shape
    tr = _rows_tile(r)
    n_blk = r // tr
    bc1 = 1.0 - ADAM_B1 ** ADAM_STEP
    bc2 = 1.0 - ADAM_B2 ** ADAM_STEP

    def body(*refs):
        g_refs, (w_ref, m_ref, v_ref, go_ref, d_ref, mo_ref, vo_ref) = refs[:n_lay], refs[n_lay:]
        for li, g_ref in enumerate(g_refs):
            @pl.when(pl.program_id(0) == li)
            def _():
                g = g_ref[0].astype(F32)
                for d in range(1, n_src):
                    g = g + g_ref[d].astype(F32)
                m_new = ADAM_B1 * m_ref[...] + (1.0 - ADAM_B1) * g
                v_new = ADAM_B2 * v_ref[...] + (1.0 - ADAM_B2) * (g * g)
                go_ref[...] = g
                mo_ref[...] = m_new
                vo_ref[...] = v_new
                d_ref[...] = -ADAM_LR * ((m_new / bc1) / (jnp.sqrt(v_new / bc2) + ADAM_EPS) + ADAM_WD * w_ref[...])

    g_specs = [pl.BlockSpec((n_src, tr, c),
                            lambda l, i, li=li: (0, jnp.where(l == li, i, jnp.where(l > li, n_blk - 1, 0)), 0))
               for li in range(n_lay)]
    spec = pl.BlockSpec((tr, c), lambda l, i: (l * n_blk + i, 0))
    return _pcall(body, name=name, grid=(n_lay, n_blk), in_specs=g_specs + [spec, spec, spec],
                  out_specs=[spec] * 4, out_shape=[jax.ShapeDtypeStruct((n_lay * r, c), F32)] * 4,
                  compiler_params=_cparams(2))(*grads, wgt, m, v)


def _silu(z):
    return z * jax.nn.sigmoid(z)


_WEIGHTS = ("c_ctx", "w_mod", "b_mod", "norm1_g", "w_in", "q_a_g", "w_uq", "kv_a_g", "w_ukv", "mla_q_g", "mla_k_g",
            "conv_w", "conv_b", "lru_gate_w", "lru_gate_b", "lru_lambda", "swa_q_g", "swa_k_g", "swa_sink", "group_g",
            "w_out", "norm2_g", "w_ff1", "w_ff2")
_SHARDED_SMALL = ("conv_w", "lru_gate_b", "lru_lambda")
_REPL_RAW = ("norm1_g", "q_a_g", "kv_a_g", "mla_q_g", "mla_k_g", "conv_b", "swa_q_g", "swa_k_g",
             "swa_sink", "group_g", "norm2_g")
MOD_ROWS = 32


def kernel(x, c, ctx, c_ctx, w_mod, b_mod, norm1_g, w_in, q_a_g, w_uq, kv_a_g, w_ukv, mla_q_g, mla_k_g, conv_w, conv_b, lru_gate_w, lru_gate_b, lru_lambda, swa_q_g, swa_k_g, swa_sink, group_g, w_out, norm2_g, w_ff1, w_ff2, loss_target, m_c_ctx, m_w_mod, m_b_mod, m_norm1_g, m_w_in, m_q_a_g, m_w_uq, m_kv_a_g, m_w_ukv, m_mla_q_g, m_mla_k_g, m_conv_w, m_conv_b, m_lru_gate_w, m_lru_gate_b, m_lru_lambda, m_swa_q_g, m_swa_k_g, m_swa_sink, m_group_g, m_w_out, m_norm2_g, m_w_ff1, m_w_ff2, v_c_ctx, v_w_mod, v_b_mod, v_norm1_g, v_w_in, v_q_a_g, v_w_uq, v_kv_a_g, v_w_ukv, v_mla_q_g, v_mla_k_g, v_conv_w, v_conv_b, v_lru_gate_w, v_lru_gate_b, v_lru_lambda, v_swa_q_g, v_swa_k_g, v_swa_sink, v_group_g, v_w_out, v_norm2_g, v_w_ff1, v_w_ff2):
    wts = dict(c_ctx=c_ctx, w_mod=w_mod, b_mod=b_mod, norm1_g=norm1_g, w_in=w_in, q_a_g=q_a_g, w_uq=w_uq,
               kv_a_g=kv_a_g, w_ukv=w_ukv, mla_q_g=mla_q_g, mla_k_g=mla_k_g, conv_w=conv_w, conv_b=conv_b,
               lru_gate_w=lru_gate_w, lru_gate_b=lru_gate_b, lru_lambda=lru_lambda, swa_q_g=swa_q_g, swa_k_g=swa_k_g,
               swa_sink=swa_sink, group_g=group_g, w_out=w_out, norm2_g=norm2_g, w_ff1=w_ff1, w_ff2=w_ff2)
    mom1 = dict(zip(_WEIGHTS, (m_c_ctx, m_w_mod, m_b_mod, m_norm1_g, m_w_in, m_q_a_g, m_w_uq, m_kv_a_g, m_w_ukv,
                               m_mla_q_g, m_mla_k_g, m_conv_w, m_conv_b, m_lru_gate_w, m_lru_gate_b, m_lru_lambda,
                               m_swa_q_g, m_swa_k_g, m_swa_sink, m_group_g, m_w_out, m_norm2_g, m_w_ff1, m_w_ff2)))
    mom2 = dict(zip(_WEIGHTS, (v_c_ctx, v_w_mod, v_b_mod, v_norm1_g, v_w_in, v_q_a_g, v_w_uq, v_kv_a_g, v_w_ukv,
                               v_mla_q_g, v_mla_k_g, v_conv_w, v_conv_b, v_lru_gate_w, v_lru_gate_b, v_lru_lambda,
                               v_swa_q_g, v_swa_k_g, v_swa_sink, v_group_g, v_w_out, v_norm2_g, v_w_ff1, v_w_ff2)))
    bsz = x.shape[0]
    n_ex = bsz * N_DEV
    me = _dev_index(_mesh_pos())
    mod_cols = w_mod.shape[-1]

    small_shapes = [c.shape, conv_w.shape, lru_gate_b.shape, lru_lambda.shape]
    shard = lambda n, li: wts[n][li].astype(BF16)
    g_small, *early_pieces = _exchange([_pack([c, conv_w, lru_gate_b, lru_lambda], F32)] + [shard(n, 0) for n in _EARLY],
                                       True, "ag_first")
    c_all, conv_w_all, gate_b_all, lam_all = _unpack(g_small, small_shapes, lead=(N_DEV,))
    c_all = c_all.reshape(n_ex, D_MODEL)
    cat_last = lambda a: jnp.moveaxis(a, 0, -2).reshape(a.shape[1:-1] + (N_DEV * a.shape[-1],))
    conv_w_full, gate_b_full, lam_full = cat_last(conv_w_all), cat_last(gate_b_all), cat_last(lam_all)

    act = jnp.zeros((MOD_ROWS, D_MODEL), F32).at[:n_ex].set(_silu(c_all)).at[n_ex].set(_silu(c_ctx))
    mod_part = jnp.concatenate([_mm(act, w_mod[li], "nn", F32, "mm_mod_l%d" % li) for li in range(DEPTH)], axis=1)
    (mod_all,) = _exchange([mod_part], True, "ag_mod")
    mods = []
    for li in range(DEPTH):
        full = jnp.moveaxis(mod_all[:, :, li * mod_cols:(li + 1) * mod_cols], 0, 1).reshape(MOD_ROWS, -1) + b_mod[li]
        mine = lax.dynamic_slice_in_dim(full, me * bsz, bsz, axis=0)
        ctx_row = jnp.broadcast_to(full[n_ex], mine.shape)
        both = jnp.stack([ctx_row, mine], axis=1).reshape(bsz, 2, N_MOD, 1, D_MODEL)
        mods.append([both[:, :, j] for j in range(N_MOD)])

    raw = {n: wts[n] for n in _REPL_RAW}
    raw.update(conv_w=conv_w_full, lru_gate_b=gate_b_full, lru_lambda=lam_full)
    small_names = list(_REPL_RAW) + list(_SHARDED_SMALL)
    sp, small_vjp, gates_vjp = [None] * DEPTH, [None] * DEPTH, [None] * DEPTH
    for li in range(DEPTH):
        sp[li], small_vjp[li] = jax.vjp(_prep_small, {n: raw[n][li] for n in small_names})
        sp[li]["wbd"], gates_vjp[li] = jax.vjp(_prep_gates, lru_gate_w[li])

    w, w_vjp, g_recv, small_recv = [{} for _ in range(DEPTH)], {}, {}, {}

    def take(li, names, pieces):
        for n, piece in zip(names, pieces):
            out, w_vjp[n, li] = jax.vjp(functools.partial(_prep_weight, n), piece)
            w[li].update(out)

    def gather_hook(li, names):
        return (lambda _: _Exchange([shard(n, li) for n in names], True), lambda got: take(li, names, got))

    def wgrad(n, li, dwl):
        if n == "w_ff1":
            return dwl["ff1"]
        (g,) = w_vjp[n, li]({k: dwl[k].astype(BF16) for k in _BIG[n][2]})
        return g

    def small_pack(li, ds_l, extra=()):
        (d_raw,) = small_vjp[li]({k: v for k, v in ds_l.items() if k != "wbd"})
        return _pack([d_raw[n] for n in small_names] + list(extra), F32)

    def gates_grad(li, ds_l):
        return gates_vjp[li](ds_l["wbd"])[0].reshape(-1, LANE)

    take(0, _EARLY, early_pieces)
    hooks_fwd = [{"mla_fwd": gather_hook(0, _LATE), "swa_fwd": gather_hook(1, _EARLY + ("w_out",))},
                 {"mla_fwd": gather_hook(1, ("w_ff1", "w_ff2"))}]
    bwd_state = {}

    def scatter_last_layer(grads_so_far):
        dwl, dsl = grads_so_far
        return _Exchange([wgrad(n, 1, dwl) for n in _LATE] + [gates_grad(1, dsl)], [False] * len(_LATE) + [True])

    def scatter_first_layer(grads_so_far):
        dwl, dsl = grads_so_far
        dw1, ds1 = bwd_state["dw1"], bwd_state["ds1"]
        bufs = [wgrad(n, 1, dw1) for n in _EARLY] + [wgrad(n, 0, dwl) for n in _LATE]
        bufs += [small_pack(1, ds1), gates_grad(0, dsl)]
        return _Exchange(bufs, [False] * (len(_EARLY) + len(_LATE)) + [True] * 2)

    def scattered_first_layer(got):
        g_recv.update(zip([(n, 1) for n in _EARLY] + [(n, 0) for n in _LATE], got[:-2]))
        small_recv[1], g_recv["lru_gate_w", 0] = got[-2:]

    def scattered_last_layer(got):
        g_recv.update(zip([(n, 1) for n in _LATE], got[:-1]))
        g_recv["lru_gate_w", 1] = got[-1]

    hooks_bwd = [{"mla_bwd": (scatter_first_layer, scattered_first_layer)},
                 {"mla_bwd": (scatter_last_layer, scattered_last_layer)}]

    tc, lat = ctx.shape[1], x.shape[1]
    tabs = {"mla": _rope_tables(lat, tc, MLA_ROPE, MLA_NOPE), "swa": _rope_tables(lat, tc, SWA_HEAD_DIM, 0)}
    stream = jnp.concatenate([ctx, x], axis=1)
    bwds = []
    for li in range(DEPTH):
        stream, bwd = _layer(li, stream, mods[li], w[li], sp[li], tabs, tc, li < DEPTH - 1, hooks_fwd[li], li == 0)
        bwds.append(bwd)
    loss_part, dstream = _loss_and_grad(stream, loss_target, tc)
    dmods = [None] * DEPTH
    dstream, dmods[1], bwd_state["dw1"], bwd_state["ds1"] = bwds[1](dstream, hooks_bwd[1])
    grad_x, dmods[0], dw0, ds0 = bwds[0](dstream, hooks_bwd[0])

    dm_rows = []
    for li in range(DEPTH):
        dm = jnp.concatenate(dmods[li], axis=-1)
        dm_rows.append(jnp.concatenate([dm[:, 1, 0], jnp.sum(dm[:, 0, 0], axis=0, keepdims=True)], axis=0))
    dm_mine = jnp.concatenate(dm_rows, axis=1)
    dm_mine = jnp.pad(dm_mine, ((0, SUBLANE - bsz - 1), (0, 0)))
    (dm_all,) = _exchange([dm_mine], True, "ag_dmod")
    g_wmod, g_bmod, dact_ctx = [], [], jnp.zeros((D_MODEL,), F32)
    for li in range(DEPTH):
        part = dm_all[:, :, li * N_MOD * D_MODEL:(li + 1) * N_MOD * D_MODEL]
        dm32 = jnp.zeros((MOD_ROWS, N_MOD * D_MODEL), F32).at[:n_ex].set(part[:, :bsz].reshape(n_ex, -1))
        dm32 = dm32.at[n_ex].set(jnp.sum(part[:, bsz], axis=0))
        g_bmod.append(jnp.sum(dm32, axis=0))
        cols = lax.dynamic_slice_in_dim(dm32, me * mod_cols, mod_cols, axis=1)
        g_wmod.append(_mm(act, cols, "tn", F32, "mm_mod_dw_l%d" % li))
        dact_ctx = dact_ctx + _mm(cols, w_mod[li], "nt", F32, "mm_mod_dx_l%d" % li)[n_ex]
    sg = jax.nn.sigmoid(c_ctx)
    g_cctx_part = dact_ctx * (sg * (1.0 + c_ctx * (1.0 - sg)))

    last = _exchange([wgrad(n, 0, dw0) for n in _EARLY] + [small_pack(0, ds0, (g_cctx_part, loss_part.reshape(1)))],
                     [False] * len(_EARLY) + [True], "rs_early")
    g_recv.update(zip([(n, 0) for n in _EARLY], last[:-1]))
    small_recv[0] = last[-1]
    layer_shapes = [raw[n].shape[1:] for n in small_names]
    tot = [_unpack(_sum_sources(small_recv[li], "sum_grads_l%d" % li), layer_shapes + [(D_MODEL,), (1,)][:2 * (li == 0)])
           for li in range(DEPTH)]
    grads = {n: jnp.stack([tot[li][j] for li in range(DEPTH)], axis=0) for j, n in enumerate(small_names)}
    grads["c_ctx"], loss = tot[0][-2], tot[0][-1][0]
    for n in _SHARDED_SMALL:
        width = wts[n].shape[-1]
        grads[n] = lax.dynamic_slice_in_dim(grads[n], me * width, width, axis=grads[n].ndim - 1)
    grads["b_mod"] = jnp.stack(g_bmod, axis=0)

    delta, new_m, new_v = {}, {}, {}
    per_layer = {n: [g_recv[n, li] for li in range(DEPTH)] for n in list(_BIG) + ["lru_gate_w"]}
    per_layer["w_mod"] = [g[None] for g in g_wmod]
    for n, srcs in per_layer.items():
        two_d = (DEPTH * math.prod(wts[n].shape[1:-1]), wts[n].shape[-1])
        srcs = [s.reshape((s.shape[0], two_d[0] // DEPTH, two_d[1])) for s in srcs]
        res = _adamw(srcs, wts[n].reshape(two_d), mom1[n].reshape(two_d), mom2[n].reshape(two_d), "adamw_" + n)
        grads[n], delta[n], new_m[n], new_v[n] = [r.reshape(wts[n].shape) for r in res]
    rest = [n for n in _WEIGHTS if n not in delta]
    shapes = [wts[n].shape for n in rest]
    res = _adamw([_pack([grads[n] for n in rest], F32)[None]], _pack([wts[n] for n in rest], F32),
                 _pack([mom1[n] for n in rest], F32), _pack([mom2[n] for n in rest], F32), "adamw_small")
    for tgt, buf in zip((delta, new_m, new_v), res[1:]):
        tgt.update(zip(rest, _unpack(buf, shapes)))

    return (loss, grad_x, *[grads[n] for n in _WEIGHTS], *[delta[n] for n in _WEIGHTS],
            *[new_m[n] for n in _WEIGHTS], *[new_v[n] for n in _WEIGHTS])
```

```python
import functools
import math

import jax
import jax.numpy as jnp
import numpy as np
from jax import lax
from jax.experimental import pallas as pl
from jax.experimental.pallas import tpu as pltpu

F32, BF16 = jnp.float32, jnp.bfloat16

N_DEV = 8
DEPTH = 2
D_MODEL = 1024
D_FF = 4096
N_MOD = 6
GRID_W = 64
WINDOW = 128
ROPE_THETA = 10000.0
EPS = 1e-6
NEG_INF = -1e30
LRU_C = 8.0
LRU_WIDTH = 512
MLA_HEADS, MLA_NOPE, MLA_ROPE, MLA_V = 8, 64, 32, 64
MLA_QK = MLA_NOPE + MLA_ROPE
MLA_Q_RANK, MLA_KV_RANK = 256, 128
SWA_HEADS, SWA_KV_HEADS, SWA_GROUP, SWA_HEAD_DIM = 8, 2, 4, 64
GROUP_WIDTH = 512
IN_SIZES = (256, 128, 32, 512, 512, 512, 128, 128)
IN_WIDTH = sum(IN_SIZES)
ADAM_LR, ADAM_B1, ADAM_B2, ADAM_EPS, ADAM_WD, ADAM_STEP = 0.001, 0.9, 0.999, 1e-08, 0.01, 10

LANE = 128
SUBLANE = 8
TB = 256
QB_SWA = 256
PACK_W = 1024
MM_K_MAX = 4608
MM_ROWS, MM_COLS_MAX = 512, 1024
MM_TOKEN_ROWS = 1152
ELEMWISE_ROWS_MAX = 256
MLA_HPS = 2
VMEM_LIMIT = 56 * 1024 * 1024
P_WIDTH = 3072
PC_SQ, PC_LX, PC_LG, PC_CQ, PC_SK, PC_SV, PC_CKV, PC_KR = 0, 1024, 1536, 2048, 2304, 2560, 2816, 2944
MIX_P = 1536


def _pcall(body, **kw):
    return pl.pallas_call(body, **kw)


def _cparams(n_grid):
    return pltpu.CompilerParams(dimension_semantics=("arbitrary",) * n_grid, vmem_limit_bytes=VMEM_LIMIT)


def _dg(a, b, ca, cb):
    return lax.dot_general(a.astype(BF16), b.astype(BF16), (((ca,), (cb,)), ((), ())),
                           preferred_element_type=F32)


@jax.custom_vjp
def _nn(a, b):
    return _dg(a, b, 1, 0)


@jax.custom_vjp
def _nt(a, b):
    return _dg(a, b, 1, 1)


@jax.custom_vjp
def _tn(a, b):
    return _dg(a, b, 0, 0)


_nn.defvjp(lambda a, b: (_nn(a, b), (a, b)), lambda r, ct: (_nt(ct, r[1]), _tn(r[0], ct)))
_nt.defvjp(lambda a, b: (_nt(a, b), (a, b)), lambda r, ct: (_nn(ct, r[1]), _tn(ct, r[0])))
_tn.defvjp(lambda a, b: (_tn(a, b), (a, b)), lambda r, ct: (_nt(r[1], ct), _nn(r[0], ct)))


@functools.partial(jax.custom_vjp, nondiff_argnums=(1, 2))
def _roll(x, shift, axis):
    return pltpu.roll(x, shift % x.shape[axis], axis)


_roll.defvjp(lambda x, shift, axis: (_roll(x, shift, axis), None),
             lambda shift, axis, _, ct: (_roll(ct, -shift, axis),))


@functools.partial(jax.custom_vjp, nondiff_argnums=(1, 2))
def _split(x, n, axis):
    w = x.shape[axis] // n
    return tuple(lax.slice_in_dim(x, i * w, (i + 1) * w, axis=axis) for i in range(n))


_split.defvjp(lambda x, n, axis: (_split(x, n, axis), None),
              lambda n, axis, _, cts: (jnp.concatenate(cts, axis=axis),))


@jax.custom_vjp
def _unstack(x):
    return tuple(x[i] for i in range(x.shape[0]))


_unstack.defvjp(lambda x: (_unstack(x), None), lambda _, cts: (jnp.stack(cts, axis=0),))


def _sig(x):
    return 0.5 * (jnp.tanh(0.5 * x) + 1.0)


def _gelu(x):
    return 0.5 * x * (1.0 + jnp.tanh(math.sqrt(2.0 / math.pi) * (x + 0.044715 * (x * x * x))))


def _rms(x, g, n):
    ms = jnp.sum(x * x, axis=-1, keepdims=True) * (1.0 / n)
    return x * lax.rsqrt(ms + EPS) * g


def _rope(y, cos, sa, sb, quarter):
    return y * cos + _roll(y, -quarter, 1) * sa + _roll(y, quarter, 1) * sb


def _softmax_rows(s, extra=None):
    m = jnp.max(s, axis=-1, keepdims=True)
    if extra is not None:
        m = jnp.maximum(m, extra)
    m = lax.stop_gradient(m)
    e = jnp.exp(s - m)
    den = jnp.sum(e, axis=-1, keepdims=True)
    if extra is not None:
        den = den + jnp.exp(extra - m)
    return e / den


class _A:
    def __init__(self, arr, block, imap, kind="row", first=None, gdtype=F32, gshape=None, gimap=None):
        self.arr, self.block, self.imap, self.kind, self.first = arr, block, imap, kind, first
        self.gdtype, self.gshape, self.gimap = gdtype, gshape, gimap


def _all_zero(*ids):
    return functools.reduce(jnp.logical_and, [i == 0 for i in ids])


def _par(arr):
    nd = arr.ndim
    return _A(arr, arr.shape, lambda *ids: (0,) * nd, "acc", first=_all_zero)


def _op_fwd(name, fn, grid, args, outs):
    n_in = len(args)

    def body(*refs):
        vals = [r[...].astype(F32) for r in refs[:n_in]]
        for r, v in zip(refs[n_in:], fn(*vals)):
            r[...] = v.astype(r.dtype)

    return _pcall(
        body, name=name, grid=grid,
        in_specs=[pl.BlockSpec(a.block, a.imap) for a in args],
        out_specs=[pl.BlockSpec(o[2], o[3]) for o in outs],
        out_shape=[jax.ShapeDtypeStruct(o[0], o[1]) for o in outs],
        compiler_params=_cparams(len(grid)),
    )(*[a.arr for a in args])


def _op_bwd(name, fn, grid, args, outs, ct_arrays, add_to_first=None):
    didx = [i for i, a in enumerate(args) if a.kind not in ("const", "fwd")]
    read = [i for i, a in enumerate(args) if a.kind != "fwd"]
    n_in, n_ct = len(read), len(outs)
    n_add = 0 if add_to_first is None else 1

    def body(*refs):
        ids = [pl.program_id(i) for i in range(len(grid))]
        vals = [jnp.zeros([d for d in a.block if d is not None], F32) for a in args]
        for i, r in zip(read, refs[:n_in]):
            vals[i] = r[...].astype(F32)

        def g(*dv):
            full = list(vals)
            for i, v in zip(didx, dv):
                full[i] = v
            return tuple(fn(*full))

        _, vjp = jax.vjp(g, *[vals[i] for i in didx])
        grads = list(vjp(tuple(r[...].astype(F32) for r in refs[n_in:n_in + n_ct])))
        if n_add:
            grads[0] = grads[0] + refs[n_in + n_ct][...]
        for gr, i, r in zip(grads, didx, refs[n_in + n_ct + n_add:]):
            a = args[i]
            if a.kind == "row":
                r[...] = gr.astype(r.dtype)
            else:
                first = a.first(*ids)

                @pl.when(first)
                def _():
                    r[...] = gr

                @pl.when(jnp.logical_not(first))
                def _():
                    r[...] += gr

    g_specs, g_shapes = [], []
    for i in didx:
        a = args[i]
        if a.kind == "row":
            g_specs.append(pl.BlockSpec(a.block, a.gimap or a.imap))
            g_shapes.append(jax.ShapeDtypeStruct(a.gshape or a.arr.shape, a.gdtype))
        else:
            g_specs.append(pl.BlockSpec(a.block, a.imap))
            g_shapes.append(jax.ShapeDtypeStruct(a.arr.shape, F32))
    return _pcall(
        body, name=name, grid=grid,
        in_specs=[pl.BlockSpec(args[i].block, args[i].imap) for i in read] + [pl.BlockSpec(o[2], o[3]) for o in outs]
        + [pl.BlockSpec(args[didx[0]].block, args[didx[0]].imap)] * n_add,
        out_specs=g_specs, out_shape=g_shapes,
        compiler_params=_cparams(len(grid)),
    )(*[args[i].arr for i in read], *ct_arrays, *([add_to_first] if n_add else []))


def _rowop(name, fn, grid, args, outs):
    res = _op_fwd(name, fn, grid, args, outs)
    return res, lambda *cts, add_to_first=None: _op_bwd(name + "_bwd", fn, grid, args, outs, cts, add_to_first)


def _pick(n, cap):
    best = None
    for t in range(LANE, cap + 1, LANE):
        if n % t == 0:
            best = t
    return best or n


def _mm(a, b, mode, out_dtype, name, epi=None, aux=None, out_split=None):
    if mode == "nn":
        (m, k), n = a.shape, b.shape[1]
    elif mode == "nt":
        (m, k), n = a.shape, b.shape[0]
    else:
        (k, m), n = a.shape, b.shape[1]
    assert k <= MM_K_MAX
    rows = MM_ROWS if mode == "tn" else MM_TOKEN_ROWS
    tm = rows if m % rows == 0 else m
    tn = n // out_split if out_split else _pick(n, MM_COLS_MAX)
    a_spec = pl.BlockSpec((k, tm), lambda j, i: (0, i)) if mode == "tn" else pl.BlockSpec((tm, k), lambda j, i: (i, 0))
    b_spec = pl.BlockSpec((tn, k), lambda j, i: (j, 0)) if mode == "nt" else pl.BlockSpec((k, tn), lambda j, i: (0, j))
    dims = {"nn": (1, 0), "nt": (1, 1), "tn": (0, 0)}[mode]
    aux_spec = pl.BlockSpec((tm, tn), lambda j, i: (i, j))
    if out_split:
        o_spec, o_shape = pl.BlockSpec((None, tm, tn), lambda j, i: (j, i, 0)), (out_split, m, tn)
    else:
        o_spec, o_shape = aux_spec, (m, n)
    n_aux = 0 if aux is None else 1
    n_out = 2 if epi == "sqrelu" else 1

    def body(*refs):
        o_refs = refs[2 + n_aux:]
        r = _dg(refs[0][...], refs[1][...], *dims)
        if epi == "sqrelu":
            o_refs[0][...] = r.astype(o_refs[0].dtype)
            rl = jnp.maximum(r, 0.0)
            o_refs[1][...] = (rl * rl).astype(o_refs[1].dtype)
        elif epi == "dsqrelu":
            pre = refs[2][...].astype(F32)
            o_refs[0][...] = (r * (2.0 * jnp.maximum(pre, 0.0))).astype(o_refs[0].dtype)
        else:
            o_refs[0][...] = r.astype(o_refs[0].dtype)

    res = _pcall(
        body, name=name, grid=(n // tn, m // tm),
        in_specs=[a_spec, b_spec] + [aux_spec] * n_aux, out_specs=[o_spec] * n_out,
        out_shape=[jax.ShapeDtypeStruct(o_shape, out_dtype)] * n_out, compiler_params=_cparams(2),
    )(a, b, *([aux] if aux is not None else []))
    return res if n_out == 2 else res[0]


ROW_CHUNK = 16


def _softmax_chunks(s_scr, n_keys, scale, emit):
    for r0 in range(0, s_scr.shape[0], ROW_CHUNK):
        rows = slice(r0, r0 + ROW_CHUNK)
        s = s_scr[rows, :n_keys]
        e = jnp.exp((s - jnp.max(s, axis=-1, keepdims=True)) * scale)
        emit(rows, e, 1.0 / jnp.sum(e, axis=-1, keepdims=True))


def _attn_fwd_block(v, n, scale, s_scr, e_scr, l_scr):
    def emit(rows, e, inv_l):
        e_scr[rows, :n] = e.astype(BF16)
        l_scr[rows, :] = jnp.broadcast_to(inv_l, (ROW_CHUNK, LANE))

    _softmax_chunks(s_scr, n, scale, emit)
    return _dg(e_scr[:, :n], v, 1, 0) * l_scr[...]


def _attn_bwd_block(q, k, o, do, scale, s_scr, dp_scr, p_scr, ds_scr):
    n = k.shape[0]

    def emit(rows, e, inv_l):
        p = e * inv_l
        delta = jnp.sum(do[rows, :] * o[rows, :], axis=-1, keepdims=True)
        p_scr[rows, :n] = p.astype(BF16)
        ds_scr[rows, :n] = (p * (dp_scr[rows, :n] - delta) * scale).astype(BF16)

    _softmax_chunks(s_scr, n, scale, emit)
    ds = ds_scr[:, :n]
    return _dg(ds, k, 1, 0), _dg(ds, q, 0, 0), _dg(p_scr[:, :n], do, 0, 0)


def _call_with_exchange(body, xchg, *, name, grid, in_specs, out_specs, out_shape, operands, scratch_shapes=()):
    if xchg is None:
        res = _pcall(body, name=name, grid=grid, in_specs=in_specs, out_specs=out_specs, out_shape=out_shape,
                     scratch_shapes=list(scratch_shapes), compiler_params=_cparams(len(grid)))(*operands)
        return list(res), []
    n_in, n_out, n_sc, n = len(in_specs), len(out_specs), len(scratch_shapes), xchg.n

    def wrapped(*refs):
        ins, x_refs = refs[:n_in], refs[n_in:n_in + n]
        outs, xo_refs = refs[n_in + n:n_in + n + n_out], refs[n_in + n + n_out:n_in + 2 * n + n_out]
        scratch, sems = refs[n_in + 2 * n + n_out:n_in + 2 * n + n_out + n_sc], refs[n_in + 2 * n + n_out + n_sc:]
        ids = [pl.program_id(i) for i in range(len(grid))]

        @pl.when(functools.reduce(jnp.logical_and, [i == 0 for i in ids]))
        def _():
            xchg.start(x_refs, xo_refs, sems)

        body(*ins, *outs, *scratch)

        @pl.when(functools.reduce(jnp.logical_and, [i == g - 1 for i, g in zip(ids, grid)]))
        def _():
            xchg.wait(x_refs, xo_refs, sems)

    res = _pcall(wrapped, name=name, grid=grid, in_specs=list(in_specs) + xchg.specs,
                 out_specs=list(out_specs) + xchg.specs, out_shape=list(out_shape) + xchg.out_shape,
                 scratch_shapes=list(scratch_shapes) + xchg.scratch, compiler_params=_cparams(len(grid)),
                 )(*operands, *xchg.bufs)
    return list(res[:n_out]), list(res[n_out:])


def _head_half(i, shape):
    lane = lax.broadcasted_iota(jnp.int32, shape, len(shape) - 1)
    return (lane < LANE // 2) if i == 0 else (lane >= LANE // 2)


def _mla_attn(q, k, v, tc, ctx_q, name, xchg=None):
    assert MLA_HPS == 2 and MLA_V == LANE // 2
    bsz, t_all, _ = q.shape
    n_t = t_all // TB
    grid = (bsz, MLA_HEADS // MLA_HPS, n_t)
    q_spec = pl.BlockSpec((None, TB, MLA_HPS * LANE), lambda b, h, t: (b, t, h))
    k_spec = pl.BlockSpec((None, t_all, MLA_HPS * LANE), lambda b, h, t: (b, 0, h))
    v_spec = pl.BlockSpec((None, t_all, LANE), lambda b, h, t: (b, 0, h))
    o_spec = pl.BlockSpec((None, TB, LANE), lambda b, h, t: (b, t, h))
    heads = [slice(i * LANE, (i + 1) * LANE) for i in range(MLA_HPS)]
    scale = MLA_QK ** -0.5
    f32_scr, bf16_scr = pltpu.VMEM((TB, t_all), F32), pltpu.VMEM((TB, t_all), BF16)
    o_shape = jax.ShapeDtypeStruct(v.shape, F32)

    def fwd_body(q_ref, k_ref, v_ref, o_ref, *scr):
        t = pl.program_id(2)

        def run(keys):
            n = keys.stop
            for i, hs in enumerate(heads):
                scr[3 * i][:, :n] = _dg(q_ref[:, hs], k_ref[keys, hs], 1, 1)
            both = [_attn_fwd_block(v_ref[keys, :], n, scale, *scr[3 * i:3 * i + 3]) for i in range(MLA_HPS)]
            o_ref[...] = jnp.where(_head_half(0, both[0].shape), both[0], both[1])

        @pl.when(t == 0)
        def _():
            if ctx_q:
                run(slice(0, tc))
            else:
                o_ref[...] = jnp.zeros_like(o_ref)

        @pl.when(t > 0)
        def _():
            run(slice(0, t_all))

    (o,), gathered = _call_with_exchange(
        fwd_body, xchg, name=name, grid=grid, in_specs=[q_spec, k_spec, v_spec], out_specs=[o_spec],
        out_shape=[o_shape], operands=(q, k, v),
        scratch_shapes=[f32_scr, bf16_scr, pltpu.VMEM((TB, LANE), F32)] * MLA_HPS)

    def bwd(do, xchg=None):
        def bwd_body(q_ref, k_ref, v_ref, o_ref, do_ref, dq_ref, dk_ref, dv_ref, *scr):
            t = pl.program_id(2)

            def run(keys, first):
                n = keys.stop
                dos = [jnp.where(_head_half(i, do_ref.shape), do_ref[...], 0.0) for i in range(MLA_HPS)]
                for i, hs in enumerate(heads):
                    scr[4 * i][:, :n] = _dg(q_ref[:, hs], k_ref[keys, hs], 1, 1)
                    scr[4 * i + 1][:, :n] = _dg(dos[i], v_ref[keys, :], 1, 1)
                dvs = []
                for i, hs in enumerate(heads):
                    dq, dk, dv = _attn_bwd_block(q_ref[:, hs], k_ref[keys, hs], o_ref[...], dos[i], scale,
                                                 *scr[4 * i:4 * i + 4])
                    dq_ref[:, hs] = dq
                    dvs.append(dv)
                    if first:
                        dk_ref[keys, hs] = dk
                    else:
                        dk_ref[keys, hs] += dk
                if first:
                    dv_ref[keys, :] = dvs[0] + dvs[1]
                else:
                    dv_ref[keys, :] += dvs[0] + dvs[1]

            @pl.when(t == 0)
            def _():
                dk_ref[...] = jnp.zeros_like(dk_ref)
                dv_ref[...] = jnp.zeros_like(dv_ref)
                if ctx_q:
                    run(slice(0, tc), True)
                else:
                    dq_ref[...] = jnp.zeros_like(dq_ref)

            @pl.when(t > 0)
            def _():
                run(slice(0, t_all), False)

        return _call_with_exchange(
            bwd_body, xchg, name=name + "_bwd", grid=grid, in_specs=[q_spec, k_spec, v_spec, o_spec, o_spec],
            out_specs=[q_spec, k_spec, v_spec],
            out_shape=[jax.ShapeDtypeStruct(q.shape, F32), jax.ShapeDtypeStruct(q.shape, F32), o_shape],
            operands=(q, k, v, o, do), scratch_shapes=[f32_scr, f32_scr, bf16_scr, bf16_scr] * MLA_HPS)

    return o, gathered, bwd


def _swa_block(q, keys, vals, sink, mask):
    qs = jnp.concatenate(list(_split(q, SWA_GROUP, 1)), axis=0)
    sk = jnp.sum(sink, axis=-1, keepdims=True) * (1.0 / LANE)
    s = _nt(qs, keys) * (SWA_HEAD_DIM ** -0.5)
    if mask is not None:
        s = jnp.where(mask, s, NEG_INF)
    o = _split(_nn(_softmax_rows(s, sk), vals + _roll(vals, LANE // 2, 1)), SWA_GROUP, 0)
    low = _head_half(0, o[0].shape)
    return jnp.concatenate([jnp.where(low, o[0], o[1]), jnp.where(low, o[2], o[3])], axis=1)


def _swa_ctx_block(q, kc, vc, sink):
    return _swa_block(q, kc, vc, sink, None)


def _swa_win_block(q, kc, kw, vc, vw, sink, mask):
    return _swa_block(q, jnp.concatenate([kc, kw], axis=0), jnp.concatenate([vc, vw], axis=0), sink, mask)


def _swa_attn(q, k, p_all, sink_b, tc, ctx_q, name, xchg=None):
    bsz, t_all, _ = q.shape
    n_q = t_all // QB_SWA
    n_cq = tc // QB_SWA
    lat = t_all - tc
    span = QB_SWA + 2 * WINDOW
    gw = SWA_GROUP * LANE
    grid = (bsz, SWA_KV_HEADS, n_q)
    q_spec = pl.BlockSpec((None, QB_SWA, gw), lambda b, g, i: (b, i, g))
    k_spec = pl.BlockSpec((None, t_all, LANE), lambda b, g, i: (b, 0, g))
    v_spec = pl.BlockSpec((None, t_all, LANE), lambda b, g, i: (b, 0, PC_SV // LANE + g))
    s_spec = pl.BlockSpec((None, SWA_GROUP * QB_SWA, LANE), lambda b, g, i: (g, 0, 0))

    def window(i):
        q0 = (i - n_cq) * QB_SWA
        w0 = jnp.clip(q0 - WINDOW, 0, lat - span)
        w0 = pl.multiple_of(w0, WINDOW)
        shape = (SWA_GROUP * QB_SWA, tc + span)
        qi = q0 + lax.broadcasted_iota(jnp.int32, shape, 0) % QB_SWA
        col = lax.broadcasted_iota(jnp.int32, shape, 1)
        kj = w0 + col - tc
        mask = (col < tc) | ((kj >= qi - WINDOW) & (kj <= qi + WINDOW))
        return w0, mask

    def fwd_body(q_ref, k_ref, v_ref, s_ref, o_ref):
        i = pl.program_id(2)

        @pl.when(i < n_cq)
        def _():
            if ctx_q:
                o_ref[...] = _swa_ctx_block(q_ref[...].astype(F32), k_ref[0:tc, :], v_ref[0:tc, :].astype(F32),
                                            s_ref[...])
            else:
                o_ref[...] = jnp.zeros_like(o_ref)

        @pl.when(i >= n_cq)
        def _():
            w0, mask = window(i)
            o_ref[...] = _swa_win_block(q_ref[...].astype(F32), k_ref[0:tc, :], k_ref[pl.ds(tc + w0, span), :],
                                        v_ref[0:tc, :].astype(F32), v_ref[pl.ds(tc + w0, span), :].astype(F32),
                                        s_ref[...], mask)

    o_spec = pl.BlockSpec((None, QB_SWA, SWA_GROUP * SWA_HEAD_DIM), lambda b, g, i: (b, i, g))
    (o,), gathered = _call_with_exchange(
        fwd_body, xchg, name=name, grid=grid, in_specs=[q_spec, k_spec, v_spec, s_spec], out_specs=[o_spec],
        out_shape=[jax.ShapeDtypeStruct((bsz, t_all, SWA_HEADS * SWA_HEAD_DIM), F32)],
        operands=(q, k, p_all, sink_b))

    def bwd(do, xchg=None):
        def bwd_body(q_ref, k_ref, v_ref, s_ref, do_ref, dq_ref, dk_ref, dv_ref, ds_ref):
            i = pl.program_id(2)

            @pl.when(i == 0)
            def _():
                dk_ref[...] = jnp.zeros_like(dk_ref)
                dv_ref[...] = jnp.zeros_like(dv_ref)
                ds_ref[...] = jnp.zeros_like(ds_ref)

            @pl.when(i < n_cq)
            def _():
                if ctx_q:
                    _, vjp = jax.vjp(_swa_ctx_block, q_ref[...].astype(F32), k_ref[0:tc, :].astype(F32),
                                     v_ref[0:tc, :].astype(F32), s_ref[...])
                    dq, dk, dv, ds = vjp(do_ref[...])
                    dq_ref[...] = dq
                    dk_ref[0:tc, :] += dk
                    dv_ref[0:tc, :] += dv
                    ds_ref[...] += ds
                else:
                    dq_ref[...] = jnp.zeros_like(dq_ref)

            @pl.when(i >= n_cq)
            def _():
                w0, mask = window(i)
                win = pl.ds(tc + w0, span)
                _, vjp = jax.vjp(functools.partial(_swa_win_block, mask=mask), q_ref[...].astype(F32),
                                 k_ref[0:tc, :].astype(F32), k_ref[win, :].astype(F32),
                                 v_ref[0:tc, :].astype(F32), v_ref[win, :].astype(F32), s_ref[...])
                dq, dkc, dkw, dvc, dvw, ds = vjp(do_ref[...])
                dq_ref[...] = dq
                dk_ref[0:tc, :] += dkc
                dk_ref[win, :] += dkw
                dv_ref[0:tc, :] += dvc
                dv_ref[win, :] += dvw
                ds_ref[...] += ds

        kv_out = pl.BlockSpec((None, t_all, LANE), lambda b, g, i: (b, 0, g))
        ds_spec = pl.BlockSpec((None, None, SWA_GROUP * QB_SWA, LANE), lambda b, g, i: (b, g, 0, 0))
        kv_shape = jax.ShapeDtypeStruct((bsz, t_all, SWA_KV_HEADS * LANE), F32)
        return _call_with_exchange(
            bwd_body, xchg, name=name + "_bwd", grid=grid, in_specs=[q_spec, k_spec, v_spec, s_spec, o_spec],
            out_specs=[q_spec, kv_out, kv_out, ds_spec],
            out_shape=[jax.ShapeDtypeStruct(q.shape, F32), kv_shape, kv_shape,
                       jax.ShapeDtypeStruct((bsz,) + sink_b.shape, F32)],
            operands=(q, k, p_all, sink_b, do))

    return o, gathered, bwd


def _scan_pair(chains, scratch):
    t_all, c = chains[0][0].shape
    n_tiles = t_all // SUBLANE
    row8 = lax.broadcasted_iota(jnp.int32, (t_all, c), 0) % SUBLANE
    refs = [scratch[0:3], scratch[3:6]]
    for (a, u, reverse), (a_s, u_s, _) in zip(chains, refs):
        for d in (1, 2, 4):
            sh = d if not reverse else t_all - d
            ar, ur = pltpu.roll(a, sh, 0), pltpu.roll(u, sh, 0)
            m = (row8 >= d) if not reverse else (row8 < SUBLANE - d)
            u = jnp.where(m, a * ur + u, u)
            a = jnp.where(m, a * ar, a)
        a_s[...] = a
        u_s[...] = u

    def step(j, carries):
        out = []
        for (_, _, reverse), (a_s, u_s, c_s), carry in zip(chains, refs, carries):
            tile = j if not reverse else n_tiles - 1 - j
            base = pl.multiple_of(tile * SUBLANE, SUBLANE)
            c_s[pl.ds(base, SUBLANE), :] = jnp.broadcast_to(carry, (SUBLANE, c))
            last = base + (0 if reverse else SUBLANE - 1)
            out.append(a_s[pl.ds(last, 1), :] * carry + u_s[pl.ds(last, 1), :])
        return tuple(out)

    lax.fori_loop(0, n_tiles, step, (jnp.zeros((1, c), F32),) * 2, unroll=4)
    return [a_s[...] * c_s[...] + u_s[...] for a_s, u_s, c_s in refs]


def _shift_rows(x, reverse_src):
    t_all = x.shape[0]
    row = lax.broadcasted_iota(jnp.int32, x.shape, 0)
    if reverse_src:
        return jnp.where(row == t_all - 1, 0.0, pltpu.roll(x, t_all - 1, 0))
    return jnp.where(row == 0, 0.0, pltpu.roll(x, 1, 0))


def _lru_scan(a0, u0, a1, u1, name):
    bsz, t_all, w = a0.shape
    grid = (bsz, w // LANE)
    spec = pl.BlockSpec((None, t_all, LANE), lambda b, c: (b, 0, c))
    scratch = [pltpu.VMEM((t_all, LANE), F32)] * 6
    shape = jax.ShapeDtypeStruct(a0.shape, F32)

    def fwd_body(a0_ref, u0_ref, a1_ref, u1_ref, h0_ref, h1_ref, *scr):
        h0_ref[...], h1_ref[...] = _scan_pair([(a0_ref[...], u0_ref[...], False), (a1_ref[...], u1_ref[...], True)],
                                              scr)

    h0, h1 = _pcall(fwd_body, name=name, grid=grid, in_specs=[spec] * 4, out_specs=[spec] * 2,
                    out_shape=[shape] * 2, scratch_shapes=scratch, compiler_params=_cparams(2))(a0, u0, a1, u1)

    def bwd(dh0, dh1):
        def bwd_body(a0_ref, h0_ref, g0_ref, a1_ref, h1_ref, g1_ref, da0_ref, du0_ref, da1_ref, du1_ref, *scr):
            g0, g1 = _scan_pair([(_shift_rows(a0_ref[...], True), g0_ref[...], True),
                                 (_shift_rows(a1_ref[...], False), g1_ref[...], False)], scr)
            du0_ref[...] = g0
            da0_ref[...] = g0 * _shift_rows(h0_ref[...], False)
            du1_ref[...] = g1
            da1_ref[...] = g1 * _shift_rows(h1_ref[...], True)

        return _pcall(bwd_body, name=name + "_bwd", grid=grid, in_specs=[spec] * 6, out_specs=[spec] * 4,
                      out_shape=[shape] * 4, scratch_shapes=scratch,
                      compiler_params=_cparams(2))(a0, h0, dh0, a1, h1, dh1)

    return h0, h1, bwd


def _f_mod(x, g, shift, scale):
    return (_rms(x, g, D_MODEL) * (1.0 + scale) + shift,)


def _f_mla_q(cq, ga, w, gh, cos, sa, sb):
    n = _rms(cq, ga, MLA_Q_RANK)
    outs = []
    for wh in _split(w, MLA_HEADS, 1):
        outs.append(_rope(_rms(_nn(n, wh), gh, MLA_QK), cos, sa, sb, MLA_ROPE // 4))
    return (jnp.concatenate(outs, axis=1),)


def _f_mla_kv(ckv, krp, ga, wk, wv, gh, cos, sa, sb):
    n = _rms(ckv, ga, MLA_KV_RANK)
    outs = []
    for wh in _split(wk, MLA_HEADS, 1):
        outs.append(_rope(_rms(_nn(n, wh) + krp, gh, MLA_QK), cos, sa, sb, MLA_ROPE // 4))
    return jnp.concatenate(outs, axis=1), _nn(n, wv)


def _f_conv(x, w0, w1, w2, w3, bias, tc):
    t_all = x.shape[0]
    row = lax.broadcasted_iota(jnp.int32, x.shape, 0)
    lo = jnp.where(row < tc, 0, tc)
    hi = jnp.where(row < tc, tc, t_all)
    y = bias + jnp.zeros_like(x)
    for kk, wk in enumerate((w0, w1, w2, w3)):
        src = row + (kk - 2)
        xs = x if kk == 2 else _roll(x, 2 - kk, 0)
        y = y + wk * jnp.where((src >= lo) & (src < hi), xs, 0.0)
    return (y,)


def _f_gates(xc, w16, b00, b01, b10, b11, sp0, sp1):
    ws = _unstack(w16)
    n_cb = LRU_WIDTH // LANE
    xcs = _split(xc, n_cb, 1)
    bias = [_split(b, n_cb, 1) for b in (b00, b01, b10, b11)]
    sps = [_split(s, n_cb, 1) for s in (sp0, sp1)]
    res = [[], [], [], []]
    for c in range(n_cb):
        for z in range(2):
            r = _sig(_nn(xcs[c], ws[c * 4 + 2 * z]) + bias[2 * z][c])
            i = _sig(_nn(xcs[c], ws[c * 4 + 2 * z + 1]) + bias[2 * z + 1][c])
            la = -LRU_C * r * sps[z][c]
            res[2 * z].append(jnp.exp(la))
            res[2 * z + 1].append(jnp.sqrt(-jnp.tanh(la) * (jnp.exp(2.0 * la) + 1.0)) * (i * xcs[c]))
    return tuple(jnp.concatenate(r, axis=1) for r in res)


def _f_swa_qk(sq, sk, gq, gk, cos, sa, sb):
    qs = [_rope(_rms(x, gq, SWA_HEAD_DIM), cos, sa, sb, SWA_HEAD_DIM // 4) for x in _split(sq, SWA_HEADS, 1)]
    ks = [_rope(_rms(x, gk, SWA_HEAD_DIM), cos, sa, sb, SWA_HEAD_DIM // 4) for x in _split(sk, SWA_KV_HEADS, 1)]
    return jnp.concatenate(qs, axis=1), jnp.concatenate(ks, axis=1)


def _f_qkv(cq, ckv, krp, sq, sk, q_a_g, wuq, mla_q_g, kv_a_g, wk, wv, mla_k_g, swa_q_g, swa_k_g,
           m_cos, m_sa, m_sb, s_cos, s_sa, s_sb):
    return (*_f_mla_q(cq, q_a_g, wuq, mla_q_g, m_cos, m_sa, m_sb),
            *_f_mla_kv(ckv, krp, kv_a_g, wk, wv, mla_k_g, m_cos, m_sa, m_sb),
            *_f_swa_qk(sq, sk, swa_q_g, swa_k_g, s_cos, s_sa, s_sb))


def _f_merge(oa, h0, h1, lg, oc, ga, gb, gc):
    ob = (h0 + h1) * _gelu(lg)
    return (jnp.concatenate([_rms(oa, ga, GROUP_WIDTH), _rms(ob, gb, GROUP_WIDTH), _rms(oc, gc, GROUP_WIDTH)],
                            axis=1),)


def _f_resid_mod(x, y, gate, g, shift, scale):
    x1 = x + gate * y
    return x1, _rms(x1, g, D_MODEL) * (1.0 + scale) + shift


def _f_resid(x, y, gate):
    return (x + gate * y,)


def _hosted(hooks, key, arg=None):
    make, done = hooks.get(key, (None, None))
    xchg = make(arg) if make is not None else None
    return xchg, (done if xchg is not None else lambda outs: None)


def _layer(li, x, mods, w, s, tabs, tc, ctx_q, hooks, latent_dx_only):
    bsz, t_all, _ = x.shape
    n_t = t_all // TB
    grid = (bsz, n_t)
    rows = lambda b, t: (b, t, 0)

    def row(arr, width=None, idx=0, gdtype=F32, gshape=None):
        width = width or arr.shape[-1]
        return _A(arr, (None, TB, width), lambda b, t: (b, t, idx), "row", gdtype=gdtype, gshape=gshape,
                  gimap=rows if gshape is not None else None)

    def out(width, dtype, imap=rows):
        return ((bsz, t_all, width), dtype, (None, TB, width), imap)

    def modarg(arr):
        return _A(arr, (None, None, 1, D_MODEL), lambda b, t: (b, jnp.minimum(t, 1), 0, 0), "acc",
                  first=lambda b, t: t <= 1)

    def tab(arr):
        return _A(arr, (TB, LANE), lambda b, t: (t, 0), "const")

    def pcol(p_all, col, width):
        return row(p_all, width, col // width, gdtype=BF16, gshape=(bsz, t_all, width))

    nm = lambda base: "%s_l%d" % (base, li)
    sh1, sc1, g1, sh2, sc2, g2 = mods
    m_all = bsz * t_all

    x_arg = row(x)
    if latent_dx_only:
        n_c = tc // TB
        x_arg.gshape, x_arg.gimap = (bsz, t_all - tc, D_MODEL), lambda b, t: (b, jnp.maximum(t - n_c, 0), 0)
    (h,), b_mod1 = _rowop(nm("mod1"), _f_mod, grid, [x_arg, _par(s["norm1_g"]), modarg(sh1), modarg(sc1)],
                          [out(D_MODEL, BF16)])
    p_all = _mm(h.reshape(m_all, D_MODEL), w["win"], "nn", BF16, nm("mm_in")).reshape(bsz, t_all, P_WIDTH)

    (q_a, k_a, v_a, q_c, k_c), b_qkv = _rowop(
        nm("qkv"), _f_qkv, grid,
        [pcol(p_all, PC_CQ, 256), pcol(p_all, PC_CKV, 128), pcol(p_all, PC_KR, 128), pcol(p_all, PC_SQ, 1024),
         pcol(p_all, PC_SK, 256)]
        + [_par(a) for a in (s["q_a_g"], w["wuq"], s["mla_q_g"], s["kv_a_g"], w["wk"], w["wv"], s["mla_k_g"],
                             s["swa_q_g"], s["swa_k_g"])]
        + [tab(a) for a in tabs["mla"] + tabs["swa"]],
        [out(MLA_HEADS * LANE, BF16), out(MLA_HEADS * LANE, BF16), out(MLA_HEADS * MLA_V, BF16),
         out(SWA_HEADS * LANE, BF16), out(SWA_KV_HEADS * LANE, BF16)])

    xchg, done = _hosted(hooks, "mla_fwd")
    o_a, got, b_attn_a = _mla_attn(q_a, k_a, v_a, tc, ctx_q, nm("mla_attn"), xchg)
    done(got)

    n_cb = LRU_WIDTH // LANE
    conv_grid = (n_cb, bsz)
    cpar = lambda arr: _A(arr, (1, LANE), lambda c, b: (0, c), "acc", first=lambda c, b: b == 0)
    conv_args = [_A(p_all, (None, t_all, LANE), lambda c, b: (b, 0, PC_LX // LANE + c), "row", gdtype=BF16,
                    gshape=(bsz, t_all, LRU_WIDTH), gimap=lambda c, b: (b, 0, c))]
    conv_args += [cpar(a) for a in s["conv_w"]] + [cpar(s["conv_b"])]
    conv_out = [((bsz, t_all, LRU_WIDTH), F32, (None, t_all, LANE), lambda c, b: (b, 0, c))]
    (xc,), b_conv = _rowop(nm("lru_conv"), functools.partial(_f_conv, tc=tc), conv_grid, conv_args, conv_out)
    rot = lambda b, t: (b, (t + n_t - 1) % n_t, 0)
    (a0, u0, a1, u1), b_gates = _rowop(
        nm("lru_gates"), _f_gates, grid,
        [row(xc), _par(s["wbd"])] + [_par(a) for a in s["gate_b"]] + [_par(a) for a in s["sp"]],
        [out(LRU_WIDTH, F32), out(LRU_WIDTH, F32), out(LRU_WIDTH, F32, rot), out(LRU_WIDTH, F32, rot)])
    h0, h1, b_scan = _lru_scan(a0, u0, a1, u1, nm("lru_scan"))
    h1_arg = _A(h1, (None, TB, LRU_WIDTH), rot, "row")

    xchg, done = _hosted(hooks, "swa_fwd")
    o_c, got, b_attn_c = _swa_attn(q_c, k_c, p_all, s["sink_b"], tc, ctx_q, nm("swa_attn"), xchg)
    done(got)

    (y_in,), b_merge = _rowop(nm("merge"), _f_merge, grid,
                              [row(o_a), row(h0), h1_arg, pcol(p_all, PC_LG, 512), row(o_c), _par(s["g_a"]),
                               _par(s["g_b"]), _par(s["g_c"])],
                              [out(MIX_P, BF16)])
    y = _mm(y_in.reshape(m_all, MIX_P), w["wout"], "nn", F32, nm("mm_out")).reshape(bsz, t_all, D_MODEL)
    (x1, hm), b_rm = _rowop(nm("resid_mod"), _f_resid_mod, grid,
                            [row(x), row(y, gdtype=BF16), modarg(g1), _par(s["norm2_g"]), modarg(sh2), modarg(sc2)],
                            [out(D_MODEL, F32), out(D_MODEL, BF16)])
    pre, act = _mm(hm.reshape(m_all, D_MODEL), w["ff1"], "nn", BF16, nm("mm_ff1"), epi="sqrelu")
    y2 = _mm(act, w["ff2"], "nn", F32, nm("mm_ff2")).reshape(bsz, t_all, D_MODEL)
    (x2,), b_res = _rowop(nm("resid"), _f_resid, grid,
                          [_A(x1, (None, TB, D_MODEL), rows, "fwd"), row(y2, gdtype=BF16), modarg(g2)],
                          [out(D_MODEL, F32)])

    def bwd(dx2, hooks):
        dw, ds = {}, {}
        dy2, dg2 = b_res(dx2)
        dy2 = dy2.reshape(m_all, D_MODEL)
        dpre = _mm(dy2, w["ff2"], "nt", BF16, nm("mm_ff2_dx"), epi="dsqrelu", aux=pre)
        dw["ff2"] = _mm(act, dy2, "tn", BF16, nm("mm_ff2_dw"))
        dhm = _mm(dpre, w["ff1"], "nt", F32, nm("mm_ff1_dx")).reshape(bsz, t_all, D_MODEL)
        dw["ff1"] = _mm(hm.reshape(m_all, D_MODEL), dpre, "tn", BF16, nm("mm_ff1_dw"), out_split=N_DEV)
        dxa, dy, dg1, ds["norm2_g"], dsh2, dsc2 = b_rm(dx2, dhm)
        dy = dy.reshape(m_all, D_MODEL)
        dy_in = _mm(dy, w["wout"], "nt", F32, nm("mm_out_dx")).reshape(bsz, t_all, MIX_P)
        dw["wout"] = _mm(y_in.reshape(m_all, MIX_P), dy, "tn", BF16, nm("mm_out_dw"))
        do_a, dh0, dh1, dlg, do_c, ds["g_a"], ds["g_b"], ds["g_c"] = b_merge(dy_in)

        (dq_c, dk_c, dsv, dsink), _ = b_attn_c(do_c)
        ds["sink_b"] = jnp.sum(dsink, axis=0)

        da0, du0, da1, du1 = b_scan(dh0, dh1)
        gates_g = b_gates(da0, du0, da1, du1)
        dxc, ds["wbd"] = gates_g[0], gates_g[1]
        ds["gate_b"], ds["sp"] = list(gates_g[2:6]), list(gates_g[6:8])
        conv_g = b_conv(dxc)
        dlx, ds["conv_w"], ds["conv_b"] = conv_g[0], list(conv_g[1:5]), conv_g[5]

        xchg, done = _hosted(hooks, "mla_bwd", (dw, ds))
        (dq_a, dk_a, dv_a), got = b_attn_a(do_a, xchg)
        done(got)
        (dcq, dckv, dkr, dsq, dsk, ds["q_a_g"], dw["wuq"], ds["mla_q_g"], ds["kv_a_g"], dw["wk"], dw["wv"],
         ds["mla_k_g"], ds["swa_q_g"], ds["swa_k_g"]) = b_qkv(dq_a, dk_a, dv_a, dq_c, dk_c)

        dp = jnp.concatenate([dsq, dlx, dlg, dcq, dsk, dsv.astype(BF16), dckv, dkr], axis=-1)
        dp = dp.reshape(m_all, P_WIDTH)
        dh = _mm(dp, w["win"], "nt", F32, nm("mm_in_dx")).reshape(bsz, t_all, D_MODEL)
        dw["win"] = _mm(h.reshape(m_all, D_MODEL), dp, "tn", BF16, nm("mm_in_dw"))
        dx, ds["norm1_g"], dsh1, dsc1 = b_mod1(dh, add_to_first=dxa)
        return dx, [dsh1, dsc1, dg1, dsh2, dsc2, dg2], dw, ds

    return x2, bwd


def _loss_and_grad(x2, target, tc):
    bsz, t_all, d = x2.shape
    n_t = t_all // TB
    n_c = tc // TB

    def body(x_ref, t_ref, l_ref, dx_ref):
        b, t = pl.program_id(0), pl.program_id(1)

        @pl.when((b == 0) & (t == 0))
        def _():
            l_ref[...] = jnp.zeros_like(l_ref)

        @pl.when(t < n_c)
        def _():
            dx_ref[...] = jnp.zeros_like(dx_ref)

        @pl.when(t >= n_c)
        def _():
            e = x_ref[...] - t_ref[...]
            dx_ref[...] = e * (1.0 / d)
            l_ref[...] += jnp.sum(e * e) * (0.5 / d)

    loss, dx = _pcall(
        body, name="loss", grid=(bsz, n_t),
        in_specs=[pl.BlockSpec((None, TB, d), lambda b, t: (b, t, 0)),
                  pl.BlockSpec((None, TB, d), lambda b, t: (b, jnp.maximum(t - n_c, 0), 0))],
        out_specs=[pl.BlockSpec((SUBLANE, LANE), lambda b, t: (0, 0)),
                   pl.BlockSpec((None, TB, d), lambda b, t: (b, t, 0))],
        out_shape=[jax.ShapeDtypeStruct((SUBLANE, LANE), F32), jax.ShapeDtypeStruct(x2.shape, F32)],
        compiler_params=_cparams(2))(x2, target)
    return loss[0, 0], dx


def _rope_tables(lat, tc, dim, lane0):
    quarter = dim // 4
    pos = np.arange(lat)
    grid_pos = np.stack([pos // GRID_W, pos % GRID_W], axis=-1).astype(np.float32)
    lane = np.arange(LANE)
    p = np.clip(lane - lane0, 0, dim - 1)
    active = (lane >= lane0) & (lane < lane0 + dim)
    axis, half, qi = p // (dim // 2), (p % (dim // 2)) // quarter, p % quarter
    inv = (np.float32(ROPE_THETA) ** (-qi.astype(np.float32) / np.float32(quarter))).astype(np.float32)
    ang = (np.where(axis[None, :] == 0, grid_pos[:, 0:1], grid_pos[:, 1:2]) * inv[None, :]).astype(np.float32)
    cos = np.where(active, np.cos(ang), 1.0).astype(np.float32)
    sin = np.where(active, np.sin(ang), 0.0).astype(np.float32)
    sa = np.where(half == 0, -sin, 0.0).astype(np.float32)
    sb = np.where(half == 1, sin, 0.0).astype(np.float32)
    ctx1, ctx0 = np.ones((tc, LANE), np.float32), np.zeros((tc, LANE), np.float32)
    return tuple(jnp.asarray(np.concatenate([c, t], 0)) for c, t in ((ctx1, cos), (ctx0, sa), (ctx0, sb)))


_BIG = {"w_in": ((D_MODEL, IN_WIDTH // N_DEV), 1, ("win",)),
        "w_uq": ((MLA_Q_RANK, MLA_HEADS * MLA_QK // N_DEV), 1, ("wuq",)),
        "w_ukv": ((MLA_KV_RANK, MLA_HEADS * (MLA_NOPE + MLA_V) // N_DEV), 1, ("wk", "wv")),
        "w_out": ((3 * GROUP_WIDTH // N_DEV, D_MODEL), 0, ("wout",)),
        "w_ff1": ((D_MODEL, D_FF // N_DEV), 1, ("ff1",)),
        "w_ff2": ((D_FF // N_DEV, D_MODEL), 0, ("ff2",))}
_EARLY = ("w_in", "w_uq", "w_ukv")
_LATE = ("w_out", "w_ff1", "w_ff2")


def _pad_heads(wm, n_heads, dim):
    out = jnp.pad(wm.reshape(wm.shape[0], n_heads, dim), ((0, 0), (0, 0), (0, LANE - dim)))
    return out.reshape(wm.shape[0], n_heads * LANE)


def _prep_weight(name, piece):
    shp, ax, _ = _BIG[name]
    full = jnp.moveaxis(piece, 0, ax).reshape(shp[:ax] + (N_DEV * shp[ax],) + shp[ax + 1:])
    if name == "w_in":
        cq, ckv, kr, lx, lg, sq, sk, sv = _split_cols(full)
        return {"win": jnp.concatenate(
            [_pad_heads(sq, SWA_HEADS, SWA_HEAD_DIM), lx, lg, cq, _pad_heads(sk, SWA_KV_HEADS, SWA_HEAD_DIM),
             _pad_heads(sv, SWA_KV_HEADS, SWA_HEAD_DIM), ckv, jnp.pad(kr, ((0, 0), (MLA_NOPE, LANE - MLA_QK)))], axis=1)}
    if name == "w_uq":
        return {"wuq": _pad_heads(full, MLA_HEADS, MLA_QK)}
    if name == "w_ukv":
        ukv = full.reshape(MLA_KV_RANK, MLA_HEADS, MLA_NOPE + MLA_V)
        return {"wk": _pad_heads(ukv[:, :, :MLA_NOPE].reshape(MLA_KV_RANK, -1), MLA_HEADS, MLA_NOPE),
                "wv": ukv[:, :, MLA_NOPE:].reshape(MLA_KV_RANK, -1)}
    return {_BIG[name][2][0]: full}


def _split_cols(wm):
    parts, start = [], 0
    for size in IN_SIZES:
        parts.append(wm[:, start:start + size])
        start += size
    return parts


def _prep_gates(gate_w):
    gw = gate_w.reshape(2, 2, 4, 2, 64, 64)
    wbd = jnp.einsum("zgknCm,nN->knCzgNm", gw, jnp.eye(2, dtype=F32)).reshape(4, LANE, 4, LANE)
    return wbd.transpose(0, 2, 1, 3).reshape(16, LANE, LANE)


def _prep_small(raw):
    r1 = lambda a: a.reshape(1, -1)
    gg = raw["group_g"]
    sink = raw["swa_sink"].reshape(SWA_KV_HEADS, SWA_GROUP, 1, 1)
    return {
        "norm1_g": r1(raw["norm1_g"]), "norm2_g": r1(raw["norm2_g"]),
        "q_a_g": r1(raw["q_a_g"]), "kv_a_g": r1(raw["kv_a_g"]),
        "mla_q_g": jnp.pad(r1(raw["mla_q_g"]), ((0, 0), (0, LANE - MLA_QK))),
        "mla_k_g": jnp.pad(r1(raw["mla_k_g"]), ((0, 0), (0, LANE - MLA_QK))),
        "swa_q_g": jnp.pad(r1(raw["swa_q_g"]), ((0, 0), (0, LANE - SWA_HEAD_DIM))),
        "swa_k_g": jnp.pad(r1(raw["swa_k_g"]), ((0, 0), (0, LANE - SWA_HEAD_DIM))),
        "conv_w": [r1(raw["conv_w"][kk]) for kk in range(4)], "conv_b": r1(raw["conv_b"]),
        "gate_b": [r1(raw["lru_gate_b"][z, g]) for z in range(2) for g in range(2)],
        "sp": [r1(jax.nn.softplus(-raw["lru_lambda"][z])) for z in range(2)],
        "sink_b": jnp.broadcast_to(sink, (SWA_KV_HEADS, SWA_GROUP, QB_SWA, LANE)).reshape(
            SWA_KV_HEADS, SWA_GROUP * QB_SWA, LANE),
        "g_a": r1(gg[:GROUP_WIDTH]), "g_b": r1(gg[GROUP_WIDTH:2 * GROUP_WIDTH]), "g_c": r1(gg[2 * GROUP_WIDTH:])}


def _mesh_pos():
    return lax.axis_index("x"), lax.axis_index("y"), lax.axis_index("c")


def _peer(pos, k):
    return tuple(1 - p if (k >> s) & 1 else p for p, s in zip(pos, (2, 1, 0)))


def _dev_index(pos):
    return 4 * pos[0] + 2 * pos[1] + pos[2]


class _Exchange:
    def __init__(self, bufs, gather):
        self.bufs = list(bufs)
        self.n = len(self.bufs)
        self.gather = [gather] * self.n if isinstance(gather, bool) else list(gather)
        self.specs = [pl.BlockSpec(memory_space=pl.ANY)] * self.n
        self.out_shape = [jax.ShapeDtypeStruct((N_DEV,) + tuple(b.shape if g else b.shape[1:]), b.dtype)
                          for b, g in zip(self.bufs, self.gather)]
        self.scratch = [pltpu.SemaphoreType.DMA(((N_DEV - 1) * self.n,)),
                        pltpu.SemaphoreType.DMA(((N_DEV - 1) * self.n,)), pltpu.SemaphoreType.DMA((self.n,))]

    def _copies(self, x_refs, o_refs, sems, with_recvs):
        send_sems, recv_sems, local_sems = sems
        pos = _mesh_pos()
        me = _dev_index(pos)
        locals_, sends, recvs = [], [], []
        for j in range(self.n):
            src_mine = x_refs[j] if self.gather[j] else x_refs[j].at[me]
            locals_.append(pltpu.make_async_copy(src_mine, o_refs[j].at[me], local_sems.at[j]))
        for k in range(1, N_DEV):
            peer = _peer(pos, k)
            pidx = _dev_index(peer)
            for j in range(self.n):
                src = x_refs[j] if self.gather[j] else x_refs[j].at[pidx]
                sem = (k - 1) * self.n + j
                sends.append(pltpu.make_async_remote_copy(
                    src_ref=src, dst_ref=o_refs[j].at[me], send_sem=send_sems.at[sem], recv_sem=recv_sems.at[sem],
                    device_id=peer, device_id_type=pl.DeviceIdType.MESH))
                if with_recvs:
                    recvs.append(pltpu.make_async_remote_copy(
                        src_ref=src, dst_ref=o_refs[j].at[pidx], send_sem=send_sems.at[sem],
                        recv_sem=recv_sems.at[sem], device_id=peer, device_id_type=pl.DeviceIdType.MESH))
        return locals_, sends, recvs

    def start(self, x_refs, o_refs, sems):
        locals_, sends, _ = self._copies(x_refs, o_refs, sems, False)
        for cp in locals_ + sends:
            cp.start()

    def wait(self, x_refs, o_refs, sems):
        locals_, sends, recvs = self._copies(x_refs, o_refs, sems, True)
        for cp in recvs:
            cp.wait_recv()
        for cp in sends:
            cp.wait_send()
        for cp in locals_:
            cp.wait()


def _exchange(bufs, gather, name):
    xchg = _Exchange(bufs, gather)
    n = xchg.n

    def body(*refs):
        xchg.start(refs[:n], refs[n:2 * n], refs[2 * n:])
        xchg.wait(refs[:n], refs[n:2 * n], refs[2 * n:])

    return _pcall(body, name=name, out_shape=xchg.out_shape, in_specs=xchg.specs, out_specs=xchg.specs,
                  scratch_shapes=xchg.scratch)(*xchg.bufs)


def _pack(arrs, dtype):
    flat = jnp.concatenate([a.reshape(-1).astype(dtype) for a in arrs])
    rows = -(-flat.size // PACK_W)
    rows = -(-rows // 16) * 16
    return jnp.pad(flat, (0, rows * PACK_W - flat.size)).reshape(rows, PACK_W)


def _unpack(buf, shapes, lead=()):
    flat = buf.reshape(lead + (-1,))
    out, off = [], 0
    for shp in shapes:
        n = math.prod(shp)
        out.append(flat[..., off:off + n].reshape(lead + tuple(shp)))
        off += n
    return out


def _sum_sources(buf, name):
    _, r, c = buf.shape
    tr = _rows_tile(r)

    def body(x_ref, o_ref):
        acc = x_ref[0]
        for d in range(1, N_DEV):
            acc = acc + x_ref[d]
        o_ref[...] = acc

    return _pcall(body, name=name, grid=(r // tr,),
                  in_specs=[pl.BlockSpec((N_DEV, tr, c), lambda i: (0, i, 0))],
                  out_specs=pl.BlockSpec((tr, c), lambda i: (i, 0)),
                  out_shape=jax.ShapeDtypeStruct((r, c), F32), compiler_params=_cparams(1))(buf)


def _rows_tile(r):
    best = r
    for t in range(SUBLANE, ELEMWISE_ROWS_MAX + 1, SUBLANE):
        if r % t == 0:
            best = t
    return best


def _adamw(grads, wgt, m, v, name):
    n_lay = len(grads)
    n_src, r, c = grads[0].shape
    tr = _rows_tile(r)
    n_blk = r // tr
    bc1 = 1.0 - ADAM_B1 ** ADAM_STEP
    bc2 = 1.0 - ADAM_B2 ** ADAM_STEP

    def body(*refs):
        g_refs, (w_ref, m_ref, v_ref, go_ref, d_ref, mo_ref, vo_ref) = refs[:n_lay], refs[n_lay:]
        for li, g_ref in enumerate(g_refs):
            @pl.when(pl.program_id(0) == li)
            def _():
                g = g_ref[0].astype(F32)
                for d in range(1, n_src):
                    g = g + g_ref[d].astype(F32)
                m_new = ADAM_B1 * m_ref[...] + (1.0 - ADAM_B1) * g
                v_new = ADAM_B2 * v_ref[...] + (1.0 - ADAM_B2) * (g * g)
                go_ref[...] = g
                mo_ref[...] = m_new
                vo_ref[...] = v_new
                d_ref[...] = -ADAM_LR * ((m_new / bc1) / (jnp.sqrt(v_new / bc2) + ADAM_EPS) + ADAM_WD * w_ref[...])

    g_specs = [pl.BlockSpec((n_src, tr, c),
                            lambda l, i, li=li: (0, jnp.where(l == li, i, jnp.where(l > li, n_blk - 1, 0)), 0))
               for li in range(n_lay)]
    spec = pl.BlockSpec((tr, c), lambda l, i: (l * n_blk + i, 0))
    return _pcall(body, name=name, grid=(n_lay, n_blk), in_specs=g_specs + [spec, spec, spec],
                  out_specs=[spec] * 4, out_shape=[jax.ShapeDtypeStruct((n_lay * r, c), F32)] * 4,
                  compiler_params=_cparams(2))(*grads, wgt, m, v)


def _silu(z):
    return z * jax.nn.sigmoid(z)


_WEIGHTS = ("c_ctx", "w_mod", "b_mod", "norm1_g", "w_in", "q_a_g", "w_uq", "kv_a_g", "w_ukv", "mla_q_g", "mla_k_g",
            "conv_w", "conv_b", "lru_gate_w", "lru_gate_b", "lru_lambda", "swa_q_g", "swa_k_g", "swa_sink", "group_g",
            "w_out", "norm2_g", "w_ff1", "w_ff2")
_SHARDED_SMALL = ("conv_w", "lru_gate_b", "lru_lambda")
_REPL_RAW = ("norm1_g", "q_a_g", "kv_a_g", "mla_q_g", "mla_k_g", "conv_b", "swa_q_g", "swa_k_g",
             "swa_sink", "group_g", "norm2_g")
MOD_ROWS = 32


def kernel(x, c, ctx, c_ctx, w_mod, b_mod, norm1_g, w_in, q_a_g, w_uq, kv_a_g, w_ukv, mla_q_g, mla_k_g, conv_w, conv_b, lru_gate_w, lru_gate_b, lru_lambda, swa_q_g, swa_k_g, swa_sink, group_g, w_out, norm2_g, w_ff1, w_ff2, loss_target, m_c_ctx, m_w_mod, m_b_mod, m_norm1_g, m_w_in, m_q_a_g, m_w_uq, m_kv_a_g, m_w_ukv, m_mla_q_g, m_mla_k_g, m_conv_w, m_conv_b, m_lru_gate_w, m_lru_gate_b, m_lru_lambda, m_swa_q_g, m_swa_k_g, m_swa_sink, m_group_g, m_w_out, m_norm2_g, m_w_ff1, m_w_ff2, v_c_ctx, v_w_mod, v_b_mod, v_norm1_g, v_w_in, v_q_a_g, v_w_uq, v_kv_a_g, v_w_ukv, v_mla_q_g, v_mla_k_g, v_conv_w, v_conv_b, v_lru_gate_w, v_lru_gate_b, v_lru_lambda, v_swa_q_g, v_swa_k_g, v_swa_sink, v_group_g, v_w_out, v_norm2_g, v_w_ff1, v_w_ff2):
    wts = dict(c_ctx=c_ctx, w_mod=w_mod, b_mod=b_mod, norm1_g=norm1_g, w_in=w_in, q_a_g=q_a_g, w_uq=w_uq,
               kv_a_g=kv_a_g, w_ukv=w_ukv, mla_q_g=mla_q_g, mla_k_g=mla_k_g, conv_w=conv_w, conv_b=conv_b,
               lru_gate_w=lru_gate_w, lru_gate_b=lru_gate_b, lru_lambda=lru_lambda, swa_q_g=swa_q_g, swa_k_g=swa_k_g,
               swa_sink=swa_sink, group_g=group_g, w_out=w_out, norm2_g=norm2_g, w_ff1=w_ff1, w_ff2=w_ff2)
    mom1 = dict(zip(_WEIGHTS, (m_c_ctx, m_w_mod, m_b_mod, m_norm1_g, m_w_in, m_q_a_g, m_w_uq, m_kv_a_g, m_w_ukv,
                               m_mla_q_g, m_mla_k_g, m_conv_w, m_conv_b, m_lru_gate_w, m_lru_gate_b, m_lru_lambda,
                               m_swa_q_g, m_swa_k_g, m_swa_sink, m_group_g, m_w_out, m_norm2_g, m_w_ff1, m_w_ff2)))
    mom2 = dict(zip(_WEIGHTS, (v_c_ctx, v_w_mod, v_b_mod, v_norm1_g, v_w_in, v_q_a_g, v_w_uq, v_kv_a_g, v_w_ukv,
                               v_mla_q_g, v_mla_k_g, v_conv_w, v_conv_b, v_lru_gate_w, v_lru_gate_b, v_lru_lambda,
                               v_swa_q_g, v_swa_k_g, v_swa_sink, v_group_g, v_w_out, v_norm2_g, v_w_ff1, v_w_ff2)))
    bsz = x.shape[0]
    n_ex = bsz * N_DEV
    me = _dev_index(_mesh_pos())
    mod_cols = w_mod.shape[-1]

    small_shapes = [c.shape, conv_w.shape, lru_gate_b.shape, lru_lambda.shape]
    shard = lambda n, li: wts[n][li].astype(BF16)
    g_small, *early_pieces = _exchange([_pack([c, conv_w, lru_gate_b, lru_lambda], F32)] + [shard(n, 0) for n in _EARLY],
                                       True, "ag_first")
    c_all, conv_w_all, gate_b_all, lam_all = _unpack(g_small, small_shapes, lead=(N_DEV,))
    c_all = c_all.reshape(n_ex, D_MODEL)
    cat_last = lambda a: jnp.moveaxis(a, 0, -2).reshape(a.shape[1:-1] + (N_DEV * a.shape[-1],))
    conv_w_full, gate_b_full, lam_full = cat_last(conv_w_all), cat_last(gate_b_all), cat_last(lam_all)

    act = jnp.zeros((MOD_ROWS, D_MODEL), F32).at[:n_ex].set(_silu(c_all)).at[n_ex].set(_silu(c_ctx))
    mod_part = jnp.concatenate([_mm(act, w_mod[li], "nn", F32, "mm_mod_l%d" % li) for li in range(DEPTH)], axis=1)
    (mod_all,) = _exchange([mod_part], True, "ag_mod")
    mods = []
    for li in range(DEPTH):
        full = jnp.moveaxis(mod_all[:, :, li * mod_cols:(li + 1) * mod_cols], 0, 1).reshape(MOD_ROWS, -1) + b_mod[li]
        mine = lax.dynamic_slice_in_dim(full, me * bsz, bsz, axis=0)
        ctx_row = jnp.broadcast_to(full[n_ex], mine.shape)
        both = jnp.stack([ctx_row, mine], axis=1).reshape(bsz, 2, N_MOD, 1, D_MODEL)
        mods.append([both[:, :, j] for j in range(N_MOD)])

    raw = {n: wts[n] for n in _REPL_RAW}
    raw.update(conv_w=conv_w_full, lru_gate_b=gate_b_full, lru_lambda=lam_full)
    small_names = list(_REPL_RAW) + list(_SHARDED_SMALL)
    sp, small_vjp, gates_vjp = [None] * DEPTH, [None] * DEPTH, [None] * DEPTH
    for li in range(DEPTH):
        sp[li], small_vjp[li] = jax.vjp(_prep_small, {n: raw[n][li] for n in small_names})
        sp[li]["wbd"], gates_vjp[li] = jax.vjp(_prep_gates, lru_gate_w[li])

    w, w_vjp, g_recv, small_recv = [{} for _ in range(DEPTH)], {}, {}, {}

    def take(li, names, pieces):
        for n, piece in zip(names, pieces):
            out, w_vjp[n, li] = jax.vjp(functools.partial(_prep_weight, n), piece)
            w[li].update(out)

    def gather_hook(li, names):
        return (lambda _: _Exchange([shard(n, li) for n in names], True), lambda got: take(li, names, got))

    def wgrad(n, li, dwl):
        if n == "w_ff1":
            return dwl["ff1"]
        (g,) = w_vjp[n, li]({k: dwl[k].astype(BF16) for k in _BIG[n][2]})
        return g

    def small_pack(li, ds_l, extra=()):
        (d_raw,) = small_vjp[li]({k: v for k, v in ds_l.items() if k != "wbd"})
        return _pack([d_raw[n] for n in small_names] + list(extra), F32)

    def gates_grad(li, ds_l):
        return gates_vjp[li](ds_l["wbd"])[0].reshape(-1, LANE)

    take(0, _EARLY, early_pieces)
    hooks_fwd = [{"mla_fwd": gather_hook(0, _LATE), "swa_fwd": gather_hook(1, _EARLY + ("w_out",))},
                 {"mla_fwd": gather_hook(1, ("w_ff1", "w_ff2"))}]
    bwd_state = {}

    def scatter_last_layer(grads_so_far):
        dwl, dsl = grads_so_far
        return _Exchange([wgrad(n, 1, dwl) for n in _LATE] + [gates_grad(1, dsl)], [False] * len(_LATE) + [True])

    def scatter_first_layer(grads_so_far):
        dwl, dsl = grads_so_far
        dw1, ds1 = bwd_state["dw1"], bwd_state["ds1"]
        bufs = [wgrad(n, 1, dw1) for n in _EARLY] + [wgrad(n, 0, dwl) for n in _LATE]
        bufs += [small_pack(1, ds1), gates_grad(0, dsl)]
        return _Exchange(bufs, [False] * (len(_EARLY) + len(_LATE)) + [True] * 2)

    def scattered_first_layer(got):
        g_recv.update(zip([(n, 1) for n in _EARLY] + [(n, 0) for n in _LATE], got[:-2]))
        small_recv[1], g_recv["lru_gate_w", 0] = got[-2:]

    def scattered_last_layer(got):
        g_recv.update(zip([(n, 1) for n in _LATE], got[:-1]))
        g_recv["lru_gate_w", 1] = got[-1]

    hooks_bwd = [{"mla_bwd": (scatter_first_layer, scattered_first_layer)},
                 {"mla_bwd": (scatter_last_layer, scattered_last_layer)}]

    tc, lat = ctx.shape[1], x.shape[1]
    tabs = {"mla": _rope_tables(lat, tc, MLA_ROPE, MLA_NOPE), "swa": _rope_tables(lat, tc, SWA_HEAD_DIM, 0)}
    stream = jnp.concatenate([ctx, x], axis=1)
    bwds = []
    for li in range(DEPTH):
        stream, bwd = _layer(li, stream, mods[li], w[li], sp[li], tabs, tc, li < DEPTH - 1, hooks_fwd[li], li == 0)
        bwds.append(bwd)
    loss_part, dstream = _loss_and_grad(stream, loss_target, tc)
    dmods = [None] * DEPTH
    dstream, dmods[1], bwd_state["dw1"], bwd_state["ds1"] = bwds[1](dstream, hooks_bwd[1])
    grad_x, dmods[0], dw0, ds0 = bwds[0](dstream, hooks_bwd[0])

    dm_rows = []
    for li in range(DEPTH):
        dm = jnp.concatenate(dmods[li], axis=-1)
        dm_rows.append(jnp.concatenate([dm[:, 1, 0], jnp.sum(dm[:, 0, 0], axis=0, keepdims=True)], axis=0))
    dm_mine = jnp.concatenate(dm_rows, axis=1)
    dm_mine = jnp.pad(dm_mine, ((0, SUBLANE - bsz - 1), (0, 0)))
    (dm_all,) = _exchange([dm_mine], True, "ag_dmod")
    g_wmod, g_bmod, dact_ctx = [], [], jnp.zeros((D_MODEL,), F32)
    for li in range(DEPTH):
        part = dm_all[:, :, li * N_MOD * D_MODEL:(li + 1) * N_MOD * D_MODEL]
        dm32 = jnp.zeros((MOD_ROWS, N_MOD * D_MODEL), F32).at[:n_ex].set(part[:, :bsz].reshape(n_ex, -1))
        dm32 = dm32.at[n_ex].set(jnp.sum(part[:, bsz], axis=0))
        g_bmod.append(jnp.sum(dm32, axis=0))
        cols = lax.dynamic_slice_in_dim(dm32, me * mod_cols, mod_cols, axis=1)
        g_wmod.append(_mm(act, cols, "tn", F32, "mm_mod_dw_l%d" % li))
        dact_ctx = dact_ctx + _mm(cols, w_mod[li], "nt", F32, "mm_mod_dx_l%d" % li)[n_ex]
    sg = jax.nn.sigmoid(c_ctx)
    g_cctx_part = dact_ctx * (sg * (1.0 + c_ctx * (1.0 - sg)))

    last = _exchange([wgrad(n, 0, dw0) for n in _EARLY] + [small_pack(0, ds0, (g_cctx_part, loss_part.reshape(1)))],
                     [False] * len(_EARLY) + [True], "rs_early")
    g_recv.update(zip([(n, 0) for n in _EARLY], last[:-1]))
    small_recv[0] = last[-1]
    layer_shapes = [raw[n].shape[1:] for n in small_names]
    tot = [_unpack(_sum_sources(small_recv[li], "sum_grads_l%d" % li), layer_shapes + [(D_MODEL,), (1,)][:2 * (li == 0)])
           for li in range(DEPTH)]
    grads = {n: jnp.stack([tot[li][j] for li in range(DEPTH)], axis=0) for j, n in enumerate(small_names)}
    grads["c_ctx"], loss = tot[0][-2], tot[0][-1][0]
    for n in _SHARDED_SMALL:
        width = wts[n].shape[-1]
        grads[n] = lax.dynamic_slice_in_dim(grads[n], me * width, width, axis=grads[n].ndim - 1)
    grads["b_mod"] = jnp.stack(g_bmod, axis=0)

    delta, new_m, new_v = {}, {}, {}
    per_layer = {n: [g_recv[n, li] for li in range(DEPTH)] for n in list(_BIG) + ["lru_gate_w"]}
    per_layer["w_mod"] = [g[None] for g in g_wmod]
    for n, srcs in per_layer.items():
        two_d = (DEPTH * math.prod(wts[n].shape[1:-1]), wts[n].shape[-1])
        srcs = [s.reshape((s.shape[0], two_d[0] // DEPTH, two_d[1])) for s in srcs]
        res = _adamw(srcs, wts[n].reshape(two_d), mom1[n].reshape(two_d), mom2[n].reshape(two_d), "adamw_" + n)
        grads[n], delta[n], new_m[n], new_v[n] = [r.reshape(wts[n].shape) for r in res]
    rest = [n for n in _WEIGHTS if n not in delta]
    shapes = [wts[n].shape for n in rest]
    res = _adamw([_pack([grads[n] for n in rest], F32)[None]], _pack([wts[n] for n in rest], F32),
                 _pack([mom1[n] for n in rest], F32), _pack([mom2[n] for n in rest], F32), "adamw_small")
    for tgt, buf in zip((delta, new_m, new_v), res[1:]):
        tgt.update(zip(rest, _unpack(buf, shapes)))

    return (loss, grad_x, *[grads[n] for n in _WEIGHTS], *[delta[n] for n in _WEIGHTS],
            *[new_m[n] for n in _WEIGHTS], *[new_v[n] for n in _WEIGHTS])
```

```python
import functools
import math

import jax
import jax.numpy as jnp
import numpy as np
from jax import lax
from jax.experimental import pallas as pl
from jax.experimental.pallas import tpu as pltpu

F32, BF16 = jnp.float32, jnp.bfloat16

N_DEV = 8
DEPTH = 2
D_MODEL = 1024
D_FF = 4096
N_MOD = 6
GRID_W = 64
WINDOW = 128
ROPE_THETA = 10000.0
EPS = 1e-6
NEG_INF = -1e30
LRU_C = 8.0
LRU_WIDTH = 512
MLA_HEADS, MLA_NOPE, MLA_ROPE, MLA_V = 8, 64, 32, 64
MLA_QK = MLA_NOPE + MLA_ROPE
MLA_Q_RANK, MLA_KV_RANK = 256, 128
SWA_HEADS, SWA_KV_HEADS, SWA_GROUP, SWA_HEAD_DIM = 8, 2, 4, 64
GROUP_WIDTH = 512
IN_SIZES = (256, 128, 32, 512, 512, 512, 128, 128)
IN_WIDTH = sum(IN_SIZES)
ADAM_LR, ADAM_B1, ADAM_B2, ADAM_EPS, ADAM_WD, ADAM_STEP = 0.001, 0.9, 0.999, 1e-08, 0.01, 10

LANE = 128
SUBLANE = 8
TB = 256
QB_SWA = 256
PACK_W = 1024
MM_K_MAX = 4608
MM_ROWS, MM_COLS_MAX = 512, 1024
MM_TOKEN_ROWS = 1152
MM_SHORT_K = 1536
ELEMWISE_ROWS_MAX = 256
MLA_HPS = 2
VMEM_LIMIT = 56 * 1024 * 1024
P_WIDTH = 3072
PC_SQ, PC_LX, PC_LG, PC_CQ, PC_SK, PC_SV, PC_CKV, PC_KR = 0, 1024, 1536, 2048, 2304, 2560, 2816, 2944
MIX_P = 1536


def _pcall(body, **kw):
    return pl.pallas_call(body, **kw)


def _cparams(n_grid):
    return pltpu.CompilerParams(dimension_semantics=("arbitrary",) * n_grid, vmem_limit_bytes=VMEM_LIMIT)


def _dg(a, b, ca, cb):
    return lax.dot_general(a.astype(BF16), b.astype(BF16), (((ca,), (cb,)), ((), ())),
                           preferred_element_type=F32)


@jax.custom_vjp
def _nn(a, b):
    return _dg(a, b, 1, 0)


@jax.custom_vjp
def _nt(a, b):
    return _dg(a, b, 1, 1)


@jax.custom_vjp
def _tn(a, b):
    return _dg(a, b, 0, 0)


_nn.defvjp(lambda a, b: (_nn(a, b), (a, b)), lambda r, ct: (_nt(ct, r[1]), _tn(r[0], ct)))
_nt.defvjp(lambda a, b: (_nt(a, b), (a, b)), lambda r, ct: (_nn(ct, r[1]), _tn(ct, r[0])))
_tn.defvjp(lambda a, b: (_tn(a, b), (a, b)), lambda r, ct: (_nt(r[1], ct), _nn(r[0], ct)))


@functools.partial(jax.custom_vjp, nondiff_argnums=(1, 2))
def _roll(x, shift, axis):
    return pltpu.roll(x, shift % x.shape[axis], axis)


_roll.defvjp(lambda x, shift, axis: (_roll(x, shift, axis), None),
             lambda shift, axis, _, ct: (_roll(ct, -shift, axis),))


@functools.partial(jax.custom_vjp, nondiff_argnums=(1, 2))
def _split(x, n, axis):
    w = x.shape[axis] // n
    return tuple(lax.slice_in_dim(x, i * w, (i + 1) * w, axis=axis) for i in range(n))


_split.defvjp(lambda x, n, axis: (_split(x, n, axis), None),
              lambda n, axis, _, cts: (jnp.concatenate(cts, axis=axis),))


@jax.custom_vjp
def _unstack(x):
    return tuple(x[i] for i in range(x.shape[0]))


_unstack.defvjp(lambda x: (_unstack(x), None), lambda _, cts: (jnp.stack(cts, axis=0),))


def _sig(x):
    return 0.5 * (jnp.tanh(0.5 * x) + 1.0)


def _gelu(x):
    return 0.5 * x * (1.0 + jnp.tanh(math.sqrt(2.0 / math.pi) * (x + 0.044715 * (x * x * x))))


def _rms(x, g, n):
    ms = jnp.sum(x * x, axis=-1, keepdims=True) * (1.0 / n)
    return x * lax.rsqrt(ms + EPS) * g


def _rope(y, cos, sa, sb, quarter):
    return y * cos + _roll(y, -quarter, 1) * sa + _roll(y, quarter, 1) * sb


def _softmax_rows(s, extra=None):
    m = jnp.max(s, axis=-1, keepdims=True)
    if extra is not None:
        m = jnp.maximum(m, extra)
    m = lax.stop_gradient(m)
    e = jnp.exp(s - m)
    den = jnp.sum(e, axis=-1, keepdims=True)
    if extra is not None:
        den = den + jnp.exp(extra - m)
    return e / den


class _A:
    def __init__(self, arr, block, imap, kind="row", first=None, gdtype=F32, gshape=None, gimap=None):
        self.arr, self.block, self.imap, self.kind, self.first = arr, block, imap, kind, first
        self.gdtype, self.gshape, self.gimap = gdtype, gshape, gimap


def _all_zero(*ids):
    return functools.reduce(jnp.logical_and, [i == 0 for i in ids])


def _par(arr):
    nd = arr.ndim
    return _A(arr, arr.shape, lambda *ids: (0,) * nd, "acc", first=_all_zero)


def _op_fwd(name, fn, grid, args, outs):
    n_in = len(args)

    def body(*refs):
        vals = [r[...].astype(F32) for r in refs[:n_in]]
        for r, v in zip(refs[n_in:], fn(*vals)):
            r[...] = v.astype(r.dtype)

    return _pcall(
        body, name=name, grid=grid,
        in_specs=[pl.BlockSpec(a.block, a.imap) for a in args],
        out_specs=[pl.BlockSpec(o[2], o[3]) for o in outs],
        out_shape=[jax.ShapeDtypeStruct(o[0], o[1]) for o in outs],
        compiler_params=_cparams(len(grid)),
    )(*[a.arr for a in args])


def _op_bwd(name, fn, grid, args, outs, ct_arrays, add_to_first=None):
    didx = [i for i, a in enumerate(args) if a.kind not in ("const", "fwd")]
    read = [i for i, a in enumerate(args) if a.kind != "fwd"]
    n_in, n_ct = len(read), len(outs)
    n_add = 0 if add_to_first is None else 1

    def body(*refs):
        ids = [pl.program_id(i) for i in range(len(grid))]
        vals = [jnp.zeros([d for d in a.block if d is not None], F32) for a in args]
        for i, r in zip(read, refs[:n_in]):
            vals[i] = r[...].astype(F32)

        def g(*dv):
            full = list(vals)
            for i, v in zip(didx, dv):
                full[i] = v
            return tuple(fn(*full))

        _, vjp = jax.vjp(g, *[vals[i] for i in didx])
        grads = list(vjp(tuple(r[...].astype(F32) for r in refs[n_in:n_in + n_ct])))
        if n_add:
            grads[0] = grads[0] + refs[n_in + n_ct][...]
        for gr, i, r in zip(grads, didx, refs[n_in + n_ct + n_add:]):
            a = args[i]
            if a.kind == "row":
                r[...] = gr.astype(r.dtype)
            else:
                first = a.first(*ids)

                @pl.when(first)
                def _():
                    r[...] = gr

                @pl.when(jnp.logical_not(first))
                def _():
                    r[...] += gr

    g_specs, g_shapes = [], []
    for i in didx:
        a = args[i]
        if a.kind == "row":
            g_specs.append(pl.BlockSpec(a.block, a.gimap or a.imap))
            g_shapes.append(jax.ShapeDtypeStruct(a.gshape or a.arr.shape, a.gdtype))
        else:
            g_specs.append(pl.BlockSpec(a.block, a.imap))
            g_shapes.append(jax.ShapeDtypeStruct(a.arr.shape, F32))
    return _pcall(
        body, name=name, grid=grid,
        in_specs=[pl.BlockSpec(args[i].block, args[i].imap) for i in read] + [pl.BlockSpec(o[2], o[3]) for o in outs]
        + [pl.BlockSpec(args[didx[0]].block, args[didx[0]].imap)] * n_add,
        out_specs=g_specs, out_shape=g_shapes,
        compiler_params=_cparams(len(grid)),
    )(*[args[i].arr for i in read], *ct_arrays, *([add_to_first] if n_add else []))


def _rowop(name, fn, grid, args, outs):
    res = _op_fwd(name, fn, grid, args, outs)
    return res, lambda *cts, add_to_first=None: _op_bwd(name + "_bwd", fn, grid, args, outs, cts, add_to_first)


def _pick(n, cap):
    best = None
    for t in range(LANE, cap + 1, LANE):
        if n % t == 0:
            best = t
    return best or n


def _mm(a, b, mode, out_dtype, name, epi=None, aux=None, out_split=None):
    if mode == "nn":
        (m, k), n = a.shape, b.shape[1]
    elif mode == "nt":
        (m, k), n = a.shape, b.shape[0]
    else:
        (k, m), n = a.shape, b.shape[1]
    assert k <= MM_K_MAX
    rows = MM_ROWS if mode == "tn" else MM_TOKEN_ROWS * (2 if k <= MM_SHORT_K else 1)
    tm = rows if m % rows == 0 else m
    tn = n // out_split if out_split else _pick(n, MM_COLS_MAX)
    a_spec = pl.BlockSpec((k, tm), lambda j, i: (0, i)) if mode == "tn" else pl.BlockSpec((tm, k), lambda j, i: (i, 0))
    b_spec = pl.BlockSpec((tn, k), lambda j, i: (j, 0)) if mode == "nt" else pl.BlockSpec((k, tn), lambda j, i: (0, j))
    dims = {"nn": (1, 0), "nt": (1, 1), "tn": (0, 0)}[mode]
    aux_spec = pl.BlockSpec((tm, tn), lambda j, i: (i, j))
    if out_split:
        o_spec, o_shape = pl.BlockSpec((None, tm, tn), lambda j, i: (j, i, 0)), (out_split, m, tn)
    else:
        o_spec, o_shape = aux_spec, (m, n)
    n_aux = 0 if aux is None else 1
    n_out = 2 if epi == "sqrelu" else 1

    def body(*refs):
        o_refs = refs[2 + n_aux:]
        r = _dg(refs[0][...], refs[1][...], *dims)
        if epi == "sqrelu":
            o_refs[0][...] = r.astype(o_refs[0].dtype)
            rl = jnp.maximum(r, 0.0)
            o_refs[1][...] = (rl * rl).astype(o_refs[1].dtype)
        elif epi == "dsqrelu":
            pre = refs[2][...].astype(F32)
            o_refs[0][...] = (r * (2.0 * jnp.maximum(pre, 0.0))).astype(o_refs[0].dtype)
        else:
            o_refs[0][...] = r.astype(o_refs[0].dtype)

    res = _pcall(
        body, name=name, grid=(n // tn, m // tm),
        in_specs=[a_spec, b_spec] + [aux_spec] * n_aux, out_specs=[o_spec] * n_out,
        out_shape=[jax.ShapeDtypeStruct(o_shape, out_dtype)] * n_out, compiler_params=_cparams(2),
    )(a, b, *([aux] if aux is not None else []))
    return res if n_out == 2 else res[0]


ROW_CHUNK = 16


def _softmax_chunks(s_scr, n_keys, scale, emit):
    for r0 in range(0, s_scr.shape[0], ROW_CHUNK):
        rows = slice(r0, r0 + ROW_CHUNK)
        s = s_scr[rows, :n_keys]
        e = jnp.exp((s - jnp.max(s, axis=-1, keepdims=True)) * scale)
        emit(rows, e, 1.0 / jnp.sum(e, axis=-1, keepdims=True))


def _attn_fwd_block(v, n, scale, s_scr, e_scr, l_scr):
    def emit(rows, e, inv_l):
        e_scr[rows, :n] = e.astype(BF16)
        l_scr[rows, :] = jnp.broadcast_to(inv_l, (ROW_CHUNK, LANE))

    _softmax_chunks(s_scr, n, scale, emit)
    return _dg(e_scr[:, :n], v, 1, 0) * l_scr[...]


def _attn_bwd_block(q, k, o, do, scale, s_scr, dp_scr, p_scr, ds_scr):
    n = k.shape[0]

    def emit(rows, e, inv_l):
        p = e * inv_l
        delta = jnp.sum(do[rows, :] * o[rows, :], axis=-1, keepdims=True)
        p_scr[rows, :n] = p.astype(BF16)
        ds_scr[rows, :n] = (p * (dp_scr[rows, :n] - delta) * scale).astype(BF16)

    _softmax_chunks(s_scr, n, scale, emit)
    ds = ds_scr[:, :n]
    return _dg(ds, k, 1, 0), _dg(ds, q, 0, 0), _dg(p_scr[:, :n], do, 0, 0)


def _call_with_exchange(body, xchg, *, name, grid, in_specs, out_specs, out_shape, operands, scratch_shapes=()):
    if xchg is None:
        res = _pcall(body, name=name, grid=grid, in_specs=in_specs, out_specs=out_specs, out_shape=out_shape,
                     scratch_shapes=list(scratch_shapes), compiler_params=_cparams(len(grid)))(*operands)
        return list(res), []
    n_in, n_out, n_sc, n = len(in_specs), len(out_specs), len(scratch_shapes), xchg.n

    def wrapped(*refs):
        ins, x_refs = refs[:n_in], refs[n_in:n_in + n]
        outs, xo_refs = refs[n_in + n:n_in + n + n_out], refs[n_in + n + n_out:n_in + 2 * n + n_out]
        scratch, sems = refs[n_in + 2 * n + n_out:n_in + 2 * n + n_out + n_sc], refs[n_in + 2 * n + n_out + n_sc:]
        ids = [pl.program_id(i) for i in range(len(grid))]

        @pl.when(functools.reduce(jnp.logical_and, [i == 0 for i in ids]))
        def _():
            xchg.start(x_refs, xo_refs, sems)

        body(*ins, *outs, *scratch)

        @pl.when(functools.reduce(jnp.logical_and, [i == g - 1 for i, g in zip(ids, grid)]))
        def _():
            xchg.wait(x_refs, xo_refs, sems)

    res = _pcall(wrapped, name=name, grid=grid, in_specs=list(in_specs) + xchg.specs,
                 out_specs=list(out_specs) + xchg.specs, out_shape=list(out_shape) + xchg.out_shape,
                 scratch_shapes=list(scratch_shapes) + xchg.scratch, compiler_params=_cparams(len(grid)),
                 )(*operands, *xchg.bufs)
    return list(res[:n_out]), list(res[n_out:])


def _head_half(i, shape):
    lane = lax.broadcasted_iota(jnp.int32, shape, len(shape) - 1)
    return (lane < LANE // 2) if i == 0 else (lane >= LANE // 2)


def _mla_attn(q, k, v, tc, ctx_q, name, xchg=None):
    assert MLA_HPS == 2 and MLA_V == LANE // 2
    bsz, t_all, _ = q.shape
    n_t = t_all // TB
    grid = (bsz, MLA_HEADS // MLA_HPS, n_t)
    q_spec = pl.BlockSpec((None, TB, MLA_HPS * LANE), lambda b, h, t: (b, t, h))
    k_spec = pl.BlockSpec((None, t_all, MLA_HPS * LANE), lambda b, h, t: (b, 0, h))
    v_spec = pl.BlockSpec((None, t_all, LANE), lambda b, h, t: (b, 0, h))
    o_spec = pl.BlockSpec((None, TB, LANE), lambda b, h, t: (b, t, h))
    heads = [slice(i * LANE, (i + 1) * LANE) for i in range(MLA_HPS)]
    scale = MLA_QK ** -0.5
    f32_scr, bf16_scr = pltpu.VMEM((TB, t_all), F32), pltpu.VMEM((TB, t_all), BF16)
    o_shape = jax.ShapeDtypeStruct(v.shape, F32)

    def fwd_body(q_ref, k_ref, v_ref, o_ref, *scr):
        t = pl.program_id(2)

        def run(keys):
            n = keys.stop
            for i, hs in enumerate(heads):
                scr[3 * i][:, :n] = _dg(q_ref[:, hs], k_ref[keys, hs], 1, 1)
            both = [_attn_fwd_block(v_ref[keys, :], n, scale, *scr[3 * i:3 * i + 3]) for i in range(MLA_HPS)]
            o_ref[...] = jnp.where(_head_half(0, both[0].shape), both[0], both[1])

        @pl.when(t == 0)
        def _():
            if ctx_q:
                run(slice(0, tc))
            else:
                o_ref[...] = jnp.zeros_like(o_ref)

        @pl.when(t > 0)
        def _():
            run(slice(0, t_all))

    (o,), gathered = _call_with_exchange(
        fwd_body, xchg, name=name, grid=grid, in_specs=[q_spec, k_spec, v_spec], out_specs=[o_spec],
        out_shape=[o_shape], operands=(q, k, v),
        scratch_shapes=[f32_scr, bf16_scr, pltpu.VMEM((TB, LANE), F32)] * MLA_HPS)

    def bwd(do, xchg=None):
        def bwd_body(q_ref, k_ref, v_ref, o_ref, do_ref, dq_ref, dk_ref, dv_ref, *scr):
            t = pl.program_id(2)

            def run(keys, first):
                n = keys.stop
                dos = [jnp.where(_head_half(i, do_ref.shape), do_ref[...], 0.0) for i in range(MLA_HPS)]
                for i, hs in enumerate(heads):
                    scr[4 * i][:, :n] = _dg(q_ref[:, hs], k_ref[keys, hs], 1, 1)
                    scr[4 * i + 1][:, :n] = _dg(dos[i], v_ref[keys, :], 1, 1)
                dvs = []
                for i, hs in enumerate(heads):
                    dq, dk, dv = _attn_bwd_block(q_ref[:, hs], k_ref[keys, hs], o_ref[...], dos[i], scale,
                                                 *scr[4 * i:4 * i + 4])
                    dq_ref[:, hs] = dq
                    dvs.append(dv)
                    if first:
                        dk_ref[keys, hs] = dk
                    else:
                        dk_ref[keys, hs] += dk
                if first:
                    dv_ref[keys, :] = dvs[0] + dvs[1]
                else:
                    dv_ref[keys, :] += dvs[0] + dvs[1]

            @pl.when(t == 0)
            def _():
                dk_ref[...] = jnp.zeros_like(dk_ref)
                dv_ref[...] = jnp.zeros_like(dv_ref)
                if ctx_q:
                    run(slice(0, tc), True)
                else:
                    dq_ref[...] = jnp.zeros_like(dq_ref)

            @pl.when(t > 0)
            def _():
                run(slice(0, t_all), False)

        return _call_with_exchange(
            bwd_body, xchg, name=name + "_bwd", grid=grid, in_specs=[q_spec, k_spec, v_spec, o_spec, o_spec],
            out_specs=[q_spec, k_spec, v_spec],
            out_shape=[jax.ShapeDtypeStruct(q.shape, F32), jax.ShapeDtypeStruct(q.shape, F32), o_shape],
            operands=(q, k, v, o, do), scratch_shapes=[f32_scr, f32_scr, bf16_scr, bf16_scr] * MLA_HPS)

    return o, gathered, bwd


def _swa_block(q, keys, vals, sink, mask):
    qs = jnp.concatenate(list(_split(q, SWA_GROUP, 1)), axis=0)
    sk = jnp.sum(sink, axis=-1, keepdims=True) * (1.0 / LANE)
    s = _nt(qs, keys) * (SWA_HEAD_DIM ** -0.5)
    if mask is not None:
        s = jnp.where(mask, s, NEG_INF)
    o = _split(_nn(_softmax_rows(s, sk), vals + _roll(vals, LANE // 2, 1)), SWA_GROUP, 0)
    low = _head_half(0, o[0].shape)
    return jnp.concatenate([jnp.where(low, o[0], o[1]), jnp.where(low, o[2], o[3])], axis=1)


def _swa_ctx_block(q, kc, vc, sink):
    return _swa_block(q, kc, vc, sink, None)


def _swa_win_block(q, kc, kw, vc, vw, sink, mask):
    return _swa_block(q, jnp.concatenate([kc, kw], axis=0), jnp.concatenate([vc, vw], axis=0), sink, mask)


def _swa_attn(q, k, p_all, sink_b, tc, ctx_q, name, xchg=None):
    bsz, t_all, _ = q.shape
    n_q = t_all // QB_SWA
    n_cq = tc // QB_SWA
    lat = t_all - tc
    span = QB_SWA + 2 * WINDOW
    gw = SWA_GROUP * LANE
    grid = (bsz, SWA_KV_HEADS, n_q)
    q_spec = pl.BlockSpec((None, QB_SWA, gw), lambda b, g, i: (b, i, g))
    k_spec = pl.BlockSpec((None, t_all, LANE), lambda b, g, i: (b, 0, g))
    v_spec = pl.BlockSpec((None, t_all, LANE), lambda b, g, i: (b, 0, PC_SV // LANE + g))
    s_spec = pl.BlockSpec((None, SWA_GROUP * QB_SWA, LANE), lambda b, g, i: (g, 0, 0))

    def window(i):
        q0 = (i - n_cq) * QB_SWA
        w0 = jnp.clip(q0 - WINDOW, 0, lat - span)
        w0 = pl.multiple_of(w0, WINDOW)
        shape = (SWA_GROUP * QB_SWA, tc + span)
        qi = q0 + lax.broadcasted_iota(jnp.int32, shape, 0) % QB_SWA
        col = lax.broadcasted_iota(jnp.int32, shape, 1)
        kj = w0 + col - tc
        mask = (col < tc) | ((kj >= qi - WINDOW) & (kj <= qi + WINDOW))
        return w0, mask

    def fwd_body(q_ref, k_ref, v_ref, s_ref, o_ref):
        i = pl.program_id(2)

        @pl.when(i < n_cq)
        def _():
            if ctx_q:
                o_ref[...] = _swa_ctx_block(q_ref[...].astype(F32), k_ref[0:tc, :], v_ref[0:tc, :].astype(F32),
                                            s_ref[...])
            else:
                o_ref[...] = jnp.zeros_like(o_ref)

        @pl.when(i >= n_cq)
        def _():
            w0, mask = window(i)
            o_ref[...] = _swa_win_block(q_ref[...].astype(F32), k_ref[0:tc, :], k_ref[pl.ds(tc + w0, span), :],
                                        v_ref[0:tc, :].astype(F32), v_ref[pl.ds(tc + w0, span), :].astype(F32),
                                        s_ref[...], mask)

    o_spec = pl.BlockSpec((None, QB_SWA, SWA_GROUP * SWA_HEAD_DIM), lambda b, g, i: (b, i, g))
    (o,), gathered = _call_with_exchange(
        fwd_body, xchg, name=name, grid=grid, in_specs=[q_spec, k_spec, v_spec, s_spec], out_specs=[o_spec],
        out_shape=[jax.ShapeDtypeStruct((bsz, t_all, SWA_HEADS * SWA_HEAD_DIM), F32)],
        operands=(q, k, p_all, sink_b))

    def bwd(do, xchg=None):
        def bwd_body(q_ref, k_ref, v_ref, s_ref, do_ref, dq_ref, dk_ref, dv_ref, ds_ref):
            i = pl.program_id(2)

            @pl.when(i == 0)
            def _():
                dk_ref[...] = jnp.zeros_like(dk_ref)
                dv_ref[...] = jnp.zeros_like(dv_ref)
                ds_ref[...] = jnp.zeros_like(ds_ref)

            @pl.when(i < n_cq)
            def _():
                if ctx_q:
                    _, vjp = jax.vjp(_swa_ctx_block, q_ref[...].astype(F32), k_ref[0:tc, :].astype(F32),
                                     v_ref[0:tc, :].astype(F32), s_ref[...])
                    dq, dk, dv, ds = vjp(do_ref[...])
                    dq_ref[...] = dq
                    dk_ref[0:tc, :] += dk
                    dv_ref[0:tc, :] += dv
                    ds_ref[...] += ds
                else:
                    dq_ref[...] = jnp.zeros_like(dq_ref)

            @pl.when(i >= n_cq)
            def _():
                w0, mask = window(i)
                win = pl.ds(tc + w0, span)
                _, vjp = jax.vjp(functools.partial(_swa_win_block, mask=mask), q_ref[...].astype(F32),
                                 k_ref[0:tc, :].astype(F32), k_ref[win, :].astype(F32),
                                 v_ref[0:tc, :].astype(F32), v_ref[win, :].astype(F32), s_ref[...])
                dq, dkc, dkw, dvc, dvw, ds = vjp(do_ref[...])
                dq_ref[...] = dq
                dk_ref[0:tc, :] += dkc
                dk_ref[win, :] += dkw
                dv_ref[0:tc, :] += dvc
                dv_ref[win, :] += dvw
                ds_ref[...] += ds

        kv_out = pl.BlockSpec((None, t_all, LANE), lambda b, g, i: (b, 0, g))
        ds_spec = pl.BlockSpec((None, None, SWA_GROUP * QB_SWA, LANE), lambda b, g, i: (b, g, 0, 0))
        kv_shape = jax.ShapeDtypeStruct((bsz, t_all, SWA_KV_HEADS * LANE), F32)
        return _call_with_exchange(
            bwd_body, xchg, name=name + "_bwd", grid=grid, in_specs=[q_spec, k_spec, v_spec, s_spec, o_spec],
            out_specs=[q_spec, kv_out, kv_out, ds_spec],
            out_shape=[jax.ShapeDtypeStruct(q.shape, F32), kv_shape, kv_shape,
                       jax.ShapeDtypeStruct((bsz,) + sink_b.shape, F32)],
            operands=(q, k, p_all, sink_b, do))

    return o, gathered, bwd


def _scan_pair(chains, scratch):
    t_all, c = chains[0][0].shape
    n_tiles = t_all // SUBLANE
    row8 = lax.broadcasted_iota(jnp.int32, (t_all, c), 0) % SUBLANE
    refs = [scratch[0:3], scratch[3:6]]
    for (a, u, reverse), (a_s, u_s, _) in zip(chains, refs):
        for d in (1, 2, 4):
            sh = d if not reverse else t_all - d
            ar, ur = pltpu.roll(a, sh, 0), pltpu.roll(u, sh, 0)
            m = (row8 >= d) if not reverse else (row8 < SUBLANE - d)
            u = jnp.where(m, a * ur + u, u)
            a = jnp.where(m, a * ar, a)
        a_s[...] = a
        u_s[...] = u

    def step(j, carries):
        out = []
        for (_, _, reverse), (a_s, u_s, c_s), carry in zip(chains, refs, carries):
            tile = j if not reverse else n_tiles - 1 - j
            base = pl.multiple_of(tile * SUBLANE, SUBLANE)
            c_s[pl.ds(base, SUBLANE), :] = jnp.broadcast_to(carry, (SUBLANE, c))
            last = base + (0 if reverse else SUBLANE - 1)
            out.append(a_s[pl.ds(last, 1), :] * carry + u_s[pl.ds(last, 1), :])
        return tuple(out)

    lax.fori_loop(0, n_tiles, step, (jnp.zeros((1, c), F32),) * 2, unroll=4)
    return [a_s[...] * c_s[...] + u_s[...] for a_s, u_s, c_s in refs]


def _shift_rows(x, reverse_src):
    t_all = x.shape[0]
    row = lax.broadcasted_iota(jnp.int32, x.shape, 0)
    if reverse_src:
        return jnp.where(row == t_all - 1, 0.0, pltpu.roll(x, t_all - 1, 0))
    return jnp.where(row == 0, 0.0, pltpu.roll(x, 1, 0))


def _lru_scan(a0, u0, a1, u1, name):
    bsz, t_all, w = a0.shape
    grid = (bsz, w // LANE)
    spec = pl.BlockSpec((None, t_all, LANE), lambda b, c: (b, 0, c))
    scratch = [pltpu.VMEM((t_all, LANE), F32)] * 6
    shape = jax.ShapeDtypeStruct(a0.shape, F32)

    def fwd_body(a0_ref, u0_ref, a1_ref, u1_ref, h0_ref, h1_ref, *scr):
        h0_ref[...], h1_ref[...] = _scan_pair([(a0_ref[...], u0_ref[...], False), (a1_ref[...], u1_ref[...], True)],
                                              scr)

    h0, h1 = _pcall(fwd_body, name=name, grid=grid, in_specs=[spec] * 4, out_specs=[spec] * 2,
                    out_shape=[shape] * 2, scratch_shapes=scratch, compiler_params=_cparams(2))(a0, u0, a1, u1)

    def bwd(dh0, dh1):
        def bwd_body(a0_ref, h0_ref, g0_ref, a1_ref, h1_ref, g1_ref, da0_ref, du0_ref, da1_ref, du1_ref, *scr):
            g0, g1 = _scan_pair([(_shift_rows(a0_ref[...], True), g0_ref[...], True),
                                 (_shift_rows(a1_ref[...], False), g1_ref[...], False)], scr)
            du0_ref[...] = g0
            da0_ref[...] = g0 * _shift_rows(h0_ref[...], False)
            du1_ref[...] = g1
            da1_ref[...] = g1 * _shift_rows(h1_ref[...], True)

        return _pcall(bwd_body, name=name + "_bwd", grid=grid, in_specs=[spec] * 6, out_specs=[spec] * 4,
                      out_shape=[shape] * 4, scratch_shapes=scratch,
                      compiler_params=_cparams(2))(a0, h0, dh0, a1, h1, dh1)

    return h0, h1, bwd


def _f_mod(x, g, shift, scale):
    return (_rms(x, g, D_MODEL) * (1.0 + scale) + shift,)


def _f_mla_q(cq, ga, w, gh, cos, sa, sb):
    n = _rms(cq, ga, MLA_Q_RANK)
    outs = []
    for wh in _split(w, MLA_HEADS, 1):
        outs.append(_rope(_rms(_nn(n, wh), gh, MLA_QK), cos, sa, sb, MLA_ROPE // 4))
    return (jnp.concatenate(outs, axis=1),)


def _f_mla_kv(ckv, krp, ga, wk, wv, gh, cos, sa, sb):
    n = _rms(ckv, ga, MLA_KV_RANK)
    outs = []
    for wh in _split(wk, MLA_HEADS, 1):
        outs.append(_rope(_rms(_nn(n, wh) + krp, gh, MLA_QK), cos, sa, sb, MLA_ROPE // 4))
    return jnp.concatenate(outs, axis=1), _nn(n, wv)


def _f_conv(x, w0, w1, w2, w3, bias, tc):
    t_all = x.shape[0]
    row = lax.broadcasted_iota(jnp.int32, x.shape, 0)
    lo = jnp.where(row < tc, 0, tc)
    hi = jnp.where(row < tc, tc, t_all)
    y = bias + jnp.zeros_like(x)
    for kk, wk in enumerate((w0, w1, w2, w3)):
        src = row + (kk - 2)
        xs = x if kk == 2 else _roll(x, 2 - kk, 0)
        y = y + wk * jnp.where((src >= lo) & (src < hi), xs, 0.0)
    return (y,)


def _f_gates(xc, w16, b00, b01, b10, b11, sp0, sp1):
    ws = _unstack(w16)
    n_cb = LRU_WIDTH // LANE
    xcs = _split(xc, n_cb, 1)
    bias = [_split(b, n_cb, 1) for b in (b00, b01, b10, b11)]
    sps = [_split(s, n_cb, 1) for s in (sp0, sp1)]
    res = [[], [], [], []]
    for c in range(n_cb):
        for z in range(2):
            r = _sig(_nn(xcs[c], ws[c * 4 + 2 * z]) + bias[2 * z][c])
            i = _sig(_nn(xcs[c], ws[c * 4 + 2 * z + 1]) + bias[2 * z + 1][c])
            la = -LRU_C * r * sps[z][c]
            res[2 * z].append(jnp.exp(la))
            res[2 * z + 1].append(jnp.sqrt(-jnp.tanh(la) * (jnp.exp(2.0 * la) + 1.0)) * (i * xcs[c]))
    return tuple(jnp.concatenate(r, axis=1) for r in res)


def _f_swa_qk(sq, sk, gq, gk, cos, sa, sb):
    qs = [_rope(_rms(x, gq, SWA_HEAD_DIM), cos, sa, sb, SWA_HEAD_DIM // 4) for x in _split(sq, SWA_HEADS, 1)]
    ks = [_rope(_rms(x, gk, SWA_HEAD_DIM), cos, sa, sb, SWA_HEAD_DIM // 4) for x in _split(sk, SWA_KV_HEADS, 1)]
    return jnp.concatenate(qs, axis=1), jnp.concatenate(ks, axis=1)


def _f_qkv(cq, ckv, krp, sq, sk, q_a_g, wuq, mla_q_g, kv_a_g, wk, wv, mla_k_g, swa_q_g, swa_k_g,
           m_cos, m_sa, m_sb, s_cos, s_sa, s_sb):
    return (*_f_mla_q(cq, q_a_g, wuq, mla_q_g, m_cos, m_sa, m_sb),
            *_f_mla_kv(ckv, krp, kv_a_g, wk, wv, mla_k_g, m_cos, m_sa, m_sb),
            *_f_swa_qk(sq, sk, swa_q_g, swa_k_g, s_cos, s_sa, s_sb))


def _f_merge(oa, h0, h1, lg, oc, ga, gb, gc):
    ob = (h0 + h1) * _gelu(lg)
    return (jnp.concatenate([_rms(oa, ga, GROUP_WIDTH), _rms(ob, gb, GROUP_WIDTH), _rms(oc, gc, GROUP_WIDTH)],
                            axis=1),)


def _f_resid_mod(x, y, gate, g, shift, scale):
    x1 = x + gate * y
    return x1, _rms(x1, g, D_MODEL) * (1.0 + scale) + shift


def _f_resid(x, y, gate):
    return (x + gate * y,)


def _hosted(hooks, key, arg=None):
    make, done = hooks.get(key, (None, None))
    xchg = make(arg) if make is not None else None
    return xchg, (done if xchg is not None else lambda outs: None)


def _layer(li, x, mods, w, s, tabs, tc, ctx_q, hooks, latent_dx_only):
    bsz, t_all, _ = x.shape
    n_t = t_all // TB
    grid = (bsz, n_t)
    rows = lambda b, t: (b, t, 0)

    def row(arr, width=None, idx=0, gdtype=F32, gshape=None):
        width = width or arr.shape[-1]
        return _A(arr, (None, TB, width), lambda b, t: (b, t, idx), "row", gdtype=gdtype, gshape=gshape,
                  gimap=rows if gshape is not None else None)

    def out(width, dtype, imap=rows):
        return ((bsz, t_all, width), dtype, (None, TB, width), imap)

    def modarg(arr):
        return _A(arr, (None, None, 1, D_MODEL), lambda b, t: (b, jnp.minimum(t, 1), 0, 0), "acc",
                  first=lambda b, t: t <= 1)

    def tab(arr):
        return _A(arr, (TB, LANE), lambda b, t: (t, 0), "const")

    def pcol(p_all, col, width):
        return row(p_all, width, col // width, gdtype=BF16, gshape=(bsz, t_all, width))

    nm = lambda base: "%s_l%d" % (base, li)
    sh1, sc1, g1, sh2, sc2, g2 = mods
    m_all = bsz * t_all

    x_arg = row(x)
    if latent_dx_only:
        n_c = tc // TB
        x_arg.gshape, x_arg.gimap = (bsz, t_all - tc, D_MODEL), lambda b, t: (b, jnp.maximum(t - n_c, 0), 0)
    (h,), b_mod1 = _rowop(nm("mod1"), _f_mod, grid, [x_arg, _par(s["norm1_g"]), modarg(sh1), modarg(sc1)],
                          [out(D_MODEL, BF16)])
    p_all = _mm(h.reshape(m_all, D_MODEL), w["win"], "nn", BF16, nm("mm_in")).reshape(bsz, t_all, P_WIDTH)

    (q_a, k_a, v_a, q_c, k_c), b_qkv = _rowop(
        nm("qkv"), _f_qkv, grid,
        [pcol(p_all, PC_CQ, 256), pcol(p_all, PC_CKV, 128), pcol(p_all, PC_KR, 128), pcol(p_all, PC_SQ, 1024),
         pcol(p_all, PC_SK, 256)]
        + [_par(a) for a in (s["q_a_g"], w["wuq"], s["mla_q_g"], s["kv_a_g"], w["wk"], w["wv"], s["mla_k_g"],
                             s["swa_q_g"], s["swa_k_g"])]
        + [tab(a) for a in tabs["mla"] + tabs["swa"]],
        [out(MLA_HEADS * LANE, BF16), out(MLA_HEADS * LANE, BF16), out(MLA_HEADS * MLA_V, BF16),
         out(SWA_HEADS * LANE, BF16), out(SWA_KV_HEADS * LANE, BF16)])

    xchg, done = _hosted(hooks, "mla_fwd")
    o_a, got, b_attn_a = _mla_attn(q_a, k_a, v_a, tc, ctx_q, nm("mla_attn"), xchg)
    done(got)

    n_cb = LRU_WIDTH // LANE
    conv_grid = (n_cb, bsz)
    cpar = lambda arr: _A(arr, (1, LANE), lambda c, b: (0, c), "acc", first=lambda c, b: b == 0)
    conv_args = [_A(p_all, (None, t_all, LANE), lambda c, b: (b, 0, PC_LX // LANE + c), "row", gdtype=BF16,
                    gshape=(bsz, t_all, LRU_WIDTH), gimap=lambda c, b: (b, 0, c))]
    conv_args += [cpar(a) for a in s["conv_w"]] + [cpar(s["conv_b"])]
    conv_out = [((bsz, t_all, LRU_WIDTH), F32, (None, t_all, LANE), lambda c, b: (b, 0, c))]
    (xc,), b_conv = _rowop(nm("lru_conv"), functools.partial(_f_conv, tc=tc), conv_grid, conv_args, conv_out)
    rot = lambda b, t: (b, (t + n_t - 1) % n_t, 0)
    (a0, u0, a1, u1), b_gates = _rowop(
        nm("lru_gates"), _f_gates, grid,
        [row(xc), _par(s["wbd"])] + [_par(a) for a in s["gate_b"]] + [_par(a) for a in s["sp"]],
        [out(LRU_WIDTH, F32), out(LRU_WIDTH, F32), out(LRU_WIDTH, F32, rot), out(LRU_WIDTH, F32, rot)])
    h0, h1, b_scan = _lru_scan(a0, u0, a1, u1, nm("lru_scan"))
    h1_arg = _A(h1, (None, TB, LRU_WIDTH), rot, "row")

    xchg, done = _hosted(hooks, "swa_fwd")
    o_c, got, b_attn_c = _swa_attn(q_c, k_c, p_all, s["sink_b"], tc, ctx_q, nm("swa_attn"), xchg)
    done(got)

    (y_in,), b_merge = _rowop(nm("merge"), _f_merge, grid,
                              [row(o_a), row(h0), h1_arg, pcol(p_all, PC_LG, 512), row(o_c), _par(s["g_a"]),
                               _par(s["g_b"]), _par(s["g_c"])],
                              [out(MIX_P, BF16)])
    y = _mm(y_in.reshape(m_all, MIX_P), w["wout"], "nn", F32, nm("mm_out")).reshape(bsz, t_all, D_MODEL)
    (x1, hm), b_rm = _rowop(nm("resid_mod"), _f_resid_mod, grid,
                            [row(x), row(y, gdtype=BF16), modarg(g1), _par(s["norm2_g"]), modarg(sh2), modarg(sc2)],
                            [out(D_MODEL, F32), out(D_MODEL, BF16)])
    pre, act = _mm(hm.reshape(m_all, D_MODEL), w["ff1"], "nn", BF16, nm("mm_ff1"), epi="sqrelu")
    y2 = _mm(act, w["ff2"], "nn", F32, nm("mm_ff2")).reshape(bsz, t_all, D_MODEL)
    (x2,), b_res = _rowop(nm("resid"), _f_resid, grid,
                          [_A(x1, (None, TB, D_MODEL), rows, "fwd"), row(y2, gdtype=BF16), modarg(g2)],
                          [out(D_MODEL, F32)])

    def bwd(dx2, hooks):
        dw, ds = {}, {}
        dy2, dg2 = b_res(dx2)
        dy2 = dy2.reshape(m_all, D_MODEL)
        dpre = _mm(dy2, w["ff2"], "nt", BF16, nm("mm_ff2_dx"), epi="dsqrelu", aux=pre)
        dw["ff2"] = _mm(act, dy2, "tn", BF16, nm("mm_ff2_dw"))
        dhm = _mm(dpre, w["ff1"], "nt", F32, nm("mm_ff1_dx")).reshape(bsz, t_all, D_MODEL)
        dw["ff1"] = _mm(hm.reshape(m_all, D_MODEL), dpre, "tn", BF16, nm("mm_ff1_dw"), out_split=N_DEV)
        dxa, dy, dg1, ds["norm2_g"], dsh2, dsc2 = b_rm(dx2, dhm)
        dy = dy.reshape(m_all, D_MODEL)
        dy_in = _mm(dy, w["wout"], "nt", F32, nm("mm_out_dx")).reshape(bsz, t_all, MIX_P)
        dw["wout"] = _mm(y_in.reshape(m_all, MIX_P), dy, "tn", BF16, nm("mm_out_dw"))
        do_a, dh0, dh1, dlg, do_c, ds["g_a"], ds["g_b"], ds["g_c"] = b_merge(dy_in)

        (dq_c, dk_c, dsv, dsink), _ = b_attn_c(do_c)
        ds["sink_b"] = jnp.sum(dsink, axis=0)

        da0, du0, da1, du1 = b_scan(dh0, dh1)
        gates_g = b_gates(da0, du0, da1, du1)
        dxc, ds["wbd"] = gates_g[0], gates_g[1]
        ds["gate_b"], ds["sp"] = list(gates_g[2:6]), list(gates_g[6:8])
        conv_g = b_conv(dxc)
        dlx, ds["conv_w"], ds["conv_b"] = conv_g[0], list(conv_g[1:5]), conv_g[5]

        xchg, done = _hosted(hooks, "mla_bwd", (dw, ds))
        (dq_a, dk_a, dv_a), got = b_attn_a(do_a, xchg)
        done(got)
        (dcq, dckv, dkr, dsq, dsk, ds["q_a_g"], dw["wuq"], ds["mla_q_g"], ds["kv_a_g"], dw["wk"], dw["wv"],
         ds["mla_k_g"], ds["swa_q_g"], ds["swa_k_g"]) = b_qkv(dq_a, dk_a, dv_a, dq_c, dk_c)

        dp = jnp.concatenate([dsq, dlx, dlg, dcq, dsk, dsv.astype(BF16), dckv, dkr], axis=-1)
        dp = dp.reshape(m_all, P_WIDTH)
        dh = _mm(dp, w["win"], "nt", F32, nm("mm_in_dx")).reshape(bsz, t_all, D_MODEL)
        dw["win"] = _mm(h.reshape(m_all, D_MODEL), dp, "tn", BF16, nm("mm_in_dw"))
        dx, ds["norm1_g"], dsh1, dsc1 = b_mod1(dh, add_to_first=dxa)
        return dx, [dsh1, dsc1, dg1, dsh2, dsc2, dg2], dw, ds

    return x2, bwd


def _loss_and_grad(x2, target, tc):
    bsz, t_all, d = x2.shape
    n_t = t_all // TB
    n_c = tc // TB

    def body(x_ref, t_ref, l_ref, dx_ref):
        b, t = pl.program_id(0), pl.program_id(1)

        @pl.when((b == 0) & (t == 0))
        def _():
            l_ref[...] = jnp.zeros_like(l_ref)

        @pl.when(t < n_c)
        def _():
            dx_ref[...] = jnp.zeros_like(dx_ref)

        @pl.when(t >= n_c)
        def _():
            e = x_ref[...] - t_ref[...]
            dx_ref[...] = e * (1.0 / d)
            l_ref[...] += jnp.sum(e * e) * (0.5 / d)

    loss, dx = _pcall(
        body, name="loss", grid=(bsz, n_t),
        in_specs=[pl.BlockSpec((None, TB, d), lambda b, t: (b, t, 0)),
                  pl.BlockSpec((None, TB, d), lambda b, t: (b, jnp.maximum(t - n_c, 0), 0))],
        out_specs=[pl.BlockSpec((SUBLANE, LANE), lambda b, t: (0, 0)),
                   pl.BlockSpec((None, TB, d), lambda b, t: (b, t, 0))],
        out_shape=[jax.ShapeDtypeStruct((SUBLANE, LANE), F32), jax.ShapeDtypeStruct(x2.shape, F32)],
        compiler_params=_cparams(2))(x2, target)
    return loss[0, 0], dx


def _rope_tables(lat, tc, dim, lane0):
    quarter = dim // 4
    pos = np.arange(lat)
    grid_pos = np.stack([pos // GRID_W, pos % GRID_W], axis=-1).astype(np.float32)
    lane = np.arange(LANE)
    p = np.clip(lane - lane0, 0, dim - 1)
    active = (lane >= lane0) & (lane < lane0 + dim)
    axis, half, qi = p // (dim // 2), (p % (dim // 2)) // quarter, p % quarter
    inv = (np.float32(ROPE_THETA) ** (-qi.astype(np.float32) / np.float32(quarter))).astype(np.float32)
    ang = (np.where(axis[None, :] == 0, grid_pos[:, 0:1], grid_pos[:, 1:2]) * inv[None, :]).astype(np.float32)
    cos = np.where(active, np.cos(ang), 1.0).astype(np.float32)
    sin = np.where(active, np.sin(ang), 0.0).astype(np.float32)
    sa = np.where(half == 0, -sin, 0.0).astype(np.float32)
    sb = np.where(half == 1, sin, 0.0).astype(np.float32)
    ctx1, ctx0 = np.ones((tc, LANE), np.float32), np.zeros((tc, LANE), np.float32)
    return tuple(jnp.asarray(np.concatenate([c, t], 0)) for c, t in ((ctx1, cos), (ctx0, sa), (ctx0, sb)))


_BIG = {"w_in": ((D_MODEL, IN_WIDTH // N_DEV), 1, ("win",)),
        "w_uq": ((MLA_Q_RANK, MLA_HEADS * MLA_QK // N_DEV), 1, ("wuq",)),
        "w_ukv": ((MLA_KV_RANK, MLA_HEADS * (MLA_NOPE + MLA_V) // N_DEV), 1, ("wk", "wv")),
        "w_out": ((3 * GROUP_WIDTH // N_DEV, D_MODEL), 0, ("wout",)),
        "w_ff1": ((D_MODEL, D_FF // N_DEV), 1, ("ff1",)),
        "w_ff2": ((D_FF // N_DEV, D_MODEL), 0, ("ff2",))}
_EARLY = ("w_in", "w_uq", "w_ukv")
_LATE = ("w_out", "w_ff1", "w_ff2")


def _pad_heads(wm, n_heads, dim):
    out = jnp.pad(wm.reshape(wm.shape[0], n_heads, dim), ((0, 0), (0, 0), (0, LANE - dim)))
    return out.reshape(wm.shape[0], n_heads * LANE)


def _prep_weight(name, piece):
    shp, ax, _ = _BIG[name]
    full = jnp.moveaxis(piece, 0, ax).reshape(shp[:ax] + (N_DEV * shp[ax],) + shp[ax + 1:])
    if name == "w_in":
        cq, ckv, kr, lx, lg, sq, sk, sv = _split_cols(full)
        return {"win": jnp.concatenate(
            [_pad_heads(sq, SWA_HEADS, SWA_HEAD_DIM), lx, lg, cq, _pad_heads(sk, SWA_KV_HEADS, SWA_HEAD_DIM),
             _pad_heads(sv, SWA_KV_HEADS, SWA_HEAD_DIM), ckv, jnp.pad(kr, ((0, 0), (MLA_NOPE, LANE - MLA_QK)))], axis=1)}
    if name == "w_uq":
        return {"wuq": _pad_heads(full, MLA_HEADS, MLA_QK)}
    if name == "w_ukv":
        ukv = full.reshape(MLA_KV_RANK, MLA_HEADS, MLA_NOPE + MLA_V)
        return {"wk": _pad_heads(ukv[:, :, :MLA_NOPE].reshape(MLA_KV_RANK, -1), MLA_HEADS, MLA_NOPE),
                "wv": ukv[:, :, MLA_NOPE:].reshape(MLA_KV_RANK, -1)}
    return {_BIG[name][2][0]: full}


def _split_cols(wm):
    parts, start = [], 0
    for size in IN_SIZES:
        parts.append(wm[:, start:start + size])
        start += size
    return parts


def _prep_gates(gate_w):
    gw = gate_w.reshape(2, 2, 4, 2, 64, 64)
    wbd = jnp.einsum("zgknCm,nN->knCzgNm", gw, jnp.eye(2, dtype=F32)).reshape(4, LANE, 4, LANE)
    return wbd.transpose(0, 2, 1, 3).reshape(16, LANE, LANE)


def _prep_small(raw):
    r1 = lambda a: a.reshape(1, -1)
    gg = raw["group_g"]
    sink = raw["swa_sink"].reshape(SWA_KV_HEADS, SWA_GROUP, 1, 1)
    return {
        "norm1_g": r1(raw["norm1_g"]), "norm2_g": r1(raw["norm2_g"]),
        "q_a_g": r1(raw["q_a_g"]), "kv_a_g": r1(raw["kv_a_g"]),
        "mla_q_g": jnp.pad(r1(raw["mla_q_g"]), ((0, 0), (0, LANE - MLA_QK))),
        "mla_k_g": jnp.pad(r1(raw["mla_k_g"]), ((0, 0), (0, LANE - MLA_QK))),
        "swa_q_g": jnp.pad(r1(raw["swa_q_g"]), ((0, 0), (0, LANE - SWA_HEAD_DIM))),
        "swa_k_g": jnp.pad(r1(raw["swa_k_g"]), ((0, 0), (0, LANE - SWA_HEAD_DIM))),
        "conv_w": [r1(raw["conv_w"][kk]) for kk in range(4)], "conv_b": r1(raw["conv_b"]),
        "gate_b": [r1(raw["lru_gate_b"][z, g]) for z in range(2) for g in range(2)],
        "sp": [r1(jax.nn.softplus(-raw["lru_lambda"][z])) for z in range(2)],
        "sink_b": jnp.broadcast_to(sink, (SWA_KV_HEADS, SWA_GROUP, QB_SWA, LANE)).reshape(
            SWA_KV_HEADS, SWA_GROUP * QB_SWA, LANE),
        "g_a": r1(gg[:GROUP_WIDTH]), "g_b": r1(gg[GROUP_WIDTH:2 * GROUP_WIDTH]), "g_c": r1(gg[2 * GROUP_WIDTH:])}


def _mesh_pos():
    return lax.axis_index("x"), lax.axis_index("y"), lax.axis_index("c")


def _peer(pos, k):
    return tuple(1 - p if (k >> s) & 1 else p for p, s in zip(pos, (2, 1, 0)))


def _dev_index(pos):
    return 4 * pos[0] + 2 * pos[1] + pos[2]


class _Exchange:
    def __init__(self, bufs, gather):
        self.bufs = list(bufs)
        self.n = len(self.bufs)
        self.gather = [gather] * self.n if isinstance(gather, bool) else list(gather)
        self.specs = [pl.BlockSpec(memory_space=pl.ANY)] * self.n
        self.out_shape = [jax.ShapeDtypeStruct((N_DEV,) + tuple(b.shape if g else b.shape[1:]), b.dtype)
                          for b, g in zip(self.bufs, self.gather)]
        self.scratch = [pltpu.SemaphoreType.DMA(((N_DEV - 1) * self.n,)),
                        pltpu.SemaphoreType.DMA(((N_DEV - 1) * self.n,)), pltpu.SemaphoreType.DMA((self.n,))]

    def _copies(self, x_refs, o_refs, sems, with_recvs):
        send_sems, recv_sems, local_sems = sems
        pos = _mesh_pos()
        me = _dev_index(pos)
        locals_, sends, recvs = [], [], []
        for j in range(self.n):
            src_mine = x_refs[j] if self.gather[j] else x_refs[j].at[me]
            locals_.append(pltpu.make_async_copy(src_mine, o_refs[j].at[me], local_sems.at[j]))
        for k in range(1, N_DEV):
            peer = _peer(pos, k)
            pidx = _dev_index(peer)
            for j in range(self.n):
                src = x_refs[j] if self.gather[j] else x_refs[j].at[pidx]
                sem = (k - 1) * self.n + j
                sends.append(pltpu.make_async_remote_copy(
                    src_ref=src, dst_ref=o_refs[j].at[me], send_sem=send_sems.at[sem], recv_sem=recv_sems.at[sem],
                    device_id=peer, device_id_type=pl.DeviceIdType.MESH))
                if with_recvs:
                    recvs.append(pltpu.make_async_remote_copy(
                        src_ref=src, dst_ref=o_refs[j].at[pidx], send_sem=send_sems.at[sem],
                        recv_sem=recv_sems.at[sem], device_id=peer, device_id_type=pl.DeviceIdType.MESH))
        return locals_, sends, recvs

    def start(self, x_refs, o_refs, sems):
        locals_, sends, _ = self._copies(x_refs, o_refs, sems, False)
        for cp in locals_ + sends:
            cp.start()

    def wait(self, x_refs, o_refs, sems):
        locals_, sends, recvs = self._copies(x_refs, o_refs, sems, True)
        for cp in recvs:
            cp.wait_recv()
        for cp in sends:
            cp.wait_send()
        for cp in locals_:
            cp.wait()


def _exchange(bufs, gather, name):
    xchg = _Exchange(bufs, gather)
    n = xchg.n

    def body(*refs):
        xchg.start(refs[:n], refs[n:2 * n], refs[2 * n:])
        xchg.wait(refs[:n], refs[n:2 * n], refs[2 * n:])

    return _pcall(body, name=name, out_shape=xchg.out_shape, in_specs=xchg.specs, out_specs=xchg.specs,
                  scratch_shapes=xchg.scratch)(*xchg.bufs)


def _pack(arrs, dtype):
    flat = jnp.concatenate([a.reshape(-1).astype(dtype) for a in arrs])
    rows = -(-flat.size // PACK_W)
    rows = -(-rows // 16) * 16
    return jnp.pad(flat, (0, rows * PACK_W - flat.size)).reshape(rows, PACK_W)


def _unpack(buf, shapes, lead=()):
    flat = buf.reshape(lead + (-1,))
    out, off = [], 0
    for shp in shapes:
        n = math.prod(shp)
        out.append(flat[..., off:off + n].reshape(lead + tuple(shp)))
        off += n
    return out


def _sum_sources(buf, name):
    _, r, c = buf.shape
    tr = _rows_tile(r)

    def body(x_ref, o_ref):
        acc = x_ref[0]
        for d in range(1, N_DEV):
            acc = acc + x_ref[d]
        o_ref[...] = acc

    return _pcall(body, name=name, grid=(r // tr,),
                  in_specs=[pl.BlockSpec((N_DEV, tr, c), lambda i: (0, i, 0))],
                  out_specs=pl.BlockSpec((tr, c), lambda i: (i, 0)),
                  out_shape=jax.ShapeDtypeStruct((r, c), F32), compiler_params=_cparams(1))(buf)


def _rows_tile(r):
    best = r
    for t in range(SUBLANE, ELEMWISE_ROWS_MAX + 1, SUBLANE):
        if r % t == 0:
            best = t
    return best


def _adamw(grads, wgt, m, v, name):
    n_lay = len(grads)
    n_src, r, c = grads[0].shape
    tr = _rows_tile(r)
    n_blk = r // tr
    bc1 = 1.0 - ADAM_B1 ** ADAM_STEP
    bc2 = 1.0 - ADAM_B2 ** ADAM_STEP

    def body(*refs):
        g_refs, (w_ref, m_ref, v_ref, go_ref, d_ref, mo_ref, vo_ref) = refs[:n_lay], refs[n_lay:]
        for li, g_ref in enumerate(g_refs):
            @pl.when(pl.program_id(0) == li)
            def _():
                g = g_ref[0].astype(F32)
                for d in range(1, n_src):
                    g = g + g_ref[d].astype(F32)
                m_new = ADAM_B1 * m_ref[...] + (1.0 - ADAM_B1) * g
                v_new = ADAM_B2 * v_ref[...] + (1.0 - ADAM_B2) * (g * g)
                go_ref[...] = g
                mo_ref[...] = m_new
                vo_ref[...] = v_new
                d_ref[...] = -ADAM_LR * ((m_new / bc1) / (jnp.sqrt(v_new / bc2) + ADAM_EPS) + ADAM_WD * w_ref[...])

    g_specs = [pl.BlockSpec((n_src, tr, c),
                            lambda l, i, li=li: (0, jnp.where(l == li, i, jnp.where(l > li, n_blk - 1, 0)), 0))
               for li in range(n_lay)]
    spec = pl.BlockSpec((tr, c), lambda l, i: (l * n_blk + i, 0))
    return _pcall(body, name=name, grid=(n_lay, n_blk), in_specs=g_specs + [spec, spec, spec],
                  out_specs=[spec] * 4, out_shape=[jax.ShapeDtypeStruct((n_lay * r, c), F32)] * 4,
                  compiler_params=_cparams(2))(*grads, wgt, m, v)


def _silu(z):
    return z * jax.nn.sigmoid(z)


_WEIGHTS = ("c_ctx", "w_mod", "b_mod", "norm1_g", "w_in", "q_a_g", "w_uq", "kv_a_g", "w_ukv", "mla_q_g", "mla_k_g",
            "conv_w", "conv_b", "lru_gate_w", "lru_gate_b", "lru_lambda", "swa_q_g", "swa_k_g", "swa_sink", "group_g",
            "w_out", "norm2_g", "w_ff1", "w_ff2")
_SHARDED_SMALL = ("conv_w", "lru_gate_b", "lru_lambda")
_REPL_RAW = ("norm1_g", "q_a_g", "kv_a_g", "mla_q_g", "mla_k_g", "conv_b", "swa_q_g", "swa_k_g",
             "swa_sink", "group_g", "norm2_g")
MOD_ROWS = 32


def kernel(x, c, ctx, c_ctx, w_mod, b_mod, norm1_g, w_in, q_a_g, w_uq, kv_a_g, w_ukv, mla_q_g, mla_k_g, conv_w, conv_b, lru_gate_w, lru_gate_b, lru_lambda, swa_q_g, swa_k_g, swa_sink, group_g, w_out, norm2_g, w_ff1, w_ff2, loss_target, m_c_ctx, m_w_mod, m_b_mod, m_norm1_g, m_w_in, m_q_a_g, m_w_uq, m_kv_a_g, m_w_ukv, m_mla_q_g, m_mla_k_g, m_conv_w, m_conv_b, m_lru_gate_w, m_lru_gate_b, m_lru_lambda, m_swa_q_g, m_swa_k_g, m_swa_sink, m_group_g, m_w_out, m_norm2_g, m_w_ff1, m_w_ff2, v_c_ctx, v_w_mod, v_b_mod, v_norm1_g, v_w_in, v_q_a_g, v_w_uq, v_kv_a_g, v_w_ukv, v_mla_q_g, v_mla_k_g, v_conv_w, v_conv_b, v_lru_gate_w, v_lru_gate_b, v_lru_lambda, v_swa_q_g, v_swa_k_g, v_swa_sink, v_group_g, v_w_out, v_norm2_g, v_w_ff1, v_w_ff2):
    wts = dict(c_ctx=c_ctx, w_mod=w_mod, b_mod=b_mod, norm1_g=norm1_g, w_in=w_in, q_a_g=q_a_g, w_uq=w_uq,
               kv_a_g=kv_a_g, w_ukv=w_ukv, mla_q_g=mla_q_g, mla_k_g=mla_k_g, conv_w=conv_w, conv_b=conv_b,
               lru_gate_w=lru_gate_w, lru_gate_b=lru_gate_b, lru_lambda=lru_lambda, swa_q_g=swa_q_g, swa_k_g=swa_k_g,
               swa_sink=swa_sink, group_g=group_g, w_out=w_out, norm2_g=norm2_g, w_ff1=w_ff1, w_ff2=w_ff2)
    mom1 = dict(zip(_WEIGHTS, (m_c_ctx, m_w_mod, m_b_mod, m_norm1_g, m_w_in, m_q_a_g, m_w_uq, m_kv_a_g, m_w_ukv,
                               m_mla_q_g, m_mla_k_g, m_conv_w, m_conv_b, m_lru_gate_w, m_lru_gate_b, m_lru_lambda,
                               m_swa_q_g, m_swa_k_g, m_swa_sink, m_group_g, m_w_out, m_norm2_g, m_w_ff1, m_w_ff2)))
    mom2 = dict(zip(_WEIGHTS, (v_c_ctx, v_w_mod, v_b_mod, v_norm1_g, v_w_in, v_q_a_g, v_w_uq, v_kv_a_g, v_w_ukv,
                               v_mla_q_g, v_mla_k_g, v_conv_w, v_conv_b, v_lru_gate_w, v_lru_gate_b, v_lru_lambda,
                               v_swa_q_g, v_swa_k_g, v_swa_sink, v_group_g, v_w_out, v_norm2_g, v_w_ff1, v_w_ff2)))
    bsz = x.shape[0]
    n_ex = bsz * N_DEV
    me = _dev_index(_mesh_pos())
    mod_cols = w_mod.shape[-1]

    small_shapes = [c.shape, conv_w.shape, lru_gate_b.shape, lru_lambda.shape]
    shard = lambda n, li: wts[n][li].astype(BF16)
    g_small, *early_pieces = _exchange([_pack([c, conv_w, lru_gate_b, lru_lambda], F32)] + [shard(n, 0) for n in _EARLY],
                                       True, "ag_first")
    c_all, conv_w_all, gate_b_all, lam_all = _unpack(g_small, small_shapes, lead=(N_DEV,))
    c_all = c_all.reshape(n_ex, D_MODEL)
    cat_last = lambda a: jnp.moveaxis(a, 0, -2).reshape(a.shape[1:-1] + (N_DEV * a.shape[-1],))
    conv_w_full, gate_b_full, lam_full = cat_last(conv_w_all), cat_last(gate_b_all), cat_last(lam_all)

    act = jnp.zeros((MOD_ROWS, D_MODEL), F32).at[:n_ex].set(_silu(c_all)).at[n_ex].set(_silu(c_ctx))
    mod_part = jnp.concatenate([_mm(act, w_mod[li], "nn", F32, "mm_mod_l%d" % li) for li in range(DEPTH)], axis=1)
    (mod_all,) = _exchange([mod_part], True, "ag_mod")
    mods = []
    for li in range(DEPTH):
        full = jnp.moveaxis(mod_all[:, :, li * mod_cols:(li + 1) * mod_cols], 0, 1).reshape(MOD_ROWS, -1) + b_mod[li]
        mine = lax.dynamic_slice_in_dim(full, me * bsz, bsz, axis=0)
        ctx_row = jnp.broadcast_to(full[n_ex], mine.shape)
        both = jnp.stack([ctx_row, mine], axis=1).reshape(bsz, 2, N_MOD, 1, D_MODEL)
        mods.append([both[:, :, j] for j in range(N_MOD)])

    raw = {n: wts[n] for n in _REPL_RAW}
    raw.update(conv_w=conv_w_full, lru_gate_b=gate_b_full, lru_lambda=lam_full)
    small_names = list(_REPL_RAW) + list(_SHARDED_SMALL)
    sp, small_vjp, gates_vjp = [None] * DEPTH, [None] * DEPTH, [None] * DEPTH
    for li in range(DEPTH):
        sp[li], small_vjp[li] = jax.vjp(_prep_small, {n: raw[n][li] for n in small_names})
        sp[li]["wbd"], gates_vjp[li] = jax.vjp(_prep_gates, lru_gate_w[li])

    w, w_vjp, g_recv, small_recv = [{} for _ in range(DEPTH)], {}, {}, {}

    def take(li, names, pieces):
        for n, piece in zip(names, pieces):
            out, w_vjp[n, li] = jax.vjp(functools.partial(_prep_weight, n), piece)
            w[li].update(out)

    def gather_hook(li, names):
        return (lambda _: _Exchange([shard(n, li) for n in names], True), lambda got: take(li, names, got))

    def wgrad(n, li, dwl):
        if n == "w_ff1":
            return dwl["ff1"]
        (g,) = w_vjp[n, li]({k: dwl[k].astype(BF16) for k in _BIG[n][2]})
        return g

    def small_pack(li, ds_l, extra=()):
        (d_raw,) = small_vjp[li]({k: v for k, v in ds_l.items() if k != "wbd"})
        return _pack([d_raw[n] for n in small_names] + list(extra), F32)

    def gates_grad(li, ds_l):
        return gates_vjp[li](ds_l["wbd"])[0].reshape(-1, LANE)

    take(0, _EARLY, early_pieces)
    hooks_fwd = [{"mla_fwd": gather_hook(0, _LATE), "swa_fwd": gather_hook(1, _EARLY + ("w_out",))},
                 {"mla_fwd": gather_hook(1, ("w_ff1", "w_ff2"))}]
    bwd_state = {}

    def scatter_last_layer(grads_so_far):
        dwl, dsl = grads_so_far
        return _Exchange([wgrad(n, 1, dwl) for n in _LATE] + [gates_grad(1, dsl)], [False] * len(_LATE) + [True])

    def scatter_first_layer(grads_so_far):
        dwl, dsl = grads_so_far
        dw1, ds1 = bwd_state["dw1"], bwd_state["ds1"]
        bufs = [wgrad(n, 1, dw1) for n in _EARLY] + [wgrad(n, 0, dwl) for n in _LATE]
        bufs += [small_pack(1, ds1), gates_grad(0, dsl)]
        return _Exchange(bufs, [False] * (len(_EARLY) + len(_LATE)) + [True] * 2)

    def scattered_first_layer(got):
        g_recv.update(zip([(n, 1) for n in _EARLY] + [(n, 0) for n in _LATE], got[:-2]))
        small_recv[1], g_recv["lru_gate_w", 0] = got[-2:]

    def scattered_last_layer(got):
        g_recv.update(zip([(n, 1) for n in _LATE], got[:-1]))
        g_recv["lru_gate_w", 1] = got[-1]

    hooks_bwd = [{"mla_bwd": (scatter_first_layer, scattered_first_layer)},
                 {"mla_bwd": (scatter_last_layer, scattered_last_layer)}]

    tc, lat = ctx.shape[1], x.shape[1]
    tabs = {"mla": _rope_tables(lat, tc, MLA_ROPE, MLA_NOPE), "swa": _rope_tables(lat, tc, SWA_HEAD_DIM, 0)}
    stream = jnp.concatenate([ctx, x], axis=1)
    bwds = []
    for li in range(DEPTH):
        stream, bwd = _layer(li, stream, mods[li], w[li], sp[li], tabs, tc, li < DEPTH - 1, hooks_fwd[li], li == 0)
        bwds.append(bwd)
    loss_part, dstream = _loss_and_grad(stream, loss_target, tc)
    dmods = [None] * DEPTH
    dstream, dmods[1], bwd_state["dw1"], bwd_state["ds1"] = bwds[1](dstream, hooks_bwd[1])
    grad_x, dmods[0], dw0, ds0 = bwds[0](dstream, hooks_bwd[0])

    dm_rows = []
    for li in range(DEPTH):
        dm = jnp.concatenate(dmods[li], axis=-1)
        dm_rows.append(jnp.concatenate([dm[:, 1, 0], jnp.sum(dm[:, 0, 0], axis=0, keepdims=True)], axis=0))
    dm_mine = jnp.concatenate(dm_rows, axis=1)
    dm_mine = jnp.pad(dm_mine, ((0, SUBLANE - bsz - 1), (0, 0)))
    (dm_all,) = _exchange([dm_mine], True, "ag_dmod")
    g_wmod, g_bmod, dact_ctx = [], [], jnp.zeros((D_MODEL,), F32)
    for li in range(DEPTH):
        part = dm_all[:, :, li * N_MOD * D_MODEL:(li + 1) * N_MOD * D_MODEL]
        dm32 = jnp.zeros((MOD_ROWS, N_MOD * D_MODEL), F32).at[:n_ex].set(part[:, :bsz].reshape(n_ex, -1))
        dm32 = dm32.at[n_ex].set(jnp.sum(part[:, bsz], axis=0))
        g_bmod.append(jnp.sum(dm32, axis=0))
        cols = lax.dynamic_slice_in_dim(dm32, me * mod_cols, mod_cols, axis=1)
        g_wmod.append(_mm(act, cols, "tn", F32, "mm_mod_dw_l%d" % li))
        dact_ctx = dact_ctx + _mm(cols, w_mod[li], "nt", F32, "mm_mod_dx_l%d" % li)[n_ex]
    sg = jax.nn.sigmoid(c_ctx)
    g_cctx_part = dact_ctx * (sg * (1.0 + c_ctx * (1.0 - sg)))

    last = _exchange([wgrad(n, 0, dw0) for n in _EARLY] + [small_pack(0, ds0, (g_cctx_part, loss_part.reshape(1)))],
                     [False] * len(_EARLY) + [True], "rs_early")
    g_recv.update(zip([(n, 0) for n in _EARLY], last[:-1]))
    small_recv[0] = last[-1]
    layer_shapes = [raw[n].shape[1:] for n in small_names]
    tot = [_unpack(_sum_sources(small_recv[li], "sum_grads_l%d" % li), layer_shapes + [(D_MODEL,), (1,)][:2 * (li == 0)])
           for li in range(DEPTH)]
    grads = {n: jnp.stack([tot[li][j] for li in range(DEPTH)], axis=0) for j, n in enumerate(small_names)}
    grads["c_ctx"], loss = tot[0][-2], tot[0][-1][0]
    for n in _SHARDED_SMALL:
        width = wts[n].shape[-1]
        grads[n] = lax.dynamic_slice_in_dim(grads[n], me * width, width, axis=grads[n].ndim - 1)
    grads["b_mod"] = jnp.stack(g_bmod, axis=0)

    delta, new_m, new_v = {}, {}, {}
    per_layer = {n: [g_recv[n, li] for li in range(DEPTH)] for n in list(_BIG) + ["lru_gate_w"]}
    per_layer["w_mod"] = [g[None] for g in g_wmod]
    for n, srcs in per_layer.items():
        two_d = (DEPTH * math.prod(wts[n].shape[1:-1]), wts[n].shape[-1])
        srcs = [s.reshape((s.shape[0], two_d[0] // DEPTH, two_d[1])) for s in srcs]
        res = _adamw(srcs, wts[n].reshape(two_d), mom1[n].reshape(two_d), mom2[n].reshape(two_d), "adamw_" + n)
        grads[n], delta[n], new_m[n], new_v[n] = [r.reshape(wts[n].shape) for r in res]
    rest = [n for n in _WEIGHTS if n not in delta]
    shapes = [wts[n].shape for n in rest]
    res = _adamw([_pack([grads[n] for n in rest], F32)[None]], _pack([wts[n] for n in rest], F32),
                 _pack([mom1[n] for n in rest], F32), _pack([mom2[n] for n in rest], F32), "adamw_small")
    for tgt, buf in zip((delta, new_m, new_v), res[1:]):
        tgt.update(zip(rest, _unpack(buf, shapes)))

    return (loss, grad_x, *[grads[n] for n in _WEIGHTS], *[delta[n] for n in _WEIGHTS],
            *[new_m[n] for n in _WEIGHTS], *[new_v[n] for n in _WEIGHTS])
```

```python
import functools
import math

import jax
import jax.numpy as jnp
import numpy as np
from jax import lax
from jax.experimental import pallas as pl
from jax.experimental.pallas import tpu as pltpu

F32, BF16 = jnp.float32, jnp.bfloat16

N_DEV = 8
DEPTH = 2
D_MODEL = 1024
D_FF = 4096
N_MOD = 6
GRID_W = 64
WINDOW = 128
ROPE_THETA = 10000.0
EPS = 1e-6
NEG_INF = -1e30
LRU_C = 8.0
LRU_WIDTH = 512
MLA_HEADS, MLA_NOPE, MLA_ROPE, MLA_V = 8, 64, 32, 64
MLA_QK = MLA_NOPE + MLA_ROPE
MLA_Q_RANK, MLA_KV_RANK = 256, 128
SWA_HEADS, SWA_KV_HEADS, SWA_GROUP, SWA_HEAD_DIM = 8, 2, 4, 64
GROUP_WIDTH = 512
IN_SIZES = (256, 128, 32, 512, 512, 512, 128, 128)
IN_WIDTH = sum(IN_SIZES)
ADAM_LR, ADAM_B1, ADAM_B2, ADAM_EPS, ADAM_WD, ADAM_STEP = 0.001, 0.9, 0.999, 1e-08, 0.01, 10

LANE = 128
SUBLANE = 8
TB = 256
QB_SWA = 256
PACK_W = 1024
MM_K_MAX = 4608
MM_COLS_MAX = 1024
MM_FEATURE_ROWS = (1024, 768, 512)
MM_TOKEN_ROWS = 1152
ELEMWISE_ROWS_MAX = 256
MLA_HPS = 2
VMEM_LIMIT = 56 * 1024 * 1024
P_WIDTH = 3072
PC_SQ, PC_LX, PC_LG, PC_CQ, PC_SK, PC_SV, PC_CKV, PC_KR = 0, 1024, 1536, 2048, 2304, 2560, 2816, 2944
MIX_P = 1536


def _pcall(body, **kw):
    return pl.pallas_call(body, **kw)


def _cparams(n_grid):
    return pltpu.CompilerParams(dimension_semantics=("arbitrary",) * n_grid, vmem_limit_bytes=VMEM_LIMIT)


def _dg(a, b, ca, cb):
    return lax.dot_general(a.astype(BF16), b.astype(BF16), (((ca,), (cb,)), ((), ())),
                           preferred_element_type=F32)


@jax.custom_vjp
def _nn(a, b):
    return _dg(a, b, 1, 0)


@jax.custom_vjp
def _nt(a, b):
    return _dg(a, b, 1, 1)


@jax.custom_vjp
def _tn(a, b):
    return _dg(a, b, 0, 0)


_nn.defvjp(lambda a, b: (_nn(a, b), (a, b)), lambda r, ct: (_nt(ct, r[1]), _tn(r[0], ct)))
_nt.defvjp(lambda a, b: (_nt(a, b), (a, b)), lambda r, ct: (_nn(ct, r[1]), _tn(ct, r[0])))
_tn.defvjp(lambda a, b: (_tn(a, b), (a, b)), lambda r, ct: (_nt(r[1], ct), _nn(r[0], ct)))


@functools.partial(jax.custom_vjp, nondiff_argnums=(1, 2))
def _roll(x, shift, axis):
    return pltpu.roll(x, shift % x.shape[axis], axis)


_roll.defvjp(lambda x, shift, axis: (_roll(x, shift, axis), None),
             lambda shift, axis, _, ct: (_roll(ct, -shift, axis),))


@functools.partial(jax.custom_vjp, nondiff_argnums=(1, 2))
def _split(x, n, axis):
    w = x.shape[axis] // n
    return tuple(lax.slice_in_dim(x, i * w, (i + 1) * w, axis=axis) for i in range(n))


_split.defvjp(lambda x, n, axis: (_split(x, n, axis), None),
              lambda n, axis, _, cts: (jnp.concatenate(cts, axis=axis),))


@jax.custom_vjp
def _unstack(x):
    return tuple(x[i] for i in range(x.shape[0]))


_unstack.defvjp(lambda x: (_unstack(x), None), lambda _, cts: (jnp.stack(cts, axis=0),))


def _sig(x):
    return 0.5 * (jnp.tanh(0.5 * x) + 1.0)


def _gelu(x):
    return 0.5 * x * (1.0 + jnp.tanh(math.sqrt(2.0 / math.pi) * (x + 0.044715 * (x * x * x))))


def _rms(x, g, n):
    ms = jnp.sum(x * x, axis=-1, keepdims=True) * (1.0 / n)
    return x * lax.rsqrt(ms + EPS) * g


def _rope(y, cos, sa, sb, quarter):
    return y * cos + _roll(y, -quarter, 1) * sa + _roll(y, quarter, 1) * sb


def _softmax_rows(s, extra=None):
    m = jnp.max(s, axis=-1, keepdims=True)
    if extra is not None:
        m = jnp.maximum(m, extra)
    m = lax.stop_gradient(m)
    e = jnp.exp(s - m)
    den = jnp.sum(e, axis=-1, keepdims=True)
    if extra is not None:
        den = den + jnp.exp(extra - m)
    return e / den


class _A:
    def __init__(self, arr, block, imap, kind="row", first=None, gdtype=F32, gshape=None, gimap=None):
        self.arr, self.block, self.imap, self.kind, self.first = arr, block, imap, kind, first
        self.gdtype, self.gshape, self.gimap = gdtype, gshape, gimap


def _all_zero(*ids):
    return functools.reduce(jnp.logical_and, [i == 0 for i in ids])


def _par(arr):
    nd = arr.ndim
    return _A(arr, arr.shape, lambda *ids: (0,) * nd, "acc", first=_all_zero)


def _op_fwd(name, fn, grid, args, outs):
    n_in = len(args)

    def body(*refs):
        vals = [r[...].astype(F32) for r in refs[:n_in]]
        for r, v in zip(refs[n_in:], fn(*vals)):
            r[...] = v.astype(r.dtype)

    return _pcall(
        body, name=name, grid=grid,
        in_specs=[pl.BlockSpec(a.block, a.imap) for a in args],
        out_specs=[pl.BlockSpec(o[2], o[3]) for o in outs],
        out_shape=[jax.ShapeDtypeStruct(o[0], o[1]) for o in outs],
        compiler_params=_cparams(len(grid)),
    )(*[a.arr for a in args])


def _op_bwd(name, fn, grid, args, outs, ct_arrays, add_to_first=None):
    didx = [i for i, a in enumerate(args) if a.kind not in ("const", "fwd")]
    read = [i for i, a in enumerate(args) if a.kind != "fwd"]
    n_in, n_ct = len(read), len(outs)
    n_add = 0 if add_to_first is None else 1

    def body(*refs):
        ids = [pl.program_id(i) for i in range(len(grid))]
        vals = [jnp.zeros([d for d in a.block if d is not None], F32) for a in args]
        for i, r in zip(read, refs[:n_in]):
            vals[i] = r[...].astype(F32)

        def g(*dv):
            full = list(vals)
            for i, v in zip(didx, dv):
                full[i] = v
            return tuple(fn(*full))

        _, vjp = jax.vjp(g, *[vals[i] for i in didx])
        grads = list(vjp(tuple(r[...].astype(F32) for r in refs[n_in:n_in + n_ct])))
        if n_add:
            grads[0] = grads[0] + refs[n_in + n_ct][...]
        for gr, i, r in zip(grads, didx, refs[n_in + n_ct + n_add:]):
            a = args[i]
            if a.kind == "row":
                r[...] = gr.astype(r.dtype)
            else:
                first = a.first(*ids)

                @pl.when(first)
                def _():
                    r[...] = gr

                @pl.when(jnp.logical_not(first))
                def _():
                    r[...] += gr

    g_specs, g_shapes = [], []
    for i in didx:
        a = args[i]
        if a.kind == "row":
            g_specs.append(pl.BlockSpec(a.block, a.gimap or a.imap))
            g_shapes.append(jax.ShapeDtypeStruct(a.gshape or a.arr.shape, a.gdtype))
        else:
            g_specs.append(pl.BlockSpec(a.block, a.imap))
            g_shapes.append(jax.ShapeDtypeStruct(a.arr.shape, F32))
    return _pcall(
        body, name=name, grid=grid,
        in_specs=[pl.BlockSpec(args[i].block, args[i].imap) for i in read] + [pl.BlockSpec(o[2], o[3]) for o in outs]
        + [pl.BlockSpec(args[didx[0]].block, args[didx[0]].imap)] * n_add,
        out_specs=g_specs, out_shape=g_shapes,
        compiler_params=_cparams(len(grid)),
    )(*[args[i].arr for i in read], *ct_arrays, *([add_to_first] if n_add else []))


def _rowop(name, fn, grid, args, outs):
    res = _op_fwd(name, fn, grid, args, outs)
    return res, lambda *cts, add_to_first=None: _op_bwd(name + "_bwd", fn, grid, args, outs, cts, add_to_first)


def _pick(n, cap):
    best = None
    for t in range(LANE, cap + 1, LANE):
        if n % t == 0:
            best = t
    return best or n


def _mm(a, b, mode, out_dtype, name, epi=None, aux=None, out_split=None):
    if mode == "nn":
        (m, k), n = a.shape, b.shape[1]
    elif mode == "nt":
        (m, k), n = a.shape, b.shape[0]
    else:
        (k, m), n = a.shape, b.shape[1]
    assert k <= MM_K_MAX
    rows = next((r for r in MM_FEATURE_ROWS if m % r == 0), m) if mode == "tn" else MM_TOKEN_ROWS
    tm = rows if m % rows == 0 else m
    tn = n // out_split if out_split else _pick(n, MM_COLS_MAX)
    a_spec = pl.BlockSpec((k, tm), lambda j, i: (0, i)) if mode == "tn" else pl.BlockSpec((tm, k), lambda j, i: (i, 0))
    b_spec = pl.BlockSpec((tn, k), lambda j, i: (j, 0)) if mode == "nt" else pl.BlockSpec((k, tn), lambda j, i: (0, j))
    dims = {"nn": (1, 0), "nt": (1, 1), "tn": (0, 0)}[mode]
    aux_spec = pl.BlockSpec((tm, tn), lambda j, i: (i, j))
    if out_split:
        o_spec, o_shape = pl.BlockSpec((None, tm, tn), lambda j, i: (j, i, 0)), (out_split, m, tn)
    else:
        o_spec, o_shape = aux_spec, (m, n)
    n_aux = 0 if aux is None else 1
    n_out = 2 if epi == "sqrelu" else 1

    def body(*refs):
        o_refs = refs[2 + n_aux:]
        r = _dg(refs[0][...], refs[1][...], *dims)
        if epi == "sqrelu":
            o_refs[0][...] = r.astype(o_refs[0].dtype)
            rl = jnp.maximum(r, 0.0)
            o_refs[1][...] = (rl * rl).astype(o_refs[1].dtype)
        elif epi == "dsqrelu":
            pre = refs[2][...].astype(F32)
            o_refs[0][...] = (r * (2.0 * jnp.maximum(pre, 0.0))).astype(o_refs[0].dtype)
        else:
            o_refs[0][...] = r.astype(o_refs[0].dtype)

    res = _pcall(
        body, name=name, grid=(n // tn, m // tm),
        in_specs=[a_spec, b_spec] + [aux_spec] * n_aux, out_specs=[o_spec] * n_out,
        out_shape=[jax.ShapeDtypeStruct(o_shape, out_dtype)] * n_out, compiler_params=_cparams(2),
    )(a, b, *([aux] if aux is not None else []))
    return res if n_out == 2 else res[0]


ROW_CHUNK = 16


def _softmax_chunks(s_scr, n_keys, scale, emit):
    for r0 in range(0, s_scr.shape[0], ROW_CHUNK):
        rows = slice(r0, r0 + ROW_CHUNK)
        s = s_scr[rows, :n_keys]
        e = jnp.exp((s - jnp.max(s, axis=-1, keepdims=True)) * scale)
        emit(rows, e, 1.0 / jnp.sum(e, axis=-1, keepdims=True))


def _attn_fwd_block(v, n, scale, s_scr, e_scr, l_scr):
    def emit(rows, e, inv_l):
        e_scr[rows, :n] = e.astype(BF16)
        l_scr[rows, :] = jnp.broadcast_to(inv_l, (ROW_CHUNK, LANE))

    _softmax_chunks(s_scr, n, scale, emit)
    return _dg(e_scr[:, :n], v, 1, 0) * l_scr[...]


def _attn_bwd_block(q, k, o, do, scale, s_scr, dp_scr, p_scr, ds_scr):
    n = k.shape[0]

    def emit(rows, e, inv_l):
        p = e * inv_l
        delta = jnp.sum(do[rows, :] * o[rows, :], axis=-1, keepdims=True)
        p_scr[rows, :n] = p.astype(BF16)
        ds_scr[rows, :n] = (p * (dp_scr[rows, :n] - delta) * scale).astype(BF16)

    _softmax_chunks(s_scr, n, scale, emit)
    ds = ds_scr[:, :n]
    return _dg(ds, k, 1, 0), _dg(ds, q, 0, 0), _dg(p_scr[:, :n], do, 0, 0)


def _call_with_exchange(body, xchg, *, name, grid, in_specs, out_specs, out_shape, operands, scratch_shapes=()):
    if xchg is None:
        res = _pcall(body, name=name, grid=grid, in_specs=in_specs, out_specs=out_specs, out_shape=out_shape,
                     scratch_shapes=list(scratch_shapes), compiler_params=_cparams(len(grid)))(*operands)
        return list(res), []
    n_in, n_out, n_sc, n = len(in_specs), len(out_specs), len(scratch_shapes), xchg.n

    def wrapped(*refs):
        ins, x_refs = refs[:n_in], refs[n_in:n_in + n]
        outs, xo_refs = refs[n_in + n:n_in + n + n_out], refs[n_in + n + n_out:n_in + 2 * n + n_out]
        scratch, sems = refs[n_in + 2 * n + n_out:n_in + 2 * n + n_out + n_sc], refs[n_in + 2 * n + n_out + n_sc:]
        ids = [pl.program_id(i) for i in range(len(grid))]

        @pl.when(functools.reduce(jnp.logical_and, [i == 0 for i in ids]))
        def _():
            xchg.start(x_refs, xo_refs, sems)

        body(*ins, *outs, *scratch)

        @pl.when(functools.reduce(jnp.logical_and, [i == g - 1 for i, g in zip(ids, grid)]))
        def _():
            xchg.wait(x_refs, xo_refs, sems)

    res = _pcall(wrapped, name=name, grid=grid, in_specs=list(in_specs) + xchg.specs,
                 out_specs=list(out_specs) + xchg.specs, out_shape=list(out_shape) + xchg.out_shape,
                 scratch_shapes=list(scratch_shapes) + xchg.scratch, compiler_params=_cparams(len(grid)),
                 )(*operands, *xchg.bufs)
    return list(res[:n_out]), list(res[n_out:])


def _head_half(i, shape):
    lane = lax.broadcasted_iota(jnp.int32, shape, len(shape) - 1)
    return (lane < LANE // 2) if i == 0 else (lane >= LANE // 2)


def _mla_attn(q, k, v, tc, ctx_q, name, xchg=None):
    assert MLA_HPS == 2 and MLA_V == LANE // 2
    bsz, t_all, _ = q.shape
    n_t = t_all // TB
    grid = (bsz, MLA_HEADS // MLA_HPS, n_t)
    q_spec = pl.BlockSpec((None, TB, MLA_HPS * LANE), lambda b, h, t: (b, t, h))
    k_spec = pl.BlockSpec((None, t_all, MLA_HPS * LANE), lambda b, h, t: (b, 0, h))
    v_spec = pl.BlockSpec((None, t_all, LANE), lambda b, h, t: (b, 0, h))
    o_spec = pl.BlockSpec((None, TB, LANE), lambda b, h, t: (b, t, h))
    heads = [slice(i * LANE, (i + 1) * LANE) for i in range(MLA_HPS)]
    scale = MLA_QK ** -0.5
    f32_scr, bf16_scr = pltpu.VMEM((TB, t_all), F32), pltpu.VMEM((TB, t_all), BF16)
    o_shape = jax.ShapeDtypeStruct(v.shape, F32)

    def fwd_body(q_ref, k_ref, v_ref, o_ref, *scr):
        t = pl.program_id(2)

        def run(keys):
            n = keys.stop
            for i, hs in enumerate(heads):
                scr[3 * i][:, :n] = _dg(q_ref[:, hs], k_ref[keys, hs], 1, 1)
            both = [_attn_fwd_block(v_ref[keys, :], n, scale, *scr[3 * i:3 * i + 3]) for i in range(MLA_HPS)]
            o_ref[...] = jnp.where(_head_half(0, both[0].shape), both[0], both[1])

        @pl.when(t == 0)
        def _():
            if ctx_q:
                run(slice(0, tc))
            else:
                o_ref[...] = jnp.zeros_like(o_ref)

        @pl.when(t > 0)
        def _():
            run(slice(0, t_all))

    (o,), gathered = _call_with_exchange(
        fwd_body, xchg, name=name, grid=grid, in_specs=[q_spec, k_spec, v_spec], out_specs=[o_spec],
        out_shape=[o_shape], operands=(q, k, v),
        scratch_shapes=[f32_scr, bf16_scr, pltpu.VMEM((TB, LANE), F32)] * MLA_HPS)

    def bwd(do, xchg=None):
        def bwd_body(q_ref, k_ref, v_ref, o_ref, do_ref, dq_ref, dk_ref, dv_ref, *scr):
            t = pl.program_id(2)

            def run(keys, first):
                n = keys.stop
                dos = [jnp.where(_head_half(i, do_ref.shape), do_ref[...], 0.0) for i in range(MLA_HPS)]
                for i, hs in enumerate(heads):
                    scr[4 * i][:, :n] = _dg(q_ref[:, hs], k_ref[keys, hs], 1, 1)
                    scr[4 * i + 1][:, :n] = _dg(dos[i], v_ref[keys, :], 1, 1)
                dvs = []
                for i, hs in enumerate(heads):
                    dq, dk, dv = _attn_bwd_block(q_ref[:, hs], k_ref[keys, hs], o_ref[...], dos[i], scale,
                                                 *scr[4 * i:4 * i + 4])
                    dq_ref[:, hs] = dq
                    dvs.append(dv)
                    if first:
                        dk_ref[keys, hs] = dk
                    else:
                        dk_ref[keys, hs] += dk
                if first:
                    dv_ref[keys, :] = dvs[0] + dvs[1]
                else:
                    dv_ref[keys, :] += dvs[0] + dvs[1]

            @pl.when(t == 0)
            def _():
                dk_ref[...] = jnp.zeros_like(dk_ref)
                dv_ref[...] = jnp.zeros_like(dv_ref)
                if ctx_q:
                    run(slice(0, tc), True)
                else:
                    dq_ref[...] = jnp.zeros_like(dq_ref)

            @pl.when(t > 0)
            def _():
                run(slice(0, t_all), False)

        return _call_with_exchange(
            bwd_body, xchg, name=name + "_bwd", grid=grid, in_specs=[q_spec, k_spec, v_spec, o_spec, o_spec],
            out_specs=[q_spec, k_spec, v_spec],
            out_shape=[jax.ShapeDtypeStruct(q.shape, F32), jax.ShapeDtypeStruct(q.shape, F32), o_shape],
            operands=(q, k, v, o, do), scratch_shapes=[f32_scr, f32_scr, bf16_scr, bf16_scr] * MLA_HPS)

    return o, gathered, bwd


def _swa_block(q, keys, vals, sink, mask):
    qs = jnp.concatenate(list(_split(q, SWA_GROUP, 1)), axis=0)
    sk = jnp.sum(sink, axis=-1, keepdims=True) * (1.0 / LANE)
    s = _nt(qs, keys) * (SWA_HEAD_DIM ** -0.5)
    if mask is not None:
        s = jnp.where(mask, s, NEG_INF)
    o = _split(_nn(_softmax_rows(s, sk), vals + _roll(vals, LANE // 2, 1)), SWA_GROUP, 0)
    low = _head_half(0, o[0].shape)
    return jnp.concatenate([jnp.where(low, o[0], o[1]), jnp.where(low, o[2], o[3])], axis=1)


def _swa_ctx_block(q, kc, vc, sink):
    return _swa_block(q, kc, vc, sink, None)


def _swa_win_block(q, kc, kw, vc, vw, sink, mask):
    return _swa_block(q, jnp.concatenate([kc, kw], axis=0), jnp.concatenate([vc, vw], axis=0), sink, mask)


def _swa_attn(q, k, p_all, sink_b, tc, ctx_q, name, xchg=None):
    bsz, t_all, _ = q.shape
    n_q = t_all // QB_SWA
    n_cq = tc // QB_SWA
    lat = t_all - tc
    span = QB_SWA + 2 * WINDOW
    gw = SWA_GROUP * LANE
    grid = (bsz, SWA_KV_HEADS, n_q)
    q_spec = pl.BlockSpec((None, QB_SWA, gw), lambda b, g, i: (b, i, g))
    k_spec = pl.BlockSpec((None, t_all, LANE), lambda b, g, i: (b, 0, g))
    v_spec = pl.BlockSpec((None, t_all, LANE), lambda b, g, i: (b, 0, PC_SV // LANE + g))
    s_spec = pl.BlockSpec((None, SWA_GROUP * QB_SWA, LANE), lambda b, g, i: (g, 0, 0))

    def window(i):
        q0 = (i - n_cq) * QB_SWA
        w0 = jnp.clip(q0 - WINDOW, 0, lat - span)
        w0 = pl.multiple_of(w0, WINDOW)
        shape = (SWA_GROUP * QB_SWA, tc + span)
        qi = q0 + lax.broadcasted_iota(jnp.int32, shape, 0) % QB_SWA
        col = lax.broadcasted_iota(jnp.int32, shape, 1)
        kj = w0 + col - tc
        mask = (col < tc) | ((kj >= qi - WINDOW) & (kj <= qi + WINDOW))
        return w0, mask

    def fwd_body(q_ref, k_ref, v_ref, s_ref, o_ref):
        i = pl.program_id(2)

        @pl.when(i < n_cq)
        def _():
            if ctx_q:
                o_ref[...] = _swa_ctx_block(q_ref[...].astype(F32), k_ref[0:tc, :], v_ref[0:tc, :].astype(F32),
                                            s_ref[...])
            else:
                o_ref[...] = jnp.zeros_like(o_ref)

        @pl.when(i >= n_cq)
        def _():
            w0, mask = window(i)
            o_ref[...] = _swa_win_block(q_ref[...].astype(F32), k_ref[0:tc, :], k_ref[pl.ds(tc + w0, span), :],
                                        v_ref[0:tc, :].astype(F32), v_ref[pl.ds(tc + w0, span), :].astype(F32),
                                        s_ref[...], mask)

    o_spec = pl.BlockSpec((None, QB_SWA, SWA_GROUP * SWA_HEAD_DIM), lambda b, g, i: (b, i, g))
    (o,), gathered = _call_with_exchange(
        fwd_body, xchg, name=name, grid=grid, in_specs=[q_spec, k_spec, v_spec, s_spec], out_specs=[o_spec],
        out_shape=[jax.ShapeDtypeStruct((bsz, t_all, SWA_HEADS * SWA_HEAD_DIM), F32)],
        operands=(q, k, p_all, sink_b))

    def bwd(do, xchg=None):
        def bwd_body(q_ref, k_ref, v_ref, s_ref, do_ref, dq_ref, dk_ref, dv_ref, ds_ref):
            i = pl.program_id(2)

            @pl.when(i == 0)
            def _():
                dk_ref[...] = jnp.zeros_like(dk_ref)
                dv_ref[...] = jnp.zeros_like(dv_ref)
                ds_ref[...] = jnp.zeros_like(ds_ref)

            @pl.when(i < n_cq)
            def _():
                if ctx_q:
                    _, vjp = jax.vjp(_swa_ctx_block, q_ref[...].astype(F32), k_ref[0:tc, :].astype(F32),
                                     v_ref[0:tc, :].astype(F32), s_ref[...])
                    dq, dk, dv, ds = vjp(do_ref[...])
                    dq_ref[...] = dq
                    dk_ref[0:tc, :] += dk
                    dv_ref[0:tc, :] += dv
                    ds_ref[...] += ds
                else:
                    dq_ref[...] = jnp.zeros_like(dq_ref)

            @pl.when(i >= n_cq)
            def _():
                w0, mask = window(i)
                win = pl.ds(tc + w0, span)
                _, vjp = jax.vjp(functools.partial(_swa_win_block, mask=mask), q_ref[...].astype(F32),
                                 k_ref[0:tc, :].astype(F32), k_ref[win, :].astype(F32),
                                 v_ref[0:tc, :].astype(F32), v_ref[win, :].astype(F32), s_ref[...])
                dq, dkc, dkw, dvc, dvw, ds = vjp(do_ref[...])
                dq_ref[...] = dq
                dk_ref[0:tc, :] += dkc
                dk_ref[win, :] += dkw
                dv_ref[0:tc, :] += dvc
                dv_ref[win, :] += dvw
                ds_ref[...] += ds

        kv_out = pl.BlockSpec((None, t_all, LANE), lambda b, g, i: (b, 0, g))
        ds_spec = pl.BlockSpec((None, None, SWA_GROUP * QB_SWA, LANE), lambda b, g, i: (b, g, 0, 0))
        kv_shape = jax.ShapeDtypeStruct((bsz, t_all, SWA_KV_HEADS * LANE), F32)
        return _call_with_exchange(
            bwd_body, xchg, name=name + "_bwd", grid=grid, in_specs=[q_spec, k_spec, v_spec, s_spec, o_spec],
            out_specs=[q_spec, kv_out, kv_out, ds_spec],
            out_shape=[jax.ShapeDtypeStruct(q.shape, F32), kv_shape, kv_shape,
                       jax.ShapeDtypeStruct((bsz,) + sink_b.shape, F32)],
            operands=(q, k, p_all, sink_b, do))

    return o, gathered, bwd


def _scan_pair(chains, scratch):
    t_all, c = chains[0][0].shape
    n_tiles = t_all // SUBLANE
    row8 = lax.broadcasted_iota(jnp.int32, (t_all, c), 0) % SUBLANE
    refs = [scratch[0:3], scratch[3:6]]
    for (a, u, reverse), (a_s, u_s, _) in zip(chains, refs):
        for d in (1, 2, 4):
            sh = d if not reverse else t_all - d
            ar, ur = pltpu.roll(a, sh, 0), pltpu.roll(u, sh, 0)
            m = (row8 >= d) if not reverse else (row8 < SUBLANE - d)
            u = jnp.where(m, a * ur + u, u)
            a = jnp.where(m, a * ar, a)
        a_s[...] = a
        u_s[...] = u

    def step(j, carries):
        out = []
        for (_, _, reverse), (a_s, u_s, c_s), carry in zip(chains, refs, carries):
            tile = j if not reverse else n_tiles - 1 - j
            base = pl.multiple_of(tile * SUBLANE, SUBLANE)
            c_s[pl.ds(base, SUBLANE), :] = jnp.broadcast_to(carry, (SUBLANE, c))
            last = base + (0 if reverse else SUBLANE - 1)
            out.append(a_s[pl.ds(last, 1), :] * carry + u_s[pl.ds(last, 1), :])
        return tuple(out)

    lax.fori_loop(0, n_tiles, step, (jnp.zeros((1, c), F32),) * 2, unroll=4)
    return [a_s[...] * c_s[...] + u_s[...] for a_s, u_s, c_s in refs]


def _shift_rows(x, reverse_src):
    t_all = x.shape[0]
    row = lax.broadcasted_iota(jnp.int32, x.shape, 0)
    if reverse_src:
        return jnp.where(row == t_all - 1, 0.0, pltpu.roll(x, t_all - 1, 0))
    return jnp.where(row == 0, 0.0, pltpu.roll(x, 1, 0))


def _lru_scan(a0, u0, a1, u1, name):
    bsz, t_all, w = a0.shape
    grid = (bsz, w // LANE)
    spec = pl.BlockSpec((None, t_all, LANE), lambda b, c: (b, 0, c))
    scratch = [pltpu.VMEM((t_all, LANE), F32)] * 6
    shape = jax.ShapeDtypeStruct(a0.shape, F32)

    def fwd_body(a0_ref, u0_ref, a1_ref, u1_ref, h0_ref, h1_ref, *scr):
        h0_ref[...], h1_ref[...] = _scan_pair([(a0_ref[...], u0_ref[...], False), (a1_ref[...], u1_ref[...], True)],
                                              scr)

    h0, h1 = _pcall(fwd_body, name=name, grid=grid, in_specs=[spec] * 4, out_specs=[spec] * 2,
                    out_shape=[shape] * 2, scratch_shapes=scratch, compiler_params=_cparams(2))(a0, u0, a1, u1)

    def bwd(dh0, dh1):
        def bwd_body(a0_ref, h0_ref, g0_ref, a1_ref, h1_ref, g1_ref, da0_ref, du0_ref, da1_ref, du1_ref, *scr):
            g0, g1 = _scan_pair([(_shift_rows(a0_ref[...], True), g0_ref[...], True),
                                 (_shift_rows(a1_ref[...], False), g1_ref[...], False)], scr)
            du0_ref[...] = g0
            da0_ref[...] = g0 * _shift_rows(h0_ref[...], False)
            du1_ref[...] = g1
            da1_ref[...] = g1 * _shift_rows(h1_ref[...], True)

        return _pcall(bwd_body, name=name + "_bwd", grid=grid, in_specs=[spec] * 6, out_specs=[spec] * 4,
                      out_shape=[shape] * 4, scratch_shapes=scratch,
                      compiler_params=_cparams(2))(a0, h0, dh0, a1, h1, dh1)

    return h0, h1, bwd


def _f_mod(x, g, shift, scale):
    return (_rms(x, g, D_MODEL) * (1.0 + scale) + shift,)


def _f_mla_q(cq, ga, w, gh, cos, sa, sb):
    n = _rms(cq, ga, MLA_Q_RANK)
    outs = []
    for wh in _split(w, MLA_HEADS, 1):
        outs.append(_rope(_rms(_nn(n, wh), gh, MLA_QK), cos, sa, sb, MLA_ROPE // 4))
    return (jnp.concatenate(outs, axis=1),)


def _f_mla_kv(ckv, krp, ga, wk, wv, gh, cos, sa, sb):
    n = _rms(ckv, ga, MLA_KV_RANK)
    outs = []
    for wh in _split(wk, MLA_HEADS, 1):
        outs.append(_rope(_rms(_nn(n, wh) + krp, gh, MLA_QK), cos, sa, sb, MLA_ROPE // 4))
    return jnp.concatenate(outs, axis=1), _nn(n, wv)


def _f_conv(x, w0, w1, w2, w3, bias, tc):
    t_all = x.shape[0]
    row = lax.broadcasted_iota(jnp.int32, x.shape, 0)
    lo = jnp.where(row < tc, 0, tc)
    hi = jnp.where(row < tc, tc, t_all)
    y = bias + jnp.zeros_like(x)
    for kk, wk in enumerate((w0, w1, w2, w3)):
        src = row + (kk - 2)
        xs = x if kk == 2 else _roll(x, 2 - kk, 0)
        y = y + wk * jnp.where((src >= lo) & (src < hi), xs, 0.0)
    return (y,)


def _f_gates(xc, w16, b00, b01, b10, b11, sp0, sp1):
    ws = _unstack(w16)
    n_cb = LRU_WIDTH // LANE
    xcs = _split(xc, n_cb, 1)
    bias = [_split(b, n_cb, 1) for b in (b00, b01, b10, b11)]
    sps = [_split(s, n_cb, 1) for s in (sp0, sp1)]
    res = [[], [], [], []]
    for c in range(n_cb):
        for z in range(2):
            r = _sig(_nn(xcs[c], ws[c * 4 + 2 * z]) + bias[2 * z][c])
            i = _sig(_nn(xcs[c], ws[c * 4 + 2 * z + 1]) + bias[2 * z + 1][c])
            la = -LRU_C * r * sps[z][c]
            res[2 * z].append(jnp.exp(la))
            res[2 * z + 1].append(jnp.sqrt(-jnp.tanh(la) * (jnp.exp(2.0 * la) + 1.0)) * (i * xcs[c]))
    return tuple(jnp.concatenate(r, axis=1) for r in res)


def _f_swa_qk(sq, sk, gq, gk, cos, sa, sb):
    qs = [_rope(_rms(x, gq, SWA_HEAD_DIM), cos, sa, sb, SWA_HEAD_DIM // 4) for x in _split(sq, SWA_HEADS, 1)]
    ks = [_rope(_rms(x, gk, SWA_HEAD_DIM), cos, sa, sb, SWA_HEAD_DIM // 4) for x in _split(sk, SWA_KV_HEADS, 1)]
    return jnp.concatenate(qs, axis=1), jnp.concatenate(ks, axis=1)


def _f_qkv(cq, ckv, krp, sq, sk, q_a_g, wuq, mla_q_g, kv_a_g, wk, wv, mla_k_g, swa_q_g, swa_k_g,
           m_cos, m_sa, m_sb, s_cos, s_sa, s_sb):
    return (*_f_mla_q(cq, q_a_g, wuq, mla_q_g, m_cos, m_sa, m_sb),
            *_f_mla_kv(ckv, krp, kv_a_g, wk, wv, mla_k_g, m_cos, m_sa, m_sb),
            *_f_swa_qk(sq, sk, swa_q_g, swa_k_g, s_cos, s_sa, s_sb))


def _f_merge(oa, h0, h1, lg, oc, ga, gb, gc):
    ob = (h0 + h1) * _gelu(lg)
    return (jnp.concatenate([_rms(oa, ga, GROUP_WIDTH), _rms(ob, gb, GROUP_WIDTH), _rms(oc, gc, GROUP_WIDTH)],
                            axis=1),)


def _f_resid_mod(x, y, gate, g, shift, scale):
    x1 = x + gate * y
    return x1, _rms(x1, g, D_MODEL) * (1.0 + scale) + shift


def _f_resid(x, y, gate):
    return (x + gate * y,)


def _hosted(hooks, key, arg=None):
    make, done = hooks.get(key, (None, None))
    xchg = make(arg) if make is not None else None
    return xchg, (done if xchg is not None else lambda outs: None)


def _layer(li, x, mods, w, s, tabs, tc, ctx_q, hooks, latent_dx_only):
    bsz, t_all, _ = x.shape
    n_t = t_all // TB
    grid = (bsz, n_t)
    rows = lambda b, t: (b, t, 0)

    def row(arr, width=None, idx=0, gdtype=F32, gshape=None):
        width = width or arr.shape[-1]
        return _A(arr, (None, TB, width), lambda b, t: (b, t, idx), "row", gdtype=gdtype, gshape=gshape,
                  gimap=rows if gshape is not None else None)

    def out(width, dtype, imap=rows):
        return ((bsz, t_all, width), dtype, (None, TB, width), imap)

    def modarg(arr):
        return _A(arr, (None, None, 1, D_MODEL), lambda b, t: (b, jnp.minimum(t, 1), 0, 0), "acc",
                  first=lambda b, t: t <= 1)

    def tab(arr):
        return _A(arr, (TB, LANE), lambda b, t: (t, 0), "const")

    def pcol(p_all, col, width):
        return row(p_all, width, col // width, gdtype=BF16, gshape=(bsz, t_all, width))

    nm = lambda base: "%s_l%d" % (base, li)
    sh1, sc1, g1, sh2, sc2, g2 = mods
    m_all = bsz * t_all

    x_arg = row(x)
    if latent_dx_only:
        n_c = tc // TB
        x_arg.gshape, x_arg.gimap = (bsz, t_all - tc, D_MODEL), lambda b, t: (b, jnp.maximum(t - n_c, 0), 0)
    (h,), b_mod1 = _rowop(nm("mod1"), _f_mod, grid, [x_arg, _par(s["norm1_g"]), modarg(sh1), modarg(sc1)],
                          [out(D_MODEL, BF16)])
    p_all = _mm(h.reshape(m_all, D_MODEL), w["win"], "nn", BF16, nm("mm_in")).reshape(bsz, t_all, P_WIDTH)

    (q_a, k_a, v_a, q_c, k_c), b_qkv = _rowop(
        nm("qkv"), _f_qkv, grid,
        [pcol(p_all, PC_CQ, 256), pcol(p_all, PC_CKV, 128), pcol(p_all, PC_KR, 128), pcol(p_all, PC_SQ, 1024),
         pcol(p_all, PC_SK, 256)]
        + [_par(a) for a in (s["q_a_g"], w["wuq"], s["mla_q_g"], s["kv_a_g"], w["wk"], w["wv"], s["mla_k_g"],
                             s["swa_q_g"], s["swa_k_g"])]
        + [tab(a) for a in tabs["mla"] + tabs["swa"]],
        [out(MLA_HEADS * LANE, BF16), out(MLA_HEADS * LANE, BF16), out(MLA_HEADS * MLA_V, BF16),
         out(SWA_HEADS * LANE, BF16), out(SWA_KV_HEADS * LANE, BF16)])

    xchg, done = _hosted(hooks, "mla_fwd")
    o_a, got, b_attn_a = _mla_attn(q_a, k_a, v_a, tc, ctx_q, nm("mla_attn"), xchg)
    done(got)

    n_cb = LRU_WIDTH // LANE
    conv_grid = (n_cb, bsz)
    cpar = lambda arr: _A(arr, (1, LANE), lambda c, b: (0, c), "acc", first=lambda c, b: b == 0)
    conv_args = [_A(p_all, (None, t_all, LANE), lambda c, b: (b, 0, PC_LX // LANE + c), "row", gdtype=BF16,
                    gshape=(bsz, t_all, LRU_WIDTH), gimap=lambda c, b: (b, 0, c))]
    conv_args += [cpar(a) for a in s["conv_w"]] + [cpar(s["conv_b"])]
    conv_out = [((bsz, t_all, LRU_WIDTH), F32, (None, t_all, LANE), lambda c, b: (b, 0, c))]
    (xc,), b_conv = _rowop(nm("lru_conv"), functools.partial(_f_conv, tc=tc), conv_grid, conv_args, conv_out)
    rot = lambda b, t: (b, (t + n_t - 1) % n_t, 0)
    (a0, u0, a1, u1), b_gates = _rowop(
        nm("lru_gates"), _f_gates, grid,
        [row(xc), _par(s["wbd"])] + [_par(a) for a in s["gate_b"]] + [_par(a) for a in s["sp"]],
        [out(LRU_WIDTH, F32), out(LRU_WIDTH, F32), out(LRU_WIDTH, F32, rot), out(LRU_WIDTH, F32, rot)])
    h0, h1, b_scan = _lru_scan(a0, u0, a1, u1, nm("lru_scan"))
    h1_arg = _A(h1, (None, TB, LRU_WIDTH), rot, "row")

    xchg, done = _hosted(hooks, "swa_fwd")
    o_c, got, b_attn_c = _swa_attn(q_c, k_c, p_all, s["sink_b"], tc, ctx_q, nm("swa_attn"), xchg)
    done(got)

    (y_in,), b_merge = _rowop(nm("merge"), _f_merge, grid,
                              [row(o_a), row(h0), h1_arg, pcol(p_all, PC_LG, 512), row(o_c), _par(s["g_a"]),
                               _par(s["g_b"]), _par(s["g_c"])],
                              [out(MIX_P, BF16)])
    y = _mm(y_in.reshape(m_all, MIX_P), w["wout"], "nn", F32, nm("mm_out")).reshape(bsz, t_all, D_MODEL)
    (x1, hm), b_rm = _rowop(nm("resid_mod"), _f_resid_mod, grid,
                            [row(x), row(y, gdtype=BF16), modarg(g1), _par(s["norm2_g"]), modarg(sh2), modarg(sc2)],
                            [out(D_MODEL, F32), out(D_MODEL, BF16)])
    pre, act = _mm(hm.reshape(m_all, D_MODEL), w["ff1"], "nn", BF16, nm("mm_ff1"), epi="sqrelu")
    y2 = _mm(act, w["ff2"], "nn", F32, nm("mm_ff2")).reshape(bsz, t_all, D_MODEL)
    (x2,), b_res = _rowop(nm("resid"), _f_resid, grid,
                          [_A(x1, (None, TB, D_MODEL), rows, "fwd"), row(y2, gdtype=BF16), modarg(g2)],
                          [out(D_MODEL, F32)])

    def bwd(dx2, hooks):
        dw, ds = {}, {}
        dy2, dg2 = b_res(dx2)
        dy2 = dy2.reshape(m_all, D_MODEL)
        dpre = _mm(dy2, w["ff2"], "nt", BF16, nm("mm_ff2_dx"), epi="dsqrelu", aux=pre)
        dw["ff2"] = _mm(act, dy2, "tn", BF16, nm("mm_ff2_dw"))
        dhm = _mm(dpre, w["ff1"], "nt", F32, nm("mm_ff1_dx")).reshape(bsz, t_all, D_MODEL)
        dw["ff1"] = _mm(hm.reshape(m_all, D_MODEL), dpre, "tn", BF16, nm("mm_ff1_dw"), out_split=N_DEV)
        dxa, dy, dg1, ds["norm2_g"], dsh2, dsc2 = b_rm(dx2, dhm)
        dy = dy.reshape(m_all, D_MODEL)
        dy_in = _mm(dy, w["wout"], "nt", F32, nm("mm_out_dx")).reshape(bsz, t_all, MIX_P)
        dw["wout"] = _mm(y_in.reshape(m_all, MIX_P), dy, "tn", BF16, nm("mm_out_dw"))
        do_a, dh0, dh1, dlg, do_c, ds["g_a"], ds["g_b"], ds["g_c"] = b_merge(dy_in)

        (dq_c, dk_c, dsv, dsink), _ = b_attn_c(do_c)
        ds["sink_b"] = jnp.sum(dsink, axis=0)

        da0, du0, da1, du1 = b_scan(dh0, dh1)
        gates_g = b_gates(da0, du0, da1, du1)
        dxc, ds["wbd"] = gates_g[0], gates_g[1]
        ds["gate_b"], ds["sp"] = list(gates_g[2:6]), list(gates_g[6:8])
        conv_g = b_conv(dxc)
        dlx, ds["conv_w"], ds["conv_b"] = conv_g[0], list(conv_g[1:5]), conv_g[5]

        xchg, done = _hosted(hooks, "mla_bwd", (dw, ds))
        (dq_a, dk_a, dv_a), got = b_attn_a(do_a, xchg)
        done(got)
        (dcq, dckv, dkr, dsq, dsk, ds["q_a_g"], dw["wuq"], ds["mla_q_g"], ds["kv_a_g"], dw["wk"], dw["wv"],
         ds["mla_k_g"], ds["swa_q_g"], ds["swa_k_g"]) = b_qkv(dq_a, dk_a, dv_a, dq_c, dk_c)

        dp = jnp.concatenate([dsq, dlx, dlg, dcq, dsk, dsv.astype(BF16), dckv, dkr], axis=-1)
        dp = dp.reshape(m_all, P_WIDTH)
        dh = _mm(dp, w["win"], "nt", F32, nm("mm_in_dx")).reshape(bsz, t_all, D_MODEL)
        dw["win"] = _mm(h.reshape(m_all, D_MODEL), dp, "tn", BF16, nm("mm_in_dw"))
        dx, ds["norm1_g"], dsh1, dsc1 = b_mod1(dh, add_to_first=dxa)
        return dx, [dsh1, dsc1, dg1, dsh2, dsc2, dg2], dw, ds

    return x2, bwd


def _loss_and_grad(x2, target, tc):
    bsz, t_all, d = x2.shape
    n_t = t_all // TB
    n_c = tc // TB

    def body(x_ref, t_ref, l_ref, dx_ref):
        b, t = pl.program_id(0), pl.program_id(1)

        @pl.when((b == 0) & (t == 0))
        def _():
            l_ref[...] = jnp.zeros_like(l_ref)

        @pl.when(t < n_c)
        def _():
            dx_ref[...] = jnp.zeros_like(dx_ref)

        @pl.when(t >= n_c)
        def _():
            e = x_ref[...] - t_ref[...]
            dx_ref[...] = e * (1.0 / d)
            l_ref[...] += jnp.sum(e * e) * (0.5 / d)

    loss, dx = _pcall(
        body, name="loss", grid=(bsz, n_t),
        in_specs=[pl.BlockSpec((None, TB, d), lambda b, t: (b, t, 0)),
                  pl.BlockSpec((None, TB, d), lambda b, t: (b, jnp.maximum(t - n_c, 0), 0))],
        out_specs=[pl.BlockSpec((SUBLANE, LANE), lambda b, t: (0, 0)),
                   pl.BlockSpec((None, TB, d), lambda b, t: (b, t, 0))],
        out_shape=[jax.ShapeDtypeStruct((SUBLANE, LANE), F32), jax.ShapeDtypeStruct(x2.shape, F32)],
        compiler_params=_cparams(2))(x2, target)
    return loss[0, 0], dx


def _rope_tables(lat, tc, dim, lane0):
    quarter = dim // 4
    pos = np.arange(lat)
    grid_pos = np.stack([pos // GRID_W, pos % GRID_W], axis=-1).astype(np.float32)
    lane = np.arange(LANE)
    p = np.clip(lane - lane0, 0, dim - 1)
    active = (lane >= lane0) & (lane < lane0 + dim)
    axis, half, qi = p // (dim // 2), (p % (dim // 2)) // quarter, p % quarter
    inv = (np.float32(ROPE_THETA) ** (-qi.astype(np.float32) / np.float32(quarter))).astype(np.float32)
    ang = (np.where(axis[None, :] == 0, grid_pos[:, 0:1], grid_pos[:, 1:2]) * inv[None, :]).astype(np.float32)
    cos = np.where(active, np.cos(ang), 1.0).astype(np.float32)
    sin = np.where(active, np.sin(ang), 0.0).astype(np.float32)
    sa = np.where(half == 0, -sin, 0.0).astype(np.float32)
    sb = np.where(half == 1, sin, 0.0).astype(np.float32)
    ctx1, ctx0 = np.ones((tc, LANE), np.float32), np.zeros((tc, LANE), np.float32)
    return tuple(jnp.asarray(np.concatenate([c, t], 0)) for c, t in ((ctx1, cos), (ctx0, sa), (ctx0, sb)))


_BIG = {"w_in": ((D_MODEL, IN_WIDTH // N_DEV), 1, ("win",)),
        "w_uq": ((MLA_Q_RANK, MLA_HEADS * MLA_QK // N_DEV), 1, ("wuq",)),
        "w_ukv": ((MLA_KV_RANK, MLA_HEADS * (MLA_NOPE + MLA_V) // N_DEV), 1, ("wk", "wv")),
        "w_out": ((3 * GROUP_WIDTH // N_DEV, D_MODEL), 0, ("wout",)),
        "w_ff1": ((D_MODEL, D_FF // N_DEV), 1, ("ff1",)),
        "w_ff2": ((D_FF // N_DEV, D_MODEL), 0, ("ff2",))}
_EARLY = ("w_in", "w_uq", "w_ukv")
_LATE = ("w_out", "w_ff1", "w_ff2")


def _pad_heads(wm, n_heads, dim):
    out = jnp.pad(wm.reshape(wm.shape[0], n_heads, dim), ((0, 0), (0, 0), (0, LANE - dim)))
    return out.reshape(wm.shape[0], n_heads * LANE)


def _prep_weight(name, piece):
    shp, ax, _ = _BIG[name]
    full = jnp.moveaxis(piece, 0, ax).reshape(shp[:ax] + (N_DEV * shp[ax],) + shp[ax + 1:])
    if name == "w_in":
        cq, ckv, kr, lx, lg, sq, sk, sv = _split_cols(full)
        return {"win": jnp.concatenate(
            [_pad_heads(sq, SWA_HEADS, SWA_HEAD_DIM), lx, lg, cq, _pad_heads(sk, SWA_KV_HEADS, SWA_HEAD_DIM),
             _pad_heads(sv, SWA_KV_HEADS, SWA_HEAD_DIM), ckv, jnp.pad(kr, ((0, 0), (MLA_NOPE, LANE - MLA_QK)))], axis=1)}
    if name == "w_uq":
        return {"wuq": _pad_heads(full, MLA_HEADS, MLA_QK)}
    if name == "w_ukv":
        ukv = full.reshape(MLA_KV_RANK, MLA_HEADS, MLA_NOPE + MLA_V)
        return {"wk": _pad_heads(ukv[:, :, :MLA_NOPE].reshape(MLA_KV_RANK, -1), MLA_HEADS, MLA_NOPE),
                "wv": ukv[:, :, MLA_NOPE:].reshape(MLA_KV_RANK, -1)}
    return {_BIG[name][2][0]: full}


def _split_cols(wm):
    parts, start = [], 0
    for size in IN_SIZES:
        parts.append(wm[:, start:start + size])
        start += size
    return parts


def _prep_gates(gate_w):
    gw = gate_w.reshape(2, 2, 4, 2, 64, 64)
    wbd = jnp.einsum("zgknCm,nN->knCzgNm", gw, jnp.eye(2, dtype=F32)).reshape(4, LANE, 4, LANE)
    return wbd.transpose(0, 2, 1, 3).reshape(16, LANE, LANE)


def _prep_small(raw):
    r1 = lambda a: a.reshape(1, -1)
    gg = raw["group_g"]
    sink = raw["swa_sink"].reshape(SWA_KV_HEADS, SWA_GROUP, 1, 1)
    return {
        "norm1_g": r1(raw["norm1_g"]), "norm2_g": r1(raw["norm2_g"]),
        "q_a_g": r1(raw["q_a_g"]), "kv_a_g": r1(raw["kv_a_g"]),
        "mla_q_g": jnp.pad(r1(raw["mla_q_g"]), ((0, 0), (0, LANE - MLA_QK))),
        "mla_k_g": jnp.pad(r1(raw["mla_k_g"]), ((0, 0), (0, LANE - MLA_QK))),
        "swa_q_g": jnp.pad(r1(raw["swa_q_g"]), ((0, 0), (0, LANE - SWA_HEAD_DIM))),
        "swa_k_g": jnp.pad(r1(raw["swa_k_g"]), ((0, 0), (0, LANE - SWA_HEAD_DIM))),
        "conv_w": [r1(raw["conv_w"][kk]) for kk in range(4)], "conv_b": r1(raw["conv_b"]),
        "gate_b": [r1(raw["lru_gate_b"][z, g]) for z in range(2) for g in range(2)],
        "sp": [r1(jax.nn.softplus(-raw["lru_lambda"][z])) for z in range(2)],
        "sink_b": jnp.broadcast_to(sink, (SWA_KV_HEADS, SWA_GROUP, QB_SWA, LANE)).reshape(
            SWA_KV_HEADS, SWA_GROUP * QB_SWA, LANE),
        "g_a": r1(gg[:GROUP_WIDTH]), "g_b": r1(gg[GROUP_WIDTH:2 * GROUP_WIDTH]), "g_c": r1(gg[2 * GROUP_WIDTH:])}


def _mesh_pos():
    return lax.axis_index("x"), lax.axis_index("y"), lax.axis_index("c")


def _peer(pos, k):
    return tuple(1 - p if (k >> s) & 1 else p for p, s in zip(pos, (2, 1, 0)))


def _dev_index(pos):
    return 4 * pos[0] + 2 * pos[1] + pos[2]


class _Exchange:
    def __init__(self, bufs, gather):
        self.bufs = list(bufs)
        self.n = len(self.bufs)
        self.gather = [gather] * self.n if isinstance(gather, bool) else list(gather)
        self.specs = [pl.BlockSpec(memory_space=pl.ANY)] * self.n
        self.out_shape = [jax.ShapeDtypeStruct((N_DEV,) + tuple(b.shape if g else b.shape[1:]), b.dtype)
                          for b, g in zip(self.bufs, self.gather)]
        self.scratch = [pltpu.SemaphoreType.DMA(((N_DEV - 1) * self.n,)),
                        pltpu.SemaphoreType.DMA(((N_DEV - 1) * self.n,)), pltpu.SemaphoreType.DMA((self.n,))]

    def _copies(self, x_refs, o_refs, sems, with_recvs):
        send_sems, recv_sems, local_sems = sems
        pos = _mesh_pos()
        me = _dev_index(pos)
        locals_, sends, recvs = [], [], []
        for j in range(self.n):
            src_mine = x_refs[j] if self.gather[j] else x_refs[j].at[me]
            locals_.append(pltpu.make_async_copy(src_mine, o_refs[j].at[me], local_sems.at[j]))
        for k in range(1, N_DEV):
            peer = _peer(pos, k)
            pidx = _dev_index(peer)
            for j in range(self.n):
                src = x_refs[j] if self.gather[j] else x_refs[j].at[pidx]
                sem = (k - 1) * self.n + j
                sends.append(pltpu.make_async_remote_copy(
                    src_ref=src, dst_ref=o_refs[j].at[me], send_sem=send_sems.at[sem], recv_sem=recv_sems.at[sem],
                    device_id=peer, device_id_type=pl.DeviceIdType.MESH))
                if with_recvs:
                    recvs.append(pltpu.make_async_remote_copy(
                        src_ref=src, dst_ref=o_refs[j].at[pidx], send_sem=send_sems.at[sem],
                        recv_sem=recv_sems.at[sem], device_id=peer, device_id_type=pl.DeviceIdType.MESH))
        return locals_, sends, recvs

    def start(self, x_refs, o_refs, sems):
        locals_, sends, _ = self._copies(x_refs, o_refs, sems, False)
        for cp in locals_ + sends:
            cp.start()

    def wait(self, x_refs, o_refs, sems):
        locals_, sends, recvs = self._copies(x_refs, o_refs, sems, True)
        for cp in recvs:
            cp.wait_recv()
        for cp in sends:
            cp.wait_send()
        for cp in locals_:
            cp.wait()


def _exchange(bufs, gather, name):
    xchg = _Exchange(bufs, gather)
    n = xchg.n

    def body(*refs):
        xchg.start(refs[:n], refs[n:2 * n], refs[2 * n:])
        xchg.wait(refs[:n], refs[n:2 * n], refs[2 * n:])

    return _pcall(body, name=name, out_shape=xchg.out_shape, in_specs=xchg.specs, out_specs=xchg.specs,
                  scratch_shapes=xchg.scratch)(*xchg.bufs)


def _pack(arrs, dtype):
    flat = jnp.concatenate([a.reshape(-1).astype(dtype) for a in arrs])
    rows = -(-flat.size // PACK_W)
    rows = -(-rows // 16) * 16
    return jnp.pad(flat, (0, rows * PACK_W - flat.size)).reshape(rows, PACK_W)


def _unpack(buf, shapes, lead=()):
    flat = buf.reshape(lead + (-1,))
    out, off = [], 0
    for shp in shapes:
        n = math.prod(shp)
        out.append(flat[..., off:off + n].reshape(lead + tuple(shp)))
        off += n
    return out


def _sum_sources(buf, name):
    _, r, c = buf.shape
    tr = _rows_tile(r)

    def body(x_ref, o_ref):
        acc = x_ref[0]
        for d in range(1, N_DEV):
            acc = acc + x_ref[d]
        o_ref[...] = acc

    return _pcall(body, name=name, grid=(r // tr,),
                  in_specs=[pl.BlockSpec((N_DEV, tr, c), lambda i: (0, i, 0))],
                  out_specs=pl.BlockSpec((tr, c), lambda i: (i, 0)),
                  out_shape=jax.ShapeDtypeStruct((r, c), F32), compiler_params=_cparams(1))(buf)


def _rows_tile(r):
    best = r
    for t in range(SUBLANE, ELEMWISE_ROWS_MAX + 1, SUBLANE):
        if r % t == 0:
            best = t
    return best


def _adamw(grads, wgt, m, v, name):
    n_lay = len(grads)
    n_src, r, c = grads[0].shape
    tr = _rows_tile(r)
    n_blk = r // tr
    bc1 = 1.0 - ADAM_B1 ** ADAM_STEP
    bc2 = 1.0 - ADAM_B2 ** ADAM_STEP

    def body(*refs):
        g_refs, (w_ref, m_ref, v_ref, go_ref, d_ref, mo_ref, vo_ref) = refs[:n_lay], refs[n_lay:]
        for li, g_ref in enumerate(g_refs):
            @pl.when(pl.program_id(0) == li)
            def _():
                g = g_ref[0].astype(F32)
                for d in range(1, n_src):
                    g = g + g_ref[d].astype(F32)
                m_new = ADAM_B1 * m_ref[...] + (1.0 - ADAM_B1) * g
                v_new = ADAM_B2 * v_ref[...] + (1.0 - ADAM_B2) * (g * g)
                go_ref[...] = g
                mo_ref[...] = m_new
                vo_ref[...] = v_new
                d_ref[...] = -ADAM_LR * ((m_new / bc1) / (jnp.sqrt(v_new / bc2) + ADAM_EPS) + ADAM_WD * w_ref[...])

    g_specs = [pl.BlockSpec((n_src, tr, c),
                            lambda l, i, li=li: (0, jnp.where(l == li, i, jnp.where(l > li, n_blk - 1, 0)), 0))
               for li in range(n_lay)]
    spec = pl.BlockSpec((tr, c), lambda l, i: (l * n_blk + i, 0))
    return _pcall(body, name=name, grid=(n_lay, n_blk), in_specs=g_specs + [spec, spec, spec],
                  out_specs=[spec] * 4, out_shape=[jax.ShapeDtypeStruct((n_lay * r, c), F32)] * 4,
                  compiler_params=_cparams(2))(*grads, wgt, m, v)


def _silu(z):
    return z * jax.nn.sigmoid(z)


_WEIGHTS = ("c_ctx", "w_mod", "b_mod", "norm1_g", "w_in", "q_a_g", "w_uq", "kv_a_g", "w_ukv", "mla_q_g", "mla_k_g",
            "conv_w", "conv_b", "lru_gate_w", "lru_gate_b", "lru_lambda", "swa_q_g", "swa_k_g", "swa_sink", "group_g",
            "w_out", "norm2_g", "w_ff1", "w_ff2")
_SHARDED_SMALL = ("conv_w", "lru_gate_b", "lru_lambda")
_REPL_RAW = ("norm1_g", "q_a_g", "kv_a_g", "mla_q_g", "mla_k_g", "conv_b", "swa_q_g", "swa_k_g",
             "swa_sink", "group_g", "norm2_g")
MOD_ROWS = 32


def kernel(x, c, ctx, c_ctx, w_mod, b_mod, norm1_g, w_in, q_a_g, w_uq, kv_a_g, w_ukv, mla_q_g, mla_k_g, conv_w, conv_b, lru_gate_w, lru_gate_b, lru_lambda, swa_q_g, swa_k_g, swa_sink, group_g, w_out, norm2_g, w_ff1, w_ff2, loss_target, m_c_ctx, m_w_mod, m_b_mod, m_norm1_g, m_w_in, m_q_a_g, m_w_uq, m_kv_a_g, m_w_ukv, m_mla_q_g, m_mla_k_g, m_conv_w, m_conv_b, m_lru_gate_w, m_lru_gate_b, m_lru_lambda, m_swa_q_g, m_swa_k_g, m_swa_sink, m_group_g, m_w_out, m_norm2_g, m_w_ff1, m_w_ff2, v_c_ctx, v_w_mod, v_b_mod, v_norm1_g, v_w_in, v_q_a_g, v_w_uq, v_kv_a_g, v_w_ukv, v_mla_q_g, v_mla_k_g, v_conv_w, v_conv_b, v_lru_gate_w, v_lru_gate_b, v_lru_lambda, v_swa_q_g, v_swa_k_g, v_swa_sink, v_group_g, v_w_out, v_norm2_g, v_w_ff1, v_w_ff2):
    wts = dict(c_ctx=c_ctx, w_mod=w_mod, b_mod=b_mod, norm1_g=norm1_g, w_in=w_in, q_a_g=q_a_g, w_uq=w_uq,
               kv_a_g=kv_a_g, w_ukv=w_ukv, mla_q_g=mla_q_g, mla_k_g=mla_k_g, conv_w=conv_w, conv_b=conv_b,
               lru_gate_w=lru_gate_w, lru_gate_b=lru_gate_b, lru_lambda=lru_lambda, swa_q_g=swa_q_g, swa_k_g=swa_k_g,
               swa_sink=swa_sink, group_g=group_g, w_out=w_out, norm2_g=norm2_g, w_ff1=w_ff1, w_ff2=w_ff2)
    mom1 = dict(zip(_WEIGHTS, (m_c_ctx, m_w_mod, m_b_mod, m_norm1_g, m_w_in, m_q_a_g, m_w_uq, m_kv_a_g, m_w_ukv,
                               m_mla_q_g, m_mla_k_g, m_conv_w, m_conv_b, m_lru_gate_w, m_lru_gate_b, m_lru_lambda,
                               m_swa_q_g, m_swa_k_g, m_swa_sink, m_group_g, m_w_out, m_norm2_g, m_w_ff1, m_w_ff2)))
    mom2 = dict(zip(_WEIGHTS, (v_c_ctx, v_w_mod, v_b_mod, v_norm1_g, v_w_in, v_q_a_g, v_w_uq, v_kv_a_g, v_w_ukv,
                               v_mla_q_g, v_mla_k_g, v_conv_w, v_conv_b, v_lru_gate_w, v_lru_gate_b, v_lru_lambda,
                               v_swa_q_g, v_swa_k_g, v_swa_sink, v_group_g, v_w_out, v_norm2_g, v_w_ff1, v_w_ff2)))
    bsz = x.shape[0]
    n_ex = bsz * N_DEV
    me = _dev_index(_mesh_pos())
    mod_cols = w_mod.shape[-1]

    small_shapes = [c.shape, conv_w.shape, lru_gate_b.shape, lru_lambda.shape]
    shard = lambda n, li: wts[n][li].astype(BF16)
    g_small, *early_pieces = _exchange([_pack([c, conv_w, lru_gate_b, lru_lambda], F32)] + [shard(n, 0) for n in _EARLY],
                                       True, "ag_first")
    c_all, conv_w_all, gate_b_all, lam_all = _unpack(g_small, small_shapes, lead=(N_DEV,))
    c_all = c_all.reshape(n_ex, D_MODEL)
    cat_last = lambda a: jnp.moveaxis(a, 0, -2).reshape(a.shape[1:-1] + (N_DEV * a.shape[-1],))
    conv_w_full, gate_b_full, lam_full = cat_last(conv_w_all), cat_last(gate_b_all), cat_last(lam_all)

    act = jnp.zeros((MOD_ROWS, D_MODEL), F32).at[:n_ex].set(_silu(c_all)).at[n_ex].set(_silu(c_ctx))
    mod_part = jnp.concatenate([_mm(act, w_mod[li], "nn", F32, "mm_mod_l%d" % li) for li in range(DEPTH)], axis=1)
    (mod_all,) = _exchange([mod_part], True, "ag_mod")
    mods = []
    for li in range(DEPTH):
        full = jnp.moveaxis(mod_all[:, :, li * mod_cols:(li + 1) * mod_cols], 0, 1).reshape(MOD_ROWS, -1) + b_mod[li]
        mine = lax.dynamic_slice_in_dim(full, me * bsz, bsz, axis=0)
        ctx_row = jnp.broadcast_to(full[n_ex], mine.shape)
        both = jnp.stack([ctx_row, mine], axis=1).reshape(bsz, 2, N_MOD, 1, D_MODEL)
        mods.append([both[:, :, j] for j in range(N_MOD)])

    raw = {n: wts[n] for n in _REPL_RAW}
    raw.update(conv_w=conv_w_full, lru_gate_b=gate_b_full, lru_lambda=lam_full)
    small_names = list(_REPL_RAW) + list(_SHARDED_SMALL)
    sp, small_vjp, gates_vjp = [None] * DEPTH, [None] * DEPTH, [None] * DEPTH
    for li in range(DEPTH):
        sp[li], small_vjp[li] = jax.vjp(_prep_small, {n: raw[n][li] for n in small_names})
        sp[li]["wbd"], gates_vjp[li] = jax.vjp(_prep_gates, lru_gate_w[li])

    w, w_vjp, g_recv, small_recv = [{} for _ in range(DEPTH)], {}, {}, {}

    def take(li, names, pieces):
        for n, piece in zip(names, pieces):
            out, w_vjp[n, li] = jax.vjp(functools.partial(_prep_weight, n), piece)
            w[li].update(out)

    def gather_hook(li, names):
        return (lambda _: _Exchange([shard(n, li) for n in names], True), lambda got: take(li, names, got))

    def wgrad(n, li, dwl):
        if n == "w_ff1":
            return dwl["ff1"]
        (g,) = w_vjp[n, li]({k: dwl[k].astype(BF16) for k in _BIG[n][2]})
        return g

    def small_pack(li, ds_l, extra=()):
        (d_raw,) = small_vjp[li]({k: v for k, v in ds_l.items() if k != "wbd"})
        return _pack([d_raw[n] for n in small_names] + list(extra), F32)

    def gates_grad(li, ds_l):
        return gates_vjp[li](ds_l["wbd"])[0].reshape(-1, LANE)

    take(0, _EARLY, early_pieces)
    hooks_fwd = [{"mla_fwd": gather_hook(0, _LATE), "swa_fwd": gather_hook(1, _EARLY + ("w_out",))},
                 {"mla_fwd": gather_hook(1, ("w_ff1", "w_ff2"))}]
    bwd_state = {}

    def scatter_last_layer(grads_so_far):
        dwl, dsl = grads_so_far
        return _Exchange([wgrad(n, 1, dwl) for n in _LATE] + [gates_grad(1, dsl)], [False] * len(_LATE) + [True])

    def scatter_first_layer(grads_so_far):
        dwl, dsl = grads_so_far
        dw1, ds1 = bwd_state["dw1"], bwd_state["ds1"]
        bufs = [wgrad(n, 1, dw1) for n in _EARLY] + [wgrad(n, 0, dwl) for n in _LATE]
        bufs += [small_pack(1, ds1), gates_grad(0, dsl)]
        return _Exchange(bufs, [False] * (len(_EARLY) + len(_LATE)) + [True] * 2)

    def scattered_first_layer(got):
        g_recv.update(zip([(n, 1) for n in _EARLY] + [(n, 0) for n in _LATE], got[:-2]))
        small_recv[1], g_recv["lru_gate_w", 0] = got[-2:]

    def scattered_last_layer(got):
        g_recv.update(zip([(n, 1) for n in _LATE], got[:-1]))
        g_recv["lru_gate_w", 1] = got[-1]

    hooks_bwd = [{"mla_bwd": (scatter_first_layer, scattered_first_layer)},
                 {"mla_bwd": (scatter_last_layer, scattered_last_layer)}]

    tc, lat = ctx.shape[1], x.shape[1]
    tabs = {"mla": _rope_tables(lat, tc, MLA_ROPE, MLA_NOPE), "swa": _rope_tables(lat, tc, SWA_HEAD_DIM, 0)}
    stream = jnp.concatenate([ctx, x], axis=1)
    bwds = []
    for li in range(DEPTH):
        stream, bwd = _layer(li, stream, mods[li], w[li], sp[li], tabs, tc, li < DEPTH - 1, hooks_fwd[li], li == 0)
        bwds.append(bwd)
    loss_part, dstream = _loss_and_grad(stream, loss_target, tc)
    dmods = [None] * DEPTH
    dstream, dmods[1], bwd_state["dw1"], bwd_state["ds1"] = bwds[1](dstream, hooks_bwd[1])
    grad_x, dmods[0], dw0, ds0 = bwds[0](dstream, hooks_bwd[0])

    dm_rows = []
    for li in range(DEPTH):
        dm = jnp.concatenate(dmods[li], axis=-1)
        dm_rows.append(jnp.concatenate([dm[:, 1, 0], jnp.sum(dm[:, 0, 0], axis=0, keepdims=True)], axis=0))
    dm_mine = jnp.concatenate(dm_rows, axis=1)
    dm_mine = jnp.pad(dm_mine, ((0, SUBLANE - bsz - 1), (0, 0)))
    (dm_all,) = _exchange([dm_mine], True, "ag_dmod")
    g_wmod, g_bmod, dact_ctx = [], [], jnp.zeros((D_MODEL,), F32)
    for li in range(DEPTH):
        part = dm_all[:, :, li * N_MOD * D_MODEL:(li + 1) * N_MOD * D_MODEL]
        dm32 = jnp.zeros((MOD_ROWS, N_MOD * D_MODEL), F32).at[:n_ex].set(part[:, :bsz].reshape(n_ex, -1))
        dm32 = dm32.at[n_ex].set(jnp.sum(part[:, bsz], axis=0))
        g_bmod.append(jnp.sum(dm32, axis=0))
        cols = lax.dynamic_slice_in_dim(dm32, me * mod_cols, mod_cols, axis=1)
        g_wmod.append(_mm(act, cols, "tn", F32, "mm_mod_dw_l%d" % li))
        dact_ctx = dact_ctx + _mm(cols, w_mod[li], "nt", F32, "mm_mod_dx_l%d" % li)[n_ex]
    sg = jax.nn.sigmoid(c_ctx)
    g_cctx_part = dact_ctx * (sg * (1.0 + c_ctx * (1.0 - sg)))

    last = _exchange([wgrad(n, 0, dw0) for n in _EARLY] + [small_pack(0, ds0, (g_cctx_part, loss_part.reshape(1)))],
                     [False] * len(_EARLY) + [True], "rs_early")
    g_recv.update(zip([(n, 0) for n in _EARLY], last[:-1]))
    small_recv[0] = last[-1]
    layer_shapes = [raw[n].shape[1:] for n in small_names]
    tot = [_unpack(_sum_sources(small_recv[li], "sum_grads_l%d" % li), layer_shapes + [(D_MODEL,), (1,)][:2 * (li == 0)])
           for li in range(DEPTH)]
    grads = {n: jnp.stack([tot[li][j] for li in range(DEPTH)], axis=0) for j, n in enumerate(small_names)}
    grads["c_ctx"], loss = tot[0][-2], tot[0][-1][0]
    for n in _SHARDED_SMALL:
        width = wts[n].shape[-1]
        grads[n] = lax.dynamic_slice_in_dim(grads[n], me * width, width, axis=grads[n].ndim - 1)
    grads["b_mod"] = jnp.stack(g_bmod, axis=0)

    delta, new_m, new_v = {}, {}, {}
    per_layer = {n: [g_recv[n, li] for li in range(DEPTH)] for n in list(_BIG) + ["lru_gate_w"]}
    per_layer["w_mod"] = [g[None] for g in g_wmod]
    for n, srcs in per_layer.items():
        two_d = (DEPTH * math.prod(wts[n].shape[1:-1]), wts[n].shape[-1])
        srcs = [s.reshape((s.shape[0], two_d[0] // DEPTH, two_d[1])) for s in srcs]
        res = _adamw(srcs, wts[n].reshape(two_d), mom1[n].reshape(two_d), mom2[n].reshape(two_d), "adamw_" + n)
        grads[n], delta[n], new_m[n], new_v[n] = [r.reshape(wts[n].shape) for r in res]
    rest = [n for n in _WEIGHTS if n not in delta]
    shapes = [wts[n].shape for n in rest]
    res = _adamw([_pack([grads[n] for n in rest], F32)[None]], _pack([wts[n] for n in rest], F32),
                 _pack([mom1[n] for n in rest], F32), _pack([mom2[n] for n in rest], F32), "adamw_small")
    for tgt, buf in zip((delta, new_m, new_v), res[1:]):
        tgt.update(zip(rest, _unpack(buf, shapes)))

    return (loss, grad_x, *[grads[n] for n in _WEIGHTS], *[delta[n] for n in _WEIGHTS],
            *[new_m[n] for n in _WEIGHTS], *[new_v[n] for n in _WEIGHTS])
```

```python
import functools
import math

import jax
import jax.numpy as jnp
import numpy as np
from jax import lax
from jax.experimental import pallas as pl
from jax.experimental.pallas import tpu as pltpu

F32, BF16 = jnp.float32, jnp.bfloat16

N_DEV = 8
DEPTH = 2
D_MODEL = 1024
D_FF = 4096
N_MOD = 6
GRID_W = 64
WINDOW = 128
ROPE_THETA = 10000.0
EPS = 1e-6
NEG_INF = -1e30
LRU_C = 8.0
LRU_WIDTH = 512
MLA_HEADS, MLA_NOPE, MLA_ROPE, MLA_V = 8, 64, 32, 64
MLA_QK = MLA_NOPE + MLA_ROPE
MLA_Q_RANK, MLA_KV_RANK = 256, 128
SWA_HEADS, SWA_KV_HEADS, SWA_GROUP, SWA_HEAD_DIM = 8, 2, 4, 64
GROUP_WIDTH = 512
IN_SIZES = (256, 128, 32, 512, 512, 512, 128, 128)
IN_WIDTH = sum(IN_SIZES)
ADAM_LR, ADAM_B1, ADAM_B2, ADAM_EPS, ADAM_WD, ADAM_STEP = 0.001, 0.9, 0.999, 1e-08, 0.01, 10

LANE = 128
SUBLANE = 8
TB = 256
QB_SWA = 256
PACK_W = 1024
MM_K_MAX = 4608
MM_COLS_MAX = 1024
MM_FEATURE_ROWS = (1024, 768, 512)
MM_TOKEN_ROWS = 1152
ELEMWISE_ROWS_MAX = 256
MLA_HPS = 2
VMEM_LIMIT = 56 * 1024 * 1024
P_WIDTH = 3072
PC_SQ, PC_LX, PC_LG, PC_CQ, PC_SK, PC_SV, PC_CKV, PC_KR = 0, 1024, 1536, 2048, 2304, 2560, 2816, 2944
MIX_P = 1536


def _pcall(body, **kw):
    return pl.pallas_call(body, **kw)


def _cparams(n_grid):
    return pltpu.CompilerParams(dimension_semantics=("arbitrary",) * n_grid, vmem_limit_bytes=VMEM_LIMIT)


def _dg(a, b, ca, cb):
    return lax.dot_general(a.astype(BF16), b.astype(BF16), (((ca,), (cb,)), ((), ())),
                           preferred_element_type=F32)


@jax.custom_vjp
def _nn(a, b):
    return _dg(a, b, 1, 0)


@jax.custom_vjp
def _nt(a, b):
    return _dg(a, b, 1, 1)


@jax.custom_vjp
def _tn(a, b):
    return _dg(a, b, 0, 0)


_nn.defvjp(lambda a, b: (_nn(a, b), (a, b)), lambda r, ct: (_nt(ct, r[1]), _tn(r[0], ct)))
_nt.defvjp(lambda a, b: (_nt(a, b), (a, b)), lambda r, ct: (_nn(ct, r[1]), _tn(ct, r[0])))
_tn.defvjp(lambda a, b: (_tn(a, b), (a, b)), lambda r, ct: (_nt(r[1], ct), _nn(r[0], ct)))


@functools.partial(jax.custom_vjp, nondiff_argnums=(1, 2))
def _roll(x, shift, axis):
    return pltpu.roll(x, shift % x.shape[axis], axis)


_roll.defvjp(lambda x, shift, axis: (_roll(x, shift, axis), None),
             lambda shift, axis, _, ct: (_roll(ct, -shift, axis),))


@functools.partial(jax.custom_vjp, nondiff_argnums=(1, 2))
def _split(x, n, axis):
    w = x.shape[axis] // n
    return tuple(lax.slice_in_dim(x, i * w, (i + 1) * w, axis=axis) for i in range(n))


_split.defvjp(lambda x, n, axis: (_split(x, n, axis), None),
              lambda n, axis, _, cts: (jnp.concatenate(cts, axis=axis),))


@jax.custom_vjp
def _unstack(x):
    return tuple(x[i] for i in range(x.shape[0]))


_unstack.defvjp(lambda x: (_unstack(x), None), lambda _, cts: (jnp.stack(cts, axis=0),))


def _sig(x):
    return 0.5 * (jnp.tanh(0.5 * x) + 1.0)


def _gelu(x):
    return 0.5 * x * (1.0 + jnp.tanh(math.sqrt(2.0 / math.pi) * (x + 0.044715 * (x * x * x))))


def _rms(x, g, n):
    ms = jnp.sum(x * x, axis=-1, keepdims=True) * (1.0 / n)
    return x * lax.rsqrt(ms + EPS) * g


def _rope(y, cos, sa, sb, quarter):
    return y * cos + _roll(y, -quarter, 1) * sa + _roll(y, quarter, 1) * sb


def _softmax_rows(s, extra=None):
    m = jnp.max(s, axis=-1, keepdims=True)
    if extra is not None:
        m = jnp.maximum(m, extra)
    m = lax.stop_gradient(m)
    e = jnp.exp(s - m)
    den = jnp.sum(e, axis=-1, keepdims=True)
    if extra is not None:
        den = den + jnp.exp(extra - m)
    return e / den


class _A:
    def __init__(self, arr, block, imap, kind="row", first=None, gdtype=F32, gshape=None, gimap=None):
        self.arr, self.block, self.imap, self.kind, self.first = arr, block, imap, kind, first
        self.gdtype, self.gshape, self.gimap = gdtype, gshape, gimap


def _all_zero(*ids):
    return functools.reduce(jnp.logical_and, [i == 0 for i in ids])


def _par(arr):
    nd = arr.ndim
    return _A(arr, arr.shape, lambda *ids: (0,) * nd, "acc", first=_all_zero)


def _op_fwd(name, fn, grid, args, outs):
    n_in = len(args)

    def body(*refs):
        vals = [r[...].astype(F32) for r in refs[:n_in]]
        for r, v in zip(refs[n_in:], fn(*vals)):
            r[...] = v.astype(r.dtype)

    return _pcall(
        body, name=name, grid=grid,
        in_specs=[pl.BlockSpec(a.block, a.imap) for a in args],
        out_specs=[pl.BlockSpec(o[2], o[3]) for o in outs],
        out_shape=[jax.ShapeDtypeStruct(o[0], o[1]) for o in outs],
        compiler_params=_cparams(len(grid)),
    )(*[a.arr for a in args])


def _op_bwd(name, fn, grid, args, outs, ct_arrays, add_to_first=None):
    didx = [i for i, a in enumerate(args) if a.kind not in ("const", "fwd")]
    read = [i for i, a in enumerate(args) if a.kind != "fwd"]
    n_in, n_ct = len(read), len(outs)
    n_add = 0 if add_to_first is None else 1

    def body(*refs):
        ids = [pl.program_id(i) for i in range(len(grid))]
        vals = [jnp.zeros([d for d in a.block if d is not None], F32) for a in args]
        for i, r in zip(read, refs[:n_in]):
            vals[i] = r[...].astype(F32)

        def g(*dv):
            full = list(vals)
            for i, v in zip(didx, dv):
                full[i] = v
            return tuple(fn(*full))

        _, vjp = jax.vjp(g, *[vals[i] for i in didx])
        grads = list(vjp(tuple(r[...].astype(F32) for r in refs[n_in:n_in + n_ct])))
        if n_add:
            grads[0] = grads[0] + refs[n_in + n_ct][...]
        for gr, i, r in zip(grads, didx, refs[n_in + n_ct + n_add:]):
            a = args[i]
            if a.kind == "row":
                r[...] = gr.astype(r.dtype)
            else:
                first = a.first(*ids)

                @pl.when(first)
                def _():
                    r[...] = gr

                @pl.when(jnp.logical_not(first))
                def _():
                    r[...] += gr

    g_specs, g_shapes = [], []
    for i in didx:
        a = args[i]
        if a.kind == "row":
            g_specs.append(pl.BlockSpec(a.block, a.gimap or a.imap))
            g_shapes.append(jax.ShapeDtypeStruct(a.gshape or a.arr.shape, a.gdtype))
        else:
            g_specs.append(pl.BlockSpec(a.block, a.imap))
            g_shapes.append(jax.ShapeDtypeStruct(a.arr.shape, F32))
    return _pcall(
        body, name=name, grid=grid,
        in_specs=[pl.BlockSpec(args[i].block, args[i].imap) for i in read] + [pl.BlockSpec(o[2], o[3]) for o in outs]
        + [pl.BlockSpec(args[didx[0]].block, args[didx[0]].imap)] * n_add,
        out_specs=g_specs, out_shape=g_shapes,
        compiler_params=_cparams(len(grid)),
    )(*[args[i].arr for i in read], *ct_arrays, *([add_to_first] if n_add else []))


def _rowop(name, fn, grid, args, outs):
    res = _op_fwd(name, fn, grid, args, outs)
    return res, lambda *cts, add_to_first=None: _op_bwd(name + "_bwd", fn, grid, args, outs, cts, add_to_first)


def _pick(n, cap):
    best = None
    for t in range(LANE, cap + 1, LANE):
        if n % t == 0:
            best = t
    return best or n


def _mm(a, b, mode, out_dtype, name, epi=None, aux=None, out_split=None):
    if mode == "nn":
        (m, k), n = a.shape, b.shape[1]
    elif mode == "nt":
        (m, k), n = a.shape, b.shape[0]
    else:
        (k, m), n = a.shape, b.shape[1]
    assert k <= MM_K_MAX
    rows = next((r for r in MM_FEATURE_ROWS if m % r == 0), m) if mode == "tn" else MM_TOKEN_ROWS
    tm = rows if m % rows == 0 else m
    tn = n // out_split if out_split else _pick(n, MM_COLS_MAX * (1 if mode == "tn" else 2))
    a_spec = pl.BlockSpec((k, tm), lambda j, i: (0, i)) if mode == "tn" else pl.BlockSpec((tm, k), lambda j, i: (i, 0))
    b_spec = pl.BlockSpec((tn, k), lambda j, i: (j, 0)) if mode == "nt" else pl.BlockSpec((k, tn), lambda j, i: (0, j))
    dims = {"nn": (1, 0), "nt": (1, 1), "tn": (0, 0)}[mode]
    aux_spec = pl.BlockSpec((tm, tn), lambda j, i: (i, j))
    if out_split:
        o_spec, o_shape = pl.BlockSpec((None, tm, tn), lambda j, i: (j, i, 0)), (out_split, m, tn)
    else:
        o_spec, o_shape = aux_spec, (m, n)
    n_aux = 0 if aux is None else 1
    n_out = 2 if epi == "sqrelu" else 1

    def body(*refs):
        o_refs = refs[2 + n_aux:]
        r = _dg(refs[0][...], refs[1][...], *dims)
        if epi == "sqrelu":
            o_refs[0][...] = r.astype(o_refs[0].dtype)
            rl = jnp.maximum(r, 0.0)
            o_refs[1][...] = (rl * rl).astype(o_refs[1].dtype)
        elif epi == "dsqrelu":
            pre = refs[2][...].astype(F32)
            o_refs[0][...] = (r * (2.0 * jnp.maximum(pre, 0.0))).astype(o_refs[0].dtype)
        else:
            o_refs[0][...] = r.astype(o_refs[0].dtype)

    res = _pcall(
        body, name=name, grid=(n // tn, m // tm),
        in_specs=[a_spec, b_spec] + [aux_spec] * n_aux, out_specs=[o_spec] * n_out,
        out_shape=[jax.ShapeDtypeStruct(o_shape, out_dtype)] * n_out, compiler_params=_cparams(2),
    )(a, b, *([aux] if aux is not None else []))
    return res if n_out == 2 else res[0]


ROW_CHUNK = 16


def _softmax_chunks(s_scr, n_keys, scale, emit):
    for r0 in range(0, s_scr.shape[0], ROW_CHUNK):
        rows = slice(r0, r0 + ROW_CHUNK)
        s = s_scr[rows, :n_keys]
        e = jnp.exp((s - jnp.max(s, axis=-1, keepdims=True)) * scale)
        emit(rows, e, 1.0 / jnp.sum(e, axis=-1, keepdims=True))


def _attn_fwd_block(v, n, scale, s_scr, e_scr, l_scr):
    def emit(rows, e, inv_l):
        e_scr[rows, :n] = e.astype(BF16)
        l_scr[rows, :] = jnp.broadcast_to(inv_l, (ROW_CHUNK, LANE))

    _softmax_chunks(s_scr, n, scale, emit)
    return _dg(e_scr[:, :n], v, 1, 0) * l_scr[...]


def _attn_bwd_block(q, k, o, do, scale, s_scr, dp_scr, p_scr, ds_scr):
    n = k.shape[0]

    def emit(rows, e, inv_l):
        p = e * inv_l
        delta = jnp.sum(do[rows, :] * o[rows, :], axis=-1, keepdims=True)
        p_scr[rows, :n] = p.astype(BF16)
        ds_scr[rows, :n] = (p * (dp_scr[rows, :n] - delta) * scale).astype(BF16)

    _softmax_chunks(s_scr, n, scale, emit)
    ds = ds_scr[:, :n]
    return _dg(ds, k, 1, 0), _dg(ds, q, 0, 0), _dg(p_scr[:, :n], do, 0, 0)


def _call_with_exchange(body, xchg, *, name, grid, in_specs, out_specs, out_shape, operands, scratch_shapes=()):
    if xchg is None:
        res = _pcall(body, name=name, grid=grid, in_specs=in_specs, out_specs=out_specs, out_shape=out_shape,
                     scratch_shapes=list(scratch_shapes), compiler_params=_cparams(len(grid)))(*operands)
        return list(res), []
    n_in, n_out, n_sc, n = len(in_specs), len(out_specs), len(scratch_shapes), xchg.n

    def wrapped(*refs):
        ins, x_refs = refs[:n_in], refs[n_in:n_in + n]
        outs, xo_refs = refs[n_in + n:n_in + n + n_out], refs[n_in + n + n_out:n_in + 2 * n + n_out]
        scratch, sems = refs[n_in + 2 * n + n_out:n_in + 2 * n + n_out + n_sc], refs[n_in + 2 * n + n_out + n_sc:]
        ids = [pl.program_id(i) for i in range(len(grid))]

        @pl.when(functools.reduce(jnp.logical_and, [i == 0 for i in ids]))
        def _():
            xchg.start(x_refs, xo_refs, sems)

        body(*ins, *outs, *scratch)

        @pl.when(functools.reduce(jnp.logical_and, [i == g - 1 for i, g in zip(ids, grid)]))
        def _():
            xchg.wait(x_refs, xo_refs, sems)

    res = _pcall(wrapped, name=name, grid=grid, in_specs=list(in_specs) + xchg.specs,
                 out_specs=list(out_specs) + xchg.specs, out_shape=list(out_shape) + xchg.out_shape,
                 scratch_shapes=list(scratch_shapes) + xchg.scratch, compiler_params=_cparams(len(grid)),
                 )(*operands, *xchg.bufs)
    return list(res[:n_out]), list(res[n_out:])


def _head_half(i, shape):
    lane = lax.broadcasted_iota(jnp.int32, shape, len(shape) - 1)
    return (lane < LANE // 2) if i == 0 else (lane >= LANE // 2)


def _mla_attn(q, k, v, tc, ctx_q, name, xchg=None):
    assert MLA_HPS == 2 and MLA_V == LANE // 2
    bsz, t_all, _ = q.shape
    n_t = t_all // TB
    grid = (bsz, MLA_HEADS // MLA_HPS, n_t)
    q_spec = pl.BlockSpec((None, TB, MLA_HPS * LANE), lambda b, h, t: (b, t, h))
    k_spec = pl.BlockSpec((None, t_all, MLA_HPS * LANE), lambda b, h, t: (b, 0, h))
    v_spec = pl.BlockSpec((None, t_all, LANE), lambda b, h, t: (b, 0, h))
    o_spec = pl.BlockSpec((None, TB, LANE), lambda b, h, t: (b, t, h))
    heads = [slice(i * LANE, (i + 1) * LANE) for i in range(MLA_HPS)]
    scale = MLA_QK ** -0.5
    f32_scr, bf16_scr = pltpu.VMEM((TB, t_all), F32), pltpu.VMEM((TB, t_all), BF16)
    o_shape = jax.ShapeDtypeStruct(v.shape, F32)

    def fwd_body(q_ref, k_ref, v_ref, o_ref, *scr):
        t = pl.program_id(2)

        def run(keys):
            n = keys.stop
            for i, hs in enumerate(heads):
                scr[3 * i][:, :n] = _dg(q_ref[:, hs], k_ref[keys, hs], 1, 1)
            both = [_attn_fwd_block(v_ref[keys, :], n, scale, *scr[3 * i:3 * i + 3]) for i in range(MLA_HPS)]
            o_ref[...] = jnp.where(_head_half(0, both[0].shape), both[0], both[1])

        @pl.when(t == 0)
        def _():
            if ctx_q:
                run(slice(0, tc))
            else:
                o_ref[...] = jnp.zeros_like(o_ref)

        @pl.when(t > 0)
        def _():
            run(slice(0, t_all))

    (o,), gathered = _call_with_exchange(
        fwd_body, xchg, name=name, grid=grid, in_specs=[q_spec, k_spec, v_spec], out_specs=[o_spec],
        out_shape=[o_shape], operands=(q, k, v),
        scratch_shapes=[f32_scr, bf16_scr, pltpu.VMEM((TB, LANE), F32)] * MLA_HPS)

    def bwd(do, xchg=None):
        def bwd_body(q_ref, k_ref, v_ref, o_ref, do_ref, dq_ref, dk_ref, dv_ref, *scr):
            t = pl.program_id(2)

            def run(keys, first):
                n = keys.stop
                dos = [jnp.where(_head_half(i, do_ref.shape), do_ref[...], 0.0) for i in range(MLA_HPS)]
                for i, hs in enumerate(heads):
                    scr[4 * i][:, :n] = _dg(q_ref[:, hs], k_ref[keys, hs], 1, 1)
                    scr[4 * i + 1][:, :n] = _dg(dos[i], v_ref[keys, :], 1, 1)
                dvs = []
                for i, hs in enumerate(heads):
                    dq, dk, dv = _attn_bwd_block(q_ref[:, hs], k_ref[keys, hs], o_ref[...], dos[i], scale,
                                                 *scr[4 * i:4 * i + 4])
                    dq_ref[:, hs] = dq
                    dvs.append(dv)
                    if first:
                        dk_ref[keys, hs] = dk
                    else:
                        dk_ref[keys, hs] += dk
                if first:
                    dv_ref[keys, :] = dvs[0] + dvs[1]
                else:
                    dv_ref[keys, :] += dvs[0] + dvs[1]

            @pl.when(t == 0)
            def _():
                dk_ref[...] = jnp.zeros_like(dk_ref)
                dv_ref[...] = jnp.zeros_like(dv_ref)
                if ctx_q:
                    run(slice(0, tc), True)
                else:
                    dq_ref[...] = jnp.zeros_like(dq_ref)

            @pl.when(t > 0)
            def _():
                run(slice(0, t_all), False)

        return _call_with_exchange(
            bwd_body, xchg, name=name + "_bwd", grid=grid, in_specs=[q_spec, k_spec, v_spec, o_spec, o_spec],
            out_specs=[q_spec, k_spec, v_spec],
            out_shape=[jax.ShapeDtypeStruct(q.shape, F32), jax.ShapeDtypeStruct(q.shape, F32), o_shape],
            operands=(q, k, v, o, do), scratch_shapes=[f32_scr, f32_scr, bf16_scr, bf16_scr] * MLA_HPS)

    return o, gathered, bwd


def _swa_block(q, keys, vals, sink, mask):
    qs = jnp.concatenate(list(_split(q, SWA_GROUP, 1)), axis=0)
    sk = jnp.sum(sink, axis=-1, keepdims=True) * (1.0 / LANE)
    s = _nt(qs, keys) * (SWA_HEAD_DIM ** -0.5)
    if mask is not None:
        s = jnp.where(mask, s, NEG_INF)
    o = _split(_nn(_softmax_rows(s, sk), vals + _roll(vals, LANE // 2, 1)), SWA_GROUP, 0)
    low = _head_half(0, o[0].shape)
    return jnp.concatenate([jnp.where(low, o[0], o[1]), jnp.where(low, o[2], o[3])], axis=1)


def _swa_ctx_block(q, kc, vc, sink):
    return _swa_block(q, kc, vc, sink, None)


def _swa_win_block(q, kc, kw, vc, vw, sink, mask):
    return _swa_block(q, jnp.concatenate([kc, kw], axis=0), jnp.concatenate([vc, vw], axis=0), sink, mask)


def _swa_attn(q, k, p_all, sink_b, tc, ctx_q, name, xchg=None):
    bsz, t_all, _ = q.shape
    n_q = t_all // QB_SWA
    n_cq = tc // QB_SWA
    lat = t_all - tc
    span = QB_SWA + 2 * WINDOW
    gw = SWA_GROUP * LANE
    grid = (bsz, SWA_KV_HEADS, n_q)
    q_spec = pl.BlockSpec((None, QB_SWA, gw), lambda b, g, i: (b, i, g))
    k_spec = pl.BlockSpec((None, t_all, LANE), lambda b, g, i: (b, 0, g))
    v_spec = pl.BlockSpec((None, t_all, LANE), lambda b, g, i: (b, 0, PC_SV // LANE + g))
    s_spec = pl.BlockSpec((None, SWA_GROUP * QB_SWA, LANE), lambda b, g, i: (g, 0, 0))

    def window(i):
        q0 = (i - n_cq) * QB_SWA
        w0 = jnp.clip(q0 - WINDOW, 0, lat - span)
        w0 = pl.multiple_of(w0, WINDOW)
        shape = (SWA_GROUP * QB_SWA, tc + span)
        qi = q0 + lax.broadcasted_iota(jnp.int32, shape, 0) % QB_SWA
        col = lax.broadcasted_iota(jnp.int32, shape, 1)
        kj = w0 + col - tc
        mask = (col < tc) | ((kj >= qi - WINDOW) & (kj <= qi + WINDOW))
        return w0, mask

    def fwd_body(q_ref, k_ref, v_ref, s_ref, o_ref):
        i = pl.program_id(2)

        @pl.when(i < n_cq)
        def _():
            if ctx_q:
                o_ref[...] = _swa_ctx_block(q_ref[...].astype(F32), k_ref[0:tc, :], v_ref[0:tc, :].astype(F32),
                                            s_ref[...])
            else:
                o_ref[...] = jnp.zeros_like(o_ref)

        @pl.when(i >= n_cq)
        def _():
            w0, mask = window(i)
            o_ref[...] = _swa_win_block(q_ref[...].astype(F32), k_ref[0:tc, :], k_ref[pl.ds(tc + w0, span), :],
                                        v_ref[0:tc, :].astype(F32), v_ref[pl.ds(tc + w0, span), :].astype(F32),
                                        s_ref[...], mask)

    o_spec = pl.BlockSpec((None, QB_SWA, SWA_GROUP * SWA_HEAD_DIM), lambda b, g, i: (b, i, g))
    (o,), gathered = _call_with_exchange(
        fwd_body, xchg, name=name, grid=grid, in_specs=[q_spec, k_spec, v_spec, s_spec], out_specs=[o_spec],
        out_shape=[jax.ShapeDtypeStruct((bsz, t_all, SWA_HEADS * SWA_HEAD_DIM), F32)],
        operands=(q, k, p_all, sink_b))

    def bwd(do, xchg=None):
        def bwd_body(q_ref, k_ref, v_ref, s_ref, do_ref, dq_ref, dk_ref, dv_ref, ds_ref):
            i = pl.program_id(2)

            @pl.when(i == 0)
            def _():
                dk_ref[...] = jnp.zeros_like(dk_ref)
                dv_ref[...] = jnp.zeros_like(dv_ref)
                ds_ref[...] = jnp.zeros_like(ds_ref)

            @pl.when(i < n_cq)
            def _():
                if ctx_q:
                    _, vjp = jax.vjp(_swa_ctx_block, q_ref[...].astype(F32), k_ref[0:tc, :].astype(F32),
                                     v_ref[0:tc, :].astype(F32), s_ref[...])
                    dq, dk, dv, ds = vjp(do_ref[...])
                    dq_ref[...] = dq
                    dk_ref[0:tc, :] += dk
                    dv_ref[0:tc, :] += dv
                    ds_ref[...] += ds
                else:
                    dq_ref[...] = jnp.zeros_like(dq_ref)

            @pl.when(i >= n_cq)
            def _():
                w0, mask = window(i)
                win = pl.ds(tc + w0, span)
                _, vjp = jax.vjp(functools.partial(_swa_win_block, mask=mask), q_ref[...].astype(F32),
                                 k_ref[0:tc, :].astype(F32), k_ref[win, :].astype(F32),
                                 v_ref[0:tc, :].astype(F32), v_ref[win, :].astype(F32), s_ref[...])
                dq, dkc, dkw, dvc, dvw, ds = vjp(do_ref[...])
                dq_ref[...] = dq
                dk_ref[0:tc, :] += dkc
                dk_ref[win, :] += dkw
                dv_ref[0:tc, :] += dvc
                dv_ref[win, :] += dvw
                ds_ref[...] += ds

        kv_out = pl.BlockSpec((None, t_all, LANE), lambda b, g, i: (b, 0, g))
        ds_spec = pl.BlockSpec((None, None, SWA_GROUP * QB_SWA, LANE), lambda b, g, i: (b, g, 0, 0))
        kv_shape = jax.ShapeDtypeStruct((bsz, t_all, SWA_KV_HEADS * LANE), F32)
        return _call_with_exchange(
            bwd_body, xchg, name=name + "_bwd", grid=grid, in_specs=[q_spec, k_spec, v_spec, s_spec, o_spec],
            out_specs=[q_spec, kv_out, kv_out, ds_spec],
            out_shape=[jax.ShapeDtypeStruct(q.shape, F32), kv_shape, kv_shape,
                       jax.ShapeDtypeStruct((bsz,) + sink_b.shape, F32)],
            operands=(q, k, p_all, sink_b, do))

    return o, gathered, bwd


def _scan_pair(chains, scratch):
    t_all, c = chains[0][0].shape
    n_tiles = t_all // SUBLANE
    row8 = lax.broadcasted_iota(jnp.int32, (t_all, c), 0) % SUBLANE
    refs = [scratch[0:3], scratch[3:6]]
    for (a, u, reverse), (a_s, u_s, _) in zip(chains, refs):
        for d in (1, 2, 4):
            sh = d if not reverse else t_all - d
            ar, ur = pltpu.roll(a, sh, 0), pltpu.roll(u, sh, 0)
            m = (row8 >= d) if not reverse else (row8 < SUBLANE - d)
            u = jnp.where(m, a * ur + u, u)
            a = jnp.where(m, a * ar, a)
        a_s[...] = a
        u_s[...] = u

    def step(j, carries):
        out = []
        for (_, _, reverse), (a_s, u_s, c_s), carry in zip(chains, refs, carries):
            tile = j if not reverse else n_tiles - 1 - j
            base = pl.multiple_of(tile * SUBLANE, SUBLANE)
            c_s[pl.ds(base, SUBLANE), :] = jnp.broadcast_to(carry, (SUBLANE, c))
            last = base + (0 if reverse else SUBLANE - 1)
            out.append(a_s[pl.ds(last, 1), :] * carry + u_s[pl.ds(last, 1), :])
        return tuple(out)

    lax.fori_loop(0, n_tiles, step, (jnp.zeros((1, c), F32),) * 2, unroll=4)
    return [a_s[...] * c_s[...] + u_s[...] for a_s, u_s, c_s in refs]


def _shift_rows(x, reverse_src):
    t_all = x.shape[0]
    row = lax.broadcasted_iota(jnp.int32, x.shape, 0)
    if reverse_src:
        return jnp.where(row == t_all - 1, 0.0, pltpu.roll(x, t_all - 1, 0))
    return jnp.where(row == 0, 0.0, pltpu.roll(x, 1, 0))


def _lru_scan(a0, u0, a1, u1, name):
    bsz, t_all, w = a0.shape
    grid = (bsz, w // LANE)
    spec = pl.BlockSpec((None, t_all, LANE), lambda b, c: (b, 0, c))
    scratch = [pltpu.VMEM((t_all, LANE), F32)] * 6
    shape = jax.ShapeDtypeStruct(a0.shape, F32)

    def fwd_body(a0_ref, u0_ref, a1_ref, u1_ref, h0_ref, h1_ref, *scr):
        h0_ref[...], h1_ref[...] = _scan_pair([(a0_ref[...], u0_ref[...], False), (a1_ref[...], u1_ref[...], True)],
                                              scr)

    h0, h1 = _pcall(fwd_body, name=name, grid=grid, in_specs=[spec] * 4, out_specs=[spec] * 2,
                    out_shape=[shape] * 2, scratch_shapes=scratch, compiler_params=_cparams(2))(a0, u0, a1, u1)

    def bwd(dh0, dh1):
        def bwd_body(a0_ref, h0_ref, g0_ref, a1_ref, h1_ref, g1_ref, da0_ref, du0_ref, da1_ref, du1_ref, *scr):
            g0, g1 = _scan_pair([(_shift_rows(a0_ref[...], True), g0_ref[...], True),
                                 (_shift_rows(a1_ref[...], False), g1_ref[...], False)], scr)
            du0_ref[...] = g0
            da0_ref[...] = g0 * _shift_rows(h0_ref[...], False)
            du1_ref[...] = g1
            da1_ref[...] = g1 * _shift_rows(h1_ref[...], True)

        return _pcall(bwd_body, name=name + "_bwd", grid=grid, in_specs=[spec] * 6, out_specs=[spec] * 4,
                      out_shape=[shape] * 4, scratch_shapes=scratch,
                      compiler_params=_cparams(2))(a0, h0, dh0, a1, h1, dh1)

    return h0, h1, bwd


def _f_mod(x, g, shift, scale):
    return (_rms(x, g, D_MODEL) * (1.0 + scale) + shift,)


def _f_mla_q(cq, ga, w, gh, cos, sa, sb):
    n = _rms(cq, ga, MLA_Q_RANK)
    outs = []
    for wh in _split(w, MLA_HEADS, 1):
        outs.append(_rope(_rms(_nn(n, wh), gh, MLA_QK), cos, sa, sb, MLA_ROPE // 4))
    return (jnp.concatenate(outs, axis=1),)


def _f_mla_kv(ckv, krp, ga, wk, wv, gh, cos, sa, sb):
    n = _rms(ckv, ga, MLA_KV_RANK)
    outs = []
    for wh in _split(wk, MLA_HEADS, 1):
        outs.append(_rope(_rms(_nn(n, wh) + krp, gh, MLA_QK), cos, sa, sb, MLA_ROPE // 4))
    return jnp.concatenate(outs, axis=1), _nn(n, wv)


def _f_conv(x, w0, w1, w2, w3, bias, tc):
    t_all = x.shape[0]
    row = lax.broadcasted_iota(jnp.int32, x.shape, 0)
    lo = jnp.where(row < tc, 0, tc)
    hi = jnp.where(row < tc, tc, t_all)
    y = bias + jnp.zeros_like(x)
    for kk, wk in enumerate((w0, w1, w2, w3)):
        src = row + (kk - 2)
        xs = x if kk == 2 else _roll(x, 2 - kk, 0)
        y = y + wk * jnp.where((src >= lo) & (src < hi), xs, 0.0)
    return (y,)


def _f_gates(xc, w16, b00, b01, b10, b11, sp0, sp1):
    ws = _unstack(w16)
    n_cb = LRU_WIDTH // LANE
    xcs = _split(xc, n_cb, 1)
    bias = [_split(b, n_cb, 1) for b in (b00, b01, b10, b11)]
    sps = [_split(s, n_cb, 1) for s in (sp0, sp1)]
    res = [[], [], [], []]
    for c in range(n_cb):
        for z in range(2):
            r = _sig(_nn(xcs[c], ws[c * 4 + 2 * z]) + bias[2 * z][c])
            i = _sig(_nn(xcs[c], ws[c * 4 + 2 * z + 1]) + bias[2 * z + 1][c])
            la = -LRU_C * r * sps[z][c]
            res[2 * z].append(jnp.exp(la))
            res[2 * z + 1].append(jnp.sqrt(-jnp.tanh(la) * (jnp.exp(2.0 * la) + 1.0)) * (i * xcs[c]))
    return tuple(jnp.concatenate(r, axis=1) for r in res)


def _f_swa_qk(sq, sk, gq, gk, cos, sa, sb):
    qs = [_rope(_rms(x, gq, SWA_HEAD_DIM), cos, sa, sb, SWA_HEAD_DIM // 4) for x in _split(sq, SWA_HEADS, 1)]
    ks = [_rope(_rms(x, gk, SWA_HEAD_DIM), cos, sa, sb, SWA_HEAD_DIM // 4) for x in _split(sk, SWA_KV_HEADS, 1)]
    return jnp.concatenate(qs, axis=1), jnp.concatenate(ks, axis=1)


def _f_qkv(cq, ckv, krp, sq, sk, q_a_g, wuq, mla_q_g, kv_a_g, wk, wv, mla_k_g, swa_q_g, swa_k_g,
           m_cos, m_sa, m_sb, s_cos, s_sa, s_sb):
    return (*_f_mla_q(cq, q_a_g, wuq, mla_q_g, m_cos, m_sa, m_sb),
            *_f_mla_kv(ckv, krp, kv_a_g, wk, wv, mla_k_g, m_cos, m_sa, m_sb),
            *_f_swa_qk(sq, sk, swa_q_g, swa_k_g, s_cos, s_sa, s_sb))


def _f_merge(oa, h0, h1, lg, oc, ga, gb, gc):
    ob = (h0 + h1) * _gelu(lg)
    return (jnp.concatenate([_rms(oa, ga, GROUP_WIDTH), _rms(ob, gb, GROUP_WIDTH), _rms(oc, gc, GROUP_WIDTH)],
                            axis=1),)


def _f_resid_mod(x, y, gate, g, shift, scale):
    x1 = x + gate * y
    return x1, _rms(x1, g, D_MODEL) * (1.0 + scale) + shift


def _f_resid(x, y, gate):
    return (x + gate * y,)


def _hosted(hooks, key, arg=None):
    make, done = hooks.get(key, (None, None))
    xchg = make(arg) if make is not None else None
    return xchg, (done if xchg is not None else lambda outs: None)


def _layer(li, x, mods, w, s, tabs, tc, ctx_q, hooks, latent_dx_only):
    bsz, t_all, _ = x.shape
    n_t = t_all // TB
    grid = (bsz, n_t)
    rows = lambda b, t: (b, t, 0)

    def row(arr, width=None, idx=0, gdtype=F32, gshape=None):
        width = width or arr.shape[-1]
        return _A(arr, (None, TB, width), lambda b, t: (b, t, idx), "row", gdtype=gdtype, gshape=gshape,
                  gimap=rows if gshape is not None else None)

    def out(width, dtype, imap=rows):
        return ((bsz, t_all, width), dtype, (None, TB, width), imap)

    def modarg(arr):
        return _A(arr, (None, None, 1, D_MODEL), lambda b, t: (b, jnp.minimum(t, 1), 0, 0), "acc",
                  first=lambda b, t: t <= 1)

    def tab(arr):
        return _A(arr, (TB, LANE), lambda b, t: (t, 0), "const")

    def pcol(p_all, col, width):
        return row(p_all, width, col // width, gdtype=BF16, gshape=(bsz, t_all, width))

    nm = lambda base: "%s_l%d" % (base, li)
    sh1, sc1, g1, sh2, sc2, g2 = mods
    m_all = bsz * t_all

    x_arg = row(x)
    if latent_dx_only:
        n_c = tc // TB
        x_arg.gshape, x_arg.gimap = (bsz, t_all - tc, D_MODEL), lambda b, t: (b, jnp.maximum(t - n_c, 0), 0)
    (h,), b_mod1 = _rowop(nm("mod1"), _f_mod, grid, [x_arg, _par(s["norm1_g"]), modarg(sh1), modarg(sc1)],
                          [out(D_MODEL, BF16)])
    p_all = _mm(h.reshape(m_all, D_MODEL), w["win"], "nn", BF16, nm("mm_in")).reshape(bsz, t_all, P_WIDTH)

    (q_a, k_a, v_a, q_c, k_c), b_qkv = _rowop(
        nm("qkv"), _f_qkv, grid,
        [pcol(p_all, PC_CQ, 256), pcol(p_all, PC_CKV, 128), pcol(p_all, PC_KR, 128), pcol(p_all, PC_SQ, 1024),
         pcol(p_all, PC_SK, 256)]
        + [_par(a) for a in (s["q_a_g"], w["wuq"], s["mla_q_g"], s["kv_a_g"], w["wk"], w["wv"], s["mla_k_g"],
                             s["swa_q_g"], s["swa_k_g"])]
        + [tab(a) for a in tabs["mla"] + tabs["swa"]],
        [out(MLA_HEADS * LANE, BF16), out(MLA_HEADS * LANE, BF16), out(MLA_HEADS * MLA_V, BF16),
         out(SWA_HEADS * LANE, BF16), out(SWA_KV_HEADS * LANE, BF16)])

    xchg, done = _hosted(hooks, "mla_fwd")
    o_a, got, b_attn_a = _mla_attn(q_a, k_a, v_a, tc, ctx_q, nm("mla_attn"), xchg)
    done(got)

    n_cb = LRU_WIDTH // LANE
    conv_grid = (n_cb, bsz)
    cpar = lambda arr: _A(arr, (1, LANE), lambda c, b: (0, c), "acc", first=lambda c, b: b == 0)
    conv_args = [_A(p_all, (None, t_all, LANE), lambda c, b: (b, 0, PC_LX // LANE + c), "row", gdtype=BF16,
                    gshape=(bsz, t_all, LRU_WIDTH), gimap=lambda c, b: (b, 0, c))]
    conv_args += [cpar(a) for a in s["conv_w"]] + [cpar(s["conv_b"])]
    conv_out = [((bsz, t_all, LRU_WIDTH), F32, (None, t_all, LANE), lambda c, b: (b, 0, c))]
    (xc,), b_conv = _rowop(nm("lru_conv"), functools.partial(_f_conv, tc=tc), conv_grid, conv_args, conv_out)
    rot = lambda b, t: (b, (t + n_t - 1) % n_t, 0)
    (a0, u0, a1, u1), b_gates = _rowop(
        nm("lru_gates"), _f_gates, grid,
        [row(xc), _par(s["wbd"])] + [_par(a) for a in s["gate_b"]] + [_par(a) for a in s["sp"]],
        [out(LRU_WIDTH, F32), out(LRU_WIDTH, F32), out(LRU_WIDTH, F32, rot), out(LRU_WIDTH, F32, rot)])
    h0, h1, b_scan = _lru_scan(a0, u0, a1, u1, nm("lru_scan"))
    h1_arg = _A(h1, (None, TB, LRU_WIDTH), rot, "row")

    xchg, done = _hosted(hooks, "swa_fwd")
    o_c, got, b_attn_c = _swa_attn(q_c, k_c, p_all, s["sink_b"], tc, ctx_q, nm("swa_attn"), xchg)
    done(got)

    (y_in,), b_merge = _rowop(nm("merge"), _f_merge, grid,
                              [row(o_a), row(h0), h1_arg, pcol(p_all, PC_LG, 512), row(o_c), _par(s["g_a"]),
                               _par(s["g_b"]), _par(s["g_c"])],
                              [out(MIX_P, BF16)])
    y = _mm(y_in.reshape(m_all, MIX_P), w["wout"], "nn", F32, nm("mm_out")).reshape(bsz, t_all, D_MODEL)
    (x1, hm), b_rm = _rowop(nm("resid_mod"), _f_resid_mod, grid,
                            [row(x), row(y, gdtype=BF16), modarg(g1), _par(s["norm2_g"]), modarg(sh2), modarg(sc2)],
                            [out(D_MODEL, F32), out(D_MODEL, BF16)])
    pre, act = _mm(hm.reshape(m_all, D_MODEL), w["ff1"], "nn", BF16, nm("mm_ff1"), epi="sqrelu")
    y2 = _mm(act, w["ff2"], "nn", F32, nm("mm_ff2")).reshape(bsz, t_all, D_MODEL)
    (x2,), b_res = _rowop(nm("resid"), _f_resid, grid,
                          [_A(x1, (None, TB, D_MODEL), rows, "fwd"), row(y2, gdtype=BF16), modarg(g2)],
                          [out(D_MODEL, F32)])

    def bwd(dx2, hooks):
        dw, ds = {}, {}
        dy2, dg2 = b_res(dx2)
        dy2 = dy2.reshape(m_all, D_MODEL)
        dpre = _mm(dy2, w["ff2"], "nt", BF16, nm("mm_ff2_dx"), epi="dsqrelu", aux=pre)
        dw["ff2"] = _mm(act, dy2, "tn", BF16, nm("mm_ff2_dw"))
        dhm = _mm(dpre, w["ff1"], "nt", F32, nm("mm_ff1_dx")).reshape(bsz, t_all, D_MODEL)
        dw["ff1"] = _mm(hm.reshape(m_all, D_MODEL), dpre, "tn", BF16, nm("mm_ff1_dw"), out_split=N_DEV)
        dxa, dy, dg1, ds["norm2_g"], dsh2, dsc2 = b_rm(dx2, dhm)
        dy = dy.reshape(m_all, D_MODEL)
        dy_in = _mm(dy, w["wout"], "nt", F32, nm("mm_out_dx")).reshape(bsz, t_all, MIX_P)
        dw["wout"] = _mm(y_in.reshape(m_all, MIX_P), dy, "tn", BF16, nm("mm_out_dw"))
        do_a, dh0, dh1, dlg, do_c, ds["g_a"], ds["g_b"], ds["g_c"] = b_merge(dy_in)

        (dq_c, dk_c, dsv, dsink), _ = b_attn_c(do_c)
        ds["sink_b"] = jnp.sum(dsink, axis=0)

        da0, du0, da1, du1 = b_scan(dh0, dh1)
        gates_g = b_gates(da0, du0, da1, du1)
        dxc, ds["wbd"] = gates_g[0], gates_g[1]
        ds["gate_b"], ds["sp"] = list(gates_g[2:6]), list(gates_g[6:8])
        conv_g = b_conv(dxc)
        dlx, ds["conv_w"], ds["conv_b"] = conv_g[0], list(conv_g[1:5]), conv_g[5]

        xchg, done = _hosted(hooks, "mla_bwd", (dw, ds))
        (dq_a, dk_a, dv_a), got = b_attn_a(do_a, xchg)
        done(got)
        (dcq, dckv, dkr, dsq, dsk, ds["q_a_g"], dw["wuq"], ds["mla_q_g"], ds["kv_a_g"], dw["wk"], dw["wv"],
         ds["mla_k_g"], ds["swa_q_g"], ds["swa_k_g"]) = b_qkv(dq_a, dk_a, dv_a, dq_c, dk_c)

        dp = jnp.concatenate([dsq, dlx, dlg, dcq, dsk, dsv.astype(BF16), dckv, dkr], axis=-1)
        dp = dp.reshape(m_all, P_WIDTH)
        dh = _mm(dp, w["win"], "nt", F32, nm("mm_in_dx")).reshape(bsz, t_all, D_MODEL)
        dw["win"] = _mm(h.reshape(m_all, D_MODEL), dp, "tn", BF16, nm("mm_in_dw"))
        dx, ds["norm1_g"], dsh1, dsc1 = b_mod1(dh, add_to_first=dxa)
        return dx, [dsh1, dsc1, dg1, dsh2, dsc2, dg2], dw, ds

    return x2, bwd


def _loss_and_grad(x2, target, tc):
    bsz, t_all, d = x2.shape
    n_t = t_all // TB
    n_c = tc // TB

    def body(x_ref, t_ref, l_ref, dx_ref):
        b, t = pl.program_id(0), pl.program_id(1)

        @pl.when((b == 0) & (t == 0))
        def _():
            l_ref[...] = jnp.zeros_like(l_ref)

        @pl.when(t < n_c)
        def _():
            dx_ref[...] = jnp.zeros_like(dx_ref)

        @pl.when(t >= n_c)
        def _():
            e = x_ref[...] - t_ref[...]
            dx_ref[...] = e * (1.0 / d)
            l_ref[...] += jnp.sum(e * e) * (0.5 / d)

    loss, dx = _pcall(
        body, name="loss", grid=(bsz, n_t),
        in_specs=[pl.BlockSpec((None, TB, d), lambda b, t: (b, t, 0)),
                  pl.BlockSpec((None, TB, d), lambda b, t: (b, jnp.maximum(t - n_c, 0), 0))],
        out_specs=[pl.BlockSpec((SUBLANE, LANE), lambda b, t: (0, 0)),
                   pl.BlockSpec((None, TB, d), lambda b, t: (b, t, 0))],
        out_shape=[jax.ShapeDtypeStruct((SUBLANE, LANE), F32), jax.ShapeDtypeStruct(x2.shape, F32)],
        compiler_params=_cparams(2))(x2, target)
    return loss[0, 0], dx


def _rope_tables(lat, tc, dim, lane0):
    quarter = dim // 4
    pos = np.arange(lat)
    grid_pos = np.stack([pos // GRID_W, pos % GRID_W], axis=-1).astype(np.float32)
    lane = np.arange(LANE)
    p = np.clip(lane - lane0, 0, dim - 1)
    active = (lane >= lane0) & (lane < lane0 + dim)
    axis, half, qi = p // (dim // 2), (p % (dim // 2)) // quarter, p % quarter
    inv = (np.float32(ROPE_THETA) ** (-qi.astype(np.float32) / np.float32(quarter))).astype(np.float32)
    ang = (np.where(axis[None, :] == 0, grid_pos[:, 0:1], grid_pos[:, 1:2]) * inv[None, :]).astype(np.float32)
    cos = np.where(active, np.cos(ang), 1.0).astype(np.float32)
    sin = np.where(active, np.sin(ang), 0.0).astype(np.float32)
    sa = np.where(half == 0, -sin, 0.0).astype(np.float32)
    sb = np.where(half == 1, sin, 0.0).astype(np.float32)
    ctx1, ctx0 = np.ones((tc, LANE), np.float32), np.zeros((tc, LANE), np.float32)
    return tuple(jnp.asarray(np.concatenate([c, t], 0)) for c, t in ((ctx1, cos), (ctx0, sa), (ctx0, sb)))


_BIG = {"w_in": ((D_MODEL, IN_WIDTH // N_DEV), 1, ("win",)),
        "w_uq": ((MLA_Q_RANK, MLA_HEADS * MLA_QK // N_DEV), 1, ("wuq",)),
        "w_ukv": ((MLA_KV_RANK, MLA_HEADS * (MLA_NOPE + MLA_V) // N_DEV), 1, ("wk", "wv")),
        "w_out": ((3 * GROUP_WIDTH // N_DEV, D_MODEL), 0, ("wout",)),
        "w_ff1": ((D_MODEL, D_FF // N_DEV), 1, ("ff1",)),
        "w_ff2": ((D_FF // N_DEV, D_MODEL), 0, ("ff2",))}
_EARLY = ("w_in", "w_uq", "w_ukv")
_LATE = ("w_out", "w_ff1", "w_ff2")


def _pad_heads(wm, n_heads, dim):
    out = jnp.pad(wm.reshape(wm.shape[0], n_heads, dim), ((0, 0), (0, 0), (0, LANE - dim)))
    return out.reshape(wm.shape[0], n_heads * LANE)


def _prep_weight(name, piece):
    shp, ax, _ = _BIG[name]
    full = jnp.moveaxis(piece, 0, ax).reshape(shp[:ax] + (N_DEV * shp[ax],) + shp[ax + 1:])
    if name == "w_in":
        cq, ckv, kr, lx, lg, sq, sk, sv = _split_cols(full)
        return {"win": jnp.concatenate(
            [_pad_heads(sq, SWA_HEADS, SWA_HEAD_DIM), lx, lg, cq, _pad_heads(sk, SWA_KV_HEADS, SWA_HEAD_DIM),
             _pad_heads(sv, SWA_KV_HEADS, SWA_HEAD_DIM), ckv, jnp.pad(kr, ((0, 0), (MLA_NOPE, LANE - MLA_QK)))], axis=1)}
    if name == "w_uq":
        return {"wuq": _pad_heads(full, MLA_HEADS, MLA_QK)}
    if name == "w_ukv":
        ukv = full.reshape(MLA_KV_RANK, MLA_HEADS, MLA_NOPE + MLA_V)
        return {"wk": _pad_heads(ukv[:, :, :MLA_NOPE].reshape(MLA_KV_RANK, -1), MLA_HEADS, MLA_NOPE),
                "wv": ukv[:, :, MLA_NOPE:].reshape(MLA_KV_RANK, -1)}
    return {_BIG[name][2][0]: full}


def _split_cols(wm):
    parts, start = [], 0
    for size in IN_SIZES:
        parts.append(wm[:, start:start + size])
        start += size
    return parts


def _prep_gates(gate_w):
    gw = gate_w.reshape(2, 2, 4, 2, 64, 64)
    wbd = jnp.einsum("zgknCm,nN->knCzgNm", gw, jnp.eye(2, dtype=F32)).reshape(4, LANE, 4, LANE)
    return wbd.transpose(0, 2, 1, 3).reshape(16, LANE, LANE)


def _prep_small(raw):
    r1 = lambda a: a.reshape(1, -1)
    gg = raw["group_g"]
    sink = raw["swa_sink"].reshape(SWA_KV_HEADS, SWA_GROUP, 1, 1)
    return {
        "norm1_g": r1(raw["norm1_g"]), "norm2_g": r1(raw["norm2_g"]),
        "q_a_g": r1(raw["q_a_g"]), "kv_a_g": r1(raw["kv_a_g"]),
        "mla_q_g": jnp.pad(r1(raw["mla_q_g"]), ((0, 0), (0, LANE - MLA_QK))),
        "mla_k_g": jnp.pad(r1(raw["mla_k_g"]), ((0, 0), (0, LANE - MLA_QK))),
        "swa_q_g": jnp.pad(r1(raw["swa_q_g"]), ((0, 0), (0, LANE - SWA_HEAD_DIM))),
        "swa_k_g": jnp.pad(r1(raw["swa_k_g"]), ((0, 0), (0, LANE - SWA_HEAD_DIM))),
        "conv_w": [r1(raw["conv_w"][kk]) for kk in range(4)], "conv_b": r1(raw["conv_b"]),
        "gate_b": [r1(raw["lru_gate_b"][z, g]) for z in range(2) for g in range(2)],
        "sp": [r1(jax.nn.softplus(-raw["lru_lambda"][z])) for z in range(2)],
        "sink_b": jnp.broadcast_to(sink, (SWA_KV_HEADS, SWA_GROUP, QB_SWA, LANE)).reshape(
            SWA_KV_HEADS, SWA_GROUP * QB_SWA, LANE),
        "g_a": r1(gg[:GROUP_WIDTH]), "g_b": r1(gg[GROUP_WIDTH:2 * GROUP_WIDTH]), "g_c": r1(gg[2 * GROUP_WIDTH:])}


def _mesh_pos():
    return lax.axis_index("x"), lax.axis_index("y"), lax.axis_index("c")


def _peer(pos, k):
    return tuple(1 - p if (k >> s) & 1 else p for p, s in zip(pos, (2, 1, 0)))


def _dev_index(pos):
    return 4 * pos[0] + 2 * pos[1] + pos[2]


class _Exchange:
    def __init__(self, bufs, gather):
        self.bufs = list(bufs)
        self.n = len(self.bufs)
        self.gather = [gather] * self.n if isinstance(gather, bool) else list(gather)
        self.specs = [pl.BlockSpec(memory_space=pl.ANY)] * self.n
        self.out_shape = [jax.ShapeDtypeStruct((N_DEV,) + tuple(b.shape if g else b.shape[1:]), b.dtype)
                          for b, g in zip(self.bufs, self.gather)]
        self.scratch = [pltpu.SemaphoreType.DMA(((N_DEV - 1) * self.n,)),
                        pltpu.SemaphoreType.DMA(((N_DEV - 1) * self.n,)), pltpu.SemaphoreType.DMA((self.n,))]

    def _copies(self, x_refs, o_refs, sems, with_recvs):
        send_sems, recv_sems, local_sems = sems
        pos = _mesh_pos()
        me = _dev_index(pos)
        locals_, sends, recvs = [], [], []
        for j in range(self.n):
            src_mine = x_refs[j] if self.gather[j] else x_refs[j].at[me]
            locals_.append(pltpu.make_async_copy(src_mine, o_refs[j].at[me], local_sems.at[j]))
        for k in range(1, N_DEV):
            peer = _peer(pos, k)
            pidx = _dev_index(peer)
            for j in range(self.n):
                src = x_refs[j] if self.gather[j] else x_refs[j].at[pidx]
                sem = (k - 1) * self.n + j
                sends.append(pltpu.make_async_remote_copy(
                    src_ref=src, dst_ref=o_refs[j].at[me], send_sem=send_sems.at[sem], recv_sem=recv_sems.at[sem],
                    device_id=peer, device_id_type=pl.DeviceIdType.MESH))
                if with_recvs:
                    recvs.append(pltpu.make_async_remote_copy(
                        src_ref=src, dst_ref=o_refs[j].at[pidx], send_sem=send_sems.at[sem],
                        recv_sem=recv_sems.at[sem], device_id=peer, device_id_type=pl.DeviceIdType.MESH))
        return locals_, sends, recvs

    def start(self, x_refs, o_refs, sems):
        locals_, sends, _ = self._copies(x_refs, o_refs, sems, False)
        for cp in locals_ + sends:
            cp.start()

    def wait(self, x_refs, o_refs, sems):
        locals_, sends, recvs = self._copies(x_refs, o_refs, sems, True)
        for cp in recvs:
            cp.wait_recv()
        for cp in sends:
            cp.wait_send()
        for cp in locals_:
            cp.wait()


def _exchange(bufs, gather, name):
    xchg = _Exchange(bufs, gather)
    n = xchg.n

    def body(*refs):
        xchg.start(refs[:n], refs[n:2 * n], refs[2 * n:])
        xchg.wait(refs[:n], refs[n:2 * n], refs[2 * n:])

    return _pcall(body, name=name, out_shape=xchg.out_shape, in_specs=xchg.specs, out_specs=xchg.specs,
                  scratch_shapes=xchg.scratch)(*xchg.bufs)


def _pack(arrs, dtype):
    flat = jnp.concatenate([a.reshape(-1).astype(dtype) for a in arrs])
    rows = -(-flat.size // PACK_W)
    rows = -(-rows // 16) * 16
    return jnp.pad(flat, (0, rows * PACK_W - flat.size)).reshape(rows, PACK_W)


def _unpack(buf, shapes, lead=()):
    flat = buf.reshape(lead + (-1,))
    out, off = [], 0
    for shp in shapes:
        n = math.prod(shp)
        out.append(flat[..., off:off + n].reshape(lead + tuple(shp)))
        off += n
    return out


def _sum_sources(buf, name):
    _, r, c = buf.shape
    tr = _rows_tile(r)

    def body(x_ref, o_ref):
        acc = x_ref[0]
        for d in range(1, N_DEV):
            acc = acc + x_ref[d]
        o_ref[...] = acc

    return _pcall(body, name=name, grid=(r // tr,),
                  in_specs=[pl.BlockSpec((N_DEV, tr, c), lambda i: (0, i, 0))],
                  out_specs=pl.BlockSpec((tr, c), lambda i: (i, 0)),
                  out_shape=jax.ShapeDtypeStruct((r, c), F32), compiler_params=_cparams(1))(buf)


def _rows_tile(r):
    best = r
    for t in range(SUBLANE, ELEMWISE_ROWS_MAX + 1, SUBLANE):
        if r % t == 0:
            best = t
    return best


def _adamw(grads, wgt, m, v, name):
    n_lay = len(grads)
    n_src, r, c = grads[0].shape
    tr = _rows_tile(r)
    n_blk = r // tr
    bc1 = 1.0 - ADAM_B1 ** ADAM_STEP
    bc2 = 1.0 - ADAM_B2 ** ADAM_STEP

    def body(*refs):
        g_refs, (w_ref, m_ref, v_ref, go_ref, d_ref, mo_ref, vo_ref) = refs[:n_lay], refs[n_lay:]
        for li, g_ref in enumerate(g_refs):
            @pl.when(pl.program_id(0) == li)
            def _():
                g = g_ref[0].astype(F32)
                for d in range(1, n_src):
                    g = g + g_ref[d].astype(F32)
                m_new = ADAM_B1 * m_ref[...] + (1.0 - ADAM_B1) * g
                v_new = ADAM_B2 * v_ref[...] + (1.0 - ADAM_B2) * (g * g)
                go_ref[...] = g
                mo_ref[...] = m_new
                vo_ref[...] = v_new
                d_ref[...] = -ADAM_LR * ((m_new / bc1) / (jnp.sqrt(v_new / bc2) + ADAM_EPS) + ADAM_WD * w_ref[...])

    g_specs = [pl.BlockSpec((n_src, tr, c),
                            lambda l, i, li=li: (0, jnp.where(l == li, i, jnp.where(l > li, n_blk - 1, 0)), 0))
               for li in range(n_lay)]
    spec = pl.BlockSpec((tr, c), lambda l, i: (l * n_blk + i, 0))
    return _pcall(body, name=name, grid=(n_lay, n_blk), in_specs=g_specs + [spec, spec, spec],
                  out_specs=[spec] * 4, out_shape=[jax.ShapeDtypeStruct((n_lay * r, c), F32)] * 4,
                  compiler_params=_cparams(2))(*grads, wgt, m, v)


def _silu(z):
    return z * jax.nn.sigmoid(z)


_WEIGHTS = ("c_ctx", "w_mod", "b_mod", "norm1_g", "w_in", "q_a_g", "w_uq", "kv_a_g", "w_ukv", "mla_q_g", "mla_k_g",
            "conv_w", "conv_b", "lru_gate_w", "lru_gate_b", "lru_lambda", "swa_q_g", "swa_k_g", "swa_sink", "group_g",
            "w_out", "norm2_g", "w_ff1", "w_ff2")
_SHARDED_SMALL = ("conv_w", "lru_gate_b", "lru_lambda")
_REPL_RAW = ("norm1_g", "q_a_g", "kv_a_g", "mla_q_g", "mla_k_g", "conv_b", "swa_q_g", "swa_k_g",
             "swa_sink", "group_g", "norm2_g")
MOD_ROWS = 32


def kernel(x, c, ctx, c_ctx, w_mod, b_mod, norm1_g, w_in, q_a_g, w_uq, kv_a_g, w_ukv, mla_q_g, mla_k_g, conv_w, conv_b, lru_gate_w, lru_gate_b, lru_lambda, swa_q_g, swa_k_g, swa_sink, group_g, w_out, norm2_g, w_ff1, w_ff2, loss_target, m_c_ctx, m_w_mod, m_b_mod, m_norm1_g, m_w_in, m_q_a_g, m_w_uq, m_kv_a_g, m_w_ukv, m_mla_q_g, m_mla_k_g, m_conv_w, m_conv_b, m_lru_gate_w, m_lru_gate_b, m_lru_lambda, m_swa_q_g, m_swa_k_g, m_swa_sink, m_group_g, m_w_out, m_norm2_g, m_w_ff1, m_w_ff2, v_c_ctx, v_w_mod, v_b_mod, v_norm1_g, v_w_in, v_q_a_g, v_w_uq, v_kv_a_g, v_w_ukv, v_mla_q_g, v_mla_k_g, v_conv_w, v_conv_b, v_lru_gate_w, v_lru_gate_b, v_lru_lambda, v_swa_q_g, v_swa_k_g, v_swa_sink, v_group_g, v_w_out, v_norm2_g, v_w_ff1, v_w_ff2):
    wts = dict(c_ctx=c_ctx, w_mod=w_mod, b_mod=b_mod, norm1_g=norm1_g, w_in=w_in, q_a_g=q_a_g, w_uq=w_uq,
               kv_a_g=kv_a_g, w_ukv=w_ukv, mla_q_g=mla_q_g, mla_k_g=mla_k_g, conv_w=conv_w, conv_b=conv_b,
               lru_gate_w=lru_gate_w, lru_gate_b=lru_gate_b, lru_lambda=lru_lambda, swa_q_g=swa_q_g, swa_k_g=swa_k_g,
               swa_sink=swa_sink, group_g=group_g, w_out=w_out, norm2_g=norm2_g, w_ff1=w_ff1, w_ff2=w_ff2)
    mom1 = dict(zip(_WEIGHTS, (m_c_ctx, m_w_mod, m_b_mod, m_norm1_g, m_w_in, m_q_a_g, m_w_uq, m_kv_a_g, m_w_ukv,
                               m_mla_q_g, m_mla_k_g, m_conv_w, m_conv_b, m_lru_gate_w, m_lru_gate_b, m_lru_lambda,
                               m_swa_q_g, m_swa_k_g, m_swa_sink, m_group_g, m_w_out, m_norm2_g, m_w_ff1, m_w_ff2)))
    mom2 = dict(zip(_WEIGHTS, (v_c_ctx, v_w_mod, v_b_mod, v_norm1_g, v_w_in, v_q_a_g, v_w_uq, v_kv_a_g, v_w_ukv,
                               v_mla_q_g, v_mla_k_g, v_conv_w, v_conv_b, v_lru_gate_w, v_lru_gate_b, v_lru_lambda,
                               v_swa_q_g, v_swa_k_g, v_swa_sink, v_group_g, v_w_out, v_norm2_g, v_w_ff1, v_w_ff2)))
    bsz = x.shape[0]
    n_ex = bsz * N_DEV
    me = _dev_index(_mesh_pos())
    mod_cols = w_mod.shape[-1]

    small_shapes = [c.shape, conv_w.shape, lru_gate_b.shape, lru_lambda.shape]
    shard = lambda n, li: wts[n][li].astype(BF16)
    g_small, *early_pieces = _exchange([_pack([c, conv_w, lru_gate_b, lru_lambda], F32)] + [shard(n, 0) for n in _EARLY],
                                       True, "ag_first")
    c_all, conv_w_all, gate_b_all, lam_all = _unpack(g_small, small_shapes, lead=(N_DEV,))
    c_all = c_all.reshape(n_ex, D_MODEL)
    cat_last = lambda a: jnp.moveaxis(a, 0, -2).reshape(a.shape[1:-1] + (N_DEV * a.shape[-1],))
    conv_w_full, gate_b_full, lam_full = cat_last(conv_w_all), cat_last(gate_b_all), cat_last(lam_all)

    act = jnp.zeros((MOD_ROWS, D_MODEL), F32).at[:n_ex].set(_silu(c_all)).at[n_ex].set(_silu(c_ctx))
    mod_part = jnp.concatenate([_mm(act, w_mod[li], "nn", F32, "mm_mod_l%d" % li) for li in range(DEPTH)], axis=1)
    (mod_all,) = _exchange([mod_part], True, "ag_mod")
    mods = []
    for li in range(DEPTH):
        full = jnp.moveaxis(mod_all[:, :, li * mod_cols:(li + 1) * mod_cols], 0, 1).reshape(MOD_ROWS, -1) + b_mod[li]
        mine = lax.dynamic_slice_in_dim(full, me * bsz, bsz, axis=0)
        ctx_row = jnp.broadcast_to(full[n_ex], mine.shape)
        both = jnp.stack([ctx_row, mine], axis=1).reshape(bsz, 2, N_MOD, 1, D_MODEL)
        mods.append([both[:, :, j] for j in range(N_MOD)])

    raw = {n: wts[n] for n in _REPL_RAW}
    raw.update(conv_w=conv_w_full, lru_gate_b=gate_b_full, lru_lambda=lam_full)
    small_names = list(_REPL_RAW) + list(_SHARDED_SMALL)
    sp, small_vjp, gates_vjp = [None] * DEPTH, [None] * DEPTH, [None] * DEPTH
    for li in range(DEPTH):
        sp[li], small_vjp[li] = jax.vjp(_prep_small, {n: raw[n][li] for n in small_names})
        sp[li]["wbd"], gates_vjp[li] = jax.vjp(_prep_gates, lru_gate_w[li])

    w, w_vjp, g_recv, small_recv = [{} for _ in range(DEPTH)], {}, {}, {}

    def take(li, names, pieces):
        for n, piece in zip(names, pieces):
            out, w_vjp[n, li] = jax.vjp(functools.partial(_prep_weight, n), piece)
            w[li].update(out)

    def gather_hook(li, names):
        return (lambda _: _Exchange([shard(n, li) for n in names], True), lambda got: take(li, names, got))

    def wgrad(n, li, dwl):
        if n == "w_ff1":
            return dwl["ff1"]
        (g,) = w_vjp[n, li]({k: dwl[k].astype(BF16) for k in _BIG[n][2]})
        return g

    def small_pack(li, ds_l, extra=()):
        (d_raw,) = small_vjp[li]({k: v for k, v in ds_l.items() if k != "wbd"})
        return _pack([d_raw[n] for n in small_names] + list(extra), F32)

    def gates_grad(li, ds_l):
        return gates_vjp[li](ds_l["wbd"])[0].reshape(-1, LANE)

    take(0, _EARLY, early_pieces)
    hooks_fwd = [{"mla_fwd": gather_hook(0, _LATE), "swa_fwd": gather_hook(1, _EARLY + ("w_out",))},
                 {"mla_fwd": gather_hook(1, ("w_ff1", "w_ff2"))}]
    bwd_state = {}

    def scatter_last_layer(grads_so_far):
        dwl, dsl = grads_so_far
        return _Exchange([wgrad(n, 1, dwl) for n in _LATE] + [gates_grad(1, dsl)], [False] * len(_LATE) + [True])

    def scatter_first_layer(grads_so_far):
        dwl, dsl = grads_so_far
        dw1, ds1 = bwd_state["dw1"], bwd_state["ds1"]
        bufs = [wgrad(n, 1, dw1) for n in _EARLY] + [wgrad(n, 0, dwl) for n in _LATE]
        bufs += [small_pack(1, ds1), gates_grad(0, dsl)]
        return _Exchange(bufs, [False] * (len(_EARLY) + len(_LATE)) + [True] * 2)

    def scattered_first_layer(got):
        g_recv.update(zip([(n, 1) for n in _EARLY] + [(n, 0) for n in _LATE], got[:-2]))
        small_recv[1], g_recv["lru_gate_w", 0] = got[-2:]

    def scattered_last_layer(got):
        g_recv.update(zip([(n, 1) for n in _LATE], got[:-1]))
        g_recv["lru_gate_w", 1] = got[-1]

    hooks_bwd = [{"mla_bwd": (scatter_first_layer, scattered_first_layer)},
                 {"mla_bwd": (scatter_last_layer, scattered_last_layer)}]

    tc, lat = ctx.shape[1], x.shape[1]
    tabs = {"mla": _rope_tables(lat, tc, MLA_ROPE, MLA_NOPE), "swa": _rope_tables(lat, tc, SWA_HEAD_DIM, 0)}
    stream = jnp.concatenate([ctx, x], axis=1)
    bwds = []
    for li in range(DEPTH):
        stream, bwd = _layer(li, stream, mods[li], w[li], sp[li], tabs, tc, li < DEPTH - 1, hooks_fwd[li], li == 0)
        bwds.append(bwd)
    loss_part, dstream = _loss_and_grad(stream, loss_target, tc)
    dmods = [None] * DEPTH
    dstream, dmods[1], bwd_state["dw1"], bwd_state["ds1"] = bwds[1](dstream, hooks_bwd[1])
    grad_x, dmods[0], dw0, ds0 = bwds[0](dstream, hooks_bwd[0])

    dm_rows = []
    for li in range(DEPTH):
        dm = jnp.concatenate(dmods[li], axis=-1)
        dm_rows.append(jnp.concatenate([dm[:, 1, 0], jnp.sum(dm[:, 0, 0], axis=0, keepdims=True)], axis=0))
    dm_mine = jnp.concatenate(dm_rows, axis=1)
    dm_mine = jnp.pad(dm_mine, ((0, SUBLANE - bsz - 1), (0, 0)))
    (dm_all,) = _exchange([dm_mine], True, "ag_dmod")
    g_wmod, g_bmod, dact_ctx = [], [], jnp.zeros((D_MODEL,), F32)
    for li in range(DEPTH):
        part = dm_all[:, :, li * N_MOD * D_MODEL:(li + 1) * N_MOD * D_MODEL]
        dm32 = jnp.zeros((MOD_ROWS, N_MOD * D_MODEL), F32).at[:n_ex].set(part[:, :bsz].reshape(n_ex, -1))
        dm32 = dm32.at[n_ex].set(jnp.sum(part[:, bsz], axis=0))
        g_bmod.append(jnp.sum(dm32, axis=0))
        cols = lax.dynamic_slice_in_dim(dm32, me * mod_cols, mod_cols, axis=1)
        g_wmod.append(_mm(act, cols, "tn", F32, "mm_mod_dw_l%d" % li))
        dact_ctx = dact_ctx + _mm(cols, w_mod[li], "nt", F32, "mm_mod_dx_l%d" % li)[n_ex]
    sg = jax.nn.sigmoid(c_ctx)
    g_cctx_part = dact_ctx * (sg * (1.0 + c_ctx * (1.0 - sg)))

    last = _exchange([wgrad(n, 0, dw0) for n in _EARLY] + [small_pack(0, ds0, (g_cctx_part, loss_part.reshape(1)))],
                     [False] * len(_EARLY) + [True], "rs_early")
    g_recv.update(zip([(n, 0) for n in _EARLY], last[:-1]))
    small_recv[0] = last[-1]
    layer_shapes = [raw[n].shape[1:] for n in small_names]
    tot = [_unpack(_sum_sources(small_recv[li], "sum_grads_l%d" % li), layer_shapes + [(D_MODEL,), (1,)][:2 * (li == 0)])
           for li in range(DEPTH)]
    grads = {n: jnp.stack([tot[li][j] for li in range(DEPTH)], axis=0) for j, n in enumerate(small_names)}
    grads["c_ctx"], loss = tot[0][-2], tot[0][-1][0]
    for n in _SHARDED_SMALL:
        width = wts[n].shape[-1]
        grads[n] = lax.dynamic_slice_in_dim(grads[n], me * width, width, axis=grads[n].ndim - 1)
    grads["b_mod"] = jnp.stack(g_bmod, axis=0)

    delta, new_m, new_v = {}, {}, {}
    per_layer = {n: [g_recv[n, li] for li in range(DEPTH)] for n in list(_BIG) + ["lru_gate_w"]}
    per_layer["w_mod"] = [g[None] for g in g_wmod]
    for n, srcs in per_layer.items():
        two_d = (DEPTH * math.prod(wts[n].shape[1:-1]), wts[n].shape[-1])
        srcs = [s.reshape((s.shape[0], two_d[0] // DEPTH, two_d[1])) for s in srcs]
        res = _adamw(srcs, wts[n].reshape(two_d), mom1[n].reshape(two_d), mom2[n].reshape(two_d), "adamw_" + n)
        grads[n], delta[n], new_m[n], new_v[n] = [r.reshape(wts[n].shape) for r in res]
    rest = [n for n in _WEIGHTS if n not in delta]
    shapes = [wts[n].shape for n in rest]
    res = _adamw([_pack([grads[n] for n in rest], F32)[None]], _pack([wts[n] for n in rest], F32),
                 _pack([mom1[n] for n in rest], F32), _pack([mom2[n] for n in rest], F32), "adamw_small")
    for tgt, buf in zip((delta, new_m, new_v), res[1:]):
        tgt.update(zip(rest, _unpack(buf, shapes)))

    return (loss, grad_x, *[grads[n] for n in _WEIGHTS], *[delta[n] for n in _WEIGHTS],
            *[new_m[n] for n in _WEIGHTS], *[new_v[n] for n in _WEIGHTS])
```

```python
import functools
import math

import jax
import jax.numpy as jnp
import numpy as np
from jax import lax
from jax.experimental import pallas as pl
from jax.experimental.pallas import tpu as pltpu

F32, BF16 = jnp.float32, jnp.bfloat16

N_DEV = 8
DEPTH = 2
D_MODEL = 1024
D_FF = 4096
N_MOD = 6
GRID_W = 64
WINDOW = 128
ROPE_THETA = 10000.0
EPS = 1e-6
NEG_INF = -1e30
LRU_C = 8.0
LRU_WIDTH = 512
MLA_HEADS, MLA_NOPE, MLA_ROPE, MLA_V = 8, 64, 32, 64
MLA_QK = MLA_NOPE + MLA_ROPE
MLA_Q_RANK, MLA_KV_RANK = 256, 128
SWA_HEADS, SWA_KV_HEADS, SWA_GROUP, SWA_HEAD_DIM = 8, 2, 4, 64
GROUP_WIDTH = 512
IN_SIZES = (256, 128, 32, 512, 512, 512, 128, 128)
IN_WIDTH = sum(IN_SIZES)
ADAM_LR, ADAM_B1, ADAM_B2, ADAM_EPS, ADAM_WD, ADAM_STEP = 0.001, 0.9, 0.999, 1e-08, 0.01, 10

LANE = 128
SUBLANE = 8
TB = 256
QB_SWA = 256
PACK_W = 1024
MM_K_MAX = 4608
MM_COLS_MAX = 1024
MM_FEATURE_ROWS = (1024, 768, 512)
MM_TOKEN_ROWS = 1152
ELEMWISE_ROWS_MAX = 256
MLA_HPS = 2
VMEM_LIMIT = 56 * 1024 * 1024
P_WIDTH = 3072
PC_SQ, PC_LX, PC_LG, PC_CQ, PC_SK, PC_SV, PC_CKV, PC_KR = 0, 1024, 1536, 2048, 2304, 2560, 2816, 2944
MIX_P = 1536


def _pcall(body, **kw):
    return pl.pallas_call(body, **kw)


def _cparams(n_grid):
    return pltpu.CompilerParams(dimension_semantics=("arbitrary",) * n_grid, vmem_limit_bytes=VMEM_LIMIT)


def _dg(a, b, ca, cb):
    return lax.dot_general(a.astype(BF16), b.astype(BF16), (((ca,), (cb,)), ((), ())),
                           preferred_element_type=F32)


@jax.custom_vjp
def _nn(a, b):
    return _dg(a, b, 1, 0)


@jax.custom_vjp
def _nt(a, b):
    return _dg(a, b, 1, 1)


@jax.custom_vjp
def _tn(a, b):
    return _dg(a, b, 0, 0)


_nn.defvjp(lambda a, b: (_nn(a, b), (a, b)), lambda r, ct: (_nt(ct, r[1]), _tn(r[0], ct)))
_nt.defvjp(lambda a, b: (_nt(a, b), (a, b)), lambda r, ct: (_nn(ct, r[1]), _tn(ct, r[0])))
_tn.defvjp(lambda a, b: (_tn(a, b), (a, b)), lambda r, ct: (_nt(r[1], ct), _nn(r[0], ct)))


@functools.partial(jax.custom_vjp, nondiff_argnums=(1, 2))
def _roll(x, shift, axis):
    return pltpu.roll(x, shift % x.shape[axis], axis)


_roll.defvjp(lambda x, shift, axis: (_roll(x, shift, axis), None),
             lambda shift, axis, _, ct: (_roll(ct, -shift, axis),))


@functools.partial(jax.custom_vjp, nondiff_argnums=(1, 2))
def _split(x, n, axis):
    w = x.shape[axis] // n
    return tuple(lax.slice_in_dim(x, i * w, (i + 1) * w, axis=axis) for i in range(n))


_split.defvjp(lambda x, n, axis: (_split(x, n, axis), None),
              lambda n, axis, _, cts: (jnp.concatenate(cts, axis=axis),))


@jax.custom_vjp
def _unstack(x):
    return tuple(x[i] for i in range(x.shape[0]))


_unstack.defvjp(lambda x: (_unstack(x), None), lambda _, cts: (jnp.stack(cts, axis=0),))


def _sig(x):
    return 0.5 * (jnp.tanh(0.5 * x) + 1.0)


def _gelu(x):
    return 0.5 * x * (1.0 + jnp.tanh(math.sqrt(2.0 / math.pi) * (x + 0.044715 * (x * x * x))))


def _rms(x, g, n):
    ms = jnp.sum(x * x, axis=-1, keepdims=True) * (1.0 / n)
    return x * lax.rsqrt(ms + EPS) * g


def _rope(y, cos, sa, sb, quarter):
    return y * cos + _roll(y, -quarter, 1) * sa + _roll(y, quarter, 1) * sb


def _softmax_rows(s, extra=None):
    m = jnp.max(s, axis=-1, keepdims=True)
    if extra is not None:
        m = jnp.maximum(m, extra)
    m = lax.stop_gradient(m)
    e = jnp.exp(s - m)
    den = jnp.sum(e, axis=-1, keepdims=True)
    if extra is not None:
        den = den + jnp.exp(extra - m)
    return e / den


class _A:
    def __init__(self, arr, block, imap, kind="row", first=None, gdtype=F32, gshape=None, gimap=None):
        self.arr, self.block, self.imap, self.kind, self.first = arr, block, imap, kind, first
        self.gdtype, self.gshape, self.gimap = gdtype, gshape, gimap


def _all_zero(*ids):
    return functools.reduce(jnp.logical_and, [i == 0 for i in ids])


def _par(arr):
    nd = arr.ndim
    return _A(arr, arr.shape, lambda *ids: (0,) * nd, "acc", first=_all_zero)


def _op_fwd(name, fn, grid, args, outs):
    n_in = len(args)

    def body(*refs):
        vals = [r[...].astype(F32) for r in refs[:n_in]]
        for r, v in zip(refs[n_in:], fn(*vals)):
            r[...] = v.astype(r.dtype)

    return _pcall(
        body, name=name, grid=grid,
        in_specs=[pl.BlockSpec(a.block, a.imap) for a in args],
        out_specs=[pl.BlockSpec(o[2], o[3]) for o in outs],
        out_shape=[jax.ShapeDtypeStruct(o[0], o[1]) for o in outs],
        compiler_params=_cparams(len(grid)),
    )(*[a.arr for a in args])


def _op_bwd(name, fn, grid, args, outs, ct_arrays, add_to_first=None):
    didx = [i for i, a in enumerate(args) if a.kind not in ("const", "fwd")]
    read = [i for i, a in enumerate(args) if a.kind != "fwd"]
    n_in, n_ct = len(read), len(outs)
    n_add = 0 if add_to_first is None else 1

    def body(*refs):
        ids = [pl.program_id(i) for i in range(len(grid))]
        vals = [jnp.zeros([d for d in a.block if d is not None], F32) for a in args]
        for i, r in zip(read, refs[:n_in]):
            vals[i] = r[...].astype(F32)

        def g(*dv):
            full = list(vals)
            for i, v in zip(didx, dv):
                full[i] = v
            return tuple(fn(*full))

        _, vjp = jax.vjp(g, *[vals[i] for i in didx])
        grads = list(vjp(tuple(r[...].astype(F32) for r in refs[n_in:n_in + n_ct])))
        if n_add:
            grads[0] = grads[0] + refs[n_in + n_ct][...]
        for gr, i, r in zip(grads, didx, refs[n_in + n_ct + n_add:]):
            a = args[i]
            if a.kind == "row":
                r[...] = gr.astype(r.dtype)
            else:
                first = a.first(*ids)

                @pl.when(first)
                def _():
                    r[...] = gr

                @pl.when(jnp.logical_not(first))
                def _():
                    r[...] += gr

    g_specs, g_shapes = [], []
    for i in didx:
        a = args[i]
        if a.kind == "row":
            g_specs.append(pl.BlockSpec(a.block, a.gimap or a.imap))
            g_shapes.append(jax.ShapeDtypeStruct(a.gshape or a.arr.shape, a.gdtype))
        else:
            g_specs.append(pl.BlockSpec(a.block, a.imap))
            g_shapes.append(jax.ShapeDtypeStruct(a.arr.shape, F32))
    return _pcall(
        body, name=name, grid=grid,
        in_specs=[pl.BlockSpec(args[i].block, args[i].imap) for i in read] + [pl.BlockSpec(o[2], o[3]) for o in outs]
        + [pl.BlockSpec(args[didx[0]].block, args[didx[0]].imap)] * n_add,
        out_specs=g_specs, out_shape=g_shapes,
        compiler_params=_cparams(len(grid)),
    )(*[args[i].arr for i in read], *ct_arrays, *([add_to_first] if n_add else []))


def _rowop(name, fn, grid, args, outs):
    res = _op_fwd(name, fn, grid, args, outs)
    return res, lambda *cts, add_to_first=None: _op_bwd(name + "_bwd", fn, grid, args, outs, cts, add_to_first)


def _pick(n, cap):
    best = None
    for t in range(LANE, cap + 1, LANE):
        if n % t == 0:
            best = t
    return best or n


def _mm(a, b, mode, out_dtype, name, epi=None, aux=None, out_split=None):
    if mode == "nn":
        (m, k), n = a.shape, b.shape[1]
    elif mode == "nt":
        (m, k), n = a.shape, b.shape[0]
    else:
        (k, m), n = a.shape, b.shape[1]
    assert k <= MM_K_MAX
    rows = next((r for r in MM_FEATURE_ROWS if m % r == 0), m) if mode == "tn" else MM_TOKEN_ROWS
    tm = rows if m % rows == 0 else m
    tn = n // out_split if out_split else _pick(n, MM_COLS_MAX * (1 if mode == "tn" else 2))
    a_spec = pl.BlockSpec((k, tm), lambda j, i: (0, i)) if mode == "tn" else pl.BlockSpec((tm, k), lambda j, i: (i, 0))
    b_spec = pl.BlockSpec((tn, k), lambda j, i: (j, 0)) if mode == "nt" else pl.BlockSpec((k, tn), lambda j, i: (0, j))
    dims = {"nn": (1, 0), "nt": (1, 1), "tn": (0, 0)}[mode]
    aux_spec = pl.BlockSpec((tm, tn), lambda j, i: (i, j))
    if out_split:
        o_spec, o_shape = pl.BlockSpec((None, tm, tn), lambda j, i: (j, i, 0)), (out_split, m, tn)
    else:
        o_spec, o_shape = aux_spec, (m, n)
    n_aux = 0 if aux is None else 1
    n_out = 2 if epi == "sqrelu" else 1

    def body(*refs):
        o_refs = refs[2 + n_aux:]
        r = _dg(refs[0][...], refs[1][...], *dims)
        if epi == "sqrelu":
            o_refs[0][...] = r.astype(o_refs[0].dtype)
            rl = jnp.maximum(r, 0.0)
            o_refs[1][...] = (rl * rl).astype(o_refs[1].dtype)
        elif epi == "dsqrelu":
            pre = refs[2][...].astype(F32)
            o_refs[0][...] = (r * (2.0 * jnp.maximum(pre, 0.0))).astype(o_refs[0].dtype)
        else:
            o_refs[0][...] = r.astype(o_refs[0].dtype)

    res = _pcall(
        body, name=name, grid=(n // tn, m // tm),
        in_specs=[a_spec, b_spec] + [aux_spec] * n_aux, out_specs=[o_spec] * n_out,
        out_shape=[jax.ShapeDtypeStruct(o_shape, out_dtype)] * n_out, compiler_params=_cparams(2),
    )(a, b, *([aux] if aux is not None else []))
    return res if n_out == 2 else res[0]


ROW_CHUNK = 16


def _softmax_chunks(s_scr, n_keys, scale, emit):
    for r0 in range(0, s_scr.shape[0], ROW_CHUNK):
        rows = slice(r0, r0 + ROW_CHUNK)
        s = s_scr[rows, :n_keys]
        e = jnp.exp((s - jnp.max(s, axis=-1, keepdims=True)) * scale)
        emit(rows, e, 1.0 / jnp.sum(e, axis=-1, keepdims=True))


def _attn_fwd_block(v, n, scale, s_scr, e_scr, l_scr):
    def emit(rows, e, inv_l):
        e_scr[rows, :n] = e.astype(BF16)
        l_scr[rows, :] = jnp.broadcast_to(inv_l, (ROW_CHUNK, LANE))

    _softmax_chunks(s_scr, n, scale, emit)
    return _dg(e_scr[:, :n], v, 1, 0) * l_scr[...]


def _attn_bwd_block(q, k, o, do, scale, s_scr, dp_scr, p_scr, ds_scr):
    n = k.shape[0]

    def emit(rows, e, inv_l):
        p = e * inv_l
        delta = jnp.sum(do[rows, :] * o[rows, :], axis=-1, keepdims=True)
        p_scr[rows, :n] = p.astype(BF16)
        ds_scr[rows, :n] = (p * (dp_scr[rows, :n] - delta) * scale).astype(BF16)

    _softmax_chunks(s_scr, n, scale, emit)
    ds = ds_scr[:, :n]
    return _dg(ds, k, 1, 0), _dg(ds, q, 0, 0), _dg(p_scr[:, :n], do, 0, 0)


def _call_with_exchange(body, xchg, *, name, grid, in_specs, out_specs, out_shape, operands, scratch_shapes=()):
    if xchg is None:
        res = _pcall(body, name=name, grid=grid, in_specs=in_specs, out_specs=out_specs, out_shape=out_shape,
                     scratch_shapes=list(scratch_shapes), compiler_params=_cparams(len(grid)))(*operands)
        return list(res), []
    n_in, n_out, n_sc, n = len(in_specs), len(out_specs), len(scratch_shapes), xchg.n

    def wrapped(*refs):
        ins, x_refs = refs[:n_in], refs[n_in:n_in + n]
        outs, xo_refs = refs[n_in + n:n_in + n + n_out], refs[n_in + n + n_out:n_in + 2 * n + n_out]
        scratch, sems = refs[n_in + 2 * n + n_out:n_in + 2 * n + n_out + n_sc], refs[n_in + 2 * n + n_out + n_sc:]
        ids = [pl.program_id(i) for i in range(len(grid))]

        @pl.when(functools.reduce(jnp.logical_and, [i == 0 for i in ids]))
        def _():
            xchg.start(x_refs, xo_refs, sems)

        body(*ins, *outs, *scratch)

        @pl.when(functools.reduce(jnp.logical_and, [i == g - 1 for i, g in zip(ids, grid)]))
        def _():
            xchg.wait(x_refs, xo_refs, sems)

    res = _pcall(wrapped, name=name, grid=grid, in_specs=list(in_specs) + xchg.specs,
                 out_specs=list(out_specs) + xchg.specs, out_shape=list(out_shape) + xchg.out_shape,
                 scratch_shapes=list(scratch_shapes) + xchg.scratch, compiler_params=_cparams(len(grid)),
                 )(*operands, *xchg.bufs)
    return list(res[:n_out]), list(res[n_out:])


def _head_half(i, shape):
    lane = lax.broadcasted_iota(jnp.int32, shape, len(shape) - 1)
    return (lane < LANE // 2) if i == 0 else (lane >= LANE // 2)


def _mla_attn(q, k, v, tc, ctx_q, name, xchg=None):
    assert MLA_HPS == 2 and MLA_V == LANE // 2
    bsz, t_all, _ = q.shape
    n_t = t_all // TB
    grid = (bsz, MLA_HEADS // MLA_HPS, n_t)
    q_spec = pl.BlockSpec((None, TB, MLA_HPS * LANE), lambda b, h, t: (b, t, h))
    k_spec = pl.BlockSpec((None, t_all, MLA_HPS * LANE), lambda b, h, t: (b, 0, h))
    v_spec = pl.BlockSpec((None, t_all, LANE), lambda b, h, t: (b, 0, h))
    o_spec = pl.BlockSpec((None, TB, LANE), lambda b, h, t: (b, t, h))
    heads = [slice(i * LANE, (i + 1) * LANE) for i in range(MLA_HPS)]
    scale = MLA_QK ** -0.5
    f32_scr, bf16_scr = pltpu.VMEM((TB, t_all), F32), pltpu.VMEM((TB, t_all), BF16)
    o_shape = jax.ShapeDtypeStruct(v.shape, F32)

    def fwd_body(q_ref, k_ref, v_ref, o_ref, *scr):
        t = pl.program_id(2)

        def run(keys):
            n = keys.stop
            for i, hs in enumerate(heads):
                scr[3 * i][:, :n] = _dg(q_ref[:, hs], k_ref[keys, hs], 1, 1)
            both = [_attn_fwd_block(v_ref[keys, :], n, scale, *scr[3 * i:3 * i + 3]) for i in range(MLA_HPS)]
            o_ref[...] = jnp.where(_head_half(0, both[0].shape), both[0], both[1])

        @pl.when(t == 0)
        def _():
            if ctx_q:
                run(slice(0, tc))
            else:
                o_ref[...] = jnp.zeros_like(o_ref)

        @pl.when(t > 0)
        def _():
            run(slice(0, t_all))

    (o,), gathered = _call_with_exchange(
        fwd_body, xchg, name=name, grid=grid, in_specs=[q_spec, k_spec, v_spec], out_specs=[o_spec],
        out_shape=[o_shape], operands=(q, k, v),
        scratch_shapes=[f32_scr, bf16_scr, pltpu.VMEM((TB, LANE), F32)] * MLA_HPS)

    def bwd(do, xchg=None):
        def bwd_body(q_ref, k_ref, v_ref, o_ref, do_ref, dq_ref, dk_ref, dv_ref, *scr):
            t = pl.program_id(2)

            def run(keys, first):
                n = keys.stop
                dos = [jnp.where(_head_half(i, do_ref.shape), do_ref[...], 0.0) for i in range(MLA_HPS)]
                for i, hs in enumerate(heads):
                    scr[4 * i][:, :n] = _dg(q_ref[:, hs], k_ref[keys, hs], 1, 1)
                    scr[4 * i + 1][:, :n] = _dg(dos[i], v_ref[keys, :], 1, 1)
                dvs = []
                for i, hs in enumerate(heads):
                    dq, dk, dv = _attn_bwd_block(q_ref[:, hs], k_ref[keys, hs], o_ref[...], dos[i], scale,
                                                 *scr[4 * i:4 * i + 4])
                    dq_ref[:, hs] = dq
                    dvs.append(dv)
                    if first:
                        dk_ref[keys, hs] = dk
                    else:
                        dk_ref[keys, hs] += dk
                if first:
                    dv_ref[keys, :] = dvs[0] + dvs[1]
                else:
                    dv_ref[keys, :] += dvs[0] + dvs[1]

            @pl.when(t == 0)
            def _():
                dk_ref[...] = jnp.zeros_like(dk_ref)
                dv_ref[...] = jnp.zeros_like(dv_ref)
                if ctx_q:
                    run(slice(0, tc), True)
                else:
                    dq_ref[...] = jnp.zeros_like(dq_ref)

            @pl.when(t > 0)
            def _():
                run(slice(0, t_all), False)

        return _call_with_exchange(
            bwd_body, xchg, name=name + "_bwd", grid=grid, in_specs=[q_spec, k_spec, v_spec, o_spec, o_spec],
            out_specs=[q_spec, k_spec, v_spec],
            out_shape=[jax.ShapeDtypeStruct(q.shape, F32), jax.ShapeDtypeStruct(q.shape, F32), o_shape],
            operands=(q, k, v, o, do), scratch_shapes=[f32_scr, f32_scr, bf16_scr, bf16_scr] * MLA_HPS)

    return o, gathered, bwd


def _swa_block(q, keys, vals, sink, mask):
    qs = jnp.concatenate(list(_split(q, SWA_GROUP, 1)), axis=0)
    sk = jnp.sum(sink, axis=-1, keepdims=True) * (1.0 / LANE)
    s = _nt(qs, keys) * (SWA_HEAD_DIM ** -0.5)
    if mask is not None:
        s = jnp.where(mask, s, NEG_INF)
    o = _split(_nn(_softmax_rows(s, sk), vals + _roll(vals, LANE // 2, 1)), SWA_GROUP, 0)
    low = _head_half(0, o[0].shape)
    return jnp.concatenate([jnp.where(low, o[0], o[1]), jnp.where(low, o[2], o[3])], axis=1)


def _swa_ctx_block(q, kc, vc, sink):
    return _swa_block(q, kc, vc, sink, None)


def _swa_win_block(q, kc, kw, vc, vw, sink, mask):
    return _swa_block(q, jnp.concatenate([kc, kw], axis=0), jnp.concatenate([vc, vw], axis=0), sink, mask)


def _swa_attn(q, k, p_all, sink_b, tc, ctx_q, name, xchg=None):
    bsz, t_all, _ = q.shape
    n_q = t_all // QB_SWA
    n_cq = tc // QB_SWA
    lat = t_all - tc
    span = QB_SWA + 2 * WINDOW
    gw = SWA_GROUP * LANE
    grid = (bsz, SWA_KV_HEADS, n_q)
    q_spec = pl.BlockSpec((None, QB_SWA, gw), lambda b, g, i: (b, i, g))
    k_spec = pl.BlockSpec((None, t_all, LANE), lambda b, g, i: (b, 0, g))
    v_spec = pl.BlockSpec((None, t_all, LANE), lambda b, g, i: (b, 0, PC_SV // LANE + g))
    s_spec = pl.BlockSpec((None, SWA_GROUP * QB_SWA, LANE), lambda b, g, i: (g, 0, 0))

    def window(i):
        q0 = (i - n_cq) * QB_SWA
        w0 = jnp.clip(q0 - WINDOW, 0, lat - span)
        w0 = pl.multiple_of(w0, WINDOW)
        shape = (SWA_GROUP * QB_SWA, tc + span)
        qi = q0 + lax.broadcasted_iota(jnp.int32, shape, 0) % QB_SWA
        col = lax.broadcasted_iota(jnp.int32, shape, 1)
        kj = w0 + col - tc
        mask = (col < tc) | ((kj >= qi - WINDOW) & (kj <= qi + WINDOW))
        return w0, mask

    def fwd_body(q_ref, k_ref, v_ref, s_ref, o_ref):
        i = pl.program_id(2)

        @pl.when(i < n_cq)
        def _():
            if ctx_q:
                o_ref[...] = _swa_ctx_block(q_ref[...].astype(F32), k_ref[0:tc, :], v_ref[0:tc, :].astype(F32),
                                            s_ref[...])
            else:
                o_ref[...] = jnp.zeros_like(o_ref)

        @pl.when(i >= n_cq)
        def _():
            w0, mask = window(i)
            o_ref[...] = _swa_win_block(q_ref[...].astype(F32), k_ref[0:tc, :], k_ref[pl.ds(tc + w0, span), :],
                                        v_ref[0:tc, :].astype(F32), v_ref[pl.ds(tc + w0, span), :].astype(F32),
                                        s_ref[...], mask)

    o_spec = pl.BlockSpec((None, QB_SWA, SWA_GROUP * SWA_HEAD_DIM), lambda b, g, i: (b, i, g))
    (o,), gathered = _call_with_exchange(
        fwd_body, xchg, name=name, grid=grid, in_specs=[q_spec, k_spec, v_spec, s_spec], out_specs=[o_spec],
        out_shape=[jax.ShapeDtypeStruct((bsz, t_all, SWA_HEADS * SWA_HEAD_DIM), F32)],
        operands=(q, k, p_all, sink_b))

    def bwd(do, xchg=None):
        def bwd_body(q_ref, k_ref, v_ref, s_ref, do_ref, dq_ref, dk_ref, dv_ref, ds_ref):
            i = pl.program_id(2)

            @pl.when(i == 0)
            def _():
                dk_ref[...] = jnp.zeros_like(dk_ref)
                dv_ref[...] = jnp.zeros_like(dv_ref)
                ds_ref[...] = jnp.zeros_like(ds_ref)

            @pl.when(i < n_cq)
            def _():
                if ctx_q:
                    _, vjp = jax.vjp(_swa_ctx_block, q_ref[...].astype(F32), k_ref[0:tc, :].astype(F32),
                                     v_ref[0:tc, :].astype(F32), s_ref[...])
                    dq, dk, dv, ds = vjp(do_ref[...])
                    dq_ref[...] = dq
                    dk_ref[0:tc, :] += dk
                    dv_ref[0:tc, :] += dv
                    ds_ref[...] += ds
                else:
                    dq_ref[...] = jnp.zeros_like(dq_ref)

            @pl.when(i >= n_cq)
            def _():
                w0, mask = window(i)
                win = pl.ds(tc + w0, span)
                _, vjp = jax.vjp(functools.partial(_swa_win_block, mask=mask), q_ref[...].astype(F32),
                                 k_ref[0:tc, :].astype(F32), k_ref[win, :].astype(F32),
                                 v_ref[0:tc, :].astype(F32), v_ref[win, :].astype(F32), s_ref[...])
                dq, dkc, dkw, dvc, dvw, ds = vjp(do_ref[...])
                dq_ref[...] = dq
                dk_ref[0:tc, :] += dkc
                dk_ref[win, :] += dkw
                dv_ref[0:tc, :] += dvc
                dv_ref[win, :] += dvw
                ds_ref[...] += ds

        kv_out = pl.BlockSpec((None, t_all, LANE), lambda b, g, i: (b, 0, g))
        ds_spec = pl.BlockSpec((None, None, SWA_GROUP * QB_SWA, LANE), lambda b, g, i: (b, g, 0, 0))
        kv_shape = jax.ShapeDtypeStruct((bsz, t_all, SWA_KV_HEADS * LANE), F32)
        return _call_with_exchange(
            bwd_body, xchg, name=name + "_bwd", grid=grid, in_specs=[q_spec, k_spec, v_spec, s_spec, o_spec],
            out_specs=[q_spec, kv_out, kv_out, ds_spec],
            out_shape=[jax.ShapeDtypeStruct(q.shape, F32), kv_shape, kv_shape,
                       jax.ShapeDtypeStruct((bsz,) + sink_b.shape, F32)],
            operands=(q, k, p_all, sink_b, do))

    return o, gathered, bwd


def _scan_pair(chains, scratch):
    t_all, c = chains[0][0].shape
    n_tiles = t_all // SUBLANE
    row8 = lax.broadcasted_iota(jnp.int32, (t_all, c), 0) % SUBLANE
    refs = [scratch[0:3], scratch[3:6]]
    for (a, u, reverse), (a_s, u_s, _) in zip(chains, refs):
        for d in (1, 2, 4):
            sh = d if not reverse else t_all - d
            ar, ur = pltpu.roll(a, sh, 0), pltpu.roll(u, sh, 0)
            m = (row8 >= d) if not reverse else (row8 < SUBLANE - d)
            u = jnp.where(m, a * ur + u, u)
            a = jnp.where(m, a * ar, a)
        a_s[...] = a
        u_s[...] = u

    def step(j, carries):
        out = []
        for (_, _, reverse), (a_s, u_s, c_s), carry in zip(chains, refs, carries):
            tile = j if not reverse else n_tiles - 1 - j
            base = pl.multiple_of(tile * SUBLANE, SUBLANE)
            c_s[pl.ds(base, SUBLANE), :] = jnp.broadcast_to(carry, (SUBLANE, c))
            last = base + (0 if reverse else SUBLANE - 1)
            out.append(a_s[pl.ds(last, 1), :] * carry + u_s[pl.ds(last, 1), :])
        return tuple(out)

    lax.fori_loop(0, n_tiles, step, (jnp.zeros((1, c), F32),) * 2, unroll=4)
    return [a_s[...] * c_s[...] + u_s[...] for a_s, u_s, c_s in refs]


def _shift_rows(x, reverse_src):
    t_all = x.shape[0]
    row = lax.broadcasted_iota(jnp.int32, x.shape, 0)
    if reverse_src:
        return jnp.where(row == t_all - 1, 0.0, pltpu.roll(x, t_all - 1, 0))
    return jnp.where(row == 0, 0.0, pltpu.roll(x, 1, 0))


def _lru_scan(a0, u0, a1, u1, name):
    bsz, t_all, w = a0.shape
    grid = (bsz, w // LANE)
    spec = pl.BlockSpec((None, t_all, LANE), lambda b, c: (b, 0, c))
    scratch = [pltpu.VMEM((t_all, LANE), F32)] * 6
    shape = jax.ShapeDtypeStruct(a0.shape, F32)

    def fwd_body(a0_ref, u0_ref, a1_ref, u1_ref, h0_ref, h1_ref, *scr):
        h0_ref[...], h1_ref[...] = _scan_pair([(a0_ref[...], u0_ref[...], False), (a1_ref[...], u1_ref[...], True)],
                                              scr)

    wide = pl.BlockSpec((None, t_all, 2 * LANE), lambda b, c: (b, 0, c))
    h0, h1 = _pcall(fwd_body, name=name, grid=(bsz, w // (2 * LANE)), in_specs=[wide] * 4, out_specs=[wide] * 2,
                    out_shape=[shape] * 2, scratch_shapes=[pltpu.VMEM((t_all, 2 * LANE), F32)] * 6,
                    compiler_params=_cparams(2))(a0, u0, a1, u1)

    def bwd(dh0, dh1):
        def bwd_body(a0_ref, h0_ref, g0_ref, a1_ref, h1_ref, g1_ref, da0_ref, du0_ref, da1_ref, du1_ref, *scr):
            g0, g1 = _scan_pair([(_shift_rows(a0_ref[...], True), g0_ref[...], True),
                                 (_shift_rows(a1_ref[...], False), g1_ref[...], False)], scr)
            du0_ref[...] = g0
            da0_ref[...] = g0 * _shift_rows(h0_ref[...], False)
            du1_ref[...] = g1
            da1_ref[...] = g1 * _shift_rows(h1_ref[...], True)

        return _pcall(bwd_body, name=name + "_bwd", grid=grid, in_specs=[spec] * 6, out_specs=[spec] * 4,
                      out_shape=[shape] * 4, scratch_shapes=scratch,
                      compiler_params=_cparams(2))(a0, h0, dh0, a1, h1, dh1)

    return h0, h1, bwd


def _f_mod(x, g, shift, scale):
    return (_rms(x, g, D_MODEL) * (1.0 + scale) + shift,)


def _f_mla_q(cq, ga, w, gh, cos, sa, sb):
    n = _rms(cq, ga, MLA_Q_RANK)
    outs = []
    for wh in _split(w, MLA_HEADS, 1):
        outs.append(_rope(_rms(_nn(n, wh), gh, MLA_QK), cos, sa, sb, MLA_ROPE // 4))
    return (jnp.concatenate(outs, axis=1),)


def _f_mla_kv(ckv, krp, ga, wk, wv, gh, cos, sa, sb):
    n = _rms(ckv, ga, MLA_KV_RANK)
    outs = []
    for wh in _split(wk, MLA_HEADS, 1):
        outs.append(_rope(_rms(_nn(n, wh) + krp, gh, MLA_QK), cos, sa, sb, MLA_ROPE // 4))
    return jnp.concatenate(outs, axis=1), _nn(n, wv)


def _f_conv(x, w0, w1, w2, w3, bias, tc):
    t_all = x.shape[0]
    row = lax.broadcasted_iota(jnp.int32, x.shape, 0)
    lo = jnp.where(row < tc, 0, tc)
    hi = jnp.where(row < tc, tc, t_all)
    y = bias + jnp.zeros_like(x)
    for kk, wk in enumerate((w0, w1, w2, w3)):
        src = row + (kk - 2)
        xs = x if kk == 2 else _roll(x, 2 - kk, 0)
        y = y + wk * jnp.where((src >= lo) & (src < hi), xs, 0.0)
    return (y,)


def _f_gates(xc, w16, b00, b01, b10, b11, sp0, sp1):
    ws = _unstack(w16)
    n_cb = LRU_WIDTH // LANE
    xcs = _split(xc, n_cb, 1)
    bias = [_split(b, n_cb, 1) for b in (b00, b01, b10, b11)]
    sps = [_split(s, n_cb, 1) for s in (sp0, sp1)]
    res = [[], [], [], []]
    for c in range(n_cb):
        for z in range(2):
            r = _sig(_nn(xcs[c], ws[c * 4 + 2 * z]) + bias[2 * z][c])
            i = _sig(_nn(xcs[c], ws[c * 4 + 2 * z + 1]) + bias[2 * z + 1][c])
            la = -LRU_C * r * sps[z][c]
            res[2 * z].append(jnp.exp(la))
            res[2 * z + 1].append(jnp.sqrt(-jnp.tanh(la) * (jnp.exp(2.0 * la) + 1.0)) * (i * xcs[c]))
    return tuple(jnp.concatenate(r, axis=1) for r in res)


def _f_swa_qk(sq, sk, gq, gk, cos, sa, sb):
    qs = [_rope(_rms(x, gq, SWA_HEAD_DIM), cos, sa, sb, SWA_HEAD_DIM // 4) for x in _split(sq, SWA_HEADS, 1)]
    ks = [_rope(_rms(x, gk, SWA_HEAD_DIM), cos, sa, sb, SWA_HEAD_DIM // 4) for x in _split(sk, SWA_KV_HEADS, 1)]
    return jnp.concatenate(qs, axis=1), jnp.concatenate(ks, axis=1)


def _f_qkv(cq, ckv, krp, sq, sk, q_a_g, wuq, mla_q_g, kv_a_g, wk, wv, mla_k_g, swa_q_g, swa_k_g,
           m_cos, m_sa, m_sb, s_cos, s_sa, s_sb):
    return (*_f_mla_q(cq, q_a_g, wuq, mla_q_g, m_cos, m_sa, m_sb),
            *_f_mla_kv(ckv, krp, kv_a_g, wk, wv, mla_k_g, m_cos, m_sa, m_sb),
            *_f_swa_qk(sq, sk, swa_q_g, swa_k_g, s_cos, s_sa, s_sb))


def _f_merge(oa, h0, h1, lg, oc, ga, gb, gc):
    ob = (h0 + h1) * _gelu(lg)
    return (jnp.concatenate([_rms(oa, ga, GROUP_WIDTH), _rms(ob, gb, GROUP_WIDTH), _rms(oc, gc, GROUP_WIDTH)],
                            axis=1),)


def _f_resid_mod(x, y, gate, g, shift, scale):
    x1 = x + gate * y
    return x1, _rms(x1, g, D_MODEL) * (1.0 + scale) + shift


def _f_resid(x, y, gate):
    return (x + gate * y,)


def _hosted(hooks, key, arg=None):
    make, done = hooks.get(key, (None, None))
    xchg = make(arg) if make is not None else None
    return xchg, (done if xchg is not None else lambda outs: None)


def _layer(li, x, mods, w, s, tabs, tc, ctx_q, hooks, latent_dx_only):
    bsz, t_all, _ = x.shape
    n_t = t_all // TB
    grid = (bsz, n_t)
    rows = lambda b, t: (b, t, 0)

    def row(arr, width=None, idx=0, gdtype=F32, gshape=None):
        width = width or arr.shape[-1]
        return _A(arr, (None, TB, width), lambda b, t: (b, t, idx), "row", gdtype=gdtype, gshape=gshape,
                  gimap=rows if gshape is not None else None)

    def out(width, dtype, imap=rows):
        return ((bsz, t_all, width), dtype, (None, TB, width), imap)

    def modarg(arr):
        return _A(arr, (None, None, 1, D_MODEL), lambda b, t: (b, jnp.minimum(t, 1), 0, 0), "acc",
                  first=lambda b, t: t <= 1)

    def tab(arr):
        return _A(arr, (TB, LANE), lambda b, t: (t, 0), "const")

    def pcol(p_all, col, width):
        return row(p_all, width, col // width, gdtype=BF16, gshape=(bsz, t_all, width))

    nm = lambda base: "%s_l%d" % (base, li)
    sh1, sc1, g1, sh2, sc2, g2 = mods
    m_all = bsz * t_all

    x_arg = row(x)
    if latent_dx_only:
        n_c = tc // TB
        x_arg.gshape, x_arg.gimap = (bsz, t_all - tc, D_MODEL), lambda b, t: (b, jnp.maximum(t - n_c, 0), 0)
    (h,), b_mod1 = _rowop(nm("mod1"), _f_mod, grid, [x_arg, _par(s["norm1_g"]), modarg(sh1), modarg(sc1)],
                          [out(D_MODEL, BF16)])
    p_all = _mm(h.reshape(m_all, D_MODEL), w["win"], "nn", BF16, nm("mm_in")).reshape(bsz, t_all, P_WIDTH)

    (q_a, k_a, v_a, q_c, k_c), b_qkv = _rowop(
        nm("qkv"), _f_qkv, grid,
        [pcol(p_all, PC_CQ, 256), pcol(p_all, PC_CKV, 128), pcol(p_all, PC_KR, 128), pcol(p_all, PC_SQ, 1024),
         pcol(p_all, PC_SK, 256)]
        + [_par(a) for a in (s["q_a_g"], w["wuq"], s["mla_q_g"], s["kv_a_g"], w["wk"], w["wv"], s["mla_k_g"],
                             s["swa_q_g"], s["swa_k_g"])]
        + [tab(a) for a in tabs["mla"] + tabs["swa"]],
        [out(MLA_HEADS * LANE, BF16), out(MLA_HEADS * LANE, BF16), out(MLA_HEADS * MLA_V, BF16),
         out(SWA_HEADS * LANE, BF16), out(SWA_KV_HEADS * LANE, BF16)])

    xchg, done = _hosted(hooks, "mla_fwd")
    o_a, got, b_attn_a = _mla_attn(q_a, k_a, v_a, tc, ctx_q, nm("mla_attn"), xchg)
    done(got)

    n_cb = LRU_WIDTH // LANE
    conv_grid = (n_cb, bsz)
    cpar = lambda arr: _A(arr, (1, LANE), lambda c, b: (0, c), "acc", first=lambda c, b: b == 0)
    conv_args = [_A(p_all, (None, t_all, LANE), lambda c, b: (b, 0, PC_LX // LANE + c), "row", gdtype=BF16,
                    gshape=(bsz, t_all, LRU_WIDTH), gimap=lambda c, b: (b, 0, c))]
    conv_args += [cpar(a) for a in s["conv_w"]] + [cpar(s["conv_b"])]
    conv_out = [((bsz, t_all, LRU_WIDTH), F32, (None, t_all, LANE), lambda c, b: (b, 0, c))]
    (xc,), b_conv = _rowop(nm("lru_conv"), functools.partial(_f_conv, tc=tc), conv_grid, conv_args, conv_out)
    rot = lambda b, t: (b, (t + n_t - 1) % n_t, 0)
    (a0, u0, a1, u1), b_gates = _rowop(
        nm("lru_gates"), _f_gates, grid,
        [row(xc), _par(s["wbd"])] + [_par(a) for a in s["gate_b"]] + [_par(a) for a in s["sp"]],
        [out(LRU_WIDTH, F32), out(LRU_WIDTH, F32), out(LRU_WIDTH, F32, rot), out(LRU_WIDTH, F32, rot)])
    h0, h1, b_scan = _lru_scan(a0, u0, a1, u1, nm("lru_scan"))
    h1_arg = _A(h1, (None, TB, LRU_WIDTH), rot, "row")

    xchg, done = _hosted(hooks, "swa_fwd")
    o_c, got, b_attn_c = _swa_attn(q_c, k_c, p_all, s["sink_b"], tc, ctx_q, nm("swa_attn"), xchg)
    done(got)

    (y_in,), b_merge = _rowop(nm("merge"), _f_merge, grid,
                              [row(o_a), row(h0), h1_arg, pcol(p_all, PC_LG, 512), row(o_c), _par(s["g_a"]),
                               _par(s["g_b"]), _par(s["g_c"])],
                              [out(MIX_P, BF16)])
    y = _mm(y_in.reshape(m_all, MIX_P), w["wout"], "nn", F32, nm("mm_out")).reshape(bsz, t_all, D_MODEL)
    (x1, hm), b_rm = _rowop(nm("resid_mod"), _f_resid_mod, grid,
                            [row(x), row(y, gdtype=BF16), modarg(g1), _par(s["norm2_g"]), modarg(sh2), modarg(sc2)],
                            [out(D_MODEL, F32), out(D_MODEL, BF16)])
    pre, act = _mm(hm.reshape(m_all, D_MODEL), w["ff1"], "nn", BF16, nm("mm_ff1"), epi="sqrelu")
    y2 = _mm(act, w["ff2"], "nn", F32, nm("mm_ff2")).reshape(bsz, t_all, D_MODEL)
    (x2,), b_res = _rowop(nm("resid"), _f_resid, grid,
                          [_A(x1, (None, TB, D_MODEL), rows, "fwd"), row(y2, gdtype=BF16), modarg(g2)],
                          [out(D_MODEL, F32)])

    def bwd(dx2, hooks):
        dw, ds = {}, {}
        dy2, dg2 = b_res(dx2)
        dy2 = dy2.reshape(m_all, D_MODEL)
        dpre = _mm(dy2, w["ff2"], "nt", BF16, nm("mm_ff2_dx"), epi="dsqrelu", aux=pre)
        dw["ff2"] = _mm(act, dy2, "tn", BF16, nm("mm_ff2_dw"))
        dhm = _mm(dpre, w["ff1"], "nt", F32, nm("mm_ff1_dx")).reshape(bsz, t_all, D_MODEL)
        dw["ff1"] = _mm(hm.reshape(m_all, D_MODEL), dpre, "tn", BF16, nm("mm_ff1_dw"), out_split=N_DEV)
        dxa, dy, dg1, ds["norm2_g"], dsh2, dsc2 = b_rm(dx2, dhm)
        dy = dy.reshape(m_all, D_MODEL)
        dy_in = _mm(dy, w["wout"], "nt", F32, nm("mm_out_dx")).reshape(bsz, t_all, MIX_P)
        dw["wout"] = _mm(y_in.reshape(m_all, MIX_P), dy, "tn", BF16, nm("mm_out_dw"))
        do_a, dh0, dh1, dlg, do_c, ds["g_a"], ds["g_b"], ds["g_c"] = b_merge(dy_in)

        (dq_c, dk_c, dsv, dsink), _ = b_attn_c(do_c)
        ds["sink_b"] = jnp.sum(dsink, axis=0)

        da0, du0, da1, du1 = b_scan(dh0, dh1)
        gates_g = b_gates(da0, du0, da1, du1)
        dxc, ds["wbd"] = gates_g[0], gates_g[1]
        ds["gate_b"], ds["sp"] = list(gates_g[2:6]), list(gates_g[6:8])
        conv_g = b_conv(dxc)
        dlx, ds["conv_w"], ds["conv_b"] = conv_g[0], list(conv_g[1:5]), conv_g[5]

        xchg, done = _hosted(hooks, "mla_bwd", (dw, ds))
        (dq_a, dk_a, dv_a), got = b_attn_a(do_a, xchg)
        done(got)
        (dcq, dckv, dkr, dsq, dsk, ds["q_a_g"], dw["wuq"], ds["mla_q_g"], ds["kv_a_g"], dw["wk"], dw["wv"],
         ds["mla_k_g"], ds["swa_q_g"], ds["swa_k_g"]) = b_qkv(dq_a, dk_a, dv_a, dq_c, dk_c)

        dp = jnp.concatenate([dsq, dlx, dlg, dcq, dsk, dsv.astype(BF16), dckv, dkr], axis=-1)
        dp = dp.reshape(m_all, P_WIDTH)
        dh = _mm(dp, w["win"], "nt", F32, nm("mm_in_dx")).reshape(bsz, t_all, D_MODEL)
        dw["win"] = _mm(h.reshape(m_all, D_MODEL), dp, "tn", BF16, nm("mm_in_dw"))
        dx, ds["norm1_g"], dsh1, dsc1 = b_mod1(dh, add_to_first=dxa)
        return dx, [dsh1, dsc1, dg1, dsh2, dsc2, dg2], dw, ds

    return x2, bwd


def _loss_and_grad(x2, target, tc):
    bsz, t_all, d = x2.shape
    n_t = t_all // TB
    n_c = tc // TB

    def body(x_ref, t_ref, l_ref, dx_ref):
        b, t = pl.program_id(0), pl.program_id(1)

        @pl.when((b == 0) & (t == 0))
        def _():
            l_ref[...] = jnp.zeros_like(l_ref)

        @pl.when(t < n_c)
        def _():
            dx_ref[...] = jnp.zeros_like(dx_ref)

        @pl.when(t >= n_c)
        def _():
            e = x_ref[...] - t_ref[...]
            dx_ref[...] = e * (1.0 / d)
            l_ref[...] += jnp.sum(e * e) * (0.5 / d)

    loss, dx = _pcall(
        body, name="loss", grid=(bsz, n_t),
        in_specs=[pl.BlockSpec((None, TB, d), lambda b, t: (b, t, 0)),
                  pl.BlockSpec((None, TB, d), lambda b, t: (b, jnp.maximum(t - n_c, 0), 0))],
        out_specs=[pl.BlockSpec((SUBLANE, LANE), lambda b, t: (0, 0)),
                   pl.BlockSpec((None, TB, d), lambda b, t: (b, t, 0))],
        out_shape=[jax.ShapeDtypeStruct((SUBLANE, LANE), F32), jax.ShapeDtypeStruct(x2.shape, F32)],
        compiler_params=_cparams(2))(x2, target)
    return loss[0, 0], dx


def _rope_tables(lat, tc, dim, lane0):
    quarter = dim // 4
    pos = np.arange(lat)
    grid_pos = np.stack([pos // GRID_W, pos % GRID_W], axis=-1).astype(np.float32)
    lane = np.arange(LANE)
    p = np.clip(lane - lane0, 0, dim - 1)
    active = (lane >= lane0) & (lane < lane0 + dim)
    axis, half, qi = p // (dim // 2), (p % (dim // 2)) // quarter, p % quarter
    inv = (np.float32(ROPE_THETA) ** (-qi.astype(np.float32) / np.float32(quarter))).astype(np.float32)
    ang = (np.where(axis[None, :] == 0, grid_pos[:, 0:1], grid_pos[:, 1:2]) * inv[None, :]).astype(np.float32)
    cos = np.where(active, np.cos(ang), 1.0).astype(np.float32)
    sin = np.where(active, np.sin(ang), 0.0).astype(np.float32)
    sa = np.where(half == 0, -sin, 0.0).astype(np.float32)
    sb = np.where(half == 1, sin, 0.0).astype(np.float32)
    ctx1, ctx0 = np.ones((tc, LANE), np.float32), np.zeros((tc, LANE), np.float32)
    return tuple(jnp.asarray(np.concatenate([c, t], 0)) for c, t in ((ctx1, cos), (ctx0, sa), (ctx0, sb)))


_BIG = {"w_in": ((D_MODEL, IN_WIDTH // N_DEV), 1, ("win",)),
        "w_uq": ((MLA_Q_RANK, MLA_HEADS * MLA_QK // N_DEV), 1, ("wuq",)),
        "w_ukv": ((MLA_KV_RANK, MLA_HEADS * (MLA_NOPE + MLA_V) // N_DEV), 1, ("wk", "wv")),
        "w_out": ((3 * GROUP_WIDTH // N_DEV, D_MODEL), 0, ("wout",)),
        "w_ff1": ((D_MODEL, D_FF // N_DEV), 1, ("ff1",)),
        "w_ff2": ((D_FF // N_DEV, D_MODEL), 0, ("ff2",))}
_EARLY = ("w_in", "w_uq", "w_ukv")
_LATE = ("w_out", "w_ff1", "w_ff2")


def _pad_heads(wm, n_heads, dim):
    out = jnp.pad(wm.reshape(wm.shape[0], n_heads, dim), ((0, 0), (0, 0), (0, LANE - dim)))
    return out.reshape(wm.shape[0], n_heads * LANE)


def _prep_weight(name, piece):
    shp, ax, _ = _BIG[name]
    full = jnp.moveaxis(piece, 0, ax).reshape(shp[:ax] + (N_DEV * shp[ax],) + shp[ax + 1:])
    if name == "w_in":
        cq, ckv, kr, lx, lg, sq, sk, sv = _split_cols(full)
        return {"win": jnp.concatenate(
            [_pad_heads(sq, SWA_HEADS, SWA_HEAD_DIM), lx, lg, cq, _pad_heads(sk, SWA_KV_HEADS, SWA_HEAD_DIM),
             _pad_heads(sv, SWA_KV_HEADS, SWA_HEAD_DIM), ckv, jnp.pad(kr, ((0, 0), (MLA_NOPE, LANE - MLA_QK)))], axis=1)}
    if name == "w_uq":
        return {"wuq": _pad_heads(full, MLA_HEADS, MLA_QK)}
    if name == "w_ukv":
        ukv = full.reshape(MLA_KV_RANK, MLA_HEADS, MLA_NOPE + MLA_V)
        return {"wk": _pad_heads(ukv[:, :, :MLA_NOPE].reshape(MLA_KV_RANK, -1), MLA_HEADS, MLA_NOPE),
                "wv": ukv[:, :, MLA_NOPE:].reshape(MLA_KV_RANK, -1)}
    return {_BIG[name][2][0]: full}


def _split_cols(wm):
    parts, start = [], 0
    for size in IN_SIZES:
        parts.append(wm[:, start:start + size])
        start += size
    return parts


def _prep_gates(gate_w):
    gw = gate_w.reshape(2, 2, 4, 2, 64, 64)
    wbd = jnp.einsum("zgknCm,nN->knCzgNm", gw, jnp.eye(2, dtype=F32)).reshape(4, LANE, 4, LANE)
    return wbd.transpose(0, 2, 1, 3).reshape(16, LANE, LANE)


def _prep_small(raw):
    r1 = lambda a: a.reshape(1, -1)
    gg = raw["group_g"]
    sink = raw["swa_sink"].reshape(SWA_KV_HEADS, SWA_GROUP, 1, 1)
    return {
        "norm1_g": r1(raw["norm1_g"]), "norm2_g": r1(raw["norm2_g"]),
        "q_a_g": r1(raw["q_a_g"]), "kv_a_g": r1(raw["kv_a_g"]),
        "mla_q_g": jnp.pad(r1(raw["mla_q_g"]), ((0, 0), (0, LANE - MLA_QK))),
        "mla_k_g": jnp.pad(r1(raw["mla_k_g"]), ((0, 0), (0, LANE - MLA_QK))),
        "swa_q_g": jnp.pad(r1(raw["swa_q_g"]), ((0, 0), (0, LANE - SWA_HEAD_DIM))),
        "swa_k_g": jnp.pad(r1(raw["swa_k_g"]), ((0, 0), (0, LANE - SWA_HEAD_DIM))),
        "conv_w": [r1(raw["conv_w"][kk]) for kk in range(4)], "conv_b": r1(raw["conv_b"]),
        "gate_b": [r1(raw["lru_gate_b"][z, g]) for z in range(2) for g in range(2)],
        "sp": [r1(jax.nn.softplus(-raw["lru_lambda"][z])) for z in range(2)],
        "sink_b": jnp.broadcast_to(sink, (SWA_KV_HEADS, SWA_GROUP, QB_SWA, LANE)).reshape(
            SWA_KV_HEADS, SWA_GROUP * QB_SWA, LANE),
        "g_a": r1(gg[:GROUP_WIDTH]), "g_b": r1(gg[GROUP_WIDTH:2 * GROUP_WIDTH]), "g_c": r1(gg[2 * GROUP_WIDTH:])}


def _mesh_pos():
    return lax.axis_index("x"), lax.axis_index("y"), lax.axis_index("c")


def _peer(pos, k):
    return tuple(1 - p if (k >> s) & 1 else p for p, s in zip(pos, (2, 1, 0)))


def _dev_index(pos):
    return 4 * pos[0] + 2 * pos[1] + pos[2]


class _Exchange:
    def __init__(self, bufs, gather):
        self.bufs = list(bufs)
        self.n = len(self.bufs)
        self.gather = [gather] * self.n if isinstance(gather, bool) else list(gather)
        self.specs = [pl.BlockSpec(memory_space=pl.ANY)] * self.n
        self.out_shape = [jax.ShapeDtypeStruct((N_DEV,) + tuple(b.shape if g else b.shape[1:]), b.dtype)
                          for b, g in zip(self.bufs, self.gather)]
        self.scratch = [pltpu.SemaphoreType.DMA(((N_DEV - 1) * self.n,)),
                        pltpu.SemaphoreType.DMA(((N_DEV - 1) * self.n,)), pltpu.SemaphoreType.DMA((self.n,))]

    def _copies(self, x_refs, o_refs, sems, with_recvs):
        send_sems, recv_sems, local_sems = sems
        pos = _mesh_pos()
        me = _dev_index(pos)
        locals_, sends, recvs = [], [], []
        for j in range(self.n):
            src_mine = x_refs[j] if self.gather[j] else x_refs[j].at[me]
            locals_.append(pltpu.make_async_copy(src_mine, o_refs[j].at[me], local_sems.at[j]))
        for k in range(1, N_DEV):
            peer = _peer(pos, k)
            pidx = _dev_index(peer)
            for j in range(self.n):
                src = x_refs[j] if self.gather[j] else x_refs[j].at[pidx]
                sem = (k - 1) * self.n + j
                sends.append(pltpu.make_async_remote_copy(
                    src_ref=src, dst_ref=o_refs[j].at[me], send_sem=send_sems.at[sem], recv_sem=recv_sems.at[sem],
                    device_id=peer, device_id_type=pl.DeviceIdType.MESH))
                if with_recvs:
                    recvs.append(pltpu.make_async_remote_copy(
                        src_ref=src, dst_ref=o_refs[j].at[pidx], send_sem=send_sems.at[sem],
                        recv_sem=recv_sems.at[sem], device_id=peer, device_id_type=pl.DeviceIdType.MESH))
        return locals_, sends, recvs

    def start(self, x_refs, o_refs, sems):
        locals_, sends, _ = self._copies(x_refs, o_refs, sems, False)
        for cp in locals_ + sends:
            cp.start()

    def wait(self, x_refs, o_refs, sems):
        locals_, sends, recvs = self._copies(x_refs, o_refs, sems, True)
        for cp in recvs:
            cp.wait_recv()
        for cp in sends:
            cp.wait_send()
        for cp in locals_:
            cp.wait()


def _exchange(bufs, gather, name):
    xchg = _Exchange(bufs, gather)
    n = xchg.n

    def body(*refs):
        xchg.start(refs[:n], refs[n:2 * n], refs[2 * n:])
        xchg.wait(refs[:n], refs[n:2 * n], refs[2 * n:])

    return _pcall(body, name=name, out_shape=xchg.out_shape, in_specs=xchg.specs, out_specs=xchg.specs,
                  scratch_shapes=xchg.scratch)(*xchg.bufs)


def _pack(arrs, dtype):
    flat = jnp.concatenate([a.reshape(-1).astype(dtype) for a in arrs])
    rows = -(-flat.size // PACK_W)
    rows = -(-rows // 16) * 16
    return jnp.pad(flat, (0, rows * PACK_W - flat.size)).reshape(rows, PACK_W)


def _unpack(buf, shapes, lead=()):
    flat = buf.reshape(lead + (-1,))
    out, off = [], 0
    for shp in shapes:
        n = math.prod(shp)
        out.append(flat[..., off:off + n].reshape(lead + tuple(shp)))
        off += n
    return out


def _sum_sources(buf, name):
    _, r, c = buf.shape
    tr = _rows_tile(r)

    def body(x_ref, o_ref):
        acc = x_ref[0]
        for d in range(1, N_DEV):
            acc = acc + x_ref[d]
        o_ref[...] = acc

    return _pcall(body, name=name, grid=(r // tr,),
                  in_specs=[pl.BlockSpec((N_DEV, tr, c), lambda i: (0, i, 0))],
                  out_specs=pl.BlockSpec((tr, c), lambda i: (i, 0)),
                  out_shape=jax.ShapeDtypeStruct((r, c), F32), compiler_params=_cparams(1))(buf)


def _rows_tile(r):
    best = r
    for t in range(SUBLANE, ELEMWISE_ROWS_MAX + 1, SUBLANE):
        if r % t == 0:
            best = t
    return best


def _adamw(grads, wgt, m, v, name):
    n_lay = len(grads)
    n_src, r, c = grads[0].shape
    tr = _rows_tile(r)
    n_blk = r // tr
    bc1 = 1.0 - ADAM_B1 ** ADAM_STEP
    bc2 = 1.0 - ADAM_B2 ** ADAM_STEP

    def body(*refs):
        g_refs, (w_ref, m_ref, v_ref, go_ref, d_ref, mo_ref, vo_ref) = refs[:n_lay], refs[n_lay:]
        for li, g_ref in enumerate(g_refs):
            @pl.when(pl.program_id(0) == li)
            def _():
                g = g_ref[0].astype(F32)
                for d in range(1, n_src):
                    g = g + g_ref[d].astype(F32)
                m_new = ADAM_B1 * m_ref[...] + (1.0 - ADAM_B1) * g
                v_new = ADAM_B2 * v_ref[...] + (1.0 - ADAM_B2) * (g * g)
                go_ref[...] = g
                mo_ref[...] = m_new
                vo_ref[...] = v_new
                d_ref[...] = -ADAM_LR * ((m_new / bc1) / (jnp.sqrt(v_new / bc2) + ADAM_EPS) + ADAM_WD * w_ref[...])

    g_specs = [pl.BlockSpec((n_src, tr, c),
                            lambda l, i, li=li: (0, jnp.where(l == li, i, jnp.where(l > li, n_blk - 1, 0)), 0))
               for li in range(n_lay)]
    spec = pl.BlockSpec((tr, c), lambda l, i: (l * n_blk + i, 0))
    return _pcall(body, name=name, grid=(n_lay, n_blk), in_specs=g_specs + [spec, spec, spec],
                  out_specs=[spec] * 4, out_shape=[jax.ShapeDtypeStruct((n_lay * r, c), F32)] * 4,
                  compiler_params=_cparams(2))(*grads, wgt, m, v)


def _silu(z):
    return z * jax.nn.sigmoid(z)


_WEIGHTS = ("c_ctx", "w_mod", "b_mod", "norm1_g", "w_in", "q_a_g", "w_uq", "kv_a_g", "w_ukv", "mla_q_g", "mla_k_g",
            "conv_w", "conv_b", "lru_gate_w", "lru_gate_b", "lru_lambda", "swa_q_g", "swa_k_g", "swa_sink", "group_g",
            "w_out", "norm2_g", "w_ff1", "w_ff2")
_SHARDED_SMALL = ("conv_w", "lru_gate_b", "lru_lambda")
_REPL_RAW = ("norm1_g", "q_a_g", "kv_a_g", "mla_q_g", "mla_k_g", "conv_b", "swa_q_g", "swa_k_g",
             "swa_sink", "group_g", "norm2_g")
MOD_ROWS = 32


def kernel(x, c, ctx, c_ctx, w_mod, b_mod, norm1_g, w_in, q_a_g, w_uq, kv_a_g, w_ukv, mla_q_g, mla_k_g, conv_w, conv_b, lru_gate_w, lru_gate_b, lru_lambda, swa_q_g, swa_k_g, swa_sink, group_g, w_out, norm2_g, w_ff1, w_ff2, loss_target, m_c_ctx, m_w_mod, m_b_mod, m_norm1_g, m_w_in, m_q_a_g, m_w_uq, m_kv_a_g, m_w_ukv, m_mla_q_g, m_mla_k_g, m_conv_w, m_conv_b, m_lru_gate_w, m_lru_gate_b, m_lru_lambda, m_swa_q_g, m_swa_k_g, m_swa_sink, m_group_g, m_w_out, m_norm2_g, m_w_ff1, m_w_ff2, v_c_ctx, v_w_mod, v_b_mod, v_norm1_g, v_w_in, v_q_a_g, v_w_uq, v_kv_a_g, v_w_ukv, v_mla_q_g, v_mla_k_g, v_conv_w, v_conv_b, v_lru_gate_w, v_lru_gate_b, v_lru_lambda, v_swa_q_g, v_swa_k_g, v_swa_sink, v_group_g, v_w_out, v_norm2_g, v_w_ff1, v_w_ff2):
    wts = dict(c_ctx=c_ctx, w_mod=w_mod, b_mod=b_mod, norm1_g=norm1_g, w_in=w_in, q_a_g=q_a_g, w_uq=w_uq,
               kv_a_g=kv_a_g, w_ukv=w_ukv, mla_q_g=mla_q_g, mla_k_g=mla_k_g, conv_w=conv_w, conv_b=conv_b,
               lru_gate_w=lru_gate_w, lru_gate_b=lru_gate_b, lru_lambda=lru_lambda, swa_q_g=swa_q_g, swa_k_g=swa_k_g,
               swa_sink=swa_sink, group_g=group_g, w_out=w_out, norm2_g=norm2_g, w_ff1=w_ff1, w_ff2=w_ff2)
    mom1 = dict(zip(_WEIGHTS, (m_c_ctx, m_w_mod, m_b_mod, m_norm1_g, m_w_in, m_q_a_g, m_w_uq, m_kv_a_g, m_w_ukv,
                               m_mla_q_g, m_mla_k_g, m_conv_w, m_conv_b, m_lru_gate_w, m_lru_gate_b, m_lru_lambda,
                               m_swa_q_g, m_swa_k_g, m_swa_sink, m_group_g, m_w_out, m_norm2_g, m_w_ff1, m_w_ff2)))
    mom2 = dict(zip(_WEIGHTS, (v_c_ctx, v_w_mod, v_b_mod, v_norm1_g, v_w_in, v_q_a_g, v_w_uq, v_kv_a_g, v_w_ukv,
                               v_mla_q_g, v_mla_k_g, v_conv_w, v_conv_b, v_lru_gate_w, v_lru_gate_b, v_lru_lambda,
                               v_swa_q_g, v_swa_k_g, v_swa_sink, v_group_g, v_w_out, v_norm2_g, v_w_ff1, v_w_ff2)))
    bsz = x.shape[0]
    n_ex = bsz * N_DEV
    me = _dev_index(_mesh_pos())
    mod_cols = w_mod.shape[-1]

    small_shapes = [c.shape, conv_w.shape, lru_gate_b.shape, lru_lambda.shape]
    shard = lambda n, li: wts[n][li].astype(BF16)
    g_small, *early_pieces = _exchange([_pack([c, conv_w, lru_gate_b, lru_lambda], F32)] + [shard(n, 0) for n in _EARLY],
                                       True, "ag_first")
    c_all, conv_w_all, gate_b_all, lam_all = _unpack(g_small, small_shapes, lead=(N_DEV,))
    c_all = c_all.reshape(n_ex, D_MODEL)
    cat_last = lambda a: jnp.moveaxis(a, 0, -2).reshape(a.shape[1:-1] + (N_DEV * a.shape[-1],))
    conv_w_full, gate_b_full, lam_full = cat_last(conv_w_all), cat_last(gate_b_all), cat_last(lam_all)

    act = jnp.zeros((MOD_ROWS, D_MODEL), F32).at[:n_ex].set(_silu(c_all)).at[n_ex].set(_silu(c_ctx))
    mod_part = jnp.concatenate([_mm(act, w_mod[li], "nn", F32, "mm_mod_l%d" % li) for li in range(DEPTH)], axis=1)
    (mod_all,) = _exchange([mod_part], True, "ag_mod")
    mods = []
    for li in range(DEPTH):
        full = jnp.moveaxis(mod_all[:, :, li * mod_cols:(li + 1) * mod_cols], 0, 1).reshape(MOD_ROWS, -1) + b_mod[li]
        mine = lax.dynamic_slice_in_dim(full, me * bsz, bsz, axis=0)
        ctx_row = jnp.broadcast_to(full[n_ex], mine.shape)
        both = jnp.stack([ctx_row, mine], axis=1).reshape(bsz, 2, N_MOD, 1, D_MODEL)
        mods.append([both[:, :, j] for j in range(N_MOD)])

    raw = {n: wts[n] for n in _REPL_RAW}
    raw.update(conv_w=conv_w_full, lru_gate_b=gate_b_full, lru_lambda=lam_full)
    small_names = list(_REPL_RAW) + list(_SHARDED_SMALL)
    sp, small_vjp, gates_vjp = [None] * DEPTH, [None] * DEPTH, [None] * DEPTH
    for li in range(DEPTH):
        sp[li], small_vjp[li] = jax.vjp(_prep_small, {n: raw[n][li] for n in small_names})
        sp[li]["wbd"], gates_vjp[li] = jax.vjp(_prep_gates, lru_gate_w[li])

    w, w_vjp, g_recv, small_recv = [{} for _ in range(DEPTH)], {}, {}, {}

    def take(li, names, pieces):
        for n, piece in zip(names, pieces):
            out, w_vjp[n, li] = jax.vjp(functools.partial(_prep_weight, n), piece)
            w[li].update(out)

    def gather_hook(li, names):
        return (lambda _: _Exchange([shard(n, li) for n in names], True), lambda got: take(li, names, got))

    def wgrad(n, li, dwl):
        if n == "w_ff1":
            return dwl["ff1"]
        (g,) = w_vjp[n, li]({k: dwl[k].astype(BF16) for k in _BIG[n][2]})
        return g

    def small_pack(li, ds_l, extra=()):
        (d_raw,) = small_vjp[li]({k: v for k, v in ds_l.items() if k != "wbd"})
        return _pack([d_raw[n] for n in small_names] + list(extra), F32)

    def gates_grad(li, ds_l):
        return gates_vjp[li](ds_l["wbd"])[0].reshape(-1, LANE)

    take(0, _EARLY, early_pieces)
    hooks_fwd = [{"mla_fwd": gather_hook(0, _LATE), "swa_fwd": gather_hook(1, _EARLY + ("w_out",))},
                 {"mla_fwd": gather_hook(1, ("w_ff1", "w_ff2"))}]
    bwd_state = {}

    def scatter_last_layer(grads_so_far):
        dwl, dsl = grads_so_far
        return _Exchange([wgrad(n, 1, dwl) for n in _LATE] + [gates_grad(1, dsl)], [False] * len(_LATE) + [True])

    def scatter_first_layer(grads_so_far):
        dwl, dsl = grads_so_far
        dw1, ds1 = bwd_state["dw1"], bwd_state["ds1"]
        bufs = [wgrad(n, 1, dw1) for n in _EARLY] + [wgrad(n, 0, dwl) for n in _LATE]
        bufs += [small_pack(1, ds1), gates_grad(0, dsl)]
        return _Exchange(bufs, [False] * (len(_EARLY) + len(_LATE)) + [True] * 2)

    def scattered_first_layer(got):
        g_recv.update(zip([(n, 1) for n in _EARLY] + [(n, 0) for n in _LATE], got[:-2]))
        small_recv[1], g_recv["lru_gate_w", 0] = got[-2:]

    def scattered_last_layer(got):
        g_recv.update(zip([(n, 1) for n in _LATE], got[:-1]))
        g_recv["lru_gate_w", 1] = got[-1]

    hooks_bwd = [{"mla_bwd": (scatter_first_layer, scattered_first_layer)},
                 {"mla_bwd": (scatter_last_layer, scattered_last_layer)}]

    tc, lat = ctx.shape[1], x.shape[1]
    tabs = {"mla": _rope_tables(lat, tc, MLA_ROPE, MLA_NOPE), "swa": _rope_tables(lat, tc, SWA_HEAD_DIM, 0)}
    stream = jnp.concatenate([ctx, x], axis=1)
    bwds = []
    for li in range(DEPTH):
        stream, bwd = _layer(li, stream, mods[li], w[li], sp[li], tabs, tc, li < DEPTH - 1, hooks_fwd[li], li == 0)
        bwds.append(bwd)
    loss_part, dstream = _loss_and_grad(stream, loss_target, tc)
    dmods = [None] * DEPTH
    dstream, dmods[1], bwd_state["dw1"], bwd_state["ds1"] = bwds[1](dstream, hooks_bwd[1])
    grad_x, dmods[0], dw0, ds0 = bwds[0](dstream, hooks_bwd[0])

    dm_rows = []
    for li in range(DEPTH):
        dm = jnp.concatenate(dmods[li], axis=-1)
        dm_rows.append(jnp.concatenate([dm[:, 1, 0], jnp.sum(dm[:, 0, 0], axis=0, keepdims=True)], axis=0))
    dm_mine = jnp.concatenate(dm_rows, axis=1)
    dm_mine = jnp.pad(dm_mine, ((0, SUBLANE - bsz - 1), (0, 0)))
    (dm_all,) = _exchange([dm_mine], True, "ag_dmod")
    g_wmod, g_bmod, dact_ctx = [], [], jnp.zeros((D_MODEL,), F32)
    for li in range(DEPTH):
        part = dm_all[:, :, li * N_MOD * D_MODEL:(li + 1) * N_MOD * D_MODEL]
        dm32 = jnp.zeros((MOD_ROWS, N_MOD * D_MODEL), F32).at[:n_ex].set(part[:, :bsz].reshape(n_ex, -1))
        dm32 = dm32.at[n_ex].set(jnp.sum(part[:, bsz], axis=0))
        g_bmod.append(jnp.sum(dm32, axis=0))
        cols = lax.dynamic_slice_in_dim(dm32, me * mod_cols, mod_cols, axis=1)
        g_wmod.append(_mm(act, cols, "tn", F32, "mm_mod_dw_l%d" % li))
        dact_ctx = dact_ctx + _mm(cols, w_mod[li], "nt", F32, "mm_mod_dx_l%d" % li)[n_ex]
    sg = jax.nn.sigmoid(c_ctx)
    g_cctx_part = dact_ctx * (sg * (1.0 + c_ctx * (1.0 - sg)))

    last = _exchange([wgrad(n, 0, dw0) for n in _EARLY] + [small_pack(0, ds0, (g_cctx_part, loss_part.reshape(1)))],
                     [False] * len(_EARLY) + [True], "rs_early")
    g_recv.update(zip([(n, 0) for n in _EARLY], last[:-1]))
    small_recv[0] = last[-1]
    layer_shapes = [raw[n].shape[1:] for n in small_names]
    tot = [_unpack(_sum_sources(small_recv[li], "sum_grads_l%d" % li), layer_shapes + [(D_MODEL,), (1,)][:2 * (li == 0)])
           for li in range(DEPTH)]
    grads = {n: jnp.stack([tot[li][j] for li in range(DEPTH)], axis=0) for j, n in enumerate(small_names)}
    grads["c_ctx"], loss = tot[0][-2], tot[0][-1][0]
    for n in _SHARDED_SMALL:
        width = wts[n].shape[-1]
        grads[n] = lax.dynamic_slice_in_dim(grads[n], me * width, width, axis=grads[n].ndim - 1)
    grads["b_mod"] = jnp.stack(g_bmod, axis=0)

    delta, new_m, new_v = {}, {}, {}
    per_layer = {n: [g_recv[n, li] for li in range(DEPTH)] for n in list(_BIG) + ["lru_gate_w"]}
    per_layer["w_mod"] = [g[None] for g in g_wmod]
    for n, srcs in per_layer.items():
        two_d = (DEPTH * math.prod(wts[n].shape[1:-1]), wts[n].shape[-1])
        srcs = [s.reshape((s.shape[0], two_d[0] // DEPTH, two_d[1])) for s in srcs]
        res = _adamw(srcs, wts[n].reshape(two_d), mom1[n].reshape(two_d), mom2[n].reshape(two_d), "adamw_" + n)
        grads[n], delta[n], new_m[n], new_v[n] = [r.reshape(wts[n].shape) for r in res]
    rest = [n for n in _WEIGHTS if n not in delta]
    shapes = [wts[n].shape for n in rest]
    res = _adamw([_pack([grads[n] for n in rest], F32)[None]], _pack([wts[n] for n in rest], F32),
                 _pack([mom1[n] for n in rest], F32), _pack([mom2[n] for n in rest], F32), "adamw_small")
    for tgt, buf in zip((delta, new_m, new_v), res[1:]):
        tgt.update(zip(rest, _unpack(buf, shapes)))

    return (loss, grad_x, *[grads[n] for n in _WEIGHTS], *[delta[n] for n in _WEIGHTS],
            *[new_m[n] for n in _WEIGHTS], *[new_v[n] for n in _WEIGHTS])
```

```python
import functools
import math

import jax
import jax.numpy as jnp
import numpy as np
from jax import lax
from jax.experimental import pallas as pl
from jax.experimental.pallas import tpu as pltpu

F32, BF16 = jnp.float32, jnp.bfloat16

N_DEV = 8
DEPTH = 2
D_MODEL = 1024
D_FF = 4096
N_MOD = 6
GRID_W = 64
WINDOW = 128
ROPE_THETA = 10000.0
EPS = 1e-6
NEG_INF = -1e30
LRU_C = 8.0
LRU_WIDTH = 512
MLA_HEADS, MLA_NOPE, MLA_ROPE, MLA_V = 8, 64, 32, 64
MLA_QK = MLA_NOPE + MLA_ROPE
MLA_Q_RANK, MLA_KV_RANK = 256, 128
SWA_HEADS, SWA_KV_HEADS, SWA_GROUP, SWA_HEAD_DIM = 8, 2, 4, 64
GROUP_WIDTH = 512
IN_SIZES = (256, 128, 32, 512, 512, 512, 128, 128)
IN_WIDTH = sum(IN_SIZES)
ADAM_LR, ADAM_B1, ADAM_B2, ADAM_EPS, ADAM_WD, ADAM_STEP = 0.001, 0.9, 0.999, 1e-08, 0.01, 10

LANE = 128
SUBLANE = 8
TB = 256
QKV_ROWS = 384
QB_SWA = 256
PACK_W = 1024
MM_K_MAX = 4608
MM_COLS_MAX = 1024
MM_FEATURE_ROWS = (1024, 768, 512)
MM_TOKEN_ROWS = 1152
ELEMWISE_ROWS_MAX = 256
MLA_HPS = 2
VMEM_LIMIT = 56 * 1024 * 1024
P_WIDTH = 3072
PC_SQ, PC_LX, PC_LG, PC_CQ, PC_SK, PC_SV, PC_CKV, PC_KR = 0, 1024, 1536, 2048, 2304, 2560, 2816, 2944
MIX_P = 1536


def _pcall(body, **kw):
    return pl.pallas_call(body, **kw)


def _cparams(n_grid):
    return pltpu.CompilerParams(dimension_semantics=("arbitrary",) * n_grid, vmem_limit_bytes=VMEM_LIMIT)


def _dg(a, b, ca, cb):
    return lax.dot_general(a.astype(BF16), b.astype(BF16), (((ca,), (cb,)), ((), ())),
                           preferred_element_type=F32)


@jax.custom_vjp
def _nn(a, b):
    return _dg(a, b, 1, 0)


@jax.custom_vjp
def _nt(a, b):
    return _dg(a, b, 1, 1)


@jax.custom_vjp
def _tn(a, b):
    return _dg(a, b, 0, 0)


_nn.defvjp(lambda a, b: (_nn(a, b), (a, b)), lambda r, ct: (_nt(ct, r[1]), _tn(r[0], ct)))
_nt.defvjp(lambda a, b: (_nt(a, b), (a, b)), lambda r, ct: (_nn(ct, r[1]), _tn(ct, r[0])))
_tn.defvjp(lambda a, b: (_tn(a, b), (a, b)), lambda r, ct: (_nt(r[1], ct), _nn(r[0], ct)))


@functools.partial(jax.custom_vjp, nondiff_argnums=(1, 2))
def _roll(x, shift, axis):
    return pltpu.roll(x, shift % x.shape[axis], axis)


_roll.defvjp(lambda x, shift, axis: (_roll(x, shift, axis), None),
             lambda shift, axis, _, ct: (_roll(ct, -shift, axis),))


@functools.partial(jax.custom_vjp, nondiff_argnums=(1, 2))
def _split(x, n, axis):
    w = x.shape[axis] // n
    return tuple(lax.slice_in_dim(x, i * w, (i + 1) * w, axis=axis) for i in range(n))


_split.defvjp(lambda x, n, axis: (_split(x, n, axis), None),
              lambda n, axis, _, cts: (jnp.concatenate(cts, axis=axis),))


@jax.custom_vjp
def _unstack(x):
    return tuple(x[i] for i in range(x.shape[0]))


_unstack.defvjp(lambda x: (_unstack(x), None), lambda _, cts: (jnp.stack(cts, axis=0),))


def _sig(x):
    return 0.5 * (jnp.tanh(0.5 * x) + 1.0)


def _gelu(x):
    return 0.5 * x * (1.0 + jnp.tanh(math.sqrt(2.0 / math.pi) * (x + 0.044715 * (x * x * x))))


def _rms(x, g, n):
    ms = jnp.sum(x * x, axis=-1, keepdims=True) * (1.0 / n)
    return x * lax.rsqrt(ms + EPS) * g


def _rope(y, cos, sa, sb, quarter):
    return y * cos + _roll(y, -quarter, 1) * sa + _roll(y, quarter, 1) * sb


def _softmax_rows(s, extra=None):
    m = jnp.max(s, axis=-1, keepdims=True)
    if extra is not None:
        m = jnp.maximum(m, extra)
    m = lax.stop_gradient(m)
    e = jnp.exp(s - m)
    den = jnp.sum(e, axis=-1, keepdims=True)
    if extra is not None:
        den = den + jnp.exp(extra - m)
    return e / den


class _A:
    def __init__(self, arr, block, imap, kind="row", first=None, gdtype=F32, gshape=None, gimap=None):
        self.arr, self.block, self.imap, self.kind, self.first = arr, block, imap, kind, first
        self.gdtype, self.gshape, self.gimap = gdtype, gshape, gimap


def _all_zero(*ids):
    return functools.reduce(jnp.logical_and, [i == 0 for i in ids])


def _par(arr):
    nd = arr.ndim
    return _A(arr, arr.shape, lambda *ids: (0,) * nd, "acc", first=_all_zero)


def _op_fwd(name, fn, grid, args, outs):
    n_in = len(args)

    def body(*refs):
        vals = [r[...].astype(F32) for r in refs[:n_in]]
        for r, v in zip(refs[n_in:], fn(*vals)):
            r[...] = v.astype(r.dtype)

    return _pcall(
        body, name=name, grid=grid,
        in_specs=[pl.BlockSpec(a.block, a.imap) for a in args],
        out_specs=[pl.BlockSpec(o[2], o[3]) for o in outs],
        out_shape=[jax.ShapeDtypeStruct(o[0], o[1]) for o in outs],
        compiler_params=_cparams(len(grid)),
    )(*[a.arr for a in args])


def _op_bwd(name, fn, grid, args, outs, ct_arrays, add_to_first=None):
    didx = [i for i, a in enumerate(args) if a.kind not in ("const", "fwd")]
    read = [i for i, a in enumerate(args) if a.kind != "fwd"]
    n_in, n_ct = len(read), len(outs)
    n_add = 0 if add_to_first is None else 1

    def body(*refs):
        ids = [pl.program_id(i) for i in range(len(grid))]
        vals = [jnp.zeros([d for d in a.block if d is not None], F32) for a in args]
        for i, r in zip(read, refs[:n_in]):
            vals[i] = r[...].astype(F32)

        def g(*dv):
            full = list(vals)
            for i, v in zip(didx, dv):
                full[i] = v
            return tuple(fn(*full))

        _, vjp = jax.vjp(g, *[vals[i] for i in didx])
        grads = list(vjp(tuple(r[...].astype(F32) for r in refs[n_in:n_in + n_ct])))
        if n_add:
            grads[0] = grads[0] + refs[n_in + n_ct][...]
        for gr, i, r in zip(grads, didx, refs[n_in + n_ct + n_add:]):
            a = args[i]
            if a.kind == "row":
                r[...] = gr.astype(r.dtype)
            else:
                first = a.first(*ids)

                @pl.when(first)
                def _():
                    r[...] = gr

                @pl.when(jnp.logical_not(first))
                def _():
                    r[...] += gr

    g_specs, g_shapes = [], []
    for i in didx:
        a = args[i]
        if a.kind == "row":
            g_specs.append(pl.BlockSpec(a.block, a.gimap or a.imap))
            g_shapes.append(jax.ShapeDtypeStruct(a.gshape or a.arr.shape, a.gdtype))
        else:
            g_specs.append(pl.BlockSpec(a.block, a.imap))
            g_shapes.append(jax.ShapeDtypeStruct(a.arr.shape, F32))
    return _pcall(
        body, name=name, grid=grid,
        in_specs=[pl.BlockSpec(args[i].block, args[i].imap) for i in read] + [pl.BlockSpec(o[2], o[3]) for o in outs]
        + [pl.BlockSpec(args[didx[0]].block, args[didx[0]].imap)] * n_add,
        out_specs=g_specs, out_shape=g_shapes,
        compiler_params=_cparams(len(grid)),
    )(*[args[i].arr for i in read], *ct_arrays, *([add_to_first] if n_add else []))


def _rowop(name, fn, grid, args, outs):
    res = _op_fwd(name, fn, grid, args, outs)
    return res, lambda *cts, add_to_first=None: _op_bwd(name + "_bwd", fn, grid, args, outs, cts, add_to_first)


def _pick(n, cap):
    best = None
    for t in range(LANE, cap + 1, LANE):
        if n % t == 0:
            best = t
    return best or n


def _mm(a, b, mode, out_dtype, name, epi=None, aux=None, out_split=None):
    if mode == "nn":
        (m, k), n = a.shape, b.shape[1]
    elif mode == "nt":
        (m, k), n = a.shape, b.shape[0]
    else:
        (k, m), n = a.shape, b.shape[1]
    assert k <= MM_K_MAX
    rows = next((r for r in MM_FEATURE_ROWS if m % r == 0), m) if mode == "tn" else MM_TOKEN_ROWS
    tm = rows if m % rows == 0 else m
    tn = n // out_split if out_split else _pick(n, MM_COLS_MAX * (1 if mode == "tn" else 2))
    a_spec = pl.BlockSpec((k, tm), lambda j, i: (0, i)) if mode == "tn" else pl.BlockSpec((tm, k), lambda j, i: (i, 0))
    b_spec = pl.BlockSpec((tn, k), lambda j, i: (j, 0)) if mode == "nt" else pl.BlockSpec((k, tn), lambda j, i: (0, j))
    dims = {"nn": (1, 0), "nt": (1, 1), "tn": (0, 0)}[mode]
    aux_spec = pl.BlockSpec((tm, tn), lambda j, i: (i, j))
    if out_split:
        o_spec, o_shape = pl.BlockSpec((None, tm, tn), lambda j, i: (j, i, 0)), (out_split, m, tn)
    else:
        o_spec, o_shape = aux_spec, (m, n)
    n_aux = 0 if aux is None else 1
    n_out = 2 if epi == "sqrelu" else 1

    def body(*refs):
        o_refs = refs[2 + n_aux:]
        r = _dg(refs[0][...], refs[1][...], *dims)
        if epi == "sqrelu":
            o_refs[0][...] = r.astype(o_refs[0].dtype)
            rl = jnp.maximum(r, 0.0)
            o_refs[1][...] = (rl * rl).astype(o_refs[1].dtype)
        elif epi == "dsqrelu":
            pre = refs[2][...].astype(F32)
            o_refs[0][...] = (r * (2.0 * jnp.maximum(pre, 0.0))).astype(o_refs[0].dtype)
        else:
            o_refs[0][...] = r.astype(o_refs[0].dtype)

    res = _pcall(
        body, name=name, grid=(n // tn, m // tm),
        in_specs=[a_spec, b_spec] + [aux_spec] * n_aux, out_specs=[o_spec] * n_out,
        out_shape=[jax.ShapeDtypeStruct(o_shape, out_dtype)] * n_out, compiler_params=_cparams(2),
    )(a, b, *([aux] if aux is not None else []))
    return res if n_out == 2 else res[0]


ROW_CHUNK = 16


def _softmax_chunks(s_scr, n_keys, scale, emit):
    for r0 in range(0, s_scr.shape[0], ROW_CHUNK):
        rows = slice(r0, r0 + ROW_CHUNK)
        s = s_scr[rows, :n_keys]
        e = jnp.exp((s - jnp.max(s, axis=-1, keepdims=True)) * scale)
        emit(rows, e, 1.0 / jnp.sum(e, axis=-1, keepdims=True))


def _attn_fwd_block(v, n, scale, s_scr, e_scr, l_scr):
    def emit(rows, e, inv_l):
        e_scr[rows, :n] = e.astype(BF16)
        l_scr[rows, :] = jnp.broadcast_to(inv_l, (ROW_CHUNK, LANE))

    _softmax_chunks(s_scr, n, scale, emit)
    return _dg(e_scr[:, :n], v, 1, 0) * l_scr[...]


def _attn_bwd_block(q, k, o, do, scale, s_scr, dp_scr, p_scr, ds_scr):
    n = k.shape[0]

    def emit(rows, e, inv_l):
        p = e * inv_l
        delta = jnp.sum(do[rows, :] * o[rows, :], axis=-1, keepdims=True)
        p_scr[rows, :n] = p.astype(BF16)
        ds_scr[rows, :n] = (p * (dp_scr[rows, :n] - delta) * scale).astype(BF16)

    _softmax_chunks(s_scr, n, scale, emit)
    ds = ds_scr[:, :n]
    return _dg(ds, k, 1, 0), _dg(ds, q, 0, 0), _dg(p_scr[:, :n], do, 0, 0)


def _call_with_exchange(body, xchg, *, name, grid, in_specs, out_specs, out_shape, operands, scratch_shapes=()):
    if xchg is None:
        res = _pcall(body, name=name, grid=grid, in_specs=in_specs, out_specs=out_specs, out_shape=out_shape,
                     scratch_shapes=list(scratch_shapes), compiler_params=_cparams(len(grid)))(*operands)
        return list(res), []
    n_in, n_out, n_sc, n = len(in_specs), len(out_specs), len(scratch_shapes), xchg.n

    def wrapped(*refs):
        ins, x_refs = refs[:n_in], refs[n_in:n_in + n]
        outs, xo_refs = refs[n_in + n:n_in + n + n_out], refs[n_in + n + n_out:n_in + 2 * n + n_out]
        scratch, sems = refs[n_in + 2 * n + n_out:n_in + 2 * n + n_out + n_sc], refs[n_in + 2 * n + n_out + n_sc:]
        ids = [pl.program_id(i) for i in range(len(grid))]

        @pl.when(functools.reduce(jnp.logical_and, [i == 0 for i in ids]))
        def _():
            xchg.start(x_refs, xo_refs, sems)

        body(*ins, *outs, *scratch)

        @pl.when(functools.reduce(jnp.logical_and, [i == g - 1 for i, g in zip(ids, grid)]))
        def _():
            xchg.wait(x_refs, xo_refs, sems)

    res = _pcall(wrapped, name=name, grid=grid, in_specs=list(in_specs) + xchg.specs,
                 out_specs=list(out_specs) + xchg.specs, out_shape=list(out_shape) + xchg.out_shape,
                 scratch_shapes=list(scratch_shapes) + xchg.scratch, compiler_params=_cparams(len(grid)),
                 )(*operands, *xchg.bufs)
    return list(res[:n_out]), list(res[n_out:])


def _head_half(i, shape):
    lane = lax.broadcasted_iota(jnp.int32, shape, len(shape) - 1)
    return (lane < LANE // 2) if i == 0 else (lane >= LANE // 2)


def _mla_attn(q, k, v, tc, ctx_q, name, xchg=None):
    assert MLA_HPS == 2 and MLA_V == LANE // 2
    bsz, t_all, _ = q.shape
    n_t = t_all // TB
    grid = (bsz, MLA_HEADS // MLA_HPS, n_t)
    q_spec = pl.BlockSpec((None, TB, MLA_HPS * LANE), lambda b, h, t: (b, t, h))
    k_spec = pl.BlockSpec((None, t_all, MLA_HPS * LANE), lambda b, h, t: (b, 0, h))
    v_spec = pl.BlockSpec((None, t_all, LANE), lambda b, h, t: (b, 0, h))
    o_spec = pl.BlockSpec((None, TB, LANE), lambda b, h, t: (b, t, h))
    heads = [slice(i * LANE, (i + 1) * LANE) for i in range(MLA_HPS)]
    scale = MLA_QK ** -0.5
    f32_scr, bf16_scr = pltpu.VMEM((TB, t_all), F32), pltpu.VMEM((TB, t_all), BF16)
    o_shape = jax.ShapeDtypeStruct(v.shape, F32)

    def fwd_body(q_ref, k_ref, v_ref, o_ref, *scr):
        t = pl.program_id(2)

        def run(keys):
            n = keys.stop
            for i, hs in enumerate(heads):
                scr[3 * i][:, :n] = _dg(q_ref[:, hs], k_ref[keys, hs], 1, 1)
            both = [_attn_fwd_block(v_ref[keys, :], n, scale, *scr[3 * i:3 * i + 3]) for i in range(MLA_HPS)]
            o_ref[...] = jnp.where(_head_half(0, both[0].shape), both[0], both[1])

        @pl.when(t == 0)
        def _():
            if ctx_q:
                run(slice(0, tc))
            else:
                o_ref[...] = jnp.zeros_like(o_ref)

        @pl.when(t > 0)
        def _():
            run(slice(0, t_all))

    (o,), gathered = _call_with_exchange(
        fwd_body, xchg, name=name, grid=grid, in_specs=[q_spec, k_spec, v_spec], out_specs=[o_spec],
        out_shape=[o_shape], operands=(q, k, v),
        scratch_shapes=[f32_scr, bf16_scr, pltpu.VMEM((TB, LANE), F32)] * MLA_HPS)

    def bwd(do, xchg=None):
        def bwd_body(q_ref, k_ref, v_ref, o_ref, do_ref, dq_ref, dk_ref, dv_ref, *scr):
            t = pl.program_id(2)

            def run(keys, first):
                n = keys.stop
                dos = [jnp.where(_head_half(i, do_ref.shape), do_ref[...], 0.0) for i in range(MLA_HPS)]
                for i, hs in enumerate(heads):
                    scr[4 * i][:, :n] = _dg(q_ref[:, hs], k_ref[keys, hs], 1, 1)
                    scr[4 * i + 1][:, :n] = _dg(dos[i], v_ref[keys, :], 1, 1)
                dvs = []
                for i, hs in enumerate(heads):
                    dq, dk, dv = _attn_bwd_block(q_ref[:, hs], k_ref[keys, hs], o_ref[...], dos[i], scale,
                                                 *scr[4 * i:4 * i + 4])
                    dq_ref[:, hs] = dq
                    dvs.append(dv)
                    if first:
                        dk_ref[keys, hs] = dk
                    else:
                        dk_ref[keys, hs] += dk
                if first:
                    dv_ref[keys, :] = dvs[0] + dvs[1]
                else:
                    dv_ref[keys, :] += dvs[0] + dvs[1]

            @pl.when(t == 0)
            def _():
                dk_ref[...] = jnp.zeros_like(dk_ref)
                dv_ref[...] = jnp.zeros_like(dv_ref)
                if ctx_q:
                    run(slice(0, tc), True)
                else:
                    dq_ref[...] = jnp.zeros_like(dq_ref)

            @pl.when(t > 0)
            def _():
                run(slice(0, t_all), False)

        return _call_with_exchange(
            bwd_body, xchg, name=name + "_bwd", grid=grid, in_specs=[q_spec, k_spec, v_spec, o_spec, o_spec],
            out_specs=[q_spec, k_spec, v_spec],
            out_shape=[jax.ShapeDtypeStruct(q.shape, F32), jax.ShapeDtypeStruct(q.shape, F32), o_shape],
            operands=(q, k, v, o, do), scratch_shapes=[f32_scr, f32_scr, bf16_scr, bf16_scr] * MLA_HPS)

    return o, gathered, bwd


def _swa_block(q, keys, vals, sink, mask):
    qs = jnp.concatenate(list(_split(q, SWA_GROUP, 1)), axis=0)
    sk = jnp.sum(sink, axis=-1, keepdims=True) * (1.0 / LANE)
    s = _nt(qs, keys) * (SWA_HEAD_DIM ** -0.5)
    if mask is not None:
        s = jnp.where(mask, s, NEG_INF)
    o = _split(_nn(_softmax_rows(s, sk), vals + _roll(vals, LANE // 2, 1)), SWA_GROUP, 0)
    low = _head_half(0, o[0].shape)
    return jnp.concatenate([jnp.where(low, o[0], o[1]), jnp.where(low, o[2], o[3])], axis=1)


def _swa_ctx_block(q, kc, vc, sink):
    return _swa_block(q, kc, vc, sink, None)


def _swa_win_block(q, kc, kw, vc, vw, sink, mask):
    return _swa_block(q, jnp.concatenate([kc, kw], axis=0), jnp.concatenate([vc, vw], axis=0), sink, mask)


def _swa_attn(q, k, p_all, sink_b, tc, ctx_q, name, xchg=None):
    bsz, t_all, _ = q.shape
    n_q = t_all // QB_SWA
    n_cq = tc // QB_SWA
    lat = t_all - tc
    span = QB_SWA + 2 * WINDOW
    gw = SWA_GROUP * LANE
    grid = (bsz, SWA_KV_HEADS, n_q)
    q_spec = pl.BlockSpec((None, QB_SWA, gw), lambda b, g, i: (b, i, g))
    k_spec = pl.BlockSpec((None, t_all, LANE), lambda b, g, i: (b, 0, g))
    v_spec = pl.BlockSpec((None, t_all, LANE), lambda b, g, i: (b, 0, PC_SV // LANE + g))
    s_spec = pl.BlockSpec((None, SWA_GROUP * QB_SWA, LANE), lambda b, g, i: (g, 0, 0))

    def window(i):
        q0 = (i - n_cq) * QB_SWA
        w0 = jnp.clip(q0 - WINDOW, 0, lat - span)
        w0 = pl.multiple_of(w0, WINDOW)
        shape = (SWA_GROUP * QB_SWA, tc + span)
        qi = q0 + lax.broadcasted_iota(jnp.int32, shape, 0) % QB_SWA
        col = lax.broadcasted_iota(jnp.int32, shape, 1)
        kj = w0 + col - tc
        mask = (col < tc) | ((kj >= qi - WINDOW) & (kj <= qi + WINDOW))
        return w0, mask

    def fwd_body(q_ref, k_ref, v_ref, s_ref, o_ref):
        i = pl.program_id(2)

        @pl.when(i < n_cq)
        def _():
            if ctx_q:
                o_ref[...] = _swa_ctx_block(q_ref[...].astype(F32), k_ref[0:tc, :], v_ref[0:tc, :].astype(F32),
                                            s_ref[...])
            else:
                o_ref[...] = jnp.zeros_like(o_ref)

        @pl.when(i >= n_cq)
        def _():
            w0, mask = window(i)
            o_ref[...] = _swa_win_block(q_ref[...].astype(F32), k_ref[0:tc, :], k_ref[pl.ds(tc + w0, span), :],
                                        v_ref[0:tc, :].astype(F32), v_ref[pl.ds(tc + w0, span), :].astype(F32),
                                        s_ref[...], mask)

    o_spec = pl.BlockSpec((None, QB_SWA, SWA_GROUP * SWA_HEAD_DIM), lambda b, g, i: (b, i, g))
    (o,), gathered = _call_with_exchange(
        fwd_body, xchg, name=name, grid=grid, in_specs=[q_spec, k_spec, v_spec, s_spec], out_specs=[o_spec],
        out_shape=[jax.ShapeDtypeStruct((bsz, t_all, SWA_HEADS * SWA_HEAD_DIM), F32)],
        operands=(q, k, p_all, sink_b))

    def bwd(do, xchg=None):
        def bwd_body(q_ref, k_ref, v_ref, s_ref, do_ref, dq_ref, dk_ref, dv_ref, ds_ref):
            i = pl.program_id(2)

            @pl.when(i == 0)
            def _():
                dk_ref[...] = jnp.zeros_like(dk_ref)
                dv_ref[...] = jnp.zeros_like(dv_ref)
                ds_ref[...] = jnp.zeros_like(ds_ref)

            @pl.when(i < n_cq)
            def _():
                if ctx_q:
                    _, vjp = jax.vjp(_swa_ctx_block, q_ref[...].astype(F32), k_ref[0:tc, :].astype(F32),
                                     v_ref[0:tc, :].astype(F32), s_ref[...])
                    dq, dk, dv, ds = vjp(do_ref[...])
                    dq_ref[...] = dq
                    dk_ref[0:tc, :] += dk
                    dv_ref[0:tc, :] += dv
                    ds_ref[...] += ds
                else:
                    dq_ref[...] = jnp.zeros_like(dq_ref)

            @pl.when(i >= n_cq)
            def _():
                w0, mask = window(i)
                win = pl.ds(tc + w0, span)
                _, vjp = jax.vjp(functools.partial(_swa_win_block, mask=mask), q_ref[...].astype(F32),
                                 k_ref[0:tc, :].astype(F32), k_ref[win, :].astype(F32),
                                 v_ref[0:tc, :].astype(F32), v_ref[win, :].astype(F32), s_ref[...])
                dq, dkc, dkw, dvc, dvw, ds = vjp(do_ref[...])
                dq_ref[...] = dq
                dk_ref[0:tc, :] += dkc
                dk_ref[win, :] += dkw
                dv_ref[0:tc, :] += dvc
                dv_ref[win, :] += dvw
                ds_ref[...] += ds

        kv_out = pl.BlockSpec((None, t_all, LANE), lambda b, g, i: (b, 0, g))
        ds_spec = pl.BlockSpec((None, None, SWA_GROUP * QB_SWA, LANE), lambda b, g, i: (b, g, 0, 0))
        kv_shape = jax.ShapeDtypeStruct((bsz, t_all, SWA_KV_HEADS * LANE), F32)
        return _call_with_exchange(
            bwd_body, xchg, name=name + "_bwd", grid=grid, in_specs=[q_spec, k_spec, v_spec, s_spec, o_spec],
            out_specs=[q_spec, kv_out, kv_out, ds_spec],
            out_shape=[jax.ShapeDtypeStruct(q.shape, F32), kv_shape, kv_shape,
                       jax.ShapeDtypeStruct((bsz,) + sink_b.shape, F32)],
            operands=(q, k, p_all, sink_b, do))

    return o, gathered, bwd


def _scan_pair(chains, scratch):
    t_all, c = chains[0][0].shape
    n_tiles = t_all // SUBLANE
    row8 = lax.broadcasted_iota(jnp.int32, (t_all, c), 0) % SUBLANE
    refs = [scratch[0:3], scratch[3:6]]
    for (a, u, reverse), (a_s, u_s, _) in zip(chains, refs):
        for d in (1, 2, 4):
            sh = d if not reverse else t_all - d
            ar, ur = pltpu.roll(a, sh, 0), pltpu.roll(u, sh, 0)
            m = (row8 >= d) if not reverse else (row8 < SUBLANE - d)
            u = jnp.where(m, a * ur + u, u)
            a = jnp.where(m, a * ar, a)
        a_s[...] = a
        u_s[...] = u

    def step(j, carries):
        out = []
        for (_, _, reverse), (a_s, u_s, c_s), carry in zip(chains, refs, carries):
            tile = j if not reverse else n_tiles - 1 - j
            base = pl.multiple_of(tile * SUBLANE, SUBLANE)
            c_s[pl.ds(base, SUBLANE), :] = jnp.broadcast_to(carry, (SUBLANE, c))
            last = base + (0 if reverse else SUBLANE - 1)
            out.append(a_s[pl.ds(last, 1), :] * carry + u_s[pl.ds(last, 1), :])
        return tuple(out)

    lax.fori_loop(0, n_tiles, step, (jnp.zeros((1, c), F32),) * 2, unroll=4)
    return [a_s[...] * c_s[...] + u_s[...] for a_s, u_s, c_s in refs]


def _shift_rows(x, reverse_src):
    t_all = x.shape[0]
    row = lax.broadcasted_iota(jnp.int32, x.shape, 0)
    if reverse_src:
        return jnp.where(row == t_all - 1, 0.0, pltpu.roll(x, t_all - 1, 0))
    return jnp.where(row == 0, 0.0, pltpu.roll(x, 1, 0))


def _lru_scan(a0, u0, a1, u1, name):
    bsz, t_all, w = a0.shape
    grid = (bsz, w // LANE)
    spec = pl.BlockSpec((None, t_all, LANE), lambda b, c: (b, 0, c))
    scratch = [pltpu.VMEM((t_all, LANE), F32)] * 6
    shape = jax.ShapeDtypeStruct(a0.shape, F32)

    def fwd_body(a0_ref, u0_ref, a1_ref, u1_ref, h0_ref, h1_ref, *scr):
        h0_ref[...], h1_ref[...] = _scan_pair([(a0_ref[...], u0_ref[...], False), (a1_ref[...], u1_ref[...], True)],
                                              scr)

    wide = pl.BlockSpec((None, t_all, 2 * LANE), lambda b, c: (b, 0, c))
    h0, h1 = _pcall(fwd_body, name=name, grid=(bsz, w // (2 * LANE)), in_specs=[wide] * 4, out_specs=[wide] * 2,
                    out_shape=[shape] * 2, scratch_shapes=[pltpu.VMEM((t_all, 2 * LANE), F32)] * 6,
                    compiler_params=_cparams(2))(a0, u0, a1, u1)

    def bwd(dh0, dh1):
        def bwd_body(a0_ref, h0_ref, g0_ref, a1_ref, h1_ref, g1_ref, da0_ref, du0_ref, da1_ref, du1_ref, *scr):
            g0, g1 = _scan_pair([(_shift_rows(a0_ref[...], True), g0_ref[...], True),
                                 (_shift_rows(a1_ref[...], False), g1_ref[...], False)], scr)
            du0_ref[...] = g0
            da0_ref[...] = g0 * _shift_rows(h0_ref[...], False)
            du1_ref[...] = g1
            da1_ref[...] = g1 * _shift_rows(h1_ref[...], True)

        return _pcall(bwd_body, name=name + "_bwd", grid=grid, in_specs=[spec] * 6, out_specs=[spec] * 4,
                      out_shape=[shape] * 4, scratch_shapes=scratch,
                      compiler_params=_cparams(2))(a0, h0, dh0, a1, h1, dh1)

    return h0, h1, bwd


def _f_mod(x, g, shift, scale):
    return (_rms(x, g, D_MODEL) * (1.0 + scale) + shift,)


def _f_mla_q(cq, ga, w, gh, cos, sa, sb):
    n = _rms(cq, ga, MLA_Q_RANK)
    outs = []
    for wh in _split(w, MLA_HEADS, 1):
        outs.append(_rope(_rms(_nn(n, wh), gh, MLA_QK), cos, sa, sb, MLA_ROPE // 4))
    return (jnp.concatenate(outs, axis=1),)


def _f_mla_kv(ckv, krp, ga, wk, wv, gh, cos, sa, sb):
    n = _rms(ckv, ga, MLA_KV_RANK)
    outs = []
    for wh in _split(wk, MLA_HEADS, 1):
        outs.append(_rope(_rms(_nn(n, wh) + krp, gh, MLA_QK), cos, sa, sb, MLA_ROPE // 4))
    return jnp.concatenate(outs, axis=1), _nn(n, wv)


def _f_conv(x, w0, w1, w2, w3, bias, tc):
    t_all = x.shape[0]
    row = lax.broadcasted_iota(jnp.int32, x.shape, 0)
    lo = jnp.where(row < tc, 0, tc)
    hi = jnp.where(row < tc, tc, t_all)
    y = bias + jnp.zeros_like(x)
    for kk, wk in enumerate((w0, w1, w2, w3)):
        src = row + (kk - 2)
        xs = x if kk == 2 else _roll(x, 2 - kk, 0)
        y = y + wk * jnp.where((src >= lo) & (src < hi), xs, 0.0)
    return (y,)


def _f_gates(xc, w16, b00, b01, b10, b11, sp0, sp1):
    ws = _unstack(w16)
    n_cb = LRU_WIDTH // LANE
    xcs = _split(xc, n_cb, 1)
    bias = [_split(b, n_cb, 1) for b in (b00, b01, b10, b11)]
    sps = [_split(s, n_cb, 1) for s in (sp0, sp1)]
    res = [[], [], [], []]
    for c in range(n_cb):
        for z in range(2):
            r = _sig(_nn(xcs[c], ws[c * 4 + 2 * z]) + bias[2 * z][c])
            i = _sig(_nn(xcs[c], ws[c * 4 + 2 * z + 1]) + bias[2 * z + 1][c])
            la = -LRU_C * r * sps[z][c]
            res[2 * z].append(jnp.exp(la))
            res[2 * z + 1].append(jnp.sqrt(-jnp.tanh(la) * (jnp.exp(2.0 * la) + 1.0)) * (i * xcs[c]))
    return tuple(jnp.concatenate(r, axis=1) for r in res)


def _f_swa_qk(sq, sk, gq, gk, cos, sa, sb):
    qs = [_rope(_rms(x, gq, SWA_HEAD_DIM), cos, sa, sb, SWA_HEAD_DIM // 4) for x in _split(sq, SWA_HEADS, 1)]
    ks = [_rope(_rms(x, gk, SWA_HEAD_DIM), cos, sa, sb, SWA_HEAD_DIM // 4) for x in _split(sk, SWA_KV_HEADS, 1)]
    return jnp.concatenate(qs, axis=1), jnp.concatenate(ks, axis=1)


def _f_qkv(cq, ckv, krp, sq, sk, q_a_g, wuq, mla_q_g, kv_a_g, wk, wv, mla_k_g, swa_q_g, swa_k_g,
           m_cos, m_sa, m_sb, s_cos, s_sa, s_sb):
    return (*_f_mla_q(cq, q_a_g, wuq, mla_q_g, m_cos, m_sa, m_sb),
            *_f_mla_kv(ckv, krp, kv_a_g, wk, wv, mla_k_g, m_cos, m_sa, m_sb),
            *_f_swa_qk(sq, sk, swa_q_g, swa_k_g, s_cos, s_sa, s_sb))


def _f_merge(oa, h0, h1, lg, oc, ga, gb, gc):
    ob = (h0 + h1) * _gelu(lg)
    return (jnp.concatenate([_rms(oa, ga, GROUP_WIDTH), _rms(ob, gb, GROUP_WIDTH), _rms(oc, gc, GROUP_WIDTH)],
                            axis=1),)


def _f_resid_mod(x, y, gate, g, shift, scale):
    x1 = x + gate * y
    return x1, _rms(x1, g, D_MODEL) * (1.0 + scale) + shift


def _f_resid(x, y, gate):
    return (x + gate * y,)


def _hosted(hooks, key, arg=None):
    make, done = hooks.get(key, (None, None))
    xchg = make(arg) if make is not None else None
    return xchg, (done if xchg is not None else lambda outs: None)


def _layer(li, x, mods, w, s, tabs, tc, ctx_q, hooks, latent_dx_only):
    bsz, t_all, _ = x.shape
    n_t = t_all // TB
    grid = (bsz, n_t)
    rows = lambda b, t: (b, t, 0)

    def row(arr, width=None, idx=0, gdtype=F32, gshape=None, tb=TB):
        width = width or arr.shape[-1]
        return _A(arr, (None, tb, width), lambda b, t: (b, t, idx), "row", gdtype=gdtype, gshape=gshape,
                  gimap=rows if gshape is not None else None)

    def out(width, dtype, imap=rows, tb=TB):
        return ((bsz, t_all, width), dtype, (None, tb, width), imap)

    def modarg(arr):
        return _A(arr, (None, None, 1, D_MODEL), lambda b, t: (b, jnp.minimum(t, 1), 0, 0), "acc",
                  first=lambda b, t: t <= 1)

    def tab(arr, tb=TB):
        return _A(arr, (tb, LANE), lambda b, t: (t, 0), "const")

    def pcol(p_all, col, width, tb=TB):
        return row(p_all, width, col // width, gdtype=BF16, gshape=(bsz, t_all, width), tb=tb)

    nm = lambda base: "%s_l%d" % (base, li)
    sh1, sc1, g1, sh2, sc2, g2 = mods
    m_all = bsz * t_all

    x_arg = row(x)
    if latent_dx_only:
        n_c = tc // TB
        x_arg.gshape, x_arg.gimap = (bsz, t_all - tc, D_MODEL), lambda b, t: (b, jnp.maximum(t - n_c, 0), 0)
    (h,), b_mod1 = _rowop(nm("mod1"), _f_mod, grid, [x_arg, _par(s["norm1_g"]), modarg(sh1), modarg(sc1)],
                          [out(D_MODEL, BF16)])
    p_all = _mm(h.reshape(m_all, D_MODEL), w["win"], "nn", BF16, nm("mm_in")).reshape(bsz, t_all, P_WIDTH)

    qt = QKV_ROWS
    (q_a, k_a, v_a, q_c, k_c), b_qkv = _rowop(
        nm("qkv"), _f_qkv, (bsz, t_all // qt),
        [pcol(p_all, PC_CQ, 256, qt), pcol(p_all, PC_CKV, 128, qt), pcol(p_all, PC_KR, 128, qt),
         pcol(p_all, PC_SQ, 1024, qt), pcol(p_all, PC_SK, 256, qt)]
        + [_par(a) for a in (s["q_a_g"], w["wuq"], s["mla_q_g"], s["kv_a_g"], w["wk"], w["wv"], s["mla_k_g"],
                             s["swa_q_g"], s["swa_k_g"])]
        + [tab(a, qt) for a in tabs["mla"] + tabs["swa"]],
        [out(MLA_HEADS * LANE, BF16, tb=qt), out(MLA_HEADS * LANE, BF16, tb=qt), out(MLA_HEADS * MLA_V, BF16, tb=qt),
         out(SWA_HEADS * LANE, BF16, tb=qt), out(SWA_KV_HEADS * LANE, BF16, tb=qt)])

    xchg, done = _hosted(hooks, "mla_fwd")
    o_a, got, b_attn_a = _mla_attn(q_a, k_a, v_a, tc, ctx_q, nm("mla_attn"), xchg)
    done(got)

    n_cb = LRU_WIDTH // LANE
    conv_grid = (n_cb, bsz)
    cpar = lambda arr: _A(arr, (1, LANE), lambda c, b: (0, c), "acc", first=lambda c, b: b == 0)
    conv_args = [_A(p_all, (None, t_all, LANE), lambda c, b: (b, 0, PC_LX // LANE + c), "row", gdtype=BF16,
                    gshape=(bsz, t_all, LRU_WIDTH), gimap=lambda c, b: (b, 0, c))]
    conv_args += [cpar(a) for a in s["conv_w"]] + [cpar(s["conv_b"])]
    conv_out = [((bsz, t_all, LRU_WIDTH), F32, (None, t_all, LANE), lambda c, b: (b, 0, c))]
    (xc,), b_conv = _rowop(nm("lru_conv"), functools.partial(_f_conv, tc=tc), conv_grid, conv_args, conv_out)
    rot = lambda b, t: (b, (t + n_t - 1) % n_t, 0)
    (a0, u0, a1, u1), b_gates = _rowop(
        nm("lru_gates"), _f_gates, grid,
        [row(xc), _par(s["wbd"])] + [_par(a) for a in s["gate_b"]] + [_par(a) for a in s["sp"]],
        [out(LRU_WIDTH, F32), out(LRU_WIDTH, F32), out(LRU_WIDTH, F32, rot), out(LRU_WIDTH, F32, rot)])
    h0, h1, b_scan = _lru_scan(a0, u0, a1, u1, nm("lru_scan"))
    h1_arg = _A(h1, (None, TB, LRU_WIDTH), rot, "row")

    xchg, done = _hosted(hooks, "swa_fwd")
    o_c, got, b_attn_c = _swa_attn(q_c, k_c, p_all, s["sink_b"], tc, ctx_q, nm("swa_attn"), xchg)
    done(got)

    (y_in,), b_merge = _rowop(nm("merge"), _f_merge, grid,
                              [row(o_a), row(h0), h1_arg, pcol(p_all, PC_LG, 512), row(o_c), _par(s["g_a"]),
                               _par(s["g_b"]), _par(s["g_c"])],
                              [out(MIX_P, BF16)])
    y = _mm(y_in.reshape(m_all, MIX_P), w["wout"], "nn", F32, nm("mm_out")).reshape(bsz, t_all, D_MODEL)
    (x1, hm), b_rm = _rowop(nm("resid_mod"), _f_resid_mod, grid,
                            [row(x), row(y, gdtype=BF16), modarg(g1), _par(s["norm2_g"]), modarg(sh2), modarg(sc2)],
                            [out(D_MODEL, F32), out(D_MODEL, BF16)])
    pre, act = _mm(hm.reshape(m_all, D_MODEL), w["ff1"], "nn", BF16, nm("mm_ff1"), epi="sqrelu")
    y2 = _mm(act, w["ff2"], "nn", F32, nm("mm_ff2")).reshape(bsz, t_all, D_MODEL)
    (x2,), b_res = _rowop(nm("resid"), _f_resid, grid,
                          [_A(x1, (None, TB, D_MODEL), rows, "fwd"), row(y2, gdtype=BF16), modarg(g2)],
                          [out(D_MODEL, F32)])

    def bwd(dx2, hooks):
        dw, ds = {}, {}
        dy2, dg2 = b_res(dx2)
        dy2 = dy2.reshape(m_all, D_MODEL)
        dpre = _mm(dy2, w["ff2"], "nt", BF16, nm("mm_ff2_dx"), epi="dsqrelu", aux=pre)
        dw["ff2"] = _mm(act, dy2, "tn", BF16, nm("mm_ff2_dw"))
        dhm = _mm(dpre, w["ff1"], "nt", F32, nm("mm_ff1_dx")).reshape(bsz, t_all, D_MODEL)
        dw["ff1"] = _mm(hm.reshape(m_all, D_MODEL), dpre, "tn", BF16, nm("mm_ff1_dw"), out_split=N_DEV)
        dxa, dy, dg1, ds["norm2_g"], dsh2, dsc2 = b_rm(dx2, dhm)
        dy = dy.reshape(m_all, D_MODEL)
        dy_in = _mm(dy, w["wout"], "nt", F32, nm("mm_out_dx")).reshape(bsz, t_all, MIX_P)
        dw["wout"] = _mm(y_in.reshape(m_all, MIX_P), dy, "tn", BF16, nm("mm_out_dw"))
        do_a, dh0, dh1, dlg, do_c, ds["g_a"], ds["g_b"], ds["g_c"] = b_merge(dy_in)

        (dq_c, dk_c, dsv, dsink), _ = b_attn_c(do_c)
        ds["sink_b"] = jnp.sum(dsink, axis=0)

        da0, du0, da1, du1 = b_scan(dh0, dh1)
        gates_g = b_gates(da0, du0, da1, du1)
        dxc, ds["wbd"] = gates_g[0], gates_g[1]
        ds["gate_b"], ds["sp"] = list(gates_g[2:6]), list(gates_g[6:8])
        conv_g = b_conv(dxc)
        dlx, ds["conv_w"], ds["conv_b"] = conv_g[0], list(conv_g[1:5]), conv_g[5]

        xchg, done = _hosted(hooks, "mla_bwd", (dw, ds))
        (dq_a, dk_a, dv_a), got = b_attn_a(do_a, xchg)
        done(got)
        (dcq, dckv, dkr, dsq, dsk, ds["q_a_g"], dw["wuq"], ds["mla_q_g"], ds["kv_a_g"], dw["wk"], dw["wv"],
         ds["mla_k_g"], ds["swa_q_g"], ds["swa_k_g"]) = b_qkv(dq_a, dk_a, dv_a, dq_c, dk_c)

        dp = jnp.concatenate([dsq, dlx, dlg, dcq, dsk, dsv.astype(BF16), dckv, dkr], axis=-1)
        dp = dp.reshape(m_all, P_WIDTH)
        dh = _mm(dp, w["win"], "nt", F32, nm("mm_in_dx")).reshape(bsz, t_all, D_MODEL)
        dw["win"] = _mm(h.reshape(m_all, D_MODEL), dp, "tn", BF16, nm("mm_in_dw"))
        dx, ds["norm1_g"], dsh1, dsc1 = b_mod1(dh, add_to_first=dxa)
        return dx, [dsh1, dsc1, dg1, dsh2, dsc2, dg2], dw, ds

    return x2, bwd


def _loss_and_grad(x2, target, tc):
    bsz, t_all, d = x2.shape
    n_t = t_all // TB
    n_c = tc // TB

    def body(x_ref, t_ref, l_ref, dx_ref):
        b, t = pl.program_id(0), pl.program_id(1)

        @pl.when((b == 0) & (t == 0))
        def _():
            l_ref[...] = jnp.zeros_like(l_ref)

        @pl.when(t < n_c)
        def _():
            dx_ref[...] = jnp.zeros_like(dx_ref)

        @pl.when(t >= n_c)
        def _():
            e = x_ref[...] - t_ref[...]
            dx_ref[...] = e * (1.0 / d)
            l_ref[...] += jnp.sum(e * e) * (0.5 / d)

    loss, dx = _pcall(
        body, name="loss", grid=(bsz, n_t),
        in_specs=[pl.BlockSpec((None, TB, d), lambda b, t: (b, t, 0)),
                  pl.BlockSpec((None, TB, d), lambda b, t: (b, jnp.maximum(t - n_c, 0), 0))],
        out_specs=[pl.BlockSpec((SUBLANE, LANE), lambda b, t: (0, 0)),
                   pl.BlockSpec((None, TB, d), lambda b, t: (b, t, 0))],
        out_shape=[jax.ShapeDtypeStruct((SUBLANE, LANE), F32), jax.ShapeDtypeStruct(x2.shape, F32)],
        compiler_params=_cparams(2))(x2, target)
    return loss[0, 0], dx


def _rope_tables(lat, tc, dim, lane0):
    quarter = dim // 4
    pos = np.arange(lat)
    grid_pos = np.stack([pos // GRID_W, pos % GRID_W], axis=-1).astype(np.float32)
    lane = np.arange(LANE)
    p = np.clip(lane - lane0, 0, dim - 1)
    active = (lane >= lane0) & (lane < lane0 + dim)
    axis, half, qi = p // (dim // 2), (p % (dim // 2)) // quarter, p % quarter
    inv = (np.float32(ROPE_THETA) ** (-qi.astype(np.float32) / np.float32(quarter))).astype(np.float32)
    ang = (np.where(axis[None, :] == 0, grid_pos[:, 0:1], grid_pos[:, 1:2]) * inv[None, :]).astype(np.float32)
    cos = np.where(active, np.cos(ang), 1.0).astype(np.float32)
    sin = np.where(active, np.sin(ang), 0.0).astype(np.float32)
    sa = np.where(half == 0, -sin, 0.0).astype(np.float32)
    sb = np.where(half == 1, sin, 0.0).astype(np.float32)
    ctx1, ctx0 = np.ones((tc, LANE), np.float32), np.zeros((tc, LANE), np.float32)
    return tuple(jnp.asarray(np.concatenate([c, t], 0)) for c, t in ((ctx1, cos), (ctx0, sa), (ctx0, sb)))


_BIG = {"w_in": ((D_MODEL, IN_WIDTH // N_DEV), 1, ("win",)),
        "w_uq": ((MLA_Q_RANK, MLA_HEADS * MLA_QK // N_DEV), 1, ("wuq",)),
        "w_ukv": ((MLA_KV_RANK, MLA_HEADS * (MLA_NOPE + MLA_V) // N_DEV), 1, ("wk", "wv")),
        "w_out": ((3 * GROUP_WIDTH // N_DEV, D_MODEL), 0, ("wout",)),
        "w_ff1": ((D_MODEL, D_FF // N_DEV), 1, ("ff1",)),
        "w_ff2": ((D_FF // N_DEV, D_MODEL), 0, ("ff2",))}
_EARLY = ("w_in", "w_uq", "w_ukv")
_LATE = ("w_out", "w_ff1", "w_ff2")


def _pad_heads(wm, n_heads, dim):
    out = jnp.pad(wm.reshape(wm.shape[0], n_heads, dim), ((0, 0), (0, 0), (0, LANE - dim)))
    return out.reshape(wm.shape[0], n_heads * LANE)


def _prep_weight(name, piece):
    shp, ax, _ = _BIG[name]
    full = jnp.moveaxis(piece, 0, ax).reshape(shp[:ax] + (N_DEV * shp[ax],) + shp[ax + 1:])
    if name == "w_in":
        cq, ckv, kr, lx, lg, sq, sk, sv = _split_cols(full)
        return {"win": jnp.concatenate(
            [_pad_heads(sq, SWA_HEADS, SWA_HEAD_DIM), lx, lg, cq, _pad_heads(sk, SWA_KV_HEADS, SWA_HEAD_DIM),
             _pad_heads(sv, SWA_KV_HEADS, SWA_HEAD_DIM), ckv, jnp.pad(kr, ((0, 0), (MLA_NOPE, LANE - MLA_QK)))], axis=1)}
    if name == "w_uq":
        return {"wuq": _pad_heads(full, MLA_HEADS, MLA_QK)}
    if name == "w_ukv":
        ukv = full.reshape(MLA_KV_RANK, MLA_HEADS, MLA_NOPE + MLA_V)
        return {"wk": _pad_heads(ukv[:, :, :MLA_NOPE].reshape(MLA_KV_RANK, -1), MLA_HEADS, MLA_NOPE),
                "wv": ukv[:, :, MLA_NOPE:].reshape(MLA_KV_RANK, -1)}
    return {_BIG[name][2][0]: full}


def _split_cols(wm):
    parts, start = [], 0
    for size in IN_SIZES:
        parts.append(wm[:, start:start + size])
        start += size
    return parts


def _prep_gates(gate_w):
    gw = gate_w.reshape(2, 2, 4, 2, 64, 64)
    wbd = jnp.einsum("zgknCm,nN->knCzgNm", gw, jnp.eye(2, dtype=F32)).reshape(4, LANE, 4, LANE)
    return wbd.transpose(0, 2, 1, 3).reshape(16, LANE, LANE)


def _prep_small(raw):
    r1 = lambda a: a.reshape(1, -1)
    gg = raw["group_g"]
    sink = raw["swa_sink"].reshape(SWA_KV_HEADS, SWA_GROUP, 1, 1)
    return {
        "norm1_g": r1(raw["norm1_g"]), "norm2_g": r1(raw["norm2_g"]),
        "q_a_g": r1(raw["q_a_g"]), "kv_a_g": r1(raw["kv_a_g"]),
        "mla_q_g": jnp.pad(r1(raw["mla_q_g"]), ((0, 0), (0, LANE - MLA_QK))),
        "mla_k_g": jnp.pad(r1(raw["mla_k_g"]), ((0, 0), (0, LANE - MLA_QK))),
        "swa_q_g": jnp.pad(r1(raw["swa_q_g"]), ((0, 0), (0, LANE - SWA_HEAD_DIM))),
        "swa_k_g": jnp.pad(r1(raw["swa_k_g"]), ((0, 0), (0, LANE - SWA_HEAD_DIM))),
        "conv_w": [r1(raw["conv_w"][kk]) for kk in range(4)], "conv_b": r1(raw["conv_b"]),
        "gate_b": [r1(raw["lru_gate_b"][z, g]) for z in range(2) for g in range(2)],
        "sp": [r1(jax.nn.softplus(-raw["lru_lambda"][z])) for z in range(2)],
        "sink_b": jnp.broadcast_to(sink, (SWA_KV_HEADS, SWA_GROUP, QB_SWA, LANE)).reshape(
            SWA_KV_HEADS, SWA_GROUP * QB_SWA, LANE),
        "g_a": r1(gg[:GROUP_WIDTH]), "g_b": r1(gg[GROUP_WIDTH:2 * GROUP_WIDTH]), "g_c": r1(gg[2 * GROUP_WIDTH:])}


def _mesh_pos():
    return lax.axis_index("x"), lax.axis_index("y"), lax.axis_index("c")


def _peer(pos, k):
    return tuple(1 - p if (k >> s) & 1 else p for p, s in zip(pos, (2, 1, 0)))


def _dev_index(pos):
    return 4 * pos[0] + 2 * pos[1] + pos[2]


class _Exchange:
    def __init__(self, bufs, gather):
        self.bufs = list(bufs)
        self.n = len(self.bufs)
        self.gather = [gather] * self.n if isinstance(gather, bool) else list(gather)
        self.specs = [pl.BlockSpec(memory_space=pl.ANY)] * self.n
        self.out_shape = [jax.ShapeDtypeStruct((N_DEV,) + tuple(b.shape if g else b.shape[1:]), b.dtype)
                          for b, g in zip(self.bufs, self.gather)]
        self.scratch = [pltpu.SemaphoreType.DMA(((N_DEV - 1) * self.n,)),
                        pltpu.SemaphoreType.DMA(((N_DEV - 1) * self.n,)), pltpu.SemaphoreType.DMA((self.n,))]

    def _copies(self, x_refs, o_refs, sems, with_recvs):
        send_sems, recv_sems, local_sems = sems
        pos = _mesh_pos()
        me = _dev_index(pos)
        locals_, sends, recvs = [], [], []
        for j in range(self.n):
            src_mine = x_refs[j] if self.gather[j] else x_refs[j].at[me]
            locals_.append(pltpu.make_async_copy(src_mine, o_refs[j].at[me], local_sems.at[j]))
        for k in range(1, N_DEV):
            peer = _peer(pos, k)
            pidx = _dev_index(peer)
            for j in range(self.n):
                src = x_refs[j] if self.gather[j] else x_refs[j].at[pidx]
                sem = (k - 1) * self.n + j
                sends.append(pltpu.make_async_remote_copy(
                    src_ref=src, dst_ref=o_refs[j].at[me], send_sem=send_sems.at[sem], recv_sem=recv_sems.at[sem],
                    device_id=peer, device_id_type=pl.DeviceIdType.MESH))
                if with_recvs:
                    recvs.append(pltpu.make_async_remote_copy(
                        src_ref=src, dst_ref=o_refs[j].at[pidx], send_sem=send_sems.at[sem],
                        recv_sem=recv_sems.at[sem], device_id=peer, device_id_type=pl.DeviceIdType.MESH))
        return locals_, sends, recvs

    def start(self, x_refs, o_refs, sems):
        locals_, sends, _ = self._copies(x_refs, o_refs, sems, False)
        for cp in locals_ + sends:
            cp.start()

    def wait(self, x_refs, o_refs, sems):
        locals_, sends, recvs = self._copies(x_refs, o_refs, sems, True)
        for cp in recvs:
            cp.wait_recv()
        for cp in sends:
            cp.wait_send()
        for cp in locals_:
            cp.wait()


def _exchange(bufs, gather, name):
    xchg = _Exchange(bufs, gather)
    n = xchg.n

    def body(*refs):
        xchg.start(refs[:n], refs[n:2 * n], refs[2 * n:])
        xchg.wait(refs[:n], refs[n:2 * n], refs[2 * n:])

    return _pcall(body, name=name, out_shape=xchg.out_shape, in_specs=xchg.specs, out_specs=xchg.specs,
                  scratch_shapes=xchg.scratch)(*xchg.bufs)


def _pack(arrs, dtype):
    flat = jnp.concatenate([a.reshape(-1).astype(dtype) for a in arrs])
    rows = -(-flat.size // PACK_W)
    rows = -(-rows // 16) * 16
    return jnp.pad(flat, (0, rows * PACK_W - flat.size)).reshape(rows, PACK_W)


def _unpack(buf, shapes, lead=()):
    flat = buf.reshape(lead + (-1,))
    out, off = [], 0
    for shp in shapes:
        n = math.prod(shp)
        out.append(flat[..., off:off + n].reshape(lead + tuple(shp)))
        off += n
    return out


def _sum_sources(buf, name):
    _, r, c = buf.shape
    tr = _rows_tile(r)

    def body(x_ref, o_ref):
        acc = x_ref[0]
        for d in range(1, N_DEV):
            acc = acc + x_ref[d]
        o_ref[...] = acc

    return _pcall(body, name=name, grid=(r // tr,),
                  in_specs=[pl.BlockSpec((N_DEV, tr, c), lambda i: (0, i, 0))],
                  out_specs=pl.BlockSpec((tr, c), lambda i: (i, 0)),
                  out_shape=jax.ShapeDtypeStruct((r, c), F32), compiler_params=_cparams(1))(buf)


def _rows_tile(r):
    best = r
    for t in range(SUBLANE, ELEMWISE_ROWS_MAX + 1, SUBLANE):
        if r % t == 0:
            best = t
    return best


def _adamw(grads, wgt, m, v, name):
    n_lay = len(grads)
    n_src, r, c = grads[0].shape
    tr = _rows_tile(r)
    n_blk = r // tr
    bc1 = 1.0 - ADAM_B1 ** ADAM_STEP
    bc2 = 1.0 - ADAM_B2 ** ADAM_STEP

    def body(*refs):
        g_refs, (w_ref, m_ref, v_ref, go_ref, d_ref, mo_ref, vo_ref) = refs[:n_lay], refs[n_lay:]
        for li, g_ref in enumerate(g_refs):
            @pl.when(pl.program_id(0) == li)
            def _():
                g = g_ref[0].astype(F32)
                for d in range(1, n_src):
                    g = g + g_ref[d].astype(F32)
                m_new = ADAM_B1 * m_ref[...] + (1.0 - ADAM_B1) * g
                v_new = ADAM_B2 * v_ref[...] + (1.0 - ADAM_B2) * (g * g)
                go_ref[...] = g
                mo_ref[...] = m_new
                vo_ref[...] = v_new
                d_ref[...] = -ADAM_LR * ((m_new / bc1) / (jnp.sqrt(v_new / bc2) + ADAM_EPS) + ADAM_WD * w_ref[...])

    g_specs = [pl.BlockSpec((n_src, tr, c),
                            lambda l, i, li=li: (0, jnp.where(l == li, i, jnp.where(l > li, n_blk - 1, 0)), 0))
               for li in range(n_lay)]
    spec = pl.BlockSpec((tr, c), lambda l, i: (l * n_blk + i, 0))
    return _pcall(body, name=name, grid=(n_lay, n_blk), in_specs=g_specs + [spec, spec, spec],
                  out_specs=[spec] * 4, out_shape=[jax.ShapeDtypeStruct((n_lay * r, c), F32)] * 4,
                  compiler_params=_cparams(2))(*grads, wgt, m, v)


def _silu(z):
    return z * jax.nn.sigmoid(z)


_WEIGHTS = ("c_ctx", "w_mod", "b_mod", "norm1_g", "w_in", "q_a_g", "w_uq", "kv_a_g", "w_ukv", "mla_q_g", "mla_k_g",
            "conv_w", "conv_b", "lru_gate_w", "lru_gate_b", "lru_lambda", "swa_q_g", "swa_k_g", "swa_sink", "group_g",
            "w_out", "norm2_g", "w_ff1", "w_ff2")
_SHARDED_SMALL = ("conv_w", "lru_gate_b", "lru_lambda")
_REPL_RAW = ("norm1_g", "q_a_g", "kv_a_g", "mla_q_g", "mla_k_g", "conv_b", "swa_q_g", "swa_k_g",
             "swa_sink", "group_g", "norm2_g")
MOD_ROWS = 32


def kernel(x, c, ctx, c_ctx, w_mod, b_mod, norm1_g, w_in, q_a_g, w_uq, kv_a_g, w_ukv, mla_q_g, mla_k_g, conv_w, conv_b, lru_gate_w, lru_gate_b, lru_lambda, swa_q_g, swa_k_g, swa_sink, group_g, w_out, norm2_g, w_ff1, w_ff2, loss_target, m_c_ctx, m_w_mod, m_b_mod, m_norm1_g, m_w_in, m_q_a_g, m_w_uq, m_kv_a_g, m_w_ukv, m_mla_q_g, m_mla_k_g, m_conv_w, m_conv_b, m_lru_gate_w, m_lru_gate_b, m_lru_lambda, m_swa_q_g, m_swa_k_g, m_swa_sink, m_group_g, m_w_out, m_norm2_g, m_w_ff1, m_w_ff2, v_c_ctx, v_w_mod, v_b_mod, v_norm1_g, v_w_in, v_q_a_g, v_w_uq, v_kv_a_g, v_w_ukv, v_mla_q_g, v_mla_k_g, v_conv_w, v_conv_b, v_lru_gate_w, v_lru_gate_b, v_lru_lambda, v_swa_q_g, v_swa_k_g, v_swa_sink, v_group_g, v_w_out, v_norm2_g, v_w_ff1, v_w_ff2):
    wts = dict(c_ctx=c_ctx, w_mod=w_mod, b_mod=b_mod, norm1_g=norm1_g, w_in=w_in, q_a_g=q_a_g, w_uq=w_uq,
               kv_a_g=kv_a_g, w_ukv=w_ukv, mla_q_g=mla_q_g, mla_k_g=mla_k_g, conv_w=conv_w, conv_b=conv_b,
               lru_gate_w=lru_gate_w, lru_gate_b=lru_gate_b, lru_lambda=lru_lambda, swa_q_g=swa_q_g, swa_k_g=swa_k_g,
               swa_sink=swa_sink, group_g=group_g, w_out=w_out, norm2_g=norm2_g, w_ff1=w_ff1, w_ff2=w_ff2)
    mom1 = dict(zip(_WEIGHTS, (m_c_ctx, m_w_mod, m_b_mod, m_norm1_g, m_w_in, m_q_a_g, m_w_uq, m_kv_a_g, m_w_ukv,
                               m_mla_q_g, m_mla_k_g, m_conv_w, m_conv_b, m_lru_gate_w, m_lru_gate_b, m_lru_lambda,
                               m_swa_q_g, m_swa_k_g, m_swa_sink, m_group_g, m_w_out, m_norm2_g, m_w_ff1, m_w_ff2)))
    mom2 = dict(zip(_WEIGHTS, (v_c_ctx, v_w_mod, v_b_mod, v_norm1_g, v_w_in, v_q_a_g, v_w_uq, v_kv_a_g, v_w_ukv,
                               v_mla_q_g, v_mla_k_g, v_conv_w, v_conv_b, v_lru_gate_w, v_lru_gate_b, v_lru_lambda,
                               v_swa_q_g, v_swa_k_g, v_swa_sink, v_group_g, v_w_out, v_norm2_g, v_w_ff1, v_w_ff2)))
    bsz = x.shape[0]
    n_ex = bsz * N_DEV
    me = _dev_index(_mesh_pos())
    mod_cols = w_mod.shape[-1]

    small_shapes = [c.shape, conv_w.shape, lru_gate_b.shape, lru_lambda.shape]
    shard = lambda n, li: wts[n][li].astype(BF16)
    g_small, *early_pieces = _exchange([_pack([c, conv_w, lru_gate_b, lru_lambda], F32)] + [shard(n, 0) for n in _EARLY],
                                       True, "ag_first")
    c_all, conv_w_all, gate_b_all, lam_all = _unpack(g_small, small_shapes, lead=(N_DEV,))
    c_all = c_all.reshape(n_ex, D_MODEL)
    cat_last = lambda a: jnp.moveaxis(a, 0, -2).reshape(a.shape[1:-1] + (N_DEV * a.shape[-1],))
    conv_w_full, gate_b_full, lam_full = cat_last(conv_w_all), cat_last(gate_b_all), cat_last(lam_all)

    act = jnp.zeros((MOD_ROWS, D_MODEL), F32).at[:n_ex].set(_silu(c_all)).at[n_ex].set(_silu(c_ctx))
    mod_part = jnp.concatenate([_mm(act, w_mod[li], "nn", F32, "mm_mod_l%d" % li) for li in range(DEPTH)], axis=1)
    (mod_all,) = _exchange([mod_part], True, "ag_mod")
    mods = []
    for li in range(DEPTH):
        full = jnp.moveaxis(mod_all[:, :, li * mod_cols:(li + 1) * mod_cols], 0, 1).reshape(MOD_ROWS, -1) + b_mod[li]
        mine = lax.dynamic_slice_in_dim(full, me * bsz, bsz, axis=0)
        ctx_row = jnp.broadcast_to(full[n_ex], mine.shape)
        both = jnp.stack([ctx_row, mine], axis=1).reshape(bsz, 2, N_MOD, 1, D_MODEL)
        mods.append([both[:, :, j] for j in range(N_MOD)])

    raw = {n: wts[n] for n in _REPL_RAW}
    raw.update(conv_w=conv_w_full, lru_gate_b=gate_b_full, lru_lambda=lam_full)
    small_names = list(_REPL_RAW) + list(_SHARDED_SMALL)
    sp, small_vjp, gates_vjp = [None] * DEPTH, [None] * DEPTH, [None] * DEPTH
    for li in range(DEPTH):
        sp[li], small_vjp[li] = jax.vjp(_prep_small, {n: raw[n][li] for n in small_names})
        sp[li]["wbd"], gates_vjp[li] = jax.vjp(_prep_gates, lru_gate_w[li])

    w, w_vjp, g_recv, small_recv = [{} for _ in range(DEPTH)], {}, {}, {}

    def take(li, names, pieces):
        for n, piece in zip(names, pieces):
            out, w_vjp[n, li] = jax.vjp(functools.partial(_prep_weight, n), piece)
            w[li].update(out)

    def gather_hook(li, names):
        return (lambda _: _Exchange([shard(n, li) for n in names], True), lambda got: take(li, names, got))

    def wgrad(n, li, dwl):
        if n == "w_ff1":
            return dwl["ff1"]
        (g,) = w_vjp[n, li]({k: dwl[k].astype(BF16) for k in _BIG[n][2]})
        return g

    def small_pack(li, ds_l, extra=()):
        (d_raw,) = small_vjp[li]({k: v for k, v in ds_l.items() if k != "wbd"})
        return _pack([d_raw[n] for n in small_names] + list(extra), F32)

    def gates_grad(li, ds_l):
        return gates_vjp[li](ds_l["wbd"])[0].reshape(-1, LANE)

    take(0, _EARLY, early_pieces)
    hooks_fwd = [{"mla_fwd": gather_hook(0, _LATE), "swa_fwd": gather_hook(1, _EARLY + ("w_out",))},
                 {"mla_fwd": gather_hook(1, ("w_ff1", "w_ff2"))}]
    bwd_state = {}

    def scatter_last_layer(grads_so_far):
        dwl, dsl = grads_so_far
        return _Exchange([wgrad(n, 1, dwl) for n in _LATE] + [gates_grad(1, dsl)], [False] * len(_LATE) + [True])

    def scatter_first_layer(grads_so_far):
        dwl, dsl = grads_so_far
        dw1, ds1 = bwd_state["dw1"], bwd_state["ds1"]
        bufs = [wgrad(n, 1, dw1) for n in _EARLY] + [wgrad(n, 0, dwl) for n in _LATE]
        bufs += [small_pack(1, ds1), gates_grad(0, dsl)]
        return _Exchange(bufs, [False] * (len(_EARLY) + len(_LATE)) + [True] * 2)

    def scattered_first_layer(got):
        g_recv.update(zip([(n, 1) for n in _EARLY] + [(n, 0) for n in _LATE], got[:-2]))
        small_recv[1], g_recv["lru_gate_w", 0] = got[-2:]

    def scattered_last_layer(got):
        g_recv.update(zip([(n, 1) for n in _LATE], got[:-1]))
        g_recv["lru_gate_w", 1] = got[-1]

    hooks_bwd = [{"mla_bwd": (scatter_first_layer, scattered_first_layer)},
                 {"mla_bwd": (scatter_last_layer, scattered_last_layer)}]

    tc, lat = ctx.shape[1], x.shape[1]
    tabs = {"mla": _rope_tables(lat, tc, MLA_ROPE, MLA_NOPE), "swa": _rope_tables(lat, tc, SWA_HEAD_DIM, 0)}
    stream = jnp.concatenate([ctx, x], axis=1)
    bwds = []
    for li in range(DEPTH):
        stream, bwd = _layer(li, stream, mods[li], w[li], sp[li], tabs, tc, li < DEPTH - 1, hooks_fwd[li], li == 0)
        bwds.append(bwd)
    loss_part, dstream = _loss_and_grad(stream, loss_target, tc)
    dmods = [None] * DEPTH
    dstream, dmods[1], bwd_state["dw1"], bwd_state["ds1"] = bwds[1](dstream, hooks_bwd[1])
    grad_x, dmods[0], dw0, ds0 = bwds[0](dstream, hooks_bwd[0])

    dm_rows = []
    for li in range(DEPTH):
        dm = jnp.concatenate(dmods[li], axis=-1)
        dm_rows.append(jnp.concatenate([dm[:, 1, 0], jnp.sum(dm[:, 0, 0], axis=0, keepdims=True)], axis=0))
    dm_mine = jnp.concatenate(dm_rows, axis=1)
    dm_mine = jnp.pad(dm_mine, ((0, SUBLANE - bsz - 1), (0, 0)))
    (dm_all,) = _exchange([dm_mine], True, "ag_dmod")
    g_wmod, g_bmod, dact_ctx = [], [], jnp.zeros((D_MODEL,), F32)
    for li in range(DEPTH):
        part = dm_all[:, :, li * N_MOD * D_MODEL:(li + 1) * N_MOD * D_MODEL]
        dm32 = jnp.zeros((MOD_ROWS, N_MOD * D_MODEL), F32).at[:n_ex].set(part[:, :bsz].reshape(n_ex, -1))
        dm32 = dm32.at[n_ex].set(jnp.sum(part[:, bsz], axis=0))
        g_bmod.append(jnp.sum(dm32, axis=0))
        cols = lax.dynamic_slice_in_dim(dm32, me * mod_cols, mod_cols, axis=1)
        g_wmod.append(_mm(act, cols, "tn", F32, "mm_mod_dw_l%d" % li))
        dact_ctx = dact_ctx + _mm(cols, w_mod[li], "nt", F32, "mm_mod_dx_l%d" % li)[n_ex]
    sg = jax.nn.sigmoid(c_ctx)
    g_cctx_part = dact_ctx * (sg * (1.0 + c_ctx * (1.0 - sg)))

    last = _exchange([wgrad(n, 0, dw0) for n in _EARLY] + [small_pack(0, ds0, (g_cctx_part, loss_part.reshape(1)))],
                     [False] * len(_EARLY) + [True], "rs_early")
    g_recv.update(zip([(n, 0) for n in _EARLY], last[:-1]))
    small_recv[0] = last[-1]
    layer_shapes = [raw[n].shape[1:] for n in small_names]
    tot = [_unpack(_sum_sources(small_recv[li], "sum_grads_l%d" % li), layer_shapes + [(D_MODEL,), (1,)][:2 * (li == 0)])
           for li in range(DEPTH)]
    grads = {n: jnp.stack([tot[li][j] for li in range(DEPTH)], axis=0) for j, n in enumerate(small_names)}
    grads["c_ctx"], loss = tot[0][-2], tot[0][-1][0]
    for n in _SHARDED_SMALL:
        width = wts[n].shape[-1]
        grads[n] = lax.dynamic_slice_in_dim(grads[n], me * width, width, axis=grads[n].ndim - 1)
    grads["b_mod"] = jnp.stack(g_bmod, axis=0)

    delta, new_m, new_v = {}, {}, {}
    per_layer = {n: [g_recv[n, li] for li in range(DEPTH)] for n in list(_BIG) + ["lru_gate_w"]}
    per_layer["w_mod"] = [g[None] for g in g_wmod]
    for n, srcs in per_layer.items():
        two_d = (DEPTH * math.prod(wts[n].shape[1:-1]), wts[n].shape[-1])
        srcs = [s.reshape((s.shape[0], two_d[0] // DEPTH, two_d[1])) for s in srcs]
        res = _adamw(srcs, wts[n].reshape(two_d), mom1[n].reshape(two_d), mom2[n].reshape(two_d), "adamw_" + n)
        grads[n], delta[n], new_m[n], new_v[n] = [r.reshape(wts[n].shape) for r in res]
    rest = [n for n in _WEIGHTS if n not in delta]
    shapes = [wts[n].shape for n in rest]
    res = _adamw([_pack([grads[n] for n in rest], F32)[None]], _pack([wts[n] for n in rest], F32),
                 _pack([mom1[n] for n in rest], F32), _pack([mom2[n] for n in rest], F32), "adamw_small")
    for tgt, buf in zip((delta, new_m, new_v), res[1:]):
        tgt.update(zip(rest, _unpack(buf, shapes)))

    return (loss, grad_x, *[grads[n] for n in _WEIGHTS], *[delta[n] for n in _WEIGHTS],
            *[new_m[n] for n in _WEIGHTS], *[new_v[n] for n in _WEIGHTS])
```
